```python
import math
import jax, jax.numpy as jnp
from jax import lax
import numpy as np

D_MODEL = 1024
BATCH = 8
SEQ = 4096
DEPTH = 1

D_PLE = 256
D_MIX = D_MODEL
ATTN_HEADS = 8
HEAD_DIM = 64
D_ATTN = ATTN_HEADS * HEAD_DIM
POOL_WINDOWS = (2, 4, 8, 16)
POOL_GROUPS = len(POOL_WINDOWS)
D_POOL = D_MIX - D_ATTN
POOL_CH = D_POOL // POOL_GROUPS
D_IN = 3 * D_ATTN + ATTN_HEADS + D_POOL
D_FF = int(math.ceil(8 * D_MODEL / 3 / 256) * 256)
Q_BLOCK = 128
RMS_EPS = 1e-6

kernel_name = "hymba_fox_poolformer_block"


def rms_norm(x, g):
    xf = x.astype(jnp.float32)
    y = xf * lax.rsqrt(jnp.mean(xf * xf, axis=-1, keepdims=True) + RMS_EPS)
    return (y * g.astype(jnp.float32)).astype(x.dtype)


def forgetting_attention(q, k, v, log_f):
    b, s, h, dh = q.shape
    scale = 1.0 / math.sqrt(dh)
    c = jnp.cumsum(log_f, axis=1).transpose(0, 2, 1)
    nb = s // Q_BLOCK
    qb = q.reshape(b, nb, Q_BLOCK, h, dh).transpose(1, 0, 2, 3, 4)
    cb = c.reshape(b, h, nb, Q_BLOCK).transpose(2, 0, 1, 3)
    starts = jnp.arange(nb, dtype=jnp.int32) * Q_BLOCK
    kpos = jnp.arange(s, dtype=jnp.int32)

    def one_block(args):
        q_i, c_i, s0 = args
        scores = jnp.einsum('bqhd,bkhd->bhqk', q_i, k).astype(jnp.float32) * scale
        scores = scores + c_i[:, :, :, None] - c[:, :, None, :]
        qpos = s0 + jnp.arange(Q_BLOCK, dtype=jnp.int32)
        causal = kpos[None, :] <= qpos[:, None]
        scores = jnp.where(causal, scores, -jnp.inf)
        probs = jax.nn.softmax(scores, axis=-1).astype(v.dtype)
        return jnp.einsum('bhqk,bkhd->bqhd', probs, v)

    out = lax.map(one_block, (qb, cb, starts))
    return out.transpose(1, 0, 2, 3, 4).reshape(b, s, h * dh)


def multiscale_pool(u, w_pool, pool_scale):
    b, s, _ = u.shape
    uf = u.astype(jnp.float32)
    cs = jnp.cumsum(uf, axis=1)
    pos = jnp.arange(s, dtype=jnp.int32)
    outs = []
    for g, w in enumerate(POOL_WINDOWS):
        lo, hi = g * POOL_CH, (g + 1) * POOL_CH
        cs_g = cs[:, :, lo:hi]
        cs_shift = jnp.pad(cs_g, ((0, 0), (w, 0), (0, 0)))[:, :s]
        count = jnp.minimum(pos + 1, w).astype(jnp.float32)[None, :, None]
        outs.append((cs_g - cs_shift) / count - uf[:, :, lo:hi])
    y = jnp.stack(outs, axis=2).astype(u.dtype)
    y = jnp.einsum('bsgc,gcd->bsgd', y, w_pool).reshape(b, s, D_POOL)
    return y * pool_scale


def _fwd_setup_inputs(seed: int = 0) -> dict:
    key = jax.random.key(seed)
    ks = jax.random.split(key, 24)
    f32 = jnp.float32

    def nrm(k, shape, fan_in):
        return jax.random.normal(k, shape, f32) * (fan_in ** -0.5)

    def gain(k, shape):
        return 1.0 + 0.05 * jax.random.normal(k, shape, f32)

    return {
        "x": jax.random.normal(ks[0], (BATCH, SEQ, D_MODEL), f32),
        "p": jax.random.normal(ks[1], (DEPTH, BATCH, SEQ, D_PLE), f32),
        "g_mix_pre": gain(ks[2], (DEPTH, D_MODEL)),
        "w_in": nrm(ks[3], (DEPTH, D_MODEL, D_IN), D_MODEL),
        "b_forget": jax.random.uniform(ks[4], (DEPTH, ATTN_HEADS), f32, 1.0, 4.0),
        "g_attn_grp": gain(ks[5], (DEPTH, D_ATTN)),
        "g_pool_grp": gain(ks[6], (DEPTH, D_POOL)),
        "w_pool": nrm(ks[7], (DEPTH, POOL_GROUPS, POOL_CH, POOL_CH), POOL_CH),
        "pool_scale": 1.0 + 0.1 * jax.random.normal(ks[8], (DEPTH, D_POOL), f32),
        "w_out": nrm(ks[9], (DEPTH, D_MIX, D_MODEL), D_MIX),
        "g_mix_post": gain(ks[10], (DEPTH, D_MODEL)),
        "g_ffn_pre": gain(ks[11], (DEPTH, D_MODEL)),
        "w_ffn_gate": nrm(ks[12], (DEPTH, D_MODEL, D_FF), D_MODEL),
        "w_ffn_up": nrm(ks[13], (DEPTH, D_MODEL, D_FF), D_MODEL),
        "w_ffn_down": nrm(ks[14], (DEPTH, D_FF, D_MODEL), D_FF),
        "g_ffn_post": gain(ks[15], (DEPTH, D_MODEL)),
        "w_ple_proj": nrm(ks[16], (DEPTH, D_PLE, D_MODEL), D_PLE),
        "g_ple": gain(ks[17], (DEPTH, D_MODEL)),
        "w_ple_gate": nrm(ks[18], (DEPTH, D_MODEL, D_MODEL), D_MODEL),
    }


def _fwd_reference(x, p, g_mix_pre, w_in, b_forget, g_attn_grp, g_pool_grp, w_pool, pool_scale,
              w_out, g_mix_post, g_ffn_pre, w_ffn_gate, w_ffn_up, w_ffn_down, g_ffn_post,
              w_ple_proj, g_ple, w_ple_gate):
    b, s, _ = x.shape
    h = x
    for i in range(DEPTH):
        hn = rms_norm(h, g_mix_pre[i])
        z = hn @ w_in[i]
        o = 0
        q = z[..., o:o + D_ATTN].reshape(b, s, ATTN_HEADS, HEAD_DIM); o += D_ATTN
        k = z[..., o:o + D_ATTN].reshape(b, s, ATTN_HEADS, HEAD_DIM); o += D_ATTN
        v = z[..., o:o + D_ATTN].reshape(b, s, ATTN_HEADS, HEAD_DIM); o += D_ATTN
        f_logit = z[..., o:o + ATTN_HEADS]; o += ATTN_HEADS
        u = z[..., o:o + D_POOL]
        log_f = jax.nn.log_sigmoid(f_logit.astype(jnp.float32) + b_forget[i].astype(jnp.float32))
        a = forgetting_attention(q, k, v, log_f)
        m = multiscale_pool(u, w_pool[i], pool_scale[i])
        mix = jnp.concatenate([rms_norm(a, g_attn_grp[i]), rms_norm(m, g_pool_grp[i])], axis=-1)
        h = h + rms_norm(mix @ w_out[i], g_mix_post[i])
        hn = rms_norm(h, g_ffn_pre[i])
        ff = (jax.nn.silu(hn @ w_ffn_gate[i]) * (hn @ w_ffn_up[i])) @ w_ffn_down[i]
        h = h + rms_norm(ff, g_ffn_post[i])
        e = rms_norm(p[i] @ w_ple_proj[i], g_ple[i])
        h = h + jax.nn.sigmoid(h @ w_ple_gate[i]) * e
    return h


import jax as _jax
import jax.numpy as _jnp

TWIN_FORMAT = 'train_step'
FWD_PARAMS = ['x', 'p', 'g_mix_pre', 'w_in', 'b_forget', 'g_attn_grp', 'g_pool_grp', 'w_pool', 'pool_scale', 'w_out', 'g_mix_post', 'g_ffn_pre', 'w_ffn_gate', 'w_ffn_up', 'w_ffn_down', 'g_ffn_post', 'w_ple_proj', 'g_ple', 'w_ple_gate']
TWIN_WEIGHTS = ['g_mix_pre', 'w_in', 'b_forget', 'g_attn_grp', 'g_pool_grp', 'w_pool', 'pool_scale', 'w_out', 'g_mix_post', 'g_ffn_pre', 'w_ffn_gate', 'w_ffn_up', 'w_ffn_down', 'g_ffn_post', 'w_ple_proj', 'g_ple', 'w_ple_gate']
TWIN_DIFF_INPUT = 'x'
TWIN_INPUTS = ['x', 'p', 'g_mix_pre', 'w_in', 'b_forget', 'g_attn_grp', 'g_pool_grp', 'w_pool', 'pool_scale', 'w_out', 'g_mix_post', 'g_ffn_pre', 'w_ffn_gate', 'w_ffn_up', 'w_ffn_down', 'g_ffn_post', 'w_ple_proj', 'g_ple', 'w_ple_gate', 'loss_target', 'm_g_mix_pre', 'm_w_in', 'm_b_forget', 'm_g_attn_grp', 'm_g_pool_grp', 'm_w_pool', 'm_pool_scale', 'm_w_out', 'm_g_mix_post', 'm_g_ffn_pre', 'm_w_ffn_gate', 'm_w_ffn_up', 'm_w_ffn_down', 'm_g_ffn_post', 'm_w_ple_proj', 'm_g_ple', 'm_w_ple_gate', 'v_g_mix_pre', 'v_w_in', 'v_b_forget', 'v_g_attn_grp', 'v_g_pool_grp', 'v_w_pool', 'v_pool_scale', 'v_w_out', 'v_g_mix_post', 'v_g_ffn_pre', 'v_w_ffn_gate', 'v_w_ffn_up', 'v_w_ffn_down', 'v_g_ffn_post', 'v_w_ple_proj', 'v_g_ple', 'v_w_ple_gate']
TWIN_OUTPUTS = ['loss', 'grad_x', 'grad_g_mix_pre', 'grad_w_in', 'grad_b_forget', 'grad_g_attn_grp', 'grad_g_pool_grp', 'grad_w_pool', 'grad_pool_scale', 'grad_w_out', 'grad_g_mix_post', 'grad_g_ffn_pre', 'grad_w_ffn_gate', 'grad_w_ffn_up', 'grad_w_ffn_down', 'grad_g_ffn_post', 'grad_w_ple_proj', 'grad_g_ple', 'grad_w_ple_gate', 'delta_g_mix_pre', 'delta_w_in', 'delta_b_forget', 'delta_g_attn_grp', 'delta_g_pool_grp', 'delta_w_pool', 'delta_pool_scale', 'delta_w_out', 'delta_g_mix_post', 'delta_g_ffn_pre', 'delta_w_ffn_gate', 'delta_w_ffn_up', 'delta_w_ffn_down', 'delta_g_ffn_post', 'delta_w_ple_proj', 'delta_g_ple', 'delta_w_ple_gate', 'new_m_g_mix_pre', 'new_m_w_in', 'new_m_b_forget', 'new_m_g_attn_grp', 'new_m_g_pool_grp', 'new_m_w_pool', 'new_m_pool_scale', 'new_m_w_out', 'new_m_g_mix_post', 'new_m_g_ffn_pre', 'new_m_w_ffn_gate', 'new_m_w_ffn_up', 'new_m_w_ffn_down', 'new_m_g_ffn_post', 'new_m_w_ple_proj', 'new_m_g_ple', 'new_m_w_ple_gate', 'new_v_g_mix_pre', 'new_v_w_in', 'new_v_b_forget', 'new_v_g_attn_grp', 'new_v_g_pool_grp', 'new_v_w_pool', 'new_v_pool_scale', 'new_v_w_out', 'new_v_g_mix_post', 'new_v_g_ffn_pre', 'new_v_w_ffn_gate', 'new_v_w_ffn_up', 'new_v_w_ffn_down', 'new_v_g_ffn_post', 'new_v_w_ple_proj', 'new_v_g_ple', 'new_v_w_ple_gate']
TWIN_LEAF_KINDS = {'loss': 'loss', 'grad_x': 'grad_x', 'grad_g_mix_pre': 'grad_w', 'grad_w_in': 'grad_w', 'grad_b_forget': 'grad_w', 'grad_g_attn_grp': 'grad_w', 'grad_g_pool_grp': 'grad_w', 'grad_w_pool': 'grad_w', 'grad_pool_scale': 'grad_w', 'grad_w_out': 'grad_w', 'grad_g_mix_post': 'grad_w', 'grad_g_ffn_pre': 'grad_w', 'grad_w_ffn_gate': 'grad_w', 'grad_w_ffn_up': 'grad_w', 'grad_w_ffn_down': 'grad_w', 'grad_g_ffn_post': 'grad_w', 'grad_w_ple_proj': 'grad_w', 'grad_g_ple': 'grad_w', 'grad_w_ple_gate': 'grad_w', 'delta_g_mix_pre': 'delta_w', 'delta_w_in': 'delta_w', 'delta_b_forget': 'delta_w', 'delta_g_attn_grp': 'delta_w', 'delta_g_pool_grp': 'delta_w', 'delta_w_pool': 'delta_w', 'delta_pool_scale': 'delta_w', 'delta_w_out': 'delta_w', 'delta_g_mix_post': 'delta_w', 'delta_g_ffn_pre': 'delta_w', 'delta_w_ffn_gate': 'delta_w', 'delta_w_ffn_up': 'delta_w', 'delta_w_ffn_down': 'delta_w', 'delta_g_ffn_post': 'delta_w', 'delta_w_ple_proj': 'delta_w', 'delta_g_ple': 'delta_w', 'delta_w_ple_gate': 'delta_w', 'new_m_g_mix_pre': 'new_m', 'new_m_w_in': 'new_m', 'new_m_b_forget': 'new_m', 'new_m_g_attn_grp': 'new_m', 'new_m_g_pool_grp': 'new_m', 'new_m_w_pool': 'new_m', 'new_m_pool_scale': 'new_m', 'new_m_w_out': 'new_m', 'new_m_g_mix_post': 'new_m', 'new_m_g_ffn_pre': 'new_m', 'new_m_w_ffn_gate': 'new_m', 'new_m_w_ffn_up': 'new_m', 'new_m_w_ffn_down': 'new_m', 'new_m_g_ffn_post': 'new_m', 'new_m_w_ple_proj': 'new_m', 'new_m_g_ple': 'new_m', 'new_m_w_ple_gate': 'new_m', 'new_v_g_mix_pre': 'new_v', 'new_v_w_in': 'new_v', 'new_v_b_forget': 'new_v', 'new_v_g_attn_grp': 'new_v', 'new_v_g_pool_grp': 'new_v', 'new_v_w_pool': 'new_v', 'new_v_pool_scale': 'new_v', 'new_v_w_out': 'new_v', 'new_v_g_mix_post': 'new_v', 'new_v_g_ffn_pre': 'new_v', 'new_v_w_ffn_gate': 'new_v', 'new_v_w_ffn_up': 'new_v', 'new_v_w_ffn_down': 'new_v', 'new_v_g_ffn_post': 'new_v', 'new_v_w_ple_proj': 'new_v', 'new_v_g_ple': 'new_v', 'new_v_w_ple_gate': 'new_v'}


def _forward(args):
    return _fwd_reference(*[args[k] for k in FWD_PARAMS])


def _output_shape():
    out = _jax.eval_shape(lambda: _forward(_fwd_setup_inputs(0)))
    return out.shape, out.dtype

N_MICROBATCH = 1
ADAM_LR = 0.001
ADAM_B1 = 0.9
ADAM_B2 = 0.999
ADAM_EPS = 1e-08
ADAM_WD = 0.01
ADAM_STEP = 10
PER_EXAMPLE_BATCH_AXIS = {'x': 0, 'p': 1, 'loss_target': 0}
SHARED_INPUTS = []
_WEIGHT_DTYPES = {'g_mix_pre': _jnp.float32, 'w_in': _jnp.float32, 'b_forget': _jnp.float32, 'g_attn_grp': _jnp.float32, 'g_pool_grp': _jnp.float32, 'w_pool': _jnp.float32, 'pool_scale': _jnp.float32, 'w_out': _jnp.float32, 'g_mix_post': _jnp.float32, 'g_ffn_pre': _jnp.float32, 'w_ffn_gate': _jnp.float32, 'w_ffn_up': _jnp.float32, 'w_ffn_down': _jnp.float32, 'g_ffn_post': _jnp.float32, 'w_ple_proj': _jnp.float32, 'g_ple': _jnp.float32, 'w_ple_gate': _jnp.float32}
MOMENT_SCALE = {'g_mix_pre': 7.815371e-01, 'w_in': 5.235322e-01, 'b_forget': 2.813783e+00, 'g_attn_grp': 6.268444e-01, 'g_pool_grp': 8.311727e-01, 'w_pool': 7.895398e-01, 'pool_scale': 8.287884e-01, 'w_out': 7.137836e-01, 'g_mix_post': 3.233373e+01, 'g_ffn_pre': 6.637610e-01, 'w_ffn_gate': 2.125261e-01, 'w_ffn_up': 3.069143e-01, 'w_ffn_down': 5.089246e-01, 'g_ffn_post': 3.239291e+01, 'w_ple_proj': 2.148743e-01, 'g_ple': 1.072778e+01, 'w_ple_gate': 1.363386e-01}


def _to_microbatches(a, axis):
    t = _jnp.moveaxis(a, axis, 0)
    t = t.reshape((N_MICROBATCH, t.shape[0] // N_MICROBATCH) + t.shape[1:])
    return _jnp.moveaxis(t, 1, axis + 1)


def setup_inputs(seed: int = 0) -> dict:
    inp = _fwd_setup_inputs(seed)
    key = _jax.random.fold_in(_jax.random.key(seed), 7919)
    shape, _ = _output_shape()
    out = dict(inp)
    out["loss_target"] = _jax.random.normal(_jax.random.fold_in(key, 0), shape, _jnp.float32)
    for i, name in enumerate(TWIN_WEIGHTS):
        w = inp[name].astype(_jnp.float32)
        if MOMENT_SCALE is None:
            s = _jnp.sqrt(_jnp.mean(_jnp.square(w)) + 1e-30)
        else:
            s = MOMENT_SCALE[name]
        km, kv = _jax.random.split(_jax.random.fold_in(key, i + 1))
        out[name] = w
        out["m_" + name] = s * _jax.random.normal(km, w.shape, _jnp.float32)
        out["v_" + name] = (s * s) * _jax.random.uniform(kv, w.shape, _jnp.float32, 0.5, 1.5)
    if N_MICROBATCH > 1:
        for name, axis in PER_EXAMPLE_BATCH_AXIS.items():
            out[name] = _to_microbatches(out[name], axis)
    return {'x': out['x'], 'p': out['p'], 'g_mix_pre': out['g_mix_pre'], 'w_in': out['w_in'], 'b_forget': out['b_forget'], 'g_attn_grp': out['g_attn_grp'], 'g_pool_grp': out['g_pool_grp'], 'w_pool': out['w_pool'], 'pool_scale': out['pool_scale'], 'w_out': out['w_out'], 'g_mix_post': out['g_mix_post'], 'g_ffn_pre': out['g_ffn_pre'], 'w_ffn_gate': out['w_ffn_gate'], 'w_ffn_up': out['w_ffn_up'], 'w_ffn_down': out['w_ffn_down'], 'g_ffn_post': out['g_ffn_post'], 'w_ple_proj': out['w_ple_proj'], 'g_ple': out['g_ple'], 'w_ple_gate': out['w_ple_gate'], 'loss_target': out['loss_target'], 'm_g_mix_pre': out['m_g_mix_pre'], 'm_w_in': out['m_w_in'], 'm_b_forget': out['m_b_forget'], 'm_g_attn_grp': out['m_g_attn_grp'], 'm_g_pool_grp': out['m_g_pool_grp'], 'm_w_pool': out['m_w_pool'], 'm_pool_scale': out['m_pool_scale'], 'm_w_out': out['m_w_out'], 'm_g_mix_post': out['m_g_mix_post'], 'm_g_ffn_pre': out['m_g_ffn_pre'], 'm_w_ffn_gate': out['m_w_ffn_gate'], 'm_w_ffn_up': out['m_w_ffn_up'], 'm_w_ffn_down': out['m_w_ffn_down'], 'm_g_ffn_post': out['m_g_ffn_post'], 'm_w_ple_proj': out['m_w_ple_proj'], 'm_g_ple': out['m_g_ple'], 'm_w_ple_gate': out['m_w_ple_gate'], 'v_g_mix_pre': out['v_g_mix_pre'], 'v_w_in': out['v_w_in'], 'v_b_forget': out['v_b_forget'], 'v_g_attn_grp': out['v_g_attn_grp'], 'v_g_pool_grp': out['v_g_pool_grp'], 'v_w_pool': out['v_w_pool'], 'v_pool_scale': out['v_pool_scale'], 'v_w_out': out['v_w_out'], 'v_g_mix_post': out['v_g_mix_post'], 'v_g_ffn_pre': out['v_g_ffn_pre'], 'v_w_ffn_gate': out['v_w_ffn_gate'], 'v_w_ffn_up': out['v_w_ffn_up'], 'v_w_ffn_down': out['v_w_ffn_down'], 'v_g_ffn_post': out['v_g_ffn_post'], 'v_w_ple_proj': out['v_w_ple_proj'], 'v_g_ple': out['v_g_ple'], 'v_w_ple_gate': out['v_w_ple_gate']}


def _loss(weights, diff, rest, loss_target):
    with _jax.named_scope("forward"):
        args = {**rest, TWIN_DIFF_INPUT: diff, **{k: w.astype(_WEIGHT_DTYPES[k]) for k, w in weights.items()}}
        y = _forward(args)
    with _jax.named_scope("loss_head"):
        err = _jnp.square(y.astype(_jnp.float32) - loss_target)
        return 0.5 * _jnp.sum(_jnp.mean(err, axis=-1)) if err.ndim else 0.5 * err


def _adamw(w, g, m, v):
    m = ADAM_B1 * m + (1.0 - ADAM_B1) * g
    v = ADAM_B2 * v + (1.0 - ADAM_B2) * _jnp.square(g)
    m_hat = m / (1.0 - ADAM_B1 ** ADAM_STEP)
    v_hat = v / (1.0 - ADAM_B2 ** ADAM_STEP)
    delta = -ADAM_LR * (m_hat / (_jnp.sqrt(v_hat) + ADAM_EPS) + ADAM_WD * w)
    return delta, m, v


def reference(x, p, g_mix_pre, w_in, b_forget, g_attn_grp, g_pool_grp, w_pool, pool_scale, w_out, g_mix_post, g_ffn_pre, w_ffn_gate, w_ffn_up, w_ffn_down, g_ffn_post, w_ple_proj, g_ple, w_ple_gate, loss_target, m_g_mix_pre, m_w_in, m_b_forget, m_g_attn_grp, m_g_pool_grp, m_w_pool, m_pool_scale, m_w_out, m_g_mix_post, m_g_ffn_pre, m_w_ffn_gate, m_w_ffn_up, m_w_ffn_down, m_g_ffn_post, m_w_ple_proj, m_g_ple, m_w_ple_gate, v_g_mix_pre, v_w_in, v_b_forget, v_g_attn_grp, v_g_pool_grp, v_w_pool, v_pool_scale, v_w_out, v_g_mix_post, v_g_ffn_pre, v_w_ffn_gate, v_w_ffn_up, v_w_ffn_down, v_g_ffn_post, v_w_ple_proj, v_g_ple, v_w_ple_gate):
    given = dict(x=x, p=p, g_mix_pre=g_mix_pre, w_in=w_in, b_forget=b_forget, g_attn_grp=g_attn_grp, g_pool_grp=g_pool_grp, w_pool=w_pool, pool_scale=pool_scale, w_out=w_out, g_mix_post=g_mix_post, g_ffn_pre=g_ffn_pre, w_ffn_gate=w_ffn_gate, w_ffn_up=w_ffn_up, w_ffn_down=w_ffn_down, g_ffn_post=g_ffn_post, w_ple_proj=w_ple_proj, g_ple=g_ple, w_ple_gate=w_ple_gate, loss_target=loss_target, m_g_mix_pre=m_g_mix_pre, m_w_in=m_w_in, m_b_forget=m_b_forget, m_g_attn_grp=m_g_attn_grp, m_g_pool_grp=m_g_pool_grp, m_w_pool=m_w_pool, m_pool_scale=m_pool_scale, m_w_out=m_w_out, m_g_mix_post=m_g_mix_post, m_g_ffn_pre=m_g_ffn_pre, m_w_ffn_gate=m_w_ffn_gate, m_w_ffn_up=m_w_ffn_up, m_w_ffn_down=m_w_ffn_down, m_g_ffn_post=m_g_ffn_post, m_w_ple_proj=m_w_ple_proj, m_g_ple=m_g_ple, m_w_ple_gate=m_w_ple_gate, v_g_mix_pre=v_g_mix_pre, v_w_in=v_w_in, v_b_forget=v_b_forget, v_g_attn_grp=v_g_attn_grp, v_g_pool_grp=v_g_pool_grp, v_w_pool=v_w_pool, v_pool_scale=v_pool_scale, v_w_out=v_w_out, v_g_mix_post=v_g_mix_post, v_g_ffn_pre=v_g_ffn_pre, v_w_ffn_gate=v_w_ffn_gate, v_w_ffn_up=v_w_ffn_up, v_w_ffn_down=v_w_ffn_down, v_g_ffn_post=v_g_ffn_post, v_w_ple_proj=v_w_ple_proj, v_g_ple=v_g_ple, v_w_ple_gate=v_w_ple_gate)
    weights = {n: given[n] for n in TWIN_WEIGHTS}
    shared = {n: given[n] for n in SHARED_INPUTS}
    per_example = {n: given[n] for n in ['x', 'p']}
    grad_fn = _jax.value_and_grad(_loss, argnums=(0, 1))

    def one_microbatch(ex, loss_target):
        ex = dict(ex)
        diff = ex.pop(TWIN_DIFF_INPUT)
        return grad_fn(weights, diff, {**shared, **ex}, loss_target)

    if N_MICROBATCH == 1:
        loss, (grad_w, grad_x) = one_microbatch(per_example, given["loss_target"])
    else:
        def body(carry, xs):
            loss_sum, grad_sum = carry
            l_k, (gw_k, gx_k) = one_microbatch(xs[0], xs[1])
            with _jax.named_scope("update"):
                return (loss_sum + l_k, _jax.tree.map(_jnp.add, grad_sum, gw_k)), gx_k

        init = (_jnp.zeros((), _jnp.float32), _jax.tree.map(_jnp.zeros_like, weights))
        (loss, grad_w), grad_x = _jax.lax.scan(body, init, (per_example, given["loss_target"]))
    with _jax.named_scope("update"):
        delta_w, new_m, new_v = {}, {}, {}
        for n in TWIN_WEIGHTS:
            delta_w[n], new_m[n], new_v[n] = _adamw(weights[n], grad_w[n], given["m_" + n], given["v_" + n])
    return (loss, grad_x, *[grad_w[n] for n in TWIN_WEIGHTS], *[delta_w[n] for n in TWIN_WEIGHTS],
            *[new_m[n] for n in TWIN_WEIGHTS], *[new_v[n] for n in TWIN_WEIGHTS])
```

```python
import functools
import math

import jax
import jax.numpy as jnp
from jax import lax
from jax.experimental import pallas as pl
from jax.experimental.pallas import tpu as pltpu

F32 = jnp.float32
BF = jnp.bfloat16
MESH = pl.DeviceIdType.MESH

D = 1024
DA = 512
DP = 512
NH = 8
HD = 64
DFF = 2816
DPLE = 256
WINS = (2, 4, 8, 16)
PC = 128
ZW = 3 * DA + 128 + DP
EPS = 1e-6
NSHARD = 4

LANE = 128
HALO = 128

IN_SH = 514
IN_PAD = 528
O_IN, O_OUT, O_G, O_U, O_D, O_PLE, O_PG = 0, 528, 784, 1488, 2192, 2896, 2960
STACK_USED = 3216
STACK_ROWS = 3328
HALF = STACK_ROWS // 2
RED_ROWS = 128

ADAM_LR, ADAM_B1, ADAM_B2, ADAM_EPS, ADAM_WD, ADAM_STEP = 0.001, 0.9, 0.999, 1e-8, 0.01, 10

VMEM_LIMIT = 56 * 1024 * 1024


def _cp(**kw):
    return pltpu.CompilerParams(vmem_limit_bytes=VMEM_LIMIT, **kw)


def _mm(a, b):
    return jnp.dot(a.astype(BF), b.astype(BF), preferred_element_type=F32)


def _mm_nt(a, b):
    return lax.dot_general(a.astype(BF), b.astype(BF), (((1,), (1,)), ((), ())), preferred_element_type=F32)


def _mm_tn(a, b):
    return lax.dot_general(a.astype(BF), b.astype(BF), (((0,), (0,)), ((), ())), preferred_element_type=F32)


def _split2(x):
    hi = x.astype(BF)
    lo = (x - hi.astype(F32)).astype(BF)
    return hi, lo


def _split3(x):
    hi = x.astype(BF)
    r = x - hi.astype(F32)
    mid = r.astype(BF)
    lo = (r - mid.astype(F32)).astype(BF)
    return hi, mid, lo


def _dot3(m, x):
    hi, mid, lo = _split3(x)
    return (jnp.dot(m, hi, preferred_element_type=F32) + jnp.dot(m, mid, preferred_element_type=F32)
            + jnp.dot(m, lo, preferred_element_type=F32))


def _dot2(m, x):
    hi, lo = _split2(x)
    return jnp.dot(m, hi, preferred_element_type=F32) + jnp.dot(m, lo, preferred_element_type=F32)


def _rstd(x):
    return lax.rsqrt(jnp.mean(x * x, axis=-1, keepdims=True) + EPS)


def _rms_bwd(dy, x, g):
    r = _rstd(x)
    xh = x * r
    dg = jnp.sum(dy * xh, axis=0, keepdims=True)
    dxh = dy * g
    dx = r * (dxh - xh * jnp.mean(dxh * xh, axis=-1, keepdims=True))
    return dx, dg


def _sigmoid(x):
    return 1.0 / (1.0 + jnp.exp(-x))


def _full(shape):
    n = len(shape)
    return pl.BlockSpec(shape, lambda *_: (0,) * n)


def _resident(shape):
    n = len(shape)
    return pl.BlockSpec(shape, lambda *_: (0,) * n, pipeline_mode=pl.Buffered(1))


def _tile(t):
    return 512 if t % 512 == 0 else t


def _tri(n, upper):
    r = lax.broadcasted_iota(jnp.int32, (n, n), 0)
    c = lax.broadcasted_iota(jnp.int32, (n, n), 1)
    return ((c >= r) if upper else (c <= r)).astype(BF)


def _band(tt, transpose):
    r = lax.broadcasted_iota(jnp.int32, (tt, tt + HALO), 0)
    c = lax.broadcasted_iota(jnp.int32, (tt, tt + HALO), 1)
    d = (c - r) if transpose else (r + HALO - c)
    return jnp.stack([((d >= 0) & (d < w)).astype(BF) for w in WINS])


def _aug_consts():
    row = lax.broadcasted_iota(jnp.int32, (3 * LANE, NH * LANE), 0)
    col = lax.broadcasted_iota(jnp.int32, (3 * LANE, NH * LANE), 1)
    piece, head = row // LANE, row % LANE
    ch, cl = col // LANE, col % LANE
    eq = ((head == ch) & (cl == HD + piece)).astype(BF)
    ek = -((head == ch) & (cl == HD + 3 + piece)).astype(BF)
    lane = lax.broadcasted_iota(jnp.int32, (1, NH * LANE), 1) % LANE
    rowq = ((lane >= HD + 3) & (lane < HD + 6)).astype(F32)
    rowk = ((lane >= HD) & (lane < HD + 3)).astype(F32)
    r2 = lax.broadcasted_iota(jnp.int32, (NH * LANE, LANE), 0)
    c2 = lax.broadcasted_iota(jnp.int32, (NH * LANE, LANE), 1)
    selq = ((r2 // LANE == c2) & (r2 % LANE == HD)).astype(BF)
    selk = ((r2 // LANE == c2) & (r2 % LANE == HD + 3)).astype(BF)
    return eq, ek, rowq, rowk, selq, selk


def _in_proj(x, g1, w_in_t, b_pad, tri, eq, ek, rowq, rowk):
    t = x.shape[0]
    tt = _tile(t)

    def body(x_ref, g_ref, w_ref, b_ref, tri_ref, eq_ref, ek_ref, rq_ref, rk_ref,
             qa_ref, ka_ref, v_ref, u_ref, fl_ref, carry):
        i = pl.program_id(0)

        @pl.when(i == 0)
        def _():
            carry[...] = jnp.zeros_like(carry)

        xv = x_ref[...]
        hn = (xv * _rstd(xv) * g_ref[...]).astype(BF)
        z = _mm_nt(hn, w_ref[...])
        fl = z[:, 3 * DA:3 * DA + LANE] + b_ref[...]
        lane = lax.broadcasted_iota(jnp.int32, fl.shape, 1)
        lf = jnp.where(lane < NH, jnp.minimum(fl, 0.0) - jnp.log(1.0 + jnp.exp(-jnp.abs(fl))), 0.0)
        c = carry[...] + _dot3(tri_ref[...], lf)
        carry[...] = carry[...] + jnp.sum(lf, axis=0, keepdims=True)
        caug = jnp.concatenate(_split3(c), axis=1)
        aug_q = jnp.dot(caug, eq_ref[...], preferred_element_type=F32) + rq_ref[...]
        aug_k = jnp.dot(caug, ek_ref[...], preferred_element_type=F32) + rk_ref[...]
        low = lax.broadcasted_iota(jnp.int32, (tt, LANE), 1) < HD
        for p in range(NH // 2):
            qp = z[:, LANE * p:LANE * (p + 1)] * (1.0 / math.sqrt(HD))
            kp = z[:, DA + LANE * p:DA + LANE * (p + 1)]
            for h, (qh, kh) in enumerate(((qp, kp), (pltpu.roll(qp, HD, 1), pltpu.roll(kp, HD, 1)))):
                lo_, hi_ = LANE * (2 * p + h), LANE * (2 * p + h + 1)
                qa_ref[:, lo_:hi_] = jnp.where(low, qh, aug_q[:, lo_:hi_]).astype(BF)
                ka_ref[:, lo_:hi_] = jnp.where(low, kh, aug_k[:, lo_:hi_]).astype(BF)
        v_ref[...] = z[:, 2 * DA:3 * DA].astype(BF)
        u_ref[...] = z[:, 3 * DA + LANE:]
        fl_ref[...] = fl

    return pl.pallas_call(
        body, name="in_proj", grid=(t // tt,),
        in_specs=[pl.BlockSpec((tt, D), lambda i: (i, 0)), _full((1, D)), _resident((ZW, D)), _full((1, LANE)),
                  _full((tt, tt)), _full((3 * LANE, NH * LANE)), _full((3 * LANE, NH * LANE)),
                  _full((1, NH * LANE)), _full((1, NH * LANE))],
        out_specs=[pl.BlockSpec((tt, NH * LANE), lambda i: (i, 0)), pl.BlockSpec((tt, NH * LANE), lambda i: (i, 0)),
                   pl.BlockSpec((tt, DA), lambda i: (i, 0)), pl.BlockSpec((tt, DP), lambda i: (i, 0)),
                   pl.BlockSpec((tt, LANE), lambda i: (i, 0))],
        out_shape=[jax.ShapeDtypeStruct((t, NH * LANE), BF), jax.ShapeDtypeStruct((t, NH * LANE), BF),
                   jax.ShapeDtypeStruct((t, DA), BF), jax.ShapeDtypeStruct((t, DP), F32),
                   jax.ShapeDtypeStruct((t, LANE), F32)],
        scratch_shapes=[pltpu.VMEM((1, LANE), F32)],
        compiler_params=_cp(),
    )(x, g1, w_in_t, b_pad, tri, eq, ek, rowq, rowk)


def _attn_fwd(qa, ka, v):
    t = qa.shape[0]
    ta = _tile(t)
    n = t // ta

    def body(q_ref, k_ref, v_ref, a_ref, lse_ref, m_ref, l_ref, acc_ref):
        i, j = pl.program_id(1), pl.program_id(2)

        @pl.when(j == 0)
        def _():
            m_ref[...] = jnp.full_like(m_ref, -1e30)
            l_ref[...] = jnp.zeros_like(l_ref)
            acc_ref[...] = jnp.zeros_like(acc_ref)

        @pl.when(j <= i)
        def _():
            row = lax.broadcasted_iota(jnp.int32, (ta, ta), 0)
            col = lax.broadcasted_iota(jnp.int32, (ta, ta), 1)
            keep = col <= row + jnp.where(j < i, ta, 0)
            v2 = v_ref[...]
            for h in range(2):
                s = _mm_nt(q_ref[:, LANE * h:LANE * (h + 1)], k_ref[:, LANE * h:LANE * (h + 1)])
                s = jnp.where(keep, s, -1e30)
                m_old = m_ref[h]
                m_new = jnp.maximum(m_old, jnp.max(s, axis=1, keepdims=True))
                pe = jnp.exp(s - m_new)
                alpha = jnp.exp(m_old - m_new)
                l_ref[h] = alpha * l_ref[h] + jnp.sum(pe, axis=1, keepdims=True)
                acc_ref[h] = alpha * acc_ref[h] + jnp.dot(pe.astype(BF), v2, preferred_element_type=F32)
                m_ref[h] = m_new

        @pl.when(j == i)
        def _():
            low = lax.broadcasted_iota(jnp.int32, (ta, LANE), 1) < HD
            a_ref[...] = jnp.where(low, acc_ref[0] / l_ref[0], acc_ref[1] / l_ref[1])
            lse_ref[...] = jnp.where(low, m_ref[0] + jnp.log(l_ref[0]), m_ref[1] + jnp.log(l_ref[1]))

    return pl.pallas_call(
        body, name="attn_fwd", grid=(NH // 2, n, n),
        in_specs=[pl.BlockSpec((ta, 2 * LANE), lambda p, i, j: (i, p)),
                  pl.BlockSpec((ta, 2 * LANE), lambda p, i, j: (jnp.minimum(i, j), p)),
                  pl.BlockSpec((ta, LANE), lambda p, i, j: (jnp.minimum(i, j), p))],
        out_specs=[pl.BlockSpec((ta, LANE), lambda p, i, j: (i, p)), pl.BlockSpec((ta, LANE), lambda p, i, j: (i, p))],
        out_shape=[jax.ShapeDtypeStruct((t, DA), F32), jax.ShapeDtypeStruct((t, DA), F32)],
        scratch_shapes=[pltpu.VMEM((2, ta, 1), F32), pltpu.VMEM((2, ta, 1), F32), pltpu.VMEM((2, ta, LANE), F32)],
        compiler_params=_cp(),
    )(qa, ka, v)


def _mix_out(a, u, x, band, w_pool, pool_scale, g_attn, g_pool, w_out, g_post):
    t = a.shape[0]
    tt = _tile(t)
    hb = tt // HALO

    def body(a_ref, u_ref, up_ref, x_ref, band_ref, wp_ref, ps_ref, ga_ref, gp_ref, wo_ref, go_ref,
             yb_ref, m_ref, o_ref, h1_ref):
        i = pl.program_id(0)
        prev = up_ref[...] * jnp.where(i > 0, 1.0, 0.0)
        tok = i * tt + lax.broadcasted_iota(jnp.int32, (tt, PC), 0)
        ms = []
        for g, w in enumerate(WINS):
            ug = u_ref[:, PC * g:PC * (g + 1)]
            ext = jnp.concatenate([prev[:, PC * g:PC * (g + 1)], ug], axis=0)
            cnt = jnp.minimum(tok + 1, w).astype(F32)
            y = (_dot2(band_ref[g], ext) / cnt - ug).astype(BF)
            yb_ref[:, PC * g:PC * (g + 1)] = y
            ms.append(_mm(y, wp_ref[g]) * ps_ref[:, PC * g:PC * (g + 1)])
        m = jnp.concatenate(ms, axis=1)
        m_ref[...] = m
        av = a_ref[...]
        mix = jnp.concatenate([av * _rstd(av) * ga_ref[...], m * _rstd(m) * gp_ref[...]], axis=1)
        o = _mm(mix, wo_ref[...])
        o_ref[...] = o
        h1_ref[...] = x_ref[...] + o * _rstd(o) * go_ref[...]

    return pl.pallas_call(
        body, name="mix_out", grid=(t // tt,),
        in_specs=[pl.BlockSpec((tt, DA), lambda i: (i, 0)), pl.BlockSpec((tt, DP), lambda i: (i, 0)),
                  pl.BlockSpec((HALO, DP), lambda i: (jnp.maximum(i * hb - 1, 0), 0)),
                  pl.BlockSpec((tt, D), lambda i: (i, 0)), _full((len(WINS), tt, tt + HALO)),
                  _full((len(WINS), PC, PC)), _full((1, DP)), _full((1, DA)), _full((1, DP)),
                  _resident((D, D)), _full((1, D))],
        out_specs=[pl.BlockSpec((tt, DP), lambda i: (i, 0)), pl.BlockSpec((tt, DP), lambda i: (i, 0)),
                   pl.BlockSpec((tt, D), lambda i: (i, 0)), pl.BlockSpec((tt, D), lambda i: (i, 0))],
        out_shape=[jax.ShapeDtypeStruct((t, DP), BF), jax.ShapeDtypeStruct((t, DP), F32),
                   jax.ShapeDtypeStruct((t, D), F32), jax.ShapeDtypeStruct((t, D), F32)],
        compiler_params=_cp(),
    )(a, u, u, x, band, w_pool, pool_scale, g_attn, g_pool, w_out, g_post)


def _ffn_fwd(h1, g_pre, wg_t, wu_t, wd, g_post, p, w_ple, g_ple, w_pg, tgt):
    t = h1.shape[0]
    tt = 256 if t % 256 == 0 else t

    def body(h1_ref, gpre_ref, wg_ref, wu_ref, wd_ref, gpost_ref, p_ref, wple_ref, gple_ref, wpg_ref, tgt_ref,
             hn_ref, gate_ref, up_ref, dff_ref, dh2_ref, loss_ref, dwpg_ref, dwple_ref, dgple_ref, dgpost_ref):
        i = pl.program_id(0)

        @pl.when(i == 0)
        def _():
            loss_ref[...] = jnp.zeros_like(loss_ref)
            dwpg_ref[...] = jnp.zeros_like(dwpg_ref)
            dwple_ref[...] = jnp.zeros_like(dwple_ref)
            dgple_ref[...] = jnp.zeros_like(dgple_ref)
            dgpost_ref[...] = jnp.zeros_like(dgpost_ref)

        h1v = h1_ref[...]
        hn = (h1v * _rstd(h1v) * gpre_ref[...]).astype(BF)
        hn_ref[...] = hn
        gate = _mm_nt(hn, wg_ref[...])
        up = _mm_nt(hn, wu_ref[...])
        gate_ref[...] = gate.astype(BF)
        up_ref[...] = up.astype(BF)
        ff = _mm(gate * _sigmoid(gate) * up, wd_ref[...])
        rff = _rstd(ff)
        ffh = ff * rff
        gpost = gpost_ref[...]
        h2 = h1v + ffh * gpost
        pv = p_ref[...]
        pe = _mm(pv, wple_ref[...])
        rpe = _rstd(pe)
        peh = pe * rpe
        gple = gple_ref[...]
        e = peh * gple
        sig = _sigmoid(_mm(h2, wpg_ref[...]))
        dv = h2 + sig * e - tgt_ref[...]
        sq = jnp.sum(jnp.sum(dv * dv, axis=1, keepdims=True), axis=0, keepdims=True)
        loss_ref[...] = loss_ref[...] + sq
        dy = dv * (1.0 / D)
        d_e = dy * sig
        d_gl = dy * e * sig * (1.0 - sig)
        dh2 = dy + _mm_nt(d_gl, wpg_ref[...])
        dh2_ref[...] = dh2
        dwpg_ref[...] = dwpg_ref[...] + _mm_tn(h2, d_gl)
        dgple_ref[...] = dgple_ref[...] + jnp.sum(d_e * peh, axis=0, keepdims=True)
        dpeh = d_e * gple
        d_pe = rpe * (dpeh - peh * jnp.mean(dpeh * peh, axis=-1, keepdims=True))
        dwple_ref[...] = dwple_ref[...] + _mm_tn(pv, d_pe)
        dgpost_ref[...] = dgpost_ref[...] + jnp.sum(dh2 * ffh, axis=0, keepdims=True)
        dffh = dh2 * gpost
        dff_ref[...] = (rff * (dffh - ffh * jnp.mean(dffh * ffh, axis=-1, keepdims=True))).astype(BF)

    row = lambda w: pl.BlockSpec((tt, w), lambda i: (i, 0))
    return pl.pallas_call(
        body, name="ffn_fwd", grid=(t // tt,),
        in_specs=[row(D), _full((1, D)), _resident((DFF, D)), _resident((DFF, D)), _resident((DFF, D)), _full((1, D)),
                  row(DPLE), _resident((DPLE, D)), _full((1, D)), _resident((D, D)), row(D)],
        out_specs=[row(D), row(DFF), row(DFF), row(D), row(D), _full((8, LANE)), _full((D, D)), _full((DPLE, D)),
                   _full((1, D)), _full((1, D))],
        out_shape=[jax.ShapeDtypeStruct((t, D), BF), jax.ShapeDtypeStruct((t, DFF), BF), jax.ShapeDtypeStruct((t, DFF), BF),
                   jax.ShapeDtypeStruct((t, D), BF), jax.ShapeDtypeStruct((t, D), F32), jax.ShapeDtypeStruct((8, LANE), F32),
                   jax.ShapeDtypeStruct((D, D), F32), jax.ShapeDtypeStruct((DPLE, D), F32),
                   jax.ShapeDtypeStruct((1, D), F32), jax.ShapeDtypeStruct((1, D), F32)],
        compiler_params=_cp(),
    )(h1, g_pre, wg_t, wu_t, wd, g_post, p, w_ple, g_ple, w_pg, tgt)


def _ffn_bwd(hn2, gate, up, dff, wg_t, wu_t, wd):
    t = hn2.shape[0]
    tt = _tile(t)
    nt = t // tt
    ch = 256
    nc = DFF // ch

    def body(hn_ref, gate_ref, up_ref, dff_ref, wg_ref, wu_ref, wd_ref,
             dwg_ref, dwu_ref, dwd_ref, dhn_ref, acc, sem):
        j, i = pl.program_id(0), pl.program_id(1)
        gate_v = gate_ref[...].astype(F32)
        up_v = up_ref[...].astype(F32)
        dffv = dff_ref[...]
        hn = hn_ref[...]
        sg = _sigmoid(gate_v)
        silu = gate_v * sg
        d_act = _mm_nt(dffv, wd_ref[...])
        d_up = (d_act * silu).astype(BF)
        d_gate = (d_act * up_v * (sg * (1.0 + gate_v * (1.0 - sg)))).astype(BF)

        @pl.when(i == 0)
        def _():
            dwg_ref[...] = jnp.zeros_like(dwg_ref)
            dwu_ref[...] = jnp.zeros_like(dwu_ref)
            dwd_ref[...] = jnp.zeros_like(dwd_ref)

        dwd_ref[...] = dwd_ref[...] + _mm_tn(silu * up_v, dffv)
        dwg_ref[...] = dwg_ref[...] + _mm_tn(d_gate, hn)
        dwu_ref[...] = dwu_ref[...] + _mm_tn(d_up, hn)
        contrib = _mm(d_gate, wg_ref[...]) + _mm(d_up, wu_ref[...])
        rows = pl.ds(pl.multiple_of(i * tt, tt), tt)

        @pl.when(j == 0)
        def _():
            acc[rows, :] = contrib

        @pl.when(j > 0)
        def _():
            acc[rows, :] = acc[rows, :] + contrib

        @pl.when((j == nc - 1) & (i == nt - 1))
        def _():
            cp = pltpu.make_async_copy(acc, dhn_ref, sem)
            cp.start()
            cp.wait()

    tok = lambda w: pl.BlockSpec((tt, w), lambda j, i: (i, 0))
    chunk = pl.BlockSpec((ch, D), lambda j, i: (j, 0))
    return pl.pallas_call(
        body, name="ffn_bwd", grid=(nc, nt),
        in_specs=[tok(D), pl.BlockSpec((tt, ch), lambda j, i: (i, j)), pl.BlockSpec((tt, ch), lambda j, i: (i, j)),
                  tok(D), chunk, chunk, chunk],
        out_specs=[chunk, chunk, chunk, pl.BlockSpec(memory_space=pl.ANY)],
        out_shape=[jax.ShapeDtypeStruct((DFF, D), F32), jax.ShapeDtypeStruct((DFF, D), F32),
                   jax.ShapeDtypeStruct((DFF, D), F32), jax.ShapeDtypeStruct((t, D), F32)],
        scratch_shapes=[pltpu.VMEM((t, D), F32), pltpu.SemaphoreType.DMA],
        compiler_params=_cp(),
    )(hn2, gate, up, dff, wg_t, wu_t, wd)


def _mix_bwd(d_hn2, dh2, h1, o, a, m, yb, g_ffn_pre, g_post, g_attn, g_pool, w_out, w_pool, pool_scale):
    t = a.shape[0]
    tt = 256 if t % 256 == 0 else t

    def body(dhn_ref, dh2_ref, h1_ref, o_ref, a_ref, m_ref, yb_ref, gfp_ref, go_ref, ga_ref, gp_ref, wo_ref, wp_ref,
             ps_ref, dh1_ref, da_ref, dyc_ref, dgfp_ref, dgo_ref, dga_ref, dgp_ref, dps_ref, dwp_ref, dwo_ref):
        i = pl.program_id(0)

        @pl.when(i == 0)
        def _():
            for r in (dgfp_ref, dgo_ref, dga_ref, dgp_ref, dps_ref, dwp_ref, dwo_ref):
                r[...] = jnp.zeros_like(r)

        d1, dg = _rms_bwd(dhn_ref[...], h1_ref[...], gfp_ref[...])
        dgfp_ref[...] = dgfp_ref[...] + dg
        dh1 = dh2_ref[...] + d1
        dh1_ref[...] = dh1
        d_o, dg = _rms_bwd(dh1, o_ref[...], go_ref[...])
        dgo_ref[...] = dgo_ref[...] + dg
        d_mix = _mm_nt(d_o, wo_ref[...])
        av, mv = a_ref[...], m_ref[...]
        mix = jnp.concatenate([av * _rstd(av) * ga_ref[...], mv * _rstd(mv) * gp_ref[...]], axis=1)
        dwo_ref[...] = dwo_ref[...] + _mm_tn(mix, d_o)
        d_a, dg = _rms_bwd(d_mix[:, :DA], av, ga_ref[...])
        dga_ref[...] = dga_ref[...] + dg
        da_ref[...] = d_a
        d_m, dg = _rms_bwd(d_mix[:, DA:], mv, gp_ref[...])
        dgp_ref[...] = dgp_ref[...] + dg
        tok = i * tt + lax.broadcasted_iota(jnp.int32, (tt, PC), 0)
        dps = []
        for g, w in enumerate(WINS):
            sl = slice(PC * g, PC * (g + 1))
            ybg = yb_ref[:, sl]
            wpg = wp_ref[g].astype(BF)
            mlin = jnp.dot(ybg, wpg, preferred_element_type=F32)
            dmg = d_m[:, sl]
            dps.append(jnp.sum(dmg * mlin, axis=0, keepdims=True))
            dml = (dmg * ps_ref[:, sl]).astype(BF)
            dwp_ref[g] = dwp_ref[g] + _mm_tn(ybg, dml)
            dyc_ref[:, sl] = _mm_nt(dml, wpg) / jnp.minimum(tok + 1, w).astype(F32)
        dps_ref[...] = dps_ref[...] + jnp.concatenate(dps, axis=1)

    row = lambda w: pl.BlockSpec((tt, w), lambda i: (i, 0))
    return pl.pallas_call(
        body, name="mix_bwd", grid=(t // tt,),
        in_specs=[row(D), row(D), row(D), row(D), row(DA), row(DP), row(DP), _full((1, D)), _full((1, D)),
                  _full((1, DA)), _full((1, DP)), _resident((D, D)), _full((len(WINS), PC, PC)), _full((1, DP))],
        out_specs=[row(D), row(DA), row(DP), _full((1, D)), _full((1, D)), _full((1, DA)), _full((1, DP)),
                   _full((1, DP)), _full((len(WINS), PC, PC)), _full((D, D))],
        out_shape=[jax.ShapeDtypeStruct((t, D), F32), jax.ShapeDtypeStruct((t, DA), F32), jax.ShapeDtypeStruct((t, DP), F32),
                   jax.ShapeDtypeStruct((1, D), F32), jax.ShapeDtypeStruct((1, D), F32), jax.ShapeDtypeStruct((1, DA), F32),
                   jax.ShapeDtypeStruct((1, DP), F32), jax.ShapeDtypeStruct((1, DP), F32),
                   jax.ShapeDtypeStruct((len(WINS), PC, PC), F32), jax.ShapeDtypeStruct((D, D), F32)],
        compiler_params=_cp(),
    )(d_hn2, dh2, h1, o, a, m, yb, g_ffn_pre, g_post, g_attn, g_pool, w_out, w_pool, pool_scale)


def _attn_bwd(qa, ka, v, a, d_a, lse):
    t = qa.shape[0]
    ta = _tile(t)
    n = t // ta

    def body(q_ref, k_ref, v_ref, o_ref, do_ref, lse_ref, dq_ref, dk_ref, dv_ref):
        j, i = pl.program_id(1), pl.program_id(2)

        @pl.when((j == 0) & (i == 0))
        def _():
            dq_ref[...] = jnp.zeros_like(dq_ref)

        @pl.when(i == 0)
        def _():
            dk_ref[...] = jnp.zeros_like(dk_ref)
            dv_ref[...] = jnp.zeros_like(dv_ref)

        @pl.when(i >= j)
        def _():
            lane = lax.broadcasted_iota(jnp.int32, (ta, LANE), 1)
            do2 = do_ref[...]
            prod = do2 * o_ref[...]
            lse2 = lse_ref[...]
            v2 = v_ref[...]
            do2b = do2.astype(BF)
            row = lax.broadcasted_iota(jnp.int32, (ta, ta), 0)
            col = lax.broadcasted_iota(jnp.int32, (ta, ta), 1)
            keep = col <= row + jnp.where(i > j, ta, 0)
            rows = pl.ds(pl.multiple_of(i * ta, ta), ta)
            for h in range(2):
                mine = (lane < HD) if h == 0 else (lane >= HD)
                delta = jnp.sum(jnp.where(mine, prod, 0.0), axis=1, keepdims=True)
                lse_h = jnp.sum(jnp.where(lane == HD * h, lse2, 0.0), axis=1, keepdims=True)
                qh = q_ref[:, LANE * h:LANE * (h + 1)]
                kh = k_ref[:, LANE * h:LANE * (h + 1)]
                s = _mm_nt(qh, kh)
                pr = jnp.where(keep, jnp.exp(s - lse_h), 0.0)
                dp = _mm_nt(jnp.where(mine, do2, 0.0), v2)
                ds = (pr * (dp - delta)).astype(BF)
                dv_ref[...] = dv_ref[...] + jnp.where(mine, _mm_tn(pr, do2b), 0.0)
                dk_ref[:, LANE * h:LANE * (h + 1)] = dk_ref[:, LANE * h:LANE * (h + 1)] + _mm_tn(ds, qh)
                dq_ref[0, rows, LANE * h:LANE * (h + 1)] = (dq_ref[0, rows, LANE * h:LANE * (h + 1)]
                                                           + jnp.dot(ds, kh, preferred_element_type=F32))

    qrow = lambda w: pl.BlockSpec((ta, w), lambda p, j, i: (jnp.maximum(i, j), p))
    krow = lambda w: pl.BlockSpec((ta, w), lambda p, j, i: (j, p))
    return pl.pallas_call(
        body, name="attn_bwd", grid=(NH // 2, n, n),
        in_specs=[qrow(2 * LANE), krow(2 * LANE), krow(LANE), qrow(LANE), qrow(LANE), qrow(LANE)],
        out_specs=[pl.BlockSpec((1, t, 2 * LANE), lambda p, j, i: (p, 0, 0)), krow(2 * LANE), krow(LANE)],
        out_shape=[jax.ShapeDtypeStruct((NH // 2, t, 2 * LANE), F32), jax.ShapeDtypeStruct((t, NH * LANE), F32),
                   jax.ShapeDtypeStruct((t, DA), F32)],
        compiler_params=_cp(),
    )(qa, ka, v, a, d_a, lse)


def _in_bwd(dqa, dka, dv, dyc, fl, x, dh1, g1, w_in_t, tri_u, selq, selk, band_t):
    t = x.shape[0]
    tt = _tile(t)
    nt = t // tt
    hb = tt // HALO
    rev = lambda s: nt - 1 - s

    def body(dqa_ref, dka_ref, dv_ref, dyc_ref, dyn_ref, fl_ref, x_ref, dh1_ref, g_ref, w_ref, tri_ref, sq_ref, sk_ref,
             band_ref, dx_ref, dw_ref, dg_ref, db_ref, carry, acc, sem):
        s = pl.program_id(0)
        i = nt - 1 - s

        @pl.when(s == 0)
        def _():
            carry[...] = jnp.zeros_like(carry)
            dg_ref[...] = jnp.zeros_like(dg_ref)
            db_ref[...] = jnp.zeros_like(db_ref)

        dq_cat = jnp.concatenate([dqa_ref[p] for p in range(NH // 2)], axis=1)
        dk_cat = dka_ref[...]
        dc = _dot2t(dq_cat, sq_ref[...]) - _dot2t(dk_cat, sk_ref[...])
        dlf = carry[...] + _dot3(tri_ref[...], dc)
        carry[...] = carry[...] + jnp.sum(dc, axis=0, keepdims=True)
        flv = fl_ref[...]
        lane = lax.broadcasted_iota(jnp.int32, flv.shape, 1)
        d_fl = jnp.where(lane < NH, dlf / (1.0 + jnp.exp(flv)), 0.0)
        db_ref[...] = db_ref[...] + jnp.sum(d_fl, axis=0, keepdims=True)
        low = lax.broadcasted_iota(jnp.int32, (tt, LANE), 1) < HD
        dqs, dks = [], []
        for p in range(NH // 2):
            b0, b1 = slice(2 * LANE * p, 2 * LANE * p + LANE), slice(2 * LANE * p + LANE, 2 * LANE * (p + 1))
            dqs.append(jnp.where(low, dq_cat[:, b0], pltpu.roll(dq_cat[:, b1], HD, 1)) * (1.0 / math.sqrt(HD)))
            dks.append(jnp.where(low, dk_cat[:, b0], pltpu.roll(dk_cat[:, b1], HD, 1)))
        nxt = dyn_ref[...] * jnp.where(i < nt - 1, 1.0, 0.0)
        tok = i * tt + lax.broadcasted_iota(jnp.int32, (tt, PC), 0)
        dus = []
        for g, w in enumerate(WINS):
            sl = slice(PC * g, PC * (g + 1))
            dycg = dyc_ref[:, sl]
            ext = jnp.concatenate([dycg, nxt[:, sl]], axis=0)
            dus.append(_dot2(band_ref[g], ext) - dycg * jnp.minimum(tok + 1, w).astype(F32))
        d_z = jnp.concatenate(dqs + dks + [dv_ref[...], d_fl] + dus, axis=1).astype(BF)
        xv = x_ref[...]
        gv = g_ref[...]
        hn = (xv * _rstd(xv) * gv).astype(BF)
        d_hn = jnp.dot(d_z, w_ref[...], preferred_element_type=F32)
        contrib = _mm_tn(d_z, hn)

        @pl.when(s == 0)
        def _():
            acc[...] = contrib

        @pl.when(s > 0)
        def _():
            acc[...] = acc[...] + contrib

        d1, dg = _rms_bwd(d_hn, xv, gv)
        dg_ref[...] = dg_ref[...] + dg
        dx_ref[...] = dh1_ref[...] + d1

        @pl.when(s == nt - 1)
        def _():
            cp = pltpu.make_async_copy(acc, dw_ref, sem)
            cp.start()
            cp.wait()

    row = lambda w: pl.BlockSpec((tt, w), lambda s: (rev(s), 0))
    return pl.pallas_call(
        body, name="in_bwd", grid=(nt,),
        in_specs=[pl.BlockSpec((NH // 2, tt, 2 * LANE), lambda s: (0, rev(s), 0)), row(NH * LANE), row(DA), row(DP),
                  pl.BlockSpec((HALO, DP), lambda s: (jnp.minimum((rev(s) + 1) * hb, nt * hb - 1), 0)),
                  row(LANE), row(D), row(D), _full((1, D)), _resident((ZW, D)), _full((tt, tt)),
                  _full((NH * LANE, LANE)), _full((NH * LANE, LANE)), _full((len(WINS), tt, tt + HALO))],
        out_specs=[row(D), pl.BlockSpec(memory_space=pl.ANY), _full((1, D)), _full((1, LANE))],
        out_shape=[jax.ShapeDtypeStruct((t, D), F32), jax.ShapeDtypeStruct((ZW, D), F32),
                   jax.ShapeDtypeStruct((1, D), F32), jax.ShapeDtypeStruct((1, LANE), F32)],
        scratch_shapes=[pltpu.VMEM((1, LANE), F32), pltpu.VMEM((ZW, D), F32), pltpu.SemaphoreType.DMA],
        compiler_params=_cp(),
    )(dqa, dka, dv, dyc, dyc, fl, x, dh1, g1, w_in_t, tri_u, selq, selk, band_t)


def _dot2t(x, sel):
    hi, lo = _split2(x)
    return jnp.dot(hi, sel, preferred_element_type=F32) + jnp.dot(lo, sel, preferred_element_type=F32)


def _local_step(x, p, tgt, sm, wts):
    w_in_t, w_out, wg_t, wu_t, wd, w_ple, w_pg = wts
    tt = _tile(x.shape[0])
    eq, ek, rowq, rowk, selq, selk = _aug_consts()
    b_pad = jnp.pad(sm["b_forget"], ((0, 0), (0, LANE - NH)))
    qa, ka, v, u, fl = _in_proj(x, sm["g_mix_pre"], w_in_t, b_pad, _tri(tt, False), eq, ek, rowq, rowk)
    a, lse = _attn_fwd(qa, ka, v)
    yb, m, o, h1 = _mix_out(a, u, x, _band(tt, False), sm["w_pool"], sm["pool_scale"], sm["g_attn_grp"],
                            sm["g_pool_grp"], w_out, sm["g_mix_post"])
    hn2, gate, up, dff, dh2, loss, dwpg, dwple, dgple, dgfpost = _ffn_fwd(
        h1, sm["g_ffn_pre"], wg_t, wu_t, wd, sm["g_ffn_post"], p, w_ple, sm["g_ple"], w_pg, tgt)
    dwg_t, dwu_t, dwd, d_hn2 = _ffn_bwd(hn2, gate, up, dff, wg_t, wu_t, wd)
    dh1, d_a, dyc, dgfpre, dgpost, dgattn, dgpool, dps, dwpool, dwout = _mix_bwd(
        d_hn2, dh2, h1, o, a, m, yb, sm["g_ffn_pre"], sm["g_mix_post"], sm["g_attn_grp"], sm["g_pool_grp"],
        w_out, sm["w_pool"], sm["pool_scale"])
    dqa, dka, dvv = _attn_bwd(qa, ka, v, a, d_a, lse)
    dx, dwin_t, dg1, dbf = _in_bwd(dqa, dka, dvv, dyc, fl, x, dh1, sm["g_mix_pre"], w_in_t, _tri(tt, True),
                                   selq, selk, _band(tt, True))
    big = (dwin_t, dwout, dwg_t, dwu_t, dwd, dwple, dwpg)
    small = dict(loss=loss[0:1, 0:1], g_mix_pre=dg1, b_forget=dbf[:, :NH], g_attn_grp=dgattn, g_pool_grp=dgpool,
                 w_pool=dwpool, pool_scale=dps, g_mix_post=dgpost, g_ffn_pre=dgfpre, g_ffn_post=dgfpost, g_ple=dgple)
    return dx, big, small


def _place():
    x, y, c = lax.axis_index("x"), lax.axis_index("y"), lax.axis_index("c")
    return x, y, c, [(1 - x, y), (x, 1 - y), (1 - x, 1 - y)]


def _half(c):
    return pl.ds(pl.multiple_of(c * HALF, 16), HALF)


ANY = pl.BlockSpec(memory_space=pl.ANY)


def _all_gather(stack):
    def body(in_ref, out_ref, send_sems, recv_sems, local_sem):
        x, y, c, chips = _place()
        sib = (x, y, 1 - c)

        def blk(px, py, cc):
            return out_ref.at[2 * px + py, _half(cc), :]

        def copy(k, src, dst, to):
            return pltpu.make_async_remote_copy(src_ref=src, dst_ref=dst, send_sem=send_sems.at[k],
                                                recv_sem=recv_sems.at[k], device_id=to, device_id_type=MESH)

        mine = pltpu.make_async_copy(in_ref, out_ref.at[2 * x + y], local_sem)
        mine.start()
        first = [copy(k, in_ref.at[_half(c), :], blk(x, y, c), (*chip, c)) for k, chip in enumerate(chips)]
        for cp in first:
            cp.start()
        passed = [copy(3 + k, blk(*chip, c), blk(*chip, c), sib) for k, chip in enumerate(chips)]
        for k, chip in enumerate(chips):
            copy(k, in_ref.at[_half(c), :], blk(*chip, c), (*chip, c)).wait_recv()
            passed[k].start()
        for k, chip in enumerate(chips):
            copy(3 + k, blk(*chip, 1 - c), blk(*chip, 1 - c), sib).wait_recv()
        for cp in first + passed:
            cp.wait_send()
        mine.wait()

    return pl.pallas_call(
        body, name="all_gather", in_specs=[ANY], out_specs=ANY,
        out_shape=jax.ShapeDtypeStruct((NSHARD,) + stack.shape, stack.dtype),
        scratch_shapes=[pltpu.SemaphoreType.DMA((6,)), pltpu.SemaphoreType.DMA((6,)), pltpu.SemaphoreType.DMA],
    )(stack)


def _pair_exchange(g):
    def body(g_ref, x_ref, send_sem, recv_sem):
        x, y, c, _ = _place()
        cp = pltpu.make_async_remote_copy(src_ref=g_ref.at[:, _half(1 - c), :], dst_ref=x_ref, send_sem=send_sem,
                                          recv_sem=recv_sem, device_id=(x, y, 1 - c), device_id_type=MESH)
        cp.start()
        cp.wait()

    return pl.pallas_call(
        body, name="pair_exchange", in_specs=[ANY], out_specs=ANY,
        out_shape=jax.ShapeDtypeStruct((NSHARD, HALF, D), F32),
        scratch_shapes=[pltpu.SemaphoreType.DMA, pltpu.SemaphoreType.DMA],
    )(g)


def _pair_sum(cidx, g, recv):
    nb = HALF // RED_ROWS

    def body(c_ref, g_ref, r_ref, out_ref):
        out_ref[...] = (g_ref[...] + r_ref[...]).astype(BF)

    return pl.pallas_call(
        body, name="pair_sum",
        grid_spec=pltpu.PrefetchScalarGridSpec(
            num_scalar_prefetch=1, grid=(NSHARD, nb),
            in_specs=[pl.BlockSpec((1, RED_ROWS, D), lambda s, i, c: (s, c[0] * nb + i, 0)),
                      pl.BlockSpec((1, RED_ROWS, D), lambda s, i, c: (s, i, 0))],
            out_specs=pl.BlockSpec((1, RED_ROWS, D), lambda s, i, c: (s, i, 0))),
        out_shape=jax.ShapeDtypeStruct((NSHARD, HALF, D), BF),
    )(cidx, g, recv)


def _chip_exchange(pb):
    def body(p_ref, y_ref, send_sems, recv_sems, local_sem):
        x, y, c, chips = _place()
        me = 2 * x + y
        mine = pltpu.make_async_copy(p_ref.at[me], y_ref.at[me], local_sem)
        mine.start()

        def copy(k, src_slot, dst_slot, to):
            return pltpu.make_async_remote_copy(src_ref=p_ref.at[src_slot], dst_ref=y_ref.at[dst_slot],
                                                send_sem=send_sems.at[k], recv_sem=recv_sems.at[k],
                                                device_id=to, device_id_type=MESH)

        sends = [copy(k, 2 * cx + cy, me, (cx, cy, c)) for k, (cx, cy) in enumerate(chips)]
        for cp in sends:
            cp.start()
        for k, (cx, cy) in enumerate(chips):
            copy(k, me, 2 * cx + cy, (cx, cy, c)).wait_recv()
        for cp in sends:
            cp.wait_send()
        mine.wait()

    return pl.pallas_call(
        body, name="chip_exchange", in_specs=[ANY], out_specs=ANY,
        out_shape=jax.ShapeDtypeStruct((NSHARD, HALF, D), BF),
        scratch_shapes=[pltpu.SemaphoreType.DMA((3,)), pltpu.SemaphoreType.DMA((3,)), pltpu.SemaphoreType.DMA],
    )(pb)


def _chip_sum(y):
    def body(y_ref, out_ref):
        acc = y_ref[0].astype(F32)
        for k in range(1, NSHARD):
            acc = acc + y_ref[k].astype(F32)
        out_ref[...] = acc

    return pl.pallas_call(
        body, name="chip_sum", grid=(HALF // RED_ROWS,),
        in_specs=[pl.BlockSpec((NSHARD, RED_ROWS, D), lambda i: (0, i, 0))],
        out_specs=pl.BlockSpec((RED_ROWS, D), lambda i: (i, 0)),
        out_shape=jax.ShapeDtypeStruct((HALF, D), F32),
    )(y)


def _pair_gather(f):
    def body(f_ref, out_ref, send_sem, recv_sem, local_sem):
        x, y, c, _ = _place()
        mine = pltpu.make_async_copy(f_ref, out_ref.at[_half(c), :], local_sem)
        mine.start()
        cp = pltpu.make_async_remote_copy(src_ref=f_ref, dst_ref=out_ref.at[_half(c), :], send_sem=send_sem,
                                          recv_sem=recv_sem, device_id=(x, y, 1 - c), device_id_type=MESH)
        cp.start()
        pltpu.make_async_remote_copy(src_ref=f_ref, dst_ref=out_ref.at[_half(1 - c), :], send_sem=send_sem,
                                     recv_sem=recv_sem, device_id=(x, y, 1 - c), device_id_type=MESH).wait_recv()
        cp.wait_send()
        mine.wait()

    return pl.pallas_call(
        body, name="pair_gather", in_specs=[ANY], out_specs=ANY,
        out_shape=jax.ShapeDtypeStruct((STACK_ROWS, D), F32),
        scratch_shapes=[pltpu.SemaphoreType.DMA, pltpu.SemaphoreType.DMA, pltpu.SemaphoreType.DMA],
    )(f)


def _all_reduce_small(v):
    rows = v.shape[0]

    def body(in_ref, out_ref, buf, send_sems, recv_sems):
        x, y, c, _ = _place()
        me = 4 * x + 2 * y + c
        flips = [(r >> 2 & 1, r >> 1 & 1, r & 1) for r in range(1, 8)]

        def peer(f):
            return tuple(1 - a if b else a for a, b in zip((x, y, c), f))

        def copy(k, slot, to):
            return pltpu.make_async_remote_copy(src_ref=in_ref, dst_ref=buf.at[slot], send_sem=send_sems.at[k],
                                                recv_sem=recv_sems.at[k], device_id=to, device_id_type=MESH)

        sends = [copy(k, me, peer(f)) for k, f in enumerate(flips)]
        for cp in sends:
            cp.start()
        buf[me] = in_ref[...]
        for k, f in enumerate(flips):
            px, py, pc = peer(f)
            copy(k, 4 * px + 2 * py + pc, peer(f)).wait_recv()
        for cp in sends:
            cp.wait_send()
        acc = buf[0]
        for k in range(1, 8):
            acc = acc + buf[k]
        out_ref[...] = acc

    vm = pl.BlockSpec(memory_space=pltpu.VMEM)
    return pl.pallas_call(
        body, name="all_reduce_small", in_specs=[vm], out_specs=vm,
        out_shape=jax.ShapeDtypeStruct(v.shape, F32),
        scratch_shapes=[pltpu.VMEM((8, rows, LANE), F32), pltpu.SemaphoreType.DMA((7,)), pltpu.SemaphoreType.DMA((7,))],
    )(v)


def _adamw_math(w, g, m, v):
    m = ADAM_B1 * m + (1.0 - ADAM_B1) * g
    v = ADAM_B2 * v + (1.0 - ADAM_B2) * (g * g)
    m_hat = m / (1.0 - ADAM_B1 ** ADAM_STEP)
    v_hat = v / (1.0 - ADAM_B2 ** ADAM_STEP)
    delta = -ADAM_LR * (m_hat / (jnp.sqrt(v_hat) + ADAM_EPS) + ADAM_WD * w)
    return delta, m, v


def _adamw(w, g, m, v):
    r, c = w.shape
    br = next(b for b in (256, 176, r) if r % b == 0)

    def body(w_ref, g_ref, m_ref, v_ref, d_ref, nm_ref, nv_ref):
        d_ref[...], nm_ref[...], nv_ref[...] = _adamw_math(w_ref[...], g_ref[...], m_ref[...], v_ref[...])

    spec = pl.BlockSpec((br, c), lambda i: (i, 0))
    return pl.pallas_call(
        body, name="adamw", grid=(r // br,), in_specs=[spec] * 4, out_specs=[spec] * 3,
        out_shape=[jax.ShapeDtypeStruct((r, c), F32)] * 3,
    )(w, g, m, v)


def _adamw_small(ws, gs, ms, vs):
    n = len(ws)

    def body(*refs):
        ins, outs = refs[:4 * n], refs[4 * n:]
        for k in range(n):
            d, m, v = _adamw_math(ins[k][...], ins[n + k][...], ins[2 * n + k][...], ins[3 * n + k][...])
            outs[k][...] = d
            outs[n + k][...] = m
            outs[2 * n + k][...] = v

    vm = pl.BlockSpec(memory_space=pltpu.VMEM)
    out = pl.pallas_call(
        body, name="adamw_small", in_specs=[vm] * (4 * n), out_specs=[vm] * (3 * n),
        out_shape=[jax.ShapeDtypeStruct(w.shape, F32) for w in ws] * 3,
    )(*ws, *gs, *ms, *vs)
    return out[:n], out[n:2 * n], out[2 * n:]


BIG = ("w_in", "w_out", "w_ffn_gate", "w_ffn_up", "w_ffn_down", "w_ple_proj", "w_ple_gate")
SMALL = ("g_mix_pre", "b_forget", "g_attn_grp", "g_pool_grp", "w_pool", "pool_scale", "g_mix_post", "g_ffn_pre",
         "g_ffn_post", "g_ple")
VECTORS = tuple(n for n in SMALL if n != "w_pool")
ORDER = ("g_mix_pre", "w_in", "b_forget", "g_attn_grp", "g_pool_grp", "w_pool", "pool_scale", "w_out", "g_mix_post",
         "g_ffn_pre", "w_ffn_gate", "w_ffn_up", "w_ffn_down", "g_ffn_post", "w_ple_proj", "g_ple", "w_ple_gate")


def _pad_rows(a, rows):
    return jnp.pad(a, ((0, rows - a.shape[0]), (0, 0)))


def _stack_shard(w_in, w_out, wg, wu, wd, wple, wpg):
    parts = [_pad_rows(w_in.T, IN_PAD), w_out, wg.T, wu.T, wd, wple.reshape(DPLE // NSHARD, D), wpg]
    return _pad_rows(jnp.concatenate(parts, axis=0), STACK_ROWS)


def _unstack_shard(s):
    return (s[O_IN:O_IN + IN_SH].T, s[O_OUT:O_G], s[O_G:O_U].T, s[O_U:O_D].T, s[O_D:O_PLE],
            s[O_PLE:O_PG].reshape(DPLE, DPLE), s[O_PG:STACK_USED])


def _unstack_full(g):
    w_in_t = g[:, O_IN:O_IN + IN_SH].reshape(NSHARD * IN_SH, D)
    w_in_t = jnp.concatenate([w_in_t[:3 * DA], _pad_rows(w_in_t[3 * DA:3 * DA + NH], LANE), w_in_t[3 * DA + NH:]], axis=0)
    cat = lambda lo, hi: g[:, lo:hi].reshape(NSHARD * (hi - lo), D)
    w_ple = g[:, O_PLE:O_PG].reshape(NSHARD, DPLE, DPLE).transpose(1, 0, 2).reshape(DPLE, D)
    return (w_in_t, cat(O_OUT, O_G), cat(O_G, O_U), cat(O_U, O_D), cat(O_D, O_PLE), w_ple, cat(O_PG, STACK_USED))


def _stack_full(big):
    dwin_t, dwout, dwg_t, dwu_t, dwd, dwple, dwpg = big
    dwin_t = jnp.concatenate([dwin_t[:3 * DA + NH], dwin_t[3 * DA + LANE:]], axis=0).reshape(NSHARD, IN_SH, D)
    dwin_t = jnp.pad(dwin_t, ((0, 0), (0, IN_PAD - IN_SH), (0, 0)))
    sh = lambda a: a.reshape(NSHARD, a.shape[0] // NSHARD, D)
    dwple = dwple.reshape(DPLE, NSHARD, DPLE).transpose(1, 0, 2).reshape(NSHARD, DPLE // NSHARD, D)
    parts = [dwin_t, sh(dwout), sh(dwg_t), sh(dwu_t), sh(dwd), dwple, sh(dwpg)]
    s = jnp.concatenate(parts, axis=1)
    return jnp.pad(s, ((0, 0), (0, STACK_ROWS - STACK_USED), (0, 0)))


def _pack_small(small):
    parts = []
    for name in ("loss",) + SMALL:
        flat = small[name].reshape(-1)
        parts.append(jnp.pad(flat, (0, -flat.shape[0] % LANE)).reshape(-1, LANE))
    v = jnp.concatenate(parts, axis=0)
    return _pad_rows(v, v.shape[0] + (-v.shape[0] % 8))


def _unpack_small(v, shapes):
    out, r = {}, 0
    for name in ("loss",) + SMALL:
        n = math.prod(shapes[name])
        rows = -(-n // LANE)
        out[name] = v[r:r + rows].reshape(-1)[:n].reshape(shapes[name])
        r += rows
    return out


def kernel(x, p, g_mix_pre, w_in, b_forget, g_attn_grp, g_pool_grp, w_pool, pool_scale, w_out, g_mix_post, g_ffn_pre, w_ffn_gate, w_ffn_up, w_ffn_down, g_ffn_post, w_ple_proj, g_ple, w_ple_gate, loss_target, m_g_mix_pre, m_w_in, m_b_forget, m_g_attn_grp, m_g_pool_grp, m_w_pool, m_pool_scale, m_w_out, m_g_mix_post, m_g_ffn_pre, m_w_ffn_gate, m_w_ffn_up, m_w_ffn_down, m_g_ffn_post, m_w_ple_proj, m_g_ple, m_w_ple_gate, v_g_mix_pre, v_w_in, v_b_forget, v_g_attn_grp, v_g_pool_grp, v_w_pool, v_pool_scale, v_w_out, v_g_mix_post, v_g_ffn_pre, v_w_ffn_gate, v_w_ffn_up, v_w_ffn_down, v_g_ffn_post, v_w_ple_proj, v_g_ple, v_w_ple_gate):
    args = dict(locals())
    strip = lambda n, a: a if n in VECTORS else a[0]
    w = {n: strip(n, args[n]) for n in ORDER}
    mom = {n: strip(n, args["m_" + n]) for n in ORDER}
    var = {n: strip(n, args["v_" + n]) for n in ORDER}
    sm = {n: w[n] for n in SMALL}

    stack = _stack_shard(*[w[n].astype(BF) for n in BIG])
    wts = _unstack_full(_all_gather(stack))
    dx, big, small = _local_step(x[0], p[0, 0], loss_target[0], sm, wts)

    shapes = {n: small[n].shape for n in small}
    red_small = _unpack_small(_all_reduce_small(_pack_small(small)), shapes)
    loss = 0.5 / D * red_small["loss"][0, 0]

    g_stack = _stack_full(big)
    cidx = lax.axis_index("c").astype(jnp.int32).reshape(1)
    partial = _pair_sum(cidx, g_stack, _pair_exchange(g_stack))
    reduced = _pair_gather(_chip_sum(_chip_exchange(partial)))
    grads = dict(zip(BIG, _unstack_shard(reduced)))
    for n in SMALL:
        grads[n] = red_small[n].reshape(w[n].shape)

    delta, new_m, new_v = {}, {}, {}
    for n in BIG:
        delta[n], new_m[n], new_v[n] = _adamw(w[n], grads[n], mom[n], var[n])
    two_d = lambda a: a.reshape(-1, a.shape[-1])
    ds, ms, vs = _adamw_small([two_d(w[n]) for n in SMALL], [two_d(grads[n]) for n in SMALL],
                              [two_d(mom[n]) for n in SMALL], [two_d(var[n]) for n in SMALL])
    for k, n in enumerate(SMALL):
        delta[n], new_m[n], new_v[n] = ds[k].reshape(w[n].shape), ms[k].reshape(w[n].shape), vs[k].reshape(w[n].shape)

    lead = lambda d: [d[n] if n in VECTORS else d[n][None] for n in ORDER]
    return (loss, dx[None], *lead(grads), *lead(delta), *lead(new_m), *lead(new_v))
```

```python
import functools
import math

import jax
import jax.numpy as jnp
from jax import lax
from jax.experimental import pallas as pl
from jax.experimental.pallas import tpu as pltpu

F32 = jnp.float32
BF = jnp.bfloat16
MESH = pl.DeviceIdType.MESH

D = 1024
DA = 512
DP = 512
NH = 8
HD = 64
DFF = 2816
DPLE = 256
WINS = (2, 4, 8, 16)
PC = 128
ZW = 3 * DA + 128 + DP
EPS = 1e-6
NSHARD = 4

LANE = 128
HALO = 128

IN_SH = 514
IN_PAD = 528
FF_SH = DFF // NSHARD
O1_OUT, USED1, ROWS1 = 528, 784, 800
O2_U, O2_D, O2_PLE, O2_PG, USED2, ROWS2 = 704, 1408, 2112, 2176, 2432, 2560
RED1, RED2 = 400, 128

ADAM_LR, ADAM_B1, ADAM_B2, ADAM_EPS, ADAM_WD, ADAM_STEP = 0.001, 0.9, 0.999, 1e-8, 0.01, 10

VMEM_LIMIT = 56 * 1024 * 1024


def _cp(**kw):
    return pltpu.CompilerParams(vmem_limit_bytes=VMEM_LIMIT, **kw)


def _mm(a, b):
    return jnp.dot(a.astype(BF), b.astype(BF), preferred_element_type=F32)


def _mm_nt(a, b):
    return lax.dot_general(a.astype(BF), b.astype(BF), (((1,), (1,)), ((), ())), preferred_element_type=F32)


def _mm_tn(a, b):
    return lax.dot_general(a.astype(BF), b.astype(BF), (((0,), (0,)), ((), ())), preferred_element_type=F32)


def _split2(x):
    hi = x.astype(BF)
    lo = (x - hi.astype(F32)).astype(BF)
    return hi, lo


def _split3(x):
    hi = x.astype(BF)
    r = x - hi.astype(F32)
    mid = r.astype(BF)
    lo = (r - mid.astype(F32)).astype(BF)
    return hi, mid, lo


def _dot3(m, x):
    hi, mid, lo = _split3(x)
    return (jnp.dot(m, hi, preferred_element_type=F32) + jnp.dot(m, mid, preferred_element_type=F32)
            + jnp.dot(m, lo, preferred_element_type=F32))


def _dot2(m, x):
    hi, lo = _split2(x)
    return jnp.dot(m, hi, preferred_element_type=F32) + jnp.dot(m, lo, preferred_element_type=F32)


def _rstd(x):
    return lax.rsqrt(jnp.mean(x * x, axis=-1, keepdims=True) + EPS)


def _rms_bwd(dy, x, g):
    r = _rstd(x)
    xh = x * r
    dg = jnp.sum(dy * xh, axis=0, keepdims=True)
    dxh = dy * g
    dx = r * (dxh - xh * jnp.mean(dxh * xh, axis=-1, keepdims=True))
    return dx, dg


def _sigmoid(x):
    return 1.0 / (1.0 + jnp.exp(-x))


ANY = pl.BlockSpec(memory_space=pl.ANY)


def _full(shape):
    n = len(shape)
    return pl.BlockSpec(shape, lambda *_: (0,) * n)


def _resident(shape):
    n = len(shape)
    return pl.BlockSpec(shape, lambda *_: (0,) * n, pipeline_mode=pl.Buffered(1))


def _tile(t):
    return 512 if t % 512 == 0 else t


def _tri(n, upper):
    r = lax.broadcasted_iota(jnp.int32, (n, n), 0)
    c = lax.broadcasted_iota(jnp.int32, (n, n), 1)
    return ((c >= r) if upper else (c <= r)).astype(BF)


def _band(tt, transpose):
    r = lax.broadcasted_iota(jnp.int32, (tt, tt + HALO), 0)
    c = lax.broadcasted_iota(jnp.int32, (tt, tt + HALO), 1)
    d = (c - r) if transpose else (r + HALO - c)
    return jnp.stack([((d >= 0) & (d < w)).astype(BF) for w in WINS])


def _aug_consts():
    row = lax.broadcasted_iota(jnp.int32, (3 * LANE, NH * LANE), 0)
    col = lax.broadcasted_iota(jnp.int32, (3 * LANE, NH * LANE), 1)
    piece, head = row // LANE, row % LANE
    ch, cl = col // LANE, col % LANE
    eq = ((head == ch) & (cl == HD + piece)).astype(BF)
    ek = -((head == ch) & (cl == HD + 3 + piece)).astype(BF)
    lane = lax.broadcasted_iota(jnp.int32, (1, NH * LANE), 1) % LANE
    rowq = ((lane >= HD + 3) & (lane < HD + 6)).astype(F32)
    rowk = ((lane >= HD) & (lane < HD + 3)).astype(F32)
    r2 = lax.broadcasted_iota(jnp.int32, (NH * LANE, LANE), 0)
    c2 = lax.broadcasted_iota(jnp.int32, (NH * LANE, LANE), 1)
    selq = ((r2 // LANE == c2) & (r2 % LANE == HD)).astype(BF)
    selk = ((r2 // LANE == c2) & (r2 % LANE == HD + 3)).astype(BF)
    return eq, ek, rowq, rowk, selq, selk


def _in_proj(x, g1, w_in_t, b_pad, tri, eq, ek, rowq, rowk, dep):
    t = x.shape[0]
    tt = _tile(t)

    def body(x_ref, g_ref, w_ref, b_ref, tri_ref, eq_ref, ek_ref, rq_ref, rk_ref, dep_ref,
             qa_ref, ka_ref, v_ref, u_ref, fl_ref, carry):
        i = pl.program_id(0)

        @pl.when(i == 0)
        def _():
            carry[...] = jnp.zeros_like(carry)

        xv = x_ref[...]
        hn = (xv * _rstd(xv) * g_ref[...]).astype(BF)
        z = _mm_nt(hn, w_ref[...])
        fl = z[:, 3 * DA:3 * DA + LANE] + b_ref[...]
        lane = lax.broadcasted_iota(jnp.int32, fl.shape, 1)
        lf = jnp.where(lane < NH, jnp.minimum(fl, 0.0) - jnp.log(1.0 + jnp.exp(-jnp.abs(fl))), 0.0)
        c = carry[...] + _dot3(tri_ref[...], lf)
        carry[...] = carry[...] + jnp.sum(lf, axis=0, keepdims=True)
        caug = jnp.concatenate(_split3(c), axis=1)
        aug_q = jnp.dot(caug, eq_ref[...], preferred_element_type=F32) + rq_ref[...]
        aug_k = jnp.dot(caug, ek_ref[...], preferred_element_type=F32) + rk_ref[...]
        low = lax.broadcasted_iota(jnp.int32, (tt, LANE), 1) < HD
        for p in range(NH // 2):
            qp = z[:, LANE * p:LANE * (p + 1)] * (1.0 / math.sqrt(HD))
            kp = z[:, DA + LANE * p:DA + LANE * (p + 1)]
            for h, (qh, kh) in enumerate(((qp, kp), (pltpu.roll(qp, HD, 1), pltpu.roll(kp, HD, 1)))):
                lo_, hi_ = LANE * (2 * p + h), LANE * (2 * p + h + 1)
                qa_ref[:, lo_:hi_] = jnp.where(low, qh, aug_q[:, lo_:hi_]).astype(BF)
                ka_ref[:, lo_:hi_] = jnp.where(low, kh, aug_k[:, lo_:hi_]).astype(BF)
        v_ref[...] = z[:, 2 * DA:3 * DA].astype(BF)
        u_ref[...] = z[:, 3 * DA + LANE:]
        fl_ref[...] = fl

    return pl.pallas_call(
        body, name="in_proj", grid=(t // tt,),
        in_specs=[pl.BlockSpec((tt, D), lambda i: (i, 0)), _full((1, D)), _resident((ZW, D)), _full((1, LANE)),
                  _full((tt, tt)), _full((3 * LANE, NH * LANE)), _full((3 * LANE, NH * LANE)),
                  _full((1, NH * LANE)), _full((1, NH * LANE)), ANY],
        out_specs=[pl.BlockSpec((tt, NH * LANE), lambda i: (i, 0)), pl.BlockSpec((tt, NH * LANE), lambda i: (i, 0)),
                   pl.BlockSpec((tt, DA), lambda i: (i, 0)), pl.BlockSpec((tt, DP), lambda i: (i, 0)),
                   pl.BlockSpec((tt, LANE), lambda i: (i, 0))],
        out_shape=[jax.ShapeDtypeStruct((t, NH * LANE), BF), jax.ShapeDtypeStruct((t, NH * LANE), BF),
                   jax.ShapeDtypeStruct((t, DA), BF), jax.ShapeDtypeStruct((t, DP), F32),
                   jax.ShapeDtypeStruct((t, LANE), F32)],
        scratch_shapes=[pltpu.VMEM((1, LANE), F32)],
        compiler_params=_cp(),
    )(x, g1, w_in_t, b_pad, tri, eq, ek, rowq, rowk, dep)


def _attn_fwd(qa, ka, v):
    t = qa.shape[0]
    ta = _tile(t)
    n = t // ta

    def body(q_ref, k_ref, v_ref, a_ref, lse_ref, m_ref, l_ref, acc_ref):
        i, j = pl.program_id(1), pl.program_id(2)

        @pl.when(j == 0)
        def _():
            m_ref[...] = jnp.full_like(m_ref, -1e30)
            l_ref[...] = jnp.zeros_like(l_ref)
            acc_ref[...] = jnp.zeros_like(acc_ref)

        @pl.when(j <= i)
        def _():
            row = lax.broadcasted_iota(jnp.int32, (ta, ta), 0)
            col = lax.broadcasted_iota(jnp.int32, (ta, ta), 1)
            keep = col <= row + jnp.where(j < i, ta, 0)
            v2 = v_ref[...]
            for h in range(2):
                s = _mm_nt(q_ref[:, LANE * h:LANE * (h + 1)], k_ref[:, LANE * h:LANE * (h + 1)])
                s = jnp.where(keep, s, -1e30)
                m_old = m_ref[h]
                m_new = jnp.maximum(m_old, jnp.max(s, axis=1, keepdims=True))
                pe = jnp.exp(s - m_new)
                alpha = jnp.exp(m_old - m_new)
                l_ref[h] = alpha * l_ref[h] + jnp.sum(pe, axis=1, keepdims=True)
                acc_ref[h] = alpha * acc_ref[h] + jnp.dot(pe.astype(BF), v2, preferred_element_type=F32)
                m_ref[h] = m_new

        @pl.when(j == i)
        def _():
            low = lax.broadcasted_iota(jnp.int32, (ta, LANE), 1) < HD
            a_ref[...] = jnp.where(low, acc_ref[0] / l_ref[0], acc_ref[1] / l_ref[1])
            lse_ref[...] = jnp.where(low, m_ref[0] + jnp.log(l_ref[0]), m_ref[1] + jnp.log(l_ref[1]))

    return pl.pallas_call(
        body, name="attn_fwd", grid=(NH // 2, n, n),
        in_specs=[pl.BlockSpec((ta, 2 * LANE), lambda p, i, j: (i, p)),
                  pl.BlockSpec((ta, 2 * LANE), lambda p, i, j: (jnp.minimum(i, j), p)),
                  pl.BlockSpec((ta, LANE), lambda p, i, j: (jnp.minimum(i, j), p))],
        out_specs=[pl.BlockSpec((ta, LANE), lambda p, i, j: (i, p)), pl.BlockSpec((ta, LANE), lambda p, i, j: (i, p))],
        out_shape=[jax.ShapeDtypeStruct((t, DA), F32), jax.ShapeDtypeStruct((t, DA), F32)],
        scratch_shapes=[pltpu.VMEM((2, ta, 1), F32), pltpu.VMEM((2, ta, 1), F32), pltpu.VMEM((2, ta, LANE), F32)],
        compiler_params=_cp(),
    )(qa, ka, v)


def _mix_out(a, u, x, band, w_pool, pool_scale, g_attn, g_pool, w_out, g_post):
    t = a.shape[0]
    tt = _tile(t)
    hb = tt // HALO

    def body(a_ref, u_ref, up_ref, x_ref, band_ref, wp_ref, ps_ref, ga_ref, gp_ref, wo_ref, go_ref,
             yb_ref, m_ref, o_ref, h1_ref):
        i = pl.program_id(0)
        prev = up_ref[...] * jnp.where(i > 0, 1.0, 0.0)
        tok = i * tt + lax.broadcasted_iota(jnp.int32, (tt, PC), 0)
        ms = []
        for g, w in enumerate(WINS):
            ug = u_ref[:, PC * g:PC * (g + 1)]
            ext = jnp.concatenate([prev[:, PC * g:PC * (g + 1)], ug], axis=0)
            cnt = jnp.minimum(tok + 1, w).astype(F32)
            y = (_dot2(band_ref[g], ext) / cnt - ug).astype(BF)
            yb_ref[:, PC * g:PC * (g + 1)] = y
            ms.append(_mm(y, wp_ref[g]) * ps_ref[:, PC * g:PC * (g + 1)])
        m = jnp.concatenate(ms, axis=1)
        m_ref[...] = m
        av = a_ref[...]
        mix = jnp.concatenate([av * _rstd(av) * ga_ref[...], m * _rstd(m) * gp_ref[...]], axis=1)
        o = _mm(mix, wo_ref[...])
        o_ref[...] = o
        h1_ref[...] = x_ref[...] + o * _rstd(o) * go_ref[...]

    return pl.pallas_call(
        body, name="mix_out", grid=(t // tt,),
        in_specs=[pl.BlockSpec((tt, DA), lambda i: (i, 0)), pl.BlockSpec((tt, DP), lambda i: (i, 0)),
                  pl.BlockSpec((HALO, DP), lambda i: (jnp.maximum(i * hb - 1, 0), 0)),
                  pl.BlockSpec((tt, D), lambda i: (i, 0)), _full((len(WINS), tt, tt + HALO)),
                  _full((len(WINS), PC, PC)), _full((1, DP)), _full((1, DA)), _full((1, DP)),
                  _resident((D, D)), _full((1, D))],
        out_specs=[pl.BlockSpec((tt, DP), lambda i: (i, 0)), pl.BlockSpec((tt, DP), lambda i: (i, 0)),
                   pl.BlockSpec((tt, D), lambda i: (i, 0)), pl.BlockSpec((tt, D), lambda i: (i, 0))],
        out_shape=[jax.ShapeDtypeStruct((t, DP), BF), jax.ShapeDtypeStruct((t, DP), F32),
                   jax.ShapeDtypeStruct((t, D), F32), jax.ShapeDtypeStruct((t, D), F32)],
        compiler_params=_cp(),
    )(a, u, u, x, band, w_pool, pool_scale, g_attn, g_pool, w_out, g_post)


def _ffn_fwd(h1, g_pre, wg_t, wu_t, wd, g_post, p, w_ple, g_ple, w_pg, tgt):
    t = h1.shape[0]
    tt = 256 if t % 256 == 0 else t

    def body(h1_ref, gpre_ref, wg_ref, wu_ref, wd_ref, gpost_ref, p_ref, wple_ref, gple_ref, wpg_ref, tgt_ref,
             hn_ref, gate_ref, up_ref, dff_ref, dh2_ref, loss_ref, dwpg_ref, dwple_ref, dgple_ref, dgpost_ref):
        i = pl.program_id(0)

        @pl.when(i == 0)
        def _():
            loss_ref[...] = jnp.zeros_like(loss_ref)
            dwpg_ref[...] = jnp.zeros_like(dwpg_ref)
            dwple_ref[...] = jnp.zeros_like(dwple_ref)
            dgple_ref[...] = jnp.zeros_like(dgple_ref)
            dgpost_ref[...] = jnp.zeros_like(dgpost_ref)

        h1v = h1_ref[...]
        hn = (h1v * _rstd(h1v) * gpre_ref[...]).astype(BF)
        hn_ref[...] = hn
        gate = _mm_nt(hn, wg_ref[...])
        up = _mm_nt(hn, wu_ref[...])
        gate_ref[...] = gate.astype(BF)
        up_ref[...] = up.astype(BF)
        ff = _mm(gate * _sigmoid(gate) * up, wd_ref[...])
        rff = _rstd(ff)
        ffh = ff * rff
        gpost = gpost_ref[...]
        h2 = h1v + ffh * gpost
        pv = p_ref[...]
        pe = _mm(pv, wple_ref[...])
        rpe = _rstd(pe)
        peh = pe * rpe
        gple = gple_ref[...]
        e = peh * gple
        sig = _sigmoid(_mm(h2, wpg_ref[...]))
        dv = h2 + sig * e - tgt_ref[...]
        sq = jnp.sum(jnp.sum(dv * dv, axis=1, keepdims=True), axis=0, keepdims=True)
        loss_ref[...] = loss_ref[...] + sq
        dy = dv * (1.0 / D)
        d_e = dy * sig
        d_gl = dy * e * sig * (1.0 - sig)
        dh2 = dy + _mm_nt(d_gl, wpg_ref[...])
        dh2_ref[...] = dh2
        dwpg_ref[...] = dwpg_ref[...] + _mm_tn(h2, d_gl)
        dgple_ref[...] = dgple_ref[...] + jnp.sum(d_e * peh, axis=0, keepdims=True)
        dpeh = d_e * gple
        d_pe = rpe * (dpeh - peh * jnp.mean(dpeh * peh, axis=-1, keepdims=True))
        dwple_ref[...] = dwple_ref[...] + _mm_tn(pv, d_pe)
        dgpost_ref[...] = dgpost_ref[...] + jnp.sum(dh2 * ffh, axis=0, keepdims=True)
        dffh = dh2 * gpost
        dff_ref[...] = (rff * (dffh - ffh * jnp.mean(dffh * ffh, axis=-1, keepdims=True))).astype(BF)

    row = lambda w: pl.BlockSpec((tt, w), lambda i: (i, 0))
    return pl.pallas_call(
        body, name="ffn_fwd", grid=(t // tt,),
        in_specs=[row(D), _full((1, D)), _resident((DFF, D)), _resident((DFF, D)), _resident((DFF, D)), _full((1, D)),
                  row(DPLE), _resident((DPLE, D)), _full((1, D)), _resident((D, D)), row(D)],
        out_specs=[row(D), row(DFF), row(DFF), row(D), row(D), _full((8, LANE)), _full((D, D)), _full((DPLE, D)),
                   _full((1, D)), _full((1, D))],
        out_shape=[jax.ShapeDtypeStruct((t, D), BF), jax.ShapeDtypeStruct((t, DFF), BF), jax.ShapeDtypeStruct((t, DFF), BF),
                   jax.ShapeDtypeStruct((t, D), BF), jax.ShapeDtypeStruct((t, D), F32), jax.ShapeDtypeStruct((8, LANE), F32),
                   jax.ShapeDtypeStruct((D, D), F32), jax.ShapeDtypeStruct((DPLE, D), F32),
                   jax.ShapeDtypeStruct((1, D), F32), jax.ShapeDtypeStruct((1, D), F32)],
        compiler_params=_cp(),
    )(h1, g_pre, wg_t, wu_t, wd, g_post, p, w_ple, g_ple, w_pg, tgt)


def _ffn_bwd(hn2, gate, up, dff, wg_t, wu_t, wd):
    t = hn2.shape[0]
    tt = _tile(t)
    nt = t // tt
    ch = 256
    nc = DFF // ch

    def body(hn_ref, gate_ref, up_ref, dff_ref, wg_ref, wu_ref, wd_ref,
             dwg_ref, dwu_ref, dwd_ref, dhn_ref, acc, sem):
        j, i = pl.program_id(0), pl.program_id(1)
        gate_v = gate_ref[...].astype(F32)
        up_v = up_ref[...].astype(F32)
        dffv = dff_ref[...]
        hn = hn_ref[...]
        sg = _sigmoid(gate_v)
        silu = gate_v * sg
        d_act = _mm_nt(dffv, wd_ref[...])
        d_up = (d_act * silu).astype(BF)
        d_gate = (d_act * up_v * (sg * (1.0 + gate_v * (1.0 - sg)))).astype(BF)

        @pl.when(i == 0)
        def _():
            dwg_ref[...] = jnp.zeros_like(dwg_ref)
            dwu_ref[...] = jnp.zeros_like(dwu_ref)
            dwd_ref[...] = jnp.zeros_like(dwd_ref)

        dwd_ref[...] = dwd_ref[...] + _mm_tn(silu * up_v, dffv)
        dwg_ref[...] = dwg_ref[...] + _mm_tn(d_gate, hn)
        dwu_ref[...] = dwu_ref[...] + _mm_tn(d_up, hn)
        contrib = _mm(d_gate, wg_ref[...]) + _mm(d_up, wu_ref[...])
        rows = pl.ds(pl.multiple_of(i * tt, tt), tt)

        @pl.when(j == 0)
        def _():
            acc[rows, :] = contrib

        @pl.when(j > 0)
        def _():
            acc[rows, :] = acc[rows, :] + contrib

        @pl.when((j == nc - 1) & (i == nt - 1))
        def _():
            cp = pltpu.make_async_copy(acc, dhn_ref, sem)
            cp.start()
            cp.wait()

    tok = lambda w: pl.BlockSpec((tt, w), lambda j, i: (i, 0))
    chunk = pl.BlockSpec((ch, D), lambda j, i: (j, 0))
    return pl.pallas_call(
        body, name="ffn_bwd", grid=(nc, nt),
        in_specs=[tok(D), pl.BlockSpec((tt, ch), lambda j, i: (i, j)), pl.BlockSpec((tt, ch), lambda j, i: (i, j)),
                  tok(D), chunk, chunk, chunk],
        out_specs=[chunk, chunk, chunk, pl.BlockSpec(memory_space=pl.ANY)],
        out_shape=[jax.ShapeDtypeStruct((DFF, D), F32), jax.ShapeDtypeStruct((DFF, D), F32),
                   jax.ShapeDtypeStruct((DFF, D), F32), jax.ShapeDtypeStruct((t, D), F32)],
        scratch_shapes=[pltpu.VMEM((t, D), F32), pltpu.SemaphoreType.DMA],
        compiler_params=_cp(),
    )(hn2, gate, up, dff, wg_t, wu_t, wd)


def _mix_bwd(d_hn2, dh2, h1, o, a, m, yb, g_ffn_pre, g_post, g_attn, g_pool, w_out, w_pool, pool_scale, dep):
    t = a.shape[0]
    tt = 256 if t % 256 == 0 else t

    def body(dhn_ref, dh2_ref, h1_ref, o_ref, a_ref, m_ref, yb_ref, gfp_ref, go_ref, ga_ref, gp_ref, wo_ref, wp_ref,
             ps_ref, dep_ref, dh1_ref, da_ref, dyc_ref, dgfp_ref, dgo_ref, dga_ref, dgp_ref, dps_ref, dwp_ref, dwo_ref):
        i = pl.program_id(0)

        @pl.when(i == 0)
        def _():
            for r in (dgfp_ref, dgo_ref, dga_ref, dgp_ref, dps_ref, dwp_ref, dwo_ref):
                r[...] = jnp.zeros_like(r)

        d1, dg = _rms_bwd(dhn_ref[...], h1_ref[...], gfp_ref[...])
        dgfp_ref[...] = dgfp_ref[...] + dg
        dh1 = dh2_ref[...] + d1
        dh1_ref[...] = dh1
        d_o, dg = _rms_bwd(dh1, o_ref[...], go_ref[...])
        dgo_ref[...] = dgo_ref[...] + dg
        d_mix = _mm_nt(d_o, wo_ref[...])
        av, mv = a_ref[...], m_ref[...]
        mix = jnp.concatenate([av * _rstd(av) * ga_ref[...], mv * _rstd(mv) * gp_ref[...]], axis=1)
        dwo_ref[...] = dwo_ref[...] + _mm_tn(mix, d_o)
        d_a, dg = _rms_bwd(d_mix[:, :DA], av, ga_ref[...])
        dga_ref[...] = dga_ref[...] + dg
        da_ref[...] = d_a
        d_m, dg = _rms_bwd(d_mix[:, DA:], mv, gp_ref[...])
        dgp_ref[...] = dgp_ref[...] + dg
        tok = i * tt + lax.broadcasted_iota(jnp.int32, (tt, PC), 0)
        dps = []
        for g, w in enumerate(WINS):
            sl = slice(PC * g, PC * (g + 1))
            ybg = yb_ref[:, sl]
            wpg = wp_ref[g].astype(BF)
            mlin = jnp.dot(ybg, wpg, preferred_element_type=F32)
            dmg = d_m[:, sl]
            dps.append(jnp.sum(dmg * mlin, axis=0, keepdims=True))
            dml = (dmg * ps_ref[:, sl]).astype(BF)
            dwp_ref[g] = dwp_ref[g] + _mm_tn(ybg, dml)
            dyc_ref[:, sl] = _mm_nt(dml, wpg) / jnp.minimum(tok + 1, w).astype(F32)
        dps_ref[...] = dps_ref[...] + jnp.concatenate(dps, axis=1)

    row = lambda w: pl.BlockSpec((tt, w), lambda i: (i, 0))
    return pl.pallas_call(
        body, name="mix_bwd", grid=(t // tt,),
        in_specs=[row(D), row(D), row(D), row(D), row(DA), row(DP), row(DP), _full((1, D)), _full((1, D)),
                  _full((1, DA)), _full((1, DP)), _resident((D, D)), _full((len(WINS), PC, PC)), _full((1, DP)), ANY],
        out_specs=[row(D), row(DA), row(DP), _full((1, D)), _full((1, D)), _full((1, DA)), _full((1, DP)),
                   _full((1, DP)), _full((len(WINS), PC, PC)), _full((D, D))],
        out_shape=[jax.ShapeDtypeStruct((t, D), F32), jax.ShapeDtypeStruct((t, DA), F32), jax.ShapeDtypeStruct((t, DP), F32),
                   jax.ShapeDtypeStruct((1, D), F32), jax.ShapeDtypeStruct((1, D), F32), jax.ShapeDtypeStruct((1, DA), F32),
                   jax.ShapeDtypeStruct((1, DP), F32), jax.ShapeDtypeStruct((1, DP), F32),
                   jax.ShapeDtypeStruct((len(WINS), PC, PC), F32), jax.ShapeDtypeStruct((D, D), F32)],
        compiler_params=_cp(),
    )(d_hn2, dh2, h1, o, a, m, yb, g_ffn_pre, g_post, g_attn, g_pool, w_out, w_pool, pool_scale, dep)


def _attn_bwd(qa, ka, v, a, d_a, lse, dep):
    t = qa.shape[0]
    ta = _tile(t)
    n = t // ta

    def body(q_ref, k_ref, v_ref, o_ref, do_ref, lse_ref, dep_ref, dq_ref, dk_ref, dv_ref):
        j, i = pl.program_id(1), pl.program_id(2)

        @pl.when((j == 0) & (i == 0))
        def _():
            dq_ref[...] = jnp.zeros_like(dq_ref)

        @pl.when(i == 0)
        def _():
            dk_ref[...] = jnp.zeros_like(dk_ref)
            dv_ref[...] = jnp.zeros_like(dv_ref)

        @pl.when(i >= j)
        def _():
            lane = lax.broadcasted_iota(jnp.int32, (ta, LANE), 1)
            do2 = do_ref[...]
            prod = do2 * o_ref[...]
            lse2 = lse_ref[...]
            v2 = v_ref[...]
            do2b = do2.astype(BF)
            row = lax.broadcasted_iota(jnp.int32, (ta, ta), 0)
            col = lax.broadcasted_iota(jnp.int32, (ta, ta), 1)
            keep = col <= row + jnp.where(i > j, ta, 0)
            rows = pl.ds(pl.multiple_of(i * ta, ta), ta)
            for h in range(2):
                mine = (lane < HD) if h == 0 else (lane >= HD)
                delta = jnp.sum(jnp.where(mine, prod, 0.0), axis=1, keepdims=True)
                lse_h = jnp.sum(jnp.where(lane == HD * h, lse2, 0.0), axis=1, keepdims=True)
                qh = q_ref[:, LANE * h:LANE * (h + 1)]
                kh = k_ref[:, LANE * h:LANE * (h + 1)]
                s = _mm_nt(qh, kh)
                pr = jnp.where(keep, jnp.exp(s - lse_h), 0.0)
                dp = _mm_nt(jnp.where(mine, do2, 0.0), v2)
                ds = (pr * (dp - delta)).astype(BF)
                dv_ref[...] = dv_ref[...] + jnp.where(mine, _mm_tn(pr, do2b), 0.0)
                dk_ref[:, LANE * h:LANE * (h + 1)] = dk_ref[:, LANE * h:LANE * (h + 1)] + _mm_tn(ds, qh)
                dq_ref[0, rows, LANE * h:LANE * (h + 1)] = (dq_ref[0, rows, LANE * h:LANE * (h + 1)]
                                                           + jnp.dot(ds, kh, preferred_element_type=F32))

    qrow = lambda w: pl.BlockSpec((ta, w), lambda p, j, i: (jnp.maximum(i, j), p))
    krow = lambda w: pl.BlockSpec((ta, w), lambda p, j, i: (j, p))
    return pl.pallas_call(
        body, name="attn_bwd", grid=(NH // 2, n, n),
        in_specs=[qrow(2 * LANE), krow(2 * LANE), krow(LANE), qrow(LANE), qrow(LANE), qrow(LANE), ANY],
        out_specs=[pl.BlockSpec((1, t, 2 * LANE), lambda p, j, i: (p, 0, 0)), krow(2 * LANE), krow(LANE)],
        out_shape=[jax.ShapeDtypeStruct((NH // 2, t, 2 * LANE), F32), jax.ShapeDtypeStruct((t, NH * LANE), F32),
                   jax.ShapeDtypeStruct((t, DA), F32)],
        compiler_params=_cp(),
    )(qa, ka, v, a, d_a, lse, dep)


def _in_bwd(dqa, dka, dv, dyc, fl, x, dh1, g1, w_in_t, tri_u, selq, selk, band_t, dep):
    t = x.shape[0]
    tt = _tile(t)
    nt = t // tt
    hb = tt // HALO
    rev = lambda s: nt - 1 - s

    def body(dqa_ref, dka_ref, dv_ref, dyc_ref, dyn_ref, fl_ref, x_ref, dh1_ref, g_ref, w_ref, tri_ref, sq_ref, sk_ref,
             band_ref, dep_ref, dx_ref, dw_ref, dg_ref, db_ref, carry, acc, sem):
        s = pl.program_id(0)
        i = nt - 1 - s

        @pl.when(s == 0)
        def _():
            carry[...] = jnp.zeros_like(carry)
            dg_ref[...] = jnp.zeros_like(dg_ref)
            db_ref[...] = jnp.zeros_like(db_ref)

        dq_cat = jnp.concatenate([dqa_ref[p] for p in range(NH // 2)], axis=1)
        dk_cat = dka_ref[...]
        dc = _dot2t(dq_cat, sq_ref[...]) - _dot2t(dk_cat, sk_ref[...])
        dlf = carry[...] + _dot3(tri_ref[...], dc)
        carry[...] = carry[...] + jnp.sum(dc, axis=0, keepdims=True)
        flv = fl_ref[...]
        lane = lax.broadcasted_iota(jnp.int32, flv.shape, 1)
        d_fl = jnp.where(lane < NH, dlf / (1.0 + jnp.exp(flv)), 0.0)
        db_ref[...] = db_ref[...] + jnp.sum(d_fl, axis=0, keepdims=True)
        low = lax.broadcasted_iota(jnp.int32, (tt, LANE), 1) < HD
        dqs, dks = [], []
        for p in range(NH // 2):
            b0, b1 = slice(2 * LANE * p, 2 * LANE * p + LANE), slice(2 * LANE * p + LANE, 2 * LANE * (p + 1))
            dqs.append(jnp.where(low, dq_cat[:, b0], pltpu.roll(dq_cat[:, b1], HD, 1)) * (1.0 / math.sqrt(HD)))
            dks.append(jnp.where(low, dk_cat[:, b0], pltpu.roll(dk_cat[:, b1], HD, 1)))
        nxt = dyn_ref[...] * jnp.where(i < nt - 1, 1.0, 0.0)
        tok = i * tt + lax.broadcasted_iota(jnp.int32, (tt, PC), 0)
        dus = []
        for g, w in enumerate(WINS):
            sl = slice(PC * g, PC * (g + 1))
            dycg = dyc_ref[:, sl]
            ext = jnp.concatenate([dycg, nxt[:, sl]], axis=0)
            dus.append(_dot2(band_ref[g], ext) - dycg * jnp.minimum(tok + 1, w).astype(F32))
        d_z = jnp.concatenate(dqs + dks + [dv_ref[...], d_fl] + dus, axis=1).astype(BF)
        xv = x_ref[...]
        gv = g_ref[...]
        hn = (xv * _rstd(xv) * gv).astype(BF)
        d_hn = jnp.dot(d_z, w_ref[...], preferred_element_type=F32)
        contrib = _mm_tn(d_z, hn)

        @pl.when(s == 0)
        def _():
            acc[...] = contrib

        @pl.when(s > 0)
        def _():
            acc[...] = acc[...] + contrib

        d1, dg = _rms_bwd(d_hn, xv, gv)
        dg_ref[...] = dg_ref[...] + dg
        dx_ref[...] = dh1_ref[...] + d1

        @pl.when(s == nt - 1)
        def _():
            cp = pltpu.make_async_copy(acc, dw_ref, sem)
            cp.start()
            cp.wait()

    row = lambda w: pl.BlockSpec((tt, w), lambda s: (rev(s), 0))
    return pl.pallas_call(
        body, name="in_bwd", grid=(nt,),
        in_specs=[pl.BlockSpec((NH // 2, tt, 2 * LANE), lambda s: (0, rev(s), 0)), row(NH * LANE), row(DA), row(DP),
                  pl.BlockSpec((HALO, DP), lambda s: (jnp.minimum((rev(s) + 1) * hb, nt * hb - 1), 0)),
                  row(LANE), row(D), row(D), _full((1, D)), _resident((ZW, D)), _full((tt, tt)),
                  _full((NH * LANE, LANE)), _full((NH * LANE, LANE)), _full((len(WINS), tt, tt + HALO)), ANY],
        out_specs=[row(D), pl.BlockSpec(memory_space=pl.ANY), _full((1, D)), _full((1, LANE))],
        out_shape=[jax.ShapeDtypeStruct((t, D), F32), jax.ShapeDtypeStruct((ZW, D), F32),
                   jax.ShapeDtypeStruct((1, D), F32), jax.ShapeDtypeStruct((1, LANE), F32)],
        scratch_shapes=[pltpu.VMEM((1, LANE), F32), pltpu.VMEM((ZW, D), F32), pltpu.SemaphoreType.DMA],
        compiler_params=_cp(),
    )(dqa, dka, dv, dyc, dyc, fl, x, dh1, g1, w_in_t, tri_u, selq, selk, band_t, dep)


def _dot2t(x, sel):
    hi, lo = _split2(x)
    return jnp.dot(hi, sel, preferred_element_type=F32) + jnp.dot(lo, sel, preferred_element_type=F32)


class _NoComm:
    def __init__(self, w2):
        self.w2 = w2
        self.dep = jnp.zeros((8, LANE), F32)

    def weights2(self, after):
        return self.w2

    def after_ffn(self, grads2):
        self.grads2 = grads2
        return self.dep

    def after_mix(self, after):
        return self.dep

    def after_attn(self, after):
        return self.dep


def _local_step(x, p, tgt, sm, w1, comm):
    w_in_t, w_out = w1
    tt = _tile(x.shape[0])
    eq, ek, rowq, rowk, selq, selk = _aug_consts()
    b_pad = jnp.pad(sm["b_forget"], ((0, 0), (0, LANE - NH)))
    qa, ka, v, u, fl = _in_proj(x, sm["g_mix_pre"], w_in_t, b_pad, _tri(tt, False), eq, ek, rowq, rowk, comm.dep)
    a, lse = _attn_fwd(qa, ka, v)
    wg_t, wu_t, wd, w_ple, w_pg = comm.weights2(a)
    yb, m, o, h1 = _mix_out(a, u, x, _band(tt, False), sm["w_pool"], sm["pool_scale"], sm["g_attn_grp"],
                            sm["g_pool_grp"], w_out, sm["g_mix_post"])
    hn2, gate, up, dff, dh2, loss, dwpg, dwple, dgple, dgfpost = _ffn_fwd(
        h1, sm["g_ffn_pre"], wg_t, wu_t, wd, sm["g_ffn_post"], p, w_ple, sm["g_ple"], w_pg, tgt)
    dwg_t, dwu_t, dwd, d_hn2 = _ffn_bwd(hn2, gate, up, dff, wg_t, wu_t, wd)
    dep = comm.after_ffn((dwg_t, dwu_t, dwd, dwple, dwpg))
    dh1, d_a, dyc, dgfpre, dgpost, dgattn, dgpool, dps, dwpool, dwout = _mix_bwd(
        d_hn2, dh2, h1, o, a, m, yb, sm["g_ffn_pre"], sm["g_mix_post"], sm["g_attn_grp"], sm["g_pool_grp"],
        w_out, sm["w_pool"], sm["pool_scale"], dep)
    dqa, dka, dvv = _attn_bwd(qa, ka, v, a, d_a, lse, comm.after_mix(dh1))
    dx, dwin_t, dg1, dbf = _in_bwd(dqa, dka, dvv, dyc, fl, x, dh1, sm["g_mix_pre"], w_in_t, _tri(tt, True),
                                   selq, selk, _band(tt, True), comm.after_attn(dvv))
    small = dict(loss=loss[0:1, 0:1], g_mix_pre=dg1, b_forget=dbf[:, :NH], g_attn_grp=dgattn, g_pool_grp=dgpool,
                 w_pool=dwpool, pool_scale=dps, g_mix_post=dgpost, g_ffn_pre=dgfpre, g_ffn_post=dgfpost, g_ple=dgple)
    return dx, (dwin_t, dwout), small


def _place():
    x, y, c = lax.axis_index("x"), lax.axis_index("y"), lax.axis_index("c")
    return x, y, c, [(1 - x, y), (x, 1 - y), (1 - x, 1 - y)]


def _rows(c, h):
    return pl.ds(pl.multiple_of(c * h, 16), h)


def _plan_gather(h):
    def plan(src, land):
        x, y, c, chips = _place()
        return [(src.at[_rows(c, h), :], land.at[2 * x + y, _rows(c, h), :], (cx, cy, c),
                 land.at[2 * cx + cy, _rows(c, h), :]) for cx, cy in chips]
    return plan


def _plan_forward(h):
    def plan(src, land):
        x, y, c, chips = _place()
        return [(src.at[2 * cx + cy, _rows(c, h), :], land.at[2 * cx + cy], (x, y, 1 - c), land.at[2 * cx + cy])
                for cx, cy in chips]
    return plan


def _plan_pair_rows(h):
    def plan(src, land):
        x, y, c, _ = _place()
        return [(src.at[:, _rows(1 - c, h), :], land, (x, y, 1 - c), land)]
    return plan


def _plan_pair_all(src, land):
    x, y, c, _ = _place()
    return [(src, land, (x, y, 1 - c), land)]


def _plan_scatter(src, land):
    x, y, c, chips = _place()
    return [(src.at[2 * cx + cy], land.at[k], (cx, cy, c), land.at[k]) for k, (cx, cy) in enumerate(chips)]


def _remote(src, dst, send_sems, recv_sems, k, peer):
    return pltpu.make_async_remote_copy(src_ref=src, dst_ref=dst, send_sem=send_sems.at[k], recv_sem=recv_sems.at[k],
                                        device_id=peer, device_id_type=MESH)


def _exchange(name, n, src, land, plan):
    def body(src_ref, land_ref, send_sems, recv_sems):
        copies = plan(src_ref, land_ref)
        for k, (s, d, peer, _) in enumerate(copies):
            _remote(s, d, send_sems, recv_sems, k, peer).start()
        for k, (s, _, peer, mine) in enumerate(copies):
            _remote(s, mine, send_sems, recv_sems, k, peer).wait_recv()
        for k, (s, d, peer, _) in enumerate(copies):
            _remote(s, d, send_sems, recv_sems, k, peer).wait_send()

    return pl.pallas_call(
        body, name=name, in_specs=[ANY], out_specs=ANY, out_shape=land,
        scratch_shapes=[pltpu.SemaphoreType.DMA((n,)), pltpu.SemaphoreType.DMA((n,))],
    )(src)


HBM = pl.BlockSpec(memory_space=pltpu.HBM)
SEM = pl.BlockSpec(memory_space=pltpu.SEMAPHORE)
EFFECT = pltpu.SideEffectType.DATAFLOW_SIDE_EFFECTING


def _exchange_start(name, n, src, land, plan):
    def body(src_ref, land_ref, send_sems, recv_sems, src_thru, land_thru, token):
        for k, (s, d, peer, _) in enumerate(plan(src_ref, land_ref)):
            _remote(s, d, send_sems, recv_sems, k, peer).start()
        token[...] = jnp.zeros_like(token)

    return pl.pallas_call(
        body, name=name,
        out_shape=(pltpu.SemaphoreType.DMA((n,)), pltpu.SemaphoreType.DMA((n,)), pltpu.HBM(src.shape, src.dtype),
                   pltpu.HBM(land.shape, land.dtype), jax.ShapeDtypeStruct((8, LANE), F32)),
        in_specs=(HBM, HBM), out_specs=(SEM, SEM, HBM, HBM, pl.BlockSpec(memory_space=pltpu.VMEM)),
        input_output_aliases={0: 2, 1: 3},
        compiler_params=pltpu.CompilerParams(has_side_effects=EFFECT),
    )(pltpu.with_memory_space_constraint(src, pltpu.HBM), pltpu.with_memory_space_constraint(land, pltpu.HBM))


def _exchange_wait(name, started, plan, after):
    send_sems, recv_sems, src, land, _ = started

    def body(src_ref, land_ref, send_sems, recv_sems, after_ref, src_out, land_out):
        for k, (s, _, peer, mine) in enumerate(plan(src_ref, land_ref)):
            cp = _remote(s, mine, send_sems, recv_sems, k, peer)
            cp.wait_send()
            cp.wait_recv()

    return pl.pallas_call(
        body, name=name, out_shape=(pltpu.HBM(src.shape, src.dtype), pltpu.HBM(land.shape, land.dtype)),
        in_specs=(HBM, HBM, SEM, SEM, ANY), out_specs=(HBM, HBM), input_output_aliases={0: 0, 1: 1},
        compiler_params=pltpu.CompilerParams(has_side_effects=EFFECT),
    )(src, land, send_sems, recv_sems, after)


def _pair_sum(name, cidx, g, recv, br):
    h = recv.shape[1]
    nb = h // br

    def body(c_ref, g_ref, r_ref, out_ref):
        out_ref[...] = (g_ref[...] + r_ref[...]).astype(BF)

    return pl.pallas_call(
        body, name=name,
        grid_spec=pltpu.PrefetchScalarGridSpec(
            num_scalar_prefetch=1, grid=(NSHARD, nb),
            in_specs=[pl.BlockSpec((1, br, D), lambda s, i, c: (s, c[0] * nb + i, 0)),
                      pl.BlockSpec((1, br, D), lambda s, i, c: (s, i, 0))],
            out_specs=pl.BlockSpec((1, br, D), lambda s, i, c: (s, i, 0))),
        out_shape=jax.ShapeDtypeStruct((NSHARD, h, D), BF),
    )(cidx, g, recv)


def _chip_sum(name, jidx, pb, y, br):
    h = y.shape[1]

    def body(j_ref, p_ref, y_ref, out_ref):
        acc = p_ref[0].astype(F32)
        for k in range(NSHARD - 1):
            acc = acc + y_ref[k].astype(F32)
        out_ref[...] = acc

    return pl.pallas_call(
        body, name=name,
        grid_spec=pltpu.PrefetchScalarGridSpec(
            num_scalar_prefetch=1, grid=(h // br,),
            in_specs=[pl.BlockSpec((1, br, D), lambda i, j: (j[0], i, 0)),
                      pl.BlockSpec((NSHARD - 1, br, D), lambda i, j: (0, i, 0))],
            out_specs=pl.BlockSpec((br, D), lambda i, j: (i, 0))),
        out_shape=jax.ShapeDtypeStruct((h, D), F32),
    )(jidx, pb, y)


def _all_reduce_small(v):
    rows = v.shape[0]

    def body(in_ref, out_ref, buf, send_sems, recv_sems):
        x, y, c, _ = _place()
        me = 4 * x + 2 * y + c
        flips = [(r >> 2 & 1, r >> 1 & 1, r & 1) for r in range(1, 8)]

        def peer(f):
            return tuple(1 - a if b else a for a, b in zip((x, y, c), f))

        def copy(k, slot, to):
            return pltpu.make_async_remote_copy(src_ref=in_ref, dst_ref=buf.at[slot], send_sem=send_sems.at[k],
                                                recv_sem=recv_sems.at[k], device_id=to, device_id_type=MESH)

        sends = [copy(k, me, peer(f)) for k, f in enumerate(flips)]
        for cp in sends:
            cp.start()
        buf[me] = in_ref[...]
        for k, f in enumerate(flips):
            px, py, pc = peer(f)
            copy(k, 4 * px + 2 * py + pc, peer(f)).wait_recv()
        for cp in sends:
            cp.wait_send()
        acc = buf[0]
        for k in range(1, 8):
            acc = acc + buf[k]
        out_ref[...] = acc

    vm = pl.BlockSpec(memory_space=pltpu.VMEM)
    return pl.pallas_call(
        body, name="all_reduce_small", in_specs=[vm], out_specs=vm,
        out_shape=jax.ShapeDtypeStruct(v.shape, F32),
        scratch_shapes=[pltpu.VMEM((8, rows, LANE), F32), pltpu.SemaphoreType.DMA((7,)), pltpu.SemaphoreType.DMA((7,))],
    )(v)


def _adamw_math(w, g, m, v):
    m = ADAM_B1 * m + (1.0 - ADAM_B1) * g
    v = ADAM_B2 * v + (1.0 - ADAM_B2) * (g * g)
    m_hat = m / (1.0 - ADAM_B1 ** ADAM_STEP)
    v_hat = v / (1.0 - ADAM_B2 ** ADAM_STEP)
    delta = -ADAM_LR * (m_hat / (jnp.sqrt(v_hat) + ADAM_EPS) + ADAM_WD * w)
    return delta, m, v


def _adamw(w, g, m, v):
    r, c = w.shape
    br = next(b for b in (256, 176, r) if r % b == 0)

    def body(w_ref, g_ref, m_ref, v_ref, d_ref, nm_ref, nv_ref):
        d_ref[...], nm_ref[...], nv_ref[...] = _adamw_math(w_ref[...], g_ref[...], m_ref[...], v_ref[...])

    spec = pl.BlockSpec((br, c), lambda i: (i, 0))
    return pl.pallas_call(
        body, name="adamw", grid=(r // br,), in_specs=[spec] * 4, out_specs=[spec] * 3,
        out_shape=[jax.ShapeDtypeStruct((r, c), F32)] * 3,
    )(w, g, m, v)


def _adamw_small(ws, gs, ms, vs):
    n = len(ws)

    def body(*refs):
        ins, outs = refs[:4 * n], refs[4 * n:]
        for k in range(n):
            d, m, v = _adamw_math(ins[k][...], ins[n + k][...], ins[2 * n + k][...], ins[3 * n + k][...])
            outs[k][...] = d
            outs[n + k][...] = m
            outs[2 * n + k][...] = v

    vm = pl.BlockSpec(memory_space=pltpu.VMEM)
    out = pl.pallas_call(
        body, name="adamw_small", in_specs=[vm] * (4 * n), out_specs=[vm] * (3 * n),
        out_shape=[jax.ShapeDtypeStruct(w.shape, F32) for w in ws] * 3,
    )(*ws, *gs, *ms, *vs)
    return out[:n], out[n:2 * n], out[2 * n:]


BIG = ("w_in", "w_out", "w_ffn_gate", "w_ffn_up", "w_ffn_down", "w_ple_proj", "w_ple_gate")
SMALL = ("g_mix_pre", "b_forget", "g_attn_grp", "g_pool_grp", "w_pool", "pool_scale", "g_mix_post", "g_ffn_pre",
         "g_ffn_post", "g_ple")
VECTORS = tuple(n for n in SMALL if n != "w_pool")
ORDER = ("g_mix_pre", "w_in", "b_forget", "g_attn_grp", "g_pool_grp", "w_pool", "pool_scale", "w_out", "g_mix_post",
         "g_ffn_pre", "w_ffn_gate", "w_ffn_up", "w_ffn_down", "g_ffn_post", "w_ple_proj", "g_ple", "w_ple_gate")


def _pad_rows(a, rows):
    return jnp.pad(a, ((0, rows - a.shape[0]), (0, 0)))


def _stack1(w_in, w_out):
    return _pad_rows(jnp.concatenate([_pad_rows(w_in.T, IN_PAD), w_out], axis=0), ROWS1)


def _stack2(wg, wu, wd, wple, wpg):
    return _pad_rows(jnp.concatenate([wg.T, wu.T, wd, wple.reshape(DPLE // NSHARD, D), wpg], axis=0), ROWS2)


def _unstack1(s):
    return s[:IN_SH].T, s[O1_OUT:USED1]


def _unstack2(s):
    return s[:O2_U].T, s[O2_U:O2_D].T, s[O2_D:O2_PLE], s[O2_PLE:O2_PG].reshape(DPLE, DPLE), s[O2_PG:USED2]


def _cat(g, lo, hi):
    return g[:, lo:hi].reshape(NSHARD * (hi - lo), D)


def _unstack1_full(g):
    w_in_t = _cat(g, 0, IN_SH)
    w_in_t = jnp.concatenate([w_in_t[:3 * DA], _pad_rows(w_in_t[3 * DA:3 * DA + NH], LANE), w_in_t[3 * DA + NH:]], axis=0)
    return w_in_t, _cat(g, O1_OUT, USED1)


def _unstack2_full(g):
    w_ple = g[:, O2_PLE:O2_PG].reshape(NSHARD, DPLE, DPLE).transpose(1, 0, 2).reshape(DPLE, D)
    return _cat(g, 0, O2_U), _cat(g, O2_U, O2_D), _cat(g, O2_D, O2_PLE), w_ple, _cat(g, O2_PG, USED2)


def _shards(a):
    return a.reshape(NSHARD, a.shape[0] // NSHARD, D)


def _stack1_full(dwin_t, dwout):
    dwin_t = jnp.concatenate([dwin_t[:3 * DA + NH], dwin_t[3 * DA + LANE:]], axis=0).reshape(NSHARD, IN_SH, D)
    dwin_t = jnp.pad(dwin_t, ((0, 0), (0, IN_PAD - IN_SH), (0, 0)))
    s = jnp.concatenate([dwin_t, _shards(dwout)], axis=1)
    return jnp.pad(s, ((0, 0), (0, ROWS1 - USED1), (0, 0)))


def _stack2_full(dwg_t, dwu_t, dwd, dwple, dwpg):
    dwple = dwple.reshape(DPLE, NSHARD, DPLE).transpose(1, 0, 2).reshape(NSHARD, DPLE // NSHARD, D)
    s = jnp.concatenate([_shards(dwg_t), _shards(dwu_t), _shards(dwd), dwple, _shards(dwpg)], axis=1)
    return jnp.pad(s, ((0, 0), (0, ROWS2 - USED2), (0, 0)))


def _assemble(land, other, own, me, c):
    h = other.shape[1]
    full = lax.dynamic_update_slice(land, other, (0, (1 - c) * h, 0))
    return lax.dynamic_update_slice(full, own[None], (me, 0, 0))


def _join_halves(mine, other, c):
    h = mine.shape[0]
    full = lax.dynamic_update_slice(lax.empty((2 * h, D), F32), mine, (c * h, 0))
    return lax.dynamic_update_slice(full, other, ((1 - c) * h, 0))


def _sds(shape, dtype):
    return jax.ShapeDtypeStruct(shape, dtype)


class _Comm:
    def __init__(self, stack2, me, c):
        self.stack2, self.me, self.c = stack2, me, c
        self.cidx = c.astype(jnp.int32).reshape(1)
        self.jidx = me.astype(jnp.int32).reshape(1)
        self.h = ROWS2 // 2
        self.gather = _exchange_start("gather2_start", 3, stack2, lax.empty((NSHARD, ROWS2, D), BF), _plan_gather(self.h))
        self.dep = self.gather[4]

    def weights2(self, after):
        own, land = _exchange_wait("gather2_wait", self.gather, _plan_gather(self.h), after)
        other = _exchange("gather2_forward", 3, land, _sds((NSHARD, self.h, D), BF), _plan_forward(self.h))
        return _unstack2_full(_assemble(land, other, own, self.me, self.c))

    def after_ffn(self, grads2):
        g = _stack2_full(*grads2)
        self.pair = _exchange_start("reduce2_pair_start", 1, g, lax.empty((NSHARD, self.h, D), F32),
                                    _plan_pair_rows(self.h))
        return self.pair[4]

    def after_mix(self, after):
        g, recv = _exchange_wait("reduce2_pair_wait", self.pair, _plan_pair_rows(self.h), after)
        pb = _pair_sum("pair_sum2", self.cidx, g, recv, RED2)
        self.chip = _exchange_start("reduce2_chip_start", 3, pb, lax.empty((NSHARD - 1, self.h, D), BF), _plan_scatter)
        return self.chip[4]

    def after_attn(self, after):
        pb, y = _exchange_wait("reduce2_chip_wait", self.chip, _plan_scatter, after)
        f = _chip_sum("chip_sum2", self.jidx, pb, y, RED2)
        self.last = _exchange_start("reduce2_gather_start", 1, f, lax.empty((self.h, D), F32), _plan_pair_all)
        return self.last[4]

    def reduced2(self, after):
        f, other = _exchange_wait("reduce2_gather_wait", self.last, _plan_pair_all, after)
        return _join_halves(f, other, self.c)


def _pack_small(small):
    parts = []
    for name in ("loss",) + SMALL:
        flat = small[name].reshape(-1)
        parts.append(jnp.pad(flat, (0, -flat.shape[0] % LANE)).reshape(-1, LANE))
    v = jnp.concatenate(parts, axis=0)
    return _pad_rows(v, v.shape[0] + (-v.shape[0] % 8))


def _unpack_small(v, shapes):
    out, r = {}, 0
    for name in ("loss",) + SMALL:
        n = math.prod(shapes[name])
        rows = -(-n // LANE)
        out[name] = v[r:r + rows].reshape(-1)[:n].reshape(shapes[name])
        r += rows
    return out


def kernel(x, p, g_mix_pre, w_in, b_forget, g_attn_grp, g_pool_grp, w_pool, pool_scale, w_out, g_mix_post, g_ffn_pre, w_ffn_gate, w_ffn_up, w_ffn_down, g_ffn_post, w_ple_proj, g_ple, w_ple_gate, loss_target, m_g_mix_pre, m_w_in, m_b_forget, m_g_attn_grp, m_g_pool_grp, m_w_pool, m_pool_scale, m_w_out, m_g_mix_post, m_g_ffn_pre, m_w_ffn_gate, m_w_ffn_up, m_w_ffn_down, m_g_ffn_post, m_w_ple_proj, m_g_ple, m_w_ple_gate, v_g_mix_pre, v_w_in, v_b_forget, v_g_attn_grp, v_g_pool_grp, v_w_pool, v_pool_scale, v_w_out, v_g_mix_post, v_g_ffn_pre, v_w_ffn_gate, v_w_ffn_up, v_w_ffn_down, v_g_ffn_post, v_w_ple_proj, v_g_ple, v_w_ple_gate):
    args = dict(locals())
    strip = lambda n, a: a if n in VECTORS else a[0]
    w = {n: strip(n, args[n]) for n in ORDER}
    mom = {n: strip(n, args["m_" + n]) for n in ORDER}
    var = {n: strip(n, args["v_" + n]) for n in ORDER}
    sm = {n: w[n] for n in SMALL}

    c = lax.axis_index("c")
    me = 2 * lax.axis_index("x") + lax.axis_index("y")
    h1 = ROWS1 // 2
    bf = lambda n: w[n].astype(BF)
    comm = _Comm(_stack2(*[bf(n) for n in BIG[2:]]), me, c)
    stack1 = _stack1(bf("w_in"), bf("w_out"))
    land = _exchange("gather1", 3, stack1, _sds((NSHARD, ROWS1, D), BF), _plan_gather(h1))
    other = _exchange("gather1_forward", 3, land, _sds((NSHARD, h1, D), BF), _plan_forward(h1))
    w1 = _unstack1_full(_assemble(land, other, stack1, me, c))
    dx, grads1, small = _local_step(x[0], p[0, 0], loss_target[0], sm, w1, comm)

    shapes = {n: small[n].shape for n in small}
    red_small = _unpack_small(_all_reduce_small(_pack_small(small)), shapes)
    loss = 0.5 / D * red_small["loss"][0, 0]

    g1 = _stack1_full(*grads1)
    recv = _exchange("reduce1_pair", 1, g1, _sds((NSHARD, h1, D), F32), _plan_pair_rows(h1))
    pb = _pair_sum("pair_sum1", comm.cidx, g1, recv, RED1)
    y = _exchange("reduce1_chip", 3, pb, _sds((NSHARD - 1, h1, D), BF), _plan_scatter)
    f = _chip_sum("chip_sum1", comm.jidx, pb, y, RED1)
    f_other = _exchange("reduce1_gather", 1, f, _sds((h1, D), F32), _plan_pair_all)
    grads = dict(zip(BIG, _unstack1(_join_halves(f, f_other, c)) + _unstack2(comm.reduced2(dx))))
    for n in SMALL:
        grads[n] = red_small[n].reshape(w[n].shape)

    delta, new_m, new_v = {}, {}, {}
    for n in BIG:
        delta[n], new_m[n], new_v[n] = _adamw(w[n], grads[n], mom[n], var[n])
    two_d = lambda a: a.reshape(-1, a.shape[-1])
    ds, ms, vs = _adamw_small([two_d(w[n]) for n in SMALL], [two_d(grads[n]) for n in SMALL],
                              [two_d(mom[n]) for n in SMALL], [two_d(var[n]) for n in SMALL])
    for k, n in enumerate(SMALL):
        delta[n], new_m[n], new_v[n] = ds[k].reshape(w[n].shape), ms[k].reshape(w[n].shape), vs[k].reshape(w[n].shape)

    lead = lambda d: [d[n] if n in VECTORS else d[n][None] for n in ORDER]
    return (loss, dx[None], *lead(grads), *lead(delta), *lead(new_m), *lead(new_v))
```

```python
import functools
import math

import jax
import jax.numpy as jnp
import numpy as np
from jax import lax
from jax.experimental import pallas as pl
from jax.experimental.pallas import tpu as pltpu

F32 = jnp.float32
BF = jnp.bfloat16
MESH = pl.DeviceIdType.MESH

D = 1024
DA = 512
DP = 512
NH = 8
HD = 64
DFF = 2816
DPLE = 256
WINS = (2, 4, 8, 16)
PC = 128
ZW = 3 * DA + 128 + DP
EPS = 1e-6
NSHARD = 4

LANE = 128
HALO = 128

IN_SH = 514
IN_PAD = 528
FF_SH = DFF // NSHARD
O1_OUT, USED1, ROWS1 = 528, 784, 800
O2_U, O2_D, O2_PLE, O2_PG, USED2, ROWS2 = 704, 1408, 2112, 2176, 2432, 2560
RED1, RED2 = 400, 128

ADAM_LR, ADAM_B1, ADAM_B2, ADAM_EPS, ADAM_WD, ADAM_STEP = 0.001, 0.9, 0.999, 1e-8, 0.01, 10

VMEM_LIMIT = 56 * 1024 * 1024


def _cp(**kw):
    return pltpu.CompilerParams(vmem_limit_bytes=VMEM_LIMIT, **kw)


def _mm(a, b):
    return jnp.dot(a.astype(BF), b.astype(BF), preferred_element_type=F32)


def _mm_nt(a, b):
    return lax.dot_general(a.astype(BF), b.astype(BF), (((1,), (1,)), ((), ())), preferred_element_type=F32)


def _mm_tn(a, b):
    return lax.dot_general(a.astype(BF), b.astype(BF), (((0,), (0,)), ((), ())), preferred_element_type=F32)


def _split2(x):
    hi = x.astype(BF)
    lo = (x - hi.astype(F32)).astype(BF)
    return hi, lo


def _split3(x):
    hi = x.astype(BF)
    r = x - hi.astype(F32)
    mid = r.astype(BF)
    lo = (r - mid.astype(F32)).astype(BF)
    return hi, mid, lo


def _dot3(m, x):
    hi, mid, lo = _split3(x)
    return (jnp.dot(m, hi, preferred_element_type=F32) + jnp.dot(m, mid, preferred_element_type=F32)
            + jnp.dot(m, lo, preferred_element_type=F32))


def _dot2(m, x):
    hi, lo = _split2(x)
    return jnp.dot(m, hi, preferred_element_type=F32) + jnp.dot(m, lo, preferred_element_type=F32)


def _rstd(x):
    return lax.rsqrt(jnp.mean(x * x, axis=-1, keepdims=True) + EPS)


def _rms_bwd(dy, x, g):
    r = _rstd(x)
    xh = x * r
    dg = jnp.sum(dy * xh, axis=0, keepdims=True)
    dxh = dy * g
    dx = r * (dxh - xh * jnp.mean(dxh * xh, axis=-1, keepdims=True))
    return dx, dg


def _sigmoid(x):
    return 1.0 / (1.0 + jnp.exp(-x))


ANY = pl.BlockSpec(memory_space=pl.ANY)


def _full(shape):
    n = len(shape)
    return pl.BlockSpec(shape, lambda *_: (0,) * n)


def _resident(shape):
    n = len(shape)
    return pl.BlockSpec(shape, lambda *_: (0,) * n, pipeline_mode=pl.Buffered(1))


def _tile(t):
    return 512 if t % 512 == 0 else t


def _tri(n, upper):
    r, c = np.indices((n, n))
    return ((c >= r) if upper else (c <= r)).astype(BF)


def _band(tt, transpose):
    r, c = np.indices((tt, tt + HALO))
    d = (c - r) if transpose else (r + HALO - c)
    return np.stack([((d >= 0) & (d < w)).astype(BF) for w in WINS])


def _aug_consts():
    row, col = np.indices((3 * LANE, NH * LANE))
    piece, head = row // LANE, row % LANE
    ch, cl = col // LANE, col % LANE
    eq = ((head == ch) & (cl == HD + piece)).astype(BF)
    ek = -((head == ch) & (cl == HD + 3 + piece)).astype(BF)
    lane = np.arange(NH * LANE)[None, :] % LANE
    rowq = ((lane >= HD + 3) & (lane < HD + 6)).astype(np.float32)
    rowk = ((lane >= HD) & (lane < HD + 3)).astype(np.float32)
    r2, c2 = np.indices((NH * LANE, LANE))
    selq = ((r2 // LANE == c2) & (r2 % LANE == HD)).astype(BF)
    selk = ((r2 // LANE == c2) & (r2 % LANE == HD + 3)).astype(BF)
    return eq, ek, rowq, rowk, selq, selk


def _in_proj(x, g1, w_in_t, b_pad, tri, eq, ek, rowq, rowk, dep):
    t = x.shape[0]
    tt = _tile(t)

    def body(x_ref, g_ref, w_ref, b_ref, tri_ref, eq_ref, ek_ref, rq_ref, rk_ref, dep_ref,
             qa_ref, ka_ref, v_ref, u_ref, fl_ref, carry):
        i = pl.program_id(0)

        @pl.when(i == 0)
        def _():
            carry[...] = jnp.zeros_like(carry)

        xv = x_ref[...]
        hn = (xv * _rstd(xv) * g_ref[...]).astype(BF)
        z = _mm_nt(hn, w_ref[...])
        fl = z[:, 3 * DA:3 * DA + LANE] + b_ref[...]
        lane = lax.broadcasted_iota(jnp.int32, fl.shape, 1)
        lf = jnp.where(lane < NH, jnp.minimum(fl, 0.0) - jnp.log(1.0 + jnp.exp(-jnp.abs(fl))), 0.0)
        c = carry[...] + _dot3(tri_ref[...], lf)
        carry[...] = carry[...] + jnp.sum(lf, axis=0, keepdims=True)
        caug = jnp.concatenate(_split3(c), axis=1)
        aug_q = jnp.dot(caug, eq_ref[...], preferred_element_type=F32) + rq_ref[...]
        aug_k = jnp.dot(caug, ek_ref[...], preferred_element_type=F32) + rk_ref[...]
        low = lax.broadcasted_iota(jnp.int32, (tt, LANE), 1) < HD
        for p in range(NH // 2):
            qp = z[:, LANE * p:LANE * (p + 1)] * (1.0 / math.sqrt(HD))
            kp = z[:, DA + LANE * p:DA + LANE * (p + 1)]
            for h, (qh, kh) in enumerate(((qp, kp), (pltpu.roll(qp, HD, 1), pltpu.roll(kp, HD, 1)))):
                lo_, hi_ = LANE * (2 * p + h), LANE * (2 * p + h + 1)
                qa_ref[:, lo_:hi_] = jnp.where(low, qh, aug_q[:, lo_:hi_]).astype(BF)
                ka_ref[:, lo_:hi_] = jnp.where(low, kh, aug_k[:, lo_:hi_]).astype(BF)
        v_ref[...] = z[:, 2 * DA:3 * DA].astype(BF)
        u_ref[...] = z[:, 3 * DA + LANE:]
        fl_ref[...] = fl

    return pl.pallas_call(
        body, name="in_proj", grid=(t // tt,),
        in_specs=[pl.BlockSpec((tt, D), lambda i: (i, 0)), _full((1, D)), _resident((ZW, D)), _full((1, LANE)),
                  _full((tt, tt)), _full((3 * LANE, NH * LANE)), _full((3 * LANE, NH * LANE)),
                  _full((1, NH * LANE)), _full((1, NH * LANE)), ANY],
        out_specs=[pl.BlockSpec((tt, NH * LANE), lambda i: (i, 0)), pl.BlockSpec((tt, NH * LANE), lambda i: (i, 0)),
                   pl.BlockSpec((tt, DA), lambda i: (i, 0)), pl.BlockSpec((tt, DP), lambda i: (i, 0)),
                   pl.BlockSpec((tt, LANE), lambda i: (i, 0))],
        out_shape=[jax.ShapeDtypeStruct((t, NH * LANE), BF), jax.ShapeDtypeStruct((t, NH * LANE), BF),
                   jax.ShapeDtypeStruct((t, DA), BF), jax.ShapeDtypeStruct((t, DP), F32),
                   jax.ShapeDtypeStruct((t, LANE), F32)],
        scratch_shapes=[pltpu.VMEM((1, LANE), F32)],
        compiler_params=_cp(),
    )(x, g1, w_in_t, b_pad, tri, eq, ek, rowq, rowk, dep)


def _attn_fwd(qa, ka, v):
    t = qa.shape[0]
    ta = _tile(t)
    n = t // ta

    def body(q_ref, k_ref, v_ref, a_ref, lse_ref, m_ref, l_ref, acc_ref):
        i = pl.program_id(1)
        m_ref[...] = jnp.full_like(m_ref, -1e30)
        l_ref[...] = jnp.zeros_like(l_ref)
        acc_ref[...] = jnp.zeros_like(acc_ref)
        qs = [q_ref[:, LANE * h:LANE * (h + 1)] for h in range(2)]
        reps = ta // LANE

        def tile(j, masked):
            rows = pl.ds(pl.multiple_of(j * ta, ta), ta)
            v2 = v_ref[rows, :]
            s = [_mm_nt(qs[h], k_ref[rows, LANE * h:LANE * (h + 1)]) for h in range(2)]
            if masked:
                keep = (lax.broadcasted_iota(jnp.int32, (ta, ta), 1) <= lax.broadcasted_iota(jnp.int32, (ta, ta), 0))
                s = [jnp.where(keep, sh, -1e30) for sh in s]
            m_old = [m_ref[h] for h in range(2)]
            m_new = [jnp.maximum(m_old[h], jnp.max(s[h], axis=1, keepdims=True)) for h in range(2)]
            pe = [jnp.exp(s[h] - jnp.tile(m_new[h], (1, reps))) for h in range(2)]
            alpha = [jnp.exp(m_old[h] - m_new[h]) for h in range(2)]
            pv = [jnp.dot(pe[h].astype(BF), v2, preferred_element_type=F32) for h in range(2)]
            for h in range(2):
                l_ref[h] = alpha[h] * l_ref[h] + jnp.sum(pe[h], axis=1, keepdims=True)
                acc_ref[h] = alpha[h] * acc_ref[h] + pv[h]
                m_ref[h] = m_new[h]

        def step(j, carry):
            tile(j, False)
            return carry

        lax.fori_loop(0, i, step, 0)
        tile(i, True)
        low = lax.broadcasted_iota(jnp.int32, (ta, LANE), 1) < HD
        a_ref[...] = jnp.where(low, acc_ref[0] / l_ref[0], acc_ref[1] / l_ref[1])
        lse_ref[...] = jnp.where(low, m_ref[0] + jnp.log(l_ref[0]), m_ref[1] + jnp.log(l_ref[1]))

    return pl.pallas_call(
        body, name="attn_fwd", grid=(NH // 2, n),
        in_specs=[pl.BlockSpec((ta, 2 * LANE), lambda p, i: (i, p)),
                  pl.BlockSpec((t, 2 * LANE), lambda p, i: (0, p)),
                  pl.BlockSpec((t, LANE), lambda p, i: (0, p))],
        out_specs=[pl.BlockSpec((ta, LANE), lambda p, i: (i, p)), pl.BlockSpec((ta, LANE), lambda p, i: (i, p))],
        out_shape=[jax.ShapeDtypeStruct((t, DA), F32), jax.ShapeDtypeStruct((t, DA), F32)],
        scratch_shapes=[pltpu.VMEM((2, ta, LANE), F32), pltpu.VMEM((2, ta, LANE), F32), pltpu.VMEM((2, ta, LANE), F32)],
        compiler_params=_cp(),
    )(qa, ka, v)


def _mix_out(a, u, x, band, w_pool, pool_scale, g_attn, g_pool, w_out, g_post):
    t = a.shape[0]
    tt = _tile(t)
    hb = tt // HALO

    def body(a_ref, u_ref, up_ref, x_ref, band_ref, wp_ref, ps_ref, ga_ref, gp_ref, wo_ref, go_ref,
             yb_ref, m_ref, o_ref, h1_ref):
        i = pl.program_id(0)
        prev = up_ref[...] * jnp.where(i > 0, 1.0, 0.0)
        tok = i * tt + lax.broadcasted_iota(jnp.int32, (tt, PC), 0)
        ms = []
        for g, w in enumerate(WINS):
            ug = u_ref[:, PC * g:PC * (g + 1)]
            ext = jnp.concatenate([prev[:, PC * g:PC * (g + 1)], ug], axis=0)
            cnt = jnp.minimum(tok + 1, w).astype(F32)
            y = (_dot2(band_ref[g], ext) / cnt - ug).astype(BF)
            yb_ref[:, PC * g:PC * (g + 1)] = y
            ms.append(_mm(y, wp_ref[g]) * ps_ref[:, PC * g:PC * (g + 1)])
        m = jnp.concatenate(ms, axis=1)
        m_ref[...] = m
        av = a_ref[...]
        mix = jnp.concatenate([av * _rstd(av) * ga_ref[...], m * _rstd(m) * gp_ref[...]], axis=1)
        o = _mm(mix, wo_ref[...])
        o_ref[...] = o
        h1_ref[...] = x_ref[...] + o * _rstd(o) * go_ref[...]

    return pl.pallas_call(
        body, name="mix_out", grid=(t // tt,),
        in_specs=[pl.BlockSpec((tt, DA), lambda i: (i, 0)), pl.BlockSpec((tt, DP), lambda i: (i, 0)),
                  pl.BlockSpec((HALO, DP), lambda i: (jnp.maximum(i * hb - 1, 0), 0)),
                  pl.BlockSpec((tt, D), lambda i: (i, 0)), _full((len(WINS), tt, tt + HALO)),
                  _full((len(WINS), PC, PC)), _full((1, DP)), _full((1, DA)), _full((1, DP)),
                  _resident((D, D)), _full((1, D))],
        out_specs=[pl.BlockSpec((tt, DP), lambda i: (i, 0)), pl.BlockSpec((tt, DP), lambda i: (i, 0)),
                   pl.BlockSpec((tt, D), lambda i: (i, 0)), pl.BlockSpec((tt, D), lambda i: (i, 0))],
        out_shape=[jax.ShapeDtypeStruct((t, DP), BF), jax.ShapeDtypeStruct((t, DP), F32),
                   jax.ShapeDtypeStruct((t, D), F32), jax.ShapeDtypeStruct((t, D), F32)],
        compiler_params=_cp(),
    )(a, u, u, x, band, w_pool, pool_scale, g_attn, g_pool, w_out, g_post)


def _ffn_fwd(h1, g_pre, wg_t, wu_t, wd, g_post, p, w_ple, g_ple, w_pg, tgt):
    t = h1.shape[0]
    tt = 256 if t % 256 == 0 else t

    def body(h1_ref, gpre_ref, wg_ref, wu_ref, wd_ref, gpost_ref, p_ref, wple_ref, gple_ref, wpg_ref, tgt_ref,
             hn_ref, gate_ref, up_ref, dff_ref, dh2_ref, loss_ref, dwpg_ref, dwple_ref, dgple_ref, dgpost_ref):
        i = pl.program_id(0)

        @pl.when(i == 0)
        def _():
            loss_ref[...] = jnp.zeros_like(loss_ref)
            dwpg_ref[...] = jnp.zeros_like(dwpg_ref)
            dwple_ref[...] = jnp.zeros_like(dwple_ref)
            dgple_ref[...] = jnp.zeros_like(dgple_ref)
            dgpost_ref[...] = jnp.zeros_like(dgpost_ref)

        h1v = h1_ref[...]
        hn = (h1v * _rstd(h1v) * gpre_ref[...]).astype(BF)
        hn_ref[...] = hn
        gate = _mm_nt(hn, wg_ref[...])
        up = _mm_nt(hn, wu_ref[...])
        gate_ref[...] = gate.astype(BF)
        up_ref[...] = up.astype(BF)
        ff = _mm(gate * _sigmoid(gate) * up, wd_ref[...])
        rff = _rstd(ff)
        ffh = ff * rff
        gpost = gpost_ref[...]
        h2 = h1v + ffh * gpost
        pv = p_ref[...]
        pe = _mm(pv, wple_ref[...])
        rpe = _rstd(pe)
        peh = pe * rpe
        gple = gple_ref[...]
        e = peh * gple
        sig = _sigmoid(_mm(h2, wpg_ref[...]))
        dv = h2 + sig * e - tgt_ref[...]
        sq = jnp.sum(jnp.sum(dv * dv, axis=1, keepdims=True), axis=0, keepdims=True)
        loss_ref[...] = loss_ref[...] + sq
        dy = dv * (1.0 / D)
        d_e = dy * sig
        d_gl = dy * e * sig * (1.0 - sig)
        dh2 = dy + _mm_nt(d_gl, wpg_ref[...])
        dh2_ref[...] = dh2
        dwpg_ref[...] = dwpg_ref[...] + _mm_tn(h2, d_gl)
        dgple_ref[...] = dgple_ref[...] + jnp.sum(d_e * peh, axis=0, keepdims=True)
        dpeh = d_e * gple
        d_pe = rpe * (dpeh - peh * jnp.mean(dpeh * peh, axis=-1, keepdims=True))
        dwple_ref[...] = dwple_ref[...] + _mm_tn(pv, d_pe)
        dgpost_ref[...] = dgpost_ref[...] + jnp.sum(dh2 * ffh, axis=0, keepdims=True)
        dffh = dh2 * gpost
        dff_ref[...] = (rff * (dffh - ffh * jnp.mean(dffh * ffh, axis=-1, keepdims=True))).astype(BF)

    row = lambda w: pl.BlockSpec((tt, w), lambda i: (i, 0))
    return pl.pallas_call(
        body, name="ffn_fwd", grid=(t // tt,),
        in_specs=[row(D), _full((1, D)), _resident((DFF, D)), _resident((DFF, D)), _resident((DFF, D)), _full((1, D)),
                  row(DPLE), _resident((DPLE, D)), _full((1, D)), _resident((D, D)), row(D)],
        out_specs=[row(D), row(DFF), row(DFF), row(D), row(D), _full((8, LANE)), _full((D, D)), _full((DPLE, D)),
                   _full((1, D)), _full((1, D))],
        out_shape=[jax.ShapeDtypeStruct((t, D), BF), jax.ShapeDtypeStruct((t, DFF), BF), jax.ShapeDtypeStruct((t, DFF), BF),
                   jax.ShapeDtypeStruct((t, D), BF), jax.ShapeDtypeStruct((t, D), F32), jax.ShapeDtypeStruct((8, LANE), F32),
                   jax.ShapeDtypeStruct((D, D), F32), jax.ShapeDtypeStruct((DPLE, D), F32),
                   jax.ShapeDtypeStruct((1, D), F32), jax.ShapeDtypeStruct((1, D), F32)],
        compiler_params=_cp(),
    )(h1, g_pre, wg_t, wu_t, wd, g_post, p, w_ple, g_ple, w_pg, tgt)


def _ffn_bwd(hn2, gate, up, dff, wg_t, wu_t, wd):
    t = hn2.shape[0]
    tt = _tile(t)
    nt = t // tt
    ch = 256
    nc = DFF // ch

    def body(hn_ref, gate_ref, up_ref, dff_ref, wg_ref, wu_ref, wd_ref,
             dwg_ref, dwu_ref, dwd_ref, dhn_ref, acc, sem):
        j, i = pl.program_id(0), pl.program_id(1)
        gate_v = gate_ref[...].astype(F32)
        up_v = up_ref[...].astype(F32)
        dffv = dff_ref[...]
        hn = hn_ref[...]
        sg = _sigmoid(gate_v)
        silu = gate_v * sg
        d_act = _mm_nt(dffv, wd_ref[...])
        d_up = (d_act * silu).astype(BF)
        d_gate = (d_act * up_v * (sg * (1.0 + gate_v * (1.0 - sg)))).astype(BF)

        @pl.when(i == 0)
        def _():
            dwg_ref[...] = jnp.zeros_like(dwg_ref)
            dwu_ref[...] = jnp.zeros_like(dwu_ref)
            dwd_ref[...] = jnp.zeros_like(dwd_ref)

        dwd_ref[...] = dwd_ref[...] + _mm_tn(silu * up_v, dffv)
        dwg_ref[...] = dwg_ref[...] + _mm_tn(d_gate, hn)
        dwu_ref[...] = dwu_ref[...] + _mm_tn(d_up, hn)
        contrib = _mm(d_gate, wg_ref[...]) + _mm(d_up, wu_ref[...])
        rows = pl.ds(pl.multiple_of(i * tt, tt), tt)

        @pl.when(j == 0)
        def _():
            acc[rows, :] = contrib

        @pl.when(j > 0)
        def _():
            acc[rows, :] = acc[rows, :] + contrib

        @pl.when((j == nc - 1) & (i == nt - 1))
        def _():
            cp = pltpu.make_async_copy(acc, dhn_ref, sem)
            cp.start()
            cp.wait()

    tok = lambda w: pl.BlockSpec((tt, w), lambda j, i: (i, 0))
    chunk = pl.BlockSpec((ch, D), lambda j, i: (j, 0))
    return pl.pallas_call(
        body, name="ffn_bwd", grid=(nc, nt),
        in_specs=[tok(D), pl.BlockSpec((tt, ch), lambda j, i: (i, j)), pl.BlockSpec((tt, ch), lambda j, i: (i, j)),
                  tok(D), chunk, chunk, chunk],
        out_specs=[chunk, chunk, chunk, pl.BlockSpec(memory_space=pl.ANY)],
        out_shape=[jax.ShapeDtypeStruct((DFF, D), F32), jax.ShapeDtypeStruct((DFF, D), F32),
                   jax.ShapeDtypeStruct((DFF, D), F32), jax.ShapeDtypeStruct((t, D), F32)],
        scratch_shapes=[pltpu.VMEM((t, D), F32), pltpu.SemaphoreType.DMA],
        compiler_params=_cp(),
    )(hn2, gate, up, dff, wg_t, wu_t, wd)


def _mix_bwd(d_hn2, dh2, h1, o, a, m, yb, g_ffn_pre, g_post, g_attn, g_pool, w_out, w_pool, pool_scale, dep):
    t = a.shape[0]
    tt = 256 if t % 256 == 0 else t

    def body(dhn_ref, dh2_ref, h1_ref, o_ref, a_ref, m_ref, yb_ref, gfp_ref, go_ref, ga_ref, gp_ref, wo_ref, wp_ref,
             ps_ref, dep_ref, dh1_ref, da_ref, dyc_ref, dgfp_ref, dgo_ref, dga_ref, dgp_ref, dps_ref, dwp_ref, dwo_ref):
        i = pl.program_id(0)

        @pl.when(i == 0)
        def _():
            for r in (dgfp_ref, dgo_ref, dga_ref, dgp_ref, dps_ref, dwp_ref, dwo_ref):
                r[...] = jnp.zeros_like(r)

        d1, dg = _rms_bwd(dhn_ref[...], h1_ref[...], gfp_ref[...])
        dgfp_ref[...] = dgfp_ref[...] + dg
        dh1 = dh2_ref[...] + d1
        dh1_ref[...] = dh1
        d_o, dg = _rms_bwd(dh1, o_ref[...], go_ref[...])
        dgo_ref[...] = dgo_ref[...] + dg
        d_mix = _mm_nt(d_o, wo_ref[...])
        av, mv = a_ref[...], m_ref[...]
        mix = jnp.concatenate([av * _rstd(av) * ga_ref[...], mv * _rstd(mv) * gp_ref[...]], axis=1)
        dwo_ref[...] = dwo_ref[...] + _mm_tn(mix, d_o)
        d_a, dg = _rms_bwd(d_mix[:, :DA], av, ga_ref[...])
        dga_ref[...] = dga_ref[...] + dg
        da_ref[...] = d_a
        d_m, dg = _rms_bwd(d_mix[:, DA:], mv, gp_ref[...])
        dgp_ref[...] = dgp_ref[...] + dg
        tok = i * tt + lax.broadcasted_iota(jnp.int32, (tt, PC), 0)
        dps = []
        for g, w in enumerate(WINS):
            sl = slice(PC * g, PC * (g + 1))
            ybg = yb_ref[:, sl]
            wpg = wp_ref[g].astype(BF)
            mlin = jnp.dot(ybg, wpg, preferred_element_type=F32)
            dmg = d_m[:, sl]
            dps.append(jnp.sum(dmg * mlin, axis=0, keepdims=True))
            dml = (dmg * ps_ref[:, sl]).astype(BF)
            dwp_ref[g] = dwp_ref[g] + _mm_tn(ybg, dml)
            dyc_ref[:, sl] = _mm_nt(dml, wpg) / jnp.minimum(tok + 1, w).astype(F32)
        dps_ref[...] = dps_ref[...] + jnp.concatenate(dps, axis=1)

    row = lambda w: pl.BlockSpec((tt, w), lambda i: (i, 0))
    return pl.pallas_call(
        body, name="mix_bwd", grid=(t // tt,),
        in_specs=[row(D), row(D), row(D), row(D), row(DA), row(DP), row(DP), _full((1, D)), _full((1, D)),
                  _full((1, DA)), _full((1, DP)), _resident((D, D)), _full((len(WINS), PC, PC)), _full((1, DP)), ANY],
        out_specs=[row(D), row(DA), row(DP), _full((1, D)), _full((1, D)), _full((1, DA)), _full((1, DP)),
                   _full((1, DP)), _full((len(WINS), PC, PC)), _full((D, D))],
        out_shape=[jax.ShapeDtypeStruct((t, D), F32), jax.ShapeDtypeStruct((t, DA), F32), jax.ShapeDtypeStruct((t, DP), F32),
                   jax.ShapeDtypeStruct((1, D), F32), jax.ShapeDtypeStruct((1, D), F32), jax.ShapeDtypeStruct((1, DA), F32),
                   jax.ShapeDtypeStruct((1, DP), F32), jax.ShapeDtypeStruct((1, DP), F32),
                   jax.ShapeDtypeStruct((len(WINS), PC, PC), F32), jax.ShapeDtypeStruct((D, D), F32)],
        compiler_params=_cp(),
    )(d_hn2, dh2, h1, o, a, m, yb, g_ffn_pre, g_post, g_attn, g_pool, w_out, w_pool, pool_scale, dep)


def _attn_bwd(qa, ka, v, a, d_a, lse, dep):
    t = qa.shape[0]
    ta = _tile(t)
    n = t // ta

    def body(q_ref, k_ref, v_ref, o_ref, do_ref, lse_ref, dep_ref, dq_ref, dk_ref, dv_ref):
        j = pl.program_id(1)

        @pl.when(j == 0)
        def _():
            dq_ref[...] = jnp.zeros_like(dq_ref)

        dk_ref[...] = jnp.zeros_like(dk_ref)
        dv_ref[...] = jnp.zeros_like(dv_ref)
        ks = [k_ref[:, LANE * h:LANE * (h + 1)] for h in range(2)]
        v2 = v_ref[...]
        lane = lax.broadcasted_iota(jnp.int32, (ta, LANE), 1)
        mine = [lane < HD, lane >= HD]

        def tile(i, masked):
            rows = pl.ds(pl.multiple_of(i * ta, ta), ta)
            do2 = do_ref[rows, :]
            prod = do2 * o_ref[rows, :]
            lse2 = lse_ref[rows, :]
            do2b = do2.astype(BF)
            qh = [q_ref[rows, LANE * h:LANE * (h + 1)] for h in range(2)]
            s = [_mm_nt(qh[h], ks[h]) for h in range(2)]
            dp = [_mm_nt(jnp.where(mine[h], do2, 0.0), v2) for h in range(2)]
            delta = [jnp.sum(jnp.where(mine[h], prod, 0.0), axis=1, keepdims=True) for h in range(2)]
            lse_h = [jnp.sum(jnp.where(lane == HD * h, lse2, 0.0), axis=1, keepdims=True) for h in range(2)]
            pr = [jnp.exp(s[h] - lse_h[h]) for h in range(2)]
            if masked:
                keep = (lax.broadcasted_iota(jnp.int32, (ta, ta), 1) <= lax.broadcasted_iota(jnp.int32, (ta, ta), 0))
                pr = [jnp.where(keep, ph, 0.0) for ph in pr]
            ds = [(pr[h] * (dp[h] - delta[h])).astype(BF) for h in range(2)]
            dv_ref[...] = dv_ref[...] + jnp.where(mine[0], _mm_tn(pr[0], do2b), _mm_tn(pr[1], do2b))
            for h in range(2):
                sl = slice(LANE * h, LANE * (h + 1))
                dk_ref[:, sl] = dk_ref[:, sl] + _mm_tn(ds[h], qh[h])
                dq_ref[0, rows, sl] = dq_ref[0, rows, sl] + jnp.dot(ds[h], ks[h], preferred_element_type=F32)

        def step(i, carry):
            tile(i, False)
            return carry

        tile(j, True)
        lax.fori_loop(j + 1, n, step, 0)

    qrow = lambda w: pl.BlockSpec((t, w), lambda p, j: (0, p))
    krow = lambda w: pl.BlockSpec((ta, w), lambda p, j: (j, p))
    return pl.pallas_call(
        body, name="attn_bwd", grid=(NH // 2, n),
        in_specs=[qrow(2 * LANE), krow(2 * LANE), krow(LANE), qrow(LANE), qrow(LANE), qrow(LANE), ANY],
        out_specs=[pl.BlockSpec((1, t, 2 * LANE), lambda p, j: (p, 0, 0)), krow(2 * LANE), krow(LANE)],
        out_shape=[jax.ShapeDtypeStruct((NH // 2, t, 2 * LANE), F32), jax.ShapeDtypeStruct((t, NH * LANE), F32),
                   jax.ShapeDtypeStruct((t, DA), F32)],
        compiler_params=_cp(),
    )(qa, ka, v, a, d_a, lse, dep)


def _in_bwd(dqa, dka, dv, dyc, fl, x, dh1, g1, w_in_t, tri_u, selq, selk, band_t, dep):
    t = x.shape[0]
    tt = _tile(t)
    nt = t // tt
    hb = tt // HALO
    rev = lambda s: nt - 1 - s

    def body(dqa_ref, dka_ref, dv_ref, dyc_ref, dyn_ref, fl_ref, x_ref, dh1_ref, g_ref, w_ref, tri_ref, sq_ref, sk_ref,
             band_ref, dep_ref, dx_ref, dw_ref, dg_ref, db_ref, carry, acc, sem):
        s = pl.program_id(0)
        i = nt - 1 - s

        @pl.when(s == 0)
        def _():
            carry[...] = jnp.zeros_like(carry)
            dg_ref[...] = jnp.zeros_like(dg_ref)
            db_ref[...] = jnp.zeros_like(db_ref)

        dq_cat = jnp.concatenate([dqa_ref[p] for p in range(NH // 2)], axis=1)
        dk_cat = dka_ref[...]
        dc = _dot2t(dq_cat, sq_ref[...]) - _dot2t(dk_cat, sk_ref[...])
        dlf = carry[...] + _dot3(tri_ref[...], dc)
        carry[...] = carry[...] + jnp.sum(dc, axis=0, keepdims=True)
        flv = fl_ref[...]
        lane = lax.broadcasted_iota(jnp.int32, flv.shape, 1)
        d_fl = jnp.where(lane < NH, dlf / (1.0 + jnp.exp(flv)), 0.0)
        db_ref[...] = db_ref[...] + jnp.sum(d_fl, axis=0, keepdims=True)
        low = lax.broadcasted_iota(jnp.int32, (tt, LANE), 1) < HD
        dqs, dks = [], []
        for p in range(NH // 2):
            b0, b1 = slice(2 * LANE * p, 2 * LANE * p + LANE), slice(2 * LANE * p + LANE, 2 * LANE * (p + 1))
            dqs.append(jnp.where(low, dq_cat[:, b0], pltpu.roll(dq_cat[:, b1], HD, 1)) * (1.0 / math.sqrt(HD)))
            dks.append(jnp.where(low, dk_cat[:, b0], pltpu.roll(dk_cat[:, b1], HD, 1)))
        nxt = dyn_ref[...] * jnp.where(i < nt - 1, 1.0, 0.0)
        tok = i * tt + lax.broadcasted_iota(jnp.int32, (tt, PC), 0)
        dus = []
        for g, w in enumerate(WINS):
            sl = slice(PC * g, PC * (g + 1))
            dycg = dyc_ref[:, sl]
            ext = jnp.concatenate([dycg, nxt[:, sl]], axis=0)
            dus.append(_dot2(band_ref[g], ext) - dycg * jnp.minimum(tok + 1, w).astype(F32))
        d_z = jnp.concatenate(dqs + dks + [dv_ref[...], d_fl] + dus, axis=1).astype(BF)
        xv = x_ref[...]
        gv = g_ref[...]
        hn = (xv * _rstd(xv) * gv).astype(BF)
        d_hn = jnp.dot(d_z, w_ref[...], preferred_element_type=F32)
        contrib = _mm_tn(d_z, hn)

        @pl.when(s == 0)
        def _():
            acc[...] = contrib

        @pl.when(s > 0)
        def _():
            acc[...] = acc[...] + contrib

        d1, dg = _rms_bwd(d_hn, xv, gv)
        dg_ref[...] = dg_ref[...] + dg
        dx_ref[...] = dh1_ref[...] + d1

        @pl.when(s == nt - 1)
        def _():
            cp = pltpu.make_async_copy(acc, dw_ref, sem)
            cp.start()
            cp.wait()

    row = lambda w: pl.BlockSpec((tt, w), lambda s: (rev(s), 0))
    return pl.pallas_call(
        body, name="in_bwd", grid=(nt,),
        in_specs=[pl.BlockSpec((NH // 2, tt, 2 * LANE), lambda s: (0, rev(s), 0)), row(NH * LANE), row(DA), row(DP),
                  pl.BlockSpec((HALO, DP), lambda s: (jnp.minimum((rev(s) + 1) * hb, nt * hb - 1), 0)),
                  row(LANE), row(D), row(D), _full((1, D)), _resident((ZW, D)), _full((tt, tt)),
                  _full((NH * LANE, LANE)), _full((NH * LANE, LANE)), _full((len(WINS), tt, tt + HALO)), ANY],
        out_specs=[row(D), pl.BlockSpec(memory_space=pl.ANY), _full((1, D)), _full((1, LANE))],
        out_shape=[jax.ShapeDtypeStruct((t, D), F32), jax.ShapeDtypeStruct((ZW, D), F32),
                   jax.ShapeDtypeStruct((1, D), F32), jax.ShapeDtypeStruct((1, LANE), F32)],
        scratch_shapes=[pltpu.VMEM((1, LANE), F32), pltpu.VMEM((ZW, D), F32), pltpu.SemaphoreType.DMA],
        compiler_params=_cp(),
    )(dqa, dka, dv, dyc, dyc, fl, x, dh1, g1, w_in_t, tri_u, selq, selk, band_t, dep)


def _dot2t(x, sel):
    hi, lo = _split2(x)
    return jnp.dot(hi, sel, preferred_element_type=F32) + jnp.dot(lo, sel, preferred_element_type=F32)


class _NoComm:
    def __init__(self, w2):
        self.w2 = w2
        self.dep = jnp.zeros((8, LANE), F32)

    def weights2(self, after):
        return self.w2

    def after_ffn(self, grads2):
        self.grads2 = grads2
        return self.dep

    def after_mix(self, after):
        return self.dep

    def after_attn(self, after):
        return self.dep


def _local_step(x, p, tgt, sm, w1, comm):
    w_in_t, w_out = w1
    tt = _tile(x.shape[0])
    eq, ek, rowq, rowk, selq, selk = _aug_consts()
    b_pad = jnp.pad(sm["b_forget"], ((0, 0), (0, LANE - NH)))
    qa, ka, v, u, fl = _in_proj(x, sm["g_mix_pre"], w_in_t, b_pad, _tri(tt, False), eq, ek, rowq, rowk, comm.dep)
    a, lse = _attn_fwd(qa, ka, v)
    wg_t, wu_t, wd, w_ple, w_pg = comm.weights2(a)
    yb, m, o, h1 = _mix_out(a, u, x, _band(tt, False), sm["w_pool"], sm["pool_scale"], sm["g_attn_grp"],
                            sm["g_pool_grp"], w_out, sm["g_mix_post"])
    hn2, gate, up, dff, dh2, loss, dwpg, dwple, dgple, dgfpost = _ffn_fwd(
        h1, sm["g_ffn_pre"], wg_t, wu_t, wd, sm["g_ffn_post"], p, w_ple, sm["g_ple"], w_pg, tgt)
    dwg_t, dwu_t, dwd, d_hn2 = _ffn_bwd(hn2, gate, up, dff, wg_t, wu_t, wd)
    dep = comm.after_ffn((dwg_t, dwu_t, dwd, dwple, dwpg))
    dh1, d_a, dyc, dgfpre, dgpost, dgattn, dgpool, dps, dwpool, dwout = _mix_bwd(
        d_hn2, dh2, h1, o, a, m, yb, sm["g_ffn_pre"], sm["g_mix_post"], sm["g_attn_grp"], sm["g_pool_grp"],
        w_out, sm["w_pool"], sm["pool_scale"], dep)
    dqa, dka, dvv = _attn_bwd(qa, ka, v, a, d_a, lse, comm.after_mix(dh1))
    dx, dwin_t, dg1, dbf = _in_bwd(dqa, dka, dvv, dyc, fl, x, dh1, sm["g_mix_pre"], w_in_t, _tri(tt, True),
                                   selq, selk, _band(tt, True), comm.after_attn(dvv))
    small = dict(loss=loss[0:1, 0:1], g_mix_pre=dg1, b_forget=dbf[:, :NH], g_attn_grp=dgattn, g_pool_grp=dgpool,
                 w_pool=dwpool, pool_scale=dps, g_mix_post=dgpost, g_ffn_pre=dgfpre, g_ffn_post=dgfpost, g_ple=dgple)
    return dx, (dwin_t, dwout), small


def _place():
    x, y, c = lax.axis_index("x"), lax.axis_index("y"), lax.axis_index("c")
    return x, y, c, [(1 - x, y), (x, 1 - y), (1 - x, 1 - y)]


def _rows(c, h):
    return pl.ds(pl.multiple_of(c * h, 16), h)


def _plan_gather(h):
    def plan(src, land):
        x, y, c, chips = _place()
        return [(src.at[_rows(c, h), :], land.at[2 * x + y, _rows(c, h), :], (cx, cy, c),
                 land.at[2 * cx + cy, _rows(c, h), :]) for cx, cy in chips]
    return plan


def _plan_forward(h):
    def plan(land_in, own, land):
        x, y, c, chips = _place()
        sib, me = (x, y, 1 - c), 2 * x + y
        return ([(land_in.at[2 * cx + cy, _rows(c, h), :], land.at[2 * cx + cy, _rows(c, h), :], sib,
                  land.at[2 * cx + cy, _rows(1 - c, h), :]) for cx, cy in chips]
                + [(own, land.at[me], sib, land.at[me])])
    return plan


def _plan_swap_halves(h):
    def plan(buf_in, buf):
        x, y, c, _ = _place()
        return [(buf_in.at[_rows(c, h), :], buf.at[_rows(c, h), :], (x, y, 1 - c), buf.at[_rows(1 - c, h), :])]
    return plan


def _plan_pair_rows(h):
    def plan(src, land):
        x, y, c, _ = _place()
        return [(src.at[:, _rows(1 - c, h), :], land, (x, y, 1 - c), land)]
    return plan


def _plan_scatter(src, land):
    x, y, c, chips = _place()
    return [(src.at[2 * cx + cy], land.at[k], (cx, cy, c), land.at[k]) for k, (cx, cy) in enumerate(chips)]


def _remote(src, dst, send_sems, recv_sems, k, peer):
    return pltpu.make_async_remote_copy(src_ref=src, dst_ref=dst, send_sem=send_sems.at[k], recv_sem=recv_sems.at[k],
                                        device_id=peer, device_id_type=MESH)


def _exchange(name, n, src, land, plan):
    def body(src_ref, land_ref, send_sems, recv_sems):
        copies = plan(src_ref, land_ref)
        for k, (s, d, peer, _) in enumerate(copies):
            _remote(s, d, send_sems, recv_sems, k, peer).start()
        for k, (s, _, peer, mine) in enumerate(copies):
            _remote(s, mine, send_sems, recv_sems, k, peer).wait_recv()
        for k, (s, d, peer, _) in enumerate(copies):
            _remote(s, d, send_sems, recv_sems, k, peer).wait_send()

    return pl.pallas_call(
        body, name=name, in_specs=[ANY], out_specs=ANY, out_shape=land,
        scratch_shapes=[pltpu.SemaphoreType.DMA((n,)), pltpu.SemaphoreType.DMA((n,))],
    )(src)


def _exchange_inplace(name, n, buf, extra, plan):
    def body(*refs):
        ins, buf_ref, send_sems, recv_sems = refs[:1 + len(extra)], refs[1 + len(extra)], refs[-2], refs[-1]
        copies = plan(*ins, buf_ref)
        for k, (s, d, peer, _) in enumerate(copies):
            _remote(s, d, send_sems, recv_sems, k, peer).start()
        for k, (s, _, peer, mine) in enumerate(copies):
            _remote(s, mine, send_sems, recv_sems, k, peer).wait_recv()
        for k, (s, d, peer, _) in enumerate(copies):
            _remote(s, d, send_sems, recv_sems, k, peer).wait_send()

    return pl.pallas_call(
        body, name=name, in_specs=[ANY] * (1 + len(extra)), out_specs=ANY, out_shape=_sds(buf.shape, buf.dtype),
        input_output_aliases={0: 0},
        scratch_shapes=[pltpu.SemaphoreType.DMA((n,)), pltpu.SemaphoreType.DMA((n,))],
    )(buf, *extra)


HBM = pl.BlockSpec(memory_space=pltpu.HBM)
SEM = pl.BlockSpec(memory_space=pltpu.SEMAPHORE)
EFFECT = pltpu.SideEffectType.DATAFLOW_SIDE_EFFECTING


def _exchange_start(name, n, src, land, plan):
    def body(src_ref, land_ref, send_sems, recv_sems, src_thru, land_thru, token):
        for k, (s, d, peer, _) in enumerate(plan(src_ref, land_ref)):
            _remote(s, d, send_sems, recv_sems, k, peer).start()
        token[...] = jnp.zeros_like(token)

    return pl.pallas_call(
        body, name=name,
        out_shape=(pltpu.SemaphoreType.DMA((n,)), pltpu.SemaphoreType.DMA((n,)), pltpu.HBM(src.shape, src.dtype),
                   pltpu.HBM(land.shape, land.dtype), jax.ShapeDtypeStruct((8, LANE), F32)),
        in_specs=(HBM, HBM), out_specs=(SEM, SEM, HBM, HBM, pl.BlockSpec(memory_space=pltpu.VMEM)),
        input_output_aliases={0: 2, 1: 3},
        compiler_params=pltpu.CompilerParams(has_side_effects=EFFECT),
    )(pltpu.with_memory_space_constraint(src, pltpu.HBM), pltpu.with_memory_space_constraint(land, pltpu.HBM))


def _exchange_wait(name, started, plan, after):
    send_sems, recv_sems, src, land, _ = started

    def body(src_ref, land_ref, send_sems, recv_sems, after_ref, src_out, land_out):
        for k, (s, _, peer, mine) in enumerate(plan(src_ref, land_ref)):
            cp = _remote(s, mine, send_sems, recv_sems, k, peer)
            cp.wait_send()
            cp.wait_recv()

    return pl.pallas_call(
        body, name=name, out_shape=(pltpu.HBM(src.shape, src.dtype), pltpu.HBM(land.shape, land.dtype)),
        in_specs=(HBM, HBM, SEM, SEM, ANY), out_specs=(HBM, HBM), input_output_aliases={0: 0, 1: 1},
        compiler_params=pltpu.CompilerParams(has_side_effects=EFFECT),
    )(src, land, send_sems, recv_sems, after)


def _pair_sum(name, cidx, g, recv, br):
    h = recv.shape[1]
    nb = h // br

    def body(c_ref, g_ref, r_ref, out_ref):
        out_ref[...] = (g_ref[...] + r_ref[...]).astype(BF)

    return pl.pallas_call(
        body, name=name,
        grid_spec=pltpu.PrefetchScalarGridSpec(
            num_scalar_prefetch=1, grid=(NSHARD, nb),
            in_specs=[pl.BlockSpec((1, br, D), lambda s, i, c: (s, c[0] * nb + i, 0)),
                      pl.BlockSpec((1, br, D), lambda s, i, c: (s, i, 0))],
            out_specs=pl.BlockSpec((1, br, D), lambda s, i, c: (s, i, 0))),
        out_shape=jax.ShapeDtypeStruct((NSHARD, h, D), BF),
    )(cidx, g, recv)


def _chip_sum(name, place, pb, y, br):
    h = y.shape[1]
    nb = h // br

    def body(pl_ref, p_ref, y_ref, out_ref):
        acc = p_ref[0].astype(F32)
        for k in range(NSHARD - 1):
            acc = acc + y_ref[k].astype(F32)
        out_ref[...] = acc

    return pl.pallas_call(
        body, name=name,
        grid_spec=pltpu.PrefetchScalarGridSpec(
            num_scalar_prefetch=1, grid=(nb,),
            in_specs=[pl.BlockSpec((1, br, D), lambda i, s: (s[0], i, 0)),
                      pl.BlockSpec((NSHARD - 1, br, D), lambda i, s: (0, i, 0))],
            out_specs=pl.BlockSpec((br, D), lambda i, s: (s[1] * nb + i, 0))),
        out_shape=jax.ShapeDtypeStruct((2 * h, D), F32),
    )(place, pb, y)


def _all_reduce_small(v):
    rows = v.shape[0]

    def body(in_ref, out_ref, buf, send_sems, recv_sems):
        x, y, c, _ = _place()
        me = 4 * x + 2 * y + c
        flips = [(r >> 2 & 1, r >> 1 & 1, r & 1) for r in range(1, 8)]

        def peer(f):
            return tuple(1 - a if b else a for a, b in zip((x, y, c), f))

        def copy(k, slot, to):
            return pltpu.make_async_remote_copy(src_ref=in_ref, dst_ref=buf.at[slot], send_sem=send_sems.at[k],
                                                recv_sem=recv_sems.at[k], device_id=to, device_id_type=MESH)

        sends = [copy(k, me, peer(f)) for k, f in enumerate(flips)]
        for cp in sends:
            cp.start()
        buf[me] = in_ref[...]
        for k, f in enumerate(flips):
            px, py, pc = peer(f)
            copy(k, 4 * px + 2 * py + pc, peer(f)).wait_recv()
        for cp in sends:
            cp.wait_send()
        acc = buf[0]
        for k in range(1, 8):
            acc = acc + buf[k]
        out_ref[...] = acc

    vm = pl.BlockSpec(memory_space=pltpu.VMEM)
    return pl.pallas_call(
        body, name="all_reduce_small", in_specs=[vm], out_specs=vm,
        out_shape=jax.ShapeDtypeStruct(v.shape, F32),
        scratch_shapes=[pltpu.VMEM((8, rows, LANE), F32), pltpu.SemaphoreType.DMA((7,)), pltpu.SemaphoreType.DMA((7,))],
    )(v)


def _adamw_math(w, g, m, v):
    m = ADAM_B1 * m + (1.0 - ADAM_B1) * g
    v = ADAM_B2 * v + (1.0 - ADAM_B2) * (g * g)
    m_hat = m / (1.0 - ADAM_B1 ** ADAM_STEP)
    v_hat = v / (1.0 - ADAM_B2 ** ADAM_STEP)
    delta = -ADAM_LR * (m_hat / (jnp.sqrt(v_hat) + ADAM_EPS) + ADAM_WD * w)
    return delta, m, v


def _adamw(w, g, m, v):
    r, c = w.shape
    br = next(b for b in (256, 176, r) if r % b == 0)

    def body(w_ref, g_ref, m_ref, v_ref, d_ref, nm_ref, nv_ref):
        d_ref[...], nm_ref[...], nv_ref[...] = _adamw_math(w_ref[...], g_ref[...], m_ref[...], v_ref[...])

    spec = pl.BlockSpec((br, c), lambda i: (i, 0))
    return pl.pallas_call(
        body, name="adamw", grid=(r // br,), in_specs=[spec] * 4, out_specs=[spec] * 3,
        out_shape=[jax.ShapeDtypeStruct((r, c), F32)] * 3,
    )(w, g, m, v)


def _adamw_small(ws, gs, ms, vs):
    n = len(ws)

    def body(*refs):
        ins, outs = refs[:4 * n], refs[4 * n:]
        for k in range(n):
            d, m, v = _adamw_math(ins[k][...], ins[n + k][...], ins[2 * n + k][...], ins[3 * n + k][...])
            outs[k][...] = d
            outs[n + k][...] = m
            outs[2 * n + k][...] = v

    vm = pl.BlockSpec(memory_space=pltpu.VMEM)
    out = pl.pallas_call(
        body, name="adamw_small", in_specs=[vm] * (4 * n), out_specs=[vm] * (3 * n),
        out_shape=[jax.ShapeDtypeStruct(w.shape, F32) for w in ws] * 3,
    )(*ws, *gs, *ms, *vs)
    return out[:n], out[n:2 * n], out[2 * n:]


BIG = ("w_in", "w_out", "w_ffn_gate", "w_ffn_up", "w_ffn_down", "w_ple_proj", "w_ple_gate")
SMALL = ("g_mix_pre", "b_forget", "g_attn_grp", "g_pool_grp", "w_pool", "pool_scale", "g_mix_post", "g_ffn_pre",
         "g_ffn_post", "g_ple")
VECTORS = tuple(n for n in SMALL if n != "w_pool")
ORDER = ("g_mix_pre", "w_in", "b_forget", "g_attn_grp", "g_pool_grp", "w_pool", "pool_scale", "w_out", "g_mix_post",
         "g_ffn_pre", "w_ffn_gate", "w_ffn_up", "w_ffn_down", "g_ffn_post", "w_ple_proj", "g_ple", "w_ple_gate")


def _pad_rows(a, rows):
    return jnp.pad(a, ((0, rows - a.shape[0]), (0, 0)))


def _stack1(w_in, w_out):
    return _pad_rows(jnp.concatenate([_pad_rows(w_in.T, IN_PAD), w_out], axis=0), ROWS1)


def _stack2(wg, wu, wd, wple, wpg):
    return _pad_rows(jnp.concatenate([wg.T, wu.T, wd, wple.reshape(DPLE // NSHARD, D), wpg], axis=0), ROWS2)


def _unstack1(s):
    return s[:IN_SH].T, s[O1_OUT:USED1]


def _unstack2(s):
    return s[:O2_U].T, s[O2_U:O2_D].T, s[O2_D:O2_PLE], s[O2_PLE:O2_PG].reshape(DPLE, DPLE), s[O2_PG:USED2]


def _cat(g, lo, hi):
    return g[:, lo:hi].reshape(NSHARD * (hi - lo), D)


def _unstack1_full(g):
    w_in_t = _cat(g, 0, IN_SH)
    w_in_t = jnp.concatenate([w_in_t[:3 * DA], _pad_rows(w_in_t[3 * DA:3 * DA + NH], LANE), w_in_t[3 * DA + NH:]], axis=0)
    return w_in_t, _cat(g, O1_OUT, USED1)


def _unstack2_full(g):
    w_ple = g[:, O2_PLE:O2_PG].reshape(NSHARD, DPLE, DPLE).transpose(1, 0, 2).reshape(DPLE, D)
    return _cat(g, 0, O2_U), _cat(g, O2_U, O2_D), _cat(g, O2_D, O2_PLE), w_ple, _cat(g, O2_PG, USED2)


def _shards(a):
    return a.reshape(NSHARD, a.shape[0] // NSHARD, D)


def _stack1_full(dwin_t, dwout):
    dwin_t = jnp.concatenate([dwin_t[:3 * DA + NH], dwin_t[3 * DA + LANE:]], axis=0).reshape(NSHARD, IN_SH, D)
    zeros = lambda r: jnp.zeros((NSHARD, r, D), F32)
    return jnp.concatenate([dwin_t, zeros(IN_PAD - IN_SH), _shards(dwout), zeros(ROWS1 - USED1)], axis=1)


def _stack2_full(dwg_t, dwu_t, dwd, dwple, dwpg):
    dwple = dwple.reshape(DPLE, NSHARD, DPLE).transpose(1, 0, 2).reshape(NSHARD, DPLE // NSHARD, D)
    return jnp.concatenate([_shards(dwg_t), _shards(dwu_t), _shards(dwd), dwple, _shards(dwpg),
                            jnp.zeros((NSHARD, ROWS2 - USED2, D), F32)], axis=1)


def _sds(shape, dtype):
    return jax.ShapeDtypeStruct(shape, dtype)


class _Comm:
    def __init__(self, stack2, me, c):
        self.stack2, self.me, self.c = stack2, me, c
        self.cidx = c.astype(jnp.int32).reshape(1)
        self.place = jnp.stack([me, c]).astype(jnp.int32)
        self.h = ROWS2 // 2
        self.gather = _exchange_start("gather2_start", 3, stack2, lax.empty((NSHARD, ROWS2, D), BF), _plan_gather(self.h))
        self.dep = self.gather[4]

    def weights2(self, after):
        own, land = _exchange_wait("gather2_wait", self.gather, _plan_gather(self.h), after)
        return _unstack2_full(_exchange_inplace("gather2_forward", 4, land, (own,), _plan_forward(self.h)))

    def after_ffn(self, grads2):
        g = _stack2_full(*grads2)
        self.pair = _exchange_start("reduce2_pair_start", 1, g, lax.empty((NSHARD, self.h, D), F32),
                                    _plan_pair_rows(self.h))
        return self.pair[4]

    def after_mix(self, after):
        g, recv = _exchange_wait("reduce2_pair_wait", self.pair, _plan_pair_rows(self.h), after)
        pb = _pair_sum("pair_sum2", self.cidx, g, recv, RED2)
        self.chip = _exchange_start("reduce2_chip_start", 3, pb, lax.empty((NSHARD - 1, self.h, D), BF), _plan_scatter)
        return self.chip[4]

    def after_attn(self, after):
        pb, y = _exchange_wait("reduce2_chip_wait", self.chip, _plan_scatter, after)
        f = _chip_sum("chip_sum2", self.place, pb, y, RED2)
        self.reduced2 = _exchange_inplace("reduce2_gather", 1, f, (), _plan_swap_halves(self.h))
        return self.reduced2


def _pack_small(small):
    parts = []
    for name in ("loss",) + SMALL:
        flat = small[name].reshape(-1)
        parts.append(jnp.pad(flat, (0, -flat.shape[0] % LANE)).reshape(-1, LANE))
    v = jnp.concatenate(parts, axis=0)
    return _pad_rows(v, v.shape[0] + (-v.shape[0] % 8))


def _unpack_small(v, shapes):
    out, r = {}, 0
    for name in ("loss",) + SMALL:
        n = math.prod(shapes[name])
        rows = -(-n // LANE)
        out[name] = v[r:r + rows].reshape(-1)[:n].reshape(shapes[name])
        r += rows
    return out


def kernel(x, p, g_mix_pre, w_in, b_forget, g_attn_grp, g_pool_grp, w_pool, pool_scale, w_out, g_mix_post, g_ffn_pre, w_ffn_gate, w_ffn_up, w_ffn_down, g_ffn_post, w_ple_proj, g_ple, w_ple_gate, loss_target, m_g_mix_pre, m_w_in, m_b_forget, m_g_attn_grp, m_g_pool_grp, m_w_pool, m_pool_scale, m_w_out, m_g_mix_post, m_g_ffn_pre, m_w_ffn_gate, m_w_ffn_up, m_w_ffn_down, m_g_ffn_post, m_w_ple_proj, m_g_ple, m_w_ple_gate, v_g_mix_pre, v_w_in, v_b_forget, v_g_attn_grp, v_g_pool_grp, v_w_pool, v_pool_scale, v_w_out, v_g_mix_post, v_g_ffn_pre, v_w_ffn_gate, v_w_ffn_up, v_w_ffn_down, v_g_ffn_post, v_w_ple_proj, v_g_ple, v_w_ple_gate):
    args = dict(locals())
    strip = lambda n, a: a if n in VECTORS else a[0]
    w = {n: strip(n, args[n]) for n in ORDER}
    mom = {n: strip(n, args["m_" + n]) for n in ORDER}
    var = {n: strip(n, args["v_" + n]) for n in ORDER}
    sm = {n: w[n] for n in SMALL}

    c = lax.axis_index("c")
    me = 2 * lax.axis_index("x") + lax.axis_index("y")
    h1 = ROWS1 // 2
    bf = lambda n: w[n].astype(BF)
    comm = _Comm(_stack2(*[bf(n) for n in BIG[2:]]), me, c)
    stack1 = _stack1(bf("w_in"), bf("w_out"))
    land = _exchange("gather1", 3, stack1, _sds((NSHARD, ROWS1, D), BF), _plan_gather(h1))
    w1 = _unstack1_full(_exchange_inplace("gather1_forward", 4, land, (stack1,), _plan_forward(h1)))
    dx, grads1, small = _local_step(x[0], p[0, 0], loss_target[0], sm, w1, comm)

    shapes = {n: small[n].shape for n in small}
    red_small = _unpack_small(_all_reduce_small(_pack_small(small)), shapes)
    loss = 0.5 / D * red_small["loss"][0, 0]

    g1 = _stack1_full(*grads1)
    recv = _exchange("reduce1_pair", 1, g1, _sds((NSHARD, h1, D), F32), _plan_pair_rows(h1))
    pb = _pair_sum("pair_sum1", comm.cidx, g1, recv, RED1)
    y = _exchange("reduce1_chip", 3, pb, _sds((NSHARD - 1, h1, D), BF), _plan_scatter)
    f = _chip_sum("chip_sum1", comm.place, pb, y, RED1)
    reduced1 = _exchange_inplace("reduce1_gather", 1, f, (), _plan_swap_halves(h1))
    grads = dict(zip(BIG, _unstack1(reduced1) + _unstack2(comm.reduced2)))
    for n in SMALL:
        grads[n] = red_small[n].reshape(w[n].shape)

    delta, new_m, new_v = {}, {}, {}
    for n in BIG:
        delta[n], new_m[n], new_v[n] = _adamw(w[n], grads[n], mom[n], var[n])
    two_d = lambda a: a.reshape(-1, a.shape[-1])
    ds, ms, vs = _adamw_small([two_d(w[n]) for n in SMALL], [two_d(grads[n]) for n in SMALL],
                              [two_d(mom[n]) for n in SMALL], [two_d(var[n]) for n in SMALL])
    for k, n in enumerate(SMALL):
        delta[n], new_m[n], new_v[n] = ds[k].reshape(w[n].shape), ms[k].reshape(w[n].shape), vs[k].reshape(w[n].shape)

    lead = lambda d: [d[n] if n in VECTORS else d[n][None] for n in ORDER]
    return (loss, dx[None], *lead(grads), *lead(delta), *lead(new_m), *lead(new_v))
```

```python
import functools
import math

import jax
import jax.numpy as jnp
import numpy as np
from jax import lax
from jax.experimental import pallas as pl
from jax.experimental.pallas import tpu as pltpu

F32 = jnp.float32
BF = jnp.bfloat16
MESH = pl.DeviceIdType.MESH

D = 1024
DA = 512
DP = 512
NH = 8
HD = 64
DFF = 2816
DPLE = 256
WINS = (2, 4, 8, 16)
PC = 128
ZW = 3 * DA + 128 + DP
EPS = 1e-6
NSHARD = 4

LANE = 128
HALO = 128

IN_SH = 514
IN_PAD = 528
FF_SH = DFF // NSHARD
O1_OUT, USED1, ROWS1 = 528, 784, 800
O2_U, O2_D, O2_PLE, O2_PG, USED2, ROWS2 = 704, 1408, 2112, 2176, 2432, 2560
RED1, RED2 = 400, 128

ADAM_LR, ADAM_B1, ADAM_B2, ADAM_EPS, ADAM_WD, ADAM_STEP = 0.001, 0.9, 0.999, 1e-8, 0.01, 10

VMEM_LIMIT = 56 * 1024 * 1024


def _cp(**kw):
    return pltpu.CompilerParams(vmem_limit_bytes=VMEM_LIMIT, **kw)


def _mm(a, b):
    return jnp.dot(a.astype(BF), b.astype(BF), preferred_element_type=F32)


def _mm_nt(a, b):
    return lax.dot_general(a.astype(BF), b.astype(BF), (((1,), (1,)), ((), ())), preferred_element_type=F32)


def _mm_tn(a, b):
    return lax.dot_general(a.astype(BF), b.astype(BF), (((0,), (0,)), ((), ())), preferred_element_type=F32)


def _split2(x):
    hi = x.astype(BF)
    lo = (x - hi.astype(F32)).astype(BF)
    return hi, lo


def _split3(x):
    hi = x.astype(BF)
    r = x - hi.astype(F32)
    mid = r.astype(BF)
    lo = (r - mid.astype(F32)).astype(BF)
    return hi, mid, lo


def _dot3(m, x):
    hi, mid, lo = _split3(x)
    return (jnp.dot(m, hi, preferred_element_type=F32) + jnp.dot(m, mid, preferred_element_type=F32)
            + jnp.dot(m, lo, preferred_element_type=F32))


def _dot2(m, x):
    hi, lo = _split2(x)
    return jnp.dot(m, hi, preferred_element_type=F32) + jnp.dot(m, lo, preferred_element_type=F32)


def _rstd(x):
    return lax.rsqrt(jnp.mean(x * x, axis=-1, keepdims=True) + EPS)


def _rms_bwd(dy, x, g):
    r = _rstd(x)
    xh = x * r
    dg = jnp.sum(dy * xh, axis=0, keepdims=True)
    dxh = dy * g
    dx = r * (dxh - xh * jnp.mean(dxh * xh, axis=-1, keepdims=True))
    return dx, dg


def _sigmoid(x):
    return 1.0 / (1.0 + jnp.exp(-x))


ANY = pl.BlockSpec(memory_space=pl.ANY)


def _full(shape):
    n = len(shape)
    return pl.BlockSpec(shape, lambda *_: (0,) * n)


def _resident(shape):
    n = len(shape)
    return pl.BlockSpec(shape, lambda *_: (0,) * n, pipeline_mode=pl.Buffered(1))


def _tile(t):
    return 512 if t % 512 == 0 else t


def _tri(n, upper):
    r, c = np.indices((n, n))
    return ((c >= r) if upper else (c <= r)).astype(BF)


def _band(tt, transpose):
    r, c = np.indices((tt, tt + HALO))
    d = (c - r) if transpose else (r + HALO - c)
    return np.stack([((d >= 0) & (d < w)).astype(BF) for w in WINS])


def _aug_consts():
    row, col = np.indices((3 * LANE, NH * LANE))
    piece, head = row // LANE, row % LANE
    ch, cl = col // LANE, col % LANE
    eq = ((head == ch) & (cl == HD + piece)).astype(BF)
    ek = -((head == ch) & (cl == HD + 3 + piece)).astype(BF)
    lane = np.arange(NH * LANE)[None, :] % LANE
    rowq = ((lane >= HD + 3) & (lane < HD + 6)).astype(np.float32)
    rowk = ((lane >= HD) & (lane < HD + 3)).astype(np.float32)
    r2, c2 = np.indices((NH * LANE, LANE))
    selq = ((r2 // LANE == c2) & (r2 % LANE == HD)).astype(BF)
    selk = ((r2 // LANE == c2) & (r2 % LANE == HD + 3)).astype(BF)
    return eq, ek, rowq, rowk, selq, selk


def _in_proj(x, g1, w_in_t, b_pad, tri, eq, ek, rowq, rowk, dep):
    t = x.shape[0]
    tt = _tile(t)

    def body(x_ref, g_ref, w_ref, b_ref, tri_ref, eq_ref, ek_ref, rq_ref, rk_ref, dep_ref,
             qa_ref, ka_ref, v_ref, u_ref, fl_ref, carry):
        i = pl.program_id(0)

        @pl.when(i == 0)
        def _():
            carry[...] = jnp.zeros_like(carry)

        xv = x_ref[...]
        hn = (xv * _rstd(xv) * g_ref[...]).astype(BF)
        z = _mm_nt(hn, w_ref[...])
        fl = z[:, 3 * DA:3 * DA + LANE] + b_ref[...]
        lane = lax.broadcasted_iota(jnp.int32, fl.shape, 1)
        lf = jnp.where(lane < NH, jnp.minimum(fl, 0.0) - jnp.log(1.0 + jnp.exp(-jnp.abs(fl))), 0.0)
        c = carry[...] + _dot3(tri_ref[...], lf)
        carry[...] = carry[...] + jnp.sum(lf, axis=0, keepdims=True)
        caug = jnp.concatenate(_split3(c), axis=1)
        aug_q = jnp.dot(caug, eq_ref[...], preferred_element_type=F32) + rq_ref[...]
        aug_k = jnp.dot(caug, ek_ref[...], preferred_element_type=F32) + rk_ref[...]
        low = lax.broadcasted_iota(jnp.int32, (tt, LANE), 1) < HD
        for p in range(NH // 2):
            qp = z[:, LANE * p:LANE * (p + 1)] * (1.0 / math.sqrt(HD))
            kp = z[:, DA + LANE * p:DA + LANE * (p + 1)]
            for h, (qh, kh) in enumerate(((qp, kp), (pltpu.roll(qp, HD, 1), pltpu.roll(kp, HD, 1)))):
                lo_, hi_ = LANE * (2 * p + h), LANE * (2 * p + h + 1)
                qa_ref[:, lo_:hi_] = jnp.where(low, qh, aug_q[:, lo_:hi_]).astype(BF)
                ka_ref[:, lo_:hi_] = jnp.where(low, kh, aug_k[:, lo_:hi_]).astype(BF)
        v_ref[...] = z[:, 2 * DA:3 * DA].astype(BF)
        u_ref[...] = z[:, 3 * DA + LANE:]
        fl_ref[...] = fl

    return pl.pallas_call(
        body, name="in_proj", grid=(t // tt,),
        in_specs=[pl.BlockSpec((tt, D), lambda i: (i, 0)), _full((1, D)), _resident((ZW, D)), _full((1, LANE)),
                  _full((tt, tt)), _full((3 * LANE, NH * LANE)), _full((3 * LANE, NH * LANE)),
                  _full((1, NH * LANE)), _full((1, NH * LANE)), ANY],
        out_specs=[pl.BlockSpec((tt, NH * LANE), lambda i: (i, 0)), pl.BlockSpec((tt, NH * LANE), lambda i: (i, 0)),
                   pl.BlockSpec((tt, DA), lambda i: (i, 0)), pl.BlockSpec((tt, DP), lambda i: (i, 0)),
                   pl.BlockSpec((tt, LANE), lambda i: (i, 0))],
        out_shape=[jax.ShapeDtypeStruct((t, NH * LANE), BF), jax.ShapeDtypeStruct((t, NH * LANE), BF),
                   jax.ShapeDtypeStruct((t, DA), BF), jax.ShapeDtypeStruct((t, DP), F32),
                   jax.ShapeDtypeStruct((t, LANE), F32)],
        scratch_shapes=[pltpu.VMEM((1, LANE), F32)],
        compiler_params=_cp(),
    )(x, g1, w_in_t, b_pad, tri, eq, ek, rowq, rowk, dep)


def _attn_fwd(qa, ka, v):
    t = qa.shape[0]
    ta = _tile(t)
    n = t // ta

    def body(q_ref, k_ref, v_ref, a_ref, lse_ref, m_ref, l_ref, acc_ref):
        i = pl.program_id(1)
        m_ref[...] = jnp.full_like(m_ref, -1e30)
        l_ref[...] = jnp.zeros_like(l_ref)
        acc_ref[...] = jnp.zeros_like(acc_ref)
        qs = [q_ref[:, LANE * h:LANE * (h + 1)] for h in range(2)]
        reps = ta // LANE

        def tile(j, masked):
            rows = pl.ds(pl.multiple_of(j * ta, ta), ta)
            v2 = v_ref[rows, :]
            s = [_mm_nt(qs[h], k_ref[rows, LANE * h:LANE * (h + 1)]) for h in range(2)]
            if masked:
                keep = (lax.broadcasted_iota(jnp.int32, (ta, ta), 1) <= lax.broadcasted_iota(jnp.int32, (ta, ta), 0))
                s = [jnp.where(keep, sh, -1e30) for sh in s]
            m_old = [m_ref[h] for h in range(2)]
            m_new = [jnp.maximum(m_old[h], jnp.max(s[h], axis=1, keepdims=True)) for h in range(2)]
            pe = [jnp.exp(s[h] - jnp.tile(m_new[h], (1, reps))) for h in range(2)]
            alpha = [jnp.exp(m_old[h] - m_new[h]) for h in range(2)]
            pv = [jnp.dot(pe[h].astype(BF), v2, preferred_element_type=F32) for h in range(2)]
            for h in range(2):
                l_ref[h] = alpha[h] * l_ref[h] + jnp.sum(pe[h], axis=1, keepdims=True)
                acc_ref[h] = alpha[h] * acc_ref[h] + pv[h]
                m_ref[h] = m_new[h]

        def step(j, carry):
            tile(j, False)
            return carry

        lax.fori_loop(0, i, step, 0)
        tile(i, True)
        low = lax.broadcasted_iota(jnp.int32, (ta, LANE), 1) < HD
        a_ref[...] = jnp.where(low, acc_ref[0] / l_ref[0], acc_ref[1] / l_ref[1])
        lse_ref[...] = jnp.where(low, m_ref[0] + jnp.log(l_ref[0]), m_ref[1] + jnp.log(l_ref[1]))

    return pl.pallas_call(
        body, name="attn_fwd", grid=(NH // 2, n),
        in_specs=[pl.BlockSpec((ta, 2 * LANE), lambda p, i: (i, p)),
                  pl.BlockSpec((t, 2 * LANE), lambda p, i: (0, p)),
                  pl.BlockSpec((t, LANE), lambda p, i: (0, p))],
        out_specs=[pl.BlockSpec((ta, LANE), lambda p, i: (i, p)), pl.BlockSpec((ta, LANE), lambda p, i: (i, p))],
        out_shape=[jax.ShapeDtypeStruct((t, DA), F32), jax.ShapeDtypeStruct((t, DA), F32)],
        scratch_shapes=[pltpu.VMEM((2, ta, LANE), F32), pltpu.VMEM((2, ta, LANE), F32), pltpu.VMEM((2, ta, LANE), F32)],
        compiler_params=_cp(),
    )(qa, ka, v)


def _mix_out(a, u, x, band, w_pool, pool_scale, g_attn, g_pool, w_out, g_post):
    t = a.shape[0]
    tt = _tile(t)
    hb = tt // HALO

    def body(a_ref, u_ref, up_ref, x_ref, band_ref, wp_ref, ps_ref, ga_ref, gp_ref, wo_ref, go_ref,
             yb_ref, m_ref, o_ref, h1_ref):
        i = pl.program_id(0)
        prev = up_ref[...] * jnp.where(i > 0, 1.0, 0.0)
        tok = i * tt + lax.broadcasted_iota(jnp.int32, (tt, PC), 0)
        ms = []
        for g, w in enumerate(WINS):
            ug = u_ref[:, PC * g:PC * (g + 1)]
            ext = jnp.concatenate([prev[:, PC * g:PC * (g + 1)], ug], axis=0)
            cnt = jnp.minimum(tok + 1, w).astype(F32)
            y = (_dot2(band_ref[g], ext) / cnt - ug).astype(BF)
            yb_ref[:, PC * g:PC * (g + 1)] = y
            ms.append(_mm(y, wp_ref[g]) * ps_ref[:, PC * g:PC * (g + 1)])
        m = jnp.concatenate(ms, axis=1)
        m_ref[...] = m
        av = a_ref[...]
        mix = jnp.concatenate([av * _rstd(av) * ga_ref[...], m * _rstd(m) * gp_ref[...]], axis=1)
        o = _mm(mix, wo_ref[...])
        o_ref[...] = o
        h1_ref[...] = x_ref[...] + o * _rstd(o) * go_ref[...]

    return pl.pallas_call(
        body, name="mix_out", grid=(t // tt,),
        in_specs=[pl.BlockSpec((tt, DA), lambda i: (i, 0)), pl.BlockSpec((tt, DP), lambda i: (i, 0)),
                  pl.BlockSpec((HALO, DP), lambda i: (jnp.maximum(i * hb - 1, 0), 0)),
                  pl.BlockSpec((tt, D), lambda i: (i, 0)), _full((len(WINS), tt, tt + HALO)),
                  _full((len(WINS), PC, PC)), _full((1, DP)), _full((1, DA)), _full((1, DP)),
                  _resident((D, D)), _full((1, D))],
        out_specs=[pl.BlockSpec((tt, DP), lambda i: (i, 0)), pl.BlockSpec((tt, DP), lambda i: (i, 0)),
                   pl.BlockSpec((tt, D), lambda i: (i, 0)), pl.BlockSpec((tt, D), lambda i: (i, 0))],
        out_shape=[jax.ShapeDtypeStruct((t, DP), BF), jax.ShapeDtypeStruct((t, DP), F32),
                   jax.ShapeDtypeStruct((t, D), F32), jax.ShapeDtypeStruct((t, D), F32)],
        compiler_params=_cp(),
    )(a, u, u, x, band, w_pool, pool_scale, g_attn, g_pool, w_out, g_post)


def _ffn_fwd(h1, g_pre, wg_t, wu_t, wd, g_post, p, w_ple, g_ple, w_pg, tgt):
    t = h1.shape[0]
    tt = 256 if t % 256 == 0 else t

    def body(h1_ref, gpre_ref, wg_ref, wu_ref, wd_ref, gpost_ref, p_ref, wple_ref, gple_ref, wpg_ref, tgt_ref,
             hn_ref, gate_ref, up_ref, dff_ref, dh2_ref, loss_ref, dwpg_ref, dwple_ref, dgple_ref, dgpost_ref):
        i = pl.program_id(0)

        @pl.when(i == 0)
        def _():
            loss_ref[...] = jnp.zeros_like(loss_ref)
            dwpg_ref[...] = jnp.zeros_like(dwpg_ref)
            dwple_ref[...] = jnp.zeros_like(dwple_ref)
            dgple_ref[...] = jnp.zeros_like(dgple_ref)
            dgpost_ref[...] = jnp.zeros_like(dgpost_ref)

        h1v = h1_ref[...]
        hn = (h1v * _rstd(h1v) * gpre_ref[...]).astype(BF)
        hn_ref[...] = hn
        gate = _mm_nt(hn, wg_ref[...])
        up = _mm_nt(hn, wu_ref[...])
        gate_ref[...] = gate.astype(BF)
        up_ref[...] = up.astype(BF)
        ff = _mm(gate * _sigmoid(gate) * up, wd_ref[...])
        rff = _rstd(ff)
        ffh = ff * rff
        gpost = gpost_ref[...]
        h2 = h1v + ffh * gpost
        pv = p_ref[...]
        pe = _mm(pv, wple_ref[...])
        rpe = _rstd(pe)
        peh = pe * rpe
        gple = gple_ref[...]
        e = peh * gple
        sig = _sigmoid(_mm(h2, wpg_ref[...]))
        dv = h2 + sig * e - tgt_ref[...]
        sq = jnp.sum(jnp.sum(dv * dv, axis=1, keepdims=True), axis=0, keepdims=True)
        loss_ref[...] = loss_ref[...] + sq
        dy = dv * (1.0 / D)
        d_e = dy * sig
        d_gl = dy * e * sig * (1.0 - sig)
        dh2 = dy + _mm_nt(d_gl, wpg_ref[...])
        dh2_ref[...] = dh2
        dwpg_ref[...] = dwpg_ref[...] + _mm_tn(h2, d_gl)
        dgple_ref[...] = dgple_ref[...] + jnp.sum(d_e * peh, axis=0, keepdims=True)
        dpeh = d_e * gple
        d_pe = rpe * (dpeh - peh * jnp.mean(dpeh * peh, axis=-1, keepdims=True))
        dwple_ref[...] = dwple_ref[...] + _mm_tn(pv, d_pe)
        dgpost_ref[...] = dgpost_ref[...] + jnp.sum(dh2 * ffh, axis=0, keepdims=True)
        dffh = dh2 * gpost
        dff_ref[...] = (rff * (dffh - ffh * jnp.mean(dffh * ffh, axis=-1, keepdims=True))).astype(BF)

    row = lambda w: pl.BlockSpec((tt, w), lambda i: (i, 0))
    return pl.pallas_call(
        body, name="ffn_fwd", grid=(t // tt,),
        in_specs=[row(D), _full((1, D)), _resident((DFF, D)), _resident((DFF, D)), _resident((DFF, D)), _full((1, D)),
                  row(DPLE), _resident((DPLE, D)), _full((1, D)), _resident((D, D)), row(D)],
        out_specs=[row(D), row(DFF), row(DFF), row(D), row(D), _full((8, LANE)), _full((D, D)), _full((DPLE, D)),
                   _full((1, D)), _full((1, D))],
        out_shape=[jax.ShapeDtypeStruct((t, D), BF), jax.ShapeDtypeStruct((t, DFF), BF), jax.ShapeDtypeStruct((t, DFF), BF),
                   jax.ShapeDtypeStruct((t, D), BF), jax.ShapeDtypeStruct((t, D), F32), jax.ShapeDtypeStruct((8, LANE), F32),
                   jax.ShapeDtypeStruct((D, D), F32), jax.ShapeDtypeStruct((DPLE, D), F32),
                   jax.ShapeDtypeStruct((1, D), F32), jax.ShapeDtypeStruct((1, D), F32)],
        compiler_params=_cp(),
    )(h1, g_pre, wg_t, wu_t, wd, g_post, p, w_ple, g_ple, w_pg, tgt)


def _ffn_bwd(hn2, gate, up, dff, wg_t, wu_t, wd):
    t = hn2.shape[0]
    tt = _tile(t)
    nt = t // tt
    ch = 256
    nc = DFF // ch

    def body(hn_ref, gate_ref, up_ref, dff_ref, wg_ref, wu_ref, wd_ref,
             dwg_ref, dwu_ref, dwd_ref, dhn_ref, acc, sem):
        j, i = pl.program_id(0), pl.program_id(1)
        gate_v = gate_ref[...].astype(F32)
        up_v = up_ref[...].astype(F32)
        dffv = dff_ref[...]
        hn = hn_ref[...]
        sg = _sigmoid(gate_v)
        silu = gate_v * sg
        d_act = _mm_nt(dffv, wd_ref[...])
        d_up = (d_act * silu).astype(BF)
        d_gate = (d_act * up_v * (sg * (1.0 + gate_v * (1.0 - sg)))).astype(BF)

        @pl.when(i == 0)
        def _():
            dwg_ref[...] = jnp.zeros_like(dwg_ref)
            dwu_ref[...] = jnp.zeros_like(dwu_ref)
            dwd_ref[...] = jnp.zeros_like(dwd_ref)

        dwd_ref[...] = dwd_ref[...] + _mm_tn(silu * up_v, dffv)
        dwg_ref[...] = dwg_ref[...] + _mm_tn(d_gate, hn)
        dwu_ref[...] = dwu_ref[...] + _mm_tn(d_up, hn)
        contrib = _mm(d_gate, wg_ref[...]) + _mm(d_up, wu_ref[...])
        rows = pl.ds(pl.multiple_of(i * tt, tt), tt)

        @pl.when(j == 0)
        def _():
            acc[rows, :] = contrib

        @pl.when(j > 0)
        def _():
            acc[rows, :] = acc[rows, :] + contrib

        @pl.when((j == nc - 1) & (i == nt - 1))
        def _():
            cp = pltpu.make_async_copy(acc, dhn_ref, sem)
            cp.start()
            cp.wait()

    tok = lambda w: pl.BlockSpec((tt, w), lambda j, i: (i, 0))
    chunk = pl.BlockSpec((ch, D), lambda j, i: (j, 0))
    return pl.pallas_call(
        body, name="ffn_bwd", grid=(nc, nt),
        in_specs=[tok(D), pl.BlockSpec((tt, ch), lambda j, i: (i, j)), pl.BlockSpec((tt, ch), lambda j, i: (i, j)),
                  tok(D), chunk, chunk, chunk],
        out_specs=[chunk, chunk, chunk, pl.BlockSpec(memory_space=pl.ANY)],
        out_shape=[jax.ShapeDtypeStruct((DFF, D), F32), jax.ShapeDtypeStruct((DFF, D), F32),
                   jax.ShapeDtypeStruct((DFF, D), F32), jax.ShapeDtypeStruct((t, D), F32)],
        scratch_shapes=[pltpu.VMEM((t, D), F32), pltpu.SemaphoreType.DMA],
        compiler_params=_cp(),
    )(hn2, gate, up, dff, wg_t, wu_t, wd)


def _mix_bwd(d_hn2, dh2, h1, o, a, m, yb, g_ffn_pre, g_post, g_attn, g_pool, w_out, w_pool, pool_scale, dep):
    t = a.shape[0]
    tt = 256 if t % 256 == 0 else t

    def body(dhn_ref, dh2_ref, h1_ref, o_ref, a_ref, m_ref, yb_ref, gfp_ref, go_ref, ga_ref, gp_ref, wo_ref, wp_ref,
             ps_ref, dep_ref, dh1_ref, da_ref, dyc_ref, dgfp_ref, dgo_ref, dga_ref, dgp_ref, dps_ref, dwp_ref, dwo_ref):
        i = pl.program_id(0)

        @pl.when(i == 0)
        def _():
            for r in (dgfp_ref, dgo_ref, dga_ref, dgp_ref, dps_ref, dwp_ref, dwo_ref):
                r[...] = jnp.zeros_like(r)

        d1, dg = _rms_bwd(dhn_ref[...], h1_ref[...], gfp_ref[...])
        dgfp_ref[...] = dgfp_ref[...] + dg
        dh1 = dh2_ref[...] + d1
        dh1_ref[...] = dh1
        d_o, dg = _rms_bwd(dh1, o_ref[...], go_ref[...])
        dgo_ref[...] = dgo_ref[...] + dg
        d_mix = _mm_nt(d_o, wo_ref[...])
        av, mv = a_ref[...], m_ref[...]
        mix = jnp.concatenate([av * _rstd(av) * ga_ref[...], mv * _rstd(mv) * gp_ref[...]], axis=1)
        dwo_ref[...] = dwo_ref[...] + _mm_tn(mix, d_o)
        d_a, dg = _rms_bwd(d_mix[:, :DA], av, ga_ref[...])
        dga_ref[...] = dga_ref[...] + dg
        da_ref[...] = d_a
        d_m, dg = _rms_bwd(d_mix[:, DA:], mv, gp_ref[...])
        dgp_ref[...] = dgp_ref[...] + dg
        tok = i * tt + lax.broadcasted_iota(jnp.int32, (tt, PC), 0)
        dps = []
        for g, w in enumerate(WINS):
            sl = slice(PC * g, PC * (g + 1))
            ybg = yb_ref[:, sl]
            wpg = wp_ref[g].astype(BF)
            mlin = jnp.dot(ybg, wpg, preferred_element_type=F32)
            dmg = d_m[:, sl]
            dps.append(jnp.sum(dmg * mlin, axis=0, keepdims=True))
            dml = (dmg * ps_ref[:, sl]).astype(BF)
            dwp_ref[g] = dwp_ref[g] + _mm_tn(ybg, dml)
            dyc_ref[:, sl] = _mm_nt(dml, wpg) / jnp.minimum(tok + 1, w).astype(F32)
        dps_ref[...] = dps_ref[...] + jnp.concatenate(dps, axis=1)

    row = lambda w: pl.BlockSpec((tt, w), lambda i: (i, 0))
    return pl.pallas_call(
        body, name="mix_bwd", grid=(t // tt,),
        in_specs=[row(D), row(D), row(D), row(D), row(DA), row(DP), row(DP), _full((1, D)), _full((1, D)),
                  _full((1, DA)), _full((1, DP)), _resident((D, D)), _full((len(WINS), PC, PC)), _full((1, DP)), ANY],
        out_specs=[row(D), row(DA), row(DP), _full((1, D)), _full((1, D)), _full((1, DA)), _full((1, DP)),
                   _full((1, DP)), _full((len(WINS), PC, PC)), _full((D, D))],
        out_shape=[jax.ShapeDtypeStruct((t, D), F32), jax.ShapeDtypeStruct((t, DA), F32), jax.ShapeDtypeStruct((t, DP), F32),
                   jax.ShapeDtypeStruct((1, D), F32), jax.ShapeDtypeStruct((1, D), F32), jax.ShapeDtypeStruct((1, DA), F32),
                   jax.ShapeDtypeStruct((1, DP), F32), jax.ShapeDtypeStruct((1, DP), F32),
                   jax.ShapeDtypeStruct((len(WINS), PC, PC), F32), jax.ShapeDtypeStruct((D, D), F32)],
        compiler_params=_cp(),
    )(d_hn2, dh2, h1, o, a, m, yb, g_ffn_pre, g_post, g_attn, g_pool, w_out, w_pool, pool_scale, dep)


def _attn_bwd(qa, ka, v, a, d_a, lse, dep):
    t = qa.shape[0]
    ta = _tile(t)
    n = t // ta

    def body(q_ref, k_ref, v_ref, o_ref, do_ref, lse_ref, dep_ref, dq_ref, dk_ref, dv_ref):
        j = pl.program_id(1)

        @pl.when(j == 0)
        def _():
            dq_ref[...] = jnp.zeros_like(dq_ref)

        dk_ref[...] = jnp.zeros_like(dk_ref)
        dv_ref[...] = jnp.zeros_like(dv_ref)
        ks = [k_ref[:, LANE * h:LANE * (h + 1)] for h in range(2)]
        v2 = v_ref[...]
        lane = lax.broadcasted_iota(jnp.int32, (ta, LANE), 1)
        mine = [lane < HD, lane >= HD]

        def tile(i, masked):
            rows = pl.ds(pl.multiple_of(i * ta, ta), ta)
            do2 = do_ref[rows, :]
            prod = do2 * o_ref[rows, :]
            lse2 = lse_ref[rows, :]
            do2b = do2.astype(BF)
            qh = [q_ref[rows, LANE * h:LANE * (h + 1)] for h in range(2)]
            s = [_mm_nt(qh[h], ks[h]) for h in range(2)]
            dp = [_mm_nt(jnp.where(mine[h], do2, 0.0), v2) for h in range(2)]
            delta = [jnp.sum(jnp.where(mine[h], prod, 0.0), axis=1, keepdims=True) for h in range(2)]
            lse_h = [jnp.sum(jnp.where(lane == HD * h, lse2, 0.0), axis=1, keepdims=True) for h in range(2)]
            pr = [jnp.exp(s[h] - lse_h[h]) for h in range(2)]
            if masked:
                keep = (lax.broadcasted_iota(jnp.int32, (ta, ta), 1) <= lax.broadcasted_iota(jnp.int32, (ta, ta), 0))
                pr = [jnp.where(keep, ph, 0.0) for ph in pr]
            ds = [(pr[h] * (dp[h] - delta[h])).astype(BF) for h in range(2)]
            dv_ref[...] = dv_ref[...] + jnp.where(mine[0], _mm_tn(pr[0], do2b), _mm_tn(pr[1], do2b))
            for h in range(2):
                sl = slice(LANE * h, LANE * (h + 1))
                dk_ref[:, sl] = dk_ref[:, sl] + _mm_tn(ds[h], qh[h])
                dq_ref[0, rows, sl] = dq_ref[0, rows, sl] + jnp.dot(ds[h], ks[h], preferred_element_type=F32)

        def step(i, carry):
            tile(i, False)
            return carry

        tile(j, True)
        lax.fori_loop(j + 1, n, step, 0)

    qrow = lambda w: pl.BlockSpec((t, w), lambda p, j: (0, p))
    krow = lambda w: pl.BlockSpec((ta, w), lambda p, j: (j, p))
    return pl.pallas_call(
        body, name="attn_bwd", grid=(NH // 2, n),
        in_specs=[qrow(2 * LANE), krow(2 * LANE), krow(LANE), qrow(LANE), qrow(LANE), qrow(LANE), ANY],
        out_specs=[pl.BlockSpec((1, t, 2 * LANE), lambda p, j: (p, 0, 0)), krow(2 * LANE), krow(LANE)],
        out_shape=[jax.ShapeDtypeStruct((NH // 2, t, 2 * LANE), F32), jax.ShapeDtypeStruct((t, NH * LANE), F32),
                   jax.ShapeDtypeStruct((t, DA), F32)],
        compiler_params=_cp(),
    )(qa, ka, v, a, d_a, lse, dep)


def _in_bwd(dqa, dka, dv, dyc, fl, x, dh1, g1, w_in_t, tri_u, selq, selk, band_t, dep):
    t = x.shape[0]
    tt = _tile(t)
    nt = t // tt
    hb = tt // HALO
    rev = lambda s: nt - 1 - s

    def body(dqa_ref, dka_ref, dv_ref, dyc_ref, dyn_ref, fl_ref, x_ref, dh1_ref, g_ref, w_ref, tri_ref, sq_ref, sk_ref,
             band_ref, dep_ref, dx_ref, dw_ref, dg_ref, db_ref, carry, acc, sem):
        s = pl.program_id(0)
        i = nt - 1 - s

        @pl.when(s == 0)
        def _():
            carry[...] = jnp.zeros_like(carry)
            dg_ref[...] = jnp.zeros_like(dg_ref)
            db_ref[...] = jnp.zeros_like(db_ref)

        dq_cat = jnp.concatenate([dqa_ref[p] for p in range(NH // 2)], axis=1)
        dk_cat = dka_ref[...]
        dc = _dot2t(dq_cat, sq_ref[...]) - _dot2t(dk_cat, sk_ref[...])
        dlf = carry[...] + _dot3(tri_ref[...], dc)
        carry[...] = carry[...] + jnp.sum(dc, axis=0, keepdims=True)
        flv = fl_ref[...]
        lane = lax.broadcasted_iota(jnp.int32, flv.shape, 1)
        d_fl = jnp.where(lane < NH, dlf / (1.0 + jnp.exp(flv)), 0.0)
        db_ref[...] = db_ref[...] + jnp.sum(d_fl, axis=0, keepdims=True)
        low = lax.broadcasted_iota(jnp.int32, (tt, LANE), 1) < HD
        dqs, dks = [], []
        for p in range(NH // 2):
            b0, b1 = slice(2 * LANE * p, 2 * LANE * p + LANE), slice(2 * LANE * p + LANE, 2 * LANE * (p + 1))
            dqs.append(jnp.where(low, dq_cat[:, b0], pltpu.roll(dq_cat[:, b1], HD, 1)) * (1.0 / math.sqrt(HD)))
            dks.append(jnp.where(low, dk_cat[:, b0], pltpu.roll(dk_cat[:, b1], HD, 1)))
        nxt = dyn_ref[...] * jnp.where(i < nt - 1, 1.0, 0.0)
        tok = i * tt + lax.broadcasted_iota(jnp.int32, (tt, PC), 0)
        dus = []
        for g, w in enumerate(WINS):
            sl = slice(PC * g, PC * (g + 1))
            dycg = dyc_ref[:, sl]
            ext = jnp.concatenate([dycg, nxt[:, sl]], axis=0)
            dus.append(_dot2(band_ref[g], ext) - dycg * jnp.minimum(tok + 1, w).astype(F32))
        d_z = jnp.concatenate(dqs + dks + [dv_ref[...], d_fl] + dus, axis=1).astype(BF)
        xv = x_ref[...]
        gv = g_ref[...]
        hn = (xv * _rstd(xv) * gv).astype(BF)
        d_hn = jnp.dot(d_z, w_ref[...], preferred_element_type=F32)
        contrib = _mm_tn(d_z, hn)

        @pl.when(s == 0)
        def _():
            acc[...] = contrib

        @pl.when(s > 0)
        def _():
            acc[...] = acc[...] + contrib

        d1, dg = _rms_bwd(d_hn, xv, gv)
        dg_ref[...] = dg_ref[...] + dg
        dx_ref[...] = dh1_ref[...] + d1

        @pl.when(s == nt - 1)
        def _():
            cp = pltpu.make_async_copy(acc, dw_ref, sem)
            cp.start()
            cp.wait()

    row = lambda w: pl.BlockSpec((tt, w), lambda s: (rev(s), 0))
    return pl.pallas_call(
        body, name="in_bwd", grid=(nt,),
        in_specs=[pl.BlockSpec((NH // 2, tt, 2 * LANE), lambda s: (0, rev(s), 0)), row(NH * LANE), row(DA), row(DP),
                  pl.BlockSpec((HALO, DP), lambda s: (jnp.minimum((rev(s) + 1) * hb, nt * hb - 1), 0)),
                  row(LANE), row(D), row(D), _full((1, D)), _resident((ZW, D)), _full((tt, tt)),
                  _full((NH * LANE, LANE)), _full((NH * LANE, LANE)), _full((len(WINS), tt, tt + HALO)), ANY],
        out_specs=[row(D), pl.BlockSpec(memory_space=pl.ANY), _full((1, D)), _full((1, LANE))],
        out_shape=[jax.ShapeDtypeStruct((t, D), F32), jax.ShapeDtypeStruct((ZW, D), F32),
                   jax.ShapeDtypeStruct((1, D), F32), jax.ShapeDtypeStruct((1, LANE), F32)],
        scratch_shapes=[pltpu.VMEM((1, LANE), F32), pltpu.VMEM((ZW, D), F32), pltpu.SemaphoreType.DMA],
        compiler_params=_cp(),
    )(dqa, dka, dv, dyc, dyc, fl, x, dh1, g1, w_in_t, tri_u, selq, selk, band_t, dep)


def _dot2t(x, sel):
    hi, lo = _split2(x)
    return jnp.dot(hi, sel, preferred_element_type=F32) + jnp.dot(lo, sel, preferred_element_type=F32)


class _NoComm:
    def __init__(self, w2):
        self.w2 = w2
        self.dep = jnp.zeros((8, LANE), F32)

    def weights2(self, after):
        return self.w2

    def after_ffn(self, grads2):
        self.grads2 = grads2
        return self.dep

    def after_mix(self, after, early):
        self.early = early
        return self.dep

    def after_attn(self, after):
        return self.dep


def _local_step(x, p, tgt, sm, w1, comm):
    w_in_t, w_out = w1
    tt = _tile(x.shape[0])
    eq, ek, rowq, rowk, selq, selk = _aug_consts()
    b_pad = jnp.pad(sm["b_forget"], ((0, 0), (0, LANE - NH)))
    qa, ka, v, u, fl = _in_proj(x, sm["g_mix_pre"], w_in_t, b_pad, _tri(tt, False), eq, ek, rowq, rowk, comm.dep)
    a, lse = _attn_fwd(qa, ka, v)
    wg_t, wu_t, wd, w_ple, w_pg = comm.weights2(a)
    yb, m, o, h1 = _mix_out(a, u, x, _band(tt, False), sm["w_pool"], sm["pool_scale"], sm["g_attn_grp"],
                            sm["g_pool_grp"], w_out, sm["g_mix_post"])
    hn2, gate, up, dff, dh2, loss, dwpg, dwple, dgple, dgfpost = _ffn_fwd(
        h1, sm["g_ffn_pre"], wg_t, wu_t, wd, sm["g_ffn_post"], p, w_ple, sm["g_ple"], w_pg, tgt)
    dwg_t, dwu_t, dwd, d_hn2 = _ffn_bwd(hn2, gate, up, dff, wg_t, wu_t, wd)
    dep = comm.after_ffn((dwg_t, dwu_t, dwd, dwple, dwpg))
    dh1, d_a, dyc, dgfpre, dgpost, dgattn, dgpool, dps, dwpool, dwout = _mix_bwd(
        d_hn2, dh2, h1, o, a, m, yb, sm["g_ffn_pre"], sm["g_mix_post"], sm["g_attn_grp"], sm["g_pool_grp"],
        w_out, sm["w_pool"], sm["pool_scale"], dep)
    early = dict(loss=loss[0:1, 0:1], g_attn_grp=dgattn, g_pool_grp=dgpool, w_pool=dwpool, pool_scale=dps,
                 g_mix_post=dgpost, g_ffn_pre=dgfpre, g_ffn_post=dgfpost, g_ple=dgple)
    dqa, dka, dvv = _attn_bwd(qa, ka, v, a, d_a, lse, comm.after_mix(dh1, early))
    dx, dwin_t, dg1, dbf = _in_bwd(dqa, dka, dvv, dyc, fl, x, dh1, sm["g_mix_pre"], w_in_t, _tri(tt, True),
                                   selq, selk, _band(tt, True), comm.after_attn(dvv))
    return dx, (dwin_t, dwout), dict(g_mix_pre=dg1, b_forget=dbf[:, :NH])


def _place():
    x, y, c = lax.axis_index("x"), lax.axis_index("y"), lax.axis_index("c")
    return x, y, c, [(1 - x, y), (x, 1 - y), (1 - x, 1 - y)]


def _rows(c, h):
    return pl.ds(pl.multiple_of(c * h, 16), h)


def _plan_gather(h):
    def plan(src, land):
        x, y, c, chips = _place()
        return [(src.at[_rows(c, h), :], land.at[2 * x + y, _rows(c, h), :], (cx, cy, c),
                 land.at[2 * cx + cy, _rows(c, h), :]) for cx, cy in chips]
    return plan


def _plan_forward(h):
    def plan(land_in, own, land):
        x, y, c, chips = _place()
        sib, me = (x, y, 1 - c), 2 * x + y
        return ([(land_in.at[2 * cx + cy, _rows(c, h), :], land.at[2 * cx + cy, _rows(c, h), :], sib,
                  land.at[2 * cx + cy, _rows(1 - c, h), :]) for cx, cy in chips]
                + [(own, land.at[me], sib, land.at[me])])
    return plan


def _plan_swap_halves(h):
    def plan(buf_in, buf):
        x, y, c, _ = _place()
        return [(buf_in.at[_rows(c, h), :], buf.at[_rows(c, h), :], (x, y, 1 - c), buf.at[_rows(1 - c, h), :])]
    return plan


def _plan_pair_rows(h):
    def plan(src, land):
        x, y, c, _ = _place()
        return [(src.at[:, _rows(1 - c, h), :], land, (x, y, 1 - c), land)]
    return plan


def _plan_scatter(src, land):
    x, y, c, chips = _place()
    return [(src.at[2 * cx + cy], land.at[k], (cx, cy, c), land.at[k]) for k, (cx, cy) in enumerate(chips)]


def _plan_all(src, land):
    x, y, c, _ = _place()
    copies = []
    for r in range(1, 8):
        px, py, pc = (1 - a if b else a for a, b in zip((x, y, c), (r >> 2 & 1, r >> 1 & 1, r & 1)))
        copies.append((src, land.at[4 * x + 2 * y + c], (px, py, pc), land.at[4 * px + 2 * py + pc]))
    return copies


def _remote(src, dst, send_sems, recv_sems, k, peer):
    return pltpu.make_async_remote_copy(src_ref=src, dst_ref=dst, send_sem=send_sems.at[k], recv_sem=recv_sems.at[k],
                                        device_id=peer, device_id_type=MESH)


def _exchange(name, n, src, land, plan):
    def body(src_ref, land_ref, send_sems, recv_sems):
        copies = plan(src_ref, land_ref)
        for k, (s, d, peer, _) in enumerate(copies):
            _remote(s, d, send_sems, recv_sems, k, peer).start()
        for k, (s, _, peer, mine) in enumerate(copies):
            _remote(s, mine, send_sems, recv_sems, k, peer).wait_recv()
        for k, (s, d, peer, _) in enumerate(copies):
            _remote(s, d, send_sems, recv_sems, k, peer).wait_send()

    return pl.pallas_call(
        body, name=name, in_specs=[ANY], out_specs=ANY, out_shape=land,
        scratch_shapes=[pltpu.SemaphoreType.DMA((n,)), pltpu.SemaphoreType.DMA((n,))],
    )(src)


def _exchange_inplace(name, n, buf, extra, plan):
    def body(*refs):
        ins, buf_ref, send_sems, recv_sems = refs[:1 + len(extra)], refs[1 + len(extra)], refs[-2], refs[-1]
        copies = plan(*ins, buf_ref)
        for k, (s, d, peer, _) in enumerate(copies):
            _remote(s, d, send_sems, recv_sems, k, peer).start()
        for k, (s, _, peer, mine) in enumerate(copies):
            _remote(s, mine, send_sems, recv_sems, k, peer).wait_recv()
        for k, (s, d, peer, _) in enumerate(copies):
            _remote(s, d, send_sems, recv_sems, k, peer).wait_send()

    return pl.pallas_call(
        body, name=name, in_specs=[ANY] * (1 + len(extra)), out_specs=ANY, out_shape=_sds(buf.shape, buf.dtype),
        input_output_aliases={0: 0},
        scratch_shapes=[pltpu.SemaphoreType.DMA((n,)), pltpu.SemaphoreType.DMA((n,))],
    )(buf, *extra)


HBM = pl.BlockSpec(memory_space=pltpu.HBM)
SEM = pl.BlockSpec(memory_space=pltpu.SEMAPHORE)
EFFECT = pltpu.SideEffectType.DATAFLOW_SIDE_EFFECTING


def _exchange_start(name, n, src, land, plan):
    def body(src_ref, land_ref, send_sems, recv_sems, src_thru, land_thru, token):
        for k, (s, d, peer, _) in enumerate(plan(src_ref, land_ref)):
            _remote(s, d, send_sems, recv_sems, k, peer).start()
        token[...] = jnp.zeros_like(token)

    return pl.pallas_call(
        body, name=name,
        out_shape=(pltpu.SemaphoreType.DMA((n,)), pltpu.SemaphoreType.DMA((n,)), pltpu.HBM(src.shape, src.dtype),
                   pltpu.HBM(land.shape, land.dtype), jax.ShapeDtypeStruct((8, LANE), F32)),
        in_specs=(HBM, HBM), out_specs=(SEM, SEM, HBM, HBM, pl.BlockSpec(memory_space=pltpu.VMEM)),
        input_output_aliases={0: 2, 1: 3},
        compiler_params=pltpu.CompilerParams(has_side_effects=EFFECT),
    )(pltpu.with_memory_space_constraint(src, pltpu.HBM), pltpu.with_memory_space_constraint(land, pltpu.HBM))


def _exchange_wait(name, started, plan, after):
    send_sems, recv_sems, src, land, _ = started

    def body(src_ref, land_ref, send_sems, recv_sems, after_ref, src_out, land_out):
        for k, (s, _, peer, mine) in enumerate(plan(src_ref, land_ref)):
            cp = _remote(s, mine, send_sems, recv_sems, k, peer)
            cp.wait_send()
            cp.wait_recv()

    return pl.pallas_call(
        body, name=name, out_shape=(pltpu.HBM(src.shape, src.dtype), pltpu.HBM(land.shape, land.dtype)),
        in_specs=(HBM, HBM, SEM, SEM, ANY), out_specs=(HBM, HBM), input_output_aliases={0: 0, 1: 1},
        compiler_params=pltpu.CompilerParams(has_side_effects=EFFECT),
    )(src, land, send_sems, recv_sems, after)


def _pair_sum(name, cidx, g, recv, br):
    h = recv.shape[1]
    nb = h // br

    def body(c_ref, g_ref, r_ref, out_ref):
        out_ref[...] = (g_ref[...] + r_ref[...]).astype(BF)

    return pl.pallas_call(
        body, name=name,
        grid_spec=pltpu.PrefetchScalarGridSpec(
            num_scalar_prefetch=1, grid=(NSHARD, nb),
            in_specs=[pl.BlockSpec((1, br, D), lambda s, i, c: (s, c[0] * nb + i, 0)),
                      pl.BlockSpec((1, br, D), lambda s, i, c: (s, i, 0))],
            out_specs=pl.BlockSpec((1, br, D), lambda s, i, c: (s, i, 0))),
        out_shape=jax.ShapeDtypeStruct((NSHARD, h, D), BF),
    )(cidx, g, recv)


def _chip_sum(name, place, pb, y, br):
    h = y.shape[1]
    nb = h // br

    def body(pl_ref, p_ref, y_ref, out_ref):
        acc = p_ref[0].astype(F32)
        for k in range(NSHARD - 1):
            acc = acc + y_ref[k].astype(F32)
        out_ref[...] = acc

    return pl.pallas_call(
        body, name=name,
        grid_spec=pltpu.PrefetchScalarGridSpec(
            num_scalar_prefetch=1, grid=(nb,),
            in_specs=[pl.BlockSpec((1, br, D), lambda i, s: (s[0], i, 0)),
                      pl.BlockSpec((NSHARD - 1, br, D), lambda i, s: (0, i, 0))],
            out_specs=pl.BlockSpec((br, D), lambda i, s: (s[1] * nb + i, 0))),
        out_shape=jax.ShapeDtypeStruct((2 * h, D), F32),
    )(place, pb, y)


def _sum_slots(v):
    def body(in_ref, out_ref):
        acc = in_ref[0]
        for k in range(1, 8):
            acc = acc + in_ref[k]
        out_ref[...] = acc

    vm = pl.BlockSpec(memory_space=pltpu.VMEM)
    return pl.pallas_call(body, name="sum_slots", in_specs=[vm], out_specs=vm,
                          out_shape=jax.ShapeDtypeStruct(v.shape[1:], F32))(v)


def _all_reduce_small(v):
    rows = v.shape[0]

    def body(in_ref, out_ref, buf, send_sems, recv_sems):
        x, y, c, _ = _place()
        me = 4 * x + 2 * y + c
        flips = [(r >> 2 & 1, r >> 1 & 1, r & 1) for r in range(1, 8)]

        def peer(f):
            return tuple(1 - a if b else a for a, b in zip((x, y, c), f))

        def copy(k, slot, to):
            return pltpu.make_async_remote_copy(src_ref=in_ref, dst_ref=buf.at[slot], send_sem=send_sems.at[k],
                                                recv_sem=recv_sems.at[k], device_id=to, device_id_type=MESH)

        sends = [copy(k, me, peer(f)) for k, f in enumerate(flips)]
        for cp in sends:
            cp.start()
        buf[me] = in_ref[...]
        for k, f in enumerate(flips):
            px, py, pc = peer(f)
            copy(k, 4 * px + 2 * py + pc, peer(f)).wait_recv()
        for cp in sends:
            cp.wait_send()
        acc = buf[0]
        for k in range(1, 8):
            acc = acc + buf[k]
        out_ref[...] = acc

    vm = pl.BlockSpec(memory_space=pltpu.VMEM)
    return pl.pallas_call(
        body, name="all_reduce_small", in_specs=[vm], out_specs=vm,
        out_shape=jax.ShapeDtypeStruct(v.shape, F32),
        scratch_shapes=[pltpu.VMEM((8, rows, LANE), F32), pltpu.SemaphoreType.DMA((7,)), pltpu.SemaphoreType.DMA((7,))],
    )(v)


def _adamw_math(w, g, m, v):
    m = ADAM_B1 * m + (1.0 - ADAM_B1) * g
    v = ADAM_B2 * v + (1.0 - ADAM_B2) * (g * g)
    m_hat = m / (1.0 - ADAM_B1 ** ADAM_STEP)
    v_hat = v / (1.0 - ADAM_B2 ** ADAM_STEP)
    delta = -ADAM_LR * (m_hat / (jnp.sqrt(v_hat) + ADAM_EPS) + ADAM_WD * w)
    return delta, m, v


def _adamw(w, g, m, v):
    r, c = w.shape
    br = next(b for b in (256, 176, r) if r % b == 0)

    def body(w_ref, g_ref, m_ref, v_ref, d_ref, nm_ref, nv_ref):
        d_ref[...], nm_ref[...], nv_ref[...] = _adamw_math(w_ref[...], g_ref[...], m_ref[...], v_ref[...])

    spec = pl.BlockSpec((br, c), lambda i: (i, 0))
    return pl.pallas_call(
        body, name="adamw", grid=(r // br,), in_specs=[spec] * 4, out_specs=[spec] * 3,
        out_shape=[jax.ShapeDtypeStruct((r, c), F32)] * 3, compiler_params=_cp(),
    )(w, g, m, v)


def _adamw_small(ws, gs, ms, vs):
    n = len(ws)

    def body(*refs):
        ins, outs = refs[:4 * n], refs[4 * n:]
        for k in range(n):
            d, m, v = _adamw_math(ins[k][...], ins[n + k][...], ins[2 * n + k][...], ins[3 * n + k][...])
            outs[k][...] = d
            outs[n + k][...] = m
            outs[2 * n + k][...] = v

    vm = pl.BlockSpec(memory_space=pltpu.VMEM)
    out = pl.pallas_call(
        body, name="adamw_small", in_specs=[vm] * (4 * n), out_specs=[vm] * (3 * n),
        out_shape=[jax.ShapeDtypeStruct(w.shape, F32) for w in ws] * 3,
    )(*ws, *gs, *ms, *vs)
    return out[:n], out[n:2 * n], out[2 * n:]


BIG = ("w_in", "w_out", "w_ffn_gate", "w_ffn_up", "w_ffn_down", "w_ple_proj", "w_ple_gate")
SMALL = ("g_mix_pre", "b_forget", "g_attn_grp", "g_pool_grp", "w_pool", "pool_scale", "g_mix_post", "g_ffn_pre",
         "g_ffn_post", "g_ple")
TRANSPOSED = ("w_in", "w_ffn_gate", "w_ffn_up")
VECTORS = tuple(n for n in SMALL if n != "w_pool")
ORDER = ("g_mix_pre", "w_in", "b_forget", "g_attn_grp", "g_pool_grp", "w_pool", "pool_scale", "w_out", "g_mix_post",
         "g_ffn_pre", "w_ffn_gate", "w_ffn_up", "w_ffn_down", "g_ffn_post", "w_ple_proj", "g_ple", "w_ple_gate")


def _pad_rows(a, rows):
    return jnp.pad(a, ((0, rows - a.shape[0]), (0, 0)))


def _stack1(w_in, w_out):
    return _pad_rows(jnp.concatenate([_pad_rows(w_in.T, IN_PAD), w_out], axis=0), ROWS1)


def _stack2(wg, wu, wd, wple, wpg):
    return _pad_rows(jnp.concatenate([wg.T, wu.T, wd, wple.reshape(DPLE // NSHARD, D), wpg], axis=0), ROWS2)


def _unstack1(s):
    return s[:IN_SH], s[O1_OUT:USED1]


def _unstack2(s):
    return s[:O2_U], s[O2_U:O2_D], s[O2_D:O2_PLE], s[O2_PLE:O2_PG].reshape(DPLE, DPLE), s[O2_PG:USED2]


def _cat(g, lo, hi):
    return g[:, lo:hi].reshape(NSHARD * (hi - lo), D)


def _unstack1_full(g):
    w_in_t = _cat(g, 0, IN_SH)
    w_in_t = jnp.concatenate([w_in_t[:3 * DA], _pad_rows(w_in_t[3 * DA:3 * DA + NH], LANE), w_in_t[3 * DA + NH:]], axis=0)
    return w_in_t, _cat(g, O1_OUT, USED1)


def _unstack2_full(g):
    w_ple = g[:, O2_PLE:O2_PG].reshape(NSHARD, DPLE, DPLE).transpose(1, 0, 2).reshape(DPLE, D)
    return _cat(g, 0, O2_U), _cat(g, O2_U, O2_D), _cat(g, O2_D, O2_PLE), w_ple, _cat(g, O2_PG, USED2)


def _shards(a):
    return a.reshape(NSHARD, a.shape[0] // NSHARD, D)


def _stack1_full(dwin_t, dwout):
    dwin_t = jnp.concatenate([dwin_t[:3 * DA + NH], dwin_t[3 * DA + LANE:]], axis=0).reshape(NSHARD, IN_SH, D)
    zeros = lambda r: jnp.zeros((NSHARD, r, D), F32)
    return jnp.concatenate([dwin_t, zeros(IN_PAD - IN_SH), _shards(dwout), zeros(ROWS1 - USED1)], axis=1)


def _stack2_full(dwg_t, dwu_t, dwd, dwple, dwpg):
    dwple = dwple.reshape(DPLE, NSHARD, DPLE).transpose(1, 0, 2).reshape(NSHARD, DPLE // NSHARD, D)
    return jnp.concatenate([_shards(dwg_t), _shards(dwu_t), _shards(dwd), dwple, _shards(dwpg),
                            jnp.zeros((NSHARD, ROWS2 - USED2, D), F32)], axis=1)


def _sds(shape, dtype):
    return jax.ShapeDtypeStruct(shape, dtype)


class _Comm:
    def __init__(self, stack2, me, c):
        self.stack2, self.me, self.c = stack2, me, c
        self.cidx = c.astype(jnp.int32).reshape(1)
        self.place = jnp.stack([me, c]).astype(jnp.int32)
        self.h = ROWS2 // 2
        self.gather = _exchange_start("gather2_start", 3, stack2, lax.empty((NSHARD, ROWS2, D), BF), _plan_gather(self.h))
        self.dep = self.gather[4]

    def weights2(self, after):
        own, land = _exchange_wait("gather2_wait", self.gather, _plan_gather(self.h), after)
        return _unstack2_full(_exchange_inplace("gather2_forward", 4, land, (own,), _plan_forward(self.h)))

    def after_ffn(self, grads2):
        g = _stack2_full(*grads2)
        self.pair = _exchange_start("reduce2_pair_start", 1, g, lax.empty((NSHARD, self.h, D), F32),
                                    _plan_pair_rows(self.h))
        return self.pair[4]

    def after_mix(self, after, early):
        g, recv = _exchange_wait("reduce2_pair_wait", self.pair, _plan_pair_rows(self.h), after)
        pb = _pair_sum("pair_sum2", self.cidx, g, recv, RED2)
        self.chip = _exchange_start("reduce2_chip_start", 3, pb, lax.empty((NSHARD - 1, self.h, D), BF), _plan_scatter)
        self.early_shapes = {n: early[n].shape for n in early}
        v = _pack_small(early)
        self.small = _exchange_start("small_start", 7, v, lax.empty((8,) + v.shape, F32), _plan_all)
        return self.chip[4] + self.small[4]

    def after_attn(self, after):
        pb, y = _exchange_wait("reduce2_chip_wait", self.chip, _plan_scatter, after)
        f = _chip_sum("chip_sum2", self.place, pb, y, RED2)
        self.reduced2 = _exchange_inplace("reduce2_gather", 1, f, (), _plan_swap_halves(self.h))
        v, land = _exchange_wait("small_wait", self.small, _plan_all, after)
        land = lax.dynamic_update_slice(land, v[None], (2 * self.me + self.c, 0, 0))
        self.early = _unpack_small(_sum_slots(land), self.early_shapes)
        return self.reduced2


def _pack_small(small):
    parts = []
    for name in small:
        flat = small[name].reshape(-1)
        parts.append(jnp.pad(flat, (0, -flat.shape[0] % LANE)).reshape(-1, LANE))
    v = jnp.concatenate(parts, axis=0)
    return _pad_rows(v, v.shape[0] + (-v.shape[0] % 8))


def _unpack_small(v, shapes):
    out, r = {}, 0
    for name in shapes:
        n = math.prod(shapes[name])
        rows = -(-n // LANE)
        out[name] = v[r:r + rows].reshape(-1)[:n].reshape(shapes[name])
        r += rows
    return out


def kernel(x, p, g_mix_pre, w_in, b_forget, g_attn_grp, g_pool_grp, w_pool, pool_scale, w_out, g_mix_post, g_ffn_pre, w_ffn_gate, w_ffn_up, w_ffn_down, g_ffn_post, w_ple_proj, g_ple, w_ple_gate, loss_target, m_g_mix_pre, m_w_in, m_b_forget, m_g_attn_grp, m_g_pool_grp, m_w_pool, m_pool_scale, m_w_out, m_g_mix_post, m_g_ffn_pre, m_w_ffn_gate, m_w_ffn_up, m_w_ffn_down, m_g_ffn_post, m_w_ple_proj, m_g_ple, m_w_ple_gate, v_g_mix_pre, v_w_in, v_b_forget, v_g_attn_grp, v_g_pool_grp, v_w_pool, v_pool_scale, v_w_out, v_g_mix_post, v_g_ffn_pre, v_w_ffn_gate, v_w_ffn_up, v_w_ffn_down, v_g_ffn_post, v_w_ple_proj, v_g_ple, v_w_ple_gate):
    args = dict(locals())
    strip = lambda n, a: a if n in VECTORS else a[0]
    w = {n: strip(n, args[n]) for n in ORDER}
    mom = {n: strip(n, args["m_" + n]) for n in ORDER}
    var = {n: strip(n, args["v_" + n]) for n in ORDER}
    sm = {n: w[n] for n in SMALL}

    c = lax.axis_index("c")
    me = 2 * lax.axis_index("x") + lax.axis_index("y")
    h1 = ROWS1 // 2
    bf = lambda n: w[n].astype(BF)
    stack1 = _stack1(bf("w_in"), bf("w_out"))
    stack2 = _stack2(*[bf(n) for n in BIG[2:]])
    land = _exchange("gather1", 3, stack1, _sds((NSHARD, ROWS1, D), BF), _plan_gather(h1))
    land, stack2 = lax.optimization_barrier((land, stack2))
    comm = _Comm(stack2, me, c)
    w1 = _unstack1_full(_exchange_inplace("gather1_forward", 4, land, (stack1,), _plan_forward(h1)))
    dx, grads1, late = _local_step(x[0], p[0, 0], loss_target[0], sm, w1, comm)

    late_shapes = {n: late[n].shape for n in late}
    red_small = {**comm.early, **_unpack_small(_all_reduce_small(_pack_small(late)), late_shapes)}
    loss = 0.5 / D * red_small["loss"][0, 0]

    g1 = _stack1_full(*grads1)
    recv = _exchange("reduce1_pair", 1, g1, _sds((NSHARD, h1, D), F32), _plan_pair_rows(h1))
    pb = _pair_sum("pair_sum1", comm.cidx, g1, recv, RED1)
    y = _exchange("reduce1_chip", 3, pb, _sds((NSHARD - 1, h1, D), BF), _plan_scatter)
    f = _chip_sum("chip_sum1", comm.place, pb, y, RED1)
    reduced1 = _exchange_inplace("reduce1_gather", 1, f, (), _plan_swap_halves(h1))
    grads = dict(zip(BIG, _unstack1(reduced1) + _unstack2(comm.reduced2)))
    for n in SMALL:
        grads[n] = red_small[n].reshape(w[n].shape)

    flip = lambda n, a: a.T if n in TRANSPOSED else a
    delta, new_m, new_v = {}, {}, {}
    for n in BIG:
        d_, m_, v_ = _adamw(flip(n, w[n]), grads[n], flip(n, mom[n]), flip(n, var[n]))
        grads[n], delta[n], new_m[n], new_v[n] = flip(n, grads[n]), flip(n, d_), flip(n, m_), flip(n, v_)
    two_d = lambda a: a.reshape(-1, a.shape[-1])
    ds, ms, vs = _adamw_small([two_d(w[n]) for n in SMALL], [two_d(grads[n]) for n in SMALL],
                              [two_d(mom[n]) for n in SMALL], [two_d(var[n]) for n in SMALL])
    for k, n in enumerate(SMALL):
        delta[n], new_m[n], new_v[n] = ds[k].reshape(w[n].shape), ms[k].reshape(w[n].shape), vs[k].reshape(w[n].shape)

    lead = lambda d: [d[n] if n in VECTORS else d[n][None] for n in ORDER]
    return (loss, dx[None], *lead(grads), *lead(delta), *lead(new_m), *lead(new_v))
```

```python
import functools
import math

import jax
import jax.numpy as jnp
import numpy as np
from jax import lax
from jax.experimental import pallas as pl
from jax.experimental.pallas import tpu as pltpu

F32 = jnp.float32
BF = jnp.bfloat16
MESH = pl.DeviceIdType.MESH

D = 1024
DA = 512
DP = 512
NH = 8
HD = 64
DFF = 2816
DPLE = 256
WINS = (2, 4, 8, 16)
PC = 128
ZW = 3 * DA + 128 + DP
EPS = 1e-6
NSHARD = 4

LANE = 128
HALO = 128

IN_SH = 514
IN_PAD = 528
FF_SH = DFF // NSHARD
O1_OUT, USED1, ROWS1 = 528, 784, 800
O2_U, O2_D, O2_PLE, O2_PG, USED2, ROWS2 = 704, 1408, 2112, 2176, 2432, 2560
RED1, RED2 = 400, 640

ADAM_LR, ADAM_B1, ADAM_B2, ADAM_EPS, ADAM_WD, ADAM_STEP = 0.001, 0.9, 0.999, 1e-8, 0.01, 10

VMEM_LIMIT = 56 * 1024 * 1024


def _cp(**kw):
    return pltpu.CompilerParams(vmem_limit_bytes=VMEM_LIMIT, **kw)


def _mm(a, b):
    return jnp.dot(a.astype(BF), b.astype(BF), preferred_element_type=F32)


def _mm_nt(a, b):
    return lax.dot_general(a.astype(BF), b.astype(BF), (((1,), (1,)), ((), ())), preferred_element_type=F32)


def _mm_tn(a, b):
    return lax.dot_general(a.astype(BF), b.astype(BF), (((0,), (0,)), ((), ())), preferred_element_type=F32)


def _split2(x):
    hi = x.astype(BF)
    lo = (x - hi.astype(F32)).astype(BF)
    return hi, lo


def _split3(x):
    hi = x.astype(BF)
    r = x - hi.astype(F32)
    mid = r.astype(BF)
    lo = (r - mid.astype(F32)).astype(BF)
    return hi, mid, lo


def _dot3(m, x):
    hi, mid, lo = _split3(x)
    return (jnp.dot(m, hi, preferred_element_type=F32) + jnp.dot(m, mid, preferred_element_type=F32)
            + jnp.dot(m, lo, preferred_element_type=F32))


def _dot2(m, x):
    hi, lo = _split2(x)
    return jnp.dot(m, hi, preferred_element_type=F32) + jnp.dot(m, lo, preferred_element_type=F32)


def _rstd(x):
    return lax.rsqrt(jnp.mean(x * x, axis=-1, keepdims=True) + EPS)


def _rms_bwd(dy, x, g):
    r = _rstd(x)
    xh = x * r
    dg = jnp.sum(dy * xh, axis=0, keepdims=True)
    dxh = dy * g
    dx = r * (dxh - xh * jnp.mean(dxh * xh, axis=-1, keepdims=True))
    return dx, dg


def _sigmoid(x):
    return 1.0 / (1.0 + jnp.exp(-x))


ANY = pl.BlockSpec(memory_space=pl.ANY)


def _full(shape):
    n = len(shape)
    return pl.BlockSpec(shape, lambda *_: (0,) * n)


def _resident(shape):
    n = len(shape)
    return pl.BlockSpec(shape, lambda *_: (0,) * n, pipeline_mode=pl.Buffered(1))


def _tile(t):
    return 512 if t % 512 == 0 else t


def _tri(n, upper):
    r, c = np.indices((n, n))
    return ((c >= r) if upper else (c <= r)).astype(BF)


def _band(tt, transpose):
    r, c = np.indices((tt, tt + HALO))
    d = (c - r) if transpose else (r + HALO - c)
    return np.stack([((d >= 0) & (d < w)).astype(BF) for w in WINS])


def _aug_consts():
    row, col = np.indices((3 * LANE, NH * LANE))
    piece, head = row // LANE, row % LANE
    ch, cl = col // LANE, col % LANE
    eq = ((head == ch) & (cl == HD + piece)).astype(BF)
    ek = -((head == ch) & (cl == HD + 3 + piece)).astype(BF)
    lane = np.arange(NH * LANE)[None, :] % LANE
    rowq = ((lane >= HD + 3) & (lane < HD + 6)).astype(np.float32)
    rowk = ((lane >= HD) & (lane < HD + 3)).astype(np.float32)
    r2, c2 = np.indices((NH * LANE, LANE))
    selq = ((r2 // LANE == c2) & (r2 % LANE == HD)).astype(BF)
    selk = ((r2 // LANE == c2) & (r2 % LANE == HD + 3)).astype(BF)
    return eq, ek, rowq, rowk, selq, selk


def _in_proj(x, g1, w_in_t, b_pad, tri, eq, ek, rowq, rowk, dep):
    t = x.shape[0]
    tt = _tile(t)

    def body(x_ref, g_ref, w_ref, b_ref, tri_ref, eq_ref, ek_ref, rq_ref, rk_ref, dep_ref,
             qa_ref, ka_ref, v_ref, u_ref, fl_ref, carry):
        i = pl.program_id(0)

        @pl.when(i == 0)
        def _():
            carry[...] = jnp.zeros_like(carry)

        xv = x_ref[...]
        hn = (xv * _rstd(xv) * g_ref[...]).astype(BF)
        z = _mm_nt(hn, w_ref[...])
        fl = z[:, 3 * DA:3 * DA + LANE] + b_ref[...]
        lane = lax.broadcasted_iota(jnp.int32, fl.shape, 1)
        lf = jnp.where(lane < NH, jnp.minimum(fl, 0.0) - jnp.log(1.0 + jnp.exp(-jnp.abs(fl))), 0.0)
        c = carry[...] + _dot3(tri_ref[...], lf)
        carry[...] = carry[...] + jnp.sum(lf, axis=0, keepdims=True)
        caug = jnp.concatenate(_split3(c), axis=1)
        aug_q = jnp.dot(caug, eq_ref[...], preferred_element_type=F32) + rq_ref[...]
        aug_k = jnp.dot(caug, ek_ref[...], preferred_element_type=F32) + rk_ref[...]
        low = lax.broadcasted_iota(jnp.int32, (tt, LANE), 1) < HD
        for p in range(NH // 2):
            qp = z[:, LANE * p:LANE * (p + 1)] * (1.0 / math.sqrt(HD))
            kp = z[:, DA + LANE * p:DA + LANE * (p + 1)]
            for h, (qh, kh) in enumerate(((qp, kp), (pltpu.roll(qp, HD, 1), pltpu.roll(kp, HD, 1)))):
                lo_, hi_ = LANE * (2 * p + h), LANE * (2 * p + h + 1)
                qa_ref[:, lo_:hi_] = jnp.where(low, qh, aug_q[:, lo_:hi_]).astype(BF)
                ka_ref[:, lo_:hi_] = jnp.where(low, kh, aug_k[:, lo_:hi_]).astype(BF)
        v_ref[...] = z[:, 2 * DA:3 * DA].astype(BF)
        u_ref[...] = z[:, 3 * DA + LANE:]
        fl_ref[...] = fl

    return pl.pallas_call(
        body, name="in_proj", grid=(t // tt,),
        in_specs=[pl.BlockSpec((tt, D), lambda i: (i, 0)), _full((1, D)), _resident((ZW, D)), _full((1, LANE)),
                  _full((tt, tt)), _full((3 * LANE, NH * LANE)), _full((3 * LANE, NH * LANE)),
                  _full((1, NH * LANE)), _full((1, NH * LANE)), ANY],
        out_specs=[pl.BlockSpec((tt, NH * LANE), lambda i: (i, 0)), pl.BlockSpec((tt, NH * LANE), lambda i: (i, 0)),
                   pl.BlockSpec((tt, DA), lambda i: (i, 0)), pl.BlockSpec((tt, DP), lambda i: (i, 0)),
                   pl.BlockSpec((tt, LANE), lambda i: (i, 0))],
        out_shape=[jax.ShapeDtypeStruct((t, NH * LANE), BF), jax.ShapeDtypeStruct((t, NH * LANE), BF),
                   jax.ShapeDtypeStruct((t, DA), BF), jax.ShapeDtypeStruct((t, DP), F32),
                   jax.ShapeDtypeStruct((t, LANE), F32)],
        scratch_shapes=[pltpu.VMEM((1, LANE), F32)],
        compiler_params=_cp(),
    )(x, g1, w_in_t, b_pad, tri, eq, ek, rowq, rowk, dep)


def _attn_fwd(qa, ka, v):
    t = qa.shape[0]
    ta = _tile(t)
    n = t // ta

    def body(q_ref, k_ref, v_ref, a_ref, lse_ref, m_ref, l_ref, acc_ref):
        i = pl.program_id(1)
        m_ref[...] = jnp.full_like(m_ref, -1e30)
        l_ref[...] = jnp.zeros_like(l_ref)
        acc_ref[...] = jnp.zeros_like(acc_ref)
        qs = [q_ref[:, LANE * h:LANE * (h + 1)] for h in range(2)]
        reps = ta // LANE

        def tile(j, masked):
            rows = pl.ds(pl.multiple_of(j * ta, ta), ta)
            v2 = v_ref[rows, :]
            s = [_mm_nt(qs[h], k_ref[rows, LANE * h:LANE * (h + 1)]) for h in range(2)]
            if masked:
                keep = (lax.broadcasted_iota(jnp.int32, (ta, ta), 1) <= lax.broadcasted_iota(jnp.int32, (ta, ta), 0))
                s = [jnp.where(keep, sh, -1e30) for sh in s]
            m_old = [m_ref[h] for h in range(2)]
            m_new = [jnp.maximum(m_old[h], jnp.max(s[h], axis=1, keepdims=True)) for h in range(2)]
            pe = [jnp.exp(s[h] - jnp.tile(m_new[h], (1, reps))) for h in range(2)]
            alpha = [jnp.exp(m_old[h] - m_new[h]) for h in range(2)]
            pv = [jnp.dot(pe[h].astype(BF), v2, preferred_element_type=F32) for h in range(2)]
            for h in range(2):
                l_ref[h] = alpha[h] * l_ref[h] + jnp.sum(pe[h], axis=1, keepdims=True)
                acc_ref[h] = alpha[h] * acc_ref[h] + pv[h]
                m_ref[h] = m_new[h]

        def step(j, carry):
            tile(j, False)
            return carry

        lax.fori_loop(0, i, step, 0)
        tile(i, True)
        low = lax.broadcasted_iota(jnp.int32, (ta, LANE), 1) < HD
        a_ref[...] = jnp.where(low, acc_ref[0] / l_ref[0], acc_ref[1] / l_ref[1])
        lse_ref[...] = jnp.where(low, m_ref[0] + jnp.log(l_ref[0]), m_ref[1] + jnp.log(l_ref[1]))

    return pl.pallas_call(
        body, name="attn_fwd", grid=(NH // 2, n),
        in_specs=[pl.BlockSpec((ta, 2 * LANE), lambda p, i: (i, p)),
                  pl.BlockSpec((t, 2 * LANE), lambda p, i: (0, p)),
                  pl.BlockSpec((t, LANE), lambda p, i: (0, p))],
        out_specs=[pl.BlockSpec((ta, LANE), lambda p, i: (i, p)), pl.BlockSpec((ta, LANE), lambda p, i: (i, p))],
        out_shape=[jax.ShapeDtypeStruct((t, DA), F32), jax.ShapeDtypeStruct((t, DA), F32)],
        scratch_shapes=[pltpu.VMEM((2, ta, LANE), F32), pltpu.VMEM((2, ta, LANE), F32), pltpu.VMEM((2, ta, LANE), F32)],
        compiler_params=_cp(),
    )(qa, ka, v)


def _mix_out(a, u, x, band, w_pool, pool_scale, g_attn, g_pool, w_out, g_post, dep):
    t = a.shape[0]
    tt = _tile(t)
    hb = tt // HALO

    def body(a_ref, u_ref, up_ref, x_ref, band_ref, wp_ref, ps_ref, ga_ref, gp_ref, wo_ref, go_ref, dep_ref,
             yb_ref, m_ref, o_ref, h1_ref):
        i = pl.program_id(0)
        prev = up_ref[...] * jnp.where(i > 0, 1.0, 0.0)
        tok = i * tt + lax.broadcasted_iota(jnp.int32, (tt, PC), 0)
        ms = []
        for g, w in enumerate(WINS):
            ug = u_ref[:, PC * g:PC * (g + 1)]
            ext = jnp.concatenate([prev[:, PC * g:PC * (g + 1)], ug], axis=0)
            cnt = jnp.minimum(tok + 1, w).astype(F32)
            y = (_dot2(band_ref[g], ext) / cnt - ug).astype(BF)
            yb_ref[:, PC * g:PC * (g + 1)] = y
            ms.append(_mm(y, wp_ref[g]) * ps_ref[:, PC * g:PC * (g + 1)])
        m = jnp.concatenate(ms, axis=1)
        m_ref[...] = m
        av = a_ref[...]
        mix = jnp.concatenate([av * _rstd(av) * ga_ref[...], m * _rstd(m) * gp_ref[...]], axis=1)
        o = _mm(mix, wo_ref[...])
        o_ref[...] = o
        h1_ref[...] = x_ref[...] + o * _rstd(o) * go_ref[...]

    return pl.pallas_call(
        body, name="mix_out", grid=(t // tt,),
        in_specs=[pl.BlockSpec((tt, DA), lambda i: (i, 0)), pl.BlockSpec((tt, DP), lambda i: (i, 0)),
                  pl.BlockSpec((HALO, DP), lambda i: (jnp.maximum(i * hb - 1, 0), 0)),
                  pl.BlockSpec((tt, D), lambda i: (i, 0)), _full((len(WINS), tt, tt + HALO)),
                  _full((len(WINS), PC, PC)), _full((1, DP)), _full((1, DA)), _full((1, DP)),
                  _resident((D, D)), _full((1, D)), ANY],
        out_specs=[pl.BlockSpec((tt, DP), lambda i: (i, 0)), pl.BlockSpec((tt, DP), lambda i: (i, 0)),
                   pl.BlockSpec((tt, D), lambda i: (i, 0)), pl.BlockSpec((tt, D), lambda i: (i, 0))],
        out_shape=[jax.ShapeDtypeStruct((t, DP), BF), jax.ShapeDtypeStruct((t, DP), F32),
                   jax.ShapeDtypeStruct((t, D), F32), jax.ShapeDtypeStruct((t, D), F32)],
        compiler_params=_cp(),
    )(a, u, u, x, band, w_pool, pool_scale, g_attn, g_pool, w_out, g_post, dep)


def _ffn_fwd(h1, g_pre, wg_t, wu_t, wd, g_post, p, w_ple, g_ple, w_pg, tgt):
    t = h1.shape[0]
    tt = 256 if t % 256 == 0 else t

    def body(h1_ref, gpre_ref, wg_ref, wu_ref, wd_ref, gpost_ref, p_ref, wple_ref, gple_ref, wpg_ref, tgt_ref,
             hn_ref, gate_ref, up_ref, dff_ref, dh2_ref, loss_ref, dwpg_ref, dwple_ref, dgple_ref, dgpost_ref):
        i = pl.program_id(0)

        @pl.when(i == 0)
        def _():
            loss_ref[...] = jnp.zeros_like(loss_ref)
            dwpg_ref[...] = jnp.zeros_like(dwpg_ref)
            dwple_ref[...] = jnp.zeros_like(dwple_ref)
            dgple_ref[...] = jnp.zeros_like(dgple_ref)
            dgpost_ref[...] = jnp.zeros_like(dgpost_ref)

        h1v = h1_ref[...]
        hn = (h1v * _rstd(h1v) * gpre_ref[...]).astype(BF)
        hn_ref[...] = hn
        gate = _mm_nt(hn, wg_ref[...])
        up = _mm_nt(hn, wu_ref[...])
        gate_ref[...] = gate.astype(BF)
        up_ref[...] = up.astype(BF)
        ff = _mm(gate * _sigmoid(gate) * up, wd_ref[...])
        rff = _rstd(ff)
        ffh = ff * rff
        gpost = gpost_ref[...]
        h2 = h1v + ffh * gpost
        pv = p_ref[...]
        pe = _mm(pv, wple_ref[...])
        rpe = _rstd(pe)
        peh = pe * rpe
        gple = gple_ref[...]
        e = peh * gple
        sig = _sigmoid(_mm(h2, wpg_ref[...]))
        dv = h2 + sig * e - tgt_ref[...]
        sq = jnp.sum(jnp.sum(dv * dv, axis=1, keepdims=True), axis=0, keepdims=True)
        loss_ref[...] = loss_ref[...] + sq
        dy = dv * (1.0 / D)
        d_e = dy * sig
        d_gl = dy * e * sig * (1.0 - sig)
        dh2 = dy + _mm_nt(d_gl, wpg_ref[...])
        dh2_ref[...] = dh2
        dwpg_ref[...] = dwpg_ref[...] + _mm_tn(h2, d_gl)
        dgple_ref[...] = dgple_ref[...] + jnp.sum(d_e * peh, axis=0, keepdims=True)
        dpeh = d_e * gple
        d_pe = rpe * (dpeh - peh * jnp.mean(dpeh * peh, axis=-1, keepdims=True))
        dwple_ref[...] = dwple_ref[...] + _mm_tn(pv, d_pe)
        dgpost_ref[...] = dgpost_ref[...] + jnp.sum(dh2 * ffh, axis=0, keepdims=True)
        dffh = dh2 * gpost
        dff_ref[...] = (rff * (dffh - ffh * jnp.mean(dffh * ffh, axis=-1, keepdims=True))).astype(BF)

    row = lambda w: pl.BlockSpec((tt, w), lambda i: (i, 0))
    return pl.pallas_call(
        body, name="ffn_fwd", grid=(t // tt,),
        in_specs=[row(D), _full((1, D)), _resident((DFF, D)), _resident((DFF, D)), _resident((DFF, D)), _full((1, D)),
                  row(DPLE), _resident((DPLE, D)), _full((1, D)), _resident((D, D)), row(D)],
        out_specs=[row(D), row(DFF), row(DFF), row(D), row(D), _full((8, LANE)), _full((D, D)), _full((DPLE, D)),
                   _full((1, D)), _full((1, D))],
        out_shape=[jax.ShapeDtypeStruct((t, D), BF), jax.ShapeDtypeStruct((t, DFF), BF), jax.ShapeDtypeStruct((t, DFF), BF),
                   jax.ShapeDtypeStruct((t, D), BF), jax.ShapeDtypeStruct((t, D), F32), jax.ShapeDtypeStruct((8, LANE), F32),
                   jax.ShapeDtypeStruct((D, D), F32), jax.ShapeDtypeStruct((DPLE, D), F32),
                   jax.ShapeDtypeStruct((1, D), F32), jax.ShapeDtypeStruct((1, D), F32)],
        compiler_params=_cp(),
    )(h1, g_pre, wg_t, wu_t, wd, g_post, p, w_ple, g_ple, w_pg, tgt)


FF_CH = 256


def _ffn_bwd(hn2, gate, up, dff, wgu, wd):
    t = hn2.shape[0]
    tt = 1024 if t % 1024 == 0 else _tile(t)
    nt = t // tt
    ch = FF_CH
    nc = DFF // ch

    def body(hn_ref, gate_ref, up_ref, dff_ref, wgu_ref, wd_ref,
             dwgu_ref, dwd_ref, dhn_ref, acc, sem):
        j, i = pl.program_id(0), pl.program_id(1)
        gate_v = gate_ref[...].astype(F32)
        up_v = up_ref[...].astype(F32)
        dffv = dff_ref[...]
        sg = _sigmoid(gate_v)
        silu = gate_v * sg
        d_act = _mm_nt(dffv, wd_ref[...])
        d_up = (d_act * silu).astype(BF)
        d_gate = (d_act * up_v * (sg * (1.0 + gate_v * (1.0 - sg)))).astype(BF)
        dgu = jnp.concatenate([d_gate, d_up], axis=1)

        @pl.when(i == 0)
        def _():
            dwgu_ref[...] = jnp.zeros_like(dwgu_ref)
            dwd_ref[...] = jnp.zeros_like(dwd_ref)

        dwd_ref[...] = dwd_ref[...] + _mm_tn(silu * up_v, dffv)
        dwgu_ref[0] = dwgu_ref[0] + _mm_tn(dgu, hn_ref[...])
        contrib = jnp.dot(dgu, wgu_ref[0], preferred_element_type=F32)
        rows = pl.ds(pl.multiple_of(i * tt, tt), tt)

        @pl.when(j == 0)
        def _():
            acc[rows, :] = contrib

        @pl.when(j > 0)
        def _():
            acc[rows, :] = acc[rows, :] + contrib

        @pl.when((j == nc - 1) & (i == nt - 1))
        def _():
            cp = pltpu.make_async_copy(acc, dhn_ref, sem)
            cp.start()
            cp.wait()

    tok = lambda w: pl.BlockSpec((tt, w), lambda j, i: (i, 0))
    chunk = pl.BlockSpec((ch, D), lambda j, i: (j, 0))
    pair = pl.BlockSpec((1, 2 * ch, D), lambda j, i: (j, 0, 0))
    return pl.pallas_call(
        body, name="ffn_bwd", grid=(nc, nt),
        in_specs=[tok(D), pl.BlockSpec((tt, ch), lambda j, i: (i, j)), pl.BlockSpec((tt, ch), lambda j, i: (i, j)),
                  tok(D), pair, chunk],
        out_specs=[pair, chunk, pl.BlockSpec(memory_space=pl.ANY)],
        out_shape=[jax.ShapeDtypeStruct((nc, 2 * ch, D), F32), jax.ShapeDtypeStruct((DFF, D), F32),
                   jax.ShapeDtypeStruct((t, D), F32)],
        scratch_shapes=[pltpu.VMEM((t, D), F32), pltpu.SemaphoreType.DMA],
        compiler_params=_cp(),
    )(hn2, gate, up, dff, wgu, wd)


def _mix_bwd(d_hn2, dh2, h1, o, a, m, yb, g_ffn_pre, g_post, g_attn, g_pool, w_out, w_pool, pool_scale, dep):
    t = a.shape[0]
    tt = 256 if t % 256 == 0 else t

    def body(dhn_ref, dh2_ref, h1_ref, o_ref, a_ref, m_ref, yb_ref, gfp_ref, go_ref, ga_ref, gp_ref, wo_ref, wp_ref,
             ps_ref, dep_ref, dh1_ref, da_ref, dyc_ref, dgfp_ref, dgo_ref, dga_ref, dgp_ref, dps_ref, dwp_ref, dwo_ref):
        i = pl.program_id(0)

        @pl.when(i == 0)
        def _():
            for r in (dgfp_ref, dgo_ref, dga_ref, dgp_ref, dps_ref, dwp_ref, dwo_ref):
                r[...] = jnp.zeros_like(r)

        d1, dg = _rms_bwd(dhn_ref[...], h1_ref[...], gfp_ref[...])
        dgfp_ref[...] = dgfp_ref[...] + dg
        dh1 = dh2_ref[...] + d1
        dh1_ref[...] = dh1
        d_o, dg = _rms_bwd(dh1, o_ref[...], go_ref[...])
        dgo_ref[...] = dgo_ref[...] + dg
        d_mix = _mm_nt(d_o, wo_ref[...])
        av, mv = a_ref[...], m_ref[...]
        mix = jnp.concatenate([av * _rstd(av) * ga_ref[...], mv * _rstd(mv) * gp_ref[...]], axis=1)
        dwo_ref[...] = dwo_ref[...] + _mm_tn(mix, d_o)
        d_a, dg = _rms_bwd(d_mix[:, :DA], av, ga_ref[...])
        dga_ref[...] = dga_ref[...] + dg
        da_ref[...] = d_a
        d_m, dg = _rms_bwd(d_mix[:, DA:], mv, gp_ref[...])
        dgp_ref[...] = dgp_ref[...] + dg
        tok = i * tt + lax.broadcasted_iota(jnp.int32, (tt, PC), 0)
        dps = []
        for g, w in enumerate(WINS):
            sl = slice(PC * g, PC * (g + 1))
            ybg = yb_ref[:, sl]
            wpg = wp_ref[g].astype(BF)
            mlin = jnp.dot(ybg, wpg, preferred_element_type=F32)
            dmg = d_m[:, sl]
            dps.append(jnp.sum(dmg * mlin, axis=0, keepdims=True))
            dml = (dmg * ps_ref[:, sl]).astype(BF)
            dwp_ref[g] = dwp_ref[g] + _mm_tn(ybg, dml)
            dyc_ref[:, sl] = _mm_nt(dml, wpg) / jnp.minimum(tok + 1, w).astype(F32)
        dps_ref[...] = dps_ref[...] + jnp.concatenate(dps, axis=1)

    row = lambda w: pl.BlockSpec((tt, w), lambda i: (i, 0))
    return pl.pallas_call(
        body, name="mix_bwd", grid=(t // tt,),
        in_specs=[row(D), row(D), row(D), row(D), row(DA), row(DP), row(DP), _full((1, D)), _full((1, D)),
                  _full((1, DA)), _full((1, DP)), _resident((D, D)), _full((len(WINS), PC, PC)), _full((1, DP)), ANY],
        out_specs=[row(D), row(DA), row(DP), _full((1, D)), _full((1, D)), _full((1, DA)), _full((1, DP)),
                   _full((1, DP)), _full((len(WINS), PC, PC)), _full((D, D))],
        out_shape=[jax.ShapeDtypeStruct((t, D), F32), jax.ShapeDtypeStruct((t, DA), F32), jax.ShapeDtypeStruct((t, DP), F32),
                   jax.ShapeDtypeStruct((1, D), F32), jax.ShapeDtypeStruct((1, D), F32), jax.ShapeDtypeStruct((1, DA), F32),
                   jax.ShapeDtypeStruct((1, DP), F32), jax.ShapeDtypeStruct((1, DP), F32),
                   jax.ShapeDtypeStruct((len(WINS), PC, PC), F32), jax.ShapeDtypeStruct((D, D), F32)],
        compiler_params=_cp(),
    )(d_hn2, dh2, h1, o, a, m, yb, g_ffn_pre, g_post, g_attn, g_pool, w_out, w_pool, pool_scale, dep)


def _attn_bwd(qa, ka, v, a, d_a, lse, dep):
    t = qa.shape[0]
    ta = _tile(t)
    n = t // ta

    def body(q_ref, k_ref, v_ref, o_ref, do_ref, lse_ref, dep_ref, dq_ref, dk_ref, dv_ref):
        j = pl.program_id(1)

        @pl.when(j == 0)
        def _():
            dq_ref[...] = jnp.zeros_like(dq_ref)

        dk_ref[...] = jnp.zeros_like(dk_ref)
        dv_ref[...] = jnp.zeros_like(dv_ref)
        ks = [k_ref[:, LANE * h:LANE * (h + 1)] for h in range(2)]
        v2 = v_ref[...]
        lane = lax.broadcasted_iota(jnp.int32, (ta, LANE), 1)
        mine = [lane < HD, lane >= HD]

        def tile(i, masked):
            rows = pl.ds(pl.multiple_of(i * ta, ta), ta)
            do2 = do_ref[rows, :]
            prod = do2 * o_ref[rows, :]
            lse2 = lse_ref[rows, :]
            do2b = do2.astype(BF)
            qh = [q_ref[rows, LANE * h:LANE * (h + 1)] for h in range(2)]
            s = [_mm_nt(qh[h], ks[h]) for h in range(2)]
            dp = [_mm_nt(jnp.where(mine[h], do2, 0.0), v2) for h in range(2)]
            delta = [jnp.sum(jnp.where(mine[h], prod, 0.0), axis=1, keepdims=True) for h in range(2)]
            lse_h = [jnp.sum(jnp.where(lane == HD * h, lse2, 0.0), axis=1, keepdims=True) for h in range(2)]
            pr = [jnp.exp(s[h] - lse_h[h]) for h in range(2)]
            if masked:
                keep = (lax.broadcasted_iota(jnp.int32, (ta, ta), 1) <= lax.broadcasted_iota(jnp.int32, (ta, ta), 0))
                pr = [jnp.where(keep, ph, 0.0) for ph in pr]
            ds = [(pr[h] * (dp[h] - delta[h])).astype(BF) for h in range(2)]
            dv_ref[...] = dv_ref[...] + jnp.where(mine[0], _mm_tn(pr[0], do2b), _mm_tn(pr[1], do2b))
            for h in range(2):
                sl = slice(LANE * h, LANE * (h + 1))
                dk_ref[:, sl] = dk_ref[:, sl] + _mm_tn(ds[h], qh[h])
                dq_ref[0, rows, sl] = dq_ref[0, rows, sl] + jnp.dot(ds[h], ks[h], preferred_element_type=F32)

        def step(i, carry):
            tile(i, False)
            return carry

        tile(j, True)
        lax.fori_loop(j + 1, n, step, 0)

    qrow = lambda w: pl.BlockSpec((t, w), lambda p, j: (0, p))
    krow = lambda w: pl.BlockSpec((ta, w), lambda p, j: (j, p))
    return pl.pallas_call(
        body, name="attn_bwd", grid=(NH // 2, n),
        in_specs=[qrow(2 * LANE), krow(2 * LANE), krow(LANE), qrow(LANE), qrow(LANE), qrow(LANE), ANY],
        out_specs=[pl.BlockSpec((1, t, 2 * LANE), lambda p, j: (p, 0, 0)), krow(2 * LANE), krow(LANE)],
        out_shape=[jax.ShapeDtypeStruct((NH // 2, t, 2 * LANE), F32), jax.ShapeDtypeStruct((t, NH * LANE), F32),
                   jax.ShapeDtypeStruct((t, DA), F32)],
        compiler_params=_cp(),
    )(qa, ka, v, a, d_a, lse, dep)


def _in_bwd(dqa, dka, dv, dyc, fl, x, dh1, g1, w_in_t, tri_u, selq, selk, band_t, dep):
    t = x.shape[0]
    tt = _tile(t)
    nt = t // tt
    hb = tt // HALO
    rev = lambda s: nt - 1 - s

    def body(dqa_ref, dka_ref, dv_ref, dyc_ref, dyn_ref, fl_ref, x_ref, dh1_ref, g_ref, w_ref, tri_ref, sq_ref, sk_ref,
             band_ref, dep_ref, dx_ref, dw_ref, dg_ref, db_ref, carry, acc, sem):
        s = pl.program_id(0)
        i = nt - 1 - s

        @pl.when(s == 0)
        def _():
            carry[...] = jnp.zeros_like(carry)
            dg_ref[...] = jnp.zeros_like(dg_ref)
            db_ref[...] = jnp.zeros_like(db_ref)

        dq_cat = jnp.concatenate([dqa_ref[p] for p in range(NH // 2)], axis=1)
        dk_cat = dka_ref[...]
        dc = _dot2t(dq_cat, sq_ref[...]) - _dot2t(dk_cat, sk_ref[...])
        dlf = carry[...] + _dot3(tri_ref[...], dc)
        carry[...] = carry[...] + jnp.sum(dc, axis=0, keepdims=True)
        flv = fl_ref[...]
        lane = lax.broadcasted_iota(jnp.int32, flv.shape, 1)
        d_fl = jnp.where(lane < NH, dlf / (1.0 + jnp.exp(flv)), 0.0)
        db_ref[...] = db_ref[...] + jnp.sum(d_fl, axis=0, keepdims=True)
        low = lax.broadcasted_iota(jnp.int32, (tt, LANE), 1) < HD
        dqs, dks = [], []
        for p in range(NH // 2):
            b0, b1 = slice(2 * LANE * p, 2 * LANE * p + LANE), slice(2 * LANE * p + LANE, 2 * LANE * (p + 1))
            dqs.append(jnp.where(low, dq_cat[:, b0], pltpu.roll(dq_cat[:, b1], HD, 1)) * (1.0 / math.sqrt(HD)))
            dks.append(jnp.where(low, dk_cat[:, b0], pltpu.roll(dk_cat[:, b1], HD, 1)))
        nxt = dyn_ref[...] * jnp.where(i < nt - 1, 1.0, 0.0)
        tok = i * tt + lax.broadcasted_iota(jnp.int32, (tt, PC), 0)
        dus = []
        for g, w in enumerate(WINS):
            sl = slice(PC * g, PC * (g + 1))
            dycg = dyc_ref[:, sl]
            ext = jnp.concatenate([dycg, nxt[:, sl]], axis=0)
            dus.append(_dot2(band_ref[g], ext) - dycg * jnp.minimum(tok + 1, w).astype(F32))
        d_z = jnp.concatenate(dqs + dks + [dv_ref[...], d_fl] + dus, axis=1).astype(BF)
        xv = x_ref[...]
        gv = g_ref[...]
        hn = (xv * _rstd(xv) * gv).astype(BF)
        d_hn = jnp.dot(d_z, w_ref[...], preferred_element_type=F32)
        contrib = _mm_tn(d_z, hn)

        @pl.when(s == 0)
        def _():
            acc[...] = contrib

        @pl.when(s > 0)
        def _():
            acc[...] = acc[...] + contrib

        d1, dg = _rms_bwd(d_hn, xv, gv)
        dg_ref[...] = dg_ref[...] + dg
        dx_ref[...] = dh1_ref[...] + d1

        @pl.when(s == nt - 1)
        def _():
            cp = pltpu.make_async_copy(acc, dw_ref, sem)
            cp.start()
            cp.wait()

    row = lambda w: pl.BlockSpec((tt, w), lambda s: (rev(s), 0))
    return pl.pallas_call(
        body, name="in_bwd", grid=(nt,),
        in_specs=[pl.BlockSpec((NH // 2, tt, 2 * LANE), lambda s: (0, rev(s), 0)), row(NH * LANE), row(DA), row(DP),
                  pl.BlockSpec((HALO, DP), lambda s: (jnp.minimum((rev(s) + 1) * hb, nt * hb - 1), 0)),
                  row(LANE), row(D), row(D), _full((1, D)), _resident((ZW, D)), _full((tt, tt)),
                  _full((NH * LANE, LANE)), _full((NH * LANE, LANE)), _full((len(WINS), tt, tt + HALO)), ANY],
        out_specs=[row(D), pl.BlockSpec(memory_space=pl.ANY), _full((1, D)), _full((1, LANE))],
        out_shape=[jax.ShapeDtypeStruct((t, D), F32), jax.ShapeDtypeStruct((ZW, D), F32),
                   jax.ShapeDtypeStruct((1, D), F32), jax.ShapeDtypeStruct((1, LANE), F32)],
        scratch_shapes=[pltpu.VMEM((1, LANE), F32), pltpu.VMEM((ZW, D), F32), pltpu.SemaphoreType.DMA],
        compiler_params=_cp(),
    )(dqa, dka, dv, dyc, dyc, fl, x, dh1, g1, w_in_t, tri_u, selq, selk, band_t, dep)


def _dot2t(x, sel):
    hi, lo = _split2(x)
    return jnp.dot(hi, sel, preferred_element_type=F32) + jnp.dot(lo, sel, preferred_element_type=F32)


class _NoComm:
    def __init__(self, w2):
        self.w2 = w2
        self.dep = jnp.zeros((8, LANE), F32)

    def after_attention(self, after):
        return self.dep

    def weights2(self, after):
        return self.w2

    def after_ffn(self, grads2):
        self.grads2 = grads2
        return self.dep

    def after_mix(self, after, early):
        self.early = early
        return self.dep

    def after_attn(self, after):
        return self.dep


def _local_step(x, p, tgt, sm, w1, comm):
    w_in_t, w_out = w1
    tt = _tile(x.shape[0])
    eq, ek, rowq, rowk, selq, selk = _aug_consts()
    b_pad = jnp.pad(sm["b_forget"], ((0, 0), (0, LANE - NH)))
    qa, ka, v, u, fl = _in_proj(x, sm["g_mix_pre"], w_in_t, b_pad, _tri(tt, False), eq, ek, rowq, rowk, comm.dep)
    a, lse = _attn_fwd(qa, ka, v)
    yb, m, o, h1 = _mix_out(a, u, x, _band(tt, False), sm["w_pool"], sm["pool_scale"], sm["g_attn_grp"],
                            sm["g_pool_grp"], w_out, sm["g_mix_post"], comm.after_attention(a))
    wg_t, wu_t, wd, w_ple, w_pg = comm.weights2(h1)
    hn2, gate, up, dff, dh2, loss, dwpg, dwple, dgple, dgfpost = _ffn_fwd(
        h1, sm["g_ffn_pre"], wg_t, wu_t, wd, sm["g_ffn_post"], p, w_ple, sm["g_ple"], w_pg, tgt)
    chunks = lambda a: a.reshape(DFF // FF_CH, FF_CH, D)
    dwgu, dwd, d_hn2 = _ffn_bwd(hn2, gate, up, dff, jnp.concatenate([chunks(wg_t), chunks(wu_t)], axis=1), wd)
    dwg_t, dwu_t = dwgu[:, :FF_CH].reshape(DFF, D), dwgu[:, FF_CH:].reshape(DFF, D)
    dep = comm.after_ffn((dwg_t, dwu_t, dwd, dwple, dwpg))
    dh1, d_a, dyc, dgfpre, dgpost, dgattn, dgpool, dps, dwpool, dwout = _mix_bwd(
        d_hn2, dh2, h1, o, a, m, yb, sm["g_ffn_pre"], sm["g_mix_post"], sm["g_attn_grp"], sm["g_pool_grp"],
        w_out, sm["w_pool"], sm["pool_scale"], dep)
    early = dict(loss=loss[0:1, 0:1], g_attn_grp=dgattn, g_pool_grp=dgpool, w_pool=dwpool, pool_scale=dps,
                 g_mix_post=dgpost, g_ffn_pre=dgfpre, g_ffn_post=dgfpost, g_ple=dgple)
    dqa, dka, dvv = _attn_bwd(qa, ka, v, a, d_a, lse, comm.after_mix(dh1, early))
    dx, dwin_t, dg1, dbf = _in_bwd(dqa, dka, dvv, dyc, fl, x, dh1, sm["g_mix_pre"], w_in_t, _tri(tt, True),
                                   selq, selk, _band(tt, True), comm.after_attn(dvv))
    return dx, (dwin_t, dwout), dict(g_mix_pre=dg1, b_forget=dbf[:, :NH])


def _place():
    x, y, c = lax.axis_index("x"), lax.axis_index("y"), lax.axis_index("c")
    return x, y, c, [(1 - x, y), (x, 1 - y), (1 - x, 1 - y)]


def _rows(c, h):
    return pl.ds(pl.multiple_of(c * h, 16), h)


def _plan_gather(h):
    def plan(src, land):
        x, y, c, chips = _place()
        return [(src.at[_rows(c, h), :], land.at[2 * x + y, _rows(c, h), :], (cx, cy, c),
                 land.at[2 * cx + cy, _rows(c, h), :]) for cx, cy in chips]
    return plan


def _plan_forward(h):
    def plan(land_in, own, land):
        x, y, c, chips = _place()
        sib, me = (x, y, 1 - c), 2 * x + y
        return ([(land_in.at[2 * cx + cy, _rows(c, h), :], land.at[2 * cx + cy, _rows(c, h), :], sib,
                  land.at[2 * cx + cy, _rows(1 - c, h), :]) for cx, cy in chips]
                + [(own, land.at[me], sib, land.at[me])])
    return plan


def _plan_swap_halves(h):
    def plan(buf_in, buf):
        x, y, c, _ = _place()
        return [(buf_in.at[_rows(c, h), :], buf.at[_rows(c, h), :], (x, y, 1 - c), buf.at[_rows(1 - c, h), :])]
    return plan


def _plan_pair_rows(h):
    def plan(src, land):
        x, y, c, _ = _place()
        return [(src.at[:, _rows(1 - c, h), :], land, (x, y, 1 - c), land)]
    return plan


def _plan_scatter(src, land):
    x, y, c, chips = _place()
    return [(src.at[2 * cx + cy], land.at[k], (cx, cy, c), land.at[k]) for k, (cx, cy) in enumerate(chips)]


def _plan_all(src, land):
    x, y, c, _ = _place()
    copies = []
    for r in range(1, 8):
        px, py, pc = (1 - a if b else a for a, b in zip((x, y, c), (r >> 2 & 1, r >> 1 & 1, r & 1)))
        copies.append((src, land.at[4 * x + 2 * y + c], (px, py, pc), land.at[4 * px + 2 * py + pc]))
    return copies


def _remote(src, dst, send_sems, recv_sems, k, peer):
    return pltpu.make_async_remote_copy(src_ref=src, dst_ref=dst, send_sem=send_sems.at[k], recv_sem=recv_sems.at[k],
                                        device_id=peer, device_id_type=MESH)


def _exchange(name, n, src, land, plan):
    def body(src_ref, land_ref, send_sems, recv_sems):
        copies = plan(src_ref, land_ref)
        for k, (s, d, peer, _) in enumerate(copies):
            _remote(s, d, send_sems, recv_sems, k, peer).start()
        for k, (s, _, peer, mine) in enumerate(copies):
            _remote(s, mine, send_sems, recv_sems, k, peer).wait_recv()
        for k, (s, d, peer, _) in enumerate(copies):
            _remote(s, d, send_sems, recv_sems, k, peer).wait_send()

    return pl.pallas_call(
        body, name=name, in_specs=[ANY], out_specs=ANY, out_shape=land,
        scratch_shapes=[pltpu.SemaphoreType.DMA((n,)), pltpu.SemaphoreType.DMA((n,))],
    )(src)


def _exchange_inplace(name, n, buf, extra, plan):
    def body(*refs):
        ins, buf_ref, send_sems, recv_sems = refs[:1 + len(extra)], refs[1 + len(extra)], refs[-2], refs[-1]
        copies = plan(*ins, buf_ref)
        for k, (s, d, peer, _) in enumerate(copies):
            _remote(s, d, send_sems, recv_sems, k, peer).start()
        for k, (s, _, peer, mine) in enumerate(copies):
            _remote(s, mine, send_sems, recv_sems, k, peer).wait_recv()
        for k, (s, d, peer, _) in enumerate(copies):
            _remote(s, d, send_sems, recv_sems, k, peer).wait_send()

    return pl.pallas_call(
        body, name=name, in_specs=[ANY] * (1 + len(extra)), out_specs=ANY, out_shape=_sds(buf.shape, buf.dtype),
        input_output_aliases={0: 0},
        scratch_shapes=[pltpu.SemaphoreType.DMA((n,)), pltpu.SemaphoreType.DMA((n,))],
    )(buf, *extra)


HBM = pl.BlockSpec(memory_space=pltpu.HBM)
SEM = pl.BlockSpec(memory_space=pltpu.SEMAPHORE)
EFFECT = pltpu.SideEffectType.DATAFLOW_SIDE_EFFECTING


def _exchange_start(name, n, src, land, plan):
    def body(src_ref, land_ref, send_sems, recv_sems, src_thru, land_thru, token):
        for k, (s, d, peer, _) in enumerate(plan(src_ref, land_ref)):
            _remote(s, d, send_sems, recv_sems, k, peer).start()
        token[...] = jnp.zeros_like(token)

    return pl.pallas_call(
        body, name=name,
        out_shape=(pltpu.SemaphoreType.DMA((n,)), pltpu.SemaphoreType.DMA((n,)), pltpu.HBM(src.shape, src.dtype),
                   pltpu.HBM(land.shape, land.dtype), jax.ShapeDtypeStruct((8, LANE), F32)),
        in_specs=(HBM, HBM), out_specs=(SEM, SEM, HBM, HBM, pl.BlockSpec(memory_space=pltpu.VMEM)),
        input_output_aliases={0: 2, 1: 3},
        compiler_params=pltpu.CompilerParams(has_side_effects=EFFECT),
    )(pltpu.with_memory_space_constraint(src, pltpu.HBM), pltpu.with_memory_space_constraint(land, pltpu.HBM))


def _exchange_wait(name, started, plan, after):
    send_sems, recv_sems, src, land, _ = started

    def body(src_ref, land_ref, send_sems, recv_sems, after_ref, src_out, land_out):
        for k, (s, _, peer, mine) in enumerate(plan(src_ref, land_ref)):
            cp = _remote(s, mine, send_sems, recv_sems, k, peer)
            cp.wait_send()
            cp.wait_recv()

    return pl.pallas_call(
        body, name=name, out_shape=(pltpu.HBM(src.shape, src.dtype), pltpu.HBM(land.shape, land.dtype)),
        in_specs=(HBM, HBM, SEM, SEM, ANY), out_specs=(HBM, HBM), input_output_aliases={0: 0, 1: 1},
        compiler_params=pltpu.CompilerParams(has_side_effects=EFFECT),
    )(src, land, send_sems, recv_sems, after)


def _pair_sum(name, cidx, g, recv, br):
    h = recv.shape[1]
    nb = h // br

    def body(c_ref, g_ref, r_ref, out_ref):
        out_ref[...] = (g_ref[...] + r_ref[...]).astype(BF)

    return pl.pallas_call(
        body, name=name,
        grid_spec=pltpu.PrefetchScalarGridSpec(
            num_scalar_prefetch=1, grid=(NSHARD, nb),
            in_specs=[pl.BlockSpec((1, br, D), lambda s, i, c: (s, c[0] * nb + i, 0)),
                      pl.BlockSpec((1, br, D), lambda s, i, c: (s, i, 0))],
            out_specs=pl.BlockSpec((1, br, D), lambda s, i, c: (s, i, 0))),
        out_shape=jax.ShapeDtypeStruct((NSHARD, h, D), BF),
    )(cidx, g, recv)


def _chip_sum(name, place, pb, y, br):
    h = y.shape[1]
    nb = h // br

    def body(pl_ref, p_ref, y_ref, out_ref):
        acc = p_ref[0].astype(F32)
        for k in range(NSHARD - 1):
            acc = acc + y_ref[k].astype(F32)
        out_ref[...] = acc

    return pl.pallas_call(
        body, name=name,
        grid_spec=pltpu.PrefetchScalarGridSpec(
            num_scalar_prefetch=1, grid=(nb,),
            in_specs=[pl.BlockSpec((1, br, D), lambda i, s: (s[0], i, 0)),
                      pl.BlockSpec((NSHARD - 1, br, D), lambda i, s: (0, i, 0))],
            out_specs=pl.BlockSpec((br, D), lambda i, s: (s[1] * nb + i, 0))),
        out_shape=jax.ShapeDtypeStruct((2 * h, D), F32),
    )(place, pb, y)


def _sum_slots(v):
    def body(in_ref, out_ref):
        acc = in_ref[0]
        for k in range(1, 8):
            acc = acc + in_ref[k]
        out_ref[...] = acc

    vm = pl.BlockSpec(memory_space=pltpu.VMEM)
    return pl.pallas_call(body, name="sum_slots", in_specs=[vm], out_specs=vm,
                          out_shape=jax.ShapeDtypeStruct(v.shape[1:], F32))(v)


def _all_reduce_small(v):
    rows = v.shape[0]

    def body(in_ref, out_ref, buf, send_sems, recv_sems):
        x, y, c, _ = _place()
        me = 4 * x + 2 * y + c
        flips = [(r >> 2 & 1, r >> 1 & 1, r & 1) for r in range(1, 8)]

        def peer(f):
            return tuple(1 - a if b else a for a, b in zip((x, y, c), f))

        def copy(k, slot, to):
            return pltpu.make_async_remote_copy(src_ref=in_ref, dst_ref=buf.at[slot], send_sem=send_sems.at[k],
                                                recv_sem=recv_sems.at[k], device_id=to, device_id_type=MESH)

        sends = [copy(k, me, peer(f)) for k, f in enumerate(flips)]
        for cp in sends:
            cp.start()
        buf[me] = in_ref[...]
        for k, f in enumerate(flips):
            px, py, pc = peer(f)
            copy(k, 4 * px + 2 * py + pc, peer(f)).wait_recv()
        for cp in sends:
            cp.wait_send()
        acc = buf[0]
        for k in range(1, 8):
            acc = acc + buf[k]
        out_ref[...] = acc

    vm = pl.BlockSpec(memory_space=pltpu.VMEM)
    return pl.pallas_call(
        body, name="all_reduce_small", in_specs=[vm], out_specs=vm,
        out_shape=jax.ShapeDtypeStruct(v.shape, F32),
        scratch_shapes=[pltpu.VMEM((8, rows, LANE), F32), pltpu.SemaphoreType.DMA((7,)), pltpu.SemaphoreType.DMA((7,))],
    )(v)


def _adamw_math(w, g, m, v):
    m = ADAM_B1 * m + (1.0 - ADAM_B1) * g
    v = ADAM_B2 * v + (1.0 - ADAM_B2) * (g * g)
    m_hat = m / (1.0 - ADAM_B1 ** ADAM_STEP)
    v_hat = v / (1.0 - ADAM_B2 ** ADAM_STEP)
    delta = -ADAM_LR * (m_hat / (jnp.sqrt(v_hat) + ADAM_EPS) + ADAM_WD * w)
    return delta, m, v


def _adamw(w, g, m, v, dep):
    r, c = w.shape
    br = next(b for b in (256, 176, r) if r % b == 0)

    def body(w_ref, g_ref, m_ref, v_ref, dep_ref, d_ref, nm_ref, nv_ref):
        d_ref[...], nm_ref[...], nv_ref[...] = _adamw_math(w_ref[...], g_ref[...], m_ref[...], v_ref[...])

    spec = pl.BlockSpec((br, c), lambda i: (i, 0))
    return pl.pallas_call(
        body, name="adamw", grid=(r // br,), in_specs=[spec] * 4 + [ANY], out_specs=[spec] * 3,
        out_shape=[jax.ShapeDtypeStruct((r, c), F32)] * 3, compiler_params=_cp(),
    )(w, g, m, v, dep)


def _adamw_small(ws, gs, ms, vs):
    n = len(ws)

    def body(*refs):
        ins, outs = refs[:4 * n], refs[4 * n:]
        for k in range(n):
            d, m, v = _adamw_math(ins[k][...], ins[n + k][...], ins[2 * n + k][...], ins[3 * n + k][...])
            outs[k][...] = d
            outs[n + k][...] = m
            outs[2 * n + k][...] = v

    vm = pl.BlockSpec(memory_space=pltpu.VMEM)
    out = pl.pallas_call(
        body, name="adamw_small", in_specs=[vm] * (4 * n), out_specs=[vm] * (3 * n),
        out_shape=[jax.ShapeDtypeStruct(w.shape, F32) for w in ws] * 3,
    )(*ws, *gs, *ms, *vs)
    return out[:n], out[n:2 * n], out[2 * n:]


BIG = ("w_in", "w_out", "w_ffn_gate", "w_ffn_up", "w_ffn_down", "w_ple_proj", "w_ple_gate")
SMALL = ("g_mix_pre", "b_forget", "g_attn_grp", "g_pool_grp", "w_pool", "pool_scale", "g_mix_post", "g_ffn_pre",
         "g_ffn_post", "g_ple")
TRANSPOSED = ("w_in", "w_ffn_gate", "w_ffn_up")
VECTORS = tuple(n for n in SMALL if n != "w_pool")
ORDER = ("g_mix_pre", "w_in", "b_forget", "g_attn_grp", "g_pool_grp", "w_pool", "pool_scale", "w_out", "g_mix_post",
         "g_ffn_pre", "w_ffn_gate", "w_ffn_up", "w_ffn_down", "g_ffn_post", "w_ple_proj", "g_ple", "w_ple_gate")


def _pad_rows(a, rows):
    return jnp.pad(a, ((0, rows - a.shape[0]), (0, 0)))


def _stack1(w_in, w_out):
    return _pad_rows(jnp.concatenate([_pad_rows(w_in.T, IN_PAD), w_out], axis=0), ROWS1)


def _stack2(wg, wu, wd, wple, wpg):
    return _pad_rows(jnp.concatenate([wg.T, wu.T, wd, wple.reshape(DPLE // NSHARD, D), wpg], axis=0), ROWS2)


def _unstack1(s):
    return s[:IN_SH], s[O1_OUT:USED1]


def _unstack2(s):
    return s[:O2_U], s[O2_U:O2_D], s[O2_D:O2_PLE], s[O2_PLE:O2_PG].reshape(DPLE, DPLE), s[O2_PG:USED2]


def _cat(g, lo, hi):
    return g[:, lo:hi].reshape(NSHARD * (hi - lo), D)


def _unstack1_full(g):
    w_in_t = _cat(g, 0, IN_SH)
    w_in_t = jnp.concatenate([w_in_t[:3 * DA], _pad_rows(w_in_t[3 * DA:3 * DA + NH], LANE), w_in_t[3 * DA + NH:]], axis=0)
    return w_in_t, _cat(g, O1_OUT, USED1)


def _unstack2_full(g):
    w_ple = g[:, O2_PLE:O2_PG].reshape(NSHARD, DPLE, DPLE).transpose(1, 0, 2).reshape(DPLE, D)
    return _cat(g, 0, O2_U), _cat(g, O2_U, O2_D), _cat(g, O2_D, O2_PLE), w_ple, _cat(g, O2_PG, USED2)


def _shards(a):
    return a.reshape(NSHARD, a.shape[0] // NSHARD, D)


def _stack1_full(dwin_t, dwout):
    dwin_t = jnp.concatenate([dwin_t[:3 * DA + NH], dwin_t[3 * DA + LANE:]], axis=0).reshape(NSHARD, IN_SH, D)
    zeros = lambda r: jnp.zeros((NSHARD, r, D), F32)
    return jnp.concatenate([dwin_t, zeros(IN_PAD - IN_SH), _shards(dwout), zeros(ROWS1 - USED1)], axis=1)


def _stack2_full(dwg_t, dwu_t, dwd, dwple, dwpg):
    dwple = dwple.reshape(DPLE, NSHARD, DPLE).transpose(1, 0, 2).reshape(NSHARD, DPLE // NSHARD, D)
    return jnp.concatenate([_shards(dwg_t), _shards(dwu_t), _shards(dwd), dwple, _shards(dwpg),
                            jnp.zeros((NSHARD, ROWS2 - USED2, D), F32)], axis=1)


def _sds(shape, dtype):
    return jax.ShapeDtypeStruct(shape, dtype)


class _Comm:
    def __init__(self, stack2, me, c):
        self.stack2, self.me, self.c = stack2, me, c
        self.cidx = c.astype(jnp.int32).reshape(1)
        self.place = jnp.stack([me, c]).astype(jnp.int32)
        self.h = ROWS2 // 2
        self.gather = _exchange_start("gather2_start", 3, stack2, lax.empty((NSHARD, ROWS2, D), BF), _plan_gather(self.h))
        self.dep = self.gather[4]

    def after_attention(self, after):
        own, land = _exchange_wait("gather2_wait", self.gather, _plan_gather(self.h), after)
        fwd = _plan_forward(self.h)
        self.forward = lambda own_ref, land_ref: fwd(land_ref, own_ref, land_ref)
        self.passing = _exchange_start("forward2_start", 4, own, land, self.forward)
        return self.passing[4]

    def weights2(self, after):
        return _unstack2_full(_exchange_wait("forward2_wait", self.passing, self.forward, after)[1])

    def after_ffn(self, grads2):
        g = _stack2_full(*grads2)
        self.pair = _exchange_start("reduce2_pair_start", 1, g, lax.empty((NSHARD, self.h, D), F32),
                                    _plan_pair_rows(self.h))
        return self.pair[4]

    def after_mix(self, after, early):
        g, recv = _exchange_wait("reduce2_pair_wait", self.pair, _plan_pair_rows(self.h), after)
        pb = _pair_sum("pair_sum2", self.cidx, g, recv, RED2)
        self.chip = _exchange_start("reduce2_chip_start", 3, pb, lax.empty((NSHARD - 1, self.h, D), BF), _plan_scatter)
        self.early_shapes = {n: early[n].shape for n in early}
        v = _pack_small(early)
        self.small = _exchange_start("small_start", 7, v, lax.empty((8,) + v.shape, F32), _plan_all)
        return self.chip[4] + self.small[4]

    def after_attn(self, after):
        pb, y = _exchange_wait("reduce2_chip_wait", self.chip, _plan_scatter, after)
        f = _chip_sum("chip_sum2", self.place, pb, y, RED2)
        self.reduced2 = _exchange_inplace("reduce2_gather", 1, f, (), _plan_swap_halves(self.h))
        v, land = _exchange_wait("small_wait", self.small, _plan_all, after)
        land = lax.dynamic_update_slice(land, v[None], (2 * self.me + self.c, 0, 0))
        self.early = _unpack_small(_sum_slots(land), self.early_shapes)
        return self.reduced2


def _pack_small(small):
    parts = []
    for name in small:
        flat = small[name].reshape(-1)
        parts.append(jnp.pad(flat, (0, -flat.shape[0] % LANE)).reshape(-1, LANE))
    v = jnp.concatenate(parts, axis=0)
    return _pad_rows(v, v.shape[0] + (-v.shape[0] % 8))


def _unpack_small(v, shapes):
    out, r = {}, 0
    for name in shapes:
        n = math.prod(shapes[name])
        rows = -(-n // LANE)
        out[name] = v[r:r + rows].reshape(-1)[:n].reshape(shapes[name])
        r += rows
    return out


def kernel(x, p, g_mix_pre, w_in, b_forget, g_attn_grp, g_pool_grp, w_pool, pool_scale, w_out, g_mix_post, g_ffn_pre, w_ffn_gate, w_ffn_up, w_ffn_down, g_ffn_post, w_ple_proj, g_ple, w_ple_gate, loss_target, m_g_mix_pre, m_w_in, m_b_forget, m_g_attn_grp, m_g_pool_grp, m_w_pool, m_pool_scale, m_w_out, m_g_mix_post, m_g_ffn_pre, m_w_ffn_gate, m_w_ffn_up, m_w_ffn_down, m_g_ffn_post, m_w_ple_proj, m_g_ple, m_w_ple_gate, v_g_mix_pre, v_w_in, v_b_forget, v_g_attn_grp, v_g_pool_grp, v_w_pool, v_pool_scale, v_w_out, v_g_mix_post, v_g_ffn_pre, v_w_ffn_gate, v_w_ffn_up, v_w_ffn_down, v_g_ffn_post, v_w_ple_proj, v_g_ple, v_w_ple_gate):
    args = dict(locals())
    strip = lambda n, a: a if n in VECTORS else a[0]
    w = {n: strip(n, args[n]) for n in ORDER}
    mom = {n: strip(n, args["m_" + n]) for n in ORDER}
    var = {n: strip(n, args["v_" + n]) for n in ORDER}
    sm = {n: w[n] for n in SMALL}

    c = lax.axis_index("c")
    me = 2 * lax.axis_index("x") + lax.axis_index("y")
    h1 = ROWS1 // 2
    bf = lambda n: w[n].astype(BF)
    stack1 = _stack1(bf("w_in"), bf("w_out"))
    stack2 = _stack2(*[bf(n) for n in BIG[2:]])
    land = _exchange("gather1", 3, stack1, _sds((NSHARD, ROWS1, D), BF), _plan_gather(h1))
    land, stack2 = lax.optimization_barrier((land, stack2))
    comm = _Comm(stack2, me, c)
    w1 = _unstack1_full(_exchange_inplace("gather1_forward", 4, land, (stack1,), _plan_forward(h1)))
    dx, grads1, late = _local_step(x[0], p[0, 0], loss_target[0], sm, w1, comm)

    late_shapes = {n: late[n].shape for n in late}
    red_small = {**comm.early, **_unpack_small(_all_reduce_small(_pack_small(late)), late_shapes)}
    loss = 0.5 / D * red_small["loss"][0, 0]

    g1 = _stack1_full(*grads1)
    recv = _exchange("reduce1_pair", 1, g1, _sds((NSHARD, h1, D), F32), _plan_pair_rows(h1))
    pb = _pair_sum("pair_sum1", comm.cidx, g1, recv, RED1)
    chip1 = _exchange_start("reduce1_chip_start", 3, pb, lax.empty((NSHARD - 1, h1, D), BF), _plan_scatter)

    flip = lambda n, a: a.T if n in TRANSPOSED else a
    grads, delta, new_m, new_v = {}, {}, {}, {}

    def update(names, shards, dep):
        for n, g in zip(names, shards):
            d_, m_, v_ = _adamw(flip(n, w[n]), g, flip(n, mom[n]), flip(n, var[n]), dep)
            grads[n], delta[n], new_m[n], new_v[n] = flip(n, g), flip(n, d_), flip(n, m_), flip(n, v_)

    update(BIG[2:], _unstack2(comm.reduced2), chip1[4])
    pb, y = _exchange_wait("reduce1_chip_wait", chip1, _plan_scatter, new_v[BIG[-1]])
    f = _chip_sum("chip_sum1", comm.place, pb, y, RED1)
    reduced1 = _exchange_inplace("reduce1_gather", 1, f, (), _plan_swap_halves(h1))
    update(BIG[:2], _unstack1(reduced1), reduced1)
    for n in SMALL:
        grads[n] = red_small[n].reshape(w[n].shape)
    two_d = lambda a: a.reshape(-1, a.shape[-1])
    ds, ms, vs = _adamw_small([two_d(w[n]) for n in SMALL], [two_d(grads[n]) for n in SMALL],
                              [two_d(mom[n]) for n in SMALL], [two_d(var[n]) for n in SMALL])
    for k, n in enumerate(SMALL):
        delta[n], new_m[n], new_v[n] = ds[k].reshape(w[n].shape), ms[k].reshape(w[n].shape), vs[k].reshape(w[n].shape)

    lead = lambda d: [d[n] if n in VECTORS else d[n][None] for n in ORDER]
    return (loss, dx[None], *lead(grads), *lead(delta), *lead(new_m), *lead(new_v))
```

```python
import functools
import math

import jax
import jax.numpy as jnp
import numpy as np
from jax import lax
from jax.experimental import pallas as pl
from jax.experimental.pallas import tpu as pltpu

F32 = jnp.float32
BF = jnp.bfloat16
MESH = pl.DeviceIdType.MESH

D = 1024
DA = 512
DP = 512
NH = 8
HD = 64
DFF = 2816
DPLE = 256
WINS = (2, 4, 8, 16)
PC = 128
ZW = 3 * DA + 128 + DP
EPS = 1e-6
NSHARD = 4

LANE = 128
HALO = 128

IN_SH = 514
IN_PAD = 528
FF_SH = DFF // NSHARD
O1_OUT, USED1, ROWS1 = 528, 784, 800
O2_U, O2_D, O2_PLE, O2_PG, USED2, ROWS2 = 704, 1408, 2112, 2176, 2432, 2560
RED1, RED2 = 400, 640

ADAM_LR, ADAM_B1, ADAM_B2, ADAM_EPS, ADAM_WD, ADAM_STEP = 0.001, 0.9, 0.999, 1e-8, 0.01, 10

VMEM_LIMIT = 56 * 1024 * 1024


def _cp(**kw):
    return pltpu.CompilerParams(vmem_limit_bytes=VMEM_LIMIT, **kw)


def _mm(a, b):
    return jnp.dot(a.astype(BF), b.astype(BF), preferred_element_type=F32)


def _mm_nt(a, b):
    return lax.dot_general(a.astype(BF), b.astype(BF), (((1,), (1,)), ((), ())), preferred_element_type=F32)


def _mm_tn(a, b):
    return lax.dot_general(a.astype(BF), b.astype(BF), (((0,), (0,)), ((), ())), preferred_element_type=F32)


def _split2(x):
    hi = x.astype(BF)
    lo = (x - hi.astype(F32)).astype(BF)
    return hi, lo


def _split3(x):
    hi = x.astype(BF)
    r = x - hi.astype(F32)
    mid = r.astype(BF)
    lo = (r - mid.astype(F32)).astype(BF)
    return hi, mid, lo


def _dot3(m, x):
    hi, mid, lo = _split3(x)
    return (jnp.dot(m, hi, preferred_element_type=F32) + jnp.dot(m, mid, preferred_element_type=F32)
            + jnp.dot(m, lo, preferred_element_type=F32))


def _dot2(m, x):
    hi, lo = _split2(x)
    return jnp.dot(m, hi, preferred_element_type=F32) + jnp.dot(m, lo, preferred_element_type=F32)


def _rstd(x):
    return lax.rsqrt(jnp.mean(x * x, axis=-1, keepdims=True) + EPS)


def _rms_bwd(dy, x, g):
    r = _rstd(x)
    xh = x * r
    dg = jnp.sum(dy * xh, axis=0, keepdims=True)
    dxh = dy * g
    dx = r * (dxh - xh * jnp.mean(dxh * xh, axis=-1, keepdims=True))
    return dx, dg


def _sigmoid(x):
    return 1.0 / (1.0 + jnp.exp(-x))


ANY = pl.BlockSpec(memory_space=pl.ANY)


def _full(shape):
    n = len(shape)
    return pl.BlockSpec(shape, lambda *_: (0,) * n)


def _resident(shape):
    n = len(shape)
    return pl.BlockSpec(shape, lambda *_: (0,) * n, pipeline_mode=pl.Buffered(1))


def _tile(t):
    return 512 if t % 512 == 0 else t


def _tri(n, upper):
    r, c = np.indices((n, n))
    return ((c >= r) if upper else (c <= r)).astype(BF)


def _band(tt, transpose):
    r, c = np.indices((tt, tt + HALO))
    d = (c - r) if transpose else (r + HALO - c)
    return np.stack([((d >= 0) & (d < w)).astype(BF) for w in WINS])


def _aug_consts():
    row, col = np.indices((3 * LANE, NH * LANE))
    piece, head = row // LANE, row % LANE
    ch, cl = col // LANE, col % LANE
    eq = ((head == ch) & (cl == HD + piece)).astype(BF)
    ek = -((head == ch) & (cl == HD + 3 + piece)).astype(BF)
    lane = np.arange(NH * LANE)[None, :] % LANE
    rowq = ((lane >= HD + 3) & (lane < HD + 6)).astype(np.float32)
    rowk = ((lane >= HD) & (lane < HD + 3)).astype(np.float32)
    r2, c2 = np.indices((NH * LANE, LANE))
    selq = ((r2 // LANE == c2) & (r2 % LANE == HD)).astype(BF)
    selk = ((r2 // LANE == c2) & (r2 % LANE == HD + 3)).astype(BF)
    return eq, ek, rowq, rowk, selq, selk


def _in_proj(x, g1, w_in_t, b_pad, tri, eq, ek, rowq, rowk, dep):
    t = x.shape[0]
    tt = _tile(t)

    def body(x_ref, g_ref, w_ref, b_ref, tri_ref, eq_ref, ek_ref, rq_ref, rk_ref, dep_ref,
             qa_ref, ka_ref, v_ref, u_ref, fl_ref, carry):
        i = pl.program_id(0)

        @pl.when(i == 0)
        def _():
            carry[...] = jnp.zeros_like(carry)

        xv = x_ref[...]
        hn = (xv * _rstd(xv) * g_ref[...]).astype(BF)
        z = _mm_nt(hn, w_ref[...])
        fl = z[:, 3 * DA:3 * DA + LANE] + b_ref[...]
        lane = lax.broadcasted_iota(jnp.int32, fl.shape, 1)
        lf = jnp.where(lane < NH, jnp.minimum(fl, 0.0) - jnp.log(1.0 + jnp.exp(-jnp.abs(fl))), 0.0)
        c = carry[...] + _dot3(tri_ref[...], lf)
        carry[...] = carry[...] + jnp.sum(lf, axis=0, keepdims=True)
        caug = jnp.concatenate(_split3(c), axis=1)
        aug_q = jnp.dot(caug, eq_ref[...], preferred_element_type=F32) + rq_ref[...]
        aug_k = jnp.dot(caug, ek_ref[...], preferred_element_type=F32) + rk_ref[...]
        low = lax.broadcasted_iota(jnp.int32, (tt, LANE), 1) < HD
        for p in range(NH // 2):
            qp = z[:, LANE * p:LANE * (p + 1)] * (1.0 / math.sqrt(HD))
            kp = z[:, DA + LANE * p:DA + LANE * (p + 1)]
            for h, (qh, kh) in enumerate(((qp, kp), (pltpu.roll(qp, HD, 1), pltpu.roll(kp, HD, 1)))):
                lo_, hi_ = LANE * (2 * p + h), LANE * (2 * p + h + 1)
                qa_ref[:, lo_:hi_] = jnp.where(low, qh, aug_q[:, lo_:hi_]).astype(BF)
                ka_ref[:, lo_:hi_] = jnp.where(low, kh, aug_k[:, lo_:hi_]).astype(BF)
        v_ref[...] = z[:, 2 * DA:3 * DA].astype(BF)
        u_ref[...] = z[:, 3 * DA + LANE:]
        fl_ref[...] = fl

    return pl.pallas_call(
        body, name="in_proj", grid=(t // tt,),
        in_specs=[pl.BlockSpec((tt, D), lambda i: (i, 0)), _full((1, D)), _resident((ZW, D)), _full((1, LANE)),
                  _full((tt, tt)), _full((3 * LANE, NH * LANE)), _full((3 * LANE, NH * LANE)),
                  _full((1, NH * LANE)), _full((1, NH * LANE)), ANY],
        out_specs=[pl.BlockSpec((tt, NH * LANE), lambda i: (i, 0)), pl.BlockSpec((tt, NH * LANE), lambda i: (i, 0)),
                   pl.BlockSpec((tt, DA), lambda i: (i, 0)), pl.BlockSpec((tt, DP), lambda i: (i, 0)),
                   pl.BlockSpec((tt, LANE), lambda i: (i, 0))],
        out_shape=[jax.ShapeDtypeStruct((t, NH * LANE), BF), jax.ShapeDtypeStruct((t, NH * LANE), BF),
                   jax.ShapeDtypeStruct((t, DA), BF), jax.ShapeDtypeStruct((t, DP), F32),
                   jax.ShapeDtypeStruct((t, LANE), F32)],
        scratch_shapes=[pltpu.VMEM((1, LANE), F32)],
        compiler_params=_cp(),
    )(x, g1, w_in_t, b_pad, tri, eq, ek, rowq, rowk, dep)


def _attn_fwd(qa, ka, v):
    t = qa.shape[0]
    ta = _tile(t)
    n = t // ta

    def body(q_ref, k_ref, v_ref, a_ref, lse_ref, m_ref, l_ref, acc_ref):
        i = pl.program_id(1)
        m_ref[...] = jnp.full_like(m_ref, -1e30)
        l_ref[...] = jnp.zeros_like(l_ref)
        acc_ref[...] = jnp.zeros_like(acc_ref)
        qs = [q_ref[:, LANE * h:LANE * (h + 1)] for h in range(2)]
        reps = ta // LANE

        def tile(j, masked):
            rows = pl.ds(pl.multiple_of(j * ta, ta), ta)
            v2 = v_ref[rows, :]
            s = [_mm_nt(qs[h], k_ref[rows, LANE * h:LANE * (h + 1)]) for h in range(2)]
            if masked:
                keep = (lax.broadcasted_iota(jnp.int32, (ta, ta), 1) <= lax.broadcasted_iota(jnp.int32, (ta, ta), 0))
                s = [jnp.where(keep, sh, -1e30) for sh in s]
            m_old = [m_ref[h] for h in range(2)]
            m_new = [jnp.maximum(m_old[h], jnp.max(s[h], axis=1, keepdims=True)) for h in range(2)]
            pe = [jnp.exp(s[h] - jnp.tile(m_new[h], (1, reps))) for h in range(2)]
            alpha = [jnp.exp(m_old[h] - m_new[h]) for h in range(2)]
            pv = [jnp.dot(pe[h].astype(BF), v2, preferred_element_type=F32) for h in range(2)]
            for h in range(2):
                l_ref[h] = alpha[h] * l_ref[h] + jnp.sum(pe[h], axis=1, keepdims=True)
                acc_ref[h] = alpha[h] * acc_ref[h] + pv[h]
                m_ref[h] = m_new[h]

        def step(j, carry):
            tile(j, False)
            return carry

        lax.fori_loop(0, i, step, 0)
        tile(i, True)
        low = lax.broadcasted_iota(jnp.int32, (ta, LANE), 1) < HD
        a_ref[...] = jnp.where(low, acc_ref[0] / l_ref[0], acc_ref[1] / l_ref[1])
        lse_ref[...] = jnp.where(low, m_ref[0] + jnp.log(l_ref[0]), m_ref[1] + jnp.log(l_ref[1]))

    return pl.pallas_call(
        body, name="attn_fwd", grid=(NH // 2, n),
        in_specs=[pl.BlockSpec((ta, 2 * LANE), lambda p, i: (i, p)),
                  pl.BlockSpec((t, 2 * LANE), lambda p, i: (0, p)),
                  pl.BlockSpec((t, LANE), lambda p, i: (0, p))],
        out_specs=[pl.BlockSpec((ta, LANE), lambda p, i: (i, p)), pl.BlockSpec((ta, LANE), lambda p, i: (i, p))],
        out_shape=[jax.ShapeDtypeStruct((t, DA), F32), jax.ShapeDtypeStruct((t, DA), F32)],
        scratch_shapes=[pltpu.VMEM((2, ta, LANE), F32), pltpu.VMEM((2, ta, LANE), F32), pltpu.VMEM((2, ta, LANE), F32)],
        compiler_params=_cp(),
    )(qa, ka, v)


def _mix_out(a, u, x, band, w_pool, pool_scale, g_attn, g_pool, w_out, g_post, dep):
    t = a.shape[0]
    tt = _tile(t)
    hb = tt // HALO

    def body(a_ref, u_ref, up_ref, x_ref, band_ref, wp_ref, ps_ref, ga_ref, gp_ref, wo_ref, go_ref, dep_ref,
             yb_ref, m_ref, o_ref, h1_ref):
        i = pl.program_id(0)
        prev = up_ref[...] * jnp.where(i > 0, 1.0, 0.0)
        tok = i * tt + lax.broadcasted_iota(jnp.int32, (tt, PC), 0)
        ms = []
        for g, w in enumerate(WINS):
            ug = u_ref[:, PC * g:PC * (g + 1)]
            ext = jnp.concatenate([prev[:, PC * g:PC * (g + 1)], ug], axis=0)
            cnt = jnp.minimum(tok + 1, w).astype(F32)
            y = (_dot2(band_ref[g], ext) / cnt - ug).astype(BF)
            yb_ref[:, PC * g:PC * (g + 1)] = y
            ms.append(_mm(y, wp_ref[g]) * ps_ref[:, PC * g:PC * (g + 1)])
        m = jnp.concatenate(ms, axis=1)
        m_ref[...] = m
        av = a_ref[...]
        mix = jnp.concatenate([av * _rstd(av) * ga_ref[...], m * _rstd(m) * gp_ref[...]], axis=1)
        o = _mm(mix, wo_ref[...])
        o_ref[...] = o
        h1_ref[...] = x_ref[...] + o * _rstd(o) * go_ref[...]

    return pl.pallas_call(
        body, name="mix_out", grid=(t // tt,),
        in_specs=[pl.BlockSpec((tt, DA), lambda i: (i, 0)), pl.BlockSpec((tt, DP), lambda i: (i, 0)),
                  pl.BlockSpec((HALO, DP), lambda i: (jnp.maximum(i * hb - 1, 0), 0)),
                  pl.BlockSpec((tt, D), lambda i: (i, 0)), _full((len(WINS), tt, tt + HALO)),
                  _full((len(WINS), PC, PC)), _full((1, DP)), _full((1, DA)), _full((1, DP)),
                  _resident((D, D)), _full((1, D)), ANY],
        out_specs=[pl.BlockSpec((tt, DP), lambda i: (i, 0)), pl.BlockSpec((tt, DP), lambda i: (i, 0)),
                   pl.BlockSpec((tt, D), lambda i: (i, 0)), pl.BlockSpec((tt, D), lambda i: (i, 0))],
        out_shape=[jax.ShapeDtypeStruct((t, DP), BF), jax.ShapeDtypeStruct((t, DP), F32),
                   jax.ShapeDtypeStruct((t, D), F32), jax.ShapeDtypeStruct((t, D), F32)],
        compiler_params=_cp(),
    )(a, u, u, x, band, w_pool, pool_scale, g_attn, g_pool, w_out, g_post, dep)


def _ffn_fwd(h1, g_pre, stacks2, g_post, p, w_ple, g_ple, w_pg, tgt):
    t = h1.shape[0]
    tt = 256 if t % 256 == 0 else t

    def body(h1_ref, gpre_ref, wg_ref, wu_ref, wd_ref, gpost_ref, p_ref, wple_ref, gple_ref, wpg_ref, tgt_ref,
             hn_ref, gate_ref, up_ref, dff_ref, dh2_ref, loss_ref, dwpg_ref, dwple_ref, dgple_ref, dgpost_ref):
        i = pl.program_id(0)

        @pl.when(i == 0)
        def _():
            loss_ref[...] = jnp.zeros_like(loss_ref)
            dwpg_ref[...] = jnp.zeros_like(dwpg_ref)
            dwple_ref[...] = jnp.zeros_like(dwple_ref)
            dgple_ref[...] = jnp.zeros_like(dgple_ref)
            dgpost_ref[...] = jnp.zeros_like(dgpost_ref)

        h1v = h1_ref[...]
        hn = (h1v * _rstd(h1v) * gpre_ref[...]).astype(BF)
        hn_ref[...] = hn
        gate = _mm_nt(hn, wg_ref[...].reshape(DFF, D))
        up = _mm_nt(hn, wu_ref[...].reshape(DFF, D))
        gate_ref[...] = gate.astype(BF)
        up_ref[...] = up.astype(BF)
        ff = _mm(gate * _sigmoid(gate) * up, wd_ref[...].reshape(DFF, D))
        rff = _rstd(ff)
        ffh = ff * rff
        gpost = gpost_ref[...]
        h2 = h1v + ffh * gpost
        pv = p_ref[...]
        pe = _mm(pv, wple_ref[...])
        rpe = _rstd(pe)
        peh = pe * rpe
        gple = gple_ref[...]
        e = peh * gple
        sig = _sigmoid(_mm(h2, wpg_ref[...]))
        dv = h2 + sig * e - tgt_ref[...]
        sq = jnp.sum(jnp.sum(dv * dv, axis=1, keepdims=True), axis=0, keepdims=True)
        loss_ref[...] = loss_ref[...] + sq
        dy = dv * (1.0 / D)
        d_e = dy * sig
        d_gl = dy * e * sig * (1.0 - sig)
        dh2 = dy + _mm_nt(d_gl, wpg_ref[...])
        dh2_ref[...] = dh2
        dwpg_ref[...] = dwpg_ref[...] + _mm_tn(h2, d_gl)
        dgple_ref[...] = dgple_ref[...] + jnp.sum(d_e * peh, axis=0, keepdims=True)
        dpeh = d_e * gple
        d_pe = rpe * (dpeh - peh * jnp.mean(dpeh * peh, axis=-1, keepdims=True))
        dwple_ref[...] = dwple_ref[...] + _mm_tn(pv, d_pe)
        dgpost_ref[...] = dgpost_ref[...] + jnp.sum(dh2 * ffh, axis=0, keepdims=True)
        dffh = dh2 * gpost
        dff_ref[...] = (rff * (dffh - ffh * jnp.mean(dffh * ffh, axis=-1, keepdims=True))).astype(BF)

    row = lambda w: pl.BlockSpec((tt, w), lambda i: (i, 0))
    shard_rows = lambda k: pl.BlockSpec((NSHARD, FF_SH, D), lambda i: (0, k, 0), pipeline_mode=pl.Buffered(1))
    return pl.pallas_call(
        body, name="ffn_fwd", grid=(t // tt,),
        in_specs=[row(D), _full((1, D)), shard_rows(0), shard_rows(1), shard_rows(2), _full((1, D)),
                  row(DPLE), _resident((DPLE, D)), _full((1, D)), _resident((D, D)), row(D)],
        out_specs=[row(D), row(DFF), row(DFF), row(D), row(D), _full((8, LANE)), _full((D, D)), _full((DPLE, D)),
                   _full((1, D)), _full((1, D))],
        out_shape=[jax.ShapeDtypeStruct((t, D), BF), jax.ShapeDtypeStruct((t, DFF), BF), jax.ShapeDtypeStruct((t, DFF), BF),
                   jax.ShapeDtypeStruct((t, D), BF), jax.ShapeDtypeStruct((t, D), F32), jax.ShapeDtypeStruct((8, LANE), F32),
                   jax.ShapeDtypeStruct((D, D), F32), jax.ShapeDtypeStruct((DPLE, D), F32),
                   jax.ShapeDtypeStruct((1, D), F32), jax.ShapeDtypeStruct((1, D), F32)],
        compiler_params=_cp(),
    )(h1, g_pre, stacks2, stacks2, stacks2, g_post, p, w_ple, g_ple, w_pg, tgt)


FF_CH = 256


def _ffn_bwd(hn2, gate, up, dff, wgu, wd):
    t = hn2.shape[0]
    tt = 1024 if t % 1024 == 0 else _tile(t)
    nt = t // tt
    ch = FF_CH
    nc = DFF // ch

    def body(hn_ref, gate_ref, up_ref, dff_ref, wgu_ref, wd_ref,
             dwgu_ref, dwd_ref, dhn_ref, acc, sem):
        j, i = pl.program_id(0), pl.program_id(1)

        @pl.when(j == 0)
        def _():
            acc[pl.ds(pl.multiple_of(i * tt, tt), tt), :] = jnp.zeros((tt, D), F32)

        @pl.when(i == 0)
        def _():
            dwgu_ref[...] = jnp.zeros_like(dwgu_ref)
            dwd_ref[...] = jnp.zeros_like(dwd_ref)

        half = tt // 2
        acts, dgus = [], []
        for hh in range(2):
            r = slice(hh * half, (hh + 1) * half)
            gate_v = gate_ref[r, :].astype(F32)
            up_v = up_ref[r, :].astype(F32)
            sg = _sigmoid(gate_v)
            silu = gate_v * sg
            d_act = _mm_nt(dff_ref[r, :], wd_ref[...])
            d_up = (d_act * silu).astype(BF)
            d_gate = (d_act * up_v * (sg * (1.0 + gate_v * (1.0 - sg)))).astype(BF)
            dgu = jnp.concatenate([d_gate, d_up], axis=1)
            rows = pl.ds(pl.multiple_of(i * tt + hh * half, half), half)
            acc[rows, :] = acc[rows, :] + jnp.dot(dgu, wgu_ref[0], preferred_element_type=F32)
            acts.append((silu * up_v).astype(BF))
            dgus.append(dgu)
        dwd_ref[...] = dwd_ref[...] + _mm_tn(jnp.concatenate(acts, axis=0), dff_ref[...])
        dwgu_ref[0] = dwgu_ref[0] + _mm_tn(jnp.concatenate(dgus, axis=0), hn_ref[...])

        @pl.when((j == nc - 1) & (i == nt - 1))
        def _():
            cp = pltpu.make_async_copy(acc, dhn_ref, sem)
            cp.start()
            cp.wait()

    tok = lambda w: pl.BlockSpec((tt, w), lambda j, i: (i, 0))
    chunk = pl.BlockSpec((ch, D), lambda j, i: (j, 0))
    pair = pl.BlockSpec((1, 2 * ch, D), lambda j, i: (j, 0, 0))
    return pl.pallas_call(
        body, name="ffn_bwd", grid=(nc, nt),
        in_specs=[tok(D), pl.BlockSpec((tt, ch), lambda j, i: (i, j)), pl.BlockSpec((tt, ch), lambda j, i: (i, j)),
                  tok(D), pair, chunk],
        out_specs=[pair, chunk, pl.BlockSpec(memory_space=pl.ANY)],
        out_shape=[jax.ShapeDtypeStruct((nc, 2 * ch, D), F32), jax.ShapeDtypeStruct((DFF, D), F32),
                   jax.ShapeDtypeStruct((t, D), F32)],
        scratch_shapes=[pltpu.VMEM((t, D), F32), pltpu.SemaphoreType.DMA],
        compiler_params=_cp(),
    )(hn2, gate, up, dff, wgu, wd)


def _mix_bwd(d_hn2, dh2, h1, o, a, m, yb, g_ffn_pre, g_post, g_attn, g_pool, w_out, w_pool, pool_scale, dep):
    t = a.shape[0]
    tt = 256 if t % 256 == 0 else t

    def body(dhn_ref, dh2_ref, h1_ref, o_ref, a_ref, m_ref, yb_ref, gfp_ref, go_ref, ga_ref, gp_ref, wo_ref, wp_ref,
             ps_ref, dep_ref, dh1_ref, da_ref, dyc_ref, dgfp_ref, dgo_ref, dga_ref, dgp_ref, dps_ref, dwp_ref, dwo_ref):
        i = pl.program_id(0)

        @pl.when(i == 0)
        def _():
            for r in (dgfp_ref, dgo_ref, dga_ref, dgp_ref, dps_ref, dwp_ref, dwo_ref):
                r[...] = jnp.zeros_like(r)

        d1, dg = _rms_bwd(dhn_ref[...], h1_ref[...], gfp_ref[...])
        dgfp_ref[...] = dgfp_ref[...] + dg
        dh1 = dh2_ref[...] + d1
        dh1_ref[...] = dh1
        d_o, dg = _rms_bwd(dh1, o_ref[...], go_ref[...])
        dgo_ref[...] = dgo_ref[...] + dg
        d_mix = _mm_nt(d_o, wo_ref[...])
        av, mv = a_ref[...], m_ref[...]
        mix = jnp.concatenate([av * _rstd(av) * ga_ref[...], mv * _rstd(mv) * gp_ref[...]], axis=1)
        dwo_ref[...] = dwo_ref[...] + _mm_tn(mix, d_o)
        d_a, dg = _rms_bwd(d_mix[:, :DA], av, ga_ref[...])
        dga_ref[...] = dga_ref[...] + dg
        da_ref[...] = d_a
        d_m, dg = _rms_bwd(d_mix[:, DA:], mv, gp_ref[...])
        dgp_ref[...] = dgp_ref[...] + dg
        tok = i * tt + lax.broadcasted_iota(jnp.int32, (tt, PC), 0)
        dps = []
        for g, w in enumerate(WINS):
            sl = slice(PC * g, PC * (g + 1))
            ybg = yb_ref[:, sl]
            wpg = wp_ref[g].astype(BF)
            mlin = jnp.dot(ybg, wpg, preferred_element_type=F32)
            dmg = d_m[:, sl]
            dps.append(jnp.sum(dmg * mlin, axis=0, keepdims=True))
            dml = (dmg * ps_ref[:, sl]).astype(BF)
            dwp_ref[g] = dwp_ref[g] + _mm_tn(ybg, dml)
            dyc_ref[:, sl] = _mm_nt(dml, wpg) / jnp.minimum(tok + 1, w).astype(F32)
        dps_ref[...] = dps_ref[...] + jnp.concatenate(dps, axis=1)

    row = lambda w: pl.BlockSpec((tt, w), lambda i: (i, 0))
    return pl.pallas_call(
        body, name="mix_bwd", grid=(t // tt,),
        in_specs=[row(D), row(D), row(D), row(D), row(DA), row(DP), row(DP), _full((1, D)), _full((1, D)),
                  _full((1, DA)), _full((1, DP)), _resident((D, D)), _full((len(WINS), PC, PC)), _full((1, DP)), ANY],
        out_specs=[row(D), row(DA), row(DP), _full((1, D)), _full((1, D)), _full((1, DA)), _full((1, DP)),
                   _full((1, DP)), _full((len(WINS), PC, PC)), _full((D, D))],
        out_shape=[jax.ShapeDtypeStruct((t, D), F32), jax.ShapeDtypeStruct((t, DA), F32), jax.ShapeDtypeStruct((t, DP), F32),
                   jax.ShapeDtypeStruct((1, D), F32), jax.ShapeDtypeStruct((1, D), F32), jax.ShapeDtypeStruct((1, DA), F32),
                   jax.ShapeDtypeStruct((1, DP), F32), jax.ShapeDtypeStruct((1, DP), F32),
                   jax.ShapeDtypeStruct((len(WINS), PC, PC), F32), jax.ShapeDtypeStruct((D, D), F32)],
        compiler_params=_cp(),
    )(d_hn2, dh2, h1, o, a, m, yb, g_ffn_pre, g_post, g_attn, g_pool, w_out, w_pool, pool_scale, dep)


def _attn_bwd(qa, ka, v, a, d_a, lse, dep):
    t = qa.shape[0]
    ta = _tile(t)
    n = t // ta

    def body(q_ref, k_ref, v_ref, o_ref, do_ref, lse_ref, dep_ref, dq_ref, dk_ref, dv_ref):
        j = pl.program_id(1)

        @pl.when(j == 0)
        def _():
            dq_ref[...] = jnp.zeros_like(dq_ref)

        dk_ref[...] = jnp.zeros_like(dk_ref)
        dv_ref[...] = jnp.zeros_like(dv_ref)
        ks = [k_ref[:, LANE * h:LANE * (h + 1)] for h in range(2)]
        v2 = v_ref[...]
        lane = lax.broadcasted_iota(jnp.int32, (ta, LANE), 1)
        mine = [lane < HD, lane >= HD]

        def tile(i, masked):
            rows = pl.ds(pl.multiple_of(i * ta, ta), ta)
            do2 = do_ref[rows, :]
            prod = do2 * o_ref[rows, :]
            lse2 = lse_ref[rows, :]
            do2b = do2.astype(BF)
            qh = [q_ref[rows, LANE * h:LANE * (h + 1)] for h in range(2)]
            s = [_mm_nt(qh[h], ks[h]) for h in range(2)]
            dp = [_mm_nt(jnp.where(mine[h], do2, 0.0), v2) for h in range(2)]
            delta = [jnp.sum(jnp.where(mine[h], prod, 0.0), axis=1, keepdims=True) for h in range(2)]
            lse_h = [jnp.sum(jnp.where(lane == HD * h, lse2, 0.0), axis=1, keepdims=True) for h in range(2)]
            pr = [jnp.exp(s[h] - lse_h[h]) for h in range(2)]
            if masked:
                keep = (lax.broadcasted_iota(jnp.int32, (ta, ta), 1) <= lax.broadcasted_iota(jnp.int32, (ta, ta), 0))
                pr = [jnp.where(keep, ph, 0.0) for ph in pr]
            ds = [(pr[h] * (dp[h] - delta[h])).astype(BF) for h in range(2)]
            dv_ref[...] = dv_ref[...] + jnp.where(mine[0], _mm_tn(pr[0], do2b), _mm_tn(pr[1], do2b))
            for h in range(2):
                sl = slice(LANE * h, LANE * (h + 1))
                dk_ref[:, sl] = dk_ref[:, sl] + _mm_tn(ds[h], qh[h])
                dq_ref[0, rows, sl] = dq_ref[0, rows, sl] + jnp.dot(ds[h], ks[h], preferred_element_type=F32)

        def step(i, carry):
            tile(i, False)
            return carry

        tile(j, True)
        lax.fori_loop(j + 1, n, step, 0)

    qrow = lambda w: pl.BlockSpec((t, w), lambda p, j: (0, p))
    krow = lambda w: pl.BlockSpec((ta, w), lambda p, j: (j, p))
    return pl.pallas_call(
        body, name="attn_bwd", grid=(NH // 2, n),
        in_specs=[qrow(2 * LANE), krow(2 * LANE), krow(LANE), qrow(LANE), qrow(LANE), qrow(LANE), ANY],
        out_specs=[pl.BlockSpec((1, t, 2 * LANE), lambda p, j: (p, 0, 0)), krow(2 * LANE), krow(LANE)],
        out_shape=[jax.ShapeDtypeStruct((NH // 2, t, 2 * LANE), F32), jax.ShapeDtypeStruct((t, NH * LANE), F32),
                   jax.ShapeDtypeStruct((t, DA), F32)],
        compiler_params=_cp(),
    )(qa, ka, v, a, d_a, lse, dep)


def _in_bwd(dqa, dka, dv, dyc, fl, x, dh1, g1, w_in_t, tri_u, selq, selk, band_t, dep):
    t = x.shape[0]
    tt = _tile(t)
    nt = t // tt
    hb = tt // HALO
    rev = lambda s: nt - 1 - s

    def body(dqa_ref, dka_ref, dv_ref, dyc_ref, dyn_ref, fl_ref, x_ref, dh1_ref, g_ref, w_ref, tri_ref, sq_ref, sk_ref,
             band_ref, dep_ref, dx_ref, dw_ref, dg_ref, db_ref, carry, acc, sem):
        s = pl.program_id(0)
        i = nt - 1 - s

        @pl.when(s == 0)
        def _():
            carry[...] = jnp.zeros_like(carry)
            acc[...] = jnp.zeros_like(acc)
            dg_ref[...] = jnp.zeros_like(dg_ref)
            db_ref[...] = jnp.zeros_like(db_ref)

        dq_cat = jnp.concatenate([dqa_ref[p] for p in range(NH // 2)], axis=1)
        dk_cat = dka_ref[...]
        dc = _dot2t(dq_cat, sq_ref[...]) - _dot2t(dk_cat, sk_ref[...])
        dlf = carry[...] + _dot3(tri_ref[...], dc)
        carry[...] = carry[...] + jnp.sum(dc, axis=0, keepdims=True)
        flv = fl_ref[...]
        lane = lax.broadcasted_iota(jnp.int32, flv.shape, 1)
        d_fl = jnp.where(lane < NH, dlf / (1.0 + jnp.exp(flv)), 0.0)
        db_ref[...] = db_ref[...] + jnp.sum(d_fl, axis=0, keepdims=True)
        low = lax.broadcasted_iota(jnp.int32, (tt, LANE), 1) < HD
        dqs, dks = [], []
        for p in range(NH // 2):
            b0, b1 = slice(2 * LANE * p, 2 * LANE * p + LANE), slice(2 * LANE * p + LANE, 2 * LANE * (p + 1))
            dqs.append(jnp.where(low, dq_cat[:, b0], pltpu.roll(dq_cat[:, b1], HD, 1)) * (1.0 / math.sqrt(HD)))
            dks.append(jnp.where(low, dk_cat[:, b0], pltpu.roll(dk_cat[:, b1], HD, 1)))
        nxt = dyn_ref[...] * jnp.where(i < nt - 1, 1.0, 0.0)
        tok = i * tt + lax.broadcasted_iota(jnp.int32, (tt, PC), 0)
        dus = []
        for g, w in enumerate(WINS):
            sl = slice(PC * g, PC * (g + 1))
            dycg = dyc_ref[:, sl]
            ext = jnp.concatenate([dycg, nxt[:, sl]], axis=0)
            dus.append(_dot2(band_ref[g], ext) - dycg * jnp.minimum(tok + 1, w).astype(F32))
        d_z = jnp.concatenate(dqs + dks + [dv_ref[...], d_fl] + dus, axis=1).astype(BF)
        xv = x_ref[...]
        gv = g_ref[...]
        hn = (xv * _rstd(xv) * gv).astype(BF)
        d_hn = jnp.dot(d_z, w_ref[...], preferred_element_type=F32)
        acc[...] = acc[...] + _mm_tn(d_z, hn)
        d1, dg = _rms_bwd(d_hn, xv, gv)
        dg_ref[...] = dg_ref[...] + dg
        dx_ref[...] = dh1_ref[...] + d1

        @pl.when(s == nt - 1)
        def _():
            cp = pltpu.make_async_copy(acc, dw_ref, sem)
            cp.start()
            cp.wait()

    row = lambda w: pl.BlockSpec((tt, w), lambda s: (rev(s), 0))
    return pl.pallas_call(
        body, name="in_bwd", grid=(nt,),
        in_specs=[pl.BlockSpec((NH // 2, tt, 2 * LANE), lambda s: (0, rev(s), 0)), row(NH * LANE), row(DA), row(DP),
                  pl.BlockSpec((HALO, DP), lambda s: (jnp.minimum((rev(s) + 1) * hb, nt * hb - 1), 0)),
                  row(LANE), row(D), row(D), _full((1, D)), _resident((ZW, D)), _full((tt, tt)),
                  _full((NH * LANE, LANE)), _full((NH * LANE, LANE)), _full((len(WINS), tt, tt + HALO)), ANY],
        out_specs=[row(D), pl.BlockSpec(memory_space=pl.ANY), _full((1, D)), _full((1, LANE))],
        out_shape=[jax.ShapeDtypeStruct((t, D), F32), jax.ShapeDtypeStruct((ZW, D), F32),
                   jax.ShapeDtypeStruct((1, D), F32), jax.ShapeDtypeStruct((1, LANE), F32)],
        scratch_shapes=[pltpu.VMEM((1, LANE), F32), pltpu.VMEM((ZW, D), F32), pltpu.SemaphoreType.DMA],
        compiler_params=_cp(),
    )(dqa, dka, dv, dyc, dyc, fl, x, dh1, g1, w_in_t, tri_u, selq, selk, band_t, dep)


def _dot2t(x, sel):
    hi, lo = _split2(x)
    return jnp.dot(hi, sel, preferred_element_type=F32) + jnp.dot(lo, sel, preferred_element_type=F32)


class _NoComm:
    def __init__(self, w2):
        self.w2 = w2
        self.dep = jnp.zeros((8, LANE), F32)

    def after_attention(self, after):
        return self.dep

    def weights2(self, after):
        return _stack2_full(*self.w2)

    def after_ffn(self, grads2):
        self.grads2 = grads2
        return self.dep

    def after_mix(self, after, early):
        self.early = early
        return self.dep

    def after_attn(self, after):
        return self.dep


def _local_step(x, p, tgt, sm, w1, comm):
    w_in_t, w_out = w1
    tt = _tile(x.shape[0])
    eq, ek, rowq, rowk, selq, selk = _aug_consts()
    b_pad = jnp.pad(sm["b_forget"], ((0, 0), (0, LANE - NH)))
    qa, ka, v, u, fl = _in_proj(x, sm["g_mix_pre"], w_in_t, b_pad, _tri(tt, False), eq, ek, rowq, rowk, comm.dep)
    a, lse = _attn_fwd(qa, ka, v)
    yb, m, o, h1 = _mix_out(a, u, x, _band(tt, False), sm["w_pool"], sm["pool_scale"], sm["g_attn_grp"],
                            sm["g_pool_grp"], w_out, sm["g_mix_post"], comm.after_attention(a))
    stacks2 = comm.weights2(h1)
    wg_t, wu_t, wd, w_ple, w_pg = _unstack2_full(stacks2)
    hn2, gate, up, dff, dh2, loss, dwpg, dwple, dgple, dgfpost = _ffn_fwd(
        h1, sm["g_ffn_pre"], stacks2, sm["g_ffn_post"], p, w_ple, sm["g_ple"], w_pg, tgt)
    chunks = lambda a: a.reshape(DFF // FF_CH, FF_CH, D)
    dwgu, dwd, d_hn2 = _ffn_bwd(hn2, gate, up, dff, jnp.concatenate([chunks(wg_t), chunks(wu_t)], axis=1), wd)
    dwg_t, dwu_t = dwgu[:, :FF_CH].reshape(DFF, D), dwgu[:, FF_CH:].reshape(DFF, D)
    dep = comm.after_ffn((dwg_t, dwu_t, dwd, dwple, dwpg))
    dh1, d_a, dyc, dgfpre, dgpost, dgattn, dgpool, dps, dwpool, dwout = _mix_bwd(
        d_hn2, dh2, h1, o, a, m, yb, sm["g_ffn_pre"], sm["g_mix_post"], sm["g_attn_grp"], sm["g_pool_grp"],
        w_out, sm["w_pool"], sm["pool_scale"], dep)
    early = dict(loss=loss[0:1, 0:1], g_attn_grp=dgattn, g_pool_grp=dgpool, w_pool=dwpool, pool_scale=dps,
                 g_mix_post=dgpost, g_ffn_pre=dgfpre, g_ffn_post=dgfpost, g_ple=dgple)
    dqa, dka, dvv = _attn_bwd(qa, ka, v, a, d_a, lse, comm.after_mix(dh1, early))
    dx, dwin_t, dg1, dbf = _in_bwd(dqa, dka, dvv, dyc, fl, x, dh1, sm["g_mix_pre"], w_in_t, _tri(tt, True),
                                   selq, selk, _band(tt, True), comm.after_attn(dvv))
    return dx, (dwin_t, dwout), dict(g_mix_pre=dg1, b_forget=dbf[:, :NH])


def _place():
    x, y, c = lax.axis_index("x"), lax.axis_index("y"), lax.axis_index("c")
    return x, y, c, [(1 - x, y), (x, 1 - y), (1 - x, 1 - y)]


def _rows(c, h):
    return pl.ds(pl.multiple_of(c * h, 16), h)


def _plan_gather(h):
    def plan(src, land):
        x, y, c, chips = _place()
        return [(src.at[_rows(c, h), :], land.at[2 * x + y, _rows(c, h), :], (cx, cy, c),
                 land.at[2 * cx + cy, _rows(c, h), :]) for cx, cy in chips]
    return plan


def _plan_forward(h):
    def plan(land_in, own, land):
        x, y, c, chips = _place()
        sib, me = (x, y, 1 - c), 2 * x + y
        return ([(land_in.at[2 * cx + cy, _rows(c, h), :], land.at[2 * cx + cy, _rows(c, h), :], sib,
                  land.at[2 * cx + cy, _rows(1 - c, h), :]) for cx, cy in chips]
                + [(own, land.at[me], sib, land.at[me])])
    return plan


def _plan_swap_halves(h):
    def plan(buf_in, buf):
        x, y, c, _ = _place()
        return [(buf_in.at[_rows(c, h), :], buf.at[_rows(c, h), :], (x, y, 1 - c), buf.at[_rows(1 - c, h), :])]
    return plan


def _plan_pair_rows(h):
    def plan(src, land):
        x, y, c, _ = _place()
        return [(src.at[:, _rows(1 - c, h), :], land, (x, y, 1 - c), land)]
    return plan


def _plan_scatter(src, land):
    x, y, c, chips = _place()
    return [(src.at[2 * cx + cy], land.at[k], (cx, cy, c), land.at[k]) for k, (cx, cy) in enumerate(chips)]


def _plan_all(src, land):
    x, y, c, _ = _place()
    copies = []
    for r in range(1, 8):
        px, py, pc = (1 - a if b else a for a, b in zip((x, y, c), (r >> 2 & 1, r >> 1 & 1, r & 1)))
        copies.append((src, land.at[4 * x + 2 * y + c], (px, py, pc), land.at[4 * px + 2 * py + pc]))
    return copies


def _remote(src, dst, send_sems, recv_sems, k, peer):
    return pltpu.make_async_remote_copy(src_ref=src, dst_ref=dst, send_sem=send_sems.at[k], recv_sem=recv_sems.at[k],
                                        device_id=peer, device_id_type=MESH)


def _exchange(name, n, src, land, plan):
    def body(src_ref, land_ref, send_sems, recv_sems):
        copies = plan(src_ref, land_ref)
        for k, (s, d, peer, _) in enumerate(copies):
            _remote(s, d, send_sems, recv_sems, k, peer).start()
        for k, (s, _, peer, mine) in enumerate(copies):
            _remote(s, mine, send_sems, recv_sems, k, peer).wait_recv()
        for k, (s, d, peer, _) in enumerate(copies):
            _remote(s, d, send_sems, recv_sems, k, peer).wait_send()

    return pl.pallas_call(
        body, name=name, in_specs=[ANY], out_specs=ANY, out_shape=land,
        scratch_shapes=[pltpu.SemaphoreType.DMA((n,)), pltpu.SemaphoreType.DMA((n,))],
    )(src)


def _exchange_inplace(name, n, buf, extra, plan):
    def body(*refs):
        ins, buf_ref, send_sems, recv_sems = refs[:1 + len(extra)], refs[1 + len(extra)], refs[-2], refs[-1]
        copies = plan(*ins, buf_ref)
        for k, (s, d, peer, _) in enumerate(copies):
            _remote(s, d, send_sems, recv_sems, k, peer).start()
        for k, (s, _, peer, mine) in enumerate(copies):
            _remote(s, mine, send_sems, recv_sems, k, peer).wait_recv()
        for k, (s, d, peer, _) in enumerate(copies):
            _remote(s, d, send_sems, recv_sems, k, peer).wait_send()

    return pl.pallas_call(
        body, name=name, in_specs=[ANY] * (1 + len(extra)), out_specs=ANY, out_shape=_sds(buf.shape, buf.dtype),
        input_output_aliases={0: 0},
        scratch_shapes=[pltpu.SemaphoreType.DMA((n,)), pltpu.SemaphoreType.DMA((n,))],
    )(buf, *extra)


HBM = pl.BlockSpec(memory_space=pltpu.HBM)
SEM = pl.BlockSpec(memory_space=pltpu.SEMAPHORE)
EFFECT = pltpu.SideEffectType.DATAFLOW_SIDE_EFFECTING


def _exchange_start(name, n, src, land, plan):
    def body(src_ref, land_ref, send_sems, recv_sems, src_thru, land_thru, token):
        for k, (s, d, peer, _) in enumerate(plan(src_ref, land_ref)):
            _remote(s, d, send_sems, recv_sems, k, peer).start()
        token[...] = jnp.zeros_like(token)

    return pl.pallas_call(
        body, name=name,
        out_shape=(pltpu.SemaphoreType.DMA((n,)), pltpu.SemaphoreType.DMA((n,)), pltpu.HBM(src.shape, src.dtype),
                   pltpu.HBM(land.shape, land.dtype), jax.ShapeDtypeStruct((8, LANE), F32)),
        in_specs=(HBM, HBM), out_specs=(SEM, SEM, HBM, HBM, pl.BlockSpec(memory_space=pltpu.VMEM)),
        input_output_aliases={0: 2, 1: 3},
        compiler_params=pltpu.CompilerParams(has_side_effects=EFFECT),
    )(pltpu.with_memory_space_constraint(src, pltpu.HBM), pltpu.with_memory_space_constraint(land, pltpu.HBM))


def _exchange_wait(name, started, plan, after):
    send_sems, recv_sems, src, land, _ = started

    def body(src_ref, land_ref, send_sems, recv_sems, after_ref, src_out, land_out):
        for k, (s, _, peer, mine) in enumerate(plan(src_ref, land_ref)):
            cp = _remote(s, mine, send_sems, recv_sems, k, peer)
            cp.wait_send()
            cp.wait_recv()

    return pl.pallas_call(
        body, name=name, out_shape=(pltpu.HBM(src.shape, src.dtype), pltpu.HBM(land.shape, land.dtype)),
        in_specs=(HBM, HBM, SEM, SEM, ANY), out_specs=(HBM, HBM), input_output_aliases={0: 0, 1: 1},
        compiler_params=pltpu.CompilerParams(has_side_effects=EFFECT),
    )(src, land, send_sems, recv_sems, after)


def _pair_sum(name, cidx, g, recv, br):
    h = recv.shape[1]
    nb = h // br

    def body(c_ref, g_ref, r_ref, out_ref):
        out_ref[...] = (g_ref[...] + r_ref[...]).astype(BF)

    return pl.pallas_call(
        body, name=name,
        grid_spec=pltpu.PrefetchScalarGridSpec(
            num_scalar_prefetch=1, grid=(NSHARD, nb),
            in_specs=[pl.BlockSpec((1, br, D), lambda s, i, c: (s, c[0] * nb + i, 0)),
                      pl.BlockSpec((1, br, D), lambda s, i, c: (s, i, 0))],
            out_specs=pl.BlockSpec((1, br, D), lambda s, i, c: (s, i, 0))),
        out_shape=jax.ShapeDtypeStruct((NSHARD, h, D), BF),
    )(cidx, g, recv)


def _chip_sum(name, place, pb, y, br):
    h = y.shape[1]
    nb = h // br

    def body(pl_ref, p_ref, y_ref, out_ref):
        acc = p_ref[0].astype(F32)
        for k in range(NSHARD - 1):
            acc = acc + y_ref[k].astype(F32)
        out_ref[...] = acc

    return pl.pallas_call(
        body, name=name,
        grid_spec=pltpu.PrefetchScalarGridSpec(
            num_scalar_prefetch=1, grid=(nb,),
            in_specs=[pl.BlockSpec((1, br, D), lambda i, s: (s[0], i, 0)),
                      pl.BlockSpec((NSHARD - 1, br, D), lambda i, s: (0, i, 0))],
            out_specs=pl.BlockSpec((br, D), lambda i, s: (s[1] * nb + i, 0))),
        out_shape=jax.ShapeDtypeStruct((2 * h, D), F32),
    )(place, pb, y)


def _sum_slots(v):
    def body(in_ref, out_ref):
        acc = in_ref[0]
        for k in range(1, 8):
            acc = acc + in_ref[k]
        out_ref[...] = acc

    vm = pl.BlockSpec(memory_space=pltpu.VMEM)
    return pl.pallas_call(body, name="sum_slots", in_specs=[vm], out_specs=vm,
                          out_shape=jax.ShapeDtypeStruct(v.shape[1:], F32))(v)


def _all_reduce_small(v):
    rows = v.shape[0]

    def body(in_ref, out_ref, buf, send_sems, recv_sems):
        x, y, c, _ = _place()
        me = 4 * x + 2 * y + c
        flips = [(r >> 2 & 1, r >> 1 & 1, r & 1) for r in range(1, 8)]

        def peer(f):
            return tuple(1 - a if b else a for a, b in zip((x, y, c), f))

        def copy(k, slot, to):
            return pltpu.make_async_remote_copy(src_ref=in_ref, dst_ref=buf.at[slot], send_sem=send_sems.at[k],
                                                recv_sem=recv_sems.at[k], device_id=to, device_id_type=MESH)

        sends = [copy(k, me, peer(f)) for k, f in enumerate(flips)]
        for cp in sends:
            cp.start()
        buf[me] = in_ref[...]
        for k, f in enumerate(flips):
            px, py, pc = peer(f)
            copy(k, 4 * px + 2 * py + pc, peer(f)).wait_recv()
        for cp in sends:
            cp.wait_send()
        acc = buf[0]
        for k in range(1, 8):
            acc = acc + buf[k]
        out_ref[...] = acc

    vm = pl.BlockSpec(memory_space=pltpu.VMEM)
    return pl.pallas_call(
        body, name="all_reduce_small", in_specs=[vm], out_specs=vm,
        out_shape=jax.ShapeDtypeStruct(v.shape, F32),
        scratch_shapes=[pltpu.VMEM((8, rows, LANE), F32), pltpu.SemaphoreType.DMA((7,)), pltpu.SemaphoreType.DMA((7,))],
    )(v)


def _adamw_math(w, g, m, v):
    m = ADAM_B1 * m + (1.0 - ADAM_B1) * g
    v = ADAM_B2 * v + (1.0 - ADAM_B2) * (g * g)
    m_hat = m / (1.0 - ADAM_B1 ** ADAM_STEP)
    v_hat = v / (1.0 - ADAM_B2 ** ADAM_STEP)
    delta = -ADAM_LR * (m_hat / (jnp.sqrt(v_hat) + ADAM_EPS) + ADAM_WD * w)
    return delta, m, v


def _adamw(w, g, m, v, dep):
    r, c = w.shape
    br = next(b for b in (256, 176, r) if r % b == 0)

    def body(w_ref, g_ref, m_ref, v_ref, dep_ref, d_ref, nm_ref, nv_ref):
        d_ref[...], nm_ref[...], nv_ref[...] = _adamw_math(w_ref[...], g_ref[...], m_ref[...], v_ref[...])

    spec = pl.BlockSpec((br, c), lambda i: (i, 0))
    return pl.pallas_call(
        body, name="adamw", grid=(r // br,), in_specs=[spec] * 4 + [ANY], out_specs=[spec] * 3,
        out_shape=[jax.ShapeDtypeStruct((r, c), F32)] * 3, compiler_params=_cp(),
    )(w, g, m, v, dep)


def _adamw_small(ws, gs, ms, vs):
    n = len(ws)

    def body(*refs):
        ins, outs = refs[:4 * n], refs[4 * n:]
        for k in range(n):
            d, m, v = _adamw_math(ins[k][...], ins[n + k][...], ins[2 * n + k][...], ins[3 * n + k][...])
            outs[k][...] = d
            outs[n + k][...] = m
            outs[2 * n + k][...] = v

    vm = pl.BlockSpec(memory_space=pltpu.VMEM)
    out = pl.pallas_call(
        body, name="adamw_small", in_specs=[vm] * (4 * n), out_specs=[vm] * (3 * n),
        out_shape=[jax.ShapeDtypeStruct(w.shape, F32) for w in ws] * 3,
    )(*ws, *gs, *ms, *vs)
    return out[:n], out[n:2 * n], out[2 * n:]


BIG = ("w_in", "w_out", "w_ffn_gate", "w_ffn_up", "w_ffn_down", "w_ple_proj", "w_ple_gate")
SMALL = ("g_mix_pre", "b_forget", "g_attn_grp", "g_pool_grp", "w_pool", "pool_scale", "g_mix_post", "g_ffn_pre",
         "g_ffn_post", "g_ple")
TRANSPOSED = ("w_in", "w_ffn_gate", "w_ffn_up")
VECTORS = tuple(n for n in SMALL if n != "w_pool")
ORDER = ("g_mix_pre", "w_in", "b_forget", "g_attn_grp", "g_pool_grp", "w_pool", "pool_scale", "w_out", "g_mix_post",
         "g_ffn_pre", "w_ffn_gate", "w_ffn_up", "w_ffn_down", "g_ffn_post", "w_ple_proj", "g_ple", "w_ple_gate")


def _pad_rows(a, rows):
    return jnp.pad(a, ((0, rows - a.shape[0]), (0, 0)))


def _stack1(w_in, w_out):
    return _pad_rows(jnp.concatenate([_pad_rows(w_in.T, IN_PAD), w_out], axis=0), ROWS1)


def _stack2(wg, wu, wd, wple, wpg):
    return _pad_rows(jnp.concatenate([wg.T, wu.T, wd, wple.reshape(DPLE // NSHARD, D), wpg], axis=0), ROWS2)


def _unstack1(s):
    return s[:IN_SH], s[O1_OUT:USED1]


def _unstack2(s):
    return s[:O2_U], s[O2_U:O2_D], s[O2_D:O2_PLE], s[O2_PLE:O2_PG].reshape(DPLE, DPLE), s[O2_PG:USED2]


def _cat(g, lo, hi):
    return g[:, lo:hi].reshape(NSHARD * (hi - lo), D)


def _unstack1_full(g):
    w_in_t = _cat(g, 0, IN_SH)
    w_in_t = jnp.concatenate([w_in_t[:3 * DA], _pad_rows(w_in_t[3 * DA:3 * DA + NH], LANE), w_in_t[3 * DA + NH:]], axis=0)
    return w_in_t, _cat(g, O1_OUT, USED1)


def _unstack2_full(g):
    w_ple = g[:, O2_PLE:O2_PG].reshape(NSHARD, DPLE, DPLE).transpose(1, 0, 2).reshape(DPLE, D)
    return _cat(g, 0, O2_U), _cat(g, O2_U, O2_D), _cat(g, O2_D, O2_PLE), w_ple, _cat(g, O2_PG, USED2)


def _shards(a):
    return a.reshape(NSHARD, a.shape[0] // NSHARD, D)


def _stack1_full(dwin_t, dwout):
    dwin_t = jnp.concatenate([dwin_t[:3 * DA + NH], dwin_t[3 * DA + LANE:]], axis=0).reshape(NSHARD, IN_SH, D)
    zeros = lambda r: jnp.zeros((NSHARD, r, D), F32)
    return jnp.concatenate([dwin_t, zeros(IN_PAD - IN_SH), _shards(dwout), zeros(ROWS1 - USED1)], axis=1)


def _stack2_full(dwg_t, dwu_t, dwd, dwple, dwpg):
    dwple = dwple.reshape(DPLE, NSHARD, DPLE).transpose(1, 0, 2).reshape(NSHARD, DPLE // NSHARD, D)
    return jnp.concatenate([_shards(dwg_t), _shards(dwu_t), _shards(dwd), dwple, _shards(dwpg),
                            jnp.zeros((NSHARD, ROWS2 - USED2, D), dwd.dtype)], axis=1)


def _sds(shape, dtype):
    return jax.ShapeDtypeStruct(shape, dtype)


class _Comm:
    def __init__(self, stack2, me, c):
        self.stack2, self.me, self.c = stack2, me, c
        self.cidx = c.astype(jnp.int32).reshape(1)
        self.place = jnp.stack([me, c]).astype(jnp.int32)
        self.h = ROWS2 // 2
        self.gather = _exchange_start("gather2_start", 3, stack2, lax.empty((NSHARD, ROWS2, D), BF), _plan_gather(self.h))
        self.dep = self.gather[4]

    def after_attention(self, after):
        own, land = _exchange_wait("gather2_wait", self.gather, _plan_gather(self.h), after)
        fwd = _plan_forward(self.h)
        self.forward = lambda own_ref, land_ref: fwd(land_ref, own_ref, land_ref)
        self.passing = _exchange_start("forward2_start", 4, own, land, self.forward)
        return self.passing[4]

    def weights2(self, after):
        return _exchange_wait("forward2_wait", self.passing, self.forward, after)[1]

    def after_ffn(self, grads2):
        g = _stack2_full(*grads2)
        self.pair = _exchange_start("reduce2_pair_start", 1, g, lax.empty((NSHARD, self.h, D), F32),
                                    _plan_pair_rows(self.h))
        return self.pair[4]

    def after_mix(self, after, early):
        g, recv = _exchange_wait("reduce2_pair_wait", self.pair, _plan_pair_rows(self.h), after)
        pb = _pair_sum("pair_sum2", self.cidx, g, recv, RED2)
        self.chip = _exchange_start("reduce2_chip_start", 3, pb, lax.empty((NSHARD - 1, self.h, D), BF), _plan_scatter)
        self.early_shapes = {n: early[n].shape for n in early}
        v = _pack_small(early)
        self.small = _exchange_start("small_start", 7, v, lax.empty((8,) + v.shape, F32), _plan_all)
        return self.chip[4] + self.small[4]

    def after_attn(self, after):
        pb, y = _exchange_wait("reduce2_chip_wait", self.chip, _plan_scatter, after)
        f = _chip_sum("chip_sum2", self.place, pb, y, RED2)
        self.reduced2 = _exchange_inplace("reduce2_gather", 1, f, (), _plan_swap_halves(self.h))
        v, land = _exchange_wait("small_wait", self.small, _plan_all, after)
        land = lax.dynamic_update_slice(land, v[None], (2 * self.me + self.c, 0, 0))
        self.early = _unpack_small(_sum_slots(land), self.early_shapes)
        return self.reduced2


def _pack_small(small):
    parts = []
    for name in small:
        flat = small[name].reshape(-1)
        parts.append(jnp.pad(flat, (0, -flat.shape[0] % LANE)).reshape(-1, LANE))
    v = jnp.concatenate(parts, axis=0)
    return _pad_rows(v, v.shape[0] + (-v.shape[0] % 8))


def _unpack_small(v, shapes):
    out, r = {}, 0
    for name in shapes:
        n = math.prod(shapes[name])
        rows = -(-n // LANE)
        out[name] = v[r:r + rows].reshape(-1)[:n].reshape(shapes[name])
        r += rows
    return out


def kernel(x, p, g_mix_pre, w_in, b_forget, g_attn_grp, g_pool_grp, w_pool, pool_scale, w_out, g_mix_post, g_ffn_pre, w_ffn_gate, w_ffn_up, w_ffn_down, g_ffn_post, w_ple_proj, g_ple, w_ple_gate, loss_target, m_g_mix_pre, m_w_in, m_b_forget, m_g_attn_grp, m_g_pool_grp, m_w_pool, m_pool_scale, m_w_out, m_g_mix_post, m_g_ffn_pre, m_w_ffn_gate, m_w_ffn_up, m_w_ffn_down, m_g_ffn_post, m_w_ple_proj, m_g_ple, m_w_ple_gate, v_g_mix_pre, v_w_in, v_b_forget, v_g_attn_grp, v_g_pool_grp, v_w_pool, v_pool_scale, v_w_out, v_g_mix_post, v_g_ffn_pre, v_w_ffn_gate, v_w_ffn_up, v_w_ffn_down, v_g_ffn_post, v_w_ple_proj, v_g_ple, v_w_ple_gate):
    args = dict(locals())
    strip = lambda n, a: a if n in VECTORS else a[0]
    w = {n: strip(n, args[n]) for n in ORDER}
    mom = {n: strip(n, args["m_" + n]) for n in ORDER}
    var = {n: strip(n, args["v_" + n]) for n in ORDER}
    sm = {n: w[n] for n in SMALL}

    c = lax.axis_index("c")
    me = 2 * lax.axis_index("x") + lax.axis_index("y")
    h1 = ROWS1 // 2
    bf = lambda n: w[n].astype(BF)
    stack1 = _stack1(bf("w_in"), bf("w_out"))
    stack2 = _stack2(*[bf(n) for n in BIG[2:]])
    land = _exchange("gather1", 3, stack1, _sds((NSHARD, ROWS1, D), BF), _plan_gather(h1))
    land, stack2 = lax.optimization_barrier((land, stack2))
    comm = _Comm(stack2, me, c)
    w1 = _unstack1_full(_exchange_inplace("gather1_forward", 4, land, (stack1,), _plan_forward(h1)))
    dx, grads1, late = _local_step(x[0], p[0, 0], loss_target[0], sm, w1, comm)

    late_shapes = {n: late[n].shape for n in late}
    red_small = {**comm.early, **_unpack_small(_all_reduce_small(_pack_small(late)), late_shapes)}
    loss = 0.5 / D * red_small["loss"][0, 0]

    g1 = _stack1_full(*grads1)
    recv = _exchange("reduce1_pair", 1, g1, _sds((NSHARD, h1, D), F32), _plan_pair_rows(h1))
    pb = _pair_sum("pair_sum1", comm.cidx, g1, recv, RED1)
    chip1 = _exchange_start("reduce1_chip_start", 3, pb, lax.empty((NSHARD - 1, h1, D), BF), _plan_scatter)

    flip = lambda n, a: a.T if n in TRANSPOSED else a
    grads, delta, new_m, new_v = {}, {}, {}, {}

    def update(names, shards, dep):
        for n, g in zip(names, shards):
            d_, m_, v_ = _adamw(flip(n, w[n]), g, flip(n, mom[n]), flip(n, var[n]), dep)
            grads[n], delta[n], new_m[n], new_v[n] = flip(n, g), flip(n, d_), flip(n, m_), flip(n, v_)

    update(BIG[2:], _unstack2(comm.reduced2), chip1[4])
    pb, y = _exchange_wait("reduce1_chip_wait", chip1, _plan_scatter, new_v[BIG[-1]])
    f = _chip_sum("chip_sum1", comm.place, pb, y, RED1)
    reduced1 = _exchange_inplace("reduce1_gather", 1, f, (), _plan_swap_halves(h1))
    update(BIG[:2], _unstack1(reduced1), reduced1)
    for n in SMALL:
        grads[n] = red_small[n].reshape(w[n].shape)
    two_d = lambda a: a.reshape(-1, a.shape[-1])
    ds, ms, vs = _adamw_small([two_d(w[n]) for n in SMALL], [two_d(grads[n]) for n in SMALL],
                              [two_d(mom[n]) for n in SMALL], [two_d(var[n]) for n in SMALL])
    for k, n in enumerate(SMALL):
        delta[n], new_m[n], new_v[n] = ds[k].reshape(w[n].shape), ms[k].reshape(w[n].shape), vs[k].reshape(w[n].shape)

    lead = lambda d: [d[n] if n in VECTORS else d[n][None] for n in ORDER]
    return (loss, dx[None], *lead(grads), *lead(delta), *lead(new_m), *lead(new_v))
```

```python
import functools
import math

import jax
import jax.numpy as jnp
import numpy as np
from jax import lax
from jax.experimental import pallas as pl
from jax.experimental.pallas import tpu as pltpu

F32 = jnp.float32
BF = jnp.bfloat16
MESH = pl.DeviceIdType.MESH

D = 1024
DA = 512
DP = 512
NH = 8
HD = 64
DFF = 2816
DPLE = 256
WINS = (2, 4, 8, 16)
PC = 128
ZW = 3 * DA + 128 + DP
EPS = 1e-6
NSHARD = 4

LANE = 128
HALO = 128

IN_SH = 514
IN_PAD = 528
FF_SH = DFF // NSHARD
O1_OUT, USED1, ROWS1 = 528, 784, 800
O2_U, O2_D, O2_PLE, O2_PG, USED2, ROWS2 = 704, 1408, 2112, 2176, 2432, 2560
RED1, RED2 = 400, 640

ADAM_LR, ADAM_B1, ADAM_B2, ADAM_EPS, ADAM_WD, ADAM_STEP = 0.001, 0.9, 0.999, 1e-8, 0.01, 10

VMEM_LIMIT = 56 * 1024 * 1024


def _cp(**kw):
    return pltpu.CompilerParams(vmem_limit_bytes=VMEM_LIMIT, **kw)


def _mm(a, b):
    return jnp.dot(a.astype(BF), b.astype(BF), preferred_element_type=F32)


def _mm_nt(a, b):
    return lax.dot_general(a.astype(BF), b.astype(BF), (((1,), (1,)), ((), ())), preferred_element_type=F32)


def _mm_tn(a, b):
    return lax.dot_general(a.astype(BF), b.astype(BF), (((0,), (0,)), ((), ())), preferred_element_type=F32)


def _split2(x):
    hi = x.astype(BF)
    lo = (x - hi.astype(F32)).astype(BF)
    return hi, lo


def _split3(x):
    hi = x.astype(BF)
    r = x - hi.astype(F32)
    mid = r.astype(BF)
    lo = (r - mid.astype(F32)).astype(BF)
    return hi, mid, lo


def _dot3(m, x):
    hi, mid, lo = _split3(x)
    return (jnp.dot(m, hi, preferred_element_type=F32) + jnp.dot(m, mid, preferred_element_type=F32)
            + jnp.dot(m, lo, preferred_element_type=F32))


def _dot2(m, x):
    hi, lo = _split2(x)
    return jnp.dot(m, hi, preferred_element_type=F32) + jnp.dot(m, lo, preferred_element_type=F32)


def _rstd(x):
    return lax.rsqrt(jnp.mean(x * x, axis=-1, keepdims=True) + EPS)


def _rms_bwd(dy, x, g):
    r = _rstd(x)
    xh = x * r
    dg = jnp.sum(dy * xh, axis=0, keepdims=True)
    dxh = dy * g
    dx = r * (dxh - xh * jnp.mean(dxh * xh, axis=-1, keepdims=True))
    return dx, dg


def _sigmoid(x):
    return 1.0 / (1.0 + jnp.exp(-x))


ANY = pl.BlockSpec(memory_space=pl.ANY)


def _full(shape):
    n = len(shape)
    return pl.BlockSpec(shape, lambda *_: (0,) * n)


def _resident(shape):
    n = len(shape)
    return pl.BlockSpec(shape, lambda *_: (0,) * n, pipeline_mode=pl.Buffered(1))


def _tile(t):
    return 512 if t % 512 == 0 else t


def _tri(n, upper):
    r, c = np.indices((n, n))
    return ((c >= r) if upper else (c <= r)).astype(BF)


def _band(tt, transpose):
    r, c = np.indices((tt, tt + HALO))
    d = (c - r) if transpose else (r + HALO - c)
    return np.stack([((d >= 0) & (d < w)).astype(BF) for w in WINS])


def _aug_consts():
    row, col = np.indices((3 * LANE, NH * LANE))
    piece, head = row // LANE, row % LANE
    ch, cl = col // LANE, col % LANE
    eq = ((head == ch) & (cl == HD + piece)).astype(BF)
    ek = -((head == ch) & (cl == HD + 3 + piece)).astype(BF)
    lane = np.arange(NH * LANE)[None, :] % LANE
    rowq = ((lane >= HD + 3) & (lane < HD + 6)).astype(np.float32)
    rowk = ((lane >= HD) & (lane < HD + 3)).astype(np.float32)
    r2, c2 = np.indices((NH * LANE, LANE))
    selq = ((r2 // LANE == c2) & (r2 % LANE == HD)).astype(BF)
    selk = ((r2 // LANE == c2) & (r2 % LANE == HD + 3)).astype(BF)
    return eq, ek, rowq, rowk, selq, selk


def _in_proj(x, g1, w_in_t, b_pad, tri, eq, ek, rowq, rowk, dep):
    t = x.shape[0]
    tt = _tile(t)

    def body(x_ref, g_ref, w_ref, b_ref, tri_ref, eq_ref, ek_ref, rq_ref, rk_ref, dep_ref,
             qa_ref, ka_ref, v_ref, u_ref, fl_ref, carry):
        i = pl.program_id(0)

        @pl.when(i == 0)
        def _():
            carry[...] = jnp.zeros_like(carry)

        xv = x_ref[...]
        hn = (xv * _rstd(xv) * g_ref[...]).astype(BF)
        z = _mm_nt(hn, w_ref[...])
        fl = z[:, 3 * DA:3 * DA + LANE] + b_ref[...]
        lane = lax.broadcasted_iota(jnp.int32, fl.shape, 1)
        lf = jnp.where(lane < NH, jnp.minimum(fl, 0.0) - jnp.log(1.0 + jnp.exp(-jnp.abs(fl))), 0.0)
        c = carry[...] + _dot3(tri_ref[...], lf)
        carry[...] = carry[...] + jnp.sum(lf, axis=0, keepdims=True)
        caug = jnp.concatenate(_split3(c), axis=1)
        aug_q = jnp.dot(caug, eq_ref[...], preferred_element_type=F32) + rq_ref[...]
        aug_k = jnp.dot(caug, ek_ref[...], preferred_element_type=F32) + rk_ref[...]
        low = lax.broadcasted_iota(jnp.int32, (tt, LANE), 1) < HD
        for p in range(NH // 2):
            qp = z[:, LANE * p:LANE * (p + 1)] * (1.0 / math.sqrt(HD))
            kp = z[:, DA + LANE * p:DA + LANE * (p + 1)]
            for h, (qh, kh) in enumerate(((qp, kp), (pltpu.roll(qp, HD, 1), pltpu.roll(kp, HD, 1)))):
                lo_, hi_ = LANE * (2 * p + h), LANE * (2 * p + h + 1)
                qa_ref[:, lo_:hi_] = jnp.where(low, qh, aug_q[:, lo_:hi_]).astype(BF)
                ka_ref[:, lo_:hi_] = jnp.where(low, kh, aug_k[:, lo_:hi_]).astype(BF)
        v_ref[...] = z[:, 2 * DA:3 * DA].astype(BF)
        u_ref[...] = z[:, 3 * DA + LANE:]
        fl_ref[...] = fl

    return pl.pallas_call(
        body, name="in_proj", grid=(t // tt,),
        in_specs=[pl.BlockSpec((tt, D), lambda i: (i, 0)), _full((1, D)), _resident((ZW, D)), _full((1, LANE)),
                  _full((tt, tt)), _full((3 * LANE, NH * LANE)), _full((3 * LANE, NH * LANE)),
                  _full((1, NH * LANE)), _full((1, NH * LANE)), ANY],
        out_specs=[pl.BlockSpec((tt, NH * LANE), lambda i: (i, 0)), pl.BlockSpec((tt, NH * LANE), lambda i: (i, 0)),
                   pl.BlockSpec((tt, DA), lambda i: (i, 0)), pl.BlockSpec((tt, DP), lambda i: (i, 0)),
                   pl.BlockSpec((tt, LANE), lambda i: (i, 0))],
        out_shape=[jax.ShapeDtypeStruct((t, NH * LANE), BF), jax.ShapeDtypeStruct((t, NH * LANE), BF),
                   jax.ShapeDtypeStruct((t, DA), BF), jax.ShapeDtypeStruct((t, DP), F32),
                   jax.ShapeDtypeStruct((t, LANE), F32)],
        scratch_shapes=[pltpu.VMEM((1, LANE), F32)],
        compiler_params=_cp(),
    )(x, g1, w_in_t, b_pad, tri, eq, ek, rowq, rowk, dep)


def _attn_fwd(qa, ka, v):
    t = qa.shape[0]
    ta = _tile(t)
    n = t // ta

    def body(q_ref, k_ref, v_ref, a_ref, lse_ref, m_ref, l_ref, acc_ref):
        i = pl.program_id(1)
        m_ref[...] = jnp.full_like(m_ref, -1e30)
        l_ref[...] = jnp.zeros_like(l_ref)
        acc_ref[...] = jnp.zeros_like(acc_ref)
        qs = [q_ref[:, LANE * h:LANE * (h + 1)] for h in range(2)]
        reps = ta // LANE

        def tile(j, masked):
            rows = pl.ds(pl.multiple_of(j * ta, ta), ta)
            v2 = v_ref[rows, :]
            s = [_mm_nt(qs[h], k_ref[rows, LANE * h:LANE * (h + 1)]) for h in range(2)]
            if masked:
                keep = (lax.broadcasted_iota(jnp.int32, (ta, ta), 1) <= lax.broadcasted_iota(jnp.int32, (ta, ta), 0))
                s = [jnp.where(keep, sh, -1e30) for sh in s]
            m_old = [m_ref[h] for h in range(2)]
            m_new = [jnp.maximum(m_old[h], jnp.max(s[h], axis=1, keepdims=True)) for h in range(2)]
            pe = [jnp.exp(s[h] - jnp.tile(m_new[h], (1, reps))) for h in range(2)]
            alpha = [jnp.exp(m_old[h] - m_new[h]) for h in range(2)]
            pv = [jnp.dot(pe[h].astype(BF), v2, preferred_element_type=F32) for h in range(2)]
            for h in range(2):
                l_ref[h] = alpha[h] * l_ref[h] + jnp.sum(pe[h], axis=1, keepdims=True)
                acc_ref[h] = alpha[h] * acc_ref[h] + pv[h]
                m_ref[h] = m_new[h]

        def step(j, carry):
            tile(j, False)
            return carry

        lax.fori_loop(0, i, step, 0)
        tile(i, True)
        low = lax.broadcasted_iota(jnp.int32, (ta, LANE), 1) < HD
        a_ref[...] = jnp.where(low, acc_ref[0] / l_ref[0], acc_ref[1] / l_ref[1])
        lse_ref[...] = jnp.where(low, m_ref[0] + jnp.log(l_ref[0]), m_ref[1] + jnp.log(l_ref[1]))

    return pl.pallas_call(
        body, name="attn_fwd", grid=(NH // 2, n),
        in_specs=[pl.BlockSpec((ta, 2 * LANE), lambda p, i: (i, p)),
                  pl.BlockSpec((t, 2 * LANE), lambda p, i: (0, p)),
                  pl.BlockSpec((t, LANE), lambda p, i: (0, p))],
        out_specs=[pl.BlockSpec((ta, LANE), lambda p, i: (i, p)), pl.BlockSpec((ta, LANE), lambda p, i: (i, p))],
        out_shape=[jax.ShapeDtypeStruct((t, DA), F32), jax.ShapeDtypeStruct((t, DA), F32)],
        scratch_shapes=[pltpu.VMEM((2, ta, LANE), F32), pltpu.VMEM((2, ta, LANE), F32), pltpu.VMEM((2, ta, LANE), F32)],
        compiler_params=_cp(),
    )(qa, ka, v)


def _mix_out(a, u, x, band, w_pool, pool_scale, g_attn, g_pool, w_out, g_post, dep):
    t = a.shape[0]
    tt = _tile(t)
    hb = tt // HALO

    def body(a_ref, u_ref, up_ref, x_ref, band_ref, wp_ref, ps_ref, ga_ref, gp_ref, wo_ref, go_ref, dep_ref,
             yb_ref, m_ref, o_ref, h1_ref):
        i = pl.program_id(0)
        prev = up_ref[...] * jnp.where(i > 0, 1.0, 0.0)
        tok = i * tt + lax.broadcasted_iota(jnp.int32, (tt, PC), 0)
        ms = []
        for g, w in enumerate(WINS):
            ug = u_ref[:, PC * g:PC * (g + 1)]
            ext = jnp.concatenate([prev[:, PC * g:PC * (g + 1)], ug], axis=0)
            cnt = jnp.minimum(tok + 1, w).astype(F32)
            y = (_dot2(band_ref[g], ext) / cnt - ug).astype(BF)
            yb_ref[:, PC * g:PC * (g + 1)] = y
            ms.append(_mm(y, wp_ref[g]) * ps_ref[:, PC * g:PC * (g + 1)])
        m = jnp.concatenate(ms, axis=1)
        m_ref[...] = m
        av = a_ref[...]
        mix = jnp.concatenate([av * _rstd(av) * ga_ref[...], m * _rstd(m) * gp_ref[...]], axis=1)
        o = _mm(mix, wo_ref[...])
        o_ref[...] = o
        h1_ref[...] = x_ref[...] + o * _rstd(o) * go_ref[...]

    return pl.pallas_call(
        body, name="mix_out", grid=(t // tt,),
        in_specs=[pl.BlockSpec((tt, DA), lambda i: (i, 0)), pl.BlockSpec((tt, DP), lambda i: (i, 0)),
                  pl.BlockSpec((HALO, DP), lambda i: (jnp.maximum(i * hb - 1, 0), 0)),
                  pl.BlockSpec((tt, D), lambda i: (i, 0)), _full((len(WINS), tt, tt + HALO)),
                  _full((len(WINS), PC, PC)), _full((1, DP)), _full((1, DA)), _full((1, DP)),
                  _resident((D, D)), _full((1, D)), ANY],
        out_specs=[pl.BlockSpec((tt, DP), lambda i: (i, 0)), pl.BlockSpec((tt, DP), lambda i: (i, 0)),
                   pl.BlockSpec((tt, D), lambda i: (i, 0)), pl.BlockSpec((tt, D), lambda i: (i, 0))],
        out_shape=[jax.ShapeDtypeStruct((t, DP), BF), jax.ShapeDtypeStruct((t, DP), F32),
                   jax.ShapeDtypeStruct((t, D), F32), jax.ShapeDtypeStruct((t, D), F32)],
        compiler_params=_cp(),
    )(a, u, u, x, band, w_pool, pool_scale, g_attn, g_pool, w_out, g_post, dep)


def _ffn_fwd(h1, g_pre, stacks2, g_post, p, w_ple, g_ple, w_pg, tgt):
    t = h1.shape[0]
    tt = 256 if t % 256 == 0 else t

    def body(h1_ref, gpre_ref, wg_ref, wu_ref, wd_ref, gpost_ref, p_ref, wple_ref, gple_ref, wpg_ref, tgt_ref,
             hn_ref, gate_ref, up_ref, dff_ref, dh2_ref, loss_ref, dwpg_ref, dwple_ref, dgple_ref, dgpost_ref):
        i = pl.program_id(0)

        @pl.when(i == 0)
        def _():
            loss_ref[...] = jnp.zeros_like(loss_ref)
            dwpg_ref[...] = jnp.zeros_like(dwpg_ref)
            dwple_ref[...] = jnp.zeros_like(dwple_ref)
            dgple_ref[...] = jnp.zeros_like(dgple_ref)
            dgpost_ref[...] = jnp.zeros_like(dgpost_ref)

        h1v = h1_ref[...]
        hn = (h1v * _rstd(h1v) * gpre_ref[...]).astype(BF)
        hn_ref[...] = hn
        gate = _mm_nt(hn, wg_ref[...].reshape(DFF, D))
        up = _mm_nt(hn, wu_ref[...].reshape(DFF, D))
        gate_ref[...] = gate.astype(BF)
        up_ref[...] = up.astype(BF)
        ff = _mm(gate * _sigmoid(gate) * up, wd_ref[...].reshape(DFF, D))
        rff = _rstd(ff)
        ffh = ff * rff
        gpost = gpost_ref[...]
        h2 = h1v + ffh * gpost
        pv = p_ref[...]
        pe = _mm(pv, wple_ref[...])
        rpe = _rstd(pe)
        peh = pe * rpe
        gple = gple_ref[...]
        e = peh * gple
        sig = _sigmoid(_mm(h2, wpg_ref[...]))
        dv = h2 + sig * e - tgt_ref[...]
        sq = jnp.sum(jnp.sum(dv * dv, axis=1, keepdims=True), axis=0, keepdims=True)
        loss_ref[...] = loss_ref[...] + sq
        dy = dv * (1.0 / D)
        d_e = dy * sig
        d_gl = dy * e * sig * (1.0 - sig)
        dh2 = dy + _mm_nt(d_gl, wpg_ref[...])
        dh2_ref[...] = dh2
        dwpg_ref[...] = dwpg_ref[...] + _mm_tn(h2, d_gl)
        dgple_ref[...] = dgple_ref[...] + jnp.sum(d_e * peh, axis=0, keepdims=True)
        dpeh = d_e * gple
        d_pe = rpe * (dpeh - peh * jnp.mean(dpeh * peh, axis=-1, keepdims=True))
        dwple_ref[...] = dwple_ref[...] + _mm_tn(pv, d_pe)
        dgpost_ref[...] = dgpost_ref[...] + jnp.sum(dh2 * ffh, axis=0, keepdims=True)
        dffh = dh2 * gpost
        dff_ref[...] = (rff * (dffh - ffh * jnp.mean(dffh * ffh, axis=-1, keepdims=True))).astype(BF)

    row = lambda w: pl.BlockSpec((tt, w), lambda i: (i, 0))
    shard_rows = lambda k: pl.BlockSpec((NSHARD, FF_SH, D), lambda i: (0, k, 0), pipeline_mode=pl.Buffered(1))
    return pl.pallas_call(
        body, name="ffn_fwd", grid=(t // tt,),
        in_specs=[row(D), _full((1, D)), shard_rows(0), shard_rows(1), shard_rows(2), _full((1, D)),
                  row(DPLE), _resident((DPLE, D)), _full((1, D)), _resident((D, D)), row(D)],
        out_specs=[row(D), row(DFF), row(DFF), row(D), row(D), _full((8, LANE)), _full((D, D)), _full((DPLE, D)),
                   _full((1, D)), _full((1, D))],
        out_shape=[jax.ShapeDtypeStruct((t, D), BF), jax.ShapeDtypeStruct((t, DFF), BF), jax.ShapeDtypeStruct((t, DFF), BF),
                   jax.ShapeDtypeStruct((t, D), BF), jax.ShapeDtypeStruct((t, D), F32), jax.ShapeDtypeStruct((8, LANE), F32),
                   jax.ShapeDtypeStruct((D, D), F32), jax.ShapeDtypeStruct((DPLE, D), F32),
                   jax.ShapeDtypeStruct((1, D), F32), jax.ShapeDtypeStruct((1, D), F32)],
        compiler_params=_cp(),
    )(h1, g_pre, stacks2, stacks2, stacks2, g_post, p, w_ple, g_ple, w_pg, tgt)


FF_CH = 256


def _ffn_bwd(hn2, gate, up, dff, wgu, wd):
    t = hn2.shape[0]
    tt = 1024 if t % 1024 == 0 else _tile(t)
    nt = t // tt
    ch = FF_CH
    nc = DFF // ch

    def body(hn_ref, gate_ref, up_ref, dff_ref, wgu_ref, wd_ref,
             dwg_ref, dwu_ref, dwd_ref, dhn_ref, acc, sem):
        j, i = pl.program_id(0), pl.program_id(1)

        @pl.when(j == 0)
        def _():
            acc[pl.ds(pl.multiple_of(i * tt, tt), tt), :] = jnp.zeros((tt, D), F32)

        @pl.when(i == 0)
        def _():
            dwg_ref[...] = jnp.zeros_like(dwg_ref)
            dwu_ref[...] = jnp.zeros_like(dwu_ref)
            dwd_ref[...] = jnp.zeros_like(dwd_ref)

        half = tt // 2
        acts, dgus = [], []
        for hh in range(2):
            r = slice(hh * half, (hh + 1) * half)
            gate_v = gate_ref[r, :].astype(F32)
            up_v = up_ref[r, :].astype(F32)
            sg = _sigmoid(gate_v)
            silu = gate_v * sg
            d_act = _mm_nt(dff_ref[r, :], wd_ref[...])
            d_up = (d_act * silu).astype(BF)
            d_gate = (d_act * up_v * (sg * (1.0 + gate_v * (1.0 - sg)))).astype(BF)
            dgu = jnp.concatenate([d_gate, d_up], axis=1)
            rows = pl.ds(pl.multiple_of(i * tt + hh * half, half), half)
            acc[rows, :] = acc[rows, :] + jnp.dot(dgu, wgu_ref[0], preferred_element_type=F32)
            acts.append((silu * up_v).astype(BF))
            dgus.append(dgu)
        dwd_ref[...] = dwd_ref[...] + _mm_tn(jnp.concatenate(acts, axis=0), dff_ref[...])
        dwgu = _mm_tn(jnp.concatenate(dgus, axis=0), hn_ref[...])
        dwg_ref[...] = dwg_ref[...] + dwgu[:ch]
        dwu_ref[...] = dwu_ref[...] + dwgu[ch:]

        @pl.when((j == nc - 1) & (i == nt - 1))
        def _():
            cp = pltpu.make_async_copy(acc, dhn_ref, sem)
            cp.start()
            cp.wait()

    tok = lambda w: pl.BlockSpec((tt, w), lambda j, i: (i, 0))
    chunk = pl.BlockSpec((ch, D), lambda j, i: (j, 0))
    pair = pl.BlockSpec((1, 2 * ch, D), lambda j, i: (j, 0, 0))
    return pl.pallas_call(
        body, name="ffn_bwd", grid=(nc, nt),
        in_specs=[tok(D), pl.BlockSpec((tt, ch), lambda j, i: (i, j)), pl.BlockSpec((tt, ch), lambda j, i: (i, j)),
                  tok(D), pair, chunk],
        out_specs=[chunk, chunk, chunk, pl.BlockSpec(memory_space=pl.ANY)],
        out_shape=[jax.ShapeDtypeStruct((DFF, D), F32), jax.ShapeDtypeStruct((DFF, D), F32),
                   jax.ShapeDtypeStruct((DFF, D), F32), jax.ShapeDtypeStruct((t, D), F32)],
        scratch_shapes=[pltpu.VMEM((t, D), F32), pltpu.SemaphoreType.DMA],
        compiler_params=_cp(),
    )(hn2, gate, up, dff, wgu, wd)


def _mix_bwd(d_hn2, dh2, h1, o, a, m, yb, g_ffn_pre, g_post, g_attn, g_pool, w_out, w_pool, pool_scale, dep):
    t = a.shape[0]
    tt = 256 if t % 256 == 0 else t

    def body(dhn_ref, dh2_ref, h1_ref, o_ref, a_ref, m_ref, yb_ref, gfp_ref, go_ref, ga_ref, gp_ref, wo_ref, wp_ref,
             ps_ref, dep_ref, dh1_ref, da_ref, dyc_ref, dgfp_ref, dgo_ref, dga_ref, dgp_ref, dps_ref, dwp_ref, dwo_ref):
        i = pl.program_id(0)

        @pl.when(i == 0)
        def _():
            for r in (dgfp_ref, dgo_ref, dga_ref, dgp_ref, dps_ref, dwp_ref, dwo_ref):
                r[...] = jnp.zeros_like(r)

        d1, dg = _rms_bwd(dhn_ref[...], h1_ref[...], gfp_ref[...])
        dgfp_ref[...] = dgfp_ref[...] + dg
        dh1 = dh2_ref[...] + d1
        dh1_ref[...] = dh1
        d_o, dg = _rms_bwd(dh1, o_ref[...], go_ref[...])
        dgo_ref[...] = dgo_ref[...] + dg
        d_mix = _mm_nt(d_o, wo_ref[...])
        av, mv = a_ref[...], m_ref[...]
        mix = jnp.concatenate([av * _rstd(av) * ga_ref[...], mv * _rstd(mv) * gp_ref[...]], axis=1)
        dwo_ref[...] = dwo_ref[...] + _mm_tn(mix, d_o)
        d_a, dg = _rms_bwd(d_mix[:, :DA], av, ga_ref[...])
        dga_ref[...] = dga_ref[...] + dg
        da_ref[...] = d_a
        d_m, dg = _rms_bwd(d_mix[:, DA:], mv, gp_ref[...])
        dgp_ref[...] = dgp_ref[...] + dg
        tok = i * tt + lax.broadcasted_iota(jnp.int32, (tt, PC), 0)
        dps = []
        for g, w in enumerate(WINS):
            sl = slice(PC * g, PC * (g + 1))
            ybg = yb_ref[:, sl]
            wpg = wp_ref[g].astype(BF)
            mlin = jnp.dot(ybg, wpg, preferred_element_type=F32)
            dmg = d_m[:, sl]
            dps.append(jnp.sum(dmg * mlin, axis=0, keepdims=True))
            dml = (dmg * ps_ref[:, sl]).astype(BF)
            dwp_ref[g] = dwp_ref[g] + _mm_tn(ybg, dml)
            dyc_ref[:, sl] = _mm_nt(dml, wpg) / jnp.minimum(tok + 1, w).astype(F32)
        dps_ref[...] = dps_ref[...] + jnp.concatenate(dps, axis=1)

    row = lambda w: pl.BlockSpec((tt, w), lambda i: (i, 0))
    return pl.pallas_call(
        body, name="mix_bwd", grid=(t // tt,),
        in_specs=[row(D), row(D), row(D), row(D), row(DA), row(DP), row(DP), _full((1, D)), _full((1, D)),
                  _full((1, DA)), _full((1, DP)), _resident((D, D)), _full((len(WINS), PC, PC)), _full((1, DP)), ANY],
        out_specs=[row(D), row(DA), row(DP), _full((1, D)), _full((1, D)), _full((1, DA)), _full((1, DP)),
                   _full((1, DP)), _full((len(WINS), PC, PC)), _full((D, D))],
        out_shape=[jax.ShapeDtypeStruct((t, D), F32), jax.ShapeDtypeStruct((t, DA), F32), jax.ShapeDtypeStruct((t, DP), F32),
                   jax.ShapeDtypeStruct((1, D), F32), jax.ShapeDtypeStruct((1, D), F32), jax.ShapeDtypeStruct((1, DA), F32),
                   jax.ShapeDtypeStruct((1, DP), F32), jax.ShapeDtypeStruct((1, DP), F32),
                   jax.ShapeDtypeStruct((len(WINS), PC, PC), F32), jax.ShapeDtypeStruct((D, D), F32)],
        compiler_params=_cp(),
    )(d_hn2, dh2, h1, o, a, m, yb, g_ffn_pre, g_post, g_attn, g_pool, w_out, w_pool, pool_scale, dep)


def _attn_bwd(qa, ka, v, a, d_a, lse, dep):
    t = qa.shape[0]
    ta = _tile(t)
    n = t // ta

    def body(q_ref, k_ref, v_ref, o_ref, do_ref, lse_ref, dep_ref, dq_ref, dk_ref, dv_ref):
        j = pl.program_id(1)

        @pl.when(j == 0)
        def _():
            dq_ref[...] = jnp.zeros_like(dq_ref)

        dk_ref[...] = jnp.zeros_like(dk_ref)
        dv_ref[...] = jnp.zeros_like(dv_ref)
        ks = [k_ref[:, LANE * h:LANE * (h + 1)] for h in range(2)]
        v2 = v_ref[...]
        lane = lax.broadcasted_iota(jnp.int32, (ta, LANE), 1)
        mine = [lane < HD, lane >= HD]

        def tile(i, masked):
            rows = pl.ds(pl.multiple_of(i * ta, ta), ta)
            do2 = do_ref[rows, :]
            prod = do2 * o_ref[rows, :]
            lse2 = lse_ref[rows, :]
            do2b = do2.astype(BF)
            qh = [q_ref[rows, LANE * h:LANE * (h + 1)] for h in range(2)]
            s = [_mm_nt(qh[h], ks[h]) for h in range(2)]
            dp = [_mm_nt(jnp.where(mine[h], do2, 0.0), v2) for h in range(2)]
            delta = [jnp.sum(jnp.where(mine[h], prod, 0.0), axis=1, keepdims=True) for h in range(2)]
            lse_h = [jnp.sum(jnp.where(lane == HD * h, lse2, 0.0), axis=1, keepdims=True) for h in range(2)]
            pr = [jnp.exp(s[h] - lse_h[h]) for h in range(2)]
            if masked:
                keep = (lax.broadcasted_iota(jnp.int32, (ta, ta), 1) <= lax.broadcasted_iota(jnp.int32, (ta, ta), 0))
                pr = [jnp.where(keep, ph, 0.0) for ph in pr]
            ds = [(pr[h] * (dp[h] - delta[h])).astype(BF) for h in range(2)]
            dv_ref[...] = dv_ref[...] + jnp.where(mine[0], _mm_tn(pr[0], do2b), _mm_tn(pr[1], do2b))
            for h in range(2):
                sl = slice(LANE * h, LANE * (h + 1))
                dk_ref[:, sl] = dk_ref[:, sl] + _mm_tn(ds[h], qh[h])
                dq_ref[0, rows, sl] = dq_ref[0, rows, sl] + jnp.dot(ds[h], ks[h], preferred_element_type=F32)

        def step(i, carry):
            tile(i, False)
            return carry

        tile(j, True)
        lax.fori_loop(j + 1, n, step, 0)

    qrow = lambda w: pl.BlockSpec((t, w), lambda p, j: (0, p))
    krow = lambda w: pl.BlockSpec((ta, w), lambda p, j: (j, p))
    return pl.pallas_call(
        body, name="attn_bwd", grid=(NH // 2, n),
        in_specs=[qrow(2 * LANE), krow(2 * LANE), krow(LANE), qrow(LANE), qrow(LANE), qrow(LANE), ANY],
        out_specs=[pl.BlockSpec((1, t, 2 * LANE), lambda p, j: (p, 0, 0)), krow(2 * LANE), krow(LANE)],
        out_shape=[jax.ShapeDtypeStruct((NH // 2, t, 2 * LANE), F32), jax.ShapeDtypeStruct((t, NH * LANE), F32),
                   jax.ShapeDtypeStruct((t, DA), F32)],
        compiler_params=_cp(),
    )(qa, ka, v, a, d_a, lse, dep)


def _in_bwd(dqa, dka, dv, dyc, fl, x, dh1, g1, w_in_t, tri_u, selq, selk, band_t, dep):
    t = x.shape[0]
    tt = _tile(t)
    nt = t // tt
    hb = tt // HALO
    rev = lambda s: nt - 1 - s

    def body(dqa_ref, dka_ref, dv_ref, dyc_ref, dyn_ref, fl_ref, x_ref, dh1_ref, g_ref, w_ref, tri_ref, sq_ref, sk_ref,
             band_ref, dep_ref, dx_ref, dw_ref, dg_ref, db_ref, carry, acc, sem):
        s = pl.program_id(0)
        i = nt - 1 - s

        @pl.when(s == 0)
        def _():
            carry[...] = jnp.zeros_like(carry)
            acc[...] = jnp.zeros_like(acc)
            dg_ref[...] = jnp.zeros_like(dg_ref)
            db_ref[...] = jnp.zeros_like(db_ref)

        dq_cat = jnp.concatenate([dqa_ref[p] for p in range(NH // 2)], axis=1)
        dk_cat = dka_ref[...]
        dc = _dot2t(dq_cat, sq_ref[...]) - _dot2t(dk_cat, sk_ref[...])
        dlf = carry[...] + _dot3(tri_ref[...], dc)
        carry[...] = carry[...] + jnp.sum(dc, axis=0, keepdims=True)
        flv = fl_ref[...]
        lane = lax.broadcasted_iota(jnp.int32, flv.shape, 1)
        d_fl = jnp.where(lane < NH, dlf / (1.0 + jnp.exp(flv)), 0.0)
        db_ref[...] = db_ref[...] + jnp.sum(d_fl, axis=0, keepdims=True)
        low = lax.broadcasted_iota(jnp.int32, (tt, LANE), 1) < HD
        dqs, dks = [], []
        for p in range(NH // 2):
            b0, b1 = slice(2 * LANE * p, 2 * LANE * p + LANE), slice(2 * LANE * p + LANE, 2 * LANE * (p + 1))
            dqs.append(jnp.where(low, dq_cat[:, b0], pltpu.roll(dq_cat[:, b1], HD, 1)) * (1.0 / math.sqrt(HD)))
            dks.append(jnp.where(low, dk_cat[:, b0], pltpu.roll(dk_cat[:, b1], HD, 1)))
        nxt = dyn_ref[...] * jnp.where(i < nt - 1, 1.0, 0.0)
        tok = i * tt + lax.broadcasted_iota(jnp.int32, (tt, PC), 0)
        dus = []
        for g, w in enumerate(WINS):
            sl = slice(PC * g, PC * (g + 1))
            dycg = dyc_ref[:, sl]
            ext = jnp.concatenate([dycg, nxt[:, sl]], axis=0)
            dus.append(_dot2(band_ref[g], ext) - dycg * jnp.minimum(tok + 1, w).astype(F32))
        d_z = jnp.concatenate(dqs + dks + [dv_ref[...], d_fl] + dus, axis=1).astype(BF)
        xv = x_ref[...]
        gv = g_ref[...]
        hn = (xv * _rstd(xv) * gv).astype(BF)
        d_hn = jnp.dot(d_z, w_ref[...], preferred_element_type=F32)
        acc[...] = acc[...] + _mm_tn(d_z, hn)
        d1, dg = _rms_bwd(d_hn, xv, gv)
        dg_ref[...] = dg_ref[...] + dg
        dx_ref[...] = dh1_ref[...] + d1

        @pl.when(s == nt - 1)
        def _():
            cp = pltpu.make_async_copy(acc, dw_ref, sem)
            cp.start()
            cp.wait()

    row = lambda w: pl.BlockSpec((tt, w), lambda s: (rev(s), 0))
    return pl.pallas_call(
        body, name="in_bwd", grid=(nt,),
        in_specs=[pl.BlockSpec((NH // 2, tt, 2 * LANE), lambda s: (0, rev(s), 0)), row(NH * LANE), row(DA), row(DP),
                  pl.BlockSpec((HALO, DP), lambda s: (jnp.minimum((rev(s) + 1) * hb, nt * hb - 1), 0)),
                  row(LANE), row(D), row(D), _full((1, D)), _resident((ZW, D)), _full((tt, tt)),
                  _full((NH * LANE, LANE)), _full((NH * LANE, LANE)), _full((len(WINS), tt, tt + HALO)), ANY],
        out_specs=[row(D), pl.BlockSpec(memory_space=pl.ANY), _full((1, D)), _full((1, LANE))],
        out_shape=[jax.ShapeDtypeStruct((t, D), F32), jax.ShapeDtypeStruct((ZW, D), F32),
                   jax.ShapeDtypeStruct((1, D), F32), jax.ShapeDtypeStruct((1, LANE), F32)],
        scratch_shapes=[pltpu.VMEM((1, LANE), F32), pltpu.VMEM((ZW, D), F32), pltpu.SemaphoreType.DMA],
        compiler_params=_cp(),
    )(dqa, dka, dv, dyc, dyc, fl, x, dh1, g1, w_in_t, tri_u, selq, selk, band_t, dep)


def _dot2t(x, sel):
    hi, lo = _split2(x)
    return jnp.dot(hi, sel, preferred_element_type=F32) + jnp.dot(lo, sel, preferred_element_type=F32)


class _NoComm:
    def __init__(self, w2):
        self.w2 = w2
        self.dep = jnp.zeros((8, LANE), F32)

    def after_attention(self, after):
        return self.dep

    def weights2(self, after):
        return _stack2_full(*self.w2)

    def after_ffn(self, grads2):
        self.grads2 = grads2
        return self.dep

    def after_mix(self, after, early):
        self.early = early
        return self.dep

    def after_attn(self, after):
        return self.dep


def _local_step(x, p, tgt, sm, w1, comm):
    w_in_t, w_out = w1
    tt = _tile(x.shape[0])
    eq, ek, rowq, rowk, selq, selk = _aug_consts()
    b_pad = jnp.pad(sm["b_forget"], ((0, 0), (0, LANE - NH)))
    qa, ka, v, u, fl = _in_proj(x, sm["g_mix_pre"], w_in_t, b_pad, _tri(tt, False), eq, ek, rowq, rowk, comm.dep)
    a, lse = _attn_fwd(qa, ka, v)
    yb, m, o, h1 = _mix_out(a, u, x, _band(tt, False), sm["w_pool"], sm["pool_scale"], sm["g_attn_grp"],
                            sm["g_pool_grp"], w_out, sm["g_mix_post"], comm.after_attention(a))
    stacks2 = comm.weights2(h1)
    wg_t, wu_t, wd, w_ple, w_pg = _unstack2_full(stacks2)
    hn2, gate, up, dff, dh2, loss, dwpg, dwple, dgple, dgfpost = _ffn_fwd(
        h1, sm["g_ffn_pre"], stacks2, sm["g_ffn_post"], p, w_ple, sm["g_ple"], w_pg, tgt)
    chunks = lambda a: a.reshape(DFF // FF_CH, FF_CH, D)
    dwg_t, dwu_t, dwd, d_hn2 = _ffn_bwd(hn2, gate, up, dff, jnp.concatenate([chunks(wg_t), chunks(wu_t)], axis=1), wd)
    dep = comm.after_ffn((dwg_t, dwu_t, dwd, dwple, dwpg))
    dh1, d_a, dyc, dgfpre, dgpost, dgattn, dgpool, dps, dwpool, dwout = _mix_bwd(
        d_hn2, dh2, h1, o, a, m, yb, sm["g_ffn_pre"], sm["g_mix_post"], sm["g_attn_grp"], sm["g_pool_grp"],
        w_out, sm["w_pool"], sm["pool_scale"], dep)
    early = dict(loss=loss[0:1, 0:1], g_attn_grp=dgattn, g_pool_grp=dgpool, w_pool=dwpool, pool_scale=dps,
                 g_mix_post=dgpost, g_ffn_pre=dgfpre, g_ffn_post=dgfpost, g_ple=dgple)
    dqa, dka, dvv = _attn_bwd(qa, ka, v, a, d_a, lse, comm.after_mix(dh1, early))
    dx, dwin_t, dg1, dbf = _in_bwd(dqa, dka, dvv, dyc, fl, x, dh1, sm["g_mix_pre"], w_in_t, _tri(tt, True),
                                   selq, selk, _band(tt, True), comm.after_attn(dvv))
    return dx, (dwin_t, dwout), dict(g_mix_pre=dg1, b_forget=dbf[:, :NH])


def _place():
    x, y, c = lax.axis_index("x"), lax.axis_index("y"), lax.axis_index("c")
    return x, y, c, [(1 - x, y), (x, 1 - y), (1 - x, 1 - y)]


def _rows(c, h):
    return pl.ds(pl.multiple_of(c * h, 16), h)


def _plan_gather(h):
    def plan(src, land):
        x, y, c, chips = _place()
        return [(src.at[_rows(c, h), :], land.at[2 * x + y, _rows(c, h), :], (cx, cy, c),
                 land.at[2 * cx + cy, _rows(c, h), :]) for cx, cy in chips]
    return plan


def _plan_forward(h):
    def plan(land_in, own, land):
        x, y, c, chips = _place()
        sib, me = (x, y, 1 - c), 2 * x + y
        return ([(land_in.at[2 * cx + cy, _rows(c, h), :], land.at[2 * cx + cy, _rows(c, h), :], sib,
                  land.at[2 * cx + cy, _rows(1 - c, h), :]) for cx, cy in chips]
                + [(own, land.at[me], sib, land.at[me])])
    return plan


def _plan_swap_halves(h):
    def plan(buf_in, buf):
        x, y, c, _ = _place()
        return [(buf_in.at[_rows(c, h), :], buf.at[_rows(c, h), :], (x, y, 1 - c), buf.at[_rows(1 - c, h), :])]
    return plan


def _plan_pair_rows(h):
    def plan(src, land):
        x, y, c, _ = _place()
        return [(src.at[:, _rows(1 - c, h), :], land, (x, y, 1 - c), land)]
    return plan


def _plan_scatter(src, land):
    x, y, c, chips = _place()
    return [(src.at[2 * cx + cy], land.at[k], (cx, cy, c), land.at[k]) for k, (cx, cy) in enumerate(chips)]


def _plan_all(src, land):
    x, y, c, _ = _place()
    copies = []
    for r in range(1, 8):
        px, py, pc = (1 - a if b else a for a, b in zip((x, y, c), (r >> 2 & 1, r >> 1 & 1, r & 1)))
        copies.append((src, land.at[4 * x + 2 * y + c], (px, py, pc), land.at[4 * px + 2 * py + pc]))
    return copies


def _remote(src, dst, send_sems, recv_sems, k, peer):
    return pltpu.make_async_remote_copy(src_ref=src, dst_ref=dst, send_sem=send_sems.at[k], recv_sem=recv_sems.at[k],
                                        device_id=peer, device_id_type=MESH)


def _exchange(name, n, src, land, plan):
    def body(src_ref, land_ref, send_sems, recv_sems):
        copies = plan(src_ref, land_ref)
        for k, (s, d, peer, _) in enumerate(copies):
            _remote(s, d, send_sems, recv_sems, k, peer).start()
        for k, (s, _, peer, mine) in enumerate(copies):
            _remote(s, mine, send_sems, recv_sems, k, peer).wait_recv()
        for k, (s, d, peer, _) in enumerate(copies):
            _remote(s, d, send_sems, recv_sems, k, peer).wait_send()

    return pl.pallas_call(
        body, name=name, in_specs=[ANY], out_specs=ANY, out_shape=land,
        scratch_shapes=[pltpu.SemaphoreType.DMA((n,)), pltpu.SemaphoreType.DMA((n,))],
    )(src)


def _exchange_inplace(name, n, buf, extra, plan):
    def body(*refs):
        ins, buf_ref, send_sems, recv_sems = refs[:1 + len(extra)], refs[1 + len(extra)], refs[-2], refs[-1]
        copies = plan(*ins, buf_ref)
        for k, (s, d, peer, _) in enumerate(copies):
            _remote(s, d, send_sems, recv_sems, k, peer).start()
        for k, (s, _, peer, mine) in enumerate(copies):
            _remote(s, mine, send_sems, recv_sems, k, peer).wait_recv()
        for k, (s, d, peer, _) in enumerate(copies):
            _remote(s, d, send_sems, recv_sems, k, peer).wait_send()

    return pl.pallas_call(
        body, name=name, in_specs=[ANY] * (1 + len(extra)), out_specs=ANY, out_shape=_sds(buf.shape, buf.dtype),
        input_output_aliases={0: 0},
        scratch_shapes=[pltpu.SemaphoreType.DMA((n,)), pltpu.SemaphoreType.DMA((n,))],
    )(buf, *extra)


HBM = pl.BlockSpec(memory_space=pltpu.HBM)
SEM = pl.BlockSpec(memory_space=pltpu.SEMAPHORE)
EFFECT = pltpu.SideEffectType.DATAFLOW_SIDE_EFFECTING


def _exchange_start(name, n, src, land, plan):
    def body(src_ref, land_ref, send_sems, recv_sems, src_thru, land_thru, token):
        for k, (s, d, peer, _) in enumerate(plan(src_ref, land_ref)):
            _remote(s, d, send_sems, recv_sems, k, peer).start()
        token[...] = jnp.zeros_like(token)

    return pl.pallas_call(
        body, name=name,
        out_shape=(pltpu.SemaphoreType.DMA((n,)), pltpu.SemaphoreType.DMA((n,)), pltpu.HBM(src.shape, src.dtype),
                   pltpu.HBM(land.shape, land.dtype), jax.ShapeDtypeStruct((8, LANE), F32)),
        in_specs=(HBM, HBM), out_specs=(SEM, SEM, HBM, HBM, pl.BlockSpec(memory_space=pltpu.VMEM)),
        input_output_aliases={0: 2, 1: 3},
        compiler_params=pltpu.CompilerParams(has_side_effects=EFFECT),
    )(pltpu.with_memory_space_constraint(src, pltpu.HBM), pltpu.with_memory_space_constraint(land, pltpu.HBM))


def _exchange_wait(name, started, plan, after):
    send_sems, recv_sems, src, land, _ = started

    def body(src_ref, land_ref, send_sems, recv_sems, after_ref, src_out, land_out):
        for k, (s, _, peer, mine) in enumerate(plan(src_ref, land_ref)):
            cp = _remote(s, mine, send_sems, recv_sems, k, peer)
            cp.wait_send()
            cp.wait_recv()

    return pl.pallas_call(
        body, name=name, out_shape=(pltpu.HBM(src.shape, src.dtype), pltpu.HBM(land.shape, land.dtype)),
        in_specs=(HBM, HBM, SEM, SEM, ANY), out_specs=(HBM, HBM), input_output_aliases={0: 0, 1: 1},
        compiler_params=pltpu.CompilerParams(has_side_effects=EFFECT),
    )(src, land, send_sems, recv_sems, after)


def _pair_sum(name, cidx, g, recv, br):
    h = recv.shape[1]
    nb = h // br

    def body(c_ref, g_ref, r_ref, out_ref):
        out_ref[...] = (g_ref[...] + r_ref[...]).astype(BF)

    return pl.pallas_call(
        body, name=name,
        grid_spec=pltpu.PrefetchScalarGridSpec(
            num_scalar_prefetch=1, grid=(NSHARD, nb),
            in_specs=[pl.BlockSpec((1, br, D), lambda s, i, c: (s, c[0] * nb + i, 0)),
                      pl.BlockSpec((1, br, D), lambda s, i, c: (s, i, 0))],
            out_specs=pl.BlockSpec((1, br, D), lambda s, i, c: (s, i, 0))),
        out_shape=jax.ShapeDtypeStruct((NSHARD, h, D), BF),
    )(cidx, g, recv)


def _chip_sum(name, place, pb, y, br):
    h = y.shape[1]
    nb = h // br

    def body(pl_ref, p_ref, y_ref, out_ref):
        acc = p_ref[0].astype(F32)
        for k in range(NSHARD - 1):
            acc = acc + y_ref[k].astype(F32)
        out_ref[...] = acc

    return pl.pallas_call(
        body, name=name,
        grid_spec=pltpu.PrefetchScalarGridSpec(
            num_scalar_prefetch=1, grid=(nb,),
            in_specs=[pl.BlockSpec((1, br, D), lambda i, s: (s[0], i, 0)),
                      pl.BlockSpec((NSHARD - 1, br, D), lambda i, s: (0, i, 0))],
            out_specs=pl.BlockSpec((br, D), lambda i, s: (s[1] * nb + i, 0))),
        out_shape=jax.ShapeDtypeStruct((2 * h, D), F32),
    )(place, pb, y)


def _sum_slots(v):
    def body(in_ref, out_ref):
        acc = in_ref[0]
        for k in range(1, 8):
            acc = acc + in_ref[k]
        out_ref[...] = acc

    vm = pl.BlockSpec(memory_space=pltpu.VMEM)
    return pl.pallas_call(body, name="sum_slots", in_specs=[vm], out_specs=vm,
                          out_shape=jax.ShapeDtypeStruct(v.shape[1:], F32))(v)


def _all_reduce_small(v):
    rows = v.shape[0]

    def body(in_ref, out_ref, buf, send_sems, recv_sems):
        x, y, c, _ = _place()
        me = 4 * x + 2 * y + c
        flips = [(r >> 2 & 1, r >> 1 & 1, r & 1) for r in range(1, 8)]

        def peer(f):
            return tuple(1 - a if b else a for a, b in zip((x, y, c), f))

        def copy(k, slot, to):
            return pltpu.make_async_remote_copy(src_ref=in_ref, dst_ref=buf.at[slot], send_sem=send_sems.at[k],
                                                recv_sem=recv_sems.at[k], device_id=to, device_id_type=MESH)

        sends = [copy(k, me, peer(f)) for k, f in enumerate(flips)]
        for cp in sends:
            cp.start()
        buf[me] = in_ref[...]
        for k, f in enumerate(flips):
            px, py, pc = peer(f)
            copy(k, 4 * px + 2 * py + pc, peer(f)).wait_recv()
        for cp in sends:
            cp.wait_send()
        acc = buf[0]
        for k in range(1, 8):
            acc = acc + buf[k]
        out_ref[...] = acc

    vm = pl.BlockSpec(memory_space=pltpu.VMEM)
    return pl.pallas_call(
        body, name="all_reduce_small", in_specs=[vm], out_specs=vm,
        out_shape=jax.ShapeDtypeStruct(v.shape, F32),
        scratch_shapes=[pltpu.VMEM((8, rows, LANE), F32), pltpu.SemaphoreType.DMA((7,)), pltpu.SemaphoreType.DMA((7,))],
    )(v)


def _adamw_math(w, g, m, v):
    m = ADAM_B1 * m + (1.0 - ADAM_B1) * g
    v = ADAM_B2 * v + (1.0 - ADAM_B2) * (g * g)
    m_hat = m / (1.0 - ADAM_B1 ** ADAM_STEP)
    v_hat = v / (1.0 - ADAM_B2 ** ADAM_STEP)
    delta = -ADAM_LR * (m_hat / (jnp.sqrt(v_hat) + ADAM_EPS) + ADAM_WD * w)
    return delta, m, v


def _adamw(w, g, m, v, dep):
    r, c = w.shape
    br = next(b for b in (256, 176, r) if r % b == 0)

    def body(w_ref, g_ref, m_ref, v_ref, dep_ref, d_ref, nm_ref, nv_ref):
        d_ref[...], nm_ref[...], nv_ref[...] = _adamw_math(w_ref[...], g_ref[...], m_ref[...], v_ref[...])

    spec = pl.BlockSpec((br, c), lambda i: (i, 0))
    return pl.pallas_call(
        body, name="adamw", grid=(r // br,), in_specs=[spec] * 4 + [ANY], out_specs=[spec] * 3,
        out_shape=[jax.ShapeDtypeStruct((r, c), F32)] * 3, compiler_params=_cp(),
    )(w, g, m, v, dep)


def _adamw_small(ws, gs, ms, vs):
    n = len(ws)

    def body(*refs):
        ins, outs = refs[:4 * n], refs[4 * n:]
        for k in range(n):
            d, m, v = _adamw_math(ins[k][...], ins[n + k][...], ins[2 * n + k][...], ins[3 * n + k][...])
            outs[k][...] = d
            outs[n + k][...] = m
            outs[2 * n + k][...] = v

    vm = pl.BlockSpec(memory_space=pltpu.VMEM)
    out = pl.pallas_call(
        body, name="adamw_small", in_specs=[vm] * (4 * n), out_specs=[vm] * (3 * n),
        out_shape=[jax.ShapeDtypeStruct(w.shape, F32) for w in ws] * 3,
    )(*ws, *gs, *ms, *vs)
    return out[:n], out[n:2 * n], out[2 * n:]


BIG = ("w_in", "w_out", "w_ffn_gate", "w_ffn_up", "w_ffn_down", "w_ple_proj", "w_ple_gate")
SMALL = ("g_mix_pre", "b_forget", "g_attn_grp", "g_pool_grp", "w_pool", "pool_scale", "g_mix_post", "g_ffn_pre",
         "g_ffn_post", "g_ple")
TRANSPOSED = ("w_in", "w_ffn_gate", "w_ffn_up")
VECTORS = tuple(n for n in SMALL if n != "w_pool")
ORDER = ("g_mix_pre", "w_in", "b_forget", "g_attn_grp", "g_pool_grp", "w_pool", "pool_scale", "w_out", "g_mix_post",
         "g_ffn_pre", "w_ffn_gate", "w_ffn_up", "w_ffn_down", "g_ffn_post", "w_ple_proj", "g_ple", "w_ple_gate")


def _pad_rows(a, rows):
    return jnp.pad(a, ((0, rows - a.shape[0]), (0, 0)))


def _stack1(w_in, w_out):
    return _pad_rows(jnp.concatenate([_pad_rows(w_in.T, IN_PAD), w_out], axis=0), ROWS1)


def _stack2(wg, wu, wd, wple, wpg):
    return _pad_rows(jnp.concatenate([wg.T, wu.T, wd, wple.reshape(DPLE // NSHARD, D), wpg], axis=0), ROWS2)


def _unstack1(s):
    return s[:IN_SH], s[O1_OUT:USED1]


def _unstack2(s):
    return s[:O2_U], s[O2_U:O2_D], s[O2_D:O2_PLE], s[O2_PLE:O2_PG].reshape(DPLE, DPLE), s[O2_PG:USED2]


def _cat(g, lo, hi):
    return g[:, lo:hi].reshape(NSHARD * (hi - lo), D)


def _unstack1_full(g):
    w_in_t = _cat(g, 0, IN_SH)
    w_in_t = jnp.concatenate([w_in_t[:3 * DA], _pad_rows(w_in_t[3 * DA:3 * DA + NH], LANE), w_in_t[3 * DA + NH:]], axis=0)
    return w_in_t, _cat(g, O1_OUT, USED1)


def _unstack2_full(g):
    w_ple = g[:, O2_PLE:O2_PG].reshape(NSHARD, DPLE, DPLE).transpose(1, 0, 2).reshape(DPLE, D)
    return _cat(g, 0, O2_U), _cat(g, O2_U, O2_D), _cat(g, O2_D, O2_PLE), w_ple, _cat(g, O2_PG, USED2)


def _shards(a):
    return a.reshape(NSHARD, a.shape[0] // NSHARD, D)


def _stack1_full(dwin_t, dwout):
    dwin_t = jnp.concatenate([dwin_t[:3 * DA + NH], dwin_t[3 * DA + LANE:]], axis=0).reshape(NSHARD, IN_SH, D)
    zeros = lambda r: jnp.zeros((NSHARD, r, D), F32)
    return jnp.concatenate([dwin_t, zeros(IN_PAD - IN_SH), _shards(dwout), zeros(ROWS1 - USED1)], axis=1)


def _stack2_full(dwg_t, dwu_t, dwd, dwple, dwpg):
    dwple = dwple.reshape(DPLE, NSHARD, DPLE).transpose(1, 0, 2).reshape(NSHARD, DPLE // NSHARD, D)
    return jnp.concatenate([_shards(dwg_t), _shards(dwu_t), _shards(dwd), dwple, _shards(dwpg),
                            jnp.zeros((NSHARD, ROWS2 - USED2, D), dwd.dtype)], axis=1)


def _sds(shape, dtype):
    return jax.ShapeDtypeStruct(shape, dtype)


class _Comm:
    def __init__(self, stack2, me, c):
        self.stack2, self.me, self.c = stack2, me, c
        self.cidx = c.astype(jnp.int32).reshape(1)
        self.place = jnp.stack([me, c]).astype(jnp.int32)
        self.h = ROWS2 // 2
        self.gather = _exchange_start("gather2_start", 3, stack2, lax.empty((NSHARD, ROWS2, D), BF), _plan_gather(self.h))
        self.dep = self.gather[4]

    def after_attention(self, after):
        own, land = _exchange_wait("gather2_wait", self.gather, _plan_gather(self.h), after)
        fwd = _plan_forward(self.h)
        self.forward = lambda own_ref, land_ref: fwd(land_ref, own_ref, land_ref)
        self.passing = _exchange_start("forward2_start", 4, own, land, self.forward)
        return self.passing[4]

    def weights2(self, after):
        return _exchange_wait("forward2_wait", self.passing, self.forward, after)[1]

    def after_ffn(self, grads2):
        g = _stack2_full(*grads2)
        self.pair = _exchange_start("reduce2_pair_start", 1, g, lax.empty((NSHARD, self.h, D), F32),
                                    _plan_pair_rows(self.h))
        return self.pair[4]

    def after_mix(self, after, early):
        g, recv = _exchange_wait("reduce2_pair_wait", self.pair, _plan_pair_rows(self.h), after)
        pb = _pair_sum("pair_sum2", self.cidx, g, recv, RED2)
        self.chip = _exchange_start("reduce2_chip_start", 3, pb, lax.empty((NSHARD - 1, self.h, D), BF), _plan_scatter)
        self.early_shapes = {n: early[n].shape for n in early}
        v = _pack_small(early)
        self.small = _exchange_start("small_start", 7, v, lax.empty((8,) + v.shape, F32), _plan_all)
        return self.chip[4] + self.small[4]

    def after_attn(self, after):
        pb, y = _exchange_wait("reduce2_chip_wait", self.chip, _plan_scatter, after)
        f = _chip_sum("chip_sum2", self.place, pb, y, RED2)
        self.reduced2 = _exchange_inplace("reduce2_gather", 1, f, (), _plan_swap_halves(self.h))
        v, land = _exchange_wait("small_wait", self.small, _plan_all, after)
        land = lax.dynamic_update_slice(land, v[None], (2 * self.me + self.c, 0, 0))
        self.early = _unpack_small(_sum_slots(land), self.early_shapes)
        return self.reduced2


def _pack_small(small):
    parts = []
    for name in small:
        flat = small[name].reshape(-1)
        parts.append(jnp.pad(flat, (0, -flat.shape[0] % LANE)).reshape(-1, LANE))
    v = jnp.concatenate(parts, axis=0)
    return _pad_rows(v, v.shape[0] + (-v.shape[0] % 8))


def _unpack_small(v, shapes):
    out, r = {}, 0
    for name in shapes:
        n = math.prod(shapes[name])
        rows = -(-n // LANE)
        out[name] = v[r:r + rows].reshape(-1)[:n].reshape(shapes[name])
        r += rows
    return out


def kernel(x, p, g_mix_pre, w_in, b_forget, g_attn_grp, g_pool_grp, w_pool, pool_scale, w_out, g_mix_post, g_ffn_pre, w_ffn_gate, w_ffn_up, w_ffn_down, g_ffn_post, w_ple_proj, g_ple, w_ple_gate, loss_target, m_g_mix_pre, m_w_in, m_b_forget, m_g_attn_grp, m_g_pool_grp, m_w_pool, m_pool_scale, m_w_out, m_g_mix_post, m_g_ffn_pre, m_w_ffn_gate, m_w_ffn_up, m_w_ffn_down, m_g_ffn_post, m_w_ple_proj, m_g_ple, m_w_ple_gate, v_g_mix_pre, v_w_in, v_b_forget, v_g_attn_grp, v_g_pool_grp, v_w_pool, v_pool_scale, v_w_out, v_g_mix_post, v_g_ffn_pre, v_w_ffn_gate, v_w_ffn_up, v_w_ffn_down, v_g_ffn_post, v_w_ple_proj, v_g_ple, v_w_ple_gate):
    args = dict(locals())
    strip = lambda n, a: a if n in VECTORS else a[0]
    w = {n: strip(n, args[n]) for n in ORDER}
    mom = {n: strip(n, args["m_" + n]) for n in ORDER}
    var = {n: strip(n, args["v_" + n]) for n in ORDER}
    sm = {n: w[n] for n in SMALL}

    c = lax.axis_index("c")
    me = 2 * lax.axis_index("x") + lax.axis_index("y")
    h1 = ROWS1 // 2
    bf = lambda n: w[n].astype(BF)
    stack1 = _stack1(bf("w_in"), bf("w_out"))
    stack2 = _stack2(*[bf(n) for n in BIG[2:]])
    land = _exchange("gather1", 3, stack1, _sds((NSHARD, ROWS1, D), BF), _plan_gather(h1))
    land, stack2 = lax.optimization_barrier((land, stack2))
    comm = _Comm(stack2, me, c)
    w1 = _unstack1_full(_exchange_inplace("gather1_forward", 4, land, (stack1,), _plan_forward(h1)))
    dx, grads1, late = _local_step(x[0], p[0, 0], loss_target[0], sm, w1, comm)

    late_shapes = {n: late[n].shape for n in late}
    red_small = {**comm.early, **_unpack_small(_all_reduce_small(_pack_small(late)), late_shapes)}
    loss = 0.5 / D * red_small["loss"][0, 0]

    flip = lambda n, a: a.T if n in TRANSPOSED else a
    grads, delta, new_m, new_v = {}, {}, {}, {}

    def update(names, shards, dep):
        for n, g in zip(names, shards):
            d_, m_, dep = _adamw(flip(n, w[n]), g, flip(n, mom[n]), flip(n, var[n]), dep)
            grads[n], delta[n], new_m[n], new_v[n] = flip(n, g), flip(n, d_), flip(n, m_), flip(n, dep)
        return dep

    shards2 = _unstack2(comm.reduced2)
    g1 = _stack1_full(*grads1)
    pair1 = _exchange_start("reduce1_pair_start", 1, g1, lax.empty((NSHARD, h1, D), F32), _plan_pair_rows(h1))
    dep = update(BIG[2:4], shards2[:2], pair1[4])
    g1, recv = _exchange_wait("reduce1_pair_wait", pair1, _plan_pair_rows(h1), dep)
    pb = _pair_sum("pair_sum1", comm.cidx, g1, recv, RED1)
    chip1 = _exchange_start("reduce1_chip_start", 3, pb, lax.empty((NSHARD - 1, h1, D), BF), _plan_scatter)
    dep = update(BIG[4:], shards2[2:], chip1[4])
    pb, y = _exchange_wait("reduce1_chip_wait", chip1, _plan_scatter, dep)
    f = _chip_sum("chip_sum1", comm.place, pb, y, RED1)
    reduced1 = _exchange_inplace("reduce1_gather", 1, f, (), _plan_swap_halves(h1))
    update(BIG[:2], _unstack1(reduced1), reduced1)
    for n in SMALL:
        grads[n] = red_small[n].reshape(w[n].shape)
    two_d = lambda a: a.reshape(-1, a.shape[-1])
    ds, ms, vs = _adamw_small([two_d(w[n]) for n in SMALL], [two_d(grads[n]) for n in SMALL],
                              [two_d(mom[n]) for n in SMALL], [two_d(var[n]) for n in SMALL])
    for k, n in enumerate(SMALL):
        delta[n], new_m[n], new_v[n] = ds[k].reshape(w[n].shape), ms[k].reshape(w[n].shape), vs[k].reshape(w[n].shape)

    lead = lambda d: [d[n] if n in VECTORS else d[n][None] for n in ORDER]
    return (loss, dx[None], *lead(grads), *lead(delta), *lead(new_m), *lead(new_v))
```

```python
import functools
import math

import jax
import jax.numpy as jnp
import numpy as np
from jax import lax
from jax.experimental import pallas as pl
from jax.experimental.pallas import tpu as pltpu

F32 = jnp.float32
BF = jnp.bfloat16
MESH = pl.DeviceIdType.MESH

D = 1024
DA = 512
DP = 512
NH = 8
HD = 64
DFF = 2816
DPLE = 256
WINS = (2, 4, 8, 16)
PC = 128
ZW = 3 * DA + 128 + DP
EPS = 1e-6
NSHARD = 4

LANE = 128
HALO = 128

IN_SH = 514
IN_PAD = 528
FF_SH = DFF // NSHARD
O1_OUT, USED1, ROWS1 = 528, 784, 800
O2_U, O2_D, O2_PLE, O2_PG, USED2, ROWS2 = 704, 1408, 2112, 2176, 2432, 2560
RED1, RED2 = 400, 640

ADAM_LR, ADAM_B1, ADAM_B2, ADAM_EPS, ADAM_WD, ADAM_STEP = 0.001, 0.9, 0.999, 1e-8, 0.01, 10

VMEM_LIMIT = 56 * 1024 * 1024


def _cp(**kw):
    return pltpu.CompilerParams(vmem_limit_bytes=VMEM_LIMIT, **kw)


def _mm(a, b):
    return jnp.dot(a.astype(BF), b.astype(BF), preferred_element_type=F32)


def _mm_nt(a, b):
    return lax.dot_general(a.astype(BF), b.astype(BF), (((1,), (1,)), ((), ())), preferred_element_type=F32)


def _mm_tn(a, b):
    return lax.dot_general(a.astype(BF), b.astype(BF), (((0,), (0,)), ((), ())), preferred_element_type=F32)


def _split2(x):
    hi = x.astype(BF)
    lo = (x - hi.astype(F32)).astype(BF)
    return hi, lo


def _split3(x):
    hi = x.astype(BF)
    r = x - hi.astype(F32)
    mid = r.astype(BF)
    lo = (r - mid.astype(F32)).astype(BF)
    return hi, mid, lo


def _dot3(m, x):
    hi, mid, lo = _split3(x)
    return (jnp.dot(m, hi, preferred_element_type=F32) + jnp.dot(m, mid, preferred_element_type=F32)
            + jnp.dot(m, lo, preferred_element_type=F32))


def _dot2(m, x):
    hi, lo = _split2(x)
    return jnp.dot(m, hi, preferred_element_type=F32) + jnp.dot(m, lo, preferred_element_type=F32)


def _rstd(x):
    return lax.rsqrt(jnp.mean(x * x, axis=-1, keepdims=True) + EPS)


def _rms_bwd(dy, x, g):
    r = _rstd(x)
    xh = x * r
    dg = jnp.sum(dy * xh, axis=0, keepdims=True)
    dxh = dy * g
    dx = r * (dxh - xh * jnp.mean(dxh * xh, axis=-1, keepdims=True))
    return dx, dg


def _sigmoid(x):
    return 1.0 / (1.0 + jnp.exp(-x))


ANY = pl.BlockSpec(memory_space=pl.ANY)


def _full(shape):
    n = len(shape)
    return pl.BlockSpec(shape, lambda *_: (0,) * n)


def _resident(shape):
    n = len(shape)
    return pl.BlockSpec(shape, lambda *_: (0,) * n, pipeline_mode=pl.Buffered(1))


def _tile(t):
    return 512 if t % 512 == 0 else t


def _tri(n, upper):
    r, c = np.indices((n, n))
    return ((c >= r) if upper else (c <= r)).astype(BF)


def _band(tt, transpose):
    r, c = np.indices((tt, tt + HALO))
    d = (c - r) if transpose else (r + HALO - c)
    return np.stack([((d >= 0) & (d < w)).astype(BF) for w in WINS])


def _aug_consts():
    row, col = np.indices((3 * LANE, NH * LANE))
    piece, head = row // LANE, row % LANE
    ch, cl = col // LANE, col % LANE
    eq = ((head == ch) & (cl == HD + piece)).astype(BF)
    ek = -((head == ch) & (cl == HD + 3 + piece)).astype(BF)
    lane = np.arange(NH * LANE)[None, :] % LANE
    rowq = ((lane >= HD + 3) & (lane < HD + 6)).astype(np.float32)
    rowk = ((lane >= HD) & (lane < HD + 3)).astype(np.float32)
    r2, c2 = np.indices((NH * LANE, LANE))
    selq = ((r2 // LANE == c2) & (r2 % LANE == HD)).astype(BF)
    selk = ((r2 // LANE == c2) & (r2 % LANE == HD + 3)).astype(BF)
    return eq, ek, rowq, rowk, selq, selk


def _in_proj(x, g1, w_in_t, b_pad, tri, eq, ek, rowq, rowk, dep):
    t = x.shape[0]
    tt = _tile(t)

    def body(x_ref, g_ref, w_ref, b_ref, tri_ref, eq_ref, ek_ref, rq_ref, rk_ref, dep_ref,
             qa_ref, ka_ref, v_ref, u_ref, fl_ref, carry):
        i = pl.program_id(0)

        @pl.when(i == 0)
        def _():
            carry[...] = jnp.zeros_like(carry)

        xv = x_ref[...]
        hn = (xv * _rstd(xv) * g_ref[...]).astype(BF)
        z = _mm_nt(hn, w_ref[...])
        fl = z[:, 3 * DA:3 * DA + LANE] + b_ref[...]
        lane = lax.broadcasted_iota(jnp.int32, fl.shape, 1)
        lf = jnp.where(lane < NH, jnp.minimum(fl, 0.0) - jnp.log(1.0 + jnp.exp(-jnp.abs(fl))), 0.0)
        c = carry[...] + _dot3(tri_ref[...], lf)
        carry[...] = carry[...] + jnp.sum(lf, axis=0, keepdims=True)
        caug = jnp.concatenate(_split3(c), axis=1)
        aug_q = jnp.dot(caug, eq_ref[...], preferred_element_type=F32) + rq_ref[...]
        aug_k = jnp.dot(caug, ek_ref[...], preferred_element_type=F32) + rk_ref[...]
        low = lax.broadcasted_iota(jnp.int32, (tt, LANE), 1) < HD
        for p in range(NH // 2):
            qp = z[:, LANE * p:LANE * (p + 1)] * (1.0 / math.sqrt(HD))
            kp = z[:, DA + LANE * p:DA + LANE * (p + 1)]
            for h, (qh, kh) in enumerate(((qp, kp), (pltpu.roll(qp, HD, 1), pltpu.roll(kp, HD, 1)))):
                lo_, hi_ = LANE * (2 * p + h), LANE * (2 * p + h + 1)
                qa_ref[:, lo_:hi_] = jnp.where(low, qh, aug_q[:, lo_:hi_]).astype(BF)
                ka_ref[:, lo_:hi_] = jnp.where(low, kh, aug_k[:, lo_:hi_]).astype(BF)
        v_ref[...] = z[:, 2 * DA:3 * DA].astype(BF)
        u_ref[...] = z[:, 3 * DA + LANE:]
        fl_ref[...] = fl

    return pl.pallas_call(
        body, name="in_proj", grid=(t // tt,),
        in_specs=[pl.BlockSpec((tt, D), lambda i: (i, 0)), _full((1, D)), _resident((ZW, D)), _full((1, LANE)),
                  _full((tt, tt)), _full((3 * LANE, NH * LANE)), _full((3 * LANE, NH * LANE)),
                  _full((1, NH * LANE)), _full((1, NH * LANE)), ANY],
        out_specs=[pl.BlockSpec((tt, NH * LANE), lambda i: (i, 0)), pl.BlockSpec((tt, NH * LANE), lambda i: (i, 0)),
                   pl.BlockSpec((tt, DA), lambda i: (i, 0)), pl.BlockSpec((tt, DP), lambda i: (i, 0)),
                   pl.BlockSpec((tt, LANE), lambda i: (i, 0))],
        out_shape=[jax.ShapeDtypeStruct((t, NH * LANE), BF), jax.ShapeDtypeStruct((t, NH * LANE), BF),
                   jax.ShapeDtypeStruct((t, DA), BF), jax.ShapeDtypeStruct((t, DP), F32),
                   jax.ShapeDtypeStruct((t, LANE), F32)],
        scratch_shapes=[pltpu.VMEM((1, LANE), F32)],
        compiler_params=_cp(),
    )(x, g1, w_in_t, b_pad, tri, eq, ek, rowq, rowk, dep)


def _attn_fwd(qa, ka, v):
    t = qa.shape[0]
    ta = _tile(t)
    n = t // ta

    def body(q_ref, k_ref, v_ref, a_ref, lse_ref, m_ref, l_ref, acc_ref):
        i = pl.program_id(1)
        m_ref[...] = jnp.full_like(m_ref, -1e30)
        l_ref[...] = jnp.zeros_like(l_ref)
        acc_ref[...] = jnp.zeros_like(acc_ref)
        qs = [q_ref[:, LANE * h:LANE * (h + 1)] for h in range(2)]
        reps = ta // LANE

        def tile(j, masked):
            rows = pl.ds(pl.multiple_of(j * ta, ta), ta)
            v2 = v_ref[rows, :]
            s = [_mm_nt(qs[h], k_ref[rows, LANE * h:LANE * (h + 1)]) for h in range(2)]
            if masked:
                keep = (lax.broadcasted_iota(jnp.int32, (ta, ta), 1) <= lax.broadcasted_iota(jnp.int32, (ta, ta), 0))
                s = [jnp.where(keep, sh, -1e30) for sh in s]
            m_old = [m_ref[h] for h in range(2)]
            m_new = [jnp.maximum(m_old[h], jnp.max(s[h], axis=1, keepdims=True)) for h in range(2)]
            pe = [jnp.exp(s[h] - jnp.tile(m_new[h], (1, reps))) for h in range(2)]
            alpha = [jnp.exp(m_old[h] - m_new[h]) for h in range(2)]
            pv = [jnp.dot(pe[h].astype(BF), v2, preferred_element_type=F32) for h in range(2)]
            for h in range(2):
                l_ref[h] = alpha[h] * l_ref[h] + jnp.sum(pe[h], axis=1, keepdims=True)
                acc_ref[h] = alpha[h] * acc_ref[h] + pv[h]
                m_ref[h] = m_new[h]

        def step(j, carry):
            tile(j, False)
            return carry

        lax.fori_loop(0, i, step, 0)
        tile(i, True)
        low = lax.broadcasted_iota(jnp.int32, (ta, LANE), 1) < HD
        a_ref[...] = jnp.where(low, acc_ref[0] / l_ref[0], acc_ref[1] / l_ref[1])
        lse_ref[...] = jnp.where(low, m_ref[0] + jnp.log(l_ref[0]), m_ref[1] + jnp.log(l_ref[1]))

    return pl.pallas_call(
        body, name="attn_fwd", grid=(NH // 2, n),
        in_specs=[pl.BlockSpec((ta, 2 * LANE), lambda p, i: (i, p)),
                  pl.BlockSpec((t, 2 * LANE), lambda p, i: (0, p)),
                  pl.BlockSpec((t, LANE), lambda p, i: (0, p))],
        out_specs=[pl.BlockSpec((ta, LANE), lambda p, i: (i, p)), pl.BlockSpec((ta, LANE), lambda p, i: (i, p))],
        out_shape=[jax.ShapeDtypeStruct((t, DA), F32), jax.ShapeDtypeStruct((t, DA), F32)],
        scratch_shapes=[pltpu.VMEM((2, ta, LANE), F32), pltpu.VMEM((2, ta, LANE), F32), pltpu.VMEM((2, ta, LANE), F32)],
        compiler_params=_cp(),
    )(qa, ka, v)


def _mix_out(a, u, x, band, w_pool, pool_scale, g_attn, g_pool, w_out, g_post, dep):
    t = a.shape[0]
    tt = _tile(t)
    hb = tt // HALO

    def body(a_ref, u_ref, up_ref, x_ref, band_ref, wp_ref, ps_ref, ga_ref, gp_ref, wo_ref, go_ref, dep_ref,
             yb_ref, m_ref, o_ref, h1_ref):
        i = pl.program_id(0)
        prev = up_ref[...] * jnp.where(i > 0, 1.0, 0.0)
        tok = i * tt + lax.broadcasted_iota(jnp.int32, (tt, PC), 0)
        ms = []
        for g, w in enumerate(WINS):
            ug = u_ref[:, PC * g:PC * (g + 1)]
            ext = jnp.concatenate([prev[:, PC * g:PC * (g + 1)], ug], axis=0)
            cnt = jnp.minimum(tok + 1, w).astype(F32)
            y = (_dot2(band_ref[g], ext) / cnt - ug).astype(BF)
            yb_ref[:, PC * g:PC * (g + 1)] = y
            ms.append(_mm(y, wp_ref[g]) * ps_ref[:, PC * g:PC * (g + 1)])
        m = jnp.concatenate(ms, axis=1)
        m_ref[...] = m
        av = a_ref[...]
        mix = jnp.concatenate([av * _rstd(av) * ga_ref[...], m * _rstd(m) * gp_ref[...]], axis=1)
        o = _mm(mix, wo_ref[...])
        o_ref[...] = o
        h1_ref[...] = x_ref[...] + o * _rstd(o) * go_ref[...]

    return pl.pallas_call(
        body, name="mix_out", grid=(t // tt,),
        in_specs=[pl.BlockSpec((tt, DA), lambda i: (i, 0)), pl.BlockSpec((tt, DP), lambda i: (i, 0)),
                  pl.BlockSpec((HALO, DP), lambda i: (jnp.maximum(i * hb - 1, 0), 0)),
                  pl.BlockSpec((tt, D), lambda i: (i, 0)), _full((len(WINS), tt, tt + HALO)),
                  _full((len(WINS), PC, PC)), _full((1, DP)), _full((1, DA)), _full((1, DP)),
                  _resident((D, D)), _full((1, D)), ANY],
        out_specs=[pl.BlockSpec((tt, DP), lambda i: (i, 0)), pl.BlockSpec((tt, DP), lambda i: (i, 0)),
                   pl.BlockSpec((tt, D), lambda i: (i, 0)), pl.BlockSpec((tt, D), lambda i: (i, 0))],
        out_shape=[jax.ShapeDtypeStruct((t, DP), BF), jax.ShapeDtypeStruct((t, DP), F32),
                   jax.ShapeDtypeStruct((t, D), F32), jax.ShapeDtypeStruct((t, D), F32)],
        compiler_params=_cp(),
    )(a, u, u, x, band, w_pool, pool_scale, g_attn, g_pool, w_out, g_post, dep)


def _ffn_fwd(h1, g_pre, stacks2, g_post, p, w_ple, g_ple, w_pg, tgt):
    t = h1.shape[0]
    tt = 256 if t % 256 == 0 else t

    def body(h1_ref, gpre_ref, wg_ref, wu_ref, wd_ref, gpost_ref, p_ref, wple_ref, gple_ref, wpg_ref, tgt_ref,
             hn_ref, gate_ref, up_ref, dff_ref, dh2_ref, loss_ref, dwpg_ref, dwple_ref, dgple_ref, dgpost_ref):
        i = pl.program_id(0)

        @pl.when(i == 0)
        def _():
            loss_ref[...] = jnp.zeros_like(loss_ref)
            dwpg_ref[...] = jnp.zeros_like(dwpg_ref)
            dwple_ref[...] = jnp.zeros_like(dwple_ref)
            dgple_ref[...] = jnp.zeros_like(dgple_ref)
            dgpost_ref[...] = jnp.zeros_like(dgpost_ref)

        h1v = h1_ref[...]
        hn = (h1v * _rstd(h1v) * gpre_ref[...]).astype(BF)
        hn_ref[...] = hn
        gate = _mm_nt(hn, wg_ref[...].reshape(DFF, D))
        up = _mm_nt(hn, wu_ref[...].reshape(DFF, D))
        gate_ref[...] = gate.astype(BF)
        up_ref[...] = up.astype(BF)
        ff = _mm(gate * _sigmoid(gate) * up, wd_ref[...].reshape(DFF, D))
        rff = _rstd(ff)
        ffh = ff * rff
        gpost = gpost_ref[...]
        h2 = h1v + ffh * gpost
        pv = p_ref[...]
        pe = _mm(pv, wple_ref[...])
        rpe = _rstd(pe)
        peh = pe * rpe
        gple = gple_ref[...]
        e = peh * gple
        sig = _sigmoid(_mm(h2, wpg_ref[...]))
        dv = h2 + sig * e - tgt_ref[...]
        sq = jnp.sum(jnp.sum(dv * dv, axis=1, keepdims=True), axis=0, keepdims=True)
        loss_ref[...] = loss_ref[...] + sq
        dy = dv * (1.0 / D)
        d_e = dy * sig
        d_gl = dy * e * sig * (1.0 - sig)
        dh2 = dy + _mm_nt(d_gl, wpg_ref[...])
        dh2_ref[...] = dh2
        dwpg_ref[...] = dwpg_ref[...] + _mm_tn(h2, d_gl)
        dgple_ref[...] = dgple_ref[...] + jnp.sum(d_e * peh, axis=0, keepdims=True)
        dpeh = d_e * gple
        d_pe = rpe * (dpeh - peh * jnp.mean(dpeh * peh, axis=-1, keepdims=True))
        dwple_ref[...] = dwple_ref[...] + _mm_tn(pv, d_pe)
        dgpost_ref[...] = dgpost_ref[...] + jnp.sum(dh2 * ffh, axis=0, keepdims=True)
        dffh = dh2 * gpost
        dff_ref[...] = (rff * (dffh - ffh * jnp.mean(dffh * ffh, axis=-1, keepdims=True))).astype(BF)

    row = lambda w: pl.BlockSpec((tt, w), lambda i: (i, 0))
    shard_rows = lambda k: pl.BlockSpec((NSHARD, FF_SH, D), lambda i: (0, k, 0), pipeline_mode=pl.Buffered(1))
    return pl.pallas_call(
        body, name="ffn_fwd", grid=(t // tt,),
        in_specs=[row(D), _full((1, D)), shard_rows(0), shard_rows(1), shard_rows(2), _full((1, D)),
                  row(DPLE), _resident((DPLE, D)), _full((1, D)), _resident((D, D)), row(D)],
        out_specs=[row(D), row(DFF), row(DFF), row(D), row(D), _full((8, LANE)), _full((D, D)), _full((DPLE, D)),
                   _full((1, D)), _full((1, D))],
        out_shape=[jax.ShapeDtypeStruct((t, D), BF), jax.ShapeDtypeStruct((t, DFF), BF), jax.ShapeDtypeStruct((t, DFF), BF),
                   jax.ShapeDtypeStruct((t, D), BF), jax.ShapeDtypeStruct((t, D), F32), jax.ShapeDtypeStruct((8, LANE), F32),
                   jax.ShapeDtypeStruct((D, D), F32), jax.ShapeDtypeStruct((DPLE, D), F32),
                   jax.ShapeDtypeStruct((1, D), F32), jax.ShapeDtypeStruct((1, D), F32)],
        compiler_params=_cp(),
    )(h1, g_pre, stacks2, stacks2, stacks2, g_post, p, w_ple, g_ple, w_pg, tgt)


FF_CH = 256


def _ffn_bwd(hn2, gate, up, dff, wgu, wd):
    t = hn2.shape[0]
    tt = 1024 if t % 1024 == 0 else _tile(t)
    nt = t // tt
    ch = FF_CH
    nc = DFF // ch

    def body(hn_ref, gate_ref, up_ref, dff_ref, wgu_ref, wd_ref,
             dwg_ref, dwu_ref, dwd_ref, dhn_ref, acc, sem):
        j, i = pl.program_id(0), pl.program_id(1)

        @pl.when(j == 0)
        def _():
            acc[pl.ds(pl.multiple_of(i * tt, tt), tt), :] = jnp.zeros((tt, D), F32)

        @pl.when(i == 0)
        def _():
            dwg_ref[...] = jnp.zeros_like(dwg_ref)
            dwu_ref[...] = jnp.zeros_like(dwu_ref)
            dwd_ref[...] = jnp.zeros_like(dwd_ref)

        half = tt // 2
        acts, dgus = [], []
        for hh in range(2):
            r = slice(hh * half, (hh + 1) * half)
            gate_v = gate_ref[r, :].astype(F32)
            up_v = up_ref[r, :].astype(F32)
            sg = _sigmoid(gate_v)
            silu = gate_v * sg
            d_act = _mm_nt(dff_ref[r, :], wd_ref[...])
            d_up = (d_act * silu).astype(BF)
            d_gate = (d_act * up_v * (sg * (1.0 + gate_v * (1.0 - sg)))).astype(BF)
            dgu = jnp.concatenate([d_gate, d_up], axis=1)
            rows = pl.ds(pl.multiple_of(i * tt + hh * half, half), half)
            acc[rows, :] = acc[rows, :] + jnp.dot(dgu, wgu_ref[0], preferred_element_type=F32)
            acts.append((silu * up_v).astype(BF))
            dgus.append(dgu)
        dwd_ref[...] = dwd_ref[...] + _mm_tn(jnp.concatenate(acts, axis=0), dff_ref[...])
        dwgu = _mm_tn(jnp.concatenate(dgus, axis=0), hn_ref[...])
        dwg_ref[...] = dwg_ref[...] + dwgu[:ch]
        dwu_ref[...] = dwu_ref[...] + dwgu[ch:]

        @pl.when((j == nc - 1) & (i == nt - 1))
        def _():
            cp = pltpu.make_async_copy(acc, dhn_ref, sem)
            cp.start()
            cp.wait()

    tok = lambda w: pl.BlockSpec((tt, w), lambda j, i: (i, 0))
    chunk = pl.BlockSpec((ch, D), lambda j, i: (j, 0))
    pair = pl.BlockSpec((1, 2 * ch, D), lambda j, i: (j, 0, 0))
    return pl.pallas_call(
        body, name="ffn_bwd", grid=(nc, nt),
        in_specs=[tok(D), pl.BlockSpec((tt, ch), lambda j, i: (i, j)), pl.BlockSpec((tt, ch), lambda j, i: (i, j)),
                  tok(D), pair, chunk],
        out_specs=[chunk, chunk, chunk, pl.BlockSpec(memory_space=pl.ANY)],
        out_shape=[jax.ShapeDtypeStruct((DFF, D), F32), jax.ShapeDtypeStruct((DFF, D), F32),
                   jax.ShapeDtypeStruct((DFF, D), F32), jax.ShapeDtypeStruct((t, D), F32)],
        scratch_shapes=[pltpu.VMEM((t, D), F32), pltpu.SemaphoreType.DMA],
        compiler_params=_cp(),
    )(hn2, gate, up, dff, wgu, wd)


def _mix_bwd(d_hn2, dh2, h1, o, a, m, yb, g_ffn_pre, g_post, g_attn, g_pool, w_out, w_pool, pool_scale, dep):
    t = a.shape[0]
    tt = 256 if t % 256 == 0 else t

    def body(dhn_ref, dh2_ref, h1_ref, o_ref, a_ref, m_ref, yb_ref, gfp_ref, go_ref, ga_ref, gp_ref, wo_ref, wp_ref,
             ps_ref, dep_ref, dh1_ref, da_ref, dyc_ref, dgfp_ref, dgo_ref, dga_ref, dgp_ref, dps_ref, dwp_ref, dwo_ref):
        i = pl.program_id(0)

        @pl.when(i == 0)
        def _():
            for r in (dgfp_ref, dgo_ref, dga_ref, dgp_ref, dps_ref, dwp_ref, dwo_ref):
                r[...] = jnp.zeros_like(r)

        d1, dg = _rms_bwd(dhn_ref[...], h1_ref[...], gfp_ref[...])
        dgfp_ref[...] = dgfp_ref[...] + dg
        dh1 = dh2_ref[...] + d1
        dh1_ref[...] = dh1
        d_o, dg = _rms_bwd(dh1, o_ref[...], go_ref[...])
        dgo_ref[...] = dgo_ref[...] + dg
        d_mix = _mm_nt(d_o, wo_ref[...])
        av, mv = a_ref[...], m_ref[...]
        mix = jnp.concatenate([av * _rstd(av) * ga_ref[...], mv * _rstd(mv) * gp_ref[...]], axis=1)
        dwo_ref[...] = dwo_ref[...] + _mm_tn(mix, d_o)
        d_a, dg = _rms_bwd(d_mix[:, :DA], av, ga_ref[...])
        dga_ref[...] = dga_ref[...] + dg
        da_ref[...] = d_a
        d_m, dg = _rms_bwd(d_mix[:, DA:], mv, gp_ref[...])
        dgp_ref[...] = dgp_ref[...] + dg
        tok = i * tt + lax.broadcasted_iota(jnp.int32, (tt, PC), 0)
        dps = []
        for g, w in enumerate(WINS):
            sl = slice(PC * g, PC * (g + 1))
            ybg = yb_ref[:, sl]
            wpg = wp_ref[g].astype(BF)
            mlin = jnp.dot(ybg, wpg, preferred_element_type=F32)
            dmg = d_m[:, sl]
            dps.append(jnp.sum(dmg * mlin, axis=0, keepdims=True))
            dml = (dmg * ps_ref[:, sl]).astype(BF)
            dwp_ref[g] = dwp_ref[g] + _mm_tn(ybg, dml)
            dyc_ref[:, sl] = _mm_nt(dml, wpg) / jnp.minimum(tok + 1, w).astype(F32)
        dps_ref[...] = dps_ref[...] + jnp.concatenate(dps, axis=1)

    row = lambda w: pl.BlockSpec((tt, w), lambda i: (i, 0))
    return pl.pallas_call(
        body, name="mix_bwd", grid=(t // tt,),
        in_specs=[row(D), row(D), row(D), row(D), row(DA), row(DP), row(DP), _full((1, D)), _full((1, D)),
                  _full((1, DA)), _full((1, DP)), _resident((D, D)), _full((len(WINS), PC, PC)), _full((1, DP)), ANY],
        out_specs=[row(D), row(DA), row(DP), _full((1, D)), _full((1, D)), _full((1, DA)), _full((1, DP)),
                   _full((1, DP)), _full((len(WINS), PC, PC)), _full((D, D))],
        out_shape=[jax.ShapeDtypeStruct((t, D), F32), jax.ShapeDtypeStruct((t, DA), F32), jax.ShapeDtypeStruct((t, DP), F32),
                   jax.ShapeDtypeStruct((1, D), F32), jax.ShapeDtypeStruct((1, D), F32), jax.ShapeDtypeStruct((1, DA), F32),
                   jax.ShapeDtypeStruct((1, DP), F32), jax.ShapeDtypeStruct((1, DP), F32),
                   jax.ShapeDtypeStruct((len(WINS), PC, PC), F32), jax.ShapeDtypeStruct((D, D), F32)],
        compiler_params=_cp(),
    )(d_hn2, dh2, h1, o, a, m, yb, g_ffn_pre, g_post, g_attn, g_pool, w_out, w_pool, pool_scale, dep)


def _attn_bwd(qa, ka, v, a, d_a, lse, dep):
    t = qa.shape[0]
    ta = _tile(t)
    n = t // ta

    def body(q_ref, k_ref, v_ref, o_ref, do_ref, lse_ref, dep_ref, dq_ref, dk_ref, dv_ref):
        j = pl.program_id(1)

        @pl.when(j == 0)
        def _():
            dq_ref[...] = jnp.zeros_like(dq_ref)

        dk_ref[...] = jnp.zeros_like(dk_ref)
        dv_ref[...] = jnp.zeros_like(dv_ref)
        ks = [k_ref[:, LANE * h:LANE * (h + 1)] for h in range(2)]
        v2 = v_ref[...]
        lane = lax.broadcasted_iota(jnp.int32, (ta, LANE), 1)
        mine = [lane < HD, lane >= HD]

        def tile(i, masked):
            rows = pl.ds(pl.multiple_of(i * ta, ta), ta)
            do2 = do_ref[rows, :]
            prod = do2 * o_ref[rows, :]
            lse2 = lse_ref[rows, :]
            do2b = do2.astype(BF)
            qh = [q_ref[rows, LANE * h:LANE * (h + 1)] for h in range(2)]
            s = [_mm_nt(qh[h], ks[h]) for h in range(2)]
            dp = [_mm_nt(jnp.where(mine[h], do2, 0.0), v2) for h in range(2)]
            delta = [jnp.sum(jnp.where(mine[h], prod, 0.0), axis=1, keepdims=True) for h in range(2)]
            lse_h = [jnp.sum(jnp.where(lane == HD * h, lse2, 0.0), axis=1, keepdims=True) for h in range(2)]
            pr = [jnp.exp(s[h] - lse_h[h]) for h in range(2)]
            if masked:
                keep = (lax.broadcasted_iota(jnp.int32, (ta, ta), 1) <= lax.broadcasted_iota(jnp.int32, (ta, ta), 0))
                pr = [jnp.where(keep, ph, 0.0) for ph in pr]
            ds = [(pr[h] * (dp[h] - delta[h])).astype(BF) for h in range(2)]
            dv_ref[...] = dv_ref[...] + jnp.where(mine[0], _mm_tn(pr[0], do2b), _mm_tn(pr[1], do2b))
            for h in range(2):
                sl = slice(LANE * h, LANE * (h + 1))
                dk_ref[:, sl] = dk_ref[:, sl] + _mm_tn(ds[h], qh[h])
                dq_ref[0, rows, sl] = dq_ref[0, rows, sl] + jnp.dot(ds[h], ks[h], preferred_element_type=F32)

        def step(i, carry):
            tile(i, False)
            return carry

        tile(j, True)
        lax.fori_loop(j + 1, n, step, 0)

    qrow = lambda w: pl.BlockSpec((t, w), lambda p, j: (0, p))
    krow = lambda w: pl.BlockSpec((ta, w), lambda p, j: (j, p))
    return pl.pallas_call(
        body, name="attn_bwd", grid=(NH // 2, n),
        in_specs=[qrow(2 * LANE), krow(2 * LANE), krow(LANE), qrow(LANE), qrow(LANE), qrow(LANE), ANY],
        out_specs=[pl.BlockSpec((1, t, 2 * LANE), lambda p, j: (p, 0, 0)), krow(2 * LANE), krow(LANE)],
        out_shape=[jax.ShapeDtypeStruct((NH // 2, t, 2 * LANE), F32), jax.ShapeDtypeStruct((t, NH * LANE), F32),
                   jax.ShapeDtypeStruct((t, DA), F32)],
        compiler_params=_cp(),
    )(qa, ka, v, a, d_a, lse, dep)


def _in_bwd(dqa, dka, dv, dyc, fl, x, dh1, g1, w_in_t, tri_u, selq, selk, band_t, dep):
    t = x.shape[0]
    tt = _tile(t)
    nt = t // tt
    hb = tt // HALO
    rev = lambda s: nt - 1 - s

    def body(dqa_ref, dka_ref, dv_ref, dyc_ref, dyn_ref, fl_ref, x_ref, dh1_ref, g_ref, w_ref, tri_ref, sq_ref, sk_ref,
             band_ref, dep_ref, dx_ref, dw_ref, dg_ref, db_ref, carry, acc, sem):
        s = pl.program_id(0)
        i = nt - 1 - s

        @pl.when(s == 0)
        def _():
            carry[...] = jnp.zeros_like(carry)
            acc[...] = jnp.zeros_like(acc)
            dg_ref[...] = jnp.zeros_like(dg_ref)
            db_ref[...] = jnp.zeros_like(db_ref)

        dq_cat = jnp.concatenate([dqa_ref[p] for p in range(NH // 2)], axis=1)
        dk_cat = dka_ref[...]
        dc = _dot2t(dq_cat, sq_ref[...]) - _dot2t(dk_cat, sk_ref[...])
        dlf = carry[...] + _dot3(tri_ref[...], dc)
        carry[...] = carry[...] + jnp.sum(dc, axis=0, keepdims=True)
        flv = fl_ref[...]
        lane = lax.broadcasted_iota(jnp.int32, flv.shape, 1)
        d_fl = jnp.where(lane < NH, dlf / (1.0 + jnp.exp(flv)), 0.0)
        db_ref[...] = db_ref[...] + jnp.sum(d_fl, axis=0, keepdims=True)
        low = lax.broadcasted_iota(jnp.int32, (tt, LANE), 1) < HD
        dqs, dks = [], []
        for p in range(NH // 2):
            b0, b1 = slice(2 * LANE * p, 2 * LANE * p + LANE), slice(2 * LANE * p + LANE, 2 * LANE * (p + 1))
            dqs.append(jnp.where(low, dq_cat[:, b0], pltpu.roll(dq_cat[:, b1], HD, 1)) * (1.0 / math.sqrt(HD)))
            dks.append(jnp.where(low, dk_cat[:, b0], pltpu.roll(dk_cat[:, b1], HD, 1)))
        nxt = dyn_ref[...] * jnp.where(i < nt - 1, 1.0, 0.0)
        tok = i * tt + lax.broadcasted_iota(jnp.int32, (tt, PC), 0)
        dus = []
        for g, w in enumerate(WINS):
            sl = slice(PC * g, PC * (g + 1))
            dycg = dyc_ref[:, sl]
            ext = jnp.concatenate([dycg, nxt[:, sl]], axis=0)
            dus.append(_dot2(band_ref[g], ext) - dycg * jnp.minimum(tok + 1, w).astype(F32))
        d_z = jnp.concatenate(dqs + dks + [dv_ref[...], d_fl] + dus, axis=1).astype(BF)
        xv = x_ref[...]
        gv = g_ref[...]
        hn = (xv * _rstd(xv) * gv).astype(BF)
        d_hn = jnp.dot(d_z, w_ref[...], preferred_element_type=F32)
        acc[...] = acc[...] + _mm_tn(d_z, hn)
        d1, dg = _rms_bwd(d_hn, xv, gv)
        dg_ref[...] = dg_ref[...] + dg
        dx_ref[...] = dh1_ref[...] + d1

        @pl.when(s == nt - 1)
        def _():
            cp = pltpu.make_async_copy(acc, dw_ref, sem)
            cp.start()
            cp.wait()

    row = lambda w: pl.BlockSpec((tt, w), lambda s: (rev(s), 0))
    return pl.pallas_call(
        body, name="in_bwd", grid=(nt,),
        in_specs=[pl.BlockSpec((NH // 2, tt, 2 * LANE), lambda s: (0, rev(s), 0)), row(NH * LANE), row(DA), row(DP),
                  pl.BlockSpec((HALO, DP), lambda s: (jnp.minimum((rev(s) + 1) * hb, nt * hb - 1), 0)),
                  row(LANE), row(D), row(D), _full((1, D)), _resident((ZW, D)), _full((tt, tt)),
                  _full((NH * LANE, LANE)), _full((NH * LANE, LANE)), _full((len(WINS), tt, tt + HALO)), ANY],
        out_specs=[row(D), pl.BlockSpec(memory_space=pl.ANY), _full((1, D)), _full((1, LANE))],
        out_shape=[jax.ShapeDtypeStruct((t, D), F32), jax.ShapeDtypeStruct((ZW, D), F32),
                   jax.ShapeDtypeStruct((1, D), F32), jax.ShapeDtypeStruct((1, LANE), F32)],
        scratch_shapes=[pltpu.VMEM((1, LANE), F32), pltpu.VMEM((ZW, D), F32), pltpu.SemaphoreType.DMA],
        compiler_params=_cp(),
    )(dqa, dka, dv, dyc, dyc, fl, x, dh1, g1, w_in_t, tri_u, selq, selk, band_t, dep)


def _dot2t(x, sel):
    hi, lo = _split2(x)
    return jnp.dot(hi, sel, preferred_element_type=F32) + jnp.dot(lo, sel, preferred_element_type=F32)


class _NoComm:
    def __init__(self, w2):
        self.w2 = w2
        self.dep = jnp.zeros((8, LANE), F32)

    def after_attention(self, after):
        return self.dep

    def weights2(self, after):
        return _stack2_full(*self.w2)

    def after_ffn(self, grads2):
        self.grads2 = grads2
        return self.dep

    def after_mix(self, after, early):
        self.early = early
        return self.dep

    def after_attn(self, after):
        return self.dep


def _local_step(x, p, tgt, sm, w1, comm):
    w_in_t, w_out = w1
    tt = _tile(x.shape[0])
    eq, ek, rowq, rowk, selq, selk = _aug_consts()
    b_pad = jnp.pad(sm["b_forget"], ((0, 0), (0, LANE - NH)))
    qa, ka, v, u, fl = _in_proj(x, sm["g_mix_pre"], w_in_t, b_pad, _tri(tt, False), eq, ek, rowq, rowk, comm.dep)
    a, lse = _attn_fwd(qa, ka, v)
    yb, m, o, h1 = _mix_out(a, u, x, _band(tt, False), sm["w_pool"], sm["pool_scale"], sm["g_attn_grp"],
                            sm["g_pool_grp"], w_out, sm["g_mix_post"], comm.after_attention(a))
    stacks2 = comm.weights2(h1)
    wg_t, wu_t, wd, w_ple, w_pg = _unstack2_full(stacks2)
    hn2, gate, up, dff, dh2, loss, dwpg, dwple, dgple, dgfpost = _ffn_fwd(
        h1, sm["g_ffn_pre"], stacks2, sm["g_ffn_post"], p, w_ple, sm["g_ple"], w_pg, tgt)
    chunks = lambda a: a.reshape(DFF // FF_CH, FF_CH, D)
    dwg_t, dwu_t, dwd, d_hn2 = _ffn_bwd(hn2, gate, up, dff, jnp.concatenate([chunks(wg_t), chunks(wu_t)], axis=1), wd)
    dep = comm.after_ffn((dwg_t, dwu_t, dwd, dwple, dwpg))
    dh1, d_a, dyc, dgfpre, dgpost, dgattn, dgpool, dps, dwpool, dwout = _mix_bwd(
        d_hn2, dh2, h1, o, a, m, yb, sm["g_ffn_pre"], sm["g_mix_post"], sm["g_attn_grp"], sm["g_pool_grp"],
        w_out, sm["w_pool"], sm["pool_scale"], dep)
    early = dict(loss=loss[0:1, 0:1], g_attn_grp=dgattn, g_pool_grp=dgpool, w_pool=dwpool, pool_scale=dps,
                 g_mix_post=dgpost, g_ffn_pre=dgfpre, g_ffn_post=dgfpost, g_ple=dgple)
    dqa, dka, dvv = _attn_bwd(qa, ka, v, a, d_a, lse, comm.after_mix(dh1, early))
    dx, dwin_t, dg1, dbf = _in_bwd(dqa, dka, dvv, dyc, fl, x, dh1, sm["g_mix_pre"], w_in_t, _tri(tt, True),
                                   selq, selk, _band(tt, True), comm.after_attn(dvv))
    return dx, (dwin_t, dwout), dict(g_mix_pre=dg1, b_forget=dbf[:, :NH])


def _place():
    x, y, c = lax.axis_index("x"), lax.axis_index("y"), lax.axis_index("c")
    return x, y, c, [(1 - x, y), (x, 1 - y), (1 - x, 1 - y)]


def _rows(c, h):
    return pl.ds(pl.multiple_of(c * h, 16), h)


def _plan_gather(h):
    def plan(src, land):
        x, y, c, chips = _place()
        return [(src.at[_rows(c, h), :], land.at[2 * x + y, _rows(c, h), :], (cx, cy, c),
                 land.at[2 * cx + cy, _rows(c, h), :]) for cx, cy in chips]
    return plan


def _plan_forward(h):
    def plan(land_in, own, land):
        x, y, c, chips = _place()
        sib, me = (x, y, 1 - c), 2 * x + y
        return ([(land_in.at[2 * cx + cy, _rows(c, h), :], land.at[2 * cx + cy, _rows(c, h), :], sib,
                  land.at[2 * cx + cy, _rows(1 - c, h), :]) for cx, cy in chips]
                + [(own, land.at[me], sib, land.at[me])])
    return plan


def _plan_swap_halves(h):
    def plan(buf_in, buf):
        x, y, c, _ = _place()
        return [(buf_in.at[_rows(c, h), :], buf.at[_rows(c, h), :], (x, y, 1 - c), buf.at[_rows(1 - c, h), :])]
    return plan


def _plan_pair_rows(h):
    def plan(src, land):
        x, y, c, _ = _place()
        return [(src.at[:, _rows(1 - c, h), :], land, (x, y, 1 - c), land)]
    return plan


def _plan_scatter(src, land):
    x, y, c, chips = _place()
    return [(src.at[2 * cx + cy], land.at[k], (cx, cy, c), land.at[k]) for k, (cx, cy) in enumerate(chips)]


def _plan_all(src, land):
    x, y, c, _ = _place()
    copies = []
    for r in range(1, 8):
        px, py, pc = (1 - a if b else a for a, b in zip((x, y, c), (r >> 2 & 1, r >> 1 & 1, r & 1)))
        copies.append((src, land.at[4 * x + 2 * y + c], (px, py, pc), land.at[4 * px + 2 * py + pc]))
    return copies


def _remote(src, dst, send_sems, recv_sems, k, peer):
    return pltpu.make_async_remote_copy(src_ref=src, dst_ref=dst, send_sem=send_sems.at[k], recv_sem=recv_sems.at[k],
                                        device_id=peer, device_id_type=MESH)


def _exchange(name, n, src, land, plan):
    def body(src_ref, land_ref, send_sems, recv_sems):
        copies = plan(src_ref, land_ref)
        for k, (s, d, peer, _) in enumerate(copies):
            _remote(s, d, send_sems, recv_sems, k, peer).start()
        for k, (s, _, peer, mine) in enumerate(copies):
            _remote(s, mine, send_sems, recv_sems, k, peer).wait_recv()
        for k, (s, d, peer, _) in enumerate(copies):
            _remote(s, d, send_sems, recv_sems, k, peer).wait_send()

    return pl.pallas_call(
        body, name=name, in_specs=[ANY], out_specs=ANY, out_shape=land,
        scratch_shapes=[pltpu.SemaphoreType.DMA((n,)), pltpu.SemaphoreType.DMA((n,))],
    )(src)


def _exchange_inplace(name, n, buf, extra, plan):
    def body(*refs):
        ins, buf_ref, send_sems, recv_sems = refs[:1 + len(extra)], refs[1 + len(extra)], refs[-2], refs[-1]
        copies = plan(*ins, buf_ref)
        for k, (s, d, peer, _) in enumerate(copies):
            _remote(s, d, send_sems, recv_sems, k, peer).start()
        for k, (s, _, peer, mine) in enumerate(copies):
            _remote(s, mine, send_sems, recv_sems, k, peer).wait_recv()
        for k, (s, d, peer, _) in enumerate(copies):
            _remote(s, d, send_sems, recv_sems, k, peer).wait_send()

    return pl.pallas_call(
        body, name=name, in_specs=[ANY] * (1 + len(extra)), out_specs=ANY, out_shape=_sds(buf.shape, buf.dtype),
        input_output_aliases={0: 0},
        scratch_shapes=[pltpu.SemaphoreType.DMA((n,)), pltpu.SemaphoreType.DMA((n,))],
    )(buf, *extra)


HBM = pl.BlockSpec(memory_space=pltpu.HBM)
SEM = pl.BlockSpec(memory_space=pltpu.SEMAPHORE)
EFFECT = pltpu.SideEffectType.DATAFLOW_SIDE_EFFECTING


def _exchange_start(name, n, src, land, plan):
    def body(src_ref, land_ref, send_sems, recv_sems, src_thru, land_thru, token):
        for k, (s, d, peer, _) in enumerate(plan(src_ref, land_ref)):
            _remote(s, d, send_sems, recv_sems, k, peer).start()
        token[...] = jnp.zeros_like(token)

    return pl.pallas_call(
        body, name=name,
        out_shape=(pltpu.SemaphoreType.DMA((n,)), pltpu.SemaphoreType.DMA((n,)), pltpu.HBM(src.shape, src.dtype),
                   pltpu.HBM(land.shape, land.dtype), jax.ShapeDtypeStruct((8, LANE), F32)),
        in_specs=(HBM, HBM), out_specs=(SEM, SEM, HBM, HBM, pl.BlockSpec(memory_space=pltpu.VMEM)),
        input_output_aliases={0: 2, 1: 3},
        compiler_params=pltpu.CompilerParams(has_side_effects=EFFECT),
    )(pltpu.with_memory_space_constraint(src, pltpu.HBM), pltpu.with_memory_space_constraint(land, pltpu.HBM))


def _exchange_wait(name, started, plan, after):
    send_sems, recv_sems, src, land, _ = started

    def body(src_ref, land_ref, send_sems, recv_sems, after_ref, src_out, land_out):
        for k, (s, _, peer, mine) in enumerate(plan(src_ref, land_ref)):
            cp = _remote(s, mine, send_sems, recv_sems, k, peer)
            cp.wait_send()
            cp.wait_recv()

    return pl.pallas_call(
        body, name=name, out_shape=(pltpu.HBM(src.shape, src.dtype), pltpu.HBM(land.shape, land.dtype)),
        in_specs=(HBM, HBM, SEM, SEM, ANY), out_specs=(HBM, HBM), input_output_aliases={0: 0, 1: 1},
        compiler_params=pltpu.CompilerParams(has_side_effects=EFFECT),
    )(src, land, send_sems, recv_sems, after)


def _pair_sum(name, cidx, g, recv, br):
    h = recv.shape[1]
    nb = h // br

    def body(c_ref, g_ref, r_ref, out_ref):
        out_ref[...] = (g_ref[...] + r_ref[...]).astype(BF)

    return pl.pallas_call(
        body, name=name,
        grid_spec=pltpu.PrefetchScalarGridSpec(
            num_scalar_prefetch=1, grid=(NSHARD, nb),
            in_specs=[pl.BlockSpec((1, br, D), lambda s, i, c: (s, c[0] * nb + i, 0)),
                      pl.BlockSpec((1, br, D), lambda s, i, c: (s, i, 0))],
            out_specs=pl.BlockSpec((1, br, D), lambda s, i, c: (s, i, 0))),
        out_shape=jax.ShapeDtypeStruct((NSHARD, h, D), BF),
    )(cidx, g, recv)


def _chip_sum(name, place, pb, y, br):
    h = y.shape[1]
    nb = h // br

    def body(pl_ref, p_ref, y_ref, out_ref):
        acc = p_ref[0].astype(F32)
        for k in range(NSHARD - 1):
            acc = acc + y_ref[k].astype(F32)
        out_ref[...] = acc

    return pl.pallas_call(
        body, name=name,
        grid_spec=pltpu.PrefetchScalarGridSpec(
            num_scalar_prefetch=1, grid=(nb,),
            in_specs=[pl.BlockSpec((1, br, D), lambda i, s: (s[0], i, 0)),
                      pl.BlockSpec((NSHARD - 1, br, D), lambda i, s: (0, i, 0))],
            out_specs=pl.BlockSpec((br, D), lambda i, s: (s[1] * nb + i, 0))),
        out_shape=jax.ShapeDtypeStruct((2 * h, D), F32),
    )(place, pb, y)


def _sum_slots(name, v):
    def body(in_ref, out_ref):
        acc = in_ref[0]
        for k in range(1, 8):
            acc = acc + in_ref[k]
        out_ref[...] = acc

    vm = pl.BlockSpec(memory_space=pltpu.VMEM)
    return pl.pallas_call(body, name=name, in_specs=[vm], out_specs=vm,
                          out_shape=jax.ShapeDtypeStruct(v.shape[1:], F32))(v)


def _adamw_math(w, g, m, v):
    m = ADAM_B1 * m + (1.0 - ADAM_B1) * g
    v = ADAM_B2 * v + (1.0 - ADAM_B2) * (g * g)
    m_hat = m / (1.0 - ADAM_B1 ** ADAM_STEP)
    v_hat = v / (1.0 - ADAM_B2 ** ADAM_STEP)
    delta = -ADAM_LR * (m_hat / (jnp.sqrt(v_hat) + ADAM_EPS) + ADAM_WD * w)
    return delta, m, v


def _adamw(w, g, m, v, dep, row0=None):
    r, c = w.shape
    br = next(b for b in (256, 176, 128, r) if r % b == 0 and (row0 or 0) % b == 0)
    first = (row0 or 0) // br

    def body(w_ref, g_ref, m_ref, v_ref, dep_ref, *outs):
        gv = g_ref[...]
        outs[-3][...], outs[-2][...], outs[-1][...] = _adamw_math(w_ref[...], gv, m_ref[...], v_ref[...])
        if row0 is not None:
            outs[0][...] = gv

    spec = pl.BlockSpec((br, c), lambda i: (i, 0))
    n_out = 3 if row0 is None else 4
    out = pl.pallas_call(
        body, name="adamw", grid=(r // br,),
        in_specs=[spec, pl.BlockSpec((br, c), lambda i: (first + i, 0)), spec, spec, ANY], out_specs=[spec] * n_out,
        out_shape=[jax.ShapeDtypeStruct((r, c), F32)] * n_out, compiler_params=_cp(),
    )(w, g, m, v, dep)
    return out if row0 is not None else [g] + list(out)


def _adamw_small(ws, gs, ms, vs):
    n = len(ws)

    def body(*refs):
        ins, outs = refs[:4 * n], refs[4 * n:]
        for k in range(n):
            d, m, v = _adamw_math(ins[k][...], ins[n + k][...], ins[2 * n + k][...], ins[3 * n + k][...])
            outs[k][...] = d
            outs[n + k][...] = m
            outs[2 * n + k][...] = v

    vm = pl.BlockSpec(memory_space=pltpu.VMEM)
    out = pl.pallas_call(
        body, name="adamw_small", in_specs=[vm] * (4 * n), out_specs=[vm] * (3 * n),
        out_shape=[jax.ShapeDtypeStruct(w.shape, F32) for w in ws] * 3,
    )(*ws, *gs, *ms, *vs)
    return out[:n], out[n:2 * n], out[2 * n:]


BIG = ("w_in", "w_out", "w_ffn_gate", "w_ffn_up", "w_ffn_down", "w_ple_proj", "w_ple_gate")
SMALL = ("g_mix_pre", "b_forget", "g_attn_grp", "g_pool_grp", "w_pool", "pool_scale", "g_mix_post", "g_ffn_pre",
         "g_ffn_post", "g_ple")
TRANSPOSED = ("w_in", "w_ffn_gate", "w_ffn_up")
VECTORS = tuple(n for n in SMALL if n != "w_pool")
ORDER = ("g_mix_pre", "w_in", "b_forget", "g_attn_grp", "g_pool_grp", "w_pool", "pool_scale", "w_out", "g_mix_post",
         "g_ffn_pre", "w_ffn_gate", "w_ffn_up", "w_ffn_down", "g_ffn_post", "w_ple_proj", "g_ple", "w_ple_gate")


def _pad_rows(a, rows):
    return jnp.pad(a, ((0, rows - a.shape[0]), (0, 0)))


def _stack1(w_in, w_out):
    return _pad_rows(jnp.concatenate([_pad_rows(w_in.T, IN_PAD), w_out], axis=0), ROWS1)


def _stack2(wg, wu, wd, wple, wpg):
    return _pad_rows(jnp.concatenate([wg.T, wu.T, wd, wple.reshape(DPLE // NSHARD, D), wpg], axis=0), ROWS2)


def _unstack1(s):
    return s[:IN_SH], s[O1_OUT:USED1]


def _cat(g, lo, hi):
    return g[:, lo:hi].reshape(NSHARD * (hi - lo), D)


def _unstack1_full(g):
    w_in_t = _cat(g, 0, IN_SH)
    w_in_t = jnp.concatenate([w_in_t[:3 * DA], _pad_rows(w_in_t[3 * DA:3 * DA + NH], LANE), w_in_t[3 * DA + NH:]], axis=0)
    return w_in_t, _cat(g, O1_OUT, USED1)


def _unstack2_full(g):
    w_ple = g[:, O2_PLE:O2_PG].reshape(NSHARD, DPLE, DPLE).transpose(1, 0, 2).reshape(DPLE, D)
    return _cat(g, 0, O2_U), _cat(g, O2_U, O2_D), _cat(g, O2_D, O2_PLE), w_ple, _cat(g, O2_PG, USED2)


def _shards(a):
    return a.reshape(NSHARD, a.shape[0] // NSHARD, D)


def _stack1_full(dwin_t, dwout):
    dwin_t = jnp.concatenate([dwin_t[:3 * DA + NH], dwin_t[3 * DA + LANE:]], axis=0).reshape(NSHARD, IN_SH, D)
    zeros = lambda r: jnp.zeros((NSHARD, r, D), F32)
    return jnp.concatenate([dwin_t, zeros(IN_PAD - IN_SH), _shards(dwout), zeros(ROWS1 - USED1)], axis=1)


def _stack2_full(dwg_t, dwu_t, dwd, dwple, dwpg):
    dwple = dwple.reshape(DPLE, NSHARD, DPLE).transpose(1, 0, 2).reshape(NSHARD, DPLE // NSHARD, D)
    return jnp.concatenate([_shards(dwg_t), _shards(dwu_t), _shards(dwd), dwple, _shards(dwpg),
                            jnp.zeros((NSHARD, ROWS2 - USED2, D), dwd.dtype)], axis=1)


def _sds(shape, dtype):
    return jax.ShapeDtypeStruct(shape, dtype)


class _Comm:
    def __init__(self, stack2, me, c):
        self.stack2, self.me, self.c = stack2, me, c
        self.cidx = c.astype(jnp.int32).reshape(1)
        self.place = jnp.stack([me, c]).astype(jnp.int32)
        self.h = ROWS2 // 2
        self.gather = _exchange_start("gather2_start", 3, stack2, lax.empty((NSHARD, ROWS2, D), BF), _plan_gather(self.h))
        self.dep = self.gather[4]

    def after_attention(self, after):
        own, land = _exchange_wait("gather2_wait", self.gather, _plan_gather(self.h), after)
        fwd = _plan_forward(self.h)
        self.forward = lambda own_ref, land_ref: fwd(land_ref, own_ref, land_ref)
        self.passing = _exchange_start("forward2_start", 4, own, land, self.forward)
        return self.passing[4]

    def weights2(self, after):
        return _exchange_wait("forward2_wait", self.passing, self.forward, after)[1]

    def after_ffn(self, grads2):
        g = _stack2_full(*grads2)
        self.pair = _exchange_start("reduce2_pair_start", 1, g, lax.empty((NSHARD, self.h, D), F32),
                                    _plan_pair_rows(self.h))
        return self.pair[4]

    def after_mix(self, after, early):
        g, recv = _exchange_wait("reduce2_pair_wait", self.pair, _plan_pair_rows(self.h), after)
        pb = _pair_sum("pair_sum2", self.cidx, g, recv, RED2)
        self.chip = _exchange_start("reduce2_chip_start", 3, pb, lax.empty((NSHARD - 1, self.h, D), BF), _plan_scatter)
        self.early_shapes = {n: early[n].shape for n in early}
        self.small = self.start_small("small", early)
        return self.chip[4] + self.small[4]

    def after_attn(self, after):
        pb, y = _exchange_wait("reduce2_chip_wait", self.chip, _plan_scatter, after)
        f = _chip_sum("chip_sum2", self.place, pb, y, RED2)
        self.reduced2 = _exchange_inplace("reduce2_gather", 1, f, (), _plan_swap_halves(self.h))
        self.early = self.finish_small("small", self.small, self.early_shapes, after)
        return self.reduced2

    def start_small(self, name, small):
        v = _pack_small(small)
        return _exchange_start(name + "_start", 7, v, lax.empty((8,) + v.shape, F32), _plan_all)

    def finish_small(self, name, started, shapes, after):
        v, land = _exchange_wait(name + "_wait", started, _plan_all, after)
        land = lax.dynamic_update_slice(land, v[None], (2 * self.me + self.c, 0, 0))
        return _unpack_small(_sum_slots(name + "_sum", land), shapes)


def _pack_small(small):
    parts = []
    for name in small:
        flat = small[name].reshape(-1)
        parts.append(jnp.pad(flat, (0, -flat.shape[0] % LANE)).reshape(-1, LANE))
    v = jnp.concatenate(parts, axis=0)
    return _pad_rows(v, v.shape[0] + (-v.shape[0] % 8))


def _unpack_small(v, shapes):
    out, r = {}, 0
    for name in shapes:
        n = math.prod(shapes[name])
        rows = -(-n // LANE)
        out[name] = v[r:r + rows].reshape(-1)[:n].reshape(shapes[name])
        r += rows
    return out


def kernel(x, p, g_mix_pre, w_in, b_forget, g_attn_grp, g_pool_grp, w_pool, pool_scale, w_out, g_mix_post, g_ffn_pre, w_ffn_gate, w_ffn_up, w_ffn_down, g_ffn_post, w_ple_proj, g_ple, w_ple_gate, loss_target, m_g_mix_pre, m_w_in, m_b_forget, m_g_attn_grp, m_g_pool_grp, m_w_pool, m_pool_scale, m_w_out, m_g_mix_post, m_g_ffn_pre, m_w_ffn_gate, m_w_ffn_up, m_w_ffn_down, m_g_ffn_post, m_w_ple_proj, m_g_ple, m_w_ple_gate, v_g_mix_pre, v_w_in, v_b_forget, v_g_attn_grp, v_g_pool_grp, v_w_pool, v_pool_scale, v_w_out, v_g_mix_post, v_g_ffn_pre, v_w_ffn_gate, v_w_ffn_up, v_w_ffn_down, v_g_ffn_post, v_w_ple_proj, v_g_ple, v_w_ple_gate):
    args = dict(locals())
    strip = lambda n, a: a if n in VECTORS else a[0]
    w = {n: strip(n, args[n]) for n in ORDER}
    mom = {n: strip(n, args["m_" + n]) for n in ORDER}
    var = {n: strip(n, args["v_" + n]) for n in ORDER}
    sm = {n: w[n] for n in SMALL}

    c = lax.axis_index("c")
    me = 2 * lax.axis_index("x") + lax.axis_index("y")
    h1 = ROWS1 // 2
    bf = lambda n: w[n].astype(BF)
    stack1 = _stack1(bf("w_in"), bf("w_out"))
    stack2 = _stack2(*[bf(n) for n in BIG[2:]])
    land = _exchange("gather1", 3, stack1, _sds((NSHARD, ROWS1, D), BF), _plan_gather(h1))
    land, stack2 = lax.optimization_barrier((land, stack2))
    comm = _Comm(stack2, me, c)
    w1 = _unstack1_full(_exchange_inplace("gather1_forward", 4, land, (stack1,), _plan_forward(h1)))
    dx, grads1, late = _local_step(x[0], p[0, 0], loss_target[0], sm, w1, comm)

    flip = lambda n, a: a.T if n in TRANSPOSED else a
    grads, delta, new_m, new_v = {}, {}, {}, {}

    def update(n, g, dep, row0=None):
        g_, d_, m_, v_ = _adamw(flip(n, w[n]), g, flip(n, mom[n]), flip(n, var[n]), dep, row0)
        grads[n], delta[n], new_m[n], new_v[n] = flip(n, g_), flip(n, d_), flip(n, m_), flip(n, v_)
        return v_

    late_shapes = {n: late[n].shape for n in late}
    small2 = comm.start_small("small2", late)
    red2 = comm.reduced2
    g1 = _stack1_full(*grads1)
    pair1 = _exchange_start("reduce1_pair_start", 1, g1, lax.empty((NSHARD, h1, D), F32), _plan_pair_rows(h1))
    dep = update("w_ple_gate", red2, pair1[4] + small2[4], O2_PG)
    dep = update("w_ple_proj", red2[O2_PLE:O2_PG].reshape(DPLE, DPLE), dep)
    g1, recv = _exchange_wait("reduce1_pair_wait", pair1, _plan_pair_rows(h1), dep)
    pb = _pair_sum("pair_sum1", comm.cidx, g1, recv, RED1)
    chip1 = _exchange_start("reduce1_chip_start", 3, pb, lax.empty((NSHARD - 1, h1, D), BF), _plan_scatter)
    dep = update("w_ffn_gate", red2, chip1[4], 0)
    dep = update("w_ffn_up", red2, dep, O2_U)
    dep = update("w_ffn_down", red2, dep, O2_D)
    red_small = {**comm.early, **comm.finish_small("small2", small2, late_shapes, dep)}
    loss = 0.5 / D * red_small["loss"][0, 0]
    for n in SMALL:
        grads[n] = red_small[n].reshape(w[n].shape)
    two_d = lambda a: a.reshape(-1, a.shape[-1])
    ds, ms, vs = _adamw_small([two_d(w[n]) for n in SMALL], [two_d(grads[n]) for n in SMALL],
                              [two_d(mom[n]) for n in SMALL], [two_d(var[n]) for n in SMALL])
    for k, n in enumerate(SMALL):
        delta[n], new_m[n], new_v[n] = ds[k].reshape(w[n].shape), ms[k].reshape(w[n].shape), vs[k].reshape(w[n].shape)
    pb, y = _exchange_wait("reduce1_chip_wait", chip1, _plan_scatter, vs[0])
    f = _chip_sum("chip_sum1", comm.place, pb, y, RED1)
    reduced1 = _exchange_inplace("reduce1_gather", 1, f, (), _plan_swap_halves(h1))
    g_in, g_out = _unstack1(reduced1)
    update("w_out", g_out, update("w_in", g_in, reduced1))

    lead = lambda d: [d[n] if n in VECTORS else d[n][None] for n in ORDER]
    return (loss, dx[None], *lead(grads), *lead(delta), *lead(new_m), *lead(new_v))
```

```python
import functools
import math

import jax
import jax.numpy as jnp
import numpy as np
from jax import lax
from jax.experimental import pallas as pl
from jax.experimental.pallas import tpu as pltpu

F32 = jnp.float32
BF = jnp.bfloat16
MESH = pl.DeviceIdType.MESH

D = 1024
DA = 512
DP = 512
NH = 8
HD = 64
DFF = 2816
DPLE = 256
WINS = (2, 4, 8, 16)
PC = 128
ZW = 3 * DA + 128 + DP
EPS = 1e-6
NSHARD = 4

LANE = 128
HALO = 128

IN_SH = 514
IN_PAD = 528
FF_SH = DFF // NSHARD
O1_OUT, USED1, ROWS1 = 528, 784, 800
O2_U, O2_D, O2_PLE, O2_PG, USED2, ROWS2 = 704, 1408, 2112, 2176, 2432, 2560
RED1, RED2 = 400, 640

ADAM_LR, ADAM_B1, ADAM_B2, ADAM_EPS, ADAM_WD, ADAM_STEP = 0.001, 0.9, 0.999, 1e-8, 0.01, 10

VMEM_LIMIT = 56 * 1024 * 1024


def _cp(**kw):
    return pltpu.CompilerParams(vmem_limit_bytes=VMEM_LIMIT, **kw)


def _mm(a, b):
    return jnp.dot(a.astype(BF), b.astype(BF), preferred_element_type=F32)


def _mm_nt(a, b):
    return lax.dot_general(a.astype(BF), b.astype(BF), (((1,), (1,)), ((), ())), preferred_element_type=F32)


def _mm_tn(a, b):
    return lax.dot_general(a.astype(BF), b.astype(BF), (((0,), (0,)), ((), ())), preferred_element_type=F32)


def _split3(x):
    hi = x.astype(BF)
    r = x - hi.astype(F32)
    mid = r.astype(BF)
    lo = (r - mid.astype(F32)).astype(BF)
    return hi, mid, lo


def _dot3(m, x):
    hi, mid, lo = _split3(x)
    return (jnp.dot(m, hi, preferred_element_type=F32) + jnp.dot(m, mid, preferred_element_type=F32)
            + jnp.dot(m, lo, preferred_element_type=F32))


def _window_sum(ext, w, back):
    n = ext.shape[0]
    s, k = ext, 1
    while k < w:
        s = s + pltpu.roll(s, k if back else n - k, 0)
        k *= 2
    return s


def _rstd(x):
    return lax.rsqrt(jnp.mean(x * x, axis=-1, keepdims=True) + EPS)


def _rms_bwd(dy, x, g):
    r = _rstd(x)
    xh = x * r
    dg = jnp.sum(dy * xh, axis=0, keepdims=True)
    dxh = dy * g
    dx = r * (dxh - xh * jnp.mean(dxh * xh, axis=-1, keepdims=True))
    return dx, dg


def _sigmoid(x):
    return 1.0 / (1.0 + jnp.exp(-x))


ANY = pl.BlockSpec(memory_space=pl.ANY)


def _full(shape):
    n = len(shape)
    return pl.BlockSpec(shape, lambda *_: (0,) * n)


def _resident(shape):
    n = len(shape)
    return pl.BlockSpec(shape, lambda *_: (0,) * n, pipeline_mode=pl.Buffered(1))


def _tile(t):
    return 512 if t % 512 == 0 else t


def _tri(n, upper):
    r, c = np.indices((n, n))
    return ((c >= r) if upper else (c <= r)).astype(BF)


def _aug_consts():
    row, col = np.indices((3 * LANE, NH * LANE))
    piece, head = row // LANE, row % LANE
    ch, cl = col // LANE, col % LANE
    eq = ((head == ch) & (cl == HD + 8 * piece + head)).astype(BF)
    ek = -((head == ch) & (cl == HD + 24 + 8 * piece + head)).astype(BF)
    off = (np.arange(NH * LANE) % LANE - HD - np.arange(NH * LANE) // LANE)[None, :]
    rowq = ((off >= 24) & (off < 48) & (off % 8 == 0)).astype(np.float32)
    rowk = ((off >= 0) & (off < 24) & (off % 8 == 0)).astype(np.float32)
    return eq, ek, rowq, rowk


def _in_proj(x, g1, w_in_t, b_pad, tri, eq, ek, rowq, rowk, dep):
    t = x.shape[0]
    tt = _tile(t)

    def body(x_ref, g_ref, w_ref, b_ref, tri_ref, eq_ref, ek_ref, rq_ref, rk_ref, dep_ref,
             qa_ref, ka_ref, v_ref, u_ref, fl_ref, carry):
        i = pl.program_id(0)

        @pl.when(i == 0)
        def _():
            carry[...] = jnp.zeros_like(carry)

        xv = x_ref[...]
        hn = (xv * _rstd(xv) * g_ref[...]).astype(BF)
        z = _mm_nt(hn, w_ref[...])
        fl = z[:, 3 * DA:3 * DA + LANE] + b_ref[...]
        lane = lax.broadcasted_iota(jnp.int32, fl.shape, 1)
        lf = jnp.where(lane < NH, jnp.minimum(fl, 0.0) - jnp.log(1.0 + jnp.exp(-jnp.abs(fl))), 0.0)
        c = carry[...] + _dot3(tri_ref[...], lf)
        carry[...] = carry[...] + jnp.sum(lf, axis=0, keepdims=True)
        caug = jnp.concatenate(_split3(c), axis=1)
        aug_q = jnp.dot(caug, eq_ref[...], preferred_element_type=F32) + rq_ref[...]
        aug_k = jnp.dot(caug, ek_ref[...], preferred_element_type=F32) + rk_ref[...]
        low = lax.broadcasted_iota(jnp.int32, (tt, LANE), 1) < HD
        for p in range(NH // 2):
            qp = z[:, LANE * p:LANE * (p + 1)] * (1.0 / math.sqrt(HD))
            kp = z[:, DA + LANE * p:DA + LANE * (p + 1)]
            for h, (qh, kh) in enumerate(((qp, kp), (pltpu.roll(qp, HD, 1), pltpu.roll(kp, HD, 1)))):
                lo_, hi_ = LANE * (2 * p + h), LANE * (2 * p + h + 1)
                qa_ref[:, lo_:hi_] = jnp.where(low, qh, aug_q[:, lo_:hi_]).astype(BF)
                ka_ref[:, lo_:hi_] = jnp.where(low, kh, aug_k[:, lo_:hi_]).astype(BF)
        v_ref[...] = z[:, 2 * DA:3 * DA].astype(BF)
        u_ref[...] = z[:, 3 * DA + LANE:]
        fl_ref[...] = fl

    return pl.pallas_call(
        body, name="in_proj", grid=(t // tt,),
        in_specs=[pl.BlockSpec((tt, D), lambda i: (i, 0)), _full((1, D)), _resident((ZW, D)), _full((1, LANE)),
                  _full((tt, tt)), _full((3 * LANE, NH * LANE)), _full((3 * LANE, NH * LANE)),
                  _full((1, NH * LANE)), _full((1, NH * LANE)), ANY],
        out_specs=[pl.BlockSpec((tt, NH * LANE), lambda i: (i, 0)), pl.BlockSpec((tt, NH * LANE), lambda i: (i, 0)),
                   pl.BlockSpec((tt, DA), lambda i: (i, 0)), pl.BlockSpec((tt, DP), lambda i: (i, 0)),
                   pl.BlockSpec((tt, LANE), lambda i: (i, 0))],
        out_shape=[jax.ShapeDtypeStruct((t, NH * LANE), BF), jax.ShapeDtypeStruct((t, NH * LANE), BF),
                   jax.ShapeDtypeStruct((t, DA), BF), jax.ShapeDtypeStruct((t, DP), F32),
                   jax.ShapeDtypeStruct((t, LANE), F32)],
        scratch_shapes=[pltpu.VMEM((1, LANE), F32)],
        compiler_params=_cp(),
    )(x, g1, w_in_t, b_pad, tri, eq, ek, rowq, rowk, dep)


def _attn_fwd(qa, ka, v):
    t = qa.shape[0]
    ta = _tile(t)
    n = t // ta

    def body(q_ref, k_ref, v_ref, a_ref, lse_ref, m_ref, l_ref, acc_ref):
        i = pl.program_id(1)
        m_ref[...] = jnp.full_like(m_ref, -1e30)
        l_ref[...] = jnp.zeros_like(l_ref)
        acc_ref[...] = jnp.zeros_like(acc_ref)
        qs = [q_ref[:, LANE * h:LANE * (h + 1)] for h in range(2)]
        reps = ta // LANE

        def tile(j, masked):
            rows = pl.ds(pl.multiple_of(j * ta, ta), ta)
            v2 = v_ref[rows, :]
            s = [_mm_nt(qs[h], k_ref[rows, LANE * h:LANE * (h + 1)]) for h in range(2)]
            if masked:
                keep = (lax.broadcasted_iota(jnp.int32, (ta, ta), 1) <= lax.broadcasted_iota(jnp.int32, (ta, ta), 0))
                s = [jnp.where(keep, sh, -1e30) for sh in s]
            m_old = [m_ref[h] for h in range(2)]
            m_new = [jnp.maximum(m_old[h], jnp.max(s[h], axis=1, keepdims=True)) for h in range(2)]
            pe = [jnp.exp(s[h] - jnp.tile(m_new[h], (1, reps))) for h in range(2)]
            alpha = [jnp.exp(m_old[h] - m_new[h]) for h in range(2)]
            pv = [jnp.dot(pe[h].astype(BF), v2, preferred_element_type=F32) for h in range(2)]
            for h in range(2):
                l_ref[h] = alpha[h] * l_ref[h] + jnp.sum(pe[h], axis=1, keepdims=True)
                acc_ref[h] = alpha[h] * acc_ref[h] + pv[h]
                m_ref[h] = m_new[h]

        def step(j, carry):
            tile(j, False)
            return carry

        lax.fori_loop(0, i, step, 0)
        tile(i, True)
        low = lax.broadcasted_iota(jnp.int32, (ta, LANE), 1) < HD
        a_ref[...] = jnp.where(low, acc_ref[0] / l_ref[0], acc_ref[1] / l_ref[1])
        lse_ref[...] = jnp.where(low, m_ref[0] + jnp.log(l_ref[0]), m_ref[1] + jnp.log(l_ref[1]))

    return pl.pallas_call(
        body, name="attn_fwd", grid=(NH // 2, n),
        in_specs=[pl.BlockSpec((ta, 2 * LANE), lambda p, i: (i, p)),
                  pl.BlockSpec((t, 2 * LANE), lambda p, i: (0, p)),
                  pl.BlockSpec((t, LANE), lambda p, i: (0, p))],
        out_specs=[pl.BlockSpec((ta, LANE), lambda p, i: (i, p)), pl.BlockSpec((ta, LANE), lambda p, i: (i, p))],
        out_shape=[jax.ShapeDtypeStruct((t, DA), F32), jax.ShapeDtypeStruct((t, DA), F32)],
        scratch_shapes=[pltpu.VMEM((2, ta, LANE), F32), pltpu.VMEM((2, ta, LANE), F32), pltpu.VMEM((2, ta, LANE), F32)],
        compiler_params=_cp(),
    )(qa, ka, v)


def _mix_out(a, u, x, w_pool, pool_scale, g_attn, g_pool, w_out, g_post, dep):
    t = a.shape[0]
    tt = _tile(t)
    hb = tt // HALO

    def body(a_ref, u_ref, up_ref, x_ref, wp_ref, ps_ref, ga_ref, gp_ref, wo_ref, go_ref, dep_ref,
             yb_ref, m_ref, o_ref, h1_ref):
        i = pl.program_id(0)
        prev = up_ref[...] * jnp.where(i > 0, 1.0, 0.0)
        tok = i * tt + lax.broadcasted_iota(jnp.int32, (tt, PC), 0)
        ms = []
        for g, w in enumerate(WINS):
            ug = u_ref[:, PC * g:PC * (g + 1)]
            ext = jnp.concatenate([prev[:, PC * g:PC * (g + 1)], ug], axis=0)
            cnt = jnp.minimum(tok + 1, w).astype(F32)
            y = (_window_sum(ext, w, True)[HALO:] / cnt - ug).astype(BF)
            yb_ref[:, PC * g:PC * (g + 1)] = y
            ms.append(_mm(y, wp_ref[g]) * ps_ref[:, PC * g:PC * (g + 1)])
        m = jnp.concatenate(ms, axis=1)
        m_ref[...] = m
        av = a_ref[...]
        mix = jnp.concatenate([av * _rstd(av) * ga_ref[...], m * _rstd(m) * gp_ref[...]], axis=1)
        o = _mm(mix, wo_ref[...])
        o_ref[...] = o
        h1_ref[...] = x_ref[...] + o * _rstd(o) * go_ref[...]

    return pl.pallas_call(
        body, name="mix_out", grid=(t // tt,),
        in_specs=[pl.BlockSpec((tt, DA), lambda i: (i, 0)), pl.BlockSpec((tt, DP), lambda i: (i, 0)),
                  pl.BlockSpec((HALO, DP), lambda i: (jnp.maximum(i * hb - 1, 0), 0)),
                  pl.BlockSpec((tt, D), lambda i: (i, 0)),
                  _full((len(WINS), PC, PC)), _full((1, DP)), _full((1, DA)), _full((1, DP)),
                  _resident((D, D)), _full((1, D)), ANY],
        out_specs=[pl.BlockSpec((tt, DP), lambda i: (i, 0)), pl.BlockSpec((tt, DP), lambda i: (i, 0)),
                   pl.BlockSpec((tt, D), lambda i: (i, 0)), pl.BlockSpec((tt, D), lambda i: (i, 0))],
        out_shape=[jax.ShapeDtypeStruct((t, DP), BF), jax.ShapeDtypeStruct((t, DP), F32),
                   jax.ShapeDtypeStruct((t, D), F32), jax.ShapeDtypeStruct((t, D), F32)],
        compiler_params=_cp(),
    )(a, u, u, x, w_pool, pool_scale, g_attn, g_pool, w_out, g_post, dep)


def _ffn_fwd(h1, g_pre, stacks2, g_post, p, w_ple, g_ple, w_pg, tgt):
    t = h1.shape[0]
    tt = 256 if t % 256 == 0 else t

    def body(h1_ref, gpre_ref, wg_ref, wu_ref, wd_ref, gpost_ref, p_ref, wple_ref, gple_ref, wpg_ref, tgt_ref,
             hn_ref, gate_ref, up_ref, dff_ref, dh2_ref, loss_ref, dwpg_ref, dwple_ref, dgple_ref, dgpost_ref):
        i = pl.program_id(0)

        @pl.when(i == 0)
        def _():
            loss_ref[...] = jnp.zeros_like(loss_ref)
            dwpg_ref[...] = jnp.zeros_like(dwpg_ref)
            dwple_ref[...] = jnp.zeros_like(dwple_ref)
            dgple_ref[...] = jnp.zeros_like(dgple_ref)
            dgpost_ref[...] = jnp.zeros_like(dgpost_ref)

        h1v = h1_ref[...]
        hn = (h1v * _rstd(h1v) * gpre_ref[...]).astype(BF)
        hn_ref[...] = hn
        gate = _mm_nt(hn, wg_ref[...].reshape(DFF, D))
        up = _mm_nt(hn, wu_ref[...].reshape(DFF, D))
        gate_ref[...] = gate.astype(BF)
        up_ref[...] = up.astype(BF)
        ff = _mm(gate * _sigmoid(gate) * up, wd_ref[...].reshape(DFF, D))
        rff = _rstd(ff)
        ffh = ff * rff
        gpost = gpost_ref[...]
        h2 = h1v + ffh * gpost
        pv = p_ref[...]
        pe = _mm(pv, wple_ref[...])
        rpe = _rstd(pe)
        peh = pe * rpe
        gple = gple_ref[...]
        e = peh * gple
        sig = _sigmoid(_mm(h2, wpg_ref[...]))
        dv = h2 + sig * e - tgt_ref[...]
        sq = jnp.sum(jnp.sum(dv * dv, axis=1, keepdims=True), axis=0, keepdims=True)
        loss_ref[...] = loss_ref[...] + sq
        dy = dv * (1.0 / D)
        d_e = dy * sig
        d_gl = dy * e * sig * (1.0 - sig)
        dh2 = dy + _mm_nt(d_gl, wpg_ref[...])
        dh2_ref[...] = dh2
        dwpg_ref[...] = dwpg_ref[...] + _mm_tn(h2, d_gl)
        dgple_ref[...] = dgple_ref[...] + jnp.sum(d_e * peh, axis=0, keepdims=True)
        dpeh = d_e * gple
        d_pe = rpe * (dpeh - peh * jnp.mean(dpeh * peh, axis=-1, keepdims=True))
        dwple_ref[...] = dwple_ref[...] + _mm_tn(pv, d_pe)
        dgpost_ref[...] = dgpost_ref[...] + jnp.sum(dh2 * ffh, axis=0, keepdims=True)
        dffh = dh2 * gpost
        dff_ref[...] = (rff * (dffh - ffh * jnp.mean(dffh * ffh, axis=-1, keepdims=True))).astype(BF)

    row = lambda w: pl.BlockSpec((tt, w), lambda i: (i, 0))
    shard_rows = lambda k: pl.BlockSpec((NSHARD, FF_SH, D), lambda i: (0, k, 0), pipeline_mode=pl.Buffered(1))
    return pl.pallas_call(
        body, name="ffn_fwd", grid=(t // tt,),
        in_specs=[row(D), _full((1, D)), shard_rows(0), shard_rows(1), shard_rows(2), _full((1, D)),
                  row(DPLE), _resident((DPLE, D)), _full((1, D)), _resident((D, D)), row(D)],
        out_specs=[row(D), row(DFF), row(DFF), row(D), row(D), _full((8, LANE)), _full((D, D)), _full((DPLE, D)),
                   _full((1, D)), _full((1, D))],
        out_shape=[jax.ShapeDtypeStruct((t, D), BF), jax.ShapeDtypeStruct((t, DFF), BF), jax.ShapeDtypeStruct((t, DFF), BF),
                   jax.ShapeDtypeStruct((t, D), BF), jax.ShapeDtypeStruct((t, D), F32), jax.ShapeDtypeStruct((8, LANE), F32),
                   jax.ShapeDtypeStruct((D, D), F32), jax.ShapeDtypeStruct((DPLE, D), F32),
                   jax.ShapeDtypeStruct((1, D), F32), jax.ShapeDtypeStruct((1, D), F32)],
        compiler_params=_cp(),
    )(h1, g_pre, stacks2, stacks2, stacks2, g_post, p, w_ple, g_ple, w_pg, tgt)


FF_CH = 256


def _ffn_bwd(hn2, gate, up, dff, wgu, wd):
    t = hn2.shape[0]
    tt = 1024 if t % 1024 == 0 else _tile(t)
    nt = t // tt
    ch = FF_CH
    nc = DFF // ch

    def body(hn_ref, gate_ref, up_ref, dff_ref, wgu_ref, wd_ref,
             dwg_ref, dwu_ref, dwd_ref, dhn_ref, acc, sem):
        j, i = pl.program_id(0), pl.program_id(1)

        @pl.when(j == 0)
        def _():
            acc[pl.ds(pl.multiple_of(i * tt, tt), tt), :] = jnp.zeros((tt, D), F32)

        @pl.when(i == 0)
        def _():
            dwg_ref[...] = jnp.zeros_like(dwg_ref)
            dwu_ref[...] = jnp.zeros_like(dwu_ref)
            dwd_ref[...] = jnp.zeros_like(dwd_ref)

        half = tt // 2
        acts, dgus = [], []
        for hh in range(2):
            r = slice(hh * half, (hh + 1) * half)
            gate_v = gate_ref[r, :].astype(F32)
            up_v = up_ref[r, :].astype(F32)
            sg = _sigmoid(gate_v)
            silu = gate_v * sg
            d_act = _mm_nt(dff_ref[r, :], wd_ref[...])
            d_up = (d_act * silu).astype(BF)
            d_gate = (d_act * up_v * (sg * (1.0 + gate_v * (1.0 - sg)))).astype(BF)
            dgu = jnp.concatenate([d_gate, d_up], axis=1)
            rows = pl.ds(pl.multiple_of(i * tt + hh * half, half), half)
            acc[rows, :] = acc[rows, :] + jnp.dot(dgu, wgu_ref[0], preferred_element_type=F32)
            acts.append((silu * up_v).astype(BF))
            dgus.append(dgu)
        dwd_ref[...] = dwd_ref[...] + _mm_tn(jnp.concatenate(acts, axis=0), dff_ref[...])
        dwgu = _mm_tn(jnp.concatenate(dgus, axis=0), hn_ref[...])
        dwg_ref[...] = dwg_ref[...] + dwgu[:ch]
        dwu_ref[...] = dwu_ref[...] + dwgu[ch:]

        @pl.when((j == nc - 1) & (i == nt - 1))
        def _():
            cp = pltpu.make_async_copy(acc, dhn_ref, sem)
            cp.start()
            cp.wait()

    tok = lambda w: pl.BlockSpec((tt, w), lambda j, i: (i, 0))
    chunk = pl.BlockSpec((ch, D), lambda j, i: (j, 0))
    pair = pl.BlockSpec((1, 2 * ch, D), lambda j, i: (j, 0, 0))
    return pl.pallas_call(
        body, name="ffn_bwd", grid=(nc, nt),
        in_specs=[tok(D), pl.BlockSpec((tt, ch), lambda j, i: (i, j)), pl.BlockSpec((tt, ch), lambda j, i: (i, j)),
                  tok(D), pair, chunk],
        out_specs=[chunk, chunk, chunk, pl.BlockSpec(memory_space=pl.ANY)],
        out_shape=[jax.ShapeDtypeStruct((DFF, D), F32), jax.ShapeDtypeStruct((DFF, D), F32),
                   jax.ShapeDtypeStruct((DFF, D), F32), jax.ShapeDtypeStruct((t, D), F32)],
        scratch_shapes=[pltpu.VMEM((t, D), F32), pltpu.SemaphoreType.DMA],
        compiler_params=_cp(),
    )(hn2, gate, up, dff, wgu, wd)


def _mix_bwd(d_hn2, dh2, h1, o, a, m, yb, g_ffn_pre, g_post, g_attn, g_pool, w_out, w_pool, pool_scale, dep):
    t = a.shape[0]
    tt = 256 if t % 256 == 0 else t

    def body(dhn_ref, dh2_ref, h1_ref, o_ref, a_ref, m_ref, yb_ref, gfp_ref, go_ref, ga_ref, gp_ref, wo_ref, wp_ref,
             ps_ref, dep_ref, dh1_ref, da_ref, dyc_ref, dgfp_ref, dgo_ref, dga_ref, dgp_ref, dps_ref, dwp_ref, dwo_ref):
        i = pl.program_id(0)

        @pl.when(i == 0)
        def _():
            for r in (dgfp_ref, dgo_ref, dga_ref, dgp_ref, dps_ref, dwp_ref, dwo_ref):
                r[...] = jnp.zeros_like(r)

        d1, dg = _rms_bwd(dhn_ref[...], h1_ref[...], gfp_ref[...])
        dgfp_ref[...] = dgfp_ref[...] + dg
        dh1 = dh2_ref[...] + d1
        dh1_ref[...] = dh1
        d_o, dg = _rms_bwd(dh1, o_ref[...], go_ref[...])
        dgo_ref[...] = dgo_ref[...] + dg
        d_mix = _mm_nt(d_o, wo_ref[...])
        av, mv = a_ref[...], m_ref[...]
        mix = jnp.concatenate([av * _rstd(av) * ga_ref[...], mv * _rstd(mv) * gp_ref[...]], axis=1)
        dwo_ref[...] = dwo_ref[...] + _mm_tn(mix, d_o)
        d_a, dg = _rms_bwd(d_mix[:, :DA], av, ga_ref[...])
        dga_ref[...] = dga_ref[...] + dg
        da_ref[...] = d_a
        d_m, dg = _rms_bwd(d_mix[:, DA:], mv, gp_ref[...])
        dgp_ref[...] = dgp_ref[...] + dg
        tok = i * tt + lax.broadcasted_iota(jnp.int32, (tt, PC), 0)
        dps = []
        for g, w in enumerate(WINS):
            sl = slice(PC * g, PC * (g + 1))
            ybg = yb_ref[:, sl]
            wpg = wp_ref[g].astype(BF)
            mlin = jnp.dot(ybg, wpg, preferred_element_type=F32)
            dmg = d_m[:, sl]
            dps.append(jnp.sum(dmg * mlin, axis=0, keepdims=True))
            dml = (dmg * ps_ref[:, sl]).astype(BF)
            dwp_ref[g] = dwp_ref[g] + _mm_tn(ybg, dml)
            dyc_ref[:, sl] = _mm_nt(dml, wpg) / jnp.minimum(tok + 1, w).astype(F32)
        dps_ref[...] = dps_ref[...] + jnp.concatenate(dps, axis=1)

    row = lambda w: pl.BlockSpec((tt, w), lambda i: (i, 0))
    return pl.pallas_call(
        body, name="mix_bwd", grid=(t // tt,),
        in_specs=[row(D), row(D), row(D), row(D), row(DA), row(DP), row(DP), _full((1, D)), _full((1, D)),
                  _full((1, DA)), _full((1, DP)), _resident((D, D)), _full((len(WINS), PC, PC)), _full((1, DP)), ANY],
        out_specs=[row(D), row(DA), row(DP), _full((1, D)), _full((1, D)), _full((1, DA)), _full((1, DP)),
                   _full((1, DP)), _full((len(WINS), PC, PC)), _full((D, D))],
        out_shape=[jax.ShapeDtypeStruct((t, D), F32), jax.ShapeDtypeStruct((t, DA), F32), jax.ShapeDtypeStruct((t, DP), F32),
                   jax.ShapeDtypeStruct((1, D), F32), jax.ShapeDtypeStruct((1, D), F32), jax.ShapeDtypeStruct((1, DA), F32),
                   jax.ShapeDtypeStruct((1, DP), F32), jax.ShapeDtypeStruct((1, DP), F32),
                   jax.ShapeDtypeStruct((len(WINS), PC, PC), F32), jax.ShapeDtypeStruct((D, D), F32)],
        compiler_params=_cp(),
    )(d_hn2, dh2, h1, o, a, m, yb, g_ffn_pre, g_post, g_attn, g_pool, w_out, w_pool, pool_scale, dep)


def _attn_bwd(qa, ka, v, a, d_a, lse, dep):
    t = qa.shape[0]
    ta = _tile(t)
    n = t // ta

    def body(q_ref, k_ref, v_ref, o_ref, do_ref, lse_ref, dep_ref, dq_ref, dk_ref, dv_ref):
        j = pl.program_id(1)

        @pl.when(j == 0)
        def _():
            dq_ref[...] = jnp.zeros_like(dq_ref)

        dk_ref[...] = jnp.zeros_like(dk_ref)
        dv_ref[...] = jnp.zeros_like(dv_ref)
        ks = [k_ref[:, LANE * h:LANE * (h + 1)] for h in range(2)]
        v2 = v_ref[...]
        lane = lax.broadcasted_iota(jnp.int32, (ta, LANE), 1)
        mine = [lane < HD, lane >= HD]

        def tile(i, masked):
            rows = pl.ds(pl.multiple_of(i * ta, ta), ta)
            do2 = do_ref[rows, :]
            prod = do2 * o_ref[rows, :]
            lse2 = lse_ref[rows, :]
            do2b = do2.astype(BF)
            qh = [q_ref[rows, LANE * h:LANE * (h + 1)] for h in range(2)]
            s = [_mm_nt(qh[h], ks[h]) for h in range(2)]
            dp = [_mm_nt(jnp.where(mine[h], do2, 0.0), v2) for h in range(2)]
            delta = [jnp.sum(jnp.where(mine[h], prod, 0.0), axis=1, keepdims=True) for h in range(2)]
            lse_h = [jnp.sum(jnp.where(lane == HD * h, lse2, 0.0), axis=1, keepdims=True) for h in range(2)]
            pr = [jnp.exp(s[h] - lse_h[h]) for h in range(2)]
            if masked:
                keep = (lax.broadcasted_iota(jnp.int32, (ta, ta), 1) <= lax.broadcasted_iota(jnp.int32, (ta, ta), 0))
                pr = [jnp.where(keep, ph, 0.0) for ph in pr]
            ds = [(pr[h] * (dp[h] - delta[h])).astype(BF) for h in range(2)]
            dv_ref[...] = dv_ref[...] + jnp.where(mine[0], _mm_tn(pr[0], do2b), _mm_tn(pr[1], do2b))
            for h in range(2):
                sl = slice(LANE * h, LANE * (h + 1))
                dk_ref[:, sl] = dk_ref[:, sl] + _mm_tn(ds[h], qh[h])
                dq_ref[0, rows, sl] = dq_ref[0, rows, sl] + jnp.dot(ds[h], ks[h], preferred_element_type=F32)

        def step(i, carry):
            tile(i, False)
            return carry

        tile(j, True)
        lax.fori_loop(j + 1, n, step, 0)

    qrow = lambda w: pl.BlockSpec((t, w), lambda p, j: (0, p))
    krow = lambda w: pl.BlockSpec((ta, w), lambda p, j: (j, p))
    return pl.pallas_call(
        body, name="attn_bwd", grid=(NH // 2, n),
        in_specs=[qrow(2 * LANE), krow(2 * LANE), krow(LANE), qrow(LANE), qrow(LANE), qrow(LANE), ANY],
        out_specs=[pl.BlockSpec((1, t, 2 * LANE), lambda p, j: (p, 0, 0)), krow(2 * LANE), krow(LANE)],
        out_shape=[jax.ShapeDtypeStruct((NH // 2, t, 2 * LANE), F32), jax.ShapeDtypeStruct((t, NH * LANE), F32),
                   jax.ShapeDtypeStruct((t, DA), F32)],
        compiler_params=_cp(),
    )(qa, ka, v, a, d_a, lse, dep)


def _in_bwd(dqa, dka, dv, dyc, fl, x, dh1, g1, w_in_t, tri_u, dep):
    t = x.shape[0]
    tt = _tile(t)
    nt = t // tt
    hb = tt // HALO
    rev = lambda s: nt - 1 - s

    def body(dqa_ref, dka_ref, dv_ref, dyc_ref, dyn_ref, fl_ref, x_ref, dh1_ref, g_ref, w_ref, tri_ref,
             dep_ref, dx_ref, dw_ref, dg_ref, db_ref, carry, acc, sem):
        s = pl.program_id(0)
        i = nt - 1 - s

        @pl.when(s == 0)
        def _():
            carry[...] = jnp.zeros_like(carry)
            acc[...] = jnp.zeros_like(acc)
            dg_ref[...] = jnp.zeros_like(dg_ref)
            db_ref[...] = jnp.zeros_like(db_ref)

        dq_cat = jnp.concatenate([dqa_ref[p] for p in range(NH // 2)], axis=1)
        dk_cat = dka_ref[...]
        off = lax.broadcasted_iota(jnp.int32, (1, NH * LANE), 1)
        off = off % LANE - HD - off // LANE

        def picked(cat, lane_off):
            kept = jnp.where(off == lane_off, cat, 0.0)
            return functools.reduce(lambda a, b: a + b, [kept[:, LANE * h:LANE * (h + 1)] for h in range(NH)])

        dc = pltpu.roll(picked(dq_cat, 0), LANE - HD, 1) - pltpu.roll(picked(dk_cat, 24), LANE - HD - 24, 1)
        dlf = carry[...] + _dot3(tri_ref[...], dc)
        carry[...] = carry[...] + jnp.sum(dc, axis=0, keepdims=True)
        flv = fl_ref[...]
        lane = lax.broadcasted_iota(jnp.int32, flv.shape, 1)
        d_fl = jnp.where(lane < NH, dlf / (1.0 + jnp.exp(flv)), 0.0)
        db_ref[...] = db_ref[...] + jnp.sum(d_fl, axis=0, keepdims=True)
        low = lax.broadcasted_iota(jnp.int32, (tt, LANE), 1) < HD
        dqs, dks = [], []
        for p in range(NH // 2):
            b0, b1 = slice(2 * LANE * p, 2 * LANE * p + LANE), slice(2 * LANE * p + LANE, 2 * LANE * (p + 1))
            dqs.append(jnp.where(low, dq_cat[:, b0], pltpu.roll(dq_cat[:, b1], HD, 1)) * (1.0 / math.sqrt(HD)))
            dks.append(jnp.where(low, dk_cat[:, b0], pltpu.roll(dk_cat[:, b1], HD, 1)))
        nxt = dyn_ref[...] * jnp.where(i < nt - 1, 1.0, 0.0)
        tok = i * tt + lax.broadcasted_iota(jnp.int32, (tt, PC), 0)
        dus = []
        for g, w in enumerate(WINS):
            sl = slice(PC * g, PC * (g + 1))
            dycg = dyc_ref[:, sl]
            ext = jnp.concatenate([dycg, nxt[:, sl]], axis=0)
            dus.append(_window_sum(ext, w, False)[:tt] - dycg * jnp.minimum(tok + 1, w).astype(F32))
        d_z = jnp.concatenate(dqs + dks + [dv_ref[...], d_fl] + dus, axis=1).astype(BF)
        xv = x_ref[...]
        gv = g_ref[...]
        hn = (xv * _rstd(xv) * gv).astype(BF)
        d_hn = jnp.dot(d_z, w_ref[...], preferred_element_type=F32)
        acc[...] = acc[...] + _mm_tn(d_z, hn)
        d1, dg = _rms_bwd(d_hn, xv, gv)
        dg_ref[...] = dg_ref[...] + dg
        dx_ref[...] = dh1_ref[...] + d1

        @pl.when(s == nt - 1)
        def _():
            cp = pltpu.make_async_copy(acc, dw_ref, sem)
            cp.start()
            cp.wait()

    row = lambda w: pl.BlockSpec((tt, w), lambda s: (rev(s), 0))
    return pl.pallas_call(
        body, name="in_bwd", grid=(nt,),
        in_specs=[pl.BlockSpec((NH // 2, tt, 2 * LANE), lambda s: (0, rev(s), 0)), row(NH * LANE), row(DA), row(DP),
                  pl.BlockSpec((HALO, DP), lambda s: (jnp.minimum((rev(s) + 1) * hb, nt * hb - 1), 0)),
                  row(LANE), row(D), row(D), _full((1, D)), _resident((ZW, D)), _full((tt, tt)),
                  ANY],
        out_specs=[row(D), pl.BlockSpec(memory_space=pl.ANY), _full((1, D)), _full((1, LANE))],
        out_shape=[jax.ShapeDtypeStruct((t, D), F32), jax.ShapeDtypeStruct((ZW, D), F32),
                   jax.ShapeDtypeStruct((1, D), F32), jax.ShapeDtypeStruct((1, LANE), F32)],
        scratch_shapes=[pltpu.VMEM((1, LANE), F32), pltpu.VMEM((ZW, D), F32), pltpu.SemaphoreType.DMA],
        compiler_params=_cp(),
    )(dqa, dka, dv, dyc, dyc, fl, x, dh1, g1, w_in_t, tri_u, dep)


class _NoComm:
    def __init__(self, w2):
        self.w2 = w2
        self.dep = jnp.zeros((8, LANE), F32)

    def after_attention(self, after):
        return self.dep

    def weights2(self, after):
        return _stack2_full(*self.w2)

    def after_ffn(self, grads2):
        self.grads2 = grads2
        return self.dep

    def after_mix(self, after, early):
        self.early = early
        return self.dep

    def after_attn(self, after):
        return self.dep


def _local_step(x, p, tgt, sm, w1, comm):
    w_in_t, w_out = w1
    tt = _tile(x.shape[0])
    eq, ek, rowq, rowk = _aug_consts()
    b_pad = jnp.pad(sm["b_forget"], ((0, 0), (0, LANE - NH)))
    qa, ka, v, u, fl = _in_proj(x, sm["g_mix_pre"], w_in_t, b_pad, _tri(tt, False), eq, ek, rowq, rowk, comm.dep)
    a, lse = _attn_fwd(qa, ka, v)
    yb, m, o, h1 = _mix_out(a, u, x, sm["w_pool"], sm["pool_scale"], sm["g_attn_grp"],
                            sm["g_pool_grp"], w_out, sm["g_mix_post"], comm.after_attention(a))
    stacks2 = comm.weights2(h1)
    wg_t, wu_t, wd, w_ple, w_pg = _unstack2_full(stacks2)
    hn2, gate, up, dff, dh2, loss, dwpg, dwple, dgple, dgfpost = _ffn_fwd(
        h1, sm["g_ffn_pre"], stacks2, sm["g_ffn_post"], p, w_ple, sm["g_ple"], w_pg, tgt)
    chunks = lambda a: a.reshape(DFF // FF_CH, FF_CH, D)
    dwg_t, dwu_t, dwd, d_hn2 = _ffn_bwd(hn2, gate, up, dff, jnp.concatenate([chunks(wg_t), chunks(wu_t)], axis=1), wd)
    dep = comm.after_ffn((dwg_t, dwu_t, dwd, dwple, dwpg))
    dh1, d_a, dyc, dgfpre, dgpost, dgattn, dgpool, dps, dwpool, dwout = _mix_bwd(
        d_hn2, dh2, h1, o, a, m, yb, sm["g_ffn_pre"], sm["g_mix_post"], sm["g_attn_grp"], sm["g_pool_grp"],
        w_out, sm["w_pool"], sm["pool_scale"], dep)
    early = dict(loss=loss[0:1, 0:1], g_attn_grp=dgattn, g_pool_grp=dgpool, w_pool=dwpool, pool_scale=dps,
                 g_mix_post=dgpost, g_ffn_pre=dgfpre, g_ffn_post=dgfpost, g_ple=dgple)
    dqa, dka, dvv = _attn_bwd(qa, ka, v, a, d_a, lse, comm.after_mix(dh1, early))
    dx, dwin_t, dg1, dbf = _in_bwd(dqa, dka, dvv, dyc, fl, x, dh1, sm["g_mix_pre"], w_in_t, _tri(tt, True),
                                   comm.after_attn(dvv))
    return dx, (dwin_t, dwout), dict(g_mix_pre=dg1, b_forget=dbf[:, :NH])


def _place():
    x, y, c = lax.axis_index("x"), lax.axis_index("y"), lax.axis_index("c")
    return x, y, c, [(1 - x, y), (x, 1 - y), (1 - x, 1 - y)]


def _rows(c, h):
    return pl.ds(pl.multiple_of(c * h, 16), h)


def _plan_gather(h):
    def plan(src, land):
        x, y, c, chips = _place()
        return [(src.at[_rows(c, h), :], land.at[2 * x + y, _rows(c, h), :], (cx, cy, c),
                 land.at[2 * cx + cy, _rows(c, h), :]) for cx, cy in chips]
    return plan


def _plan_forward(h):
    def plan(land_in, own, land):
        x, y, c, chips = _place()
        sib, me = (x, y, 1 - c), 2 * x + y
        return ([(land_in.at[2 * cx + cy, _rows(c, h), :], land.at[2 * cx + cy, _rows(c, h), :], sib,
                  land.at[2 * cx + cy, _rows(1 - c, h), :]) for cx, cy in chips]
                + [(own, land.at[me], sib, land.at[me])])
    return plan


def _plan_swap_halves(h):
    def plan(buf_in, buf):
        x, y, c, _ = _place()
        return [(buf_in.at[_rows(c, h), :], buf.at[_rows(c, h), :], (x, y, 1 - c), buf.at[_rows(1 - c, h), :])]
    return plan


def _plan_pair_rows(h):
    def plan(src, land):
        x, y, c, _ = _place()
        return [(src.at[:, _rows(1 - c, h), :], land, (x, y, 1 - c), land)]
    return plan


def _plan_scatter(src, land):
    x, y, c, chips = _place()
    return [(src.at[2 * cx + cy], land.at[k], (cx, cy, c), land.at[k]) for k, (cx, cy) in enumerate(chips)]


def _plan_all(src, land):
    x, y, c, _ = _place()
    copies = []
    for r in range(1, 8):
        px, py, pc = (1 - a if b else a for a, b in zip((x, y, c), (r >> 2 & 1, r >> 1 & 1, r & 1)))
        copies.append((src, land.at[4 * x + 2 * y + c], (px, py, pc), land.at[4 * px + 2 * py + pc]))
    return copies


def _remote(src, dst, send_sems, recv_sems, k, peer):
    return pltpu.make_async_remote_copy(src_ref=src, dst_ref=dst, send_sem=send_sems.at[k], recv_sem=recv_sems.at[k],
                                        device_id=peer, device_id_type=MESH)


def _exchange(name, n, src, land, plan):
    def body(src_ref, land_ref, send_sems, recv_sems):
        copies = plan(src_ref, land_ref)
        for k, (s, d, peer, _) in enumerate(copies):
            _remote(s, d, send_sems, recv_sems, k, peer).start()
        for k, (s, _, peer, mine) in enumerate(copies):
            _remote(s, mine, send_sems, recv_sems, k, peer).wait_recv()
        for k, (s, d, peer, _) in enumerate(copies):
            _remote(s, d, send_sems, recv_sems, k, peer).wait_send()

    return pl.pallas_call(
        body, name=name, in_specs=[ANY], out_specs=ANY, out_shape=land,
        scratch_shapes=[pltpu.SemaphoreType.DMA((n,)), pltpu.SemaphoreType.DMA((n,))],
    )(src)


def _exchange_inplace(name, n, buf, extra, plan):
    def body(*refs):
        ins, buf_ref, send_sems, recv_sems = refs[:1 + len(extra)], refs[1 + len(extra)], refs[-2], refs[-1]
        copies = plan(*ins, buf_ref)
        for k, (s, d, peer, _) in enumerate(copies):
            _remote(s, d, send_sems, recv_sems, k, peer).start()
        for k, (s, _, peer, mine) in enumerate(copies):
            _remote(s, mine, send_sems, recv_sems, k, peer).wait_recv()
        for k, (s, d, peer, _) in enumerate(copies):
            _remote(s, d, send_sems, recv_sems, k, peer).wait_send()

    return pl.pallas_call(
        body, name=name, in_specs=[ANY] * (1 + len(extra)), out_specs=ANY, out_shape=_sds(buf.shape, buf.dtype),
        input_output_aliases={0: 0},
        scratch_shapes=[pltpu.SemaphoreType.DMA((n,)), pltpu.SemaphoreType.DMA((n,))],
    )(buf, *extra)


HBM = pl.BlockSpec(memory_space=pltpu.HBM)
SEM = pl.BlockSpec(memory_space=pltpu.SEMAPHORE)
EFFECT = pltpu.SideEffectType.DATAFLOW_SIDE_EFFECTING


def _exchange_start(name, n, src, land, plan):
    def body(src_ref, land_ref, send_sems, recv_sems, src_thru, land_thru, token):
        for k, (s, d, peer, _) in enumerate(plan(src_ref, land_ref)):
            _remote(s, d, send_sems, recv_sems, k, peer).start()
        token[...] = jnp.zeros_like(token)

    return pl.pallas_call(
        body, name=name,
        out_shape=(pltpu.SemaphoreType.DMA((n,)), pltpu.SemaphoreType.DMA((n,)), pltpu.HBM(src.shape, src.dtype),
                   pltpu.HBM(land.shape, land.dtype), jax.ShapeDtypeStruct((8, LANE), F32)),
        in_specs=(HBM, HBM), out_specs=(SEM, SEM, HBM, HBM, pl.BlockSpec(memory_space=pltpu.VMEM)),
        input_output_aliases={0: 2, 1: 3},
        compiler_params=pltpu.CompilerParams(has_side_effects=EFFECT),
    )(pltpu.with_memory_space_constraint(src, pltpu.HBM), pltpu.with_memory_space_constraint(land, pltpu.HBM))


def _exchange_wait(name, started, plan, after):
    send_sems, recv_sems, src, land, _ = started

    def body(src_ref, land_ref, send_sems, recv_sems, after_ref, src_out, land_out):
        for k, (s, _, peer, mine) in enumerate(plan(src_ref, land_ref)):
            cp = _remote(s, mine, send_sems, recv_sems, k, peer)
            cp.wait_send()
            cp.wait_recv()

    return pl.pallas_call(
        body, name=name, out_shape=(pltpu.HBM(src.shape, src.dtype), pltpu.HBM(land.shape, land.dtype)),
        in_specs=(HBM, HBM, SEM, SEM, ANY), out_specs=(HBM, HBM), input_output_aliases={0: 0, 1: 1},
        compiler_params=pltpu.CompilerParams(has_side_effects=EFFECT),
    )(src, land, send_sems, recv_sems, after)


def _pair_sum(name, cidx, g, recv, br):
    h = recv.shape[1]
    nb = h // br

    def body(c_ref, g_ref, r_ref, out_ref):
        out_ref[...] = (g_ref[...] + r_ref[...]).astype(BF)

    return pl.pallas_call(
        body, name=name,
        grid_spec=pltpu.PrefetchScalarGridSpec(
            num_scalar_prefetch=1, grid=(NSHARD, nb),
            in_specs=[pl.BlockSpec((1, br, D), lambda s, i, c: (s, c[0] * nb + i, 0)),
                      pl.BlockSpec((1, br, D), lambda s, i, c: (s, i, 0))],
            out_specs=pl.BlockSpec((1, br, D), lambda s, i, c: (s, i, 0))),
        out_shape=jax.ShapeDtypeStruct((NSHARD, h, D), BF),
    )(cidx, g, recv)


def _chip_sum(name, place, pb, y, br):
    h = y.shape[1]
    nb = h // br

    def body(pl_ref, p_ref, y_ref, out_ref):
        acc = p_ref[0].astype(F32)
        for k in range(NSHARD - 1):
            acc = acc + y_ref[k].astype(F32)
        out_ref[...] = acc

    return pl.pallas_call(
        body, name=name,
        grid_spec=pltpu.PrefetchScalarGridSpec(
            num_scalar_prefetch=1, grid=(nb,),
            in_specs=[pl.BlockSpec((1, br, D), lambda i, s: (s[0], i, 0)),
                      pl.BlockSpec((NSHARD - 1, br, D), lambda i, s: (0, i, 0))],
            out_specs=pl.BlockSpec((br, D), lambda i, s: (s[1] * nb + i, 0))),
        out_shape=jax.ShapeDtypeStruct((2 * h, D), F32),
    )(place, pb, y)


def _sum_slots(name, v):
    def body(in_ref, out_ref):
        acc = in_ref[0]
        for k in range(1, 8):
            acc = acc + in_ref[k]
        out_ref[...] = acc

    vm = pl.BlockSpec(memory_space=pltpu.VMEM)
    return pl.pallas_call(body, name=name, in_specs=[vm], out_specs=vm,
                          out_shape=jax.ShapeDtypeStruct(v.shape[1:], F32))(v)


def _adamw_math(w, g, m, v):
    m = ADAM_B1 * m + (1.0 - ADAM_B1) * g
    v = ADAM_B2 * v + (1.0 - ADAM_B2) * (g * g)
    m_hat = m / (1.0 - ADAM_B1 ** ADAM_STEP)
    v_hat = v / (1.0 - ADAM_B2 ** ADAM_STEP)
    delta = -ADAM_LR * (m_hat / (jnp.sqrt(v_hat) + ADAM_EPS) + ADAM_WD * w)
    return delta, m, v


def _adamw(w, g, m, v, dep, row0=None):
    r, c = w.shape
    br = next(b for b in (256, 176, 128, r) if r % b == 0 and (row0 or 0) % b == 0)
    first = (row0 or 0) // br

    def body(w_ref, g_ref, m_ref, v_ref, dep_ref, *outs):
        gv = g_ref[...]
        outs[-3][...], outs[-2][...], outs[-1][...] = _adamw_math(w_ref[...], gv, m_ref[...], v_ref[...])
        if row0 is not None:
            outs[0][...] = gv

    spec = pl.BlockSpec((br, c), lambda i: (i, 0))
    n_out = 3 if row0 is None else 4
    out = pl.pallas_call(
        body, name="adamw", grid=(r // br,),
        in_specs=[spec, pl.BlockSpec((br, c), lambda i: (first + i, 0)), spec, spec, ANY], out_specs=[spec] * n_out,
        out_shape=[jax.ShapeDtypeStruct((r, c), F32)] * n_out, compiler_params=_cp(),
    )(w, g, m, v, dep)
    return out if row0 is not None else [g] + list(out)


def _adamw_small(ws, gs, ms, vs):
    n = len(ws)

    def body(*refs):
        ins, outs = refs[:4 * n], refs[4 * n:]
        for k in range(n):
            d, m, v = _adamw_math(ins[k][...], ins[n + k][...], ins[2 * n + k][...], ins[3 * n + k][...])
            outs[k][...] = d
            outs[n + k][...] = m
            outs[2 * n + k][...] = v

    vm = pl.BlockSpec(memory_space=pltpu.VMEM)
    out = pl.pallas_call(
        body, name="adamw_small", in_specs=[vm] * (4 * n), out_specs=[vm] * (3 * n),
        out_shape=[jax.ShapeDtypeStruct(w.shape, F32) for w in ws] * 3,
    )(*ws, *gs, *ms, *vs)
    return out[:n], out[n:2 * n], out[2 * n:]


BIG = ("w_in", "w_out", "w_ffn_gate", "w_ffn_up", "w_ffn_down", "w_ple_proj", "w_ple_gate")
SMALL = ("g_mix_pre", "b_forget", "g_attn_grp", "g_pool_grp", "w_pool", "pool_scale", "g_mix_post", "g_ffn_pre",
         "g_ffn_post", "g_ple")
TRANSPOSED = ("w_in", "w_ffn_gate", "w_ffn_up")
VECTORS = tuple(n for n in SMALL if n != "w_pool")
ORDER = ("g_mix_pre", "w_in", "b_forget", "g_attn_grp", "g_pool_grp", "w_pool", "pool_scale", "w_out", "g_mix_post",
         "g_ffn_pre", "w_ffn_gate", "w_ffn_up", "w_ffn_down", "g_ffn_post", "w_ple_proj", "g_ple", "w_ple_gate")


def _pad_rows(a, rows):
    return jnp.pad(a, ((0, rows - a.shape[0]), (0, 0)))


def _stack1(w_in, w_out):
    return _pad_rows(jnp.concatenate([_pad_rows(w_in.T, IN_PAD), w_out], axis=0), ROWS1)


def _stack2(wg, wu, wd, wple, wpg):
    return _pad_rows(jnp.concatenate([wg.T, wu.T, wd, wple.reshape(DPLE // NSHARD, D), wpg], axis=0), ROWS2)


def _unstack1(s):
    return s[:IN_SH], s[O1_OUT:USED1]


def _cat(g, lo, hi):
    return g[:, lo:hi].reshape(NSHARD * (hi - lo), D)


def _unstack1_full(g):
    w_in_t = _cat(g, 0, IN_SH)
    w_in_t = jnp.concatenate([w_in_t[:3 * DA], _pad_rows(w_in_t[3 * DA:3 * DA + NH], LANE), w_in_t[3 * DA + NH:]], axis=0)
    return w_in_t, _cat(g, O1_OUT, USED1)


def _unstack2_full(g):
    w_ple = g[:, O2_PLE:O2_PG].reshape(NSHARD, DPLE, DPLE).transpose(1, 0, 2).reshape(DPLE, D)
    return _cat(g, 0, O2_U), _cat(g, O2_U, O2_D), _cat(g, O2_D, O2_PLE), w_ple, _cat(g, O2_PG, USED2)


def _shards(a):
    return a.reshape(NSHARD, a.shape[0] // NSHARD, D)


def _stack1_full(dwin_t, dwout):
    dwin_t = jnp.concatenate([dwin_t[:3 * DA + NH], dwin_t[3 * DA + LANE:]], axis=0).reshape(NSHARD, IN_SH, D)
    zeros = lambda r: jnp.zeros((NSHARD, r, D), F32)
    return jnp.concatenate([dwin_t, zeros(IN_PAD - IN_SH), _shards(dwout), zeros(ROWS1 - USED1)], axis=1)


def _stack2_full(dwg_t, dwu_t, dwd, dwple, dwpg):
    dwple = dwple.reshape(DPLE, NSHARD, DPLE).transpose(1, 0, 2).reshape(NSHARD, DPLE // NSHARD, D)
    return jnp.concatenate([_shards(dwg_t), _shards(dwu_t), _shards(dwd), dwple, _shards(dwpg),
                            jnp.zeros((NSHARD, ROWS2 - USED2, D), dwd.dtype)], axis=1)


def _sds(shape, dtype):
    return jax.ShapeDtypeStruct(shape, dtype)


class _Comm:
    def __init__(self, stack2, me, c):
        self.stack2, self.me, self.c = stack2, me, c
        self.cidx = c.astype(jnp.int32).reshape(1)
        self.place = jnp.stack([me, c]).astype(jnp.int32)
        self.h = ROWS2 // 2
        self.gather = _exchange_start("gather2_start", 3, stack2, lax.empty((NSHARD, ROWS2, D), BF), _plan_gather(self.h))
        self.dep = self.gather[4]

    def after_attention(self, after):
        own, land = _exchange_wait("gather2_wait", self.gather, _plan_gather(self.h), after)
        fwd = _plan_forward(self.h)
        self.forward = lambda own_ref, land_ref: fwd(land_ref, own_ref, land_ref)
        self.passing = _exchange_start("forward2_start", 4, own, land, self.forward)
        return self.passing[4]

    def weights2(self, after):
        return _exchange_wait("forward2_wait", self.passing, self.forward, after)[1]

    def after_ffn(self, grads2):
        g = _stack2_full(*grads2)
        self.pair = _exchange_start("reduce2_pair_start", 1, g, lax.empty((NSHARD, self.h, D), F32),
                                    _plan_pair_rows(self.h))
        return self.pair[4]

    def after_mix(self, after, early):
        g, recv = _exchange_wait("reduce2_pair_wait", self.pair, _plan_pair_rows(self.h), after)
        pb = _pair_sum("pair_sum2", self.cidx, g, recv, RED2)
        self.chip = _exchange_start("reduce2_chip_start", 3, pb, lax.empty((NSHARD - 1, self.h, D), BF), _plan_scatter)
        self.early_shapes = {n: early[n].shape for n in early}
        self.small = self.start_small("small", early)
        return self.chip[4] + self.small[4]

    def after_attn(self, after):
        pb, y = _exchange_wait("reduce2_chip_wait", self.chip, _plan_scatter, after)
        f = _chip_sum("chip_sum2", self.place, pb, y, RED2)
        self.reduced2 = _exchange_inplace("reduce2_gather", 1, f, (), _plan_swap_halves(self.h))
        self.early = self.finish_small("small", self.small, self.early_shapes, after)
        return self.reduced2

    def start_small(self, name, small):
        v = _pack_small(small)
        return _exchange_start(name + "_start", 7, v, lax.empty((8,) + v.shape, F32), _plan_all)

    def finish_small(self, name, started, shapes, after):
        v, land = _exchange_wait(name + "_wait", started, _plan_all, after)
        land = lax.dynamic_update_slice(land, v[None], (2 * self.me + self.c, 0, 0))
        return _unpack_small(_sum_slots(name + "_sum", land), shapes)


def _pack_small(small):
    parts = []
    for name in small:
        flat = small[name].reshape(-1)
        parts.append(jnp.pad(flat, (0, -flat.shape[0] % LANE)).reshape(-1, LANE))
    v = jnp.concatenate(parts, axis=0)
    return _pad_rows(v, v.shape[0] + (-v.shape[0] % 8))


def _unpack_small(v, shapes):
    out, r = {}, 0
    for name in shapes:
        n = math.prod(shapes[name])
        rows = -(-n // LANE)
        out[name] = v[r:r + rows].reshape(-1)[:n].reshape(shapes[name])
        r += rows
    return out


def kernel(x, p, g_mix_pre, w_in, b_forget, g_attn_grp, g_pool_grp, w_pool, pool_scale, w_out, g_mix_post, g_ffn_pre, w_ffn_gate, w_ffn_up, w_ffn_down, g_ffn_post, w_ple_proj, g_ple, w_ple_gate, loss_target, m_g_mix_pre, m_w_in, m_b_forget, m_g_attn_grp, m_g_pool_grp, m_w_pool, m_pool_scale, m_w_out, m_g_mix_post, m_g_ffn_pre, m_w_ffn_gate, m_w_ffn_up, m_w_ffn_down, m_g_ffn_post, m_w_ple_proj, m_g_ple, m_w_ple_gate, v_g_mix_pre, v_w_in, v_b_forget, v_g_attn_grp, v_g_pool_grp, v_w_pool, v_pool_scale, v_w_out, v_g_mix_post, v_g_ffn_pre, v_w_ffn_gate, v_w_ffn_up, v_w_ffn_down, v_g_ffn_post, v_w_ple_proj, v_g_ple, v_w_ple_gate):
    args = dict(locals())
    strip = lambda n, a: a if n in VECTORS else a[0]
    w = {n: strip(n, args[n]) for n in ORDER}
    mom = {n: strip(n, args["m_" + n]) for n in ORDER}
    var = {n: strip(n, args["v_" + n]) for n in ORDER}
    sm = {n: w[n] for n in SMALL}

    c = lax.axis_index("c")
    me = 2 * lax.axis_index("x") + lax.axis_index("y")
    h1 = ROWS1 // 2
    bf = lambda n: w[n].astype(BF)
    stack1 = _stack1(bf("w_in"), bf("w_out"))
    stack2 = _stack2(*[bf(n) for n in BIG[2:]])
    land = _exchange("gather1", 3, stack1, _sds((NSHARD, ROWS1, D), BF), _plan_gather(h1))
    land, stack2 = lax.optimization_barrier((land, stack2))
    comm = _Comm(stack2, me, c)
    w1 = _unstack1_full(_exchange_inplace("gather1_forward", 4, land, (stack1,), _plan_forward(h1)))
    dx, grads1, late = _local_step(x[0], p[0, 0], loss_target[0], sm, w1, comm)

    flip = lambda n, a: a.T if n in TRANSPOSED else a
    grads, delta, new_m, new_v = {}, {}, {}, {}

    def update(n, g, dep, row0=None):
        g_, d_, m_, v_ = _adamw(flip(n, w[n]), g, flip(n, mom[n]), flip(n, var[n]), dep, row0)
        grads[n], delta[n], new_m[n], new_v[n] = flip(n, g_), flip(n, d_), flip(n, m_), flip(n, v_)
        return v_

    late_shapes = {n: late[n].shape for n in late}
    small2 = comm.start_small("small2", late)
    red2 = comm.reduced2
    g1 = _stack1_full(*grads1)
    pair1 = _exchange_start("reduce1_pair_start", 1, g1, lax.empty((NSHARD, h1, D), F32), _plan_pair_rows(h1))
    dep = update("w_ple_gate", red2, pair1[4] + small2[4], O2_PG)
    dep = update("w_ple_proj", red2[O2_PLE:O2_PG].reshape(DPLE, DPLE), dep)
    g1, recv = _exchange_wait("reduce1_pair_wait", pair1, _plan_pair_rows(h1), dep)
    pb = _pair_sum("pair_sum1", comm.cidx, g1, recv, RED1)
    chip1 = _exchange_start("reduce1_chip_start", 3, pb, lax.empty((NSHARD - 1, h1, D), BF), _plan_scatter)
    dep = update("w_ffn_gate", red2, chip1[4], 0)
    dep = update("w_ffn_up", red2, dep, O2_U)
    dep = update("w_ffn_down", red2, dep, O2_D)
    red_small = {**comm.early, **comm.finish_small("small2", small2, late_shapes, dep)}
    loss = 0.5 / D * red_small["loss"][0, 0]
    for n in SMALL:
        grads[n] = red_small[n].reshape(w[n].shape)
    two_d = lambda a: a.reshape(-1, a.shape[-1])
    ds, ms, vs = _adamw_small([two_d(w[n]) for n in SMALL], [two_d(grads[n]) for n in SMALL],
                              [two_d(mom[n]) for n in SMALL], [two_d(var[n]) for n in SMALL])
    for k, n in enumerate(SMALL):
        delta[n], new_m[n], new_v[n] = ds[k].reshape(w[n].shape), ms[k].reshape(w[n].shape), vs[k].reshape(w[n].shape)
    pb, y = _exchange_wait("reduce1_chip_wait", chip1, _plan_scatter, vs[0])
    f = _chip_sum("chip_sum1", comm.place, pb, y, RED1)
    reduced1 = _exchange_inplace("reduce1_gather", 1, f, (), _plan_swap_halves(h1))
    g_in, g_out = _unstack1(reduced1)
    update("w_out", g_out, update("w_in", g_in, reduced1))

    lead = lambda d: [d[n] if n in VECTORS else d[n][None] for n in ORDER]
    return (loss, dx[None], *lead(grads), *lead(delta), *lead(new_m), *lead(new_v))
```

```python
import functools
import math

import jax
import jax.numpy as jnp
import numpy as np
from jax import lax
from jax.experimental import pallas as pl
from jax.experimental.pallas import tpu as pltpu

F32 = jnp.float32
BF = jnp.bfloat16
MESH = pl.DeviceIdType.MESH

D = 1024
DA = 512
DP = 512
NH = 8
HD = 64
DFF = 2816
DPLE = 256
WINS = (2, 4, 8, 16)
PC = 128
ZW = 3 * DA + 128 + DP
EPS = 1e-6
NSHARD = 4

LANE = 128
HALO = 128

IN_SH = 514
IN_PAD = 528
FF_SH = DFF // NSHARD
O1_OUT, USED1, ROWS1 = 528, 784, 800
O2_U, O2_D, O2_PLE, O2_PG, USED2, ROWS2 = 704, 1408, 2112, 2176, 2432, 2560
RED1, RED2 = 400, 640

ADAM_LR, ADAM_B1, ADAM_B2, ADAM_EPS, ADAM_WD, ADAM_STEP = 0.001, 0.9, 0.999, 1e-8, 0.01, 10

VMEM_LIMIT = 56 * 1024 * 1024


def _cp(**kw):
    return pltpu.CompilerParams(vmem_limit_bytes=VMEM_LIMIT, **kw)


def _mm(a, b):
    return jnp.dot(a.astype(BF), b.astype(BF), preferred_element_type=F32)


def _mm_nt(a, b):
    return lax.dot_general(a.astype(BF), b.astype(BF), (((1,), (1,)), ((), ())), preferred_element_type=F32)


def _mm_tn(a, b):
    return lax.dot_general(a.astype(BF), b.astype(BF), (((0,), (0,)), ((), ())), preferred_element_type=F32)


def _split3(x):
    hi = x.astype(BF)
    r = x - hi.astype(F32)
    mid = r.astype(BF)
    lo = (r - mid.astype(F32)).astype(BF)
    return hi, mid, lo


def _dot3(m, x):
    hi, mid, lo = _split3(x)
    return (jnp.dot(m, hi, preferred_element_type=F32) + jnp.dot(m, mid, preferred_element_type=F32)
            + jnp.dot(m, lo, preferred_element_type=F32))


def _window_sum(ext, w, back):
    n = ext.shape[0]
    s, k = ext, 1
    while k < w:
        s = s + pltpu.roll(s, k if back else n - k, 0)
        k *= 2
    return s


def _rstd(x):
    return lax.rsqrt(jnp.mean(x * x, axis=-1, keepdims=True) + EPS)


def _rms_bwd(dy, x, g):
    r = _rstd(x)
    xh = x * r
    dg = jnp.sum(dy * xh, axis=0, keepdims=True)
    dxh = dy * g
    dx = r * (dxh - xh * jnp.mean(dxh * xh, axis=-1, keepdims=True))
    return dx, dg


def _sigmoid(x):
    return 1.0 / (1.0 + jnp.exp(-x))


ANY = pl.BlockSpec(memory_space=pl.ANY)


def _full(shape):
    n = len(shape)
    return pl.BlockSpec(shape, lambda *_: (0,) * n)


def _resident(shape):
    n = len(shape)
    return pl.BlockSpec(shape, lambda *_: (0,) * n, pipeline_mode=pl.Buffered(1))


def _tile(t):
    return 512 if t % 512 == 0 else t


def _tri(n, upper):
    r, c = np.indices((n, n))
    return ((c >= r) if upper else (c <= r)).astype(BF)


def _aug_consts():
    row, col = np.indices((3 * LANE, NH * LANE))
    piece, head = row // LANE, row % LANE
    ch, cl = col // LANE, col % LANE
    eq = ((head == ch) & (cl == HD + 8 * piece + head)).astype(BF)
    ek = -((head == ch) & (cl == HD + 24 + 8 * piece + head)).astype(BF)
    off = (np.arange(NH * LANE) % LANE - HD - np.arange(NH * LANE) // LANE)[None, :]
    rowq = ((off >= 24) & (off < 48) & (off % 8 == 0)).astype(np.float32)
    rowk = ((off >= 0) & (off < 24) & (off % 8 == 0)).astype(np.float32)
    return eq, ek, rowq, rowk


def _in_proj(x, g1, w_in_t, b_pad, tri, eq, ek, rowq, rowk, dep):
    t = x.shape[0]
    tt = _tile(t)

    def body(x_ref, g_ref, w_ref, b_ref, tri_ref, eq_ref, ek_ref, rq_ref, rk_ref, dep_ref,
             qa_ref, ka_ref, v_ref, u_ref, fl_ref, carry):
        i = pl.program_id(0)

        @pl.when(i == 0)
        def _():
            carry[...] = jnp.zeros_like(carry)

        xv = x_ref[...]
        hn = (xv * _rstd(xv) * g_ref[...]).astype(BF)
        z = _mm_nt(hn, w_ref[...])
        fl = z[:, 3 * DA:3 * DA + LANE] + b_ref[...]
        lane = lax.broadcasted_iota(jnp.int32, fl.shape, 1)
        lf = jnp.where(lane < NH, jnp.minimum(fl, 0.0) - jnp.log(1.0 + jnp.exp(-jnp.abs(fl))), 0.0)
        c = carry[...] + _dot3(tri_ref[...], lf)
        carry[...] = carry[...] + jnp.sum(lf, axis=0, keepdims=True)
        caug = jnp.concatenate(_split3(c), axis=1)
        aug_q = jnp.dot(caug, eq_ref[...], preferred_element_type=F32) + rq_ref[...]
        aug_k = jnp.dot(caug, ek_ref[...], preferred_element_type=F32) + rk_ref[...]
        low = lax.broadcasted_iota(jnp.int32, (tt, LANE), 1) < HD
        for p in range(NH // 2):
            qp = z[:, LANE * p:LANE * (p + 1)] * (1.0 / math.sqrt(HD))
            kp = z[:, DA + LANE * p:DA + LANE * (p + 1)]
            for h, (qh, kh) in enumerate(((qp, kp), (pltpu.roll(qp, HD, 1), pltpu.roll(kp, HD, 1)))):
                lo_, hi_ = LANE * (2 * p + h), LANE * (2 * p + h + 1)
                qa_ref[:, lo_:hi_] = jnp.where(low, qh, aug_q[:, lo_:hi_]).astype(BF)
                ka_ref[:, lo_:hi_] = jnp.where(low, kh, aug_k[:, lo_:hi_]).astype(BF)
        v_ref[...] = z[:, 2 * DA:3 * DA].astype(BF)
        u_ref[...] = z[:, 3 * DA + LANE:]
        fl_ref[...] = fl

    return pl.pallas_call(
        body, name="in_proj", grid=(t // tt,),
        in_specs=[pl.BlockSpec((tt, D), lambda i: (i, 0)), _full((1, D)), _resident((ZW, D)), _full((1, LANE)),
                  _full((tt, tt)), _full((3 * LANE, NH * LANE)), _full((3 * LANE, NH * LANE)),
                  _full((1, NH * LANE)), _full((1, NH * LANE)), ANY],
        out_specs=[pl.BlockSpec((tt, NH * LANE), lambda i: (i, 0)), pl.BlockSpec((tt, NH * LANE), lambda i: (i, 0)),
                   pl.BlockSpec((tt, DA), lambda i: (i, 0)), pl.BlockSpec((tt, DP), lambda i: (i, 0)),
                   pl.BlockSpec((tt, LANE), lambda i: (i, 0))],
        out_shape=[jax.ShapeDtypeStruct((t, NH * LANE), BF), jax.ShapeDtypeStruct((t, NH * LANE), BF),
                   jax.ShapeDtypeStruct((t, DA), BF), jax.ShapeDtypeStruct((t, DP), F32),
                   jax.ShapeDtypeStruct((t, LANE), F32)],
        scratch_shapes=[pltpu.VMEM((1, LANE), F32)],
        compiler_params=_cp(),
    )(x, g1, w_in_t, b_pad, tri, eq, ek, rowq, rowk, dep)


def _attn_fwd(qa, ka, v):
    t = qa.shape[0]
    ta = _tile(t)
    n = t // ta

    def body(q_ref, k_ref, v_ref, a_ref, lse_ref, m_ref, l_ref, acc_ref):
        i = pl.program_id(1)
        m_ref[...] = jnp.full_like(m_ref, -1e30)
        l_ref[...] = jnp.zeros_like(l_ref)
        acc_ref[...] = jnp.zeros_like(acc_ref)
        qs = [q_ref[:, LANE * h:LANE * (h + 1)] for h in range(2)]
        reps = ta // LANE

        def tile(j, masked):
            rows = pl.ds(pl.multiple_of(j * ta, ta), ta)
            v2 = v_ref[rows, :]
            s = [_mm_nt(qs[h], k_ref[rows, LANE * h:LANE * (h + 1)]) for h in range(2)]
            if masked:
                keep = (lax.broadcasted_iota(jnp.int32, (ta, ta), 1) <= lax.broadcasted_iota(jnp.int32, (ta, ta), 0))
                s = [jnp.where(keep, sh, -1e30) for sh in s]
            m_old = [m_ref[h] for h in range(2)]
            m_new = [jnp.maximum(m_old[h], jnp.max(s[h], axis=1, keepdims=True)) for h in range(2)]
            pe = [jnp.exp(s[h] - jnp.tile(m_new[h], (1, reps))) for h in range(2)]
            alpha = [jnp.exp(m_old[h] - m_new[h]) for h in range(2)]
            pv = [jnp.dot(pe[h].astype(BF), v2, preferred_element_type=F32) for h in range(2)]
            for h in range(2):
                l_ref[h] = alpha[h] * l_ref[h] + jnp.sum(pe[h], axis=1, keepdims=True)
                acc_ref[h] = alpha[h] * acc_ref[h] + pv[h]
                m_ref[h] = m_new[h]

        def step(j, carry):
            tile(j, False)
            return carry

        lax.fori_loop(0, i, step, 0)
        tile(i, True)
        low = lax.broadcasted_iota(jnp.int32, (ta, LANE), 1) < HD
        a_ref[...] = jnp.where(low, acc_ref[0] / l_ref[0], acc_ref[1] / l_ref[1])
        lse_ref[...] = jnp.where(low, m_ref[0] + jnp.log(l_ref[0]), m_ref[1] + jnp.log(l_ref[1]))

    return pl.pallas_call(
        body, name="attn_fwd", grid=(NH // 2, n),
        in_specs=[pl.BlockSpec((ta, 2 * LANE), lambda p, i: (i, p)),
                  pl.BlockSpec((t, 2 * LANE), lambda p, i: (0, p)),
                  pl.BlockSpec((t, LANE), lambda p, i: (0, p))],
        out_specs=[pl.BlockSpec((ta, LANE), lambda p, i: (i, p)), pl.BlockSpec((ta, LANE), lambda p, i: (i, p))],
        out_shape=[jax.ShapeDtypeStruct((t, DA), F32), jax.ShapeDtypeStruct((t, DA), F32)],
        scratch_shapes=[pltpu.VMEM((2, ta, LANE), F32), pltpu.VMEM((2, ta, LANE), F32), pltpu.VMEM((2, ta, LANE), F32)],
        compiler_params=_cp(),
    )(qa, ka, v)


def _mix_out(a, u, x, w_pool, pool_scale, g_attn, g_pool, w_out, g_post, dep):
    t = a.shape[0]
    tt = _tile(t)
    hb = tt // HALO

    def body(a_ref, u_ref, up_ref, x_ref, wp_ref, ps_ref, ga_ref, gp_ref, wo_ref, go_ref, dep_ref,
             yb_ref, m_ref, o_ref, h1_ref):
        i = pl.program_id(0)
        prev = up_ref[...] * jnp.where(i > 0, 1.0, 0.0)
        tok = i * tt + lax.broadcasted_iota(jnp.int32, (tt, PC), 0)
        ms = []
        for g, w in enumerate(WINS):
            ug = u_ref[:, PC * g:PC * (g + 1)]
            ext = jnp.concatenate([prev[:, PC * g:PC * (g + 1)], ug], axis=0)
            cnt = jnp.minimum(tok + 1, w).astype(F32)
            y = (_window_sum(ext, w, True)[HALO:] / cnt - ug).astype(BF)
            yb_ref[:, PC * g:PC * (g + 1)] = y
            ms.append(_mm(y, wp_ref[g]) * ps_ref[:, PC * g:PC * (g + 1)])
        m = jnp.concatenate(ms, axis=1)
        m_ref[...] = m
        av = a_ref[...]
        mix = jnp.concatenate([av * _rstd(av) * ga_ref[...], m * _rstd(m) * gp_ref[...]], axis=1)
        o = _mm(mix, wo_ref[...])
        o_ref[...] = o
        h1_ref[...] = x_ref[...] + o * _rstd(o) * go_ref[...]

    return pl.pallas_call(
        body, name="mix_out", grid=(t // tt,),
        in_specs=[pl.BlockSpec((tt, DA), lambda i: (i, 0)), pl.BlockSpec((tt, DP), lambda i: (i, 0)),
                  pl.BlockSpec((HALO, DP), lambda i: (jnp.maximum(i * hb - 1, 0), 0)),
                  pl.BlockSpec((tt, D), lambda i: (i, 0)),
                  _full((len(WINS), PC, PC)), _full((1, DP)), _full((1, DA)), _full((1, DP)),
                  _resident((D, D)), _full((1, D)), ANY],
        out_specs=[pl.BlockSpec((tt, DP), lambda i: (i, 0)), pl.BlockSpec((tt, DP), lambda i: (i, 0)),
                   pl.BlockSpec((tt, D), lambda i: (i, 0)), pl.BlockSpec((tt, D), lambda i: (i, 0))],
        out_shape=[jax.ShapeDtypeStruct((t, DP), BF), jax.ShapeDtypeStruct((t, DP), F32),
                   jax.ShapeDtypeStruct((t, D), F32), jax.ShapeDtypeStruct((t, D), F32)],
        compiler_params=_cp(),
    )(a, u, u, x, w_pool, pool_scale, g_attn, g_pool, w_out, g_post, dep)


def _ffn_fwd(h1, g_pre, stacks2, g_post, p, w_ple, g_ple, w_pg, tgt):
    t = h1.shape[0]
    tt = 256 if t % 256 == 0 else t

    def body(h1_ref, gpre_ref, wg_ref, wu_ref, wd_ref, gpost_ref, p_ref, wple_ref, gple_ref, wpg_ref, tgt_ref,
             hn_ref, gate_ref, up_ref, dff_ref, dh2_ref, loss_ref, dwpg_ref, dwple_ref, dgple_ref, dgpost_ref):
        i = pl.program_id(0)

        @pl.when(i == 0)
        def _():
            loss_ref[...] = jnp.zeros_like(loss_ref)
            dwpg_ref[...] = jnp.zeros_like(dwpg_ref)
            dwple_ref[...] = jnp.zeros_like(dwple_ref)
            dgple_ref[...] = jnp.zeros_like(dgple_ref)
            dgpost_ref[...] = jnp.zeros_like(dgpost_ref)

        h1v = h1_ref[...]
        hn = (h1v * _rstd(h1v) * gpre_ref[...]).astype(BF)
        hn_ref[...] = hn
        gate = _mm_nt(hn, wg_ref[...].reshape(DFF, D))
        up = _mm_nt(hn, wu_ref[...].reshape(DFF, D))
        for k in range(DFF // FF_CH):
            gate_ref[k] = gate[:, FF_CH * k:FF_CH * (k + 1)].astype(BF)
            up_ref[k] = up[:, FF_CH * k:FF_CH * (k + 1)].astype(BF)
        ff = _mm(gate * _sigmoid(gate) * up, wd_ref[...].reshape(DFF, D))
        rff = _rstd(ff)
        ffh = ff * rff
        gpost = gpost_ref[...]
        h2 = h1v + ffh * gpost
        pv = p_ref[...]
        pe = _mm(pv, wple_ref[...])
        rpe = _rstd(pe)
        peh = pe * rpe
        gple = gple_ref[...]
        e = peh * gple
        sig = _sigmoid(_mm(h2, wpg_ref[...]))
        dv = h2 + sig * e - tgt_ref[...]
        sq = jnp.sum(jnp.sum(dv * dv, axis=1, keepdims=True), axis=0, keepdims=True)
        loss_ref[...] = loss_ref[...] + sq
        dy = dv * (1.0 / D)
        d_e = dy * sig
        d_gl = dy * e * sig * (1.0 - sig)
        dh2 = dy + _mm_nt(d_gl, wpg_ref[...])
        dh2_ref[...] = dh2
        dwpg_ref[...] = dwpg_ref[...] + _mm_tn(h2, d_gl)
        dgple_ref[...] = dgple_ref[...] + jnp.sum(d_e * peh, axis=0, keepdims=True)
        dpeh = d_e * gple
        d_pe = rpe * (dpeh - peh * jnp.mean(dpeh * peh, axis=-1, keepdims=True))
        dwple_ref[...] = dwple_ref[...] + _mm_tn(pv, d_pe)
        dgpost_ref[...] = dgpost_ref[...] + jnp.sum(dh2 * ffh, axis=0, keepdims=True)
        dffh = dh2 * gpost
        dff_ref[...] = (rff * (dffh - ffh * jnp.mean(dffh * ffh, axis=-1, keepdims=True))).astype(BF)

    row = lambda w: pl.BlockSpec((tt, w), lambda i: (i, 0))
    chunked = pl.BlockSpec((DFF // FF_CH, tt, FF_CH), lambda i: (0, i, 0))
    shard_rows = lambda k: pl.BlockSpec((NSHARD, FF_SH, D), lambda i: (0, k, 0), pipeline_mode=pl.Buffered(1))
    return pl.pallas_call(
        body, name="ffn_fwd", grid=(t // tt,),
        in_specs=[row(D), _full((1, D)), shard_rows(0), shard_rows(1), shard_rows(2), _full((1, D)),
                  row(DPLE), _resident((DPLE, D)), _full((1, D)), _resident((D, D)), row(D)],
        out_specs=[row(D), chunked, chunked, row(D), row(D), _full((8, LANE)), _full((D, D)), _full((DPLE, D)),
                   _full((1, D)), _full((1, D))],
        out_shape=[jax.ShapeDtypeStruct((t, D), BF), jax.ShapeDtypeStruct((DFF // FF_CH, t, FF_CH), BF),
                   jax.ShapeDtypeStruct((DFF // FF_CH, t, FF_CH), BF),
                   jax.ShapeDtypeStruct((t, D), BF), jax.ShapeDtypeStruct((t, D), F32), jax.ShapeDtypeStruct((8, LANE), F32),
                   jax.ShapeDtypeStruct((D, D), F32), jax.ShapeDtypeStruct((DPLE, D), F32),
                   jax.ShapeDtypeStruct((1, D), F32), jax.ShapeDtypeStruct((1, D), F32)],
        compiler_params=_cp(),
    )(h1, g_pre, stacks2, stacks2, stacks2, g_post, p, w_ple, g_ple, w_pg, tgt)


FF_CH = 256


def _ffn_bwd(hn2, gate, up, dff, wgu, wd):
    t = hn2.shape[0]
    tt = 1024 if t % 1024 == 0 else _tile(t)
    nt = t // tt
    ch = FF_CH
    nc = DFF // ch

    def body(hn_ref, gate_ref, up_ref, dff_ref, wgu_ref, wd_ref,
             dwg_ref, dwu_ref, dwd_ref, dhn_ref, acc, sem):
        j, i = pl.program_id(0), pl.program_id(1)

        @pl.when(j == 0)
        def _():
            acc[pl.ds(pl.multiple_of(i * tt, tt), tt), :] = jnp.zeros((tt, D), F32)

        @pl.when(i == 0)
        def _():
            dwg_ref[...] = jnp.zeros_like(dwg_ref)
            dwu_ref[...] = jnp.zeros_like(dwu_ref)
            dwd_ref[...] = jnp.zeros_like(dwd_ref)

        parts = 4 if tt % 1024 == 0 else 2
        half = tt // parts
        acts, dgus = [], []
        for hh in range(parts):
            r = slice(hh * half, (hh + 1) * half)
            gate_v = gate_ref[0, r, :].astype(F32)
            up_v = up_ref[0, r, :].astype(F32)
            sg = _sigmoid(gate_v)
            silu = gate_v * sg
            d_act = _mm_nt(dff_ref[r, :], wd_ref[...])
            d_up = (d_act * silu).astype(BF)
            d_gate = (d_act * up_v * (sg * (1.0 + gate_v * (1.0 - sg)))).astype(BF)
            dgu = jnp.concatenate([d_gate, d_up], axis=1)
            rows = pl.ds(pl.multiple_of(i * tt + hh * half, half), half)
            acc[rows, :] = acc[rows, :] + jnp.dot(dgu, wgu_ref[0], preferred_element_type=F32)
            acts.append((silu * up_v).astype(BF))
            dgus.append(dgu)
        dwd_ref[...] = dwd_ref[...] + _mm_tn(jnp.concatenate(acts, axis=0), dff_ref[...])
        dwgu = _mm_tn(jnp.concatenate(dgus, axis=0), hn_ref[...])
        dwg_ref[...] = dwg_ref[...] + dwgu[:ch]
        dwu_ref[...] = dwu_ref[...] + dwgu[ch:]

        @pl.when((j == nc - 1) & (i == nt - 1))
        def _():
            cp = pltpu.make_async_copy(acc, dhn_ref, sem)
            cp.start()
            cp.wait()

    tok = lambda w: pl.BlockSpec((tt, w), lambda j, i: (i, 0))
    chunk = pl.BlockSpec((ch, D), lambda j, i: (j, 0))
    pair = pl.BlockSpec((1, 2 * ch, D), lambda j, i: (j, 0, 0))
    return pl.pallas_call(
        body, name="ffn_bwd", grid=(nc, nt),
        in_specs=[tok(D), pl.BlockSpec((1, tt, ch), lambda j, i: (j, i, 0)), pl.BlockSpec((1, tt, ch), lambda j, i: (j, i, 0)),
                  tok(D), pair, chunk],
        out_specs=[chunk, chunk, chunk, pl.BlockSpec(memory_space=pl.ANY)],
        out_shape=[jax.ShapeDtypeStruct((DFF, D), F32), jax.ShapeDtypeStruct((DFF, D), F32),
                   jax.ShapeDtypeStruct((DFF, D), F32), jax.ShapeDtypeStruct((t, D), F32)],
        scratch_shapes=[pltpu.VMEM((t, D), F32), pltpu.SemaphoreType.DMA],
        compiler_params=_cp(),
    )(hn2, gate, up, dff, wgu, wd)


def _mix_bwd(d_hn2, dh2, h1, o, a, m, yb, g_ffn_pre, g_post, g_attn, g_pool, w_out, w_pool, pool_scale, dep):
    t = a.shape[0]
    tt = 256 if t % 256 == 0 else t

    def body(dhn_ref, dh2_ref, h1_ref, o_ref, a_ref, m_ref, yb_ref, gfp_ref, go_ref, ga_ref, gp_ref, wo_ref, wp_ref,
             ps_ref, dep_ref, dh1_ref, da_ref, dyc_ref, dgfp_ref, dgo_ref, dga_ref, dgp_ref, dps_ref, dwp_ref, dwo_ref):
        i = pl.program_id(0)

        @pl.when(i == 0)
        def _():
            for r in (dgfp_ref, dgo_ref, dga_ref, dgp_ref, dps_ref, dwp_ref, dwo_ref):
                r[...] = jnp.zeros_like(r)

        d1, dg = _rms_bwd(dhn_ref[...], h1_ref[...], gfp_ref[...])
        dgfp_ref[...] = dgfp_ref[...] + dg
        dh1 = dh2_ref[...] + d1
        dh1_ref[...] = dh1
        d_o, dg = _rms_bwd(dh1, o_ref[...], go_ref[...])
        dgo_ref[...] = dgo_ref[...] + dg
        d_mix = _mm_nt(d_o, wo_ref[...])
        av, mv = a_ref[...], m_ref[...]
        mix = jnp.concatenate([av * _rstd(av) * ga_ref[...], mv * _rstd(mv) * gp_ref[...]], axis=1)
        dwo_ref[...] = dwo_ref[...] + _mm_tn(mix, d_o)
        d_a, dg = _rms_bwd(d_mix[:, :DA], av, ga_ref[...])
        dga_ref[...] = dga_ref[...] + dg
        da_ref[...] = d_a
        d_m, dg = _rms_bwd(d_mix[:, DA:], mv, gp_ref[...])
        dgp_ref[...] = dgp_ref[...] + dg
        tok = i * tt + lax.broadcasted_iota(jnp.int32, (tt, PC), 0)
        dps = []
        for g, w in enumerate(WINS):
            sl = slice(PC * g, PC * (g + 1))
            ybg = yb_ref[:, sl]
            wpg = wp_ref[g].astype(BF)
            mlin = jnp.dot(ybg, wpg, preferred_element_type=F32)
            dmg = d_m[:, sl]
            dps.append(jnp.sum(dmg * mlin, axis=0, keepdims=True))
            dml = (dmg * ps_ref[:, sl]).astype(BF)
            dwp_ref[g] = dwp_ref[g] + _mm_tn(ybg, dml)
            dyc_ref[:, sl] = _mm_nt(dml, wpg) / jnp.minimum(tok + 1, w).astype(F32)
        dps_ref[...] = dps_ref[...] + jnp.concatenate(dps, axis=1)

    row = lambda w: pl.BlockSpec((tt, w), lambda i: (i, 0))
    return pl.pallas_call(
        body, name="mix_bwd", grid=(t // tt,),
        in_specs=[row(D), row(D), row(D), row(D), row(DA), row(DP), row(DP), _full((1, D)), _full((1, D)),
                  _full((1, DA)), _full((1, DP)), _resident((D, D)), _full((len(WINS), PC, PC)), _full((1, DP)), ANY],
        out_specs=[row(D), row(DA), row(DP), _full((1, D)), _full((1, D)), _full((1, DA)), _full((1, DP)),
                   _full((1, DP)), _full((len(WINS), PC, PC)), _full((D, D))],
        out_shape=[jax.ShapeDtypeStruct((t, D), F32), jax.ShapeDtypeStruct((t, DA), F32), jax.ShapeDtypeStruct((t, DP), F32),
                   jax.ShapeDtypeStruct((1, D), F32), jax.ShapeDtypeStruct((1, D), F32), jax.ShapeDtypeStruct((1, DA), F32),
                   jax.ShapeDtypeStruct((1, DP), F32), jax.ShapeDtypeStruct((1, DP), F32),
                   jax.ShapeDtypeStruct((len(WINS), PC, PC), F32), jax.ShapeDtypeStruct((D, D), F32)],
        compiler_params=_cp(),
    )(d_hn2, dh2, h1, o, a, m, yb, g_ffn_pre, g_post, g_attn, g_pool, w_out, w_pool, pool_scale, dep)


def _attn_bwd(qa, ka, v, a, d_a, lse, dep):
    t = qa.shape[0]
    ta = _tile(t)
    n = t // ta

    def body(q_ref, k_ref, v_ref, o_ref, do_ref, lse_ref, dep_ref, dq_ref, dk_ref, dv_ref):
        j = pl.program_id(1)

        @pl.when(j == 0)
        def _():
            dq_ref[...] = jnp.zeros_like(dq_ref)

        dk_ref[...] = jnp.zeros_like(dk_ref)
        dv_ref[...] = jnp.zeros_like(dv_ref)
        ks = [k_ref[:, LANE * h:LANE * (h + 1)] for h in range(2)]
        v2 = v_ref[...]
        lane = lax.broadcasted_iota(jnp.int32, (ta, LANE), 1)
        mine = [lane < HD, lane >= HD]

        def tile(i, masked):
            rows = pl.ds(pl.multiple_of(i * ta, ta), ta)
            do2 = do_ref[rows, :]
            prod = do2 * o_ref[rows, :]
            lse2 = lse_ref[rows, :]
            do2b = do2.astype(BF)
            qh = [q_ref[rows, LANE * h:LANE * (h + 1)] for h in range(2)]
            s = [_mm_nt(qh[h], ks[h]) for h in range(2)]
            dp = [_mm_nt(jnp.where(mine[h], do2, 0.0), v2) for h in range(2)]
            delta = [jnp.sum(jnp.where(mine[h], prod, 0.0), axis=1, keepdims=True) for h in range(2)]
            lse_h = [jnp.sum(jnp.where(lane == HD * h, lse2, 0.0), axis=1, keepdims=True) for h in range(2)]
            pr = [jnp.exp(s[h] - lse_h[h]) for h in range(2)]
            if masked:
                keep = (lax.broadcasted_iota(jnp.int32, (ta, ta), 1) <= lax.broadcasted_iota(jnp.int32, (ta, ta), 0))
                pr = [jnp.where(keep, ph, 0.0) for ph in pr]
            ds = [(pr[h] * (dp[h] - delta[h])).astype(BF) for h in range(2)]
            dv_ref[...] = dv_ref[...] + jnp.where(mine[0], _mm_tn(pr[0], do2b), _mm_tn(pr[1], do2b))
            for h in range(2):
                sl = slice(LANE * h, LANE * (h + 1))
                dk_ref[:, sl] = dk_ref[:, sl] + _mm_tn(ds[h], qh[h])
                dq_ref[0, rows, sl] = dq_ref[0, rows, sl] + jnp.dot(ds[h], ks[h], preferred_element_type=F32)

        def step(i, carry):
            tile(i, False)
            return carry

        tile(j, True)
        lax.fori_loop(j + 1, n, step, 0)

    qrow = lambda w: pl.BlockSpec((t, w), lambda p, j: (0, p))
    krow = lambda w: pl.BlockSpec((ta, w), lambda p, j: (j, p))
    return pl.pallas_call(
        body, name="attn_bwd", grid=(NH // 2, n),
        in_specs=[qrow(2 * LANE), krow(2 * LANE), krow(LANE), qrow(LANE), qrow(LANE), qrow(LANE), ANY],
        out_specs=[pl.BlockSpec((1, t, 2 * LANE), lambda p, j: (p, 0, 0)), krow(2 * LANE), krow(LANE)],
        out_shape=[jax.ShapeDtypeStruct((NH // 2, t, 2 * LANE), F32), jax.ShapeDtypeStruct((t, NH * LANE), F32),
                   jax.ShapeDtypeStruct((t, DA), F32)],
        compiler_params=_cp(),
    )(qa, ka, v, a, d_a, lse, dep)


def _in_bwd(dqa, dka, dv, dyc, fl, x, dh1, g1, w_in_t, tri_u, dep):
    t = x.shape[0]
    tt = _tile(t)
    nt = t // tt
    hb = tt // HALO
    rev = lambda s: nt - 1 - s

    def body(dqa_ref, dka_ref, dv_ref, dyc_ref, dyn_ref, fl_ref, x_ref, dh1_ref, g_ref, w_ref, tri_ref,
             dep_ref, dx_ref, dw_ref, dg_ref, db_ref, carry, acc, sem):
        s = pl.program_id(0)
        i = nt - 1 - s

        @pl.when(s == 0)
        def _():
            carry[...] = jnp.zeros_like(carry)
            acc[...] = jnp.zeros_like(acc)
            dg_ref[...] = jnp.zeros_like(dg_ref)
            db_ref[...] = jnp.zeros_like(db_ref)

        dq_cat = jnp.concatenate([dqa_ref[p] for p in range(NH // 2)], axis=1)
        dk_cat = dka_ref[...]
        off = lax.broadcasted_iota(jnp.int32, (1, NH * LANE), 1)
        off = off % LANE - HD - off // LANE

        def picked(cat, lane_off):
            kept = jnp.where(off == lane_off, cat, 0.0)
            return functools.reduce(lambda a, b: a + b, [kept[:, LANE * h:LANE * (h + 1)] for h in range(NH)])

        dc = pltpu.roll(picked(dq_cat, 0), LANE - HD, 1) - pltpu.roll(picked(dk_cat, 24), LANE - HD - 24, 1)
        dlf = carry[...] + _dot3(tri_ref[...], dc)
        carry[...] = carry[...] + jnp.sum(dc, axis=0, keepdims=True)
        flv = fl_ref[...]
        lane = lax.broadcasted_iota(jnp.int32, flv.shape, 1)
        d_fl = jnp.where(lane < NH, dlf / (1.0 + jnp.exp(flv)), 0.0)
        db_ref[...] = db_ref[...] + jnp.sum(d_fl, axis=0, keepdims=True)
        low = lax.broadcasted_iota(jnp.int32, (tt, LANE), 1) < HD
        dqs, dks = [], []
        for p in range(NH // 2):
            b0, b1 = slice(2 * LANE * p, 2 * LANE * p + LANE), slice(2 * LANE * p + LANE, 2 * LANE * (p + 1))
            dqs.append(jnp.where(low, dq_cat[:, b0], pltpu.roll(dq_cat[:, b1], HD, 1)) * (1.0 / math.sqrt(HD)))
            dks.append(jnp.where(low, dk_cat[:, b0], pltpu.roll(dk_cat[:, b1], HD, 1)))
        nxt = dyn_ref[...] * jnp.where(i < nt - 1, 1.0, 0.0)
        tok = i * tt + lax.broadcasted_iota(jnp.int32, (tt, PC), 0)
        dus = []
        for g, w in enumerate(WINS):
            sl = slice(PC * g, PC * (g + 1))
            dycg = dyc_ref[:, sl]
            ext = jnp.concatenate([dycg, nxt[:, sl]], axis=0)
            dus.append(_window_sum(ext, w, False)[:tt] - dycg * jnp.minimum(tok + 1, w).astype(F32))
        d_z = jnp.concatenate(dqs + dks + [dv_ref[...], d_fl] + dus, axis=1).astype(BF)
        xv = x_ref[...]
        gv = g_ref[...]
        hn = (xv * _rstd(xv) * gv).astype(BF)
        d_hn = jnp.dot(d_z, w_ref[...], preferred_element_type=F32)
        acc[...] = acc[...] + _mm_tn(d_z, hn)
        d1, dg = _rms_bwd(d_hn, xv, gv)
        dg_ref[...] = dg_ref[...] + dg
        dx_ref[...] = dh1_ref[...] + d1

        @pl.when(s == nt - 1)
        def _():
            cp = pltpu.make_async_copy(acc, dw_ref, sem)
            cp.start()
            cp.wait()

    row = lambda w: pl.BlockSpec((tt, w), lambda s: (rev(s), 0))
    return pl.pallas_call(
        body, name="in_bwd", grid=(nt,),
        in_specs=[pl.BlockSpec((NH // 2, tt, 2 * LANE), lambda s: (0, rev(s), 0)), row(NH * LANE), row(DA), row(DP),
                  pl.BlockSpec((HALO, DP), lambda s: (jnp.minimum((rev(s) + 1) * hb, nt * hb - 1), 0)),
                  row(LANE), row(D), row(D), _full((1, D)), _resident((ZW, D)), _full((tt, tt)),
                  ANY],
        out_specs=[row(D), pl.BlockSpec(memory_space=pl.ANY), _full((1, D)), _full((1, LANE))],
        out_shape=[jax.ShapeDtypeStruct((t, D), F32), jax.ShapeDtypeStruct((ZW, D), F32),
                   jax.ShapeDtypeStruct((1, D), F32), jax.ShapeDtypeStruct((1, LANE), F32)],
        scratch_shapes=[pltpu.VMEM((1, LANE), F32), pltpu.VMEM((ZW, D), F32), pltpu.SemaphoreType.DMA],
        compiler_params=_cp(),
    )(dqa, dka, dv, dyc, dyc, fl, x, dh1, g1, w_in_t, tri_u, dep)


class _NoComm:
    def __init__(self, w2):
        self.w2 = w2
        self.dep = jnp.zeros((8, LANE), F32)

    def after_attention(self, after):
        return self.dep

    def weights2(self, after):
        return _stack2_full(*self.w2)

    def after_ffn(self, grads2):
        self.grads2 = grads2
        return self.dep

    def after_mix(self, after, early):
        self.early = early
        return self.dep

    def after_attn(self, after):
        return self.dep


def _local_step(x, p, tgt, sm, w1, comm):
    w_in_t, w_out = w1
    tt = _tile(x.shape[0])
    eq, ek, rowq, rowk = _aug_consts()
    b_pad = jnp.pad(sm["b_forget"], ((0, 0), (0, LANE - NH)))
    qa, ka, v, u, fl = _in_proj(x, sm["g_mix_pre"], w_in_t, b_pad, _tri(tt, False), eq, ek, rowq, rowk, comm.dep)
    a, lse = _attn_fwd(qa, ka, v)
    yb, m, o, h1 = _mix_out(a, u, x, sm["w_pool"], sm["pool_scale"], sm["g_attn_grp"],
                            sm["g_pool_grp"], w_out, sm["g_mix_post"], comm.after_attention(a))
    stacks2 = comm.weights2(h1)
    wg_t, wu_t, wd, w_ple, w_pg = _unstack2_full(stacks2)
    hn2, gate, up, dff, dh2, loss, dwpg, dwple, dgple, dgfpost = _ffn_fwd(
        h1, sm["g_ffn_pre"], stacks2, sm["g_ffn_post"], p, w_ple, sm["g_ple"], w_pg, tgt)
    chunks = lambda a: a.reshape(DFF // FF_CH, FF_CH, D)
    dwg_t, dwu_t, dwd, d_hn2 = _ffn_bwd(hn2, gate, up, dff, jnp.concatenate([chunks(wg_t), chunks(wu_t)], axis=1), wd)
    dep = comm.after_ffn((dwg_t, dwu_t, dwd, dwple, dwpg))
    dh1, d_a, dyc, dgfpre, dgpost, dgattn, dgpool, dps, dwpool, dwout = _mix_bwd(
        d_hn2, dh2, h1, o, a, m, yb, sm["g_ffn_pre"], sm["g_mix_post"], sm["g_attn_grp"], sm["g_pool_grp"],
        w_out, sm["w_pool"], sm["pool_scale"], dep)
    early = dict(loss=loss[0:1, 0:1], g_attn_grp=dgattn, g_pool_grp=dgpool, w_pool=dwpool, pool_scale=dps,
                 g_mix_post=dgpost, g_ffn_pre=dgfpre, g_ffn_post=dgfpost, g_ple=dgple)
    dqa, dka, dvv = _attn_bwd(qa, ka, v, a, d_a, lse, comm.after_mix(dh1, early))
    dx, dwin_t, dg1, dbf = _in_bwd(dqa, dka, dvv, dyc, fl, x, dh1, sm["g_mix_pre"], w_in_t, _tri(tt, True),
                                   comm.after_attn(dvv))
    return dx, (dwin_t, dwout), dict(g_mix_pre=dg1, b_forget=dbf[:, :NH])


def _place():
    x, y, c = lax.axis_index("x"), lax.axis_index("y"), lax.axis_index("c")
    return x, y, c, [(1 - x, y), (x, 1 - y), (1 - x, 1 - y)]


def _rows(c, h):
    return pl.ds(pl.multiple_of(c * h, 16), h)


def _plan_gather(h):
    def plan(src, land):
        x, y, c, chips = _place()
        return [(src.at[_rows(c, h), :], land.at[2 * x + y, _rows(c, h), :], (cx, cy, c),
                 land.at[2 * cx + cy, _rows(c, h), :]) for cx, cy in chips]
    return plan


def _plan_forward(h):
    def plan(land_in, own, land):
        x, y, c, chips = _place()
        sib, me = (x, y, 1 - c), 2 * x + y
        return ([(land_in.at[2 * cx + cy, _rows(c, h), :], land.at[2 * cx + cy, _rows(c, h), :], sib,
                  land.at[2 * cx + cy, _rows(1 - c, h), :]) for cx, cy in chips]
                + [(own, land.at[me], sib, land.at[me])])
    return plan


def _plan_swap_halves(h):
    def plan(buf_in, buf):
        x, y, c, _ = _place()
        return [(buf_in.at[_rows(c, h), :], buf.at[_rows(c, h), :], (x, y, 1 - c), buf.at[_rows(1 - c, h), :])]
    return plan


def _plan_pair_rows(h):
    def plan(src, land):
        x, y, c, _ = _place()
        return [(src.at[:, _rows(1 - c, h), :], land, (x, y, 1 - c), land)]
    return plan


def _plan_scatter(src, land):
    x, y, c, chips = _place()
    return [(src.at[2 * cx + cy], land.at[k], (cx, cy, c), land.at[k]) for k, (cx, cy) in enumerate(chips)]


def _plan_all(src, land):
    x, y, c, _ = _place()
    copies = []
    for r in range(1, 8):
        px, py, pc = (1 - a if b else a for a, b in zip((x, y, c), (r >> 2 & 1, r >> 1 & 1, r & 1)))
        copies.append((src, land.at[4 * x + 2 * y + c], (px, py, pc), land.at[4 * px + 2 * py + pc]))
    return copies


def _remote(src, dst, send_sems, recv_sems, k, peer):
    return pltpu.make_async_remote_copy(src_ref=src, dst_ref=dst, send_sem=send_sems.at[k], recv_sem=recv_sems.at[k],
                                        device_id=peer, device_id_type=MESH)


def _exchange(name, n, src, land, plan):
    def body(src_ref, land_ref, send_sems, recv_sems):
        copies = plan(src_ref, land_ref)
        for k, (s, d, peer, _) in enumerate(copies):
            _remote(s, d, send_sems, recv_sems, k, peer).start()
        for k, (s, _, peer, mine) in enumerate(copies):
            _remote(s, mine, send_sems, recv_sems, k, peer).wait_recv()
        for k, (s, d, peer, _) in enumerate(copies):
            _remote(s, d, send_sems, recv_sems, k, peer).wait_send()

    return pl.pallas_call(
        body, name=name, in_specs=[ANY], out_specs=ANY, out_shape=land,
        scratch_shapes=[pltpu.SemaphoreType.DMA((n,)), pltpu.SemaphoreType.DMA((n,))],
    )(src)


def _exchange_inplace(name, n, buf, extra, plan):
    def body(*refs):
        ins, buf_ref, send_sems, recv_sems = refs[:1 + len(extra)], refs[1 + len(extra)], refs[-2], refs[-1]
        copies = plan(*ins, buf_ref)
        for k, (s, d, peer, _) in enumerate(copies):
            _remote(s, d, send_sems, recv_sems, k, peer).start()
        for k, (s, _, peer, mine) in enumerate(copies):
            _remote(s, mine, send_sems, recv_sems, k, peer).wait_recv()
        for k, (s, d, peer, _) in enumerate(copies):
            _remote(s, d, send_sems, recv_sems, k, peer).wait_send()

    return pl.pallas_call(
        body, name=name, in_specs=[ANY] * (1 + len(extra)), out_specs=ANY, out_shape=_sds(buf.shape, buf.dtype),
        input_output_aliases={0: 0},
        scratch_shapes=[pltpu.SemaphoreType.DMA((n,)), pltpu.SemaphoreType.DMA((n,))],
    )(buf, *extra)


HBM = pl.BlockSpec(memory_space=pltpu.HBM)
SEM = pl.BlockSpec(memory_space=pltpu.SEMAPHORE)
EFFECT = pltpu.SideEffectType.DATAFLOW_SIDE_EFFECTING


def _exchange_start(name, n, src, land, plan):
    def body(src_ref, land_ref, send_sems, recv_sems, src_thru, land_thru, token):
        for k, (s, d, peer, _) in enumerate(plan(src_ref, land_ref)):
            _remote(s, d, send_sems, recv_sems, k, peer).start()
        token[...] = jnp.zeros_like(token)

    return pl.pallas_call(
        body, name=name,
        out_shape=(pltpu.SemaphoreType.DMA((n,)), pltpu.SemaphoreType.DMA((n,)), pltpu.HBM(src.shape, src.dtype),
                   pltpu.HBM(land.shape, land.dtype), jax.ShapeDtypeStruct((8, LANE), F32)),
        in_specs=(HBM, HBM), out_specs=(SEM, SEM, HBM, HBM, pl.BlockSpec(memory_space=pltpu.VMEM)),
        input_output_aliases={0: 2, 1: 3},
        compiler_params=pltpu.CompilerParams(has_side_effects=EFFECT),
    )(pltpu.with_memory_space_constraint(src, pltpu.HBM), pltpu.with_memory_space_constraint(land, pltpu.HBM))


def _exchange_wait(name, started, plan, after):
    send_sems, recv_sems, src, land, _ = started

    def body(src_ref, land_ref, send_sems, recv_sems, after_ref, src_out, land_out):
        for k, (s, _, peer, mine) in enumerate(plan(src_ref, land_ref)):
            cp = _remote(s, mine, send_sems, recv_sems, k, peer)
            cp.wait_send()
            cp.wait_recv()

    return pl.pallas_call(
        body, name=name, out_shape=(pltpu.HBM(src.shape, src.dtype), pltpu.HBM(land.shape, land.dtype)),
        in_specs=(HBM, HBM, SEM, SEM, ANY), out_specs=(HBM, HBM), input_output_aliases={0: 0, 1: 1},
        compiler_params=pltpu.CompilerParams(has_side_effects=EFFECT),
    )(src, land, send_sems, recv_sems, after)


def _pair_sum(name, cidx, g, recv, br):
    h = recv.shape[1]
    nb = h // br

    def body(c_ref, g_ref, r_ref, out_ref):
        out_ref[...] = (g_ref[...] + r_ref[...]).astype(BF)

    return pl.pallas_call(
        body, name=name,
        grid_spec=pltpu.PrefetchScalarGridSpec(
            num_scalar_prefetch=1, grid=(NSHARD, nb),
            in_specs=[pl.BlockSpec((1, br, D), lambda s, i, c: (s, c[0] * nb + i, 0)),
                      pl.BlockSpec((1, br, D), lambda s, i, c: (s, i, 0))],
            out_specs=pl.BlockSpec((1, br, D), lambda s, i, c: (s, i, 0))),
        out_shape=jax.ShapeDtypeStruct((NSHARD, h, D), BF),
    )(cidx, g, recv)


def _chip_sum(name, place, pb, y, br):
    h = y.shape[1]
    nb = h // br

    def body(pl_ref, p_ref, y_ref, out_ref):
        acc = p_ref[0].astype(F32)
        for k in range(NSHARD - 1):
            acc = acc + y_ref[k].astype(F32)
        out_ref[...] = acc

    return pl.pallas_call(
        body, name=name,
        grid_spec=pltpu.PrefetchScalarGridSpec(
            num_scalar_prefetch=1, grid=(nb,),
            in_specs=[pl.BlockSpec((1, br, D), lambda i, s: (s[0], i, 0)),
                      pl.BlockSpec((NSHARD - 1, br, D), lambda i, s: (0, i, 0))],
            out_specs=pl.BlockSpec((br, D), lambda i, s: (s[1] * nb + i, 0))),
        out_shape=jax.ShapeDtypeStruct((2 * h, D), F32),
    )(place, pb, y)


def _sum_slots(name, v):
    def body(in_ref, out_ref):
        acc = in_ref[0]
        for k in range(1, 8):
            acc = acc + in_ref[k]
        out_ref[...] = acc

    vm = pl.BlockSpec(memory_space=pltpu.VMEM)
    return pl.pallas_call(body, name=name, in_specs=[vm], out_specs=vm,
                          out_shape=jax.ShapeDtypeStruct(v.shape[1:], F32))(v)


def _adamw_math(w, g, m, v):
    m = ADAM_B1 * m + (1.0 - ADAM_B1) * g
    v = ADAM_B2 * v + (1.0 - ADAM_B2) * (g * g)
    m_hat = m / (1.0 - ADAM_B1 ** ADAM_STEP)
    v_hat = v / (1.0 - ADAM_B2 ** ADAM_STEP)
    delta = -ADAM_LR * (m_hat / (jnp.sqrt(v_hat) + ADAM_EPS) + ADAM_WD * w)
    return delta, m, v


def _adamw(w, g, m, v, dep, row0=None):
    r, c = w.shape
    br = next(b for b in (256, 176, 128, r) if r % b == 0 and (row0 or 0) % b == 0)
    first = (row0 or 0) // br

    def body(w_ref, g_ref, m_ref, v_ref, dep_ref, *outs):
        gv = g_ref[...]
        outs[-3][...], outs[-2][...], outs[-1][...] = _adamw_math(w_ref[...], gv, m_ref[...], v_ref[...])
        if row0 is not None:
            outs[0][...] = gv

    spec = pl.BlockSpec((br, c), lambda i: (i, 0))
    n_out = 3 if row0 is None else 4
    out = pl.pallas_call(
        body, name="adamw", grid=(r // br,),
        in_specs=[spec, pl.BlockSpec((br, c), lambda i: (first + i, 0)), spec, spec, ANY], out_specs=[spec] * n_out,
        out_shape=[jax.ShapeDtypeStruct((r, c), F32)] * n_out, compiler_params=_cp(),
    )(w, g, m, v, dep)
    return out if row0 is not None else [g] + list(out)


def _adamw_small(ws, gs, ms, vs):
    n = len(ws)

    def body(*refs):
        ins, outs = refs[:4 * n], refs[4 * n:]
        for k in range(n):
            d, m, v = _adamw_math(ins[k][...], ins[n + k][...], ins[2 * n + k][...], ins[3 * n + k][...])
            outs[k][...] = d
            outs[n + k][...] = m
            outs[2 * n + k][...] = v

    vm = pl.BlockSpec(memory_space=pltpu.VMEM)
    out = pl.pallas_call(
        body, name="adamw_small", in_specs=[vm] * (4 * n), out_specs=[vm] * (3 * n),
        out_shape=[jax.ShapeDtypeStruct(w.shape, F32) for w in ws] * 3,
    )(*ws, *gs, *ms, *vs)
    return out[:n], out[n:2 * n], out[2 * n:]


BIG = ("w_in", "w_out", "w_ffn_gate", "w_ffn_up", "w_ffn_down", "w_ple_proj", "w_ple_gate")
SMALL = ("g_mix_pre", "b_forget", "g_attn_grp", "g_pool_grp", "w_pool", "pool_scale", "g_mix_post", "g_ffn_pre",
         "g_ffn_post", "g_ple")
TRANSPOSED = ("w_in", "w_ffn_gate", "w_ffn_up")
VECTORS = tuple(n for n in SMALL if n != "w_pool")
ORDER = ("g_mix_pre", "w_in", "b_forget", "g_attn_grp", "g_pool_grp", "w_pool", "pool_scale", "w_out", "g_mix_post",
         "g_ffn_pre", "w_ffn_gate", "w_ffn_up", "w_ffn_down", "g_ffn_post", "w_ple_proj", "g_ple", "w_ple_gate")


def _pad_rows(a, rows):
    return jnp.pad(a, ((0, rows - a.shape[0]), (0, 0)))


def _stack1(w_in, w_out):
    return _pad_rows(jnp.concatenate([_pad_rows(w_in.T, IN_PAD), w_out], axis=0), ROWS1)


def _stack2(wg, wu, wd, wple, wpg):
    return _pad_rows(jnp.concatenate([wg.T, wu.T, wd, wple.reshape(DPLE // NSHARD, D), wpg], axis=0), ROWS2)


def _unstack1(s):
    return s[:IN_SH], s[O1_OUT:USED1]


def _cat(g, lo, hi):
    return g[:, lo:hi].reshape(NSHARD * (hi - lo), D)


def _unstack1_full(g):
    w_in_t = _cat(g, 0, IN_SH)
    w_in_t = jnp.concatenate([w_in_t[:3 * DA], _pad_rows(w_in_t[3 * DA:3 * DA + NH], LANE), w_in_t[3 * DA + NH:]], axis=0)
    return w_in_t, _cat(g, O1_OUT, USED1)


def _unstack2_full(g):
    w_ple = g[:, O2_PLE:O2_PG].reshape(NSHARD, DPLE, DPLE).transpose(1, 0, 2).reshape(DPLE, D)
    return _cat(g, 0, O2_U), _cat(g, O2_U, O2_D), _cat(g, O2_D, O2_PLE), w_ple, _cat(g, O2_PG, USED2)


def _shards(a):
    return a.reshape(NSHARD, a.shape[0] // NSHARD, D)


def _stack1_full(dwin_t, dwout):
    dwin_t = jnp.concatenate([dwin_t[:3 * DA + NH], dwin_t[3 * DA + LANE:]], axis=0).reshape(NSHARD, IN_SH, D)
    zeros = lambda r: jnp.zeros((NSHARD, r, D), F32)
    return jnp.concatenate([dwin_t, zeros(IN_PAD - IN_SH), _shards(dwout), zeros(ROWS1 - USED1)], axis=1)


def _stack2_full(dwg_t, dwu_t, dwd, dwple, dwpg):
    dwple = dwple.reshape(DPLE, NSHARD, DPLE).transpose(1, 0, 2).reshape(NSHARD, DPLE // NSHARD, D)
    return jnp.concatenate([_shards(dwg_t), _shards(dwu_t), _shards(dwd), dwple, _shards(dwpg),
                            jnp.zeros((NSHARD, ROWS2 - USED2, D), dwd.dtype)], axis=1)


def _sds(shape, dtype):
    return jax.ShapeDtypeStruct(shape, dtype)


class _Comm:
    def __init__(self, stack2, me, c):
        self.stack2, self.me, self.c = stack2, me, c
        self.cidx = c.astype(jnp.int32).reshape(1)
        self.place = jnp.stack([me, c]).astype(jnp.int32)
        self.h = ROWS2 // 2
        self.gather = _exchange_start("gather2_start", 3, stack2, lax.empty((NSHARD, ROWS2, D), BF), _plan_gather(self.h))
        self.dep = self.gather[4]

    def after_attention(self, after):
        own, land = _exchange_wait("gather2_wait", self.gather, _plan_gather(self.h), after)
        fwd = _plan_forward(self.h)
        self.forward = lambda own_ref, land_ref: fwd(land_ref, own_ref, land_ref)
        self.passing = _exchange_start("forward2_start", 4, own, land, self.forward)
        return self.passing[4]

    def weights2(self, after):
        return _exchange_wait("forward2_wait", self.passing, self.forward, after)[1]

    def after_ffn(self, grads2):
        g = _stack2_full(*grads2)
        self.pair = _exchange_start("reduce2_pair_start", 1, g, lax.empty((NSHARD, self.h, D), F32),
                                    _plan_pair_rows(self.h))
        return self.pair[4]

    def after_mix(self, after, early):
        g, recv = _exchange_wait("reduce2_pair_wait", self.pair, _plan_pair_rows(self.h), after)
        pb = _pair_sum("pair_sum2", self.cidx, g, recv, RED2)
        self.chip = _exchange_start("reduce2_chip_start", 3, pb, lax.empty((NSHARD - 1, self.h, D), BF), _plan_scatter)
        self.early_shapes = {n: early[n].shape for n in early}
        self.small = self.start_small("small", early)
        return self.chip[4] + self.small[4]

    def after_attn(self, after):
        pb, y = _exchange_wait("reduce2_chip_wait", self.chip, _plan_scatter, after)
        f = _chip_sum("chip_sum2", self.place, pb, y, RED2)
        self.reduced2 = _exchange_inplace("reduce2_gather", 1, f, (), _plan_swap_halves(self.h))
        self.early = self.finish_small("small", self.small, self.early_shapes, after)
        return self.reduced2

    def start_small(self, name, small):
        v = _pack_small(small)
        return _exchange_start(name + "_start", 7, v, lax.empty((8,) + v.shape, F32), _plan_all)

    def finish_small(self, name, started, shapes, after):
        v, land = _exchange_wait(name + "_wait", started, _plan_all, after)
        land = lax.dynamic_update_slice(land, v[None], (2 * self.me + self.c, 0, 0))
        return _unpack_small(_sum_slots(name + "_sum", land), shapes)


def _pack_small(small):
    parts = []
    for name in small:
        flat = small[name].reshape(-1)
        parts.append(jnp.pad(flat, (0, -flat.shape[0] % LANE)).reshape(-1, LANE))
    v = jnp.concatenate(parts, axis=0)
    return _pad_rows(v, v.shape[0] + (-v.shape[0] % 8))


def _unpack_small(v, shapes):
    out, r = {}, 0
    for name in shapes:
        n = math.prod(shapes[name])
        rows = -(-n // LANE)
        out[name] = v[r:r + rows].reshape(-1)[:n].reshape(shapes[name])
        r += rows
    return out


def kernel(x, p, g_mix_pre, w_in, b_forget, g_attn_grp, g_pool_grp, w_pool, pool_scale, w_out, g_mix_post, g_ffn_pre, w_ffn_gate, w_ffn_up, w_ffn_down, g_ffn_post, w_ple_proj, g_ple, w_ple_gate, loss_target, m_g_mix_pre, m_w_in, m_b_forget, m_g_attn_grp, m_g_pool_grp, m_w_pool, m_pool_scale, m_w_out, m_g_mix_post, m_g_ffn_pre, m_w_ffn_gate, m_w_ffn_up, m_w_ffn_down, m_g_ffn_post, m_w_ple_proj, m_g_ple, m_w_ple_gate, v_g_mix_pre, v_w_in, v_b_forget, v_g_attn_grp, v_g_pool_grp, v_w_pool, v_pool_scale, v_w_out, v_g_mix_post, v_g_ffn_pre, v_w_ffn_gate, v_w_ffn_up, v_w_ffn_down, v_g_ffn_post, v_w_ple_proj, v_g_ple, v_w_ple_gate):
    args = dict(locals())
    strip = lambda n, a: a if n in VECTORS else a[0]
    w = {n: strip(n, args[n]) for n in ORDER}
    mom = {n: strip(n, args["m_" + n]) for n in ORDER}
    var = {n: strip(n, args["v_" + n]) for n in ORDER}
    sm = {n: w[n] for n in SMALL}

    c = lax.axis_index("c")
    me = 2 * lax.axis_index("x") + lax.axis_index("y")
    h1 = ROWS1 // 2
    bf = lambda n: w[n].astype(BF)
    stack1 = _stack1(bf("w_in"), bf("w_out"))
    stack2 = _stack2(*[bf(n) for n in BIG[2:]])
    land = _exchange("gather1", 3, stack1, _sds((NSHARD, ROWS1, D), BF), _plan_gather(h1))
    land, stack2 = lax.optimization_barrier((land, stack2))
    comm = _Comm(stack2, me, c)
    w1 = _unstack1_full(_exchange_inplace("gather1_forward", 4, land, (stack1,), _plan_forward(h1)))
    dx, grads1, late = _local_step(x[0], p[0, 0], loss_target[0], sm, w1, comm)

    flip = lambda n, a: a.T if n in TRANSPOSED else a
    grads, delta, new_m, new_v = {}, {}, {}, {}

    def update(n, g, dep, row0=None):
        g_, d_, m_, v_ = _adamw(flip(n, w[n]), g, flip(n, mom[n]), flip(n, var[n]), dep, row0)
        grads[n], delta[n], new_m[n], new_v[n] = flip(n, g_), flip(n, d_), flip(n, m_), flip(n, v_)
        return v_

    late_shapes = {n: late[n].shape for n in late}
    small2 = comm.start_small("small2", late)
    red2 = comm.reduced2
    g1 = _stack1_full(*grads1)
    pair1 = _exchange_start("reduce1_pair_start", 1, g1, lax.empty((NSHARD, h1, D), F32), _plan_pair_rows(h1))
    dep = update("w_ple_gate", red2, pair1[4] + small2[4], O2_PG)
    dep = update("w_ple_proj", red2[O2_PLE:O2_PG].reshape(DPLE, DPLE), dep)
    g1, recv = _exchange_wait("reduce1_pair_wait", pair1, _plan_pair_rows(h1), dep)
    pb = _pair_sum("pair_sum1", comm.cidx, g1, recv, RED1)
    chip1 = _exchange_start("reduce1_chip_start", 3, pb, lax.empty((NSHARD - 1, h1, D), BF), _plan_scatter)
    dep = update("w_ffn_gate", red2, chip1[4], 0)
    dep = update("w_ffn_up", red2, dep, O2_U)
    dep = update("w_ffn_down", red2, dep, O2_D)
    red_small = {**comm.early, **comm.finish_small("small2", small2, late_shapes, dep)}
    loss = 0.5 / D * red_small["loss"][0, 0]
    for n in SMALL:
        grads[n] = red_small[n].reshape(w[n].shape)
    two_d = lambda a: a.reshape(-1, a.shape[-1])
    ds, ms, vs = _adamw_small([two_d(w[n]) for n in SMALL], [two_d(grads[n]) for n in SMALL],
                              [two_d(mom[n]) for n in SMALL], [two_d(var[n]) for n in SMALL])
    for k, n in enumerate(SMALL):
        delta[n], new_m[n], new_v[n] = ds[k].reshape(w[n].shape), ms[k].reshape(w[n].shape), vs[k].reshape(w[n].shape)
    pb, y = _exchange_wait("reduce1_chip_wait", chip1, _plan_scatter, vs[0])
    f = _chip_sum("chip_sum1", comm.place, pb, y, RED1)
    reduced1 = _exchange_inplace("reduce1_gather", 1, f, (), _plan_swap_halves(h1))
    g_in, g_out = _unstack1(reduced1)
    update("w_out", g_out, update("w_in", g_in, reduced1))

    lead = lambda d: [d[n] if n in VECTORS else d[n][None] for n in ORDER]
    return (loss, dx[None], *lead(grads), *lead(delta), *lead(new_m), *lead(new_v))
```

```python
import functools
import math

import jax
import jax.numpy as jnp
import numpy as np
from jax import lax
from jax.experimental import pallas as pl
from jax.experimental.pallas import tpu as pltpu

F32 = jnp.float32
BF = jnp.bfloat16
MESH = pl.DeviceIdType.MESH

D = 1024
DA = 512
DP = 512
NH = 8
HD = 64
DFF = 2816
DPLE = 256
WINS = (2, 4, 8, 16)
PC = 128
ZW = 3 * DA + 128 + DP
EPS = 1e-6
NSHARD = 4

LANE = 128
HALO = 128

IN_SH = 514
IN_PAD = 528
FF_SH = DFF // NSHARD
O1_OUT, USED1, ROWS1 = 528, 784, 800
O2_U, O2_D, O2_PLE, O2_PG, USED2, ROWS2 = 704, 1408, 2112, 2176, 2432, 2560
RED1, RED2 = 400, 640

ADAM_LR, ADAM_B1, ADAM_B2, ADAM_EPS, ADAM_WD, ADAM_STEP = 0.001, 0.9, 0.999, 1e-8, 0.01, 10

VMEM_LIMIT = 56 * 1024 * 1024


def _cp(**kw):
    return pltpu.CompilerParams(vmem_limit_bytes=VMEM_LIMIT, **kw)


def _mm(a, b):
    return jnp.dot(a.astype(BF), b.astype(BF), preferred_element_type=F32)


def _mm_nt(a, b):
    return lax.dot_general(a.astype(BF), b.astype(BF), (((1,), (1,)), ((), ())), preferred_element_type=F32)


def _mm_tn(a, b):
    return lax.dot_general(a.astype(BF), b.astype(BF), (((0,), (0,)), ((), ())), preferred_element_type=F32)


def _split3(x):
    hi = x.astype(BF)
    r = x - hi.astype(F32)
    mid = r.astype(BF)
    lo = (r - mid.astype(F32)).astype(BF)
    return hi, mid, lo


def _dot3(m, x):
    hi, mid, lo = _split3(x)
    return (jnp.dot(m, hi, preferred_element_type=F32) + jnp.dot(m, mid, preferred_element_type=F32)
            + jnp.dot(m, lo, preferred_element_type=F32))


def _window_sum(ext, w, back):
    n = ext.shape[0]
    s, k = ext, 1
    while k < w:
        s = s + pltpu.roll(s, k if back else n - k, 0)
        k *= 2
    return s


def _rstd(x):
    return lax.rsqrt(jnp.mean(x * x, axis=-1, keepdims=True) + EPS)


def _rms_bwd(dy, x, g):
    r = _rstd(x)
    xh = x * r
    dg = jnp.sum(dy * xh, axis=0, keepdims=True)
    dxh = dy * g
    dx = r * (dxh - xh * jnp.mean(dxh * xh, axis=-1, keepdims=True))
    return dx, dg


def _sigmoid(x):
    return 1.0 / (1.0 + jnp.exp(-x))


ANY = pl.BlockSpec(memory_space=pl.ANY)


def _full(shape):
    n = len(shape)
    return pl.BlockSpec(shape, lambda *_: (0,) * n)


def _resident(shape):
    n = len(shape)
    return pl.BlockSpec(shape, lambda *_: (0,) * n, pipeline_mode=pl.Buffered(1))


def _tile(t):
    return 512 if t % 512 == 0 else t


def _tri(n, upper):
    r, c = np.indices((n, n))
    return ((c >= r) if upper else (c <= r)).astype(BF)


def _aug_consts():
    row, col = np.indices((3 * LANE, NH * LANE))
    piece, head = row // LANE, row % LANE
    ch, cl = col // LANE, col % LANE
    eq = ((head == ch) & (cl == HD + 8 * piece + head)).astype(BF)
    ek = -((head == ch) & (cl == HD + 24 + 8 * piece + head)).astype(BF)
    off = (np.arange(NH * LANE) % LANE - HD - np.arange(NH * LANE) // LANE)[None, :]
    rowq = ((off >= 24) & (off < 48) & (off % 8 == 0)).astype(np.float32)
    rowk = ((off >= 0) & (off < 24) & (off % 8 == 0)).astype(np.float32)
    return eq, ek, rowq, rowk


def _in_proj(x, g1, w_in_t, b_pad, tri, eq, ek, rowq, rowk, dep):
    t = x.shape[0]
    tt = _tile(t)

    def body(x_ref, g_ref, w_ref, b_ref, tri_ref, eq_ref, ek_ref, rq_ref, rk_ref, dep_ref,
             qa_ref, ka_ref, v_ref, u_ref, fl_ref, carry):
        i = pl.program_id(0)

        @pl.when(i == 0)
        def _():
            carry[...] = jnp.zeros_like(carry)

        xv = x_ref[...]
        hn = (xv * _rstd(xv) * g_ref[...]).astype(BF)
        z = _mm_nt(hn, w_ref[...])
        fl = z[:, 3 * DA:3 * DA + LANE] + b_ref[...]
        lane = lax.broadcasted_iota(jnp.int32, fl.shape, 1)
        lf = jnp.where(lane < NH, jnp.minimum(fl, 0.0) - jnp.log(1.0 + jnp.exp(-jnp.abs(fl))), 0.0)
        c = carry[...] + _dot3(tri_ref[...], lf)
        carry[...] = carry[...] + jnp.sum(lf, axis=0, keepdims=True)
        caug = jnp.concatenate(_split3(c), axis=1)
        aug_q = jnp.dot(caug, eq_ref[...], preferred_element_type=F32) + rq_ref[...]
        aug_k = jnp.dot(caug, ek_ref[...], preferred_element_type=F32) + rk_ref[...]
        low = lax.broadcasted_iota(jnp.int32, (tt, LANE), 1) < HD
        for p in range(NH // 2):
            qp = z[:, LANE * p:LANE * (p + 1)] * (1.0 / math.sqrt(HD))
            kp = z[:, DA + LANE * p:DA + LANE * (p + 1)]
            for h, (qh, kh) in enumerate(((qp, kp), (pltpu.roll(qp, HD, 1), pltpu.roll(kp, HD, 1)))):
                lo_, hi_ = LANE * (2 * p + h), LANE * (2 * p + h + 1)
                qa_ref[:, lo_:hi_] = jnp.where(low, qh, aug_q[:, lo_:hi_]).astype(BF)
                ka_ref[:, lo_:hi_] = jnp.where(low, kh, aug_k[:, lo_:hi_]).astype(BF)
        v_ref[...] = z[:, 2 * DA:3 * DA].astype(BF)
        u_ref[...] = z[:, 3 * DA + LANE:]
        fl_ref[...] = fl

    return pl.pallas_call(
        body, name="in_proj", grid=(t // tt,),
        in_specs=[pl.BlockSpec((tt, D), lambda i: (i, 0)), _full((1, D)), _resident((ZW, D)), _full((1, LANE)),
                  _full((tt, tt)), _full((3 * LANE, NH * LANE)), _full((3 * LANE, NH * LANE)),
                  _full((1, NH * LANE)), _full((1, NH * LANE)), ANY],
        out_specs=[pl.BlockSpec((tt, NH * LANE), lambda i: (i, 0)), pl.BlockSpec((tt, NH * LANE), lambda i: (i, 0)),
                   pl.BlockSpec((tt, DA), lambda i: (i, 0)), pl.BlockSpec((tt, DP), lambda i: (i, 0)),
                   pl.BlockSpec((tt, LANE), lambda i: (i, 0))],
        out_shape=[jax.ShapeDtypeStruct((t, NH * LANE), BF), jax.ShapeDtypeStruct((t, NH * LANE), BF),
                   jax.ShapeDtypeStruct((t, DA), BF), jax.ShapeDtypeStruct((t, DP), F32),
                   jax.ShapeDtypeStruct((t, LANE), F32)],
        scratch_shapes=[pltpu.VMEM((1, LANE), F32)],
        compiler_params=_cp(),
    )(x, g1, w_in_t, b_pad, tri, eq, ek, rowq, rowk, dep)


def _attn_fwd(qa, ka, v):
    t = qa.shape[0]
    ta = _tile(t)
    n = t // ta

    def body(q_ref, k_ref, v_ref, a_ref, lse_ref, m_ref, l_ref, acc_ref):
        i = pl.program_id(1)
        m_ref[...] = jnp.full_like(m_ref, -1e30)
        l_ref[...] = jnp.zeros_like(l_ref)
        acc_ref[...] = jnp.zeros_like(acc_ref)
        qs = [q_ref[:, LANE * h:LANE * (h + 1)] for h in range(2)]
        reps = ta // LANE

        def tile(j, masked):
            rows = pl.ds(pl.multiple_of(j * ta, ta), ta)
            v2 = v_ref[rows, :]
            s = [_mm_nt(qs[h], k_ref[rows, LANE * h:LANE * (h + 1)]) for h in range(2)]
            if masked:
                keep = (lax.broadcasted_iota(jnp.int32, (ta, ta), 1) <= lax.broadcasted_iota(jnp.int32, (ta, ta), 0))
                s = [jnp.where(keep, sh, -1e30) for sh in s]
            m_old = [m_ref[h] for h in range(2)]
            m_new = [jnp.maximum(m_old[h], jnp.max(s[h], axis=1, keepdims=True)) for h in range(2)]
            pe = [jnp.exp(s[h] - jnp.tile(m_new[h], (1, reps))) for h in range(2)]
            alpha = [jnp.exp(m_old[h] - m_new[h]) for h in range(2)]
            pv = [jnp.dot(pe[h].astype(BF), v2, preferred_element_type=F32) for h in range(2)]
            for h in range(2):
                l_ref[h] = alpha[h] * l_ref[h] + jnp.sum(pe[h], axis=1, keepdims=True)
                acc_ref[h] = alpha[h] * acc_ref[h] + pv[h]
                m_ref[h] = m_new[h]

        def step(j, carry):
            tile(j, False)
            return carry

        lax.fori_loop(0, i, step, 0)
        tile(i, True)
        low = lax.broadcasted_iota(jnp.int32, (ta, LANE), 1) < HD
        a_ref[...] = jnp.where(low, acc_ref[0] / l_ref[0], acc_ref[1] / l_ref[1])
        lse_ref[...] = jnp.where(low, m_ref[0] + jnp.log(l_ref[0]), m_ref[1] + jnp.log(l_ref[1]))

    return pl.pallas_call(
        body, name="attn_fwd", grid=(NH // 2, n),
        in_specs=[pl.BlockSpec((ta, 2 * LANE), lambda p, i: (i, p)),
                  pl.BlockSpec((t, 2 * LANE), lambda p, i: (0, p)),
                  pl.BlockSpec((t, LANE), lambda p, i: (0, p))],
        out_specs=[pl.BlockSpec((ta, LANE), lambda p, i: (i, p)), pl.BlockSpec((ta, LANE), lambda p, i: (i, p))],
        out_shape=[jax.ShapeDtypeStruct((t, DA), F32), jax.ShapeDtypeStruct((t, DA), F32)],
        scratch_shapes=[pltpu.VMEM((2, ta, LANE), F32), pltpu.VMEM((2, ta, LANE), F32), pltpu.VMEM((2, ta, LANE), F32)],
        compiler_params=_cp(),
    )(qa, ka, v)


def _mix_out(a, u, x, w_pool, pool_scale, g_attn, g_pool, w_out, g_post, dep):
    t = a.shape[0]
    tt = _tile(t)
    hb = tt // HALO

    def body(a_ref, u_ref, up_ref, x_ref, wp_ref, ps_ref, ga_ref, gp_ref, wo_ref, go_ref, dep_ref,
             yb_ref, m_ref, o_ref, h1_ref):
        i = pl.program_id(0)
        prev = up_ref[...] * jnp.where(i > 0, 1.0, 0.0)
        tok = i * tt + lax.broadcasted_iota(jnp.int32, (tt, PC), 0)
        ms = []
        for g, w in enumerate(WINS):
            ug = u_ref[:, PC * g:PC * (g + 1)]
            ext = jnp.concatenate([prev[:, PC * g:PC * (g + 1)], ug], axis=0)
            cnt = jnp.minimum(tok + 1, w).astype(F32)
            y = (_window_sum(ext, w, True)[HALO:] / cnt - ug).astype(BF)
            yb_ref[:, PC * g:PC * (g + 1)] = y
            ms.append(_mm(y, wp_ref[g]) * ps_ref[:, PC * g:PC * (g + 1)])
        m = jnp.concatenate(ms, axis=1)
        m_ref[...] = m
        av = a_ref[...]
        mix = jnp.concatenate([av * _rstd(av) * ga_ref[...], m * _rstd(m) * gp_ref[...]], axis=1)
        o = _mm(mix, wo_ref[...])
        o_ref[...] = o
        h1_ref[...] = x_ref[...] + o * _rstd(o) * go_ref[...]

    return pl.pallas_call(
        body, name="mix_out", grid=(t // tt,),
        in_specs=[pl.BlockSpec((tt, DA), lambda i: (i, 0)), pl.BlockSpec((tt, DP), lambda i: (i, 0)),
                  pl.BlockSpec((HALO, DP), lambda i: (jnp.maximum(i * hb - 1, 0), 0)),
                  pl.BlockSpec((tt, D), lambda i: (i, 0)),
                  _full((len(WINS), PC, PC)), _full((1, DP)), _full((1, DA)), _full((1, DP)),
                  _resident((D, D)), _full((1, D)), ANY],
        out_specs=[pl.BlockSpec((tt, DP), lambda i: (i, 0)), pl.BlockSpec((tt, DP), lambda i: (i, 0)),
                   pl.BlockSpec((tt, D), lambda i: (i, 0)), pl.BlockSpec((tt, D), lambda i: (i, 0))],
        out_shape=[jax.ShapeDtypeStruct((t, DP), BF), jax.ShapeDtypeStruct((t, DP), F32),
                   jax.ShapeDtypeStruct((t, D), F32), jax.ShapeDtypeStruct((t, D), F32)],
        compiler_params=_cp(),
    )(a, u, u, x, w_pool, pool_scale, g_attn, g_pool, w_out, g_post, dep)


def _ffn_fwd(h1, g_pre, stacks2, g_post, p, w_ple, g_ple, w_pg, tgt):
    t = h1.shape[0]
    tt = 256 if t % 256 == 0 else t

    def body(h1_ref, gpre_ref, wg_ref, wu_ref, wd_ref, gpost_ref, p_ref, wple_ref, gple_ref, wpg_ref, tgt_ref,
             hn_ref, gate_ref, up_ref, dff_ref, dh2_ref, loss_ref, dwpg_ref, dwple_ref, dgple_ref, dgpost_ref):
        i = pl.program_id(0)

        @pl.when(i == 0)
        def _():
            loss_ref[...] = jnp.zeros_like(loss_ref)
            dwpg_ref[...] = jnp.zeros_like(dwpg_ref)
            dwple_ref[...] = jnp.zeros_like(dwple_ref)
            dgple_ref[...] = jnp.zeros_like(dgple_ref)
            dgpost_ref[...] = jnp.zeros_like(dgpost_ref)

        h1v = h1_ref[...]
        hn = (h1v * _rstd(h1v) * gpre_ref[...]).astype(BF)
        hn_ref[...] = hn
        gate = _mm_nt(hn, wg_ref[...].reshape(DFF, D))
        up = _mm_nt(hn, wu_ref[...].reshape(DFF, D))
        for k in range(DFF // FF_CH):
            gate_ref[k] = gate[:, FF_CH * k:FF_CH * (k + 1)].astype(BF)
            up_ref[k] = up[:, FF_CH * k:FF_CH * (k + 1)].astype(BF)
        ff = _mm(gate * _sigmoid(gate) * up, wd_ref[...].reshape(DFF, D))
        rff = _rstd(ff)
        ffh = ff * rff
        gpost = gpost_ref[...]
        h2 = h1v + ffh * gpost
        pv = p_ref[...]
        pe = _mm(pv, wple_ref[...])
        rpe = _rstd(pe)
        peh = pe * rpe
        gple = gple_ref[...]
        e = peh * gple
        sig = _sigmoid(_mm(h2, wpg_ref[...]))
        dv = h2 + sig * e - tgt_ref[...]
        sq = jnp.sum(jnp.sum(dv * dv, axis=1, keepdims=True), axis=0, keepdims=True)
        loss_ref[...] = loss_ref[...] + sq
        dy = dv * (1.0 / D)
        d_e = dy * sig
        d_gl = dy * e * sig * (1.0 - sig)
        dh2 = dy + _mm_nt(d_gl, wpg_ref[...])
        dh2_ref[...] = dh2
        dwpg_ref[...] = dwpg_ref[...] + _mm_tn(h2, d_gl)
        dgple_ref[...] = dgple_ref[...] + jnp.sum(d_e * peh, axis=0, keepdims=True)
        dpeh = d_e * gple
        d_pe = rpe * (dpeh - peh * jnp.mean(dpeh * peh, axis=-1, keepdims=True))
        dwple_ref[...] = dwple_ref[...] + _mm_tn(pv, d_pe)
        dgpost_ref[...] = dgpost_ref[...] + jnp.sum(dh2 * ffh, axis=0, keepdims=True)
        dffh = dh2 * gpost
        dff_ref[...] = (rff * (dffh - ffh * jnp.mean(dffh * ffh, axis=-1, keepdims=True))).astype(BF)

    row = lambda w: pl.BlockSpec((tt, w), lambda i: (i, 0))
    chunked = pl.BlockSpec((DFF // FF_CH, tt, FF_CH), lambda i: (0, i, 0))
    shard_rows = lambda k: pl.BlockSpec((NSHARD, FF_SH, D), lambda i: (0, k, 0), pipeline_mode=pl.Buffered(1))
    return pl.pallas_call(
        body, name="ffn_fwd", grid=(t // tt,),
        in_specs=[row(D), _full((1, D)), shard_rows(0), shard_rows(1), shard_rows(2), _full((1, D)),
                  row(DPLE), _resident((DPLE, D)), _full((1, D)), _resident((D, D)), row(D)],
        out_specs=[row(D), chunked, chunked, row(D), row(D), _full((8, LANE)), _full((D, D)), _full((DPLE, D)),
                   _full((1, D)), _full((1, D))],
        out_shape=[jax.ShapeDtypeStruct((t, D), BF), jax.ShapeDtypeStruct((DFF // FF_CH, t, FF_CH), BF),
                   jax.ShapeDtypeStruct((DFF // FF_CH, t, FF_CH), BF),
                   jax.ShapeDtypeStruct((t, D), BF), jax.ShapeDtypeStruct((t, D), F32), jax.ShapeDtypeStruct((8, LANE), F32),
                   jax.ShapeDtypeStruct((D, D), F32), jax.ShapeDtypeStruct((DPLE, D), F32),
                   jax.ShapeDtypeStruct((1, D), F32), jax.ShapeDtypeStruct((1, D), F32)],
        compiler_params=_cp(),
    )(h1, g_pre, stacks2, stacks2, stacks2, g_post, p, w_ple, g_ple, w_pg, tgt)


FF_CH = 256


def _ffn_bwd(hn2, gate, up, dff, wgu, wd):
    t = hn2.shape[0]
    tt = 1024 if t % 1024 == 0 else _tile(t)
    nt = t // tt
    ch = FF_CH
    nc = DFF // ch

    def body(hn_ref, gate_ref, up_ref, dff_ref, wgu_ref, wd_ref,
             dwg_ref, dwu_ref, dwd_ref, dhn_ref, acc, sem):
        j, i = pl.program_id(0), pl.program_id(1)

        @pl.when(j == 0)
        def _():
            acc[pl.ds(pl.multiple_of(i * tt, tt), tt), :] = jnp.zeros((tt, D), F32)

        @pl.when(i == 0)
        def _():
            dwg_ref[...] = jnp.zeros_like(dwg_ref)
            dwu_ref[...] = jnp.zeros_like(dwu_ref)
            dwd_ref[...] = jnp.zeros_like(dwd_ref)

        parts = 4 if tt % 1024 == 0 else 2
        half = tt // parts
        acts, dgus = [], []
        for hh in range(parts):
            r = slice(hh * half, (hh + 1) * half)
            gate_v = gate_ref[0, r, :].astype(F32)
            up_v = up_ref[0, r, :].astype(F32)
            sg = _sigmoid(gate_v)
            silu = gate_v * sg
            d_act = _mm_nt(dff_ref[r, :], wd_ref[...])
            d_up = (d_act * silu).astype(BF)
            d_gate = (d_act * up_v * (sg * (1.0 + gate_v * (1.0 - sg)))).astype(BF)
            dgu = jnp.concatenate([d_gate, d_up], axis=1)
            rows = pl.ds(pl.multiple_of(i * tt + hh * half, half), half)
            acc[rows, :] = acc[rows, :] + jnp.dot(dgu, wgu_ref[0], preferred_element_type=F32)
            acts.append((silu * up_v).astype(BF))
            dgus.append(dgu)
        dwd_ref[...] = dwd_ref[...] + _mm_tn(jnp.concatenate(acts, axis=0), dff_ref[...])
        dwgu = _mm_tn(jnp.concatenate(dgus, axis=0), hn_ref[...])
        dwg_ref[...] = dwg_ref[...] + dwgu[:ch]
        dwu_ref[...] = dwu_ref[...] + dwgu[ch:]

        @pl.when((j == nc - 1) & (i == nt - 1))
        def _():
            cp = pltpu.make_async_copy(acc, dhn_ref, sem)
            cp.start()
            cp.wait()

    tok = lambda w: pl.BlockSpec((tt, w), lambda j, i: (i, 0))
    chunk = pl.BlockSpec((ch, D), lambda j, i: (j, 0))
    pair = pl.BlockSpec((1, 2 * ch, D), lambda j, i: (j, 0, 0))
    return pl.pallas_call(
        body, name="ffn_bwd", grid=(nc, nt),
        in_specs=[tok(D), pl.BlockSpec((1, tt, ch), lambda j, i: (j, i, 0)), pl.BlockSpec((1, tt, ch), lambda j, i: (j, i, 0)),
                  tok(D), pair, chunk],
        out_specs=[chunk, chunk, chunk, pl.BlockSpec(memory_space=pl.ANY)],
        out_shape=[jax.ShapeDtypeStruct((DFF, D), F32), jax.ShapeDtypeStruct((DFF, D), F32),
                   jax.ShapeDtypeStruct((DFF, D), F32), jax.ShapeDtypeStruct((t, D), F32)],
        scratch_shapes=[pltpu.VMEM((t, D), F32), pltpu.SemaphoreType.DMA],
        compiler_params=_cp(),
    )(hn2, gate, up, dff, wgu, wd)


def _mix_bwd(d_hn2, dh2, h1, o, a, m, yb, g_ffn_pre, g_post, g_attn, g_pool, w_out, w_pool, pool_scale, dep):
    t = a.shape[0]
    tt = 256 if t % 256 == 0 else t

    def body(dhn_ref, dh2_ref, h1_ref, o_ref, a_ref, m_ref, yb_ref, gfp_ref, go_ref, ga_ref, gp_ref, wo_ref, wp_ref,
             ps_ref, dep_ref, dh1_ref, da_ref, dyc_ref, dgfp_ref, dgo_ref, dga_ref, dgp_ref, dps_ref, dwp_ref, dwo_ref):
        i = pl.program_id(0)

        @pl.when(i == 0)
        def _():
            for r in (dgfp_ref, dgo_ref, dga_ref, dgp_ref, dps_ref, dwp_ref, dwo_ref):
                r[...] = jnp.zeros_like(r)

        d1, dg = _rms_bwd(dhn_ref[...], h1_ref[...], gfp_ref[...])
        dgfp_ref[...] = dgfp_ref[...] + dg
        dh1 = dh2_ref[...] + d1
        dh1_ref[...] = dh1
        d_o, dg = _rms_bwd(dh1, o_ref[...], go_ref[...])
        dgo_ref[...] = dgo_ref[...] + dg
        d_mix = _mm_nt(d_o, wo_ref[...])
        av, mv = a_ref[...], m_ref[...]
        mix = jnp.concatenate([av * _rstd(av) * ga_ref[...], mv * _rstd(mv) * gp_ref[...]], axis=1)
        dwo_ref[...] = dwo_ref[...] + _mm_tn(mix, d_o)
        d_a, dg = _rms_bwd(d_mix[:, :DA], av, ga_ref[...])
        dga_ref[...] = dga_ref[...] + dg
        da_ref[...] = d_a
        d_m, dg = _rms_bwd(d_mix[:, DA:], mv, gp_ref[...])
        dgp_ref[...] = dgp_ref[...] + dg
        tok = i * tt + lax.broadcasted_iota(jnp.int32, (tt, PC), 0)
        dps = []
        for g, w in enumerate(WINS):
            sl = slice(PC * g, PC * (g + 1))
            ybg = yb_ref[:, sl]
            wpg = wp_ref[g].astype(BF)
            mlin = jnp.dot(ybg, wpg, preferred_element_type=F32)
            dmg = d_m[:, sl]
            dps.append(jnp.sum(dmg * mlin, axis=0, keepdims=True))
            dml = (dmg * ps_ref[:, sl]).astype(BF)
            dwp_ref[g] = dwp_ref[g] + _mm_tn(ybg, dml)
            dyc_ref[:, sl] = _mm_nt(dml, wpg) / jnp.minimum(tok + 1, w).astype(F32)
        dps_ref[...] = dps_ref[...] + jnp.concatenate(dps, axis=1)

    row = lambda w: pl.BlockSpec((tt, w), lambda i: (i, 0))
    return pl.pallas_call(
        body, name="mix_bwd", grid=(t // tt,),
        in_specs=[row(D), row(D), row(D), row(D), row(DA), row(DP), row(DP), _full((1, D)), _full((1, D)),
                  _full((1, DA)), _full((1, DP)), _resident((D, D)), _full((len(WINS), PC, PC)), _full((1, DP)), ANY],
        out_specs=[row(D), row(DA), row(DP), _full((1, D)), _full((1, D)), _full((1, DA)), _full((1, DP)),
                   _full((1, DP)), _full((len(WINS), PC, PC)), _full((D, D))],
        out_shape=[jax.ShapeDtypeStruct((t, D), F32), jax.ShapeDtypeStruct((t, DA), F32), jax.ShapeDtypeStruct((t, DP), F32),
                   jax.ShapeDtypeStruct((1, D), F32), jax.ShapeDtypeStruct((1, D), F32), jax.ShapeDtypeStruct((1, DA), F32),
                   jax.ShapeDtypeStruct((1, DP), F32), jax.ShapeDtypeStruct((1, DP), F32),
                   jax.ShapeDtypeStruct((len(WINS), PC, PC), F32), jax.ShapeDtypeStruct((D, D), F32)],
        compiler_params=_cp(),
    )(d_hn2, dh2, h1, o, a, m, yb, g_ffn_pre, g_post, g_attn, g_pool, w_out, w_pool, pool_scale, dep)


def _attn_bwd(qa, ka, v, a, d_a, lse, dep):
    t = qa.shape[0]
    ta = _tile(t)
    n = t // ta

    def body(q_ref, k_ref, v_ref, o_ref, do_ref, lse_ref, dep_ref, dq_ref, dk_ref, dv_ref):
        j = pl.program_id(1)

        @pl.when(j == 0)
        def _():
            dq_ref[...] = jnp.zeros_like(dq_ref)

        dk_ref[...] = jnp.zeros_like(dk_ref)
        dv_ref[...] = jnp.zeros_like(dv_ref)
        ks = [k_ref[:, LANE * h:LANE * (h + 1)] for h in range(2)]
        v2 = v_ref[...]
        lane = lax.broadcasted_iota(jnp.int32, (ta, LANE), 1)
        mine = [lane < HD, lane >= HD]

        def tile(i, masked):
            rows = pl.ds(pl.multiple_of(i * ta, ta), ta)
            do2 = do_ref[rows, :]
            prod = do2 * o_ref[rows, :]
            lse2 = lse_ref[rows, :]
            do2b = do2.astype(BF)
            qh = [q_ref[rows, LANE * h:LANE * (h + 1)] for h in range(2)]
            s = [_mm_nt(qh[h], ks[h]) for h in range(2)]
            dp = [_mm_nt(jnp.where(mine[h], do2, 0.0), v2) for h in range(2)]
            delta = [jnp.sum(jnp.where(mine[h], prod, 0.0), axis=1, keepdims=True) for h in range(2)]
            lse_h = [jnp.sum(jnp.where(lane == HD * h, lse2, 0.0), axis=1, keepdims=True) for h in range(2)]
            pr = [jnp.exp(s[h] - lse_h[h]) for h in range(2)]
            if masked:
                keep = (lax.broadcasted_iota(jnp.int32, (ta, ta), 1) <= lax.broadcasted_iota(jnp.int32, (ta, ta), 0))
                pr = [jnp.where(keep, ph, 0.0) for ph in pr]
            ds = [(pr[h] * (dp[h] - delta[h])).astype(BF) for h in range(2)]
            dv_ref[...] = dv_ref[...] + jnp.where(mine[0], _mm_tn(pr[0], do2b), _mm_tn(pr[1], do2b))
            for h in range(2):
                sl = slice(LANE * h, LANE * (h + 1))
                dk_ref[:, sl] = dk_ref[:, sl] + _mm_tn(ds[h], qh[h])
                dq_ref[0, rows, sl] = dq_ref[0, rows, sl] + jnp.dot(ds[h], ks[h], preferred_element_type=F32)

        def step(i, carry):
            tile(i, False)
            return carry

        tile(j, True)
        lax.fori_loop(j + 1, n, step, 0)

    qrow = lambda w: pl.BlockSpec((t, w), lambda p, j: (0, p))
    krow = lambda w: pl.BlockSpec((ta, w), lambda p, j: (j, p))
    return pl.pallas_call(
        body, name="attn_bwd", grid=(NH // 2, n),
        in_specs=[qrow(2 * LANE), krow(2 * LANE), krow(LANE), qrow(LANE), qrow(LANE), qrow(LANE), ANY],
        out_specs=[pl.BlockSpec((1, t, 2 * LANE), lambda p, j: (p, 0, 0)), krow(2 * LANE), krow(LANE)],
        out_shape=[jax.ShapeDtypeStruct((NH // 2, t, 2 * LANE), F32), jax.ShapeDtypeStruct((t, NH * LANE), F32),
                   jax.ShapeDtypeStruct((t, DA), F32)],
        compiler_params=_cp(),
    )(qa, ka, v, a, d_a, lse, dep)


def _in_bwd(dqa, dka, dv, dyc, fl, x, dh1, g1, w_in_t, tri_u, dep):
    t = x.shape[0]
    tt = _tile(t)
    nt = t // tt
    hb = tt // HALO
    rev = lambda s: nt - 1 - s

    def body(dqa_ref, dka_ref, dv_ref, dyc_ref, dyn_ref, fl_ref, x_ref, dh1_ref, g_ref, w_ref, tri_ref,
             dep_ref, dx_ref, dw_ref, dg_ref, db_ref, carry, acc, sem):
        s = pl.program_id(0)
        i = nt - 1 - s

        @pl.when(s == 0)
        def _():
            carry[...] = jnp.zeros_like(carry)
            acc[...] = jnp.zeros_like(acc)
            dg_ref[...] = jnp.zeros_like(dg_ref)
            db_ref[...] = jnp.zeros_like(db_ref)

        dq_cat = jnp.concatenate([dqa_ref[p] for p in range(NH // 2)], axis=1)
        dk_cat = dka_ref[...]
        off = lax.broadcasted_iota(jnp.int32, (1, NH * LANE), 1)
        off = off % LANE - HD - off // LANE

        def picked(cat, lane_off):
            kept = jnp.where(off == lane_off, cat, 0.0)
            return functools.reduce(lambda a, b: a + b, [kept[:, LANE * h:LANE * (h + 1)] for h in range(NH)])

        dc = pltpu.roll(picked(dq_cat, 0), LANE - HD, 1) - pltpu.roll(picked(dk_cat, 24), LANE - HD - 24, 1)
        dlf = carry[...] + _dot3(tri_ref[...], dc)
        carry[...] = carry[...] + jnp.sum(dc, axis=0, keepdims=True)
        flv = fl_ref[...]
        lane = lax.broadcasted_iota(jnp.int32, flv.shape, 1)
        d_fl = jnp.where(lane < NH, dlf / (1.0 + jnp.exp(flv)), 0.0)
        db_ref[...] = db_ref[...] + jnp.sum(d_fl, axis=0, keepdims=True)
        low = lax.broadcasted_iota(jnp.int32, (tt, LANE), 1) < HD
        dqs, dks = [], []
        for p in range(NH // 2):
            b0, b1 = slice(2 * LANE * p, 2 * LANE * p + LANE), slice(2 * LANE * p + LANE, 2 * LANE * (p + 1))
            dqs.append(jnp.where(low, dq_cat[:, b0], pltpu.roll(dq_cat[:, b1], HD, 1)) * (1.0 / math.sqrt(HD)))
            dks.append(jnp.where(low, dk_cat[:, b0], pltpu.roll(dk_cat[:, b1], HD, 1)))
        nxt = dyn_ref[...] * jnp.where(i < nt - 1, 1.0, 0.0)
        tok = i * tt + lax.broadcasted_iota(jnp.int32, (tt, PC), 0)
        dus = []
        for g, w in enumerate(WINS):
            sl = slice(PC * g, PC * (g + 1))
            dycg = dyc_ref[:, sl]
            ext = jnp.concatenate([dycg, nxt[:, sl]], axis=0)
            dus.append(_window_sum(ext, w, False)[:tt] - dycg * jnp.minimum(tok + 1, w).astype(F32))
        d_z = jnp.concatenate(dqs + dks + [dv_ref[...], d_fl] + dus, axis=1).astype(BF)
        xv = x_ref[...]
        gv = g_ref[...]
        hn = (xv * _rstd(xv) * gv).astype(BF)
        d_hn = jnp.dot(d_z, w_ref[...], preferred_element_type=F32)
        acc[...] = acc[...] + _mm_tn(d_z, hn)
        d1, dg = _rms_bwd(d_hn, xv, gv)
        dg_ref[...] = dg_ref[...] + dg
        dx_ref[...] = dh1_ref[...] + d1

        @pl.when(s == nt - 1)
        def _():
            cp = pltpu.make_async_copy(acc, dw_ref, sem)
            cp.start()
            cp.wait()

    row = lambda w: pl.BlockSpec((tt, w), lambda s: (rev(s), 0))
    return pl.pallas_call(
        body, name="in_bwd", grid=(nt,),
        in_specs=[pl.BlockSpec((NH // 2, tt, 2 * LANE), lambda s: (0, rev(s), 0)), row(NH * LANE), row(DA), row(DP),
                  pl.BlockSpec((HALO, DP), lambda s: (jnp.minimum((rev(s) + 1) * hb, nt * hb - 1), 0)),
                  row(LANE), row(D), row(D), _full((1, D)), _resident((ZW, D)), _full((tt, tt)),
                  ANY],
        out_specs=[row(D), pl.BlockSpec(memory_space=pl.ANY), _full((1, D)), _full((1, LANE))],
        out_shape=[jax.ShapeDtypeStruct((t, D), F32), jax.ShapeDtypeStruct((ZW, D), F32),
                   jax.ShapeDtypeStruct((1, D), F32), jax.ShapeDtypeStruct((1, LANE), F32)],
        scratch_shapes=[pltpu.VMEM((1, LANE), F32), pltpu.VMEM((ZW, D), F32), pltpu.SemaphoreType.DMA],
        compiler_params=_cp(),
    )(dqa, dka, dv, dyc, dyc, fl, x, dh1, g1, w_in_t, tri_u, dep)


class _NoComm:
    def __init__(self, w2):
        self.w2 = w2
        self.dep = jnp.zeros((8, LANE), F32)

    def after_attention(self, after):
        return self.dep

    def weights2(self, after):
        return _stack2_full(*self.w2)

    def after_ffn(self, grads2):
        self.grads2 = grads2
        return self.dep

    def after_mix(self, after, early):
        self.early = early
        return self.dep

    def after_attn(self, after):
        return self.dep


def _local_step(x, p, tgt, sm, w1, comm):
    w_in_t, w_out = w1
    tt = _tile(x.shape[0])
    eq, ek, rowq, rowk = _aug_consts()
    b_pad = jnp.pad(sm["b_forget"], ((0, 0), (0, LANE - NH)))
    qa, ka, v, u, fl = _in_proj(x, sm["g_mix_pre"], w_in_t, b_pad, _tri(tt, False), eq, ek, rowq, rowk, comm.dep)
    a, lse = _attn_fwd(qa, ka, v)
    yb, m, o, h1 = _mix_out(a, u, x, sm["w_pool"], sm["pool_scale"], sm["g_attn_grp"],
                            sm["g_pool_grp"], w_out, sm["g_mix_post"], comm.after_attention(a))
    stacks2 = comm.weights2(h1)
    wg_t, wu_t, wd, w_ple, w_pg = _unstack2_full(stacks2)
    hn2, gate, up, dff, dh2, loss, dwpg, dwple, dgple, dgfpost = _ffn_fwd(
        h1, sm["g_ffn_pre"], stacks2, sm["g_ffn_post"], p, w_ple, sm["g_ple"], w_pg, tgt)
    chunks = lambda a: a.reshape(DFF // FF_CH, FF_CH, D)
    dwg_t, dwu_t, dwd, d_hn2 = _ffn_bwd(hn2, gate, up, dff, jnp.concatenate([chunks(wg_t), chunks(wu_t)], axis=1), wd)
    dep = comm.after_ffn((dwg_t, dwu_t, dwd, dwple, dwpg))
    dh1, d_a, dyc, dgfpre, dgpost, dgattn, dgpool, dps, dwpool, dwout = _mix_bwd(
        d_hn2, dh2, h1, o, a, m, yb, sm["g_ffn_pre"], sm["g_mix_post"], sm["g_attn_grp"], sm["g_pool_grp"],
        w_out, sm["w_pool"], sm["pool_scale"], dep)
    early = dict(loss=loss[0:1, 0:1], g_attn_grp=dgattn, g_pool_grp=dgpool, w_pool=dwpool, pool_scale=dps,
                 g_mix_post=dgpost, g_ffn_pre=dgfpre, g_ffn_post=dgfpost, g_ple=dgple)
    dqa, dka, dvv = _attn_bwd(qa, ka, v, a, d_a, lse, comm.after_mix(dh1, early))
    dx, dwin_t, dg1, dbf = _in_bwd(dqa, dka, dvv, dyc, fl, x, dh1, sm["g_mix_pre"], w_in_t, _tri(tt, True),
                                   comm.after_attn(dvv))
    return dx, (dwin_t, dwout), dict(g_mix_pre=dg1, b_forget=dbf[:, :NH])


def _place():
    x, y, c = lax.axis_index("x"), lax.axis_index("y"), lax.axis_index("c")
    return x, y, c, [(1 - x, y), (x, 1 - y), (1 - x, 1 - y)]


def _rows(c, h):
    return pl.ds(pl.multiple_of(c * h, 16), h)


def _plan_gather(h):
    def plan(src, land):
        x, y, c, chips = _place()
        return [(src.at[_rows(c, h), :], land.at[2 * x + y, _rows(c, h), :], (cx, cy, c),
                 land.at[2 * cx + cy, _rows(c, h), :]) for cx, cy in chips]
    return plan


def _plan_forward(h):
    def plan(land_in, own, land):
        x, y, c, chips = _place()
        sib, me = (x, y, 1 - c), 2 * x + y
        return ([(land_in.at[2 * cx + cy, _rows(c, h), :], land.at[2 * cx + cy, _rows(c, h), :], sib,
                  land.at[2 * cx + cy, _rows(1 - c, h), :]) for cx, cy in chips]
                + [(own, land.at[me], sib, land.at[me])])
    return plan


def _plan_swap_halves(h):
    def plan(buf_in, buf):
        x, y, c, _ = _place()
        return [(buf_in.at[_rows(c, h), :], buf.at[_rows(c, h), :], (x, y, 1 - c), buf.at[_rows(1 - c, h), :])]
    return plan


def _plan_pair_rows(h):
    def plan(src, land):
        x, y, c, _ = _place()
        return [(src.at[:, _rows(1 - c, h), :], land, (x, y, 1 - c), land)]
    return plan


def _plan_scatter(src, land):
    x, y, c, chips = _place()
    return [(src.at[2 * cx + cy], land.at[k], (cx, cy, c), land.at[k]) for k, (cx, cy) in enumerate(chips)]


def _plan_scatter8(h):
    def plan(src, land):
        x, y, c, chips = _place()
        copies = [(src.at[2 * x + y, _rows(1 - c, h), :], land.at[0], (x, y, 1 - c), land.at[0])]
        for k, (cx, cy) in enumerate(chips):
            for d, other in enumerate((c, 1 - c)):
                copies.append((src.at[2 * cx + cy, _rows(other, h), :], land.at[1 + 2 * k + c], (cx, cy, other),
                               land.at[1 + 2 * k + other]))
        return copies
    return plan


def _plan_all(src, land):
    x, y, c, _ = _place()
    copies = []
    for r in range(1, 8):
        px, py, pc = (1 - a if b else a for a, b in zip((x, y, c), (r >> 2 & 1, r >> 1 & 1, r & 1)))
        copies.append((src, land.at[4 * x + 2 * y + c], (px, py, pc), land.at[4 * px + 2 * py + pc]))
    return copies


def _remote(src, dst, send_sems, recv_sems, k, peer):
    return pltpu.make_async_remote_copy(src_ref=src, dst_ref=dst, send_sem=send_sems.at[k], recv_sem=recv_sems.at[k],
                                        device_id=peer, device_id_type=MESH)


def _exchange(name, n, src, land, plan):
    def body(src_ref, land_ref, send_sems, recv_sems):
        copies = plan(src_ref, land_ref)
        for k, (s, d, peer, _) in enumerate(copies):
            _remote(s, d, send_sems, recv_sems, k, peer).start()
        for k, (s, _, peer, mine) in enumerate(copies):
            _remote(s, mine, send_sems, recv_sems, k, peer).wait_recv()
        for k, (s, d, peer, _) in enumerate(copies):
            _remote(s, d, send_sems, recv_sems, k, peer).wait_send()

    return pl.pallas_call(
        body, name=name, in_specs=[ANY], out_specs=ANY, out_shape=land,
        scratch_shapes=[pltpu.SemaphoreType.DMA((n,)), pltpu.SemaphoreType.DMA((n,))],
    )(src)


def _exchange_inplace(name, n, buf, extra, plan):
    def body(*refs):
        ins, buf_ref, send_sems, recv_sems = refs[:1 + len(extra)], refs[1 + len(extra)], refs[-2], refs[-1]
        copies = plan(*ins, buf_ref)
        for k, (s, d, peer, _) in enumerate(copies):
            _remote(s, d, send_sems, recv_sems, k, peer).start()
        for k, (s, _, peer, mine) in enumerate(copies):
            _remote(s, mine, send_sems, recv_sems, k, peer).wait_recv()
        for k, (s, d, peer, _) in enumerate(copies):
            _remote(s, d, send_sems, recv_sems, k, peer).wait_send()

    return pl.pallas_call(
        body, name=name, in_specs=[ANY] * (1 + len(extra)), out_specs=ANY, out_shape=_sds(buf.shape, buf.dtype),
        input_output_aliases={0: 0},
        scratch_shapes=[pltpu.SemaphoreType.DMA((n,)), pltpu.SemaphoreType.DMA((n,))],
    )(buf, *extra)


HBM = pl.BlockSpec(memory_space=pltpu.HBM)
SEM = pl.BlockSpec(memory_space=pltpu.SEMAPHORE)
EFFECT = pltpu.SideEffectType.DATAFLOW_SIDE_EFFECTING


def _exchange_start(name, n, src, land, plan):
    def body(src_ref, land_ref, send_sems, recv_sems, src_thru, land_thru, token):
        for k, (s, d, peer, _) in enumerate(plan(src_ref, land_ref)):
            _remote(s, d, send_sems, recv_sems, k, peer).start()
        token[...] = jnp.zeros_like(token)

    return pl.pallas_call(
        body, name=name,
        out_shape=(pltpu.SemaphoreType.DMA((n,)), pltpu.SemaphoreType.DMA((n,)), pltpu.HBM(src.shape, src.dtype),
                   pltpu.HBM(land.shape, land.dtype), jax.ShapeDtypeStruct((8, LANE), F32)),
        in_specs=(HBM, HBM), out_specs=(SEM, SEM, HBM, HBM, pl.BlockSpec(memory_space=pltpu.VMEM)),
        input_output_aliases={0: 2, 1: 3},
        compiler_params=pltpu.CompilerParams(has_side_effects=EFFECT),
    )(pltpu.with_memory_space_constraint(src, pltpu.HBM), pltpu.with_memory_space_constraint(land, pltpu.HBM))


def _exchange_wait(name, started, plan, after):
    send_sems, recv_sems, src, land, _ = started

    def body(src_ref, land_ref, send_sems, recv_sems, after_ref, src_out, land_out):
        for k, (s, _, peer, mine) in enumerate(plan(src_ref, land_ref)):
            cp = _remote(s, mine, send_sems, recv_sems, k, peer)
            cp.wait_send()
            cp.wait_recv()

    return pl.pallas_call(
        body, name=name, out_shape=(pltpu.HBM(src.shape, src.dtype), pltpu.HBM(land.shape, land.dtype)),
        in_specs=(HBM, HBM, SEM, SEM, ANY), out_specs=(HBM, HBM), input_output_aliases={0: 0, 1: 1},
        compiler_params=pltpu.CompilerParams(has_side_effects=EFFECT),
    )(src, land, send_sems, recv_sems, after)


def _pair_sum(name, cidx, g, recv, br):
    h = recv.shape[1]
    nb = h // br

    def body(c_ref, g_ref, r_ref, out_ref):
        out_ref[...] = (g_ref[...] + r_ref[...]).astype(BF)

    return pl.pallas_call(
        body, name=name,
        grid_spec=pltpu.PrefetchScalarGridSpec(
            num_scalar_prefetch=1, grid=(NSHARD, nb),
            in_specs=[pl.BlockSpec((1, br, D), lambda s, i, c: (s, c[0] * nb + i, 0)),
                      pl.BlockSpec((1, br, D), lambda s, i, c: (s, i, 0))],
            out_specs=pl.BlockSpec((1, br, D), lambda s, i, c: (s, i, 0))),
        out_shape=jax.ShapeDtypeStruct((NSHARD, h, D), BF),
    )(cidx, g, recv)


def _chip_sum(name, place, pb, y, br):
    h = y.shape[1]
    nb = h // br

    def body(pl_ref, p_ref, y_ref, out_ref):
        acc = p_ref[0].astype(F32)
        for k in range(NSHARD - 1):
            acc = acc + y_ref[k].astype(F32)
        out_ref[...] = acc

    return pl.pallas_call(
        body, name=name,
        grid_spec=pltpu.PrefetchScalarGridSpec(
            num_scalar_prefetch=1, grid=(nb,),
            in_specs=[pl.BlockSpec((1, br, D), lambda i, s: (s[0], i, 0)),
                      pl.BlockSpec((NSHARD - 1, br, D), lambda i, s: (0, i, 0))],
            out_specs=pl.BlockSpec((br, D), lambda i, s: (s[1] * nb + i, 0))),
        out_shape=jax.ShapeDtypeStruct((2 * h, D), F32),
    )(place, pb, y)


def _sum8(name, place, g, land, br):
    h = land.shape[1]
    nb = h // br

    def body(pl_ref, g_ref, y_ref, out_ref):
        acc = g_ref[0].astype(F32)
        for k in range(land.shape[0]):
            acc = acc + y_ref[k].astype(F32)
        out_ref[...] = acc

    return pl.pallas_call(
        body, name=name,
        grid_spec=pltpu.PrefetchScalarGridSpec(
            num_scalar_prefetch=1, grid=(nb,),
            in_specs=[pl.BlockSpec((1, br, D), lambda i, s: (s[0], s[1] * nb + i, 0)),
                      pl.BlockSpec((land.shape[0], br, D), lambda i, s: (0, i, 0))],
            out_specs=pl.BlockSpec((br, D), lambda i, s: (s[1] * nb + i, 0))),
        out_shape=jax.ShapeDtypeStruct((2 * h, D), F32), compiler_params=_cp(),
    )(place, g, land)


def _sum_slots(name, v):
    def body(in_ref, out_ref):
        acc = in_ref[0]
        for k in range(1, 8):
            acc = acc + in_ref[k]
        out_ref[...] = acc

    vm = pl.BlockSpec(memory_space=pltpu.VMEM)
    return pl.pallas_call(body, name=name, in_specs=[vm], out_specs=vm,
                          out_shape=jax.ShapeDtypeStruct(v.shape[1:], F32))(v)


def _adamw_math(w, g, m, v):
    m = ADAM_B1 * m + (1.0 - ADAM_B1) * g
    v = ADAM_B2 * v + (1.0 - ADAM_B2) * (g * g)
    m_hat = m / (1.0 - ADAM_B1 ** ADAM_STEP)
    v_hat = v / (1.0 - ADAM_B2 ** ADAM_STEP)
    delta = -ADAM_LR * (m_hat / (jnp.sqrt(v_hat) + ADAM_EPS) + ADAM_WD * w)
    return delta, m, v


def _adamw(w, g, m, v, dep, row0=None):
    r, c = w.shape
    br = next(b for b in (256, 176, 128, r) if r % b == 0 and (row0 or 0) % b == 0)
    first = (row0 or 0) // br

    def body(w_ref, g_ref, m_ref, v_ref, dep_ref, *outs):
        gv = g_ref[...]
        outs[-3][...], outs[-2][...], outs[-1][...] = _adamw_math(w_ref[...], gv, m_ref[...], v_ref[...])
        if row0 is not None:
            outs[0][...] = gv

    spec = pl.BlockSpec((br, c), lambda i: (i, 0))
    n_out = 3 if row0 is None else 4
    out = pl.pallas_call(
        body, name="adamw", grid=(r // br,),
        in_specs=[spec, pl.BlockSpec((br, c), lambda i: (first + i, 0)), spec, spec, ANY], out_specs=[spec] * n_out,
        out_shape=[jax.ShapeDtypeStruct((r, c), F32)] * n_out, compiler_params=_cp(),
    )(w, g, m, v, dep)
    return out if row0 is not None else [g] + list(out)


def _adamw_small(ws, gs, ms, vs):
    n = len(ws)

    def body(*refs):
        ins, outs = refs[:4 * n], refs[4 * n:]
        for k in range(n):
            d, m, v = _adamw_math(ins[k][...], ins[n + k][...], ins[2 * n + k][...], ins[3 * n + k][...])
            outs[k][...] = d
            outs[n + k][...] = m
            outs[2 * n + k][...] = v

    vm = pl.BlockSpec(memory_space=pltpu.VMEM)
    out = pl.pallas_call(
        body, name="adamw_small", in_specs=[vm] * (4 * n), out_specs=[vm] * (3 * n),
        out_shape=[jax.ShapeDtypeStruct(w.shape, F32) for w in ws] * 3,
    )(*ws, *gs, *ms, *vs)
    return out[:n], out[n:2 * n], out[2 * n:]


BIG = ("w_in", "w_out", "w_ffn_gate", "w_ffn_up", "w_ffn_down", "w_ple_proj", "w_ple_gate")
SMALL = ("g_mix_pre", "b_forget", "g_attn_grp", "g_pool_grp", "w_pool", "pool_scale", "g_mix_post", "g_ffn_pre",
         "g_ffn_post", "g_ple")
TRANSPOSED = ("w_in", "w_ffn_gate", "w_ffn_up")
VECTORS = tuple(n for n in SMALL if n != "w_pool")
ORDER = ("g_mix_pre", "w_in", "b_forget", "g_attn_grp", "g_pool_grp", "w_pool", "pool_scale", "w_out", "g_mix_post",
         "g_ffn_pre", "w_ffn_gate", "w_ffn_up", "w_ffn_down", "g_ffn_post", "w_ple_proj", "g_ple", "w_ple_gate")


def _pad_rows(a, rows):
    return jnp.pad(a, ((0, rows - a.shape[0]), (0, 0)))


def _stack1(w_in, w_out):
    return _pad_rows(jnp.concatenate([_pad_rows(w_in.T, IN_PAD), w_out], axis=0), ROWS1)


def _stack2(wg, wu, wd, wple, wpg):
    return _pad_rows(jnp.concatenate([wg.T, wu.T, wd, wple.reshape(DPLE // NSHARD, D), wpg], axis=0), ROWS2)


def _unstack1(s):
    return s[:IN_SH], s[O1_OUT:USED1]


def _cat(g, lo, hi):
    return g[:, lo:hi].reshape(NSHARD * (hi - lo), D)


def _unstack1_full(g):
    w_in_t = _cat(g, 0, IN_SH)
    w_in_t = jnp.concatenate([w_in_t[:3 * DA], _pad_rows(w_in_t[3 * DA:3 * DA + NH], LANE), w_in_t[3 * DA + NH:]], axis=0)
    return w_in_t, _cat(g, O1_OUT, USED1)


def _unstack2_full(g):
    w_ple = g[:, O2_PLE:O2_PG].reshape(NSHARD, DPLE, DPLE).transpose(1, 0, 2).reshape(DPLE, D)
    return _cat(g, 0, O2_U), _cat(g, O2_U, O2_D), _cat(g, O2_D, O2_PLE), w_ple, _cat(g, O2_PG, USED2)


def _shards(a):
    return a.reshape(NSHARD, a.shape[0] // NSHARD, D)


def _stack1_full(dwin_t, dwout):
    dwin_t = jnp.concatenate([dwin_t[:3 * DA + NH], dwin_t[3 * DA + LANE:]], axis=0).reshape(NSHARD, IN_SH, D)
    zeros = lambda r: jnp.zeros((NSHARD, r, D), F32)
    return jnp.concatenate([dwin_t, zeros(IN_PAD - IN_SH), _shards(dwout), zeros(ROWS1 - USED1)], axis=1)


def _stack2_full(dwg_t, dwu_t, dwd, dwple, dwpg):
    dwple = dwple.reshape(DPLE, NSHARD, DPLE).transpose(1, 0, 2).reshape(NSHARD, DPLE // NSHARD, D)
    return jnp.concatenate([_shards(dwg_t), _shards(dwu_t), _shards(dwd), dwple, _shards(dwpg),
                            jnp.zeros((NSHARD, ROWS2 - USED2, D), dwd.dtype)], axis=1)


def _sds(shape, dtype):
    return jax.ShapeDtypeStruct(shape, dtype)


class _Comm:
    def __init__(self, stack2, me, c):
        self.stack2, self.me, self.c = stack2, me, c
        self.cidx = c.astype(jnp.int32).reshape(1)
        self.place = jnp.stack([me, c]).astype(jnp.int32)
        self.h = ROWS2 // 2
        self.gather = _exchange_start("gather2_start", 3, stack2, lax.empty((NSHARD, ROWS2, D), BF), _plan_gather(self.h))
        self.dep = self.gather[4]

    def after_attention(self, after):
        own, land = _exchange_wait("gather2_wait", self.gather, _plan_gather(self.h), after)
        fwd = _plan_forward(self.h)
        self.forward = lambda own_ref, land_ref: fwd(land_ref, own_ref, land_ref)
        self.passing = _exchange_start("forward2_start", 4, own, land, self.forward)
        return self.passing[4]

    def weights2(self, after):
        return _exchange_wait("forward2_wait", self.passing, self.forward, after)[1]

    def after_ffn(self, grads2):
        g = _stack2_full(*[a.astype(BF) for a in grads2])
        self.scatter = _plan_scatter8(self.h)
        self.chip = _exchange_start("reduce2_start", 7, g, lax.empty((7, self.h, D), BF), self.scatter)
        return self.chip[4]

    def after_mix(self, after, early):
        self.early_shapes = {n: early[n].shape for n in early}
        self.small = self.start_small("small", early)
        return self.small[4]

    def after_attn(self, after):
        g, y = _exchange_wait("reduce2_wait", self.chip, self.scatter, after)
        f = _sum8("sum2", self.place, g, y, RED2 // 2)
        self.reduced2 = _exchange_inplace("reduce2_gather", 1, f, (), _plan_swap_halves(self.h))
        self.early = self.finish_small("small", self.small, self.early_shapes, after)
        return self.reduced2

    def start_small(self, name, small):
        v = _pack_small(small)
        return _exchange_start(name + "_start", 7, v, lax.empty((8,) + v.shape, F32), _plan_all)

    def finish_small(self, name, started, shapes, after):
        v, land = _exchange_wait(name + "_wait", started, _plan_all, after)
        land = lax.dynamic_update_slice(land, v[None], (2 * self.me + self.c, 0, 0))
        return _unpack_small(_sum_slots(name + "_sum", land), shapes)


def _pack_small(small):
    parts = []
    for name in small:
        flat = small[name].reshape(-1)
        parts.append(jnp.pad(flat, (0, -flat.shape[0] % LANE)).reshape(-1, LANE))
    v = jnp.concatenate(parts, axis=0)
    return _pad_rows(v, v.shape[0] + (-v.shape[0] % 8))


def _unpack_small(v, shapes):
    out, r = {}, 0
    for name in shapes:
        n = math.prod(shapes[name])
        rows = -(-n // LANE)
        out[name] = v[r:r + rows].reshape(-1)[:n].reshape(shapes[name])
        r += rows
    return out


def kernel(x, p, g_mix_pre, w_in, b_forget, g_attn_grp, g_pool_grp, w_pool, pool_scale, w_out, g_mix_post, g_ffn_pre, w_ffn_gate, w_ffn_up, w_ffn_down, g_ffn_post, w_ple_proj, g_ple, w_ple_gate, loss_target, m_g_mix_pre, m_w_in, m_b_forget, m_g_attn_grp, m_g_pool_grp, m_w_pool, m_pool_scale, m_w_out, m_g_mix_post, m_g_ffn_pre, m_w_ffn_gate, m_w_ffn_up, m_w_ffn_down, m_g_ffn_post, m_w_ple_proj, m_g_ple, m_w_ple_gate, v_g_mix_pre, v_w_in, v_b_forget, v_g_attn_grp, v_g_pool_grp, v_w_pool, v_pool_scale, v_w_out, v_g_mix_post, v_g_ffn_pre, v_w_ffn_gate, v_w_ffn_up, v_w_ffn_down, v_g_ffn_post, v_w_ple_proj, v_g_ple, v_w_ple_gate):
    args = dict(locals())
    strip = lambda n, a: a if n in VECTORS else a[0]
    w = {n: strip(n, args[n]) for n in ORDER}
    mom = {n: strip(n, args["m_" + n]) for n in ORDER}
    var = {n: strip(n, args["v_" + n]) for n in ORDER}
    sm = {n: w[n] for n in SMALL}

    c = lax.axis_index("c")
    me = 2 * lax.axis_index("x") + lax.axis_index("y")
    h1 = ROWS1 // 2
    bf = lambda n: w[n].astype(BF)
    stack1 = _stack1(bf("w_in"), bf("w_out"))
    stack2 = _stack2(*[bf(n) for n in BIG[2:]])
    land = _exchange("gather1", 3, stack1, _sds((NSHARD, ROWS1, D), BF), _plan_gather(h1))
    land, stack2 = lax.optimization_barrier((land, stack2))
    comm = _Comm(stack2, me, c)
    w1 = _unstack1_full(_exchange_inplace("gather1_forward", 4, land, (stack1,), _plan_forward(h1)))
    dx, grads1, late = _local_step(x[0], p[0, 0], loss_target[0], sm, w1, comm)

    flip = lambda n, a: a.T if n in TRANSPOSED else a
    grads, delta, new_m, new_v = {}, {}, {}, {}

    def update(n, g, dep, row0=None):
        g_, d_, m_, v_ = _adamw(flip(n, w[n]), g, flip(n, mom[n]), flip(n, var[n]), dep, row0)
        grads[n], delta[n], new_m[n], new_v[n] = flip(n, g_), flip(n, d_), flip(n, m_), flip(n, v_)
        return v_

    late_shapes = {n: late[n].shape for n in late}
    small2 = comm.start_small("small2", late)
    red2 = comm.reduced2
    g1 = _stack1_full(*grads1)
    pair1 = _exchange_start("reduce1_pair_start", 1, g1, lax.empty((NSHARD, h1, D), F32), _plan_pair_rows(h1))
    dep = update("w_ple_gate", red2, pair1[4] + small2[4], O2_PG)
    dep = update("w_ple_proj", red2[O2_PLE:O2_PG].reshape(DPLE, DPLE), dep)
    g1, recv = _exchange_wait("reduce1_pair_wait", pair1, _plan_pair_rows(h1), dep)
    pb = _pair_sum("pair_sum1", comm.cidx, g1, recv, RED1)
    chip1 = _exchange_start("reduce1_chip_start", 3, pb, lax.empty((NSHARD - 1, h1, D), BF), _plan_scatter)
    dep = update("w_ffn_gate", red2, chip1[4], 0)
    dep = update("w_ffn_up", red2, dep, O2_U)
    dep = update("w_ffn_down", red2, dep, O2_D)
    red_small = {**comm.early, **comm.finish_small("small2", small2, late_shapes, dep)}
    loss = 0.5 / D * red_small["loss"][0, 0]
    for n in SMALL:
        grads[n] = red_small[n].reshape(w[n].shape)
    two_d = lambda a: a.reshape(-1, a.shape[-1])
    ds, ms, vs = _adamw_small([two_d(w[n]) for n in SMALL], [two_d(grads[n]) for n in SMALL],
                              [two_d(mom[n]) for n in SMALL], [two_d(var[n]) for n in SMALL])
    for k, n in enumerate(SMALL):
        delta[n], new_m[n], new_v[n] = ds[k].reshape(w[n].shape), ms[k].reshape(w[n].shape), vs[k].reshape(w[n].shape)
    pb, y = _exchange_wait("reduce1_chip_wait", chip1, _plan_scatter, vs[0])
    f = _chip_sum("chip_sum1", comm.place, pb, y, RED1)
    reduced1 = _exchange_inplace("reduce1_gather", 1, f, (), _plan_swap_halves(h1))
    g_in, g_out = _unstack1(reduced1)
    update("w_out", g_out, update("w_in", g_in, reduced1))

    lead = lambda d: [d[n] if n in VECTORS else d[n][None] for n in ORDER]
    return (loss, dx[None], *lead(grads), *lead(delta), *lead(new_m), *lead(new_v))
```

```python
import functools
import math

import jax
import jax.numpy as jnp
import numpy as np
from jax import lax
from jax.experimental import pallas as pl
from jax.experimental.pallas import tpu as pltpu

F32 = jnp.float32
BF = jnp.bfloat16
MESH = pl.DeviceIdType.MESH

D = 1024
DA = 512
DP = 512
NH = 8
HD = 64
DFF = 2816
DPLE = 256
WINS = (2, 4, 8, 16)
PC = 128
ZW = 3 * DA + 128 + DP
EPS = 1e-6
NSHARD = 4

LANE = 128
HALO = 128

IN_SH = 514
IN_PAD = 528
FF_SH = DFF // NSHARD
O1_OUT, USED1, ROWS1 = 528, 784, 800
O2_U, O2_D, O2_PLE, O2_PG, USED2, ROWS2 = 704, 1408, 2112, 2176, 2432, 2560
RED1, RED2 = 400, 640

ADAM_LR, ADAM_B1, ADAM_B2, ADAM_EPS, ADAM_WD, ADAM_STEP = 0.001, 0.9, 0.999, 1e-8, 0.01, 10

VMEM_LIMIT = 56 * 1024 * 1024


def _cp(**kw):
    return pltpu.CompilerParams(vmem_limit_bytes=VMEM_LIMIT, **kw)


def _mm(a, b):
    return jnp.dot(a.astype(BF), b.astype(BF), preferred_element_type=F32)


def _mm_nt(a, b):
    return lax.dot_general(a.astype(BF), b.astype(BF), (((1,), (1,)), ((), ())), preferred_element_type=F32)


def _mm_tn(a, b):
    return lax.dot_general(a.astype(BF), b.astype(BF), (((0,), (0,)), ((), ())), preferred_element_type=F32)


def _split3(x):
    hi = x.astype(BF)
    r = x - hi.astype(F32)
    mid = r.astype(BF)
    lo = (r - mid.astype(F32)).astype(BF)
    return hi, mid, lo


def _dot3(m, x):
    hi, mid, lo = _split3(x)
    return (jnp.dot(m, hi, preferred_element_type=F32) + jnp.dot(m, mid, preferred_element_type=F32)
            + jnp.dot(m, lo, preferred_element_type=F32))


def _window_sum(ext, w, back):
    n = ext.shape[0]
    s, k = ext, 1
    while k < w:
        s = s + pltpu.roll(s, k if back else n - k, 0)
        k *= 2
    return s


def _rstd(x):
    return lax.rsqrt(jnp.mean(x * x, axis=-1, keepdims=True) + EPS)


def _rms_bwd(dy, x, g):
    r = _rstd(x)
    xh = x * r
    dg = jnp.sum(dy * xh, axis=0, keepdims=True)
    dxh = dy * g
    dx = r * (dxh - xh * jnp.mean(dxh * xh, axis=-1, keepdims=True))
    return dx, dg


def _sigmoid(x):
    return 1.0 / (1.0 + jnp.exp(-x))


ANY = pl.BlockSpec(memory_space=pl.ANY)


def _full(shape):
    n = len(shape)
    return pl.BlockSpec(shape, lambda *_: (0,) * n)


def _resident(shape):
    n = len(shape)
    return pl.BlockSpec(shape, lambda *_: (0,) * n, pipeline_mode=pl.Buffered(1))


def _tile(t):
    return 512 if t % 512 == 0 else t


def _tri(n, upper):
    r, c = np.indices((n, n))
    return ((c >= r) if upper else (c <= r)).astype(BF)


def _aug_consts():
    row, col = np.indices((3 * LANE, NH * LANE))
    piece, head = row // LANE, row % LANE
    ch, cl = col // LANE, col % LANE
    eq = ((head == ch) & (cl == HD + 8 * piece + head)).astype(BF)
    ek = -((head == ch) & (cl == HD + 24 + 8 * piece + head)).astype(BF)
    off = (np.arange(NH * LANE) % LANE - HD - np.arange(NH * LANE) // LANE)[None, :]
    rowq = ((off >= 24) & (off < 48) & (off % 8 == 0)).astype(np.float32)
    rowk = ((off >= 0) & (off < 24) & (off % 8 == 0)).astype(np.float32)
    return eq, ek, rowq, rowk


def _in_proj(x, g1, w_in_t, b_pad, tri, eq, ek, rowq, rowk, dep):
    t = x.shape[0]
    tt = _tile(t)

    def body(x_ref, g_ref, w_ref, b_ref, tri_ref, eq_ref, ek_ref, rq_ref, rk_ref, dep_ref,
             qa_ref, ka_ref, v_ref, u_ref, fl_ref, carry):
        i = pl.program_id(0)

        @pl.when(i == 0)
        def _():
            carry[...] = jnp.zeros_like(carry)

        xv = x_ref[...]
        hn = (xv * _rstd(xv) * g_ref[...]).astype(BF)
        z = _mm_nt(hn, w_ref[...])
        fl = z[:, 3 * DA:3 * DA + LANE] + b_ref[...]
        lane = lax.broadcasted_iota(jnp.int32, fl.shape, 1)
        lf = jnp.where(lane < NH, jnp.minimum(fl, 0.0) - jnp.log(1.0 + jnp.exp(-jnp.abs(fl))), 0.0)
        c = carry[...] + _dot3(tri_ref[...], lf)
        carry[...] = carry[...] + jnp.sum(lf, axis=0, keepdims=True)
        caug = jnp.concatenate(_split3(c), axis=1)
        aug_q = jnp.dot(caug, eq_ref[...], preferred_element_type=F32) + rq_ref[...]
        aug_k = jnp.dot(caug, ek_ref[...], preferred_element_type=F32) + rk_ref[...]
        low = lax.broadcasted_iota(jnp.int32, (tt, LANE), 1) < HD
        for p in range(NH // 2):
            qp = z[:, LANE * p:LANE * (p + 1)] * (1.0 / math.sqrt(HD))
            kp = z[:, DA + LANE * p:DA + LANE * (p + 1)]
            for h, (qh, kh) in enumerate(((qp, kp), (pltpu.roll(qp, HD, 1), pltpu.roll(kp, HD, 1)))):
                lo_, hi_ = LANE * (2 * p + h), LANE * (2 * p + h + 1)
                qa_ref[:, lo_:hi_] = jnp.where(low, qh, aug_q[:, lo_:hi_]).astype(BF)
                ka_ref[:, lo_:hi_] = jnp.where(low, kh, aug_k[:, lo_:hi_]).astype(BF)
        v_ref[...] = z[:, 2 * DA:3 * DA].astype(BF)
        u_ref[...] = z[:, 3 * DA + LANE:]
        fl_ref[...] = fl

    return pl.pallas_call(
        body, name="in_proj", grid=(t // tt,),
        in_specs=[pl.BlockSpec((tt, D), lambda i: (i, 0)), _full((1, D)), _resident((ZW, D)), _full((1, LANE)),
                  _full((tt, tt)), _full((3 * LANE, NH * LANE)), _full((3 * LANE, NH * LANE)),
                  _full((1, NH * LANE)), _full((1, NH * LANE)), ANY],
        out_specs=[pl.BlockSpec((tt, NH * LANE), lambda i: (i, 0)), pl.BlockSpec((tt, NH * LANE), lambda i: (i, 0)),
                   pl.BlockSpec((tt, DA), lambda i: (i, 0)), pl.BlockSpec((tt, DP), lambda i: (i, 0)),
                   pl.BlockSpec((tt, LANE), lambda i: (i, 0))],
        out_shape=[jax.ShapeDtypeStruct((t, NH * LANE), BF), jax.ShapeDtypeStruct((t, NH * LANE), BF),
                   jax.ShapeDtypeStruct((t, DA), BF), jax.ShapeDtypeStruct((t, DP), F32),
                   jax.ShapeDtypeStruct((t, LANE), F32)],
        scratch_shapes=[pltpu.VMEM((1, LANE), F32)],
        compiler_params=_cp(),
    )(x, g1, w_in_t, b_pad, tri, eq, ek, rowq, rowk, dep)


def _attn_fwd(qa, ka, v):
    t = qa.shape[0]
    ta = _tile(t)
    n = t // ta

    def body(q_ref, k_ref, v_ref, a_ref, lse_ref, m_ref, l_ref, acc_ref):
        i = pl.program_id(1)
        m_ref[...] = jnp.full_like(m_ref, -1e30)
        l_ref[...] = jnp.zeros_like(l_ref)
        acc_ref[...] = jnp.zeros_like(acc_ref)
        qs = [q_ref[:, LANE * h:LANE * (h + 1)] for h in range(2)]
        reps = ta // LANE

        def tile(j, masked):
            rows = pl.ds(pl.multiple_of(j * ta, ta), ta)
            v2 = v_ref[rows, :]
            s = [_mm_nt(qs[h], k_ref[rows, LANE * h:LANE * (h + 1)]) for h in range(2)]
            if masked:
                keep = (lax.broadcasted_iota(jnp.int32, (ta, ta), 1) <= lax.broadcasted_iota(jnp.int32, (ta, ta), 0))
                s = [jnp.where(keep, sh, -1e30) for sh in s]
            m_old = [m_ref[h] for h in range(2)]
            m_new = [jnp.maximum(m_old[h], jnp.max(s[h], axis=1, keepdims=True)) for h in range(2)]
            pe = [jnp.exp(s[h] - jnp.tile(m_new[h], (1, reps))) for h in range(2)]
            alpha = [jnp.exp(m_old[h] - m_new[h]) for h in range(2)]
            pv = [jnp.dot(pe[h].astype(BF), v2, preferred_element_type=F32) for h in range(2)]
            for h in range(2):
                l_ref[h] = alpha[h] * l_ref[h] + jnp.sum(pe[h], axis=1, keepdims=True)
                acc_ref[h] = alpha[h] * acc_ref[h] + pv[h]
                m_ref[h] = m_new[h]

        def step(j, carry):
            tile(j, False)
            return carry

        lax.fori_loop(0, i, step, 0)
        tile(i, True)
        low = lax.broadcasted_iota(jnp.int32, (ta, LANE), 1) < HD
        a_ref[...] = jnp.where(low, acc_ref[0] / l_ref[0], acc_ref[1] / l_ref[1])
        lse_ref[...] = jnp.where(low, m_ref[0] + jnp.log(l_ref[0]), m_ref[1] + jnp.log(l_ref[1]))

    return pl.pallas_call(
        body, name="attn_fwd", grid=(NH // 2, n),
        in_specs=[pl.BlockSpec((ta, 2 * LANE), lambda p, i: (i, p)),
                  pl.BlockSpec((t, 2 * LANE), lambda p, i: (0, p)),
                  pl.BlockSpec((t, LANE), lambda p, i: (0, p))],
        out_specs=[pl.BlockSpec((ta, LANE), lambda p, i: (i, p)), pl.BlockSpec((ta, LANE), lambda p, i: (i, p))],
        out_shape=[jax.ShapeDtypeStruct((t, DA), F32), jax.ShapeDtypeStruct((t, DA), F32)],
        scratch_shapes=[pltpu.VMEM((2, ta, LANE), F32), pltpu.VMEM((2, ta, LANE), F32), pltpu.VMEM((2, ta, LANE), F32)],
        compiler_params=_cp(),
    )(qa, ka, v)


def _mix_out(a, u, x, w_pool, pool_scale, g_attn, g_pool, w_out, g_post, dep):
    t = a.shape[0]
    tt = _tile(t)
    hb = tt // HALO

    def body(a_ref, u_ref, up_ref, x_ref, wp_ref, ps_ref, ga_ref, gp_ref, wo_ref, go_ref, dep_ref,
             yb_ref, m_ref, o_ref, h1_ref):
        i = pl.program_id(0)
        prev = up_ref[...] * jnp.where(i > 0, 1.0, 0.0)
        tok = i * tt + lax.broadcasted_iota(jnp.int32, (tt, PC), 0)
        ms = []
        for g, w in enumerate(WINS):
            ug = u_ref[:, PC * g:PC * (g + 1)]
            ext = jnp.concatenate([prev[:, PC * g:PC * (g + 1)], ug], axis=0)
            cnt = jnp.minimum(tok + 1, w).astype(F32)
            y = (_window_sum(ext, w, True)[HALO:] / cnt - ug).astype(BF)
            yb_ref[:, PC * g:PC * (g + 1)] = y
            ms.append(_mm(y, wp_ref[g]) * ps_ref[:, PC * g:PC * (g + 1)])
        m = jnp.concatenate(ms, axis=1)
        m_ref[...] = m
        av = a_ref[...]
        mix = jnp.concatenate([av * _rstd(av) * ga_ref[...], m * _rstd(m) * gp_ref[...]], axis=1)
        o = _mm(mix, wo_ref[...])
        o_ref[...] = o
        h1_ref[...] = x_ref[...] + o * _rstd(o) * go_ref[...]

    return pl.pallas_call(
        body, name="mix_out", grid=(t // tt,),
        in_specs=[pl.BlockSpec((tt, DA), lambda i: (i, 0)), pl.BlockSpec((tt, DP), lambda i: (i, 0)),
                  pl.BlockSpec((HALO, DP), lambda i: (jnp.maximum(i * hb - 1, 0), 0)),
                  pl.BlockSpec((tt, D), lambda i: (i, 0)),
                  _full((len(WINS), PC, PC)), _full((1, DP)), _full((1, DA)), _full((1, DP)),
                  _resident((D, D)), _full((1, D)), ANY],
        out_specs=[pl.BlockSpec((tt, DP), lambda i: (i, 0)), pl.BlockSpec((tt, DP), lambda i: (i, 0)),
                   pl.BlockSpec((tt, D), lambda i: (i, 0)), pl.BlockSpec((tt, D), lambda i: (i, 0))],
        out_shape=[jax.ShapeDtypeStruct((t, DP), BF), jax.ShapeDtypeStruct((t, DP), F32),
                   jax.ShapeDtypeStruct((t, D), F32), jax.ShapeDtypeStruct((t, D), F32)],
        compiler_params=_cp(),
    )(a, u, u, x, w_pool, pool_scale, g_attn, g_pool, w_out, g_post, dep)


def _ffn_fwd(h1, g_pre, stacks2, g_post, p, w_ple, g_ple, w_pg, tgt):
    t = h1.shape[0]
    tt = 256 if t % 256 == 0 else t

    def body(h1_ref, gpre_ref, wg_ref, wu_ref, wd_ref, gpost_ref, p_ref, wple_ref, gple_ref, wpg_ref, tgt_ref,
             hn_ref, gate_ref, up_ref, dff_ref, dh2_ref, loss_ref, dwpg_ref, dwple_ref, dgple_ref, dgpost_ref):
        i = pl.program_id(0)

        @pl.when(i == 0)
        def _():
            loss_ref[...] = jnp.zeros_like(loss_ref)
            dwpg_ref[...] = jnp.zeros_like(dwpg_ref)
            dwple_ref[...] = jnp.zeros_like(dwple_ref)
            dgple_ref[...] = jnp.zeros_like(dgple_ref)
            dgpost_ref[...] = jnp.zeros_like(dgpost_ref)

        h1v = h1_ref[...]
        hn = (h1v * _rstd(h1v) * gpre_ref[...]).astype(BF)
        hn_ref[...] = hn
        gate = _mm_nt(hn, wg_ref[...].reshape(DFF, D))
        up = _mm_nt(hn, wu_ref[...].reshape(DFF, D))
        for k in range(DFF // FF_CH):
            gate_ref[k] = gate[:, FF_CH * k:FF_CH * (k + 1)].astype(BF)
            up_ref[k] = up[:, FF_CH * k:FF_CH * (k + 1)].astype(BF)
        ff = _mm(gate * _sigmoid(gate) * up, wd_ref[...].reshape(DFF, D))
        rff = _rstd(ff)
        ffh = ff * rff
        gpost = gpost_ref[...]
        h2 = h1v + ffh * gpost
        pv = p_ref[...]
        pe = _mm(pv, wple_ref[...])
        rpe = _rstd(pe)
        peh = pe * rpe
        gple = gple_ref[...]
        e = peh * gple
        sig = _sigmoid(_mm(h2, wpg_ref[...]))
        dv = h2 + sig * e - tgt_ref[...]
        sq = jnp.sum(jnp.sum(dv * dv, axis=1, keepdims=True), axis=0, keepdims=True)
        loss_ref[...] = loss_ref[...] + sq
        dy = dv * (1.0 / D)
        d_e = dy * sig
        d_gl = dy * e * sig * (1.0 - sig)
        dh2 = dy + _mm_nt(d_gl, wpg_ref[...])
        dh2_ref[...] = dh2
        dwpg_ref[...] = dwpg_ref[...] + _mm_tn(h2, d_gl)
        dgple_ref[...] = dgple_ref[...] + jnp.sum(d_e * peh, axis=0, keepdims=True)
        dpeh = d_e * gple
        d_pe = rpe * (dpeh - peh * jnp.mean(dpeh * peh, axis=-1, keepdims=True))
        dwple_ref[...] = dwple_ref[...] + _mm_tn(pv, d_pe)
        dgpost_ref[...] = dgpost_ref[...] + jnp.sum(dh2 * ffh, axis=0, keepdims=True)
        dffh = dh2 * gpost
        dff_ref[...] = (rff * (dffh - ffh * jnp.mean(dffh * ffh, axis=-1, keepdims=True))).astype(BF)

    row = lambda w: pl.BlockSpec((tt, w), lambda i: (i, 0))
    chunked = pl.BlockSpec((DFF // FF_CH, tt, FF_CH), lambda i: (0, i, 0))
    shard_rows = lambda k: pl.BlockSpec((NSHARD, FF_SH, D), lambda i: (0, k, 0), pipeline_mode=pl.Buffered(1))
    return pl.pallas_call(
        body, name="ffn_fwd", grid=(t // tt,),
        in_specs=[row(D), _full((1, D)), shard_rows(0), shard_rows(1), shard_rows(2), _full((1, D)),
                  row(DPLE), _resident((DPLE, D)), _full((1, D)), _resident((D, D)), row(D)],
        out_specs=[row(D), chunked, chunked, row(D), row(D), _full((8, LANE)), _full((D, D)), _full((DPLE, D)),
                   _full((1, D)), _full((1, D))],
        out_shape=[jax.ShapeDtypeStruct((t, D), BF), jax.ShapeDtypeStruct((DFF // FF_CH, t, FF_CH), BF),
                   jax.ShapeDtypeStruct((DFF // FF_CH, t, FF_CH), BF),
                   jax.ShapeDtypeStruct((t, D), BF), jax.ShapeDtypeStruct((t, D), F32), jax.ShapeDtypeStruct((8, LANE), F32),
                   jax.ShapeDtypeStruct((D, D), F32), jax.ShapeDtypeStruct((DPLE, D), F32),
                   jax.ShapeDtypeStruct((1, D), F32), jax.ShapeDtypeStruct((1, D), F32)],
        compiler_params=_cp(),
    )(h1, g_pre, stacks2, stacks2, stacks2, g_post, p, w_ple, g_ple, w_pg, tgt)


FF_CH = 256


def _ffn_bwd(hn2, gate, up, dff, wgu, wd):
    t = hn2.shape[0]
    tt = 1024 if t % 1024 == 0 else _tile(t)
    nt = t // tt
    ch = FF_CH
    nc = DFF // ch

    def body(hn_ref, gate_ref, up_ref, dff_ref, wgu_ref, wd_ref,
             dwg_ref, dwu_ref, dwd_ref, dhn_ref, acc, gu_acc, d_acc, sem):
        j, i = pl.program_id(0), pl.program_id(1)

        @pl.when(j == 0)
        def _():
            acc[pl.ds(pl.multiple_of(i * tt, tt), tt), :] = jnp.zeros((tt, D), F32)

        @pl.when(i == 0)
        def _():
            gu_acc[...] = jnp.zeros_like(gu_acc)
            d_acc[...] = jnp.zeros_like(d_acc)

        parts = 4 if tt % 1024 == 0 else 2
        half = tt // parts
        acts, dgus = [], []
        for hh in range(parts):
            r = slice(hh * half, (hh + 1) * half)
            gate_v = gate_ref[0, r, :].astype(F32)
            up_v = up_ref[0, r, :].astype(F32)
            sg = _sigmoid(gate_v)
            silu = gate_v * sg
            d_act = _mm_nt(dff_ref[r, :], wd_ref[...])
            d_up = (d_act * silu).astype(BF)
            d_gate = (d_act * up_v * (sg * (1.0 + gate_v * (1.0 - sg)))).astype(BF)
            dgu = jnp.concatenate([d_gate, d_up], axis=1)
            rows = pl.ds(pl.multiple_of(i * tt + hh * half, half), half)
            acc[rows, :] = acc[rows, :] + jnp.dot(dgu, wgu_ref[0], preferred_element_type=F32)
            acts.append((silu * up_v).astype(BF))
            dgus.append(dgu)
        d_acc[...] = d_acc[...] + _mm_tn(jnp.concatenate(acts, axis=0), dff_ref[...])
        gu_acc[...] = gu_acc[...] + _mm_tn(jnp.concatenate(dgus, axis=0), hn_ref[...])

        @pl.when(i == nt - 1)
        def _():
            dwg_ref[...] = gu_acc[:ch].astype(BF)
            dwu_ref[...] = gu_acc[ch:].astype(BF)
            dwd_ref[...] = d_acc[...].astype(BF)

        @pl.when((j == nc - 1) & (i == nt - 1))
        def _():
            cp = pltpu.make_async_copy(acc, dhn_ref, sem)
            cp.start()
            cp.wait()

    tok = lambda w: pl.BlockSpec((tt, w), lambda j, i: (i, 0))
    chunk = pl.BlockSpec((ch, D), lambda j, i: (j, 0))
    pair = pl.BlockSpec((1, 2 * ch, D), lambda j, i: (j, 0, 0))
    return pl.pallas_call(
        body, name="ffn_bwd", grid=(nc, nt),
        in_specs=[tok(D), pl.BlockSpec((1, tt, ch), lambda j, i: (j, i, 0)), pl.BlockSpec((1, tt, ch), lambda j, i: (j, i, 0)),
                  tok(D), pair, chunk],
        out_specs=[chunk, chunk, chunk, pl.BlockSpec(memory_space=pl.ANY)],
        out_shape=[jax.ShapeDtypeStruct((DFF, D), BF), jax.ShapeDtypeStruct((DFF, D), BF),
                   jax.ShapeDtypeStruct((DFF, D), BF), jax.ShapeDtypeStruct((t, D), F32)],
        scratch_shapes=[pltpu.VMEM((t, D), F32), pltpu.VMEM((2 * ch, D), F32), pltpu.VMEM((ch, D), F32),
                        pltpu.SemaphoreType.DMA],
        compiler_params=_cp(),
    )(hn2, gate, up, dff, wgu, wd)


def _mix_bwd(d_hn2, dh2, h1, o, a, m, yb, g_ffn_pre, g_post, g_attn, g_pool, w_out, w_pool, pool_scale, dep):
    t = a.shape[0]
    tt = 256 if t % 256 == 0 else t

    def body(dhn_ref, dh2_ref, h1_ref, o_ref, a_ref, m_ref, yb_ref, gfp_ref, go_ref, ga_ref, gp_ref, wo_ref, wp_ref,
             ps_ref, dep_ref, dh1_ref, da_ref, dyc_ref, dgfp_ref, dgo_ref, dga_ref, dgp_ref, dps_ref, dwp_ref, dwo_ref):
        i = pl.program_id(0)

        @pl.when(i == 0)
        def _():
            for r in (dgfp_ref, dgo_ref, dga_ref, dgp_ref, dps_ref, dwp_ref, dwo_ref):
                r[...] = jnp.zeros_like(r)

        d1, dg = _rms_bwd(dhn_ref[...], h1_ref[...], gfp_ref[...])
        dgfp_ref[...] = dgfp_ref[...] + dg
        dh1 = dh2_ref[...] + d1
        dh1_ref[...] = dh1
        d_o, dg = _rms_bwd(dh1, o_ref[...], go_ref[...])
        dgo_ref[...] = dgo_ref[...] + dg
        d_mix = _mm_nt(d_o, wo_ref[...])
        av, mv = a_ref[...], m_ref[...]
        mix = jnp.concatenate([av * _rstd(av) * ga_ref[...], mv * _rstd(mv) * gp_ref[...]], axis=1)
        dwo_ref[...] = dwo_ref[...] + _mm_tn(mix, d_o)
        d_a, dg = _rms_bwd(d_mix[:, :DA], av, ga_ref[...])
        dga_ref[...] = dga_ref[...] + dg
        da_ref[...] = d_a
        d_m, dg = _rms_bwd(d_mix[:, DA:], mv, gp_ref[...])
        dgp_ref[...] = dgp_ref[...] + dg
        tok = i * tt + lax.broadcasted_iota(jnp.int32, (tt, PC), 0)
        dps = []
        for g, w in enumerate(WINS):
            sl = slice(PC * g, PC * (g + 1))
            ybg = yb_ref[:, sl]
            wpg = wp_ref[g].astype(BF)
            mlin = jnp.dot(ybg, wpg, preferred_element_type=F32)
            dmg = d_m[:, sl]
            dps.append(jnp.sum(dmg * mlin, axis=0, keepdims=True))
            dml = (dmg * ps_ref[:, sl]).astype(BF)
            dwp_ref[g] = dwp_ref[g] + _mm_tn(ybg, dml)
            dyc_ref[:, sl] = _mm_nt(dml, wpg) / jnp.minimum(tok + 1, w).astype(F32)
        dps_ref[...] = dps_ref[...] + jnp.concatenate(dps, axis=1)

    row = lambda w: pl.BlockSpec((tt, w), lambda i: (i, 0))
    return pl.pallas_call(
        body, name="mix_bwd", grid=(t // tt,),
        in_specs=[row(D), row(D), row(D), row(D), row(DA), row(DP), row(DP), _full((1, D)), _full((1, D)),
                  _full((1, DA)), _full((1, DP)), _resident((D, D)), _full((len(WINS), PC, PC)), _full((1, DP)), ANY],
        out_specs=[row(D), row(DA), row(DP), _full((1, D)), _full((1, D)), _full((1, DA)), _full((1, DP)),
                   _full((1, DP)), _full((len(WINS), PC, PC)), _full((D, D))],
        out_shape=[jax.ShapeDtypeStruct((t, D), F32), jax.ShapeDtypeStruct((t, DA), F32), jax.ShapeDtypeStruct((t, DP), F32),
                   jax.ShapeDtypeStruct((1, D), F32), jax.ShapeDtypeStruct((1, D), F32), jax.ShapeDtypeStruct((1, DA), F32),
                   jax.ShapeDtypeStruct((1, DP), F32), jax.ShapeDtypeStruct((1, DP), F32),
                   jax.ShapeDtypeStruct((len(WINS), PC, PC), F32), jax.ShapeDtypeStruct((D, D), F32)],
        compiler_params=_cp(),
    )(d_hn2, dh2, h1, o, a, m, yb, g_ffn_pre, g_post, g_attn, g_pool, w_out, w_pool, pool_scale, dep)


def _attn_bwd(qa, ka, v, a, d_a, lse, dep):
    t = qa.shape[0]
    ta = _tile(t)
    n = t // ta

    def body(q_ref, k_ref, v_ref, o_ref, do_ref, lse_ref, dep_ref, dq_ref, dk_ref, dv_ref):
        j = pl.program_id(1)

        @pl.when(j == 0)
        def _():
            dq_ref[...] = jnp.zeros_like(dq_ref)

        dk_ref[...] = jnp.zeros_like(dk_ref)
        dv_ref[...] = jnp.zeros_like(dv_ref)
        ks = [k_ref[:, LANE * h:LANE * (h + 1)] for h in range(2)]
        v2 = v_ref[...]
        lane = lax.broadcasted_iota(jnp.int32, (ta, LANE), 1)
        mine = [lane < HD, lane >= HD]

        def tile(i, masked):
            rows = pl.ds(pl.multiple_of(i * ta, ta), ta)
            do2 = do_ref[rows, :]
            prod = do2 * o_ref[rows, :]
            lse2 = lse_ref[rows, :]
            do2b = do2.astype(BF)
            qh = [q_ref[rows, LANE * h:LANE * (h + 1)] for h in range(2)]
            s = [_mm_nt(qh[h], ks[h]) for h in range(2)]
            dp = [_mm_nt(jnp.where(mine[h], do2, 0.0), v2) for h in range(2)]
            delta = [jnp.sum(jnp.where(mine[h], prod, 0.0), axis=1, keepdims=True) for h in range(2)]
            lse_h = [jnp.sum(jnp.where(lane == HD * h, lse2, 0.0), axis=1, keepdims=True) for h in range(2)]
            pr = [jnp.exp(s[h] - lse_h[h]) for h in range(2)]
            if masked:
                keep = (lax.broadcasted_iota(jnp.int32, (ta, ta), 1) <= lax.broadcasted_iota(jnp.int32, (ta, ta), 0))
                pr = [jnp.where(keep, ph, 0.0) for ph in pr]
            ds = [(pr[h] * (dp[h] - delta[h])).astype(BF) for h in range(2)]
            dv_ref[...] = dv_ref[...] + jnp.where(mine[0], _mm_tn(pr[0], do2b), _mm_tn(pr[1], do2b))
            for h in range(2):
                sl = slice(LANE * h, LANE * (h + 1))
                dk_ref[:, sl] = dk_ref[:, sl] + _mm_tn(ds[h], qh[h])
                dq_ref[0, rows, sl] = dq_ref[0, rows, sl] + jnp.dot(ds[h], ks[h], preferred_element_type=F32)

        def step(i, carry):
            tile(i, False)
            return carry

        tile(j, True)
        lax.fori_loop(j + 1, n, step, 0)

    qrow = lambda w: pl.BlockSpec((t, w), lambda p, j: (0, p))
    krow = lambda w: pl.BlockSpec((ta, w), lambda p, j: (j, p))
    return pl.pallas_call(
        body, name="attn_bwd", grid=(NH // 2, n),
        in_specs=[qrow(2 * LANE), krow(2 * LANE), krow(LANE), qrow(LANE), qrow(LANE), qrow(LANE), ANY],
        out_specs=[pl.BlockSpec((1, t, 2 * LANE), lambda p, j: (p, 0, 0)), krow(2 * LANE), krow(LANE)],
        out_shape=[jax.ShapeDtypeStruct((NH // 2, t, 2 * LANE), F32), jax.ShapeDtypeStruct((t, NH * LANE), F32),
                   jax.ShapeDtypeStruct((t, DA), F32)],
        compiler_params=_cp(),
    )(qa, ka, v, a, d_a, lse, dep)


def _in_bwd(dqa, dka, dv, dyc, fl, x, dh1, g1, w_in_t, tri_u, dep):
    t = x.shape[0]
    tt = _tile(t)
    nt = t // tt
    hb = tt // HALO
    rev = lambda s: nt - 1 - s

    def body(dqa_ref, dka_ref, dv_ref, dyc_ref, dyn_ref, fl_ref, x_ref, dh1_ref, g_ref, w_ref, tri_ref,
             dep_ref, dx_ref, dw_ref, dg_ref, db_ref, carry, acc, sem):
        s = pl.program_id(0)
        i = nt - 1 - s

        @pl.when(s == 0)
        def _():
            carry[...] = jnp.zeros_like(carry)
            acc[...] = jnp.zeros_like(acc)
            dg_ref[...] = jnp.zeros_like(dg_ref)
            db_ref[...] = jnp.zeros_like(db_ref)

        dq_cat = jnp.concatenate([dqa_ref[p] for p in range(NH // 2)], axis=1)
        dk_cat = dka_ref[...]
        off = lax.broadcasted_iota(jnp.int32, (1, NH * LANE), 1)
        off = off % LANE - HD - off // LANE

        def picked(cat, lane_off):
            kept = jnp.where(off == lane_off, cat, 0.0)
            return functools.reduce(lambda a, b: a + b, [kept[:, LANE * h:LANE * (h + 1)] for h in range(NH)])

        dc = pltpu.roll(picked(dq_cat, 0), LANE - HD, 1) - pltpu.roll(picked(dk_cat, 24), LANE - HD - 24, 1)
        dlf = carry[...] + _dot3(tri_ref[...], dc)
        carry[...] = carry[...] + jnp.sum(dc, axis=0, keepdims=True)
        flv = fl_ref[...]
        lane = lax.broadcasted_iota(jnp.int32, flv.shape, 1)
        d_fl = jnp.where(lane < NH, dlf / (1.0 + jnp.exp(flv)), 0.0)
        db_ref[...] = db_ref[...] + jnp.sum(d_fl, axis=0, keepdims=True)
        low = lax.broadcasted_iota(jnp.int32, (tt, LANE), 1) < HD
        dqs, dks = [], []
        for p in range(NH // 2):
            b0, b1 = slice(2 * LANE * p, 2 * LANE * p + LANE), slice(2 * LANE * p + LANE, 2 * LANE * (p + 1))
            dqs.append(jnp.where(low, dq_cat[:, b0], pltpu.roll(dq_cat[:, b1], HD, 1)) * (1.0 / math.sqrt(HD)))
            dks.append(jnp.where(low, dk_cat[:, b0], pltpu.roll(dk_cat[:, b1], HD, 1)))
        nxt = dyn_ref[...] * jnp.where(i < nt - 1, 1.0, 0.0)
        tok = i * tt + lax.broadcasted_iota(jnp.int32, (tt, PC), 0)
        dus = []
        for g, w in enumerate(WINS):
            sl = slice(PC * g, PC * (g + 1))
            dycg = dyc_ref[:, sl]
            ext = jnp.concatenate([dycg, nxt[:, sl]], axis=0)
            dus.append(_window_sum(ext, w, False)[:tt] - dycg * jnp.minimum(tok + 1, w).astype(F32))
        d_z = jnp.concatenate(dqs + dks + [dv_ref[...], d_fl] + dus, axis=1).astype(BF)
        xv = x_ref[...]
        gv = g_ref[...]
        hn = (xv * _rstd(xv) * gv).astype(BF)
        d_hn = jnp.dot(d_z, w_ref[...], preferred_element_type=F32)
        acc[...] = acc[...] + _mm_tn(d_z, hn)
        d1, dg = _rms_bwd(d_hn, xv, gv)
        dg_ref[...] = dg_ref[...] + dg
        dx_ref[...] = dh1_ref[...] + d1

        @pl.when(s == nt - 1)
        def _():
            cp = pltpu.make_async_copy(acc, dw_ref, sem)
            cp.start()
            cp.wait()

    row = lambda w: pl.BlockSpec((tt, w), lambda s: (rev(s), 0))
    return pl.pallas_call(
        body, name="in_bwd", grid=(nt,),
        in_specs=[pl.BlockSpec((NH // 2, tt, 2 * LANE), lambda s: (0, rev(s), 0)), row(NH * LANE), row(DA), row(DP),
                  pl.BlockSpec((HALO, DP), lambda s: (jnp.minimum((rev(s) + 1) * hb, nt * hb - 1), 0)),
                  row(LANE), row(D), row(D), _full((1, D)), _resident((ZW, D)), _full((tt, tt)),
                  ANY],
        out_specs=[row(D), pl.BlockSpec(memory_space=pl.ANY), _full((1, D)), _full((1, LANE))],
        out_shape=[jax.ShapeDtypeStruct((t, D), F32), jax.ShapeDtypeStruct((ZW, D), F32),
                   jax.ShapeDtypeStruct((1, D), F32), jax.ShapeDtypeStruct((1, LANE), F32)],
        scratch_shapes=[pltpu.VMEM((1, LANE), F32), pltpu.VMEM((ZW, D), F32), pltpu.SemaphoreType.DMA],
        compiler_params=_cp(),
    )(dqa, dka, dv, dyc, dyc, fl, x, dh1, g1, w_in_t, tri_u, dep)


class _NoComm:
    def __init__(self, w2):
        self.w2 = w2
        self.dep = jnp.zeros((8, LANE), F32)

    def after_attention(self, after):
        return self.dep

    def weights2(self, after):
        return _stack2_full(*self.w2)

    def after_ffn(self, grads2):
        self.grads2 = grads2
        return self.dep

    def after_mix(self, after, early):
        self.early = early
        return self.dep

    def after_attn(self, after):
        return self.dep


def _local_step(x, p, tgt, sm, w1, comm):
    w_in_t, w_out = w1
    tt = _tile(x.shape[0])
    eq, ek, rowq, rowk = _aug_consts()
    b_pad = jnp.pad(sm["b_forget"], ((0, 0), (0, LANE - NH)))
    qa, ka, v, u, fl = _in_proj(x, sm["g_mix_pre"], w_in_t, b_pad, _tri(tt, False), eq, ek, rowq, rowk, comm.dep)
    a, lse = _attn_fwd(qa, ka, v)
    yb, m, o, h1 = _mix_out(a, u, x, sm["w_pool"], sm["pool_scale"], sm["g_attn_grp"],
                            sm["g_pool_grp"], w_out, sm["g_mix_post"], comm.after_attention(a))
    stacks2 = comm.weights2(h1)
    wg_t, wu_t, wd, w_ple, w_pg = _unstack2_full(stacks2)
    hn2, gate, up, dff, dh2, loss, dwpg, dwple, dgple, dgfpost = _ffn_fwd(
        h1, sm["g_ffn_pre"], stacks2, sm["g_ffn_post"], p, w_ple, sm["g_ple"], w_pg, tgt)
    chunks = lambda a: a.reshape(DFF // FF_CH, FF_CH, D)
    dwg_t, dwu_t, dwd, d_hn2 = _ffn_bwd(hn2, gate, up, dff, jnp.concatenate([chunks(wg_t), chunks(wu_t)], axis=1), wd)
    dep = comm.after_ffn((dwg_t, dwu_t, dwd, dwple, dwpg))
    dh1, d_a, dyc, dgfpre, dgpost, dgattn, dgpool, dps, dwpool, dwout = _mix_bwd(
        d_hn2, dh2, h1, o, a, m, yb, sm["g_ffn_pre"], sm["g_mix_post"], sm["g_attn_grp"], sm["g_pool_grp"],
        w_out, sm["w_pool"], sm["pool_scale"], dep)
    early = dict(loss=loss[0:1, 0:1], g_attn_grp=dgattn, g_pool_grp=dgpool, w_pool=dwpool, pool_scale=dps,
                 g_mix_post=dgpost, g_ffn_pre=dgfpre, g_ffn_post=dgfpost, g_ple=dgple)
    dqa, dka, dvv = _attn_bwd(qa, ka, v, a, d_a, lse, comm.after_mix(dh1, early))
    dx, dwin_t, dg1, dbf = _in_bwd(dqa, dka, dvv, dyc, fl, x, dh1, sm["g_mix_pre"], w_in_t, _tri(tt, True),
                                   comm.after_attn(dvv))
    return dx, (dwin_t, dwout), dict(g_mix_pre=dg1, b_forget=dbf[:, :NH])


def _place():
    x, y, c = lax.axis_index("x"), lax.axis_index("y"), lax.axis_index("c")
    return x, y, c, [(1 - x, y), (x, 1 - y), (1 - x, 1 - y)]


def _rows(c, h):
    return pl.ds(pl.multiple_of(c * h, 16), h)


def _plan_gather(h):
    def plan(src, land):
        x, y, c, chips = _place()
        return [(src.at[_rows(c, h), :], land.at[2 * x + y, _rows(c, h), :], (cx, cy, c),
                 land.at[2 * cx + cy, _rows(c, h), :]) for cx, cy in chips]
    return plan


def _plan_forward(h):
    def plan(land_in, own, land):
        x, y, c, chips = _place()
        sib, me = (x, y, 1 - c), 2 * x + y
        return ([(land_in.at[2 * cx + cy, _rows(c, h), :], land.at[2 * cx + cy, _rows(c, h), :], sib,
                  land.at[2 * cx + cy, _rows(1 - c, h), :]) for cx, cy in chips]
                + [(own, land.at[me], sib, land.at[me])])
    return plan


def _plan_swap_halves(h):
    def plan(buf_in, buf):
        x, y, c, _ = _place()
        return [(buf_in.at[_rows(c, h), :], buf.at[_rows(c, h), :], (x, y, 1 - c), buf.at[_rows(1 - c, h), :])]
    return plan


def _plan_pair_rows(h):
    def plan(src, land):
        x, y, c, _ = _place()
        return [(src.at[:, _rows(1 - c, h), :], land, (x, y, 1 - c), land)]
    return plan


def _plan_scatter(src, land):
    x, y, c, chips = _place()
    return [(src.at[2 * cx + cy], land.at[k], (cx, cy, c), land.at[k]) for k, (cx, cy) in enumerate(chips)]


def _plan_scatter8(h):
    def plan(src, land):
        x, y, c, chips = _place()
        copies = [(src.at[2 * x + y, _rows(1 - c, h), :], land.at[0], (x, y, 1 - c), land.at[0])]
        for k, (cx, cy) in enumerate(chips):
            for d, other in enumerate((c, 1 - c)):
                copies.append((src.at[2 * cx + cy, _rows(other, h), :], land.at[1 + 2 * k + c], (cx, cy, other),
                               land.at[1 + 2 * k + other]))
        return copies
    return plan


def _plan_all(src, land):
    x, y, c, _ = _place()
    copies = []
    for r in range(1, 8):
        px, py, pc = (1 - a if b else a for a, b in zip((x, y, c), (r >> 2 & 1, r >> 1 & 1, r & 1)))
        copies.append((src, land.at[4 * x + 2 * y + c], (px, py, pc), land.at[4 * px + 2 * py + pc]))
    return copies


def _remote(src, dst, send_sems, recv_sems, k, peer):
    return pltpu.make_async_remote_copy(src_ref=src, dst_ref=dst, send_sem=send_sems.at[k], recv_sem=recv_sems.at[k],
                                        device_id=peer, device_id_type=MESH)


def _exchange(name, n, src, land, plan):
    def body(src_ref, land_ref, send_sems, recv_sems):
        copies = plan(src_ref, land_ref)
        for k, (s, d, peer, _) in enumerate(copies):
            _remote(s, d, send_sems, recv_sems, k, peer).start()
        for k, (s, _, peer, mine) in enumerate(copies):
            _remote(s, mine, send_sems, recv_sems, k, peer).wait_recv()
        for k, (s, d, peer, _) in enumerate(copies):
            _remote(s, d, send_sems, recv_sems, k, peer).wait_send()

    return pl.pallas_call(
        body, name=name, in_specs=[ANY], out_specs=ANY, out_shape=land,
        scratch_shapes=[pltpu.SemaphoreType.DMA((n,)), pltpu.SemaphoreType.DMA((n,))],
    )(src)


def _exchange_inplace(name, n, buf, extra, plan):
    def body(*refs):
        ins, buf_ref, send_sems, recv_sems = refs[:1 + len(extra)], refs[1 + len(extra)], refs[-2], refs[-1]
        copies = plan(*ins, buf_ref)
        for k, (s, d, peer, _) in enumerate(copies):
            _remote(s, d, send_sems, recv_sems, k, peer).start()
        for k, (s, _, peer, mine) in enumerate(copies):
            _remote(s, mine, send_sems, recv_sems, k, peer).wait_recv()
        for k, (s, d, peer, _) in enumerate(copies):
            _remote(s, d, send_sems, recv_sems, k, peer).wait_send()

    return pl.pallas_call(
        body, name=name, in_specs=[ANY] * (1 + len(extra)), out_specs=ANY, out_shape=_sds(buf.shape, buf.dtype),
        input_output_aliases={0: 0},
        scratch_shapes=[pltpu.SemaphoreType.DMA((n,)), pltpu.SemaphoreType.DMA((n,))],
    )(buf, *extra)


HBM = pl.BlockSpec(memory_space=pltpu.HBM)
SEM = pl.BlockSpec(memory_space=pltpu.SEMAPHORE)
EFFECT = pltpu.SideEffectType.DATAFLOW_SIDE_EFFECTING


def _exchange_start(name, n, src, land, plan):
    def body(src_ref, land_ref, send_sems, recv_sems, src_thru, land_thru, token):
        for k, (s, d, peer, _) in enumerate(plan(src_ref, land_ref)):
            _remote(s, d, send_sems, recv_sems, k, peer).start()
        token[...] = jnp.zeros_like(token)

    return pl.pallas_call(
        body, name=name,
        out_shape=(pltpu.SemaphoreType.DMA((n,)), pltpu.SemaphoreType.DMA((n,)), pltpu.HBM(src.shape, src.dtype),
                   pltpu.HBM(land.shape, land.dtype), jax.ShapeDtypeStruct((8, LANE), F32)),
        in_specs=(HBM, HBM), out_specs=(SEM, SEM, HBM, HBM, pl.BlockSpec(memory_space=pltpu.VMEM)),
        input_output_aliases={0: 2, 1: 3},
        compiler_params=pltpu.CompilerParams(has_side_effects=EFFECT),
    )(pltpu.with_memory_space_constraint(src, pltpu.HBM), pltpu.with_memory_space_constraint(land, pltpu.HBM))


def _exchange_wait(name, started, plan, after):
    send_sems, recv_sems, src, land, _ = started

    def body(src_ref, land_ref, send_sems, recv_sems, after_ref, src_out, land_out):
        for k, (s, _, peer, mine) in enumerate(plan(src_ref, land_ref)):
            cp = _remote(s, mine, send_sems, recv_sems, k, peer)
            cp.wait_send()
            cp.wait_recv()

    return pl.pallas_call(
        body, name=name, out_shape=(pltpu.HBM(src.shape, src.dtype), pltpu.HBM(land.shape, land.dtype)),
        in_specs=(HBM, HBM, SEM, SEM, ANY), out_specs=(HBM, HBM), input_output_aliases={0: 0, 1: 1},
        compiler_params=pltpu.CompilerParams(has_side_effects=EFFECT),
    )(src, land, send_sems, recv_sems, after)


def _pair_sum(name, cidx, g, recv, br):
    h = recv.shape[1]
    nb = h // br

    def body(c_ref, g_ref, r_ref, out_ref):
        out_ref[...] = (g_ref[...] + r_ref[...]).astype(BF)

    return pl.pallas_call(
        body, name=name,
        grid_spec=pltpu.PrefetchScalarGridSpec(
            num_scalar_prefetch=1, grid=(NSHARD, nb),
            in_specs=[pl.BlockSpec((1, br, D), lambda s, i, c: (s, c[0] * nb + i, 0)),
                      pl.BlockSpec((1, br, D), lambda s, i, c: (s, i, 0))],
            out_specs=pl.BlockSpec((1, br, D), lambda s, i, c: (s, i, 0))),
        out_shape=jax.ShapeDtypeStruct((NSHARD, h, D), BF),
    )(cidx, g, recv)


def _chip_sum(name, place, pb, y, br):
    h = y.shape[1]
    nb = h // br

    def body(pl_ref, p_ref, y_ref, out_ref):
        acc = p_ref[0].astype(F32)
        for k in range(NSHARD - 1):
            acc = acc + y_ref[k].astype(F32)
        out_ref[...] = acc

    return pl.pallas_call(
        body, name=name,
        grid_spec=pltpu.PrefetchScalarGridSpec(
            num_scalar_prefetch=1, grid=(nb,),
            in_specs=[pl.BlockSpec((1, br, D), lambda i, s: (s[0], i, 0)),
                      pl.BlockSpec((NSHARD - 1, br, D), lambda i, s: (0, i, 0))],
            out_specs=pl.BlockSpec((br, D), lambda i, s: (s[1] * nb + i, 0))),
        out_shape=jax.ShapeDtypeStruct((2 * h, D), F32),
    )(place, pb, y)


def _sum8(name, place, g, land, br):
    h = land.shape[1]
    nb = h // br

    def body(pl_ref, g_ref, y_ref, out_ref):
        acc = g_ref[0].astype(F32)
        for k in range(land.shape[0]):
            acc = acc + y_ref[k].astype(F32)
        out_ref[...] = acc

    return pl.pallas_call(
        body, name=name,
        grid_spec=pltpu.PrefetchScalarGridSpec(
            num_scalar_prefetch=1, grid=(nb,),
            in_specs=[pl.BlockSpec((1, br, D), lambda i, s: (s[0], s[1] * nb + i, 0)),
                      pl.BlockSpec((land.shape[0], br, D), lambda i, s: (0, i, 0))],
            out_specs=pl.BlockSpec((br, D), lambda i, s: (s[1] * nb + i, 0))),
        out_shape=jax.ShapeDtypeStruct((2 * h, D), F32), compiler_params=_cp(),
    )(place, g, land)


def _sum_slots(name, v):
    def body(in_ref, out_ref):
        acc = in_ref[0]
        for k in range(1, 8):
            acc = acc + in_ref[k]
        out_ref[...] = acc

    vm = pl.BlockSpec(memory_space=pltpu.VMEM)
    return pl.pallas_call(body, name=name, in_specs=[vm], out_specs=vm,
                          out_shape=jax.ShapeDtypeStruct(v.shape[1:], F32))(v)


def _adamw_math(w, g, m, v):
    m = ADAM_B1 * m + (1.0 - ADAM_B1) * g
    v = ADAM_B2 * v + (1.0 - ADAM_B2) * (g * g)
    m_hat = m / (1.0 - ADAM_B1 ** ADAM_STEP)
    v_hat = v / (1.0 - ADAM_B2 ** ADAM_STEP)
    delta = -ADAM_LR * (m_hat / (jnp.sqrt(v_hat) + ADAM_EPS) + ADAM_WD * w)
    return delta, m, v


def _adamw(w, g, m, v, dep, row0=None):
    r, c = w.shape
    br = next(b for b in (256, 176, 128, r) if r % b == 0 and (row0 or 0) % b == 0)
    first = (row0 or 0) // br

    def body(w_ref, g_ref, m_ref, v_ref, dep_ref, *outs):
        gv = g_ref[...]
        outs[-3][...], outs[-2][...], outs[-1][...] = _adamw_math(w_ref[...], gv, m_ref[...], v_ref[...])
        if row0 is not None:
            outs[0][...] = gv

    spec = pl.BlockSpec((br, c), lambda i: (i, 0))
    n_out = 3 if row0 is None else 4
    out = pl.pallas_call(
        body, name="adamw", grid=(r // br,),
        in_specs=[spec, pl.BlockSpec((br, c), lambda i: (first + i, 0)), spec, spec, ANY], out_specs=[spec] * n_out,
        out_shape=[jax.ShapeDtypeStruct((r, c), F32)] * n_out, compiler_params=_cp(),
    )(w, g, m, v, dep)
    return out if row0 is not None else [g] + list(out)


def _adamw_small(ws, gs, ms, vs):
    n = len(ws)

    def body(*refs):
        ins, outs = refs[:4 * n], refs[4 * n:]
        for k in range(n):
            d, m, v = _adamw_math(ins[k][...], ins[n + k][...], ins[2 * n + k][...], ins[3 * n + k][...])
            outs[k][...] = d
            outs[n + k][...] = m
            outs[2 * n + k][...] = v

    vm = pl.BlockSpec(memory_space=pltpu.VMEM)
    out = pl.pallas_call(
        body, name="adamw_small", in_specs=[vm] * (4 * n), out_specs=[vm] * (3 * n),
        out_shape=[jax.ShapeDtypeStruct(w.shape, F32) for w in ws] * 3,
    )(*ws, *gs, *ms, *vs)
    return out[:n], out[n:2 * n], out[2 * n:]


BIG = ("w_in", "w_out", "w_ffn_gate", "w_ffn_up", "w_ffn_down", "w_ple_proj", "w_ple_gate")
SMALL = ("g_mix_pre", "b_forget", "g_attn_grp", "g_pool_grp", "w_pool", "pool_scale", "g_mix_post", "g_ffn_pre",
         "g_ffn_post", "g_ple")
TRANSPOSED = ("w_in", "w_ffn_gate", "w_ffn_up")
VECTORS = tuple(n for n in SMALL if n != "w_pool")
ORDER = ("g_mix_pre", "w_in", "b_forget", "g_attn_grp", "g_pool_grp", "w_pool", "pool_scale", "w_out", "g_mix_post",
         "g_ffn_pre", "w_ffn_gate", "w_ffn_up", "w_ffn_down", "g_ffn_post", "w_ple_proj", "g_ple", "w_ple_gate")


def _pad_rows(a, rows):
    return jnp.pad(a, ((0, rows - a.shape[0]), (0, 0)))


def _stack1(w_in, w_out):
    return _pad_rows(jnp.concatenate([_pad_rows(w_in.T, IN_PAD), w_out], axis=0), ROWS1)


def _stack2(wg, wu, wd, wple, wpg):
    return _pad_rows(jnp.concatenate([wg.T, wu.T, wd, wple.reshape(DPLE // NSHARD, D), wpg], axis=0), ROWS2)


def _unstack1(s):
    return s[:IN_SH], s[O1_OUT:USED1]


def _cat(g, lo, hi):
    return g[:, lo:hi].reshape(NSHARD * (hi - lo), D)


def _unstack1_full(g):
    w_in_t = _cat(g, 0, IN_SH)
    w_in_t = jnp.concatenate([w_in_t[:3 * DA], _pad_rows(w_in_t[3 * DA:3 * DA + NH], LANE), w_in_t[3 * DA + NH:]], axis=0)
    return w_in_t, _cat(g, O1_OUT, USED1)


def _unstack2_full(g):
    w_ple = g[:, O2_PLE:O2_PG].reshape(NSHARD, DPLE, DPLE).transpose(1, 0, 2).reshape(DPLE, D)
    return _cat(g, 0, O2_U), _cat(g, O2_U, O2_D), _cat(g, O2_D, O2_PLE), w_ple, _cat(g, O2_PG, USED2)


def _shards(a):
    return a.reshape(NSHARD, a.shape[0] // NSHARD, D)


def _stack1_full(dwin_t, dwout):
    dwin_t = jnp.concatenate([dwin_t[:3 * DA + NH], dwin_t[3 * DA + LANE:]], axis=0).reshape(NSHARD, IN_SH, D)
    zeros = lambda r: jnp.zeros((NSHARD, r, D), F32)
    return jnp.concatenate([dwin_t, zeros(IN_PAD - IN_SH), _shards(dwout), zeros(ROWS1 - USED1)], axis=1)


def _stack2_full(dwg_t, dwu_t, dwd, dwple, dwpg):
    dwple = dwple.reshape(DPLE, NSHARD, DPLE).transpose(1, 0, 2).reshape(NSHARD, DPLE // NSHARD, D)
    return jnp.concatenate([_shards(dwg_t), _shards(dwu_t), _shards(dwd), dwple, _shards(dwpg),
                            jnp.zeros((NSHARD, ROWS2 - USED2, D), dwd.dtype)], axis=1)


def _sds(shape, dtype):
    return jax.ShapeDtypeStruct(shape, dtype)


class _Comm:
    def __init__(self, stack2, me, c):
        self.stack2, self.me, self.c = stack2, me, c
        self.cidx = c.astype(jnp.int32).reshape(1)
        self.place = jnp.stack([me, c]).astype(jnp.int32)
        self.h = ROWS2 // 2
        self.gather = _exchange_start("gather2_start", 3, stack2, lax.empty((NSHARD, ROWS2, D), BF), _plan_gather(self.h))
        self.dep = self.gather[4]

    def after_attention(self, after):
        own, land = _exchange_wait("gather2_wait", self.gather, _plan_gather(self.h), after)
        fwd = _plan_forward(self.h)
        self.forward = lambda own_ref, land_ref: fwd(land_ref, own_ref, land_ref)
        self.passing = _exchange_start("forward2_start", 4, own, land, self.forward)
        return self.passing[4]

    def weights2(self, after):
        return _exchange_wait("forward2_wait", self.passing, self.forward, after)[1]

    def after_ffn(self, grads2):
        g = _stack2_full(*[a.astype(BF) for a in grads2])
        self.scatter = _plan_scatter8(self.h)
        self.chip = _exchange_start("reduce2_start", 7, g, lax.empty((7, self.h, D), BF), self.scatter)
        return self.chip[4]

    def after_mix(self, after, early):
        self.early_shapes = {n: early[n].shape for n in early}
        self.small = self.start_small("small", early)
        return self.small[4]

    def after_attn(self, after):
        g, y = _exchange_wait("reduce2_wait", self.chip, self.scatter, after)
        f = _sum8("sum2", self.place, g, y, RED2 // 2)
        self.reduced2 = _exchange_inplace("reduce2_gather", 1, f, (), _plan_swap_halves(self.h))
        self.early = self.finish_small("small", self.small, self.early_shapes, after)
        return self.reduced2

    def start_small(self, name, small):
        v = _pack_small(small)
        return _exchange_start(name + "_start", 7, v, lax.empty((8,) + v.shape, F32), _plan_all)

    def finish_small(self, name, started, shapes, after):
        v, land = _exchange_wait(name + "_wait", started, _plan_all, after)
        land = lax.dynamic_update_slice(land, v[None], (2 * self.me + self.c, 0, 0))
        return _unpack_small(_sum_slots(name + "_sum", land), shapes)


def _pack_small(small):
    parts = []
    for name in small:
        flat = small[name].reshape(-1)
        parts.append(jnp.pad(flat, (0, -flat.shape[0] % LANE)).reshape(-1, LANE))
    v = jnp.concatenate(parts, axis=0)
    return _pad_rows(v, v.shape[0] + (-v.shape[0] % 8))


def _unpack_small(v, shapes):
    out, r = {}, 0
    for name in shapes:
        n = math.prod(shapes[name])
        rows = -(-n // LANE)
        out[name] = v[r:r + rows].reshape(-1)[:n].reshape(shapes[name])
        r += rows
    return out


def kernel(x, p, g_mix_pre, w_in, b_forget, g_attn_grp, g_pool_grp, w_pool, pool_scale, w_out, g_mix_post, g_ffn_pre, w_ffn_gate, w_ffn_up, w_ffn_down, g_ffn_post, w_ple_proj, g_ple, w_ple_gate, loss_target, m_g_mix_pre, m_w_in, m_b_forget, m_g_attn_grp, m_g_pool_grp, m_w_pool, m_pool_scale, m_w_out, m_g_mix_post, m_g_ffn_pre, m_w_ffn_gate, m_w_ffn_up, m_w_ffn_down, m_g_ffn_post, m_w_ple_proj, m_g_ple, m_w_ple_gate, v_g_mix_pre, v_w_in, v_b_forget, v_g_attn_grp, v_g_pool_grp, v_w_pool, v_pool_scale, v_w_out, v_g_mix_post, v_g_ffn_pre, v_w_ffn_gate, v_w_ffn_up, v_w_ffn_down, v_g_ffn_post, v_w_ple_proj, v_g_ple, v_w_ple_gate):
    args = dict(locals())
    strip = lambda n, a: a if n in VECTORS else a[0]
    w = {n: strip(n, args[n]) for n in ORDER}
    mom = {n: strip(n, args["m_" + n]) for n in ORDER}
    var = {n: strip(n, args["v_" + n]) for n in ORDER}
    sm = {n: w[n] for n in SMALL}

    c = lax.axis_index("c")
    me = 2 * lax.axis_index("x") + lax.axis_index("y")
    h1 = ROWS1 // 2
    bf = lambda n: w[n].astype(BF)
    stack1 = _stack1(bf("w_in"), bf("w_out"))
    stack2 = _stack2(*[bf(n) for n in BIG[2:]])
    land = _exchange("gather1", 3, stack1, _sds((NSHARD, ROWS1, D), BF), _plan_gather(h1))
    land, stack2 = lax.optimization_barrier((land, stack2))
    comm = _Comm(stack2, me, c)
    w1 = _unstack1_full(_exchange_inplace("gather1_forward", 4, land, (stack1,), _plan_forward(h1)))
    dx, grads1, late = _local_step(x[0], p[0, 0], loss_target[0], sm, w1, comm)

    flip = lambda n, a: a.T if n in TRANSPOSED else a
    grads, delta, new_m, new_v = {}, {}, {}, {}

    def update(n, g, dep, row0=None):
        g_, d_, m_, v_ = _adamw(flip(n, w[n]), g, flip(n, mom[n]), flip(n, var[n]), dep, row0)
        grads[n], delta[n], new_m[n], new_v[n] = flip(n, g_), flip(n, d_), flip(n, m_), flip(n, v_)
        return v_

    late_shapes = {n: late[n].shape for n in late}
    small2 = comm.start_small("small2", late)
    red2 = comm.reduced2
    g1 = _stack1_full(*grads1)
    pair1 = _exchange_start("reduce1_pair_start", 1, g1, lax.empty((NSHARD, h1, D), F32), _plan_pair_rows(h1))
    dep = update("w_ple_gate", red2, pair1[4] + small2[4], O2_PG)
    dep = update("w_ple_proj", red2[O2_PLE:O2_PG].reshape(DPLE, DPLE), dep)
    g1, recv = _exchange_wait("reduce1_pair_wait", pair1, _plan_pair_rows(h1), dep)
    pb = _pair_sum("pair_sum1", comm.cidx, g1, recv, RED1)
    chip1 = _exchange_start("reduce1_chip_start", 3, pb, lax.empty((NSHARD - 1, h1, D), BF), _plan_scatter)
    dep = update("w_ffn_gate", red2, chip1[4], 0)
    dep = update("w_ffn_up", red2, dep, O2_U)
    dep = update("w_ffn_down", red2, dep, O2_D)
    red_small = {**comm.early, **comm.finish_small("small2", small2, late_shapes, dep)}
    loss = 0.5 / D * red_small["loss"][0, 0]
    for n in SMALL:
        grads[n] = red_small[n].reshape(w[n].shape)
    two_d = lambda a: a.reshape(-1, a.shape[-1])
    ds, ms, vs = _adamw_small([two_d(w[n]) for n in SMALL], [two_d(grads[n]) for n in SMALL],
                              [two_d(mom[n]) for n in SMALL], [two_d(var[n]) for n in SMALL])
    for k, n in enumerate(SMALL):
        delta[n], new_m[n], new_v[n] = ds[k].reshape(w[n].shape), ms[k].reshape(w[n].shape), vs[k].reshape(w[n].shape)
    pb, y = _exchange_wait("reduce1_chip_wait", chip1, _plan_scatter, vs[0])
    f = _chip_sum("chip_sum1", comm.place, pb, y, RED1)
    reduced1 = _exchange_inplace("reduce1_gather", 1, f, (), _plan_swap_halves(h1))
    g_in, g_out = _unstack1(reduced1)
    update("w_out", g_out, update("w_in", g_in, reduced1))

    lead = lambda d: [d[n] if n in VECTORS else d[n][None] for n in ORDER]
    return (loss, dx[None], *lead(grads), *lead(delta), *lead(new_m), *lead(new_v))
```

```python
import functools
import math

import jax
import jax.numpy as jnp
import numpy as np
from jax import lax
from jax.experimental import pallas as pl
from jax.experimental.pallas import tpu as pltpu

F32 = jnp.float32
BF = jnp.bfloat16
MESH = pl.DeviceIdType.MESH

D = 1024
DA = 512
DP = 512
NH = 8
HD = 64
DFF = 2816
DPLE = 256
WINS = (2, 4, 8, 16)
PC = 128
ZW = 3 * DA + 128 + DP
EPS = 1e-6
NSHARD = 4

LANE = 128
HALO = 128

IN_SH = 514
IN_PAD = 528
FF_SH = DFF // NSHARD
O1_OUT, USED1, ROWS1 = 528, 784, 800
O2_U, O2_D, O2_PLE, O2_PG, USED2, ROWS2 = 704, 1408, 2112, 2176, 2432, 2560
RED1, RED2 = 400, 640

ADAM_LR, ADAM_B1, ADAM_B2, ADAM_EPS, ADAM_WD, ADAM_STEP = 0.001, 0.9, 0.999, 1e-8, 0.01, 10

VMEM_LIMIT = 56 * 1024 * 1024


def _cp(**kw):
    return pltpu.CompilerParams(vmem_limit_bytes=VMEM_LIMIT, **kw)


def _mm(a, b):
    return jnp.dot(a.astype(BF), b.astype(BF), preferred_element_type=F32)


def _mm_nt(a, b):
    return lax.dot_general(a.astype(BF), b.astype(BF), (((1,), (1,)), ((), ())), preferred_element_type=F32)


def _mm_tn(a, b):
    return lax.dot_general(a.astype(BF), b.astype(BF), (((0,), (0,)), ((), ())), preferred_element_type=F32)


def _split3(x):
    hi = x.astype(BF)
    r = x - hi.astype(F32)
    mid = r.astype(BF)
    lo = (r - mid.astype(F32)).astype(BF)
    return hi, mid, lo


def _dot3(m, x):
    hi, mid, lo = _split3(x)
    return (jnp.dot(m, hi, preferred_element_type=F32) + jnp.dot(m, mid, preferred_element_type=F32)
            + jnp.dot(m, lo, preferred_element_type=F32))


def _window_sum(ext, w, back):
    n = ext.shape[0]
    s, k = ext, 1
    while k < w:
        s = s + pltpu.roll(s, k if back else n - k, 0)
        k *= 2
    return s


def _rstd(x):
    return lax.rsqrt(jnp.mean(x * x, axis=-1, keepdims=True) + EPS)


def _rms_bwd(dy, x, g):
    r = _rstd(x)
    xh = x * r
    dg = jnp.sum(dy * xh, axis=0, keepdims=True)
    dxh = dy * g
    dx = r * (dxh - xh * jnp.mean(dxh * xh, axis=-1, keepdims=True))
    return dx, dg


def _sigmoid(x):
    return 1.0 / (1.0 + jnp.exp(-x))


ANY = pl.BlockSpec(memory_space=pl.ANY)


def _full(shape):
    n = len(shape)
    return pl.BlockSpec(shape, lambda *_: (0,) * n)


def _resident(shape):
    n = len(shape)
    return pl.BlockSpec(shape, lambda *_: (0,) * n, pipeline_mode=pl.Buffered(1))


def _tile(t):
    return 512 if t % 512 == 0 else t


def _tri(n, upper):
    r, c = np.indices((n, n))
    return ((c >= r) if upper else (c <= r)).astype(BF)


def _aug_consts():
    row, col = np.indices((3 * LANE, NH * LANE))
    piece, head = row // LANE, row % LANE
    ch, cl = col // LANE, col % LANE
    eq = ((head == ch) & (cl == HD + 8 * piece + head)).astype(BF)
    ek = -((head == ch) & (cl == HD + 24 + 8 * piece + head)).astype(BF)
    off = (np.arange(NH * LANE) % LANE - HD - np.arange(NH * LANE) // LANE)[None, :]
    rowq = ((off >= 24) & (off < 48) & (off % 8 == 0)).astype(np.float32)
    rowk = ((off >= 0) & (off < 24) & (off % 8 == 0)).astype(np.float32)
    return eq, ek, rowq, rowk


def _in_proj(x, g1, w_in_t, b_pad, tri, eq, ek, rowq, rowk, dep):
    t = x.shape[0]
    tt = _tile(t)

    def body(x_ref, g_ref, w_ref, b_ref, tri_ref, eq_ref, ek_ref, rq_ref, rk_ref, dep_ref,
             qa_ref, ka_ref, v_ref, u_ref, fl_ref, carry):
        i = pl.program_id(0)

        @pl.when(i == 0)
        def _():
            carry[...] = jnp.zeros_like(carry)

        xv = x_ref[...]
        hn = (xv * _rstd(xv) * g_ref[...]).astype(BF)
        z = _mm_nt(hn, w_ref[...])
        fl = z[:, 3 * DA:3 * DA + LANE] + b_ref[...]
        lane = lax.broadcasted_iota(jnp.int32, fl.shape, 1)
        lf = jnp.where(lane < NH, jnp.minimum(fl, 0.0) - jnp.log(1.0 + jnp.exp(-jnp.abs(fl))), 0.0)
        c = carry[...] + _dot3(tri_ref[...], lf)
        carry[...] = carry[...] + jnp.sum(lf, axis=0, keepdims=True)
        caug = jnp.concatenate(_split3(c), axis=1)
        aug_q = jnp.dot(caug, eq_ref[...], preferred_element_type=F32) + rq_ref[...]
        aug_k = jnp.dot(caug, ek_ref[...], preferred_element_type=F32) + rk_ref[...]
        low = lax.broadcasted_iota(jnp.int32, (tt, LANE), 1) < HD
        for p in range(NH // 2):
            qp = z[:, LANE * p:LANE * (p + 1)] * (1.0 / math.sqrt(HD))
            kp = z[:, DA + LANE * p:DA + LANE * (p + 1)]
            for h, (qh, kh) in enumerate(((qp, kp), (pltpu.roll(qp, HD, 1), pltpu.roll(kp, HD, 1)))):
                lo_, hi_ = LANE * (2 * p + h), LANE * (2 * p + h + 1)
                qa_ref[:, lo_:hi_] = jnp.where(low, qh, aug_q[:, lo_:hi_]).astype(BF)
                ka_ref[:, lo_:hi_] = jnp.where(low, kh, aug_k[:, lo_:hi_]).astype(BF)
        v_ref[...] = z[:, 2 * DA:3 * DA].astype(BF)
        u_ref[...] = z[:, 3 * DA + LANE:]
        fl_ref[...] = fl

    return pl.pallas_call(
        body, name="in_proj", grid=(t // tt,),
        in_specs=[pl.BlockSpec((tt, D), lambda i: (i, 0)), _full((1, D)), _resident((ZW, D)), _full((1, LANE)),
                  _full((tt, tt)), _full((3 * LANE, NH * LANE)), _full((3 * LANE, NH * LANE)),
                  _full((1, NH * LANE)), _full((1, NH * LANE)), ANY],
        out_specs=[pl.BlockSpec((tt, NH * LANE), lambda i: (i, 0)), pl.BlockSpec((tt, NH * LANE), lambda i: (i, 0)),
                   pl.BlockSpec((tt, DA), lambda i: (i, 0)), pl.BlockSpec((tt, DP), lambda i: (i, 0)),
                   pl.BlockSpec((tt, LANE), lambda i: (i, 0))],
        out_shape=[jax.ShapeDtypeStruct((t, NH * LANE), BF), jax.ShapeDtypeStruct((t, NH * LANE), BF),
                   jax.ShapeDtypeStruct((t, DA), BF), jax.ShapeDtypeStruct((t, DP), F32),
                   jax.ShapeDtypeStruct((t, LANE), F32)],
        scratch_shapes=[pltpu.VMEM((1, LANE), F32)],
        compiler_params=_cp(),
    )(x, g1, w_in_t, b_pad, tri, eq, ek, rowq, rowk, dep)


def _attn_fwd(qa, ka, v):
    t = qa.shape[0]
    ta = _tile(t)
    n = t // ta

    def body(q_ref, k_ref, v_ref, a_ref, lse_ref, m_ref, l_ref, acc_ref):
        i = pl.program_id(1)
        m_ref[...] = jnp.full_like(m_ref, -1e30)
        l_ref[...] = jnp.zeros_like(l_ref)
        acc_ref[...] = jnp.zeros_like(acc_ref)
        qs = [q_ref[:, LANE * h:LANE * (h + 1)] for h in range(2)]
        reps = ta // LANE

        def tile(j, width, masked):
            rows = pl.ds(pl.multiple_of(j * ta, ta), width * ta)
            v2 = v_ref[rows, :]
            s = [_mm_nt(qs[h], k_ref[rows, LANE * h:LANE * (h + 1)]) for h in range(2)]
            if masked:
                keep = (lax.broadcasted_iota(jnp.int32, (ta, ta), 1) <= lax.broadcasted_iota(jnp.int32, (ta, ta), 0))
                s = [jnp.where(keep, sh, -1e30) for sh in s]
            m_old = [m_ref[h] for h in range(2)]
            m_new = [jnp.maximum(m_old[h], jnp.max(s[h], axis=1, keepdims=True)) for h in range(2)]
            pe = [jnp.exp(s[h] - jnp.tile(m_new[h], (1, width * reps))) for h in range(2)]
            alpha = [jnp.exp(m_old[h] - m_new[h]) for h in range(2)]
            pv = [jnp.dot(pe[h].astype(BF), v2, preferred_element_type=F32) for h in range(2)]
            for h in range(2):
                l_ref[h] = alpha[h] * l_ref[h] + jnp.sum(pe[h], axis=1, keepdims=True)
                acc_ref[h] = alpha[h] * acc_ref[h] + pv[h]
                m_ref[h] = m_new[h]

        def step(jj, carry):
            tile(2 * jj, 2, False)
            return carry

        lax.fori_loop(0, i // 2, step, 0)

        @pl.when(i % 2 == 1)
        def _():
            tile(i - 1, 1, False)

        tile(i, 1, True)
        low = lax.broadcasted_iota(jnp.int32, (ta, LANE), 1) < HD
        a_ref[...] = jnp.where(low, acc_ref[0] / l_ref[0], acc_ref[1] / l_ref[1])
        lse_ref[...] = jnp.where(low, m_ref[0] + jnp.log(l_ref[0]), m_ref[1] + jnp.log(l_ref[1]))

    return pl.pallas_call(
        body, name="attn_fwd", grid=(NH // 2, n),
        in_specs=[pl.BlockSpec((ta, 2 * LANE), lambda p, i: (i, p)),
                  pl.BlockSpec((t, 2 * LANE), lambda p, i: (0, p)),
                  pl.BlockSpec((t, LANE), lambda p, i: (0, p))],
        out_specs=[pl.BlockSpec((ta, LANE), lambda p, i: (i, p)), pl.BlockSpec((ta, LANE), lambda p, i: (i, p))],
        out_shape=[jax.ShapeDtypeStruct((t, DA), F32), jax.ShapeDtypeStruct((t, DA), F32)],
        scratch_shapes=[pltpu.VMEM((2, ta, LANE), F32), pltpu.VMEM((2, ta, LANE), F32), pltpu.VMEM((2, ta, LANE), F32)],
        compiler_params=_cp(),
    )(qa, ka, v)


def _mix_out(a, u, x, w_pool, pool_scale, g_attn, g_pool, w_out, g_post, dep):
    t = a.shape[0]
    tt = _tile(t)
    hb = tt // HALO

    def body(a_ref, u_ref, up_ref, x_ref, wp_ref, ps_ref, ga_ref, gp_ref, wo_ref, go_ref, dep_ref,
             yb_ref, m_ref, o_ref, h1_ref):
        i = pl.program_id(0)
        prev = up_ref[...] * jnp.where(i > 0, 1.0, 0.0)
        tok = i * tt + lax.broadcasted_iota(jnp.int32, (tt, PC), 0)
        ms = []
        for g, w in enumerate(WINS):
            ug = u_ref[:, PC * g:PC * (g + 1)]
            ext = jnp.concatenate([prev[:, PC * g:PC * (g + 1)], ug], axis=0)
            cnt = jnp.minimum(tok + 1, w).astype(F32)
            y = (_window_sum(ext, w, True)[HALO:] / cnt - ug).astype(BF)
            yb_ref[:, PC * g:PC * (g + 1)] = y
            ms.append(_mm(y, wp_ref[g]) * ps_ref[:, PC * g:PC * (g + 1)])
        m = jnp.concatenate(ms, axis=1)
        m_ref[...] = m
        av = a_ref[...]
        mix = jnp.concatenate([av * _rstd(av) * ga_ref[...], m * _rstd(m) * gp_ref[...]], axis=1)
        o = _mm(mix, wo_ref[...])
        o_ref[...] = o
        h1_ref[...] = x_ref[...] + o * _rstd(o) * go_ref[...]

    return pl.pallas_call(
        body, name="mix_out", grid=(t // tt,),
        in_specs=[pl.BlockSpec((tt, DA), lambda i: (i, 0)), pl.BlockSpec((tt, DP), lambda i: (i, 0)),
                  pl.BlockSpec((HALO, DP), lambda i: (jnp.maximum(i * hb - 1, 0), 0)),
                  pl.BlockSpec((tt, D), lambda i: (i, 0)),
                  _full((len(WINS), PC, PC)), _full((1, DP)), _full((1, DA)), _full((1, DP)),
                  _resident((D, D)), _full((1, D)), ANY],
        out_specs=[pl.BlockSpec((tt, DP), lambda i: (i, 0)), pl.BlockSpec((tt, DP), lambda i: (i, 0)),
                   pl.BlockSpec((tt, D), lambda i: (i, 0)), pl.BlockSpec((tt, D), lambda i: (i, 0))],
        out_shape=[jax.ShapeDtypeStruct((t, DP), BF), jax.ShapeDtypeStruct((t, DP), F32),
                   jax.ShapeDtypeStruct((t, D), F32), jax.ShapeDtypeStruct((t, D), F32)],
        compiler_params=_cp(),
    )(a, u, u, x, w_pool, pool_scale, g_attn, g_pool, w_out, g_post, dep)


def _ffn_fwd(h1, g_pre, stacks2, g_post, p, w_ple, g_ple, w_pg, tgt):
    t = h1.shape[0]
    tt = 256 if t % 256 == 0 else t

    def body(h1_ref, gpre_ref, wg_ref, wu_ref, wd_ref, gpost_ref, p_ref, wple_ref, gple_ref, wpg_ref, tgt_ref,
             hn_ref, gate_ref, up_ref, dff_ref, dh2_ref, loss_ref, dwpg_ref, dwple_ref, dgple_ref, dgpost_ref):
        i = pl.program_id(0)

        @pl.when(i == 0)
        def _():
            loss_ref[...] = jnp.zeros_like(loss_ref)
            dwpg_ref[...] = jnp.zeros_like(dwpg_ref)
            dwple_ref[...] = jnp.zeros_like(dwple_ref)
            dgple_ref[...] = jnp.zeros_like(dgple_ref)
            dgpost_ref[...] = jnp.zeros_like(dgpost_ref)

        h1v = h1_ref[...]
        hn = (h1v * _rstd(h1v) * gpre_ref[...]).astype(BF)
        hn_ref[...] = hn
        gate = _mm_nt(hn, wg_ref[...].reshape(DFF, D))
        up = _mm_nt(hn, wu_ref[...].reshape(DFF, D))
        for k in range(DFF // FF_CH):
            gate_ref[k] = gate[:, FF_CH * k:FF_CH * (k + 1)].astype(BF)
            up_ref[k] = up[:, FF_CH * k:FF_CH * (k + 1)].astype(BF)
        ff = _mm(gate * _sigmoid(gate) * up, wd_ref[...].reshape(DFF, D))
        rff = _rstd(ff)
        ffh = ff * rff
        gpost = gpost_ref[...]
        h2 = h1v + ffh * gpost
        pv = p_ref[...]
        pe = _mm(pv, wple_ref[...])
        rpe = _rstd(pe)
        peh = pe * rpe
        gple = gple_ref[...]
        e = peh * gple
        sig = _sigmoid(_mm(h2, wpg_ref[...]))
        dv = h2 + sig * e - tgt_ref[...]
        sq = jnp.sum(jnp.sum(dv * dv, axis=1, keepdims=True), axis=0, keepdims=True)
        loss_ref[...] = loss_ref[...] + sq
        dy = dv * (1.0 / D)
        d_e = dy * sig
        d_gl = dy * e * sig * (1.0 - sig)
        dh2 = dy + _mm_nt(d_gl, wpg_ref[...])
        dh2_ref[...] = dh2
        dwpg_ref[...] = dwpg_ref[...] + _mm_tn(h2, d_gl)
        dgple_ref[...] = dgple_ref[...] + jnp.sum(d_e * peh, axis=0, keepdims=True)
        dpeh = d_e * gple
        d_pe = rpe * (dpeh - peh * jnp.mean(dpeh * peh, axis=-1, keepdims=True))
        dwple_ref[...] = dwple_ref[...] + _mm_tn(pv, d_pe)
        dgpost_ref[...] = dgpost_ref[...] + jnp.sum(dh2 * ffh, axis=0, keepdims=True)
        dffh = dh2 * gpost
        dff_ref[...] = (rff * (dffh - ffh * jnp.mean(dffh * ffh, axis=-1, keepdims=True))).astype(BF)

    row = lambda w: pl.BlockSpec((tt, w), lambda i: (i, 0))
    chunked = pl.BlockSpec((DFF // FF_CH, tt, FF_CH), lambda i: (0, i, 0))
    shard_rows = lambda k: pl.BlockSpec((NSHARD, FF_SH, D), lambda i: (0, k, 0), pipeline_mode=pl.Buffered(1))
    return pl.pallas_call(
        body, name="ffn_fwd", grid=(t // tt,),
        in_specs=[row(D), _full((1, D)), shard_rows(0), shard_rows(1), shard_rows(2), _full((1, D)),
                  row(DPLE), _resident((DPLE, D)), _full((1, D)), _resident((D, D)), row(D)],
        out_specs=[row(D), chunked, chunked, row(D), row(D), _full((8, LANE)), _full((D, D)), _full((DPLE, D)),
                   _full((1, D)), _full((1, D))],
        out_shape=[jax.ShapeDtypeStruct((t, D), BF), jax.ShapeDtypeStruct((DFF // FF_CH, t, FF_CH), BF),
                   jax.ShapeDtypeStruct((DFF // FF_CH, t, FF_CH), BF),
                   jax.ShapeDtypeStruct((t, D), BF), jax.ShapeDtypeStruct((t, D), F32), jax.ShapeDtypeStruct((8, LANE), F32),
                   jax.ShapeDtypeStruct((D, D), F32), jax.ShapeDtypeStruct((DPLE, D), F32),
                   jax.ShapeDtypeStruct((1, D), F32), jax.ShapeDtypeStruct((1, D), F32)],
        compiler_params=_cp(),
    )(h1, g_pre, stacks2, stacks2, stacks2, g_post, p, w_ple, g_ple, w_pg, tgt)


FF_CH = 256


def _ffn_bwd(hn2, gate, up, dff, wgu, wd):
    t = hn2.shape[0]
    tt = 1024 if t % 1024 == 0 else _tile(t)
    nt = t // tt
    ch = FF_CH
    nc = DFF // ch

    def body(hn_ref, gate_ref, up_ref, dff_ref, wgu_ref, wd_ref,
             dwg_ref, dwu_ref, dwd_ref, dhn_ref, acc, gu_acc, d_acc, sem):
        j, i = pl.program_id(0), pl.program_id(1)

        @pl.when(j == 0)
        def _():
            acc[pl.ds(pl.multiple_of(i * tt, tt), tt), :] = jnp.zeros((tt, D), F32)

        @pl.when(i == 0)
        def _():
            gu_acc[...] = jnp.zeros_like(gu_acc)
            d_acc[...] = jnp.zeros_like(d_acc)

        parts = 4 if tt % 1024 == 0 else 2
        half = tt // parts
        acts, dgus = [], []
        for hh in range(parts):
            r = slice(hh * half, (hh + 1) * half)
            gate_v = gate_ref[0, r, :].astype(F32)
            up_v = up_ref[0, r, :].astype(F32)
            sg = _sigmoid(gate_v)
            silu = gate_v * sg
            d_act = _mm_nt(dff_ref[r, :], wd_ref[...])
            d_up = (d_act * silu).astype(BF)
            d_gate = (d_act * up_v * (sg * (1.0 + gate_v * (1.0 - sg)))).astype(BF)
            dgu = jnp.concatenate([d_gate, d_up], axis=1)
            rows = pl.ds(pl.multiple_of(i * tt + hh * half, half), half)
            acc[rows, :] = acc[rows, :] + jnp.dot(dgu, wgu_ref[0], preferred_element_type=F32)
            acts.append((silu * up_v).astype(BF))
            dgus.append(dgu)
        d_acc[...] = d_acc[...] + _mm_tn(jnp.concatenate(acts, axis=0), dff_ref[...])
        gu_acc[...] = gu_acc[...] + _mm_tn(jnp.concatenate(dgus, axis=0), hn_ref[...])

        @pl.when(i == nt - 1)
        def _():
            dwg_ref[...] = gu_acc[:ch].astype(BF)
            dwu_ref[...] = gu_acc[ch:].astype(BF)
            dwd_ref[...] = d_acc[...].astype(BF)

        @pl.when((j == nc - 1) & (i == nt - 1))
        def _():
            cp = pltpu.make_async_copy(acc, dhn_ref, sem)
            cp.start()
            cp.wait()

    tok = lambda w: pl.BlockSpec((tt, w), lambda j, i: (i, 0))
    chunk = pl.BlockSpec((ch, D), lambda j, i: (j, 0))
    pair = pl.BlockSpec((1, 2 * ch, D), lambda j, i: (j, 0, 0))
    return pl.pallas_call(
        body, name="ffn_bwd", grid=(nc, nt),
        in_specs=[tok(D), pl.BlockSpec((1, tt, ch), lambda j, i: (j, i, 0)), pl.BlockSpec((1, tt, ch), lambda j, i: (j, i, 0)),
                  tok(D), pair, chunk],
        out_specs=[chunk, chunk, chunk, pl.BlockSpec(memory_space=pl.ANY)],
        out_shape=[jax.ShapeDtypeStruct((DFF, D), BF), jax.ShapeDtypeStruct((DFF, D), BF),
                   jax.ShapeDtypeStruct((DFF, D), BF), jax.ShapeDtypeStruct((t, D), F32)],
        scratch_shapes=[pltpu.VMEM((t, D), F32), pltpu.VMEM((2 * ch, D), F32), pltpu.VMEM((ch, D), F32),
                        pltpu.SemaphoreType.DMA],
        compiler_params=_cp(),
    )(hn2, gate, up, dff, wgu, wd)


def _mix_bwd(d_hn2, dh2, h1, o, a, m, yb, g_ffn_pre, g_post, g_attn, g_pool, w_out, w_pool, pool_scale, dep):
    t = a.shape[0]
    tt = 256 if t % 256 == 0 else t

    def body(dhn_ref, dh2_ref, h1_ref, o_ref, a_ref, m_ref, yb_ref, gfp_ref, go_ref, ga_ref, gp_ref, wo_ref, wp_ref,
             ps_ref, dep_ref, dh1_ref, da_ref, dyc_ref, dgfp_ref, dgo_ref, dga_ref, dgp_ref, dps_ref, dwp_ref, dwo_ref):
        i = pl.program_id(0)

        @pl.when(i == 0)
        def _():
            for r in (dgfp_ref, dgo_ref, dga_ref, dgp_ref, dps_ref, dwp_ref, dwo_ref):
                r[...] = jnp.zeros_like(r)

        d1, dg = _rms_bwd(dhn_ref[...], h1_ref[...], gfp_ref[...])
        dgfp_ref[...] = dgfp_ref[...] + dg
        dh1 = dh2_ref[...] + d1
        dh1_ref[...] = dh1
        d_o, dg = _rms_bwd(dh1, o_ref[...], go_ref[...])
        dgo_ref[...] = dgo_ref[...] + dg
        d_mix = _mm_nt(d_o, wo_ref[...])
        av, mv = a_ref[...], m_ref[...]
        mix = jnp.concatenate([av * _rstd(av) * ga_ref[...], mv * _rstd(mv) * gp_ref[...]], axis=1)
        dwo_ref[...] = dwo_ref[...] + _mm_tn(mix, d_o)
        d_a, dg = _rms_bwd(d_mix[:, :DA], av, ga_ref[...])
        dga_ref[...] = dga_ref[...] + dg
        da_ref[...] = d_a
        d_m, dg = _rms_bwd(d_mix[:, DA:], mv, gp_ref[...])
        dgp_ref[...] = dgp_ref[...] + dg
        tok = i * tt + lax.broadcasted_iota(jnp.int32, (tt, PC), 0)
        dps = []
        for g, w in enumerate(WINS):
            sl = slice(PC * g, PC * (g + 1))
            ybg = yb_ref[:, sl]
            wpg = wp_ref[g].astype(BF)
            mlin = jnp.dot(ybg, wpg, preferred_element_type=F32)
            dmg = d_m[:, sl]
            dps.append(jnp.sum(dmg * mlin, axis=0, keepdims=True))
            dml = (dmg * ps_ref[:, sl]).astype(BF)
            dwp_ref[g] = dwp_ref[g] + _mm_tn(ybg, dml)
            dyc_ref[:, sl] = _mm_nt(dml, wpg) / jnp.minimum(tok + 1, w).astype(F32)
        dps_ref[...] = dps_ref[...] + jnp.concatenate(dps, axis=1)

    row = lambda w: pl.BlockSpec((tt, w), lambda i: (i, 0))
    return pl.pallas_call(
        body, name="mix_bwd", grid=(t // tt,),
        in_specs=[row(D), row(D), row(D), row(D), row(DA), row(DP), row(DP), _full((1, D)), _full((1, D)),
                  _full((1, DA)), _full((1, DP)), _resident((D, D)), _full((len(WINS), PC, PC)), _full((1, DP)), ANY],
        out_specs=[row(D), row(DA), row(DP), _full((1, D)), _full((1, D)), _full((1, DA)), _full((1, DP)),
                   _full((1, DP)), _full((len(WINS), PC, PC)), _full((D, D))],
        out_shape=[jax.ShapeDtypeStruct((t, D), F32), jax.ShapeDtypeStruct((t, DA), F32), jax.ShapeDtypeStruct((t, DP), F32),
                   jax.ShapeDtypeStruct((1, D), F32), jax.ShapeDtypeStruct((1, D), F32), jax.ShapeDtypeStruct((1, DA), F32),
                   jax.ShapeDtypeStruct((1, DP), F32), jax.ShapeDtypeStruct((1, DP), F32),
                   jax.ShapeDtypeStruct((len(WINS), PC, PC), F32), jax.ShapeDtypeStruct((D, D), F32)],
        compiler_params=_cp(),
    )(d_hn2, dh2, h1, o, a, m, yb, g_ffn_pre, g_post, g_attn, g_pool, w_out, w_pool, pool_scale, dep)


def _attn_bwd(qa, ka, v, a, d_a, lse, dep):
    t = qa.shape[0]
    ta = _tile(t)
    n = t // ta

    def body(q_ref, k_ref, v_ref, o_ref, do_ref, lse_ref, dep_ref, dq_ref, dk_ref, dv_ref):
        j = pl.program_id(1)

        @pl.when(j == 0)
        def _():
            dq_ref[...] = jnp.zeros_like(dq_ref)

        dk_ref[...] = jnp.zeros_like(dk_ref)
        dv_ref[...] = jnp.zeros_like(dv_ref)
        ks = [k_ref[:, LANE * h:LANE * (h + 1)] for h in range(2)]
        v2 = v_ref[...]
        lane = lax.broadcasted_iota(jnp.int32, (ta, LANE), 1)
        mine = [lane < HD, lane >= HD]

        def tile(i, masked):
            rows = pl.ds(pl.multiple_of(i * ta, ta), ta)
            do2 = do_ref[rows, :]
            prod = do2 * o_ref[rows, :]
            lse2 = lse_ref[rows, :]
            do2b = do2.astype(BF)
            qh = [q_ref[rows, LANE * h:LANE * (h + 1)] for h in range(2)]
            s = [_mm_nt(qh[h], ks[h]) for h in range(2)]
            dp = [_mm_nt(jnp.where(mine[h], do2, 0.0), v2) for h in range(2)]
            delta = [jnp.sum(jnp.where(mine[h], prod, 0.0), axis=1, keepdims=True) for h in range(2)]
            lse_h = [jnp.sum(jnp.where(lane == HD * h, lse2, 0.0), axis=1, keepdims=True) for h in range(2)]
            pr = [jnp.exp(s[h] - lse_h[h]) for h in range(2)]
            if masked:
                keep = (lax.broadcasted_iota(jnp.int32, (ta, ta), 1) <= lax.broadcasted_iota(jnp.int32, (ta, ta), 0))
                pr = [jnp.where(keep, ph, 0.0) for ph in pr]
            ds = [(pr[h] * (dp[h] - delta[h])).astype(BF) for h in range(2)]
            dv_ref[...] = dv_ref[...] + jnp.where(mine[0], _mm_tn(pr[0], do2b), _mm_tn(pr[1], do2b))
            for h in range(2):
                sl = slice(LANE * h, LANE * (h + 1))
                dk_ref[:, sl] = dk_ref[:, sl] + _mm_tn(ds[h], qh[h])
                dq_ref[0, rows, sl] = dq_ref[0, rows, sl] + jnp.dot(ds[h], ks[h], preferred_element_type=F32)

        def step(i, carry):
            tile(i, False)
            return carry

        tile(j, True)
        lax.fori_loop(j + 1, n, step, 0)

    qrow = lambda w: pl.BlockSpec((t, w), lambda p, j: (0, p))
    krow = lambda w: pl.BlockSpec((ta, w), lambda p, j: (j, p))
    return pl.pallas_call(
        body, name="attn_bwd", grid=(NH // 2, n),
        in_specs=[qrow(2 * LANE), krow(2 * LANE), krow(LANE), qrow(LANE), qrow(LANE), qrow(LANE), ANY],
        out_specs=[pl.BlockSpec((1, t, 2 * LANE), lambda p, j: (p, 0, 0)), krow(2 * LANE), krow(LANE)],
        out_shape=[jax.ShapeDtypeStruct((NH // 2, t, 2 * LANE), F32), jax.ShapeDtypeStruct((t, NH * LANE), F32),
                   jax.ShapeDtypeStruct((t, DA), F32)],
        compiler_params=_cp(),
    )(qa, ka, v, a, d_a, lse, dep)


def _in_bwd(dqa, dka, dv, dyc, fl, x, dh1, g1, w_in_t, tri_u, dep):
    t = x.shape[0]
    tt = _tile(t)
    nt = t // tt
    hb = tt // HALO
    rev = lambda s: nt - 1 - s

    def body(dqa_ref, dka_ref, dv_ref, dyc_ref, dyn_ref, fl_ref, x_ref, dh1_ref, g_ref, w_ref, tri_ref,
             dep_ref, dx_ref, dw_ref, dg_ref, db_ref, carry, acc, sem):
        s = pl.program_id(0)
        i = nt - 1 - s

        @pl.when(s == 0)
        def _():
            carry[...] = jnp.zeros_like(carry)
            acc[...] = jnp.zeros_like(acc)
            dg_ref[...] = jnp.zeros_like(dg_ref)
            db_ref[...] = jnp.zeros_like(db_ref)

        dq_cat = jnp.concatenate([dqa_ref[p] for p in range(NH // 2)], axis=1)
        dk_cat = dka_ref[...]
        off = lax.broadcasted_iota(jnp.int32, (1, NH * LANE), 1)
        off = off % LANE - HD - off // LANE

        def picked(cat, lane_off):
            kept = jnp.where(off == lane_off, cat, 0.0)
            return functools.reduce(lambda a, b: a + b, [kept[:, LANE * h:LANE * (h + 1)] for h in range(NH)])

        dc = pltpu.roll(picked(dq_cat, 0), LANE - HD, 1) - pltpu.roll(picked(dk_cat, 24), LANE - HD - 24, 1)
        dlf = carry[...] + _dot3(tri_ref[...], dc)
        carry[...] = carry[...] + jnp.sum(dc, axis=0, keepdims=True)
        flv = fl_ref[...]
        lane = lax.broadcasted_iota(jnp.int32, flv.shape, 1)
        d_fl = jnp.where(lane < NH, dlf / (1.0 + jnp.exp(flv)), 0.0)
        db_ref[...] = db_ref[...] + jnp.sum(d_fl, axis=0, keepdims=True)
        low = lax.broadcasted_iota(jnp.int32, (tt, LANE), 1) < HD
        dqs, dks = [], []
        for p in range(NH // 2):
            b0, b1 = slice(2 * LANE * p, 2 * LANE * p + LANE), slice(2 * LANE * p + LANE, 2 * LANE * (p + 1))
            dqs.append(jnp.where(low, dq_cat[:, b0], pltpu.roll(dq_cat[:, b1], HD, 1)) * (1.0 / math.sqrt(HD)))
            dks.append(jnp.where(low, dk_cat[:, b0], pltpu.roll(dk_cat[:, b1], HD, 1)))
        nxt = dyn_ref[...] * jnp.where(i < nt - 1, 1.0, 0.0)
        tok = i * tt + lax.broadcasted_iota(jnp.int32, (tt, PC), 0)
        dus = []
        for g, w in enumerate(WINS):
            sl = slice(PC * g, PC * (g + 1))
            dycg = dyc_ref[:, sl]
            ext = jnp.concatenate([dycg, nxt[:, sl]], axis=0)
            dus.append(_window_sum(ext, w, False)[:tt] - dycg * jnp.minimum(tok + 1, w).astype(F32))
        d_z = jnp.concatenate(dqs + dks + [dv_ref[...], d_fl] + dus, axis=1).astype(BF)
        xv = x_ref[...]
        gv = g_ref[...]
        hn = (xv * _rstd(xv) * gv).astype(BF)
        d_hn = jnp.dot(d_z, w_ref[...], preferred_element_type=F32)
        acc[...] = acc[...] + _mm_tn(d_z, hn)
        d1, dg = _rms_bwd(d_hn, xv, gv)
        dg_ref[...] = dg_ref[...] + dg
        dx_ref[...] = dh1_ref[...] + d1

        @pl.when(s == nt - 1)
        def _():
            cp = pltpu.make_async_copy(acc, dw_ref, sem)
            cp.start()
            cp.wait()

    row = lambda w: pl.BlockSpec((tt, w), lambda s: (rev(s), 0))
    return pl.pallas_call(
        body, name="in_bwd", grid=(nt,),
        in_specs=[pl.BlockSpec((NH // 2, tt, 2 * LANE), lambda s: (0, rev(s), 0)), row(NH * LANE), row(DA), row(DP),
                  pl.BlockSpec((HALO, DP), lambda s: (jnp.minimum((rev(s) + 1) * hb, nt * hb - 1), 0)),
                  row(LANE), row(D), row(D), _full((1, D)), _resident((ZW, D)), _full((tt, tt)),
                  ANY],
        out_specs=[row(D), pl.BlockSpec(memory_space=pl.ANY), _full((1, D)), _full((1, LANE))],
        out_shape=[jax.ShapeDtypeStruct((t, D), F32), jax.ShapeDtypeStruct((ZW, D), F32),
                   jax.ShapeDtypeStruct((1, D), F32), jax.ShapeDtypeStruct((1, LANE), F32)],
        scratch_shapes=[pltpu.VMEM((1, LANE), F32), pltpu.VMEM((ZW, D), F32), pltpu.SemaphoreType.DMA],
        compiler_params=_cp(),
    )(dqa, dka, dv, dyc, dyc, fl, x, dh1, g1, w_in_t, tri_u, dep)


class _NoComm:
    def __init__(self, w2):
        self.w2 = w2
        self.dep = jnp.zeros((8, LANE), F32)

    def after_attention(self, after):
        return self.dep

    def weights2(self, after):
        return _stack2_full(*self.w2)

    def after_ffn(self, grads2):
        self.grads2 = grads2
        return self.dep

    def after_mix(self, after, early):
        self.early = early
        return self.dep

    def after_attn(self, after):
        return self.dep


def _local_step(x, p, tgt, sm, w1, comm):
    w_in_t, w_out = w1
    tt = _tile(x.shape[0])
    eq, ek, rowq, rowk = _aug_consts()
    b_pad = jnp.pad(sm["b_forget"], ((0, 0), (0, LANE - NH)))
    qa, ka, v, u, fl = _in_proj(x, sm["g_mix_pre"], w_in_t, b_pad, _tri(tt, False), eq, ek, rowq, rowk, comm.dep)
    a, lse = _attn_fwd(qa, ka, v)
    yb, m, o, h1 = _mix_out(a, u, x, sm["w_pool"], sm["pool_scale"], sm["g_attn_grp"],
                            sm["g_pool_grp"], w_out, sm["g_mix_post"], comm.after_attention(a))
    stacks2 = comm.weights2(h1)
    wg_t, wu_t, wd, w_ple, w_pg = _unstack2_full(stacks2)
    hn2, gate, up, dff, dh2, loss, dwpg, dwple, dgple, dgfpost = _ffn_fwd(
        h1, sm["g_ffn_pre"], stacks2, sm["g_ffn_post"], p, w_ple, sm["g_ple"], w_pg, tgt)
    chunks = lambda a: a.reshape(DFF // FF_CH, FF_CH, D)
    dwg_t, dwu_t, dwd, d_hn2 = _ffn_bwd(hn2, gate, up, dff, jnp.concatenate([chunks(wg_t), chunks(wu_t)], axis=1), wd)
    dep = comm.after_ffn((dwg_t, dwu_t, dwd, dwple, dwpg))
    dh1, d_a, dyc, dgfpre, dgpost, dgattn, dgpool, dps, dwpool, dwout = _mix_bwd(
        d_hn2, dh2, h1, o, a, m, yb, sm["g_ffn_pre"], sm["g_mix_post"], sm["g_attn_grp"], sm["g_pool_grp"],
        w_out, sm["w_pool"], sm["pool_scale"], dep)
    early = dict(loss=loss[0:1, 0:1], g_attn_grp=dgattn, g_pool_grp=dgpool, w_pool=dwpool, pool_scale=dps,
                 g_mix_post=dgpost, g_ffn_pre=dgfpre, g_ffn_post=dgfpost, g_ple=dgple)
    dqa, dka, dvv = _attn_bwd(qa, ka, v, a, d_a, lse, comm.after_mix(dh1, early))
    dx, dwin_t, dg1, dbf = _in_bwd(dqa, dka, dvv, dyc, fl, x, dh1, sm["g_mix_pre"], w_in_t, _tri(tt, True),
                                   comm.after_attn(dvv))
    return dx, (dwin_t, dwout), dict(g_mix_pre=dg1, b_forget=dbf[:, :NH])


def _place():
    x, y, c = lax.axis_index("x"), lax.axis_index("y"), lax.axis_index("c")
    return x, y, c, [(1 - x, y), (x, 1 - y), (1 - x, 1 - y)]


def _rows(c, h):
    return pl.ds(pl.multiple_of(c * h, 16), h)


def _plan_gather(h):
    def plan(src, land):
        x, y, c, chips = _place()
        return [(src.at[_rows(c, h), :], land.at[2 * x + y, _rows(c, h), :], (cx, cy, c),
                 land.at[2 * cx + cy, _rows(c, h), :]) for cx, cy in chips]
    return plan


def _plan_forward(h):
    def plan(land_in, own, land):
        x, y, c, chips = _place()
        sib, me = (x, y, 1 - c), 2 * x + y
        return ([(land_in.at[2 * cx + cy, _rows(c, h), :], land.at[2 * cx + cy, _rows(c, h), :], sib,
                  land.at[2 * cx + cy, _rows(1 - c, h), :]) for cx, cy in chips]
                + [(own, land.at[me], sib, land.at[me])])
    return plan


def _plan_swap_halves(h):
    def plan(buf_in, buf):
        x, y, c, _ = _place()
        return [(buf_in.at[_rows(c, h), :], buf.at[_rows(c, h), :], (x, y, 1 - c), buf.at[_rows(1 - c, h), :])]
    return plan


def _plan_pair_rows(h):
    def plan(src, land):
        x, y, c, _ = _place()
        return [(src.at[:, _rows(1 - c, h), :], land, (x, y, 1 - c), land)]
    return plan


def _plan_scatter(src, land):
    x, y, c, chips = _place()
    return [(src.at[2 * cx + cy], land.at[k], (cx, cy, c), land.at[k]) for k, (cx, cy) in enumerate(chips)]


def _plan_scatter8(h):
    def plan(src, land):
        x, y, c, chips = _place()
        copies = [(src.at[2 * x + y, _rows(1 - c, h), :], land.at[0], (x, y, 1 - c), land.at[0])]
        for k, (cx, cy) in enumerate(chips):
            for d, other in enumerate((c, 1 - c)):
                copies.append((src.at[2 * cx + cy, _rows(other, h), :], land.at[1 + 2 * k + c], (cx, cy, other),
                               land.at[1 + 2 * k + other]))
        return copies
    return plan


def _plan_all(src, land):
    x, y, c, _ = _place()
    copies = []
    for r in range(1, 8):
        px, py, pc = (1 - a if b else a for a, b in zip((x, y, c), (r >> 2 & 1, r >> 1 & 1, r & 1)))
        copies.append((src, land.at[4 * x + 2 * y + c], (px, py, pc), land.at[4 * px + 2 * py + pc]))
    return copies


def _remote(src, dst, send_sems, recv_sems, k, peer):
    return pltpu.make_async_remote_copy(src_ref=src, dst_ref=dst, send_sem=send_sems.at[k], recv_sem=recv_sems.at[k],
                                        device_id=peer, device_id_type=MESH)


def _exchange(name, n, src, land, plan):
    def body(src_ref, land_ref, send_sems, recv_sems):
        copies = plan(src_ref, land_ref)
        for k, (s, d, peer, _) in enumerate(copies):
            _remote(s, d, send_sems, recv_sems, k, peer).start()
        for k, (s, _, peer, mine) in enumerate(copies):
            _remote(s, mine, send_sems, recv_sems, k, peer).wait_recv()
        for k, (s, d, peer, _) in enumerate(copies):
            _remote(s, d, send_sems, recv_sems, k, peer).wait_send()

    return pl.pallas_call(
        body, name=name, in_specs=[ANY], out_specs=ANY, out_shape=land,
        scratch_shapes=[pltpu.SemaphoreType.DMA((n,)), pltpu.SemaphoreType.DMA((n,))],
    )(src)


def _exchange_inplace(name, n, buf, extra, plan):
    def body(*refs):
        ins, buf_ref, send_sems, recv_sems = refs[:1 + len(extra)], refs[1 + len(extra)], refs[-2], refs[-1]
        copies = plan(*ins, buf_ref)
        for k, (s, d, peer, _) in enumerate(copies):
            _remote(s, d, send_sems, recv_sems, k, peer).start()
        for k, (s, _, peer, mine) in enumerate(copies):
            _remote(s, mine, send_sems, recv_sems, k, peer).wait_recv()
        for k, (s, d, peer, _) in enumerate(copies):
            _remote(s, d, send_sems, recv_sems, k, peer).wait_send()

    return pl.pallas_call(
        body, name=name, in_specs=[ANY] * (1 + len(extra)), out_specs=ANY, out_shape=_sds(buf.shape, buf.dtype),
        input_output_aliases={0: 0},
        scratch_shapes=[pltpu.SemaphoreType.DMA((n,)), pltpu.SemaphoreType.DMA((n,))],
    )(buf, *extra)


HBM = pl.BlockSpec(memory_space=pltpu.HBM)
SEM = pl.BlockSpec(memory_space=pltpu.SEMAPHORE)
EFFECT = pltpu.SideEffectType.DATAFLOW_SIDE_EFFECTING


def _exchange_start(name, n, src, land, plan):
    def body(src_ref, land_ref, send_sems, recv_sems, src_thru, land_thru, token):
        for k, (s, d, peer, _) in enumerate(plan(src_ref, land_ref)):
            _remote(s, d, send_sems, recv_sems, k, peer).start()
        token[...] = jnp.zeros_like(token)

    return pl.pallas_call(
        body, name=name,
        out_shape=(pltpu.SemaphoreType.DMA((n,)), pltpu.SemaphoreType.DMA((n,)), pltpu.HBM(src.shape, src.dtype),
                   pltpu.HBM(land.shape, land.dtype), jax.ShapeDtypeStruct((8, LANE), F32)),
        in_specs=(HBM, HBM), out_specs=(SEM, SEM, HBM, HBM, pl.BlockSpec(memory_space=pltpu.VMEM)),
        input_output_aliases={0: 2, 1: 3},
        compiler_params=pltpu.CompilerParams(has_side_effects=EFFECT),
    )(pltpu.with_memory_space_constraint(src, pltpu.HBM), pltpu.with_memory_space_constraint(land, pltpu.HBM))


def _exchange_wait(name, started, plan, after):
    send_sems, recv_sems, src, land, _ = started

    def body(src_ref, land_ref, send_sems, recv_sems, after_ref, src_out, land_out):
        for k, (s, _, peer, mine) in enumerate(plan(src_ref, land_ref)):
            cp = _remote(s, mine, send_sems, recv_sems, k, peer)
            cp.wait_send()
            cp.wait_recv()

    return pl.pallas_call(
        body, name=name, out_shape=(pltpu.HBM(src.shape, src.dtype), pltpu.HBM(land.shape, land.dtype)),
        in_specs=(HBM, HBM, SEM, SEM, ANY), out_specs=(HBM, HBM), input_output_aliases={0: 0, 1: 1},
        compiler_params=pltpu.CompilerParams(has_side_effects=EFFECT),
    )(src, land, send_sems, recv_sems, after)


def _pair_sum(name, cidx, g, recv, br):
    h = recv.shape[1]
    nb = h // br

    def body(c_ref, g_ref, r_ref, out_ref):
        out_ref[...] = (g_ref[...] + r_ref[...]).astype(BF)

    return pl.pallas_call(
        body, name=name,
        grid_spec=pltpu.PrefetchScalarGridSpec(
            num_scalar_prefetch=1, grid=(NSHARD, nb),
            in_specs=[pl.BlockSpec((1, br, D), lambda s, i, c: (s, c[0] * nb + i, 0)),
                      pl.BlockSpec((1, br, D), lambda s, i, c: (s, i, 0))],
            out_specs=pl.BlockSpec((1, br, D), lambda s, i, c: (s, i, 0))),
        out_shape=jax.ShapeDtypeStruct((NSHARD, h, D), BF),
    )(cidx, g, recv)


def _chip_sum(name, place, pb, y, br):
    h = y.shape[1]
    nb = h // br

    def body(pl_ref, p_ref, y_ref, out_ref):
        acc = p_ref[0].astype(F32)
        for k in range(NSHARD - 1):
            acc = acc + y_ref[k].astype(F32)
        out_ref[...] = acc

    return pl.pallas_call(
        body, name=name,
        grid_spec=pltpu.PrefetchScalarGridSpec(
            num_scalar_prefetch=1, grid=(nb,),
            in_specs=[pl.BlockSpec((1, br, D), lambda i, s: (s[0], i, 0)),
                      pl.BlockSpec((NSHARD - 1, br, D), lambda i, s: (0, i, 0))],
            out_specs=pl.BlockSpec((br, D), lambda i, s: (s[1] * nb + i, 0))),
        out_shape=jax.ShapeDtypeStruct((2 * h, D), F32),
    )(place, pb, y)


def _sum8(name, place, g, land, br):
    h = land.shape[1]
    nb = h // br

    def body(pl_ref, g_ref, y_ref, out_ref):
        acc = g_ref[0].astype(F32)
        for k in range(land.shape[0]):
            acc = acc + y_ref[k].astype(F32)
        out_ref[...] = acc

    return pl.pallas_call(
        body, name=name,
        grid_spec=pltpu.PrefetchScalarGridSpec(
            num_scalar_prefetch=1, grid=(nb,),
            in_specs=[pl.BlockSpec((1, br, D), lambda i, s: (s[0], s[1] * nb + i, 0)),
                      pl.BlockSpec((land.shape[0], br, D), lambda i, s: (0, i, 0))],
            out_specs=pl.BlockSpec((br, D), lambda i, s: (s[1] * nb + i, 0))),
        out_shape=jax.ShapeDtypeStruct((2 * h, D), F32), compiler_params=_cp(),
    )(place, g, land)


def _sum_slots(name, v):
    def body(in_ref, out_ref):
        acc = in_ref[0]
        for k in range(1, 8):
            acc = acc + in_ref[k]
        out_ref[...] = acc

    vm = pl.BlockSpec(memory_space=pltpu.VMEM)
    return pl.pallas_call(body, name=name, in_specs=[vm], out_specs=vm,
                          out_shape=jax.ShapeDtypeStruct(v.shape[1:], F32))(v)


def _adamw_math(w, g, m, v):
    m = ADAM_B1 * m + (1.0 - ADAM_B1) * g
    v = ADAM_B2 * v + (1.0 - ADAM_B2) * (g * g)
    m_hat = m / (1.0 - ADAM_B1 ** ADAM_STEP)
    v_hat = v / (1.0 - ADAM_B2 ** ADAM_STEP)
    delta = -ADAM_LR * (m_hat / (jnp.sqrt(v_hat) + ADAM_EPS) + ADAM_WD * w)
    return delta, m, v


def _adamw(w, g, m, v, dep, row0=None):
    r, c = w.shape
    br = next(b for b in (256, 176, 128, r) if r % b == 0 and (row0 or 0) % b == 0)
    first = (row0 or 0) // br

    def body(w_ref, g_ref, m_ref, v_ref, dep_ref, *outs):
        gv = g_ref[...]
        outs[-3][...], outs[-2][...], outs[-1][...] = _adamw_math(w_ref[...], gv, m_ref[...], v_ref[...])
        if row0 is not None:
            outs[0][...] = gv

    spec = pl.BlockSpec((br, c), lambda i: (i, 0))
    n_out = 3 if row0 is None else 4
    out = pl.pallas_call(
        body, name="adamw", grid=(r // br,),
        in_specs=[spec, pl.BlockSpec((br, c), lambda i: (first + i, 0)), spec, spec, ANY], out_specs=[spec] * n_out,
        out_shape=[jax.ShapeDtypeStruct((r, c), F32)] * n_out, compiler_params=_cp(),
    )(w, g, m, v, dep)
    return out if row0 is not None else [g] + list(out)


def _adamw_small(ws, gs, ms, vs):
    n = len(ws)

    def body(*refs):
        ins, outs = refs[:4 * n], refs[4 * n:]
        for k in range(n):
            d, m, v = _adamw_math(ins[k][...], ins[n + k][...], ins[2 * n + k][...], ins[3 * n + k][...])
            outs[k][...] = d
            outs[n + k][...] = m
            outs[2 * n + k][...] = v

    vm = pl.BlockSpec(memory_space=pltpu.VMEM)
    out = pl.pallas_call(
        body, name="adamw_small", in_specs=[vm] * (4 * n), out_specs=[vm] * (3 * n),
        out_shape=[jax.ShapeDtypeStruct(w.shape, F32) for w in ws] * 3,
    )(*ws, *gs, *ms, *vs)
    return out[:n], out[n:2 * n], out[2 * n:]


BIG = ("w_in", "w_out", "w_ffn_gate", "w_ffn_up", "w_ffn_down", "w_ple_proj", "w_ple_gate")
SMALL = ("g_mix_pre", "b_forget", "g_attn_grp", "g_pool_grp", "w_pool", "pool_scale", "g_mix_post", "g_ffn_pre",
         "g_ffn_post", "g_ple")
TRANSPOSED = ("w_in", "w_ffn_gate", "w_ffn_up")
VECTORS = tuple(n for n in SMALL if n != "w_pool")
ORDER = ("g_mix_pre", "w_in", "b_forget", "g_attn_grp", "g_pool_grp", "w_pool", "pool_scale", "w_out", "g_mix_post",
         "g_ffn_pre", "w_ffn_gate", "w_ffn_up", "w_ffn_down", "g_ffn_post", "w_ple_proj", "g_ple", "w_ple_gate")


def _pad_rows(a, rows):
    return jnp.pad(a, ((0, rows - a.shape[0]), (0, 0)))


def _stack1(w_in, w_out):
    return _pad_rows(jnp.concatenate([_pad_rows(w_in.T, IN_PAD), w_out], axis=0), ROWS1)


def _stack2(wg, wu, wd, wple, wpg):
    return _pad_rows(jnp.concatenate([wg.T, wu.T, wd, wple.reshape(DPLE // NSHARD, D), wpg], axis=0), ROWS2)


def _unstack1(s):
    return s[:IN_SH], s[O1_OUT:USED1]


def _cat(g, lo, hi):
    return g[:, lo:hi].reshape(NSHARD * (hi - lo), D)


def _unstack1_full(g):
    w_in_t = _cat(g, 0, IN_SH)
    w_in_t = jnp.concatenate([w_in_t[:3 * DA], _pad_rows(w_in_t[3 * DA:3 * DA + NH], LANE), w_in_t[3 * DA + NH:]], axis=0)
    return w_in_t, _cat(g, O1_OUT, USED1)


def _unstack2_full(g):
    w_ple = g[:, O2_PLE:O2_PG].reshape(NSHARD, DPLE, DPLE).transpose(1, 0, 2).reshape(DPLE, D)
    return _cat(g, 0, O2_U), _cat(g, O2_U, O2_D), _cat(g, O2_D, O2_PLE), w_ple, _cat(g, O2_PG, USED2)


def _shards(a):
    return a.reshape(NSHARD, a.shape[0] // NSHARD, D)


def _stack1_full(dwin_t, dwout):
    dwin_t = jnp.concatenate([dwin_t[:3 * DA + NH], dwin_t[3 * DA + LANE:]], axis=0).reshape(NSHARD, IN_SH, D)
    zeros = lambda r: jnp.zeros((NSHARD, r, D), F32)
    return jnp.concatenate([dwin_t, zeros(IN_PAD - IN_SH), _shards(dwout), zeros(ROWS1 - USED1)], axis=1)


def _stack2_full(dwg_t, dwu_t, dwd, dwple, dwpg):
    dwple = dwple.reshape(DPLE, NSHARD, DPLE).transpose(1, 0, 2).reshape(NSHARD, DPLE // NSHARD, D)
    return jnp.concatenate([_shards(dwg_t), _shards(dwu_t), _shards(dwd), dwple, _shards(dwpg),
                            jnp.zeros((NSHARD, ROWS2 - USED2, D), dwd.dtype)], axis=1)


def _sds(shape, dtype):
    return jax.ShapeDtypeStruct(shape, dtype)


class _Comm:
    def __init__(self, stack2, me, c):
        self.stack2, self.me, self.c = stack2, me, c
        self.cidx = c.astype(jnp.int32).reshape(1)
        self.place = jnp.stack([me, c]).astype(jnp.int32)
        self.h = ROWS2 // 2
        self.gather = _exchange_start("gather2_start", 3, stack2, lax.empty((NSHARD, ROWS2, D), BF), _plan_gather(self.h))
        self.dep = self.gather[4]

    def after_attention(self, after):
        own, land = _exchange_wait("gather2_wait", self.gather, _plan_gather(self.h), after)
        fwd = _plan_forward(self.h)
        self.forward = lambda own_ref, land_ref: fwd(land_ref, own_ref, land_ref)
        self.passing = _exchange_start("forward2_start", 4, own, land, self.forward)
        return self.passing[4]

    def weights2(self, after):
        return _exchange_wait("forward2_wait", self.passing, self.forward, after)[1]

    def after_ffn(self, grads2):
        g = _stack2_full(*[a.astype(BF) for a in grads2])
        self.scatter = _plan_scatter8(self.h)
        self.chip = _exchange_start("reduce2_start", 7, g, lax.empty((7, self.h, D), BF), self.scatter)
        return self.chip[4]

    def after_mix(self, after, early):
        self.early_shapes = {n: early[n].shape for n in early}
        self.small = self.start_small("small", early)
        return self.small[4]

    def after_attn(self, after):
        g, y = _exchange_wait("reduce2_wait", self.chip, self.scatter, after)
        f = _sum8("sum2", self.place, g, y, RED2 // 2)
        self.early = self.finish_small("small", self.small, self.early_shapes, after)
        swap = _plan_swap_halves(self.h)
        self.swap = lambda _, buf: swap(buf, buf)
        self.swapping = _exchange_start("reduce2_gather_start", 1, self.dep, f, self.swap)
        return self.swapping[4]

    def reduced2(self, after):
        return _exchange_wait("reduce2_gather_wait", self.swapping, self.swap, after)[1]

    def start_small(self, name, small):
        v = _pack_small(small)
        return _exchange_start(name + "_start", 7, v, lax.empty((8,) + v.shape, F32), _plan_all)

    def finish_small(self, name, started, shapes, after):
        v, land = _exchange_wait(name + "_wait", started, _plan_all, after)
        land = lax.dynamic_update_slice(land, v[None], (2 * self.me + self.c, 0, 0))
        return _unpack_small(_sum_slots(name + "_sum", land), shapes)


def _pack_small(small):
    parts = []
    for name in small:
        flat = small[name].reshape(-1)
        parts.append(jnp.pad(flat, (0, -flat.shape[0] % LANE)).reshape(-1, LANE))
    v = jnp.concatenate(parts, axis=0)
    return _pad_rows(v, v.shape[0] + (-v.shape[0] % 8))


def _unpack_small(v, shapes):
    out, r = {}, 0
    for name in shapes:
        n = math.prod(shapes[name])
        rows = -(-n // LANE)
        out[name] = v[r:r + rows].reshape(-1)[:n].reshape(shapes[name])
        r += rows
    return out


def kernel(x, p, g_mix_pre, w_in, b_forget, g_attn_grp, g_pool_grp, w_pool, pool_scale, w_out, g_mix_post, g_ffn_pre, w_ffn_gate, w_ffn_up, w_ffn_down, g_ffn_post, w_ple_proj, g_ple, w_ple_gate, loss_target, m_g_mix_pre, m_w_in, m_b_forget, m_g_attn_grp, m_g_pool_grp, m_w_pool, m_pool_scale, m_w_out, m_g_mix_post, m_g_ffn_pre, m_w_ffn_gate, m_w_ffn_up, m_w_ffn_down, m_g_ffn_post, m_w_ple_proj, m_g_ple, m_w_ple_gate, v_g_mix_pre, v_w_in, v_b_forget, v_g_attn_grp, v_g_pool_grp, v_w_pool, v_pool_scale, v_w_out, v_g_mix_post, v_g_ffn_pre, v_w_ffn_gate, v_w_ffn_up, v_w_ffn_down, v_g_ffn_post, v_w_ple_proj, v_g_ple, v_w_ple_gate):
    args = dict(locals())
    strip = lambda n, a: a if n in VECTORS else a[0]
    w = {n: strip(n, args[n]) for n in ORDER}
    mom = {n: strip(n, args["m_" + n]) for n in ORDER}
    var = {n: strip(n, args["v_" + n]) for n in ORDER}
    sm = {n: w[n] for n in SMALL}

    c = lax.axis_index("c")
    me = 2 * lax.axis_index("x") + lax.axis_index("y")
    h1 = ROWS1 // 2
    bf = lambda n: w[n].astype(BF)
    stack1 = _stack1(bf("w_in"), bf("w_out"))
    stack2 = _stack2(*[bf(n) for n in BIG[2:]])
    land = _exchange("gather1", 3, stack1, _sds((NSHARD, ROWS1, D), BF), _plan_gather(h1))
    land, stack2 = lax.optimization_barrier((land, stack2))
    comm = _Comm(stack2, me, c)
    w1 = _unstack1_full(_exchange_inplace("gather1_forward", 4, land, (stack1,), _plan_forward(h1)))
    dx, grads1, late = _local_step(x[0], p[0, 0], loss_target[0], sm, w1, comm)

    flip = lambda n, a: a.T if n in TRANSPOSED else a
    grads, delta, new_m, new_v = {}, {}, {}, {}

    def update(n, g, dep, row0=None):
        g_, d_, m_, v_ = _adamw(flip(n, w[n]), g, flip(n, mom[n]), flip(n, var[n]), dep, row0)
        grads[n], delta[n], new_m[n], new_v[n] = flip(n, g_), flip(n, d_), flip(n, m_), flip(n, v_)
        return v_

    late_shapes = {n: late[n].shape for n in late}
    small2 = comm.start_small("small2", late)
    red2 = comm.reduced2(dx)
    g1 = _stack1_full(*grads1)
    pair1 = _exchange_start("reduce1_pair_start", 1, g1, lax.empty((NSHARD, h1, D), F32), _plan_pair_rows(h1))
    dep = update("w_ple_gate", red2, pair1[4] + small2[4], O2_PG)
    dep = update("w_ple_proj", red2[O2_PLE:O2_PG].reshape(DPLE, DPLE), dep)
    g1, recv = _exchange_wait("reduce1_pair_wait", pair1, _plan_pair_rows(h1), dep)
    pb = _pair_sum("pair_sum1", comm.cidx, g1, recv, RED1)
    chip1 = _exchange_start("reduce1_chip_start", 3, pb, lax.empty((NSHARD - 1, h1, D), BF), _plan_scatter)
    dep = update("w_ffn_gate", red2, chip1[4], 0)
    dep = update("w_ffn_up", red2, dep, O2_U)
    dep = update("w_ffn_down", red2, dep, O2_D)
    red_small = {**comm.early, **comm.finish_small("small2", small2, late_shapes, dep)}
    loss = 0.5 / D * red_small["loss"][0, 0]
    for n in SMALL:
        grads[n] = red_small[n].reshape(w[n].shape)
    two_d = lambda a: a.reshape(-1, a.shape[-1])
    ds, ms, vs = _adamw_small([two_d(w[n]) for n in SMALL], [two_d(grads[n]) for n in SMALL],
                              [two_d(mom[n]) for n in SMALL], [two_d(var[n]) for n in SMALL])
    for k, n in enumerate(SMALL):
        delta[n], new_m[n], new_v[n] = ds[k].reshape(w[n].shape), ms[k].reshape(w[n].shape), vs[k].reshape(w[n].shape)
    pb, y = _exchange_wait("reduce1_chip_wait", chip1, _plan_scatter, vs[0])
    f = _chip_sum("chip_sum1", comm.place, pb, y, RED1)
    reduced1 = _exchange_inplace("reduce1_gather", 1, f, (), _plan_swap_halves(h1))
    g_in, g_out = _unstack1(reduced1)
    update("w_out", g_out, update("w_in", g_in, reduced1))

    lead = lambda d: [d[n] if n in VECTORS else d[n][None] for n in ORDER]
    return (loss, dx[None], *lead(grads), *lead(delta), *lead(new_m), *lead(new_v))
```

```python
import functools
import math

import jax
import jax.numpy as jnp
import numpy as np
from jax import lax
from jax.experimental import pallas as pl
from jax.experimental.pallas import tpu as pltpu

F32 = jnp.float32
BF = jnp.bfloat16
MESH = pl.DeviceIdType.MESH

D = 1024
DA = 512
DP = 512
NH = 8
HD = 64
DFF = 2816
DPLE = 256
WINS = (2, 4, 8, 16)
PC = 128
ZW = 3 * DA + 128 + DP
EPS = 1e-6
NSHARD = 4

LANE = 128
HALO = 128

IN_SH = 514
IN_PAD = 528
FF_SH = DFF // NSHARD
O1_OUT, USED1, ROWS1 = 528, 784, 800
O2_U, O2_D, O2_PLE, O2_PG, USED2, ROWS2 = 704, 1408, 2112, 2176, 2432, 2560
RED1, RED2 = 400, 640

ADAM_LR, ADAM_B1, ADAM_B2, ADAM_EPS, ADAM_WD, ADAM_STEP = 0.001, 0.9, 0.999, 1e-8, 0.01, 10

VMEM_LIMIT = 56 * 1024 * 1024


def _cp(**kw):
    return pltpu.CompilerParams(vmem_limit_bytes=VMEM_LIMIT, **kw)


def _mm(a, b):
    return jnp.dot(a.astype(BF), b.astype(BF), preferred_element_type=F32)


def _mm_nt(a, b):
    return lax.dot_general(a.astype(BF), b.astype(BF), (((1,), (1,)), ((), ())), preferred_element_type=F32)


def _mm_tn(a, b):
    return lax.dot_general(a.astype(BF), b.astype(BF), (((0,), (0,)), ((), ())), preferred_element_type=F32)


def _split3(x):
    hi = x.astype(BF)
    r = x - hi.astype(F32)
    mid = r.astype(BF)
    lo = (r - mid.astype(F32)).astype(BF)
    return hi, mid, lo


def _dot3(m, x):
    hi, mid, lo = _split3(x)
    return (jnp.dot(m, hi, preferred_element_type=F32) + jnp.dot(m, mid, preferred_element_type=F32)
            + jnp.dot(m, lo, preferred_element_type=F32))


def _window_sum(ext, w, back):
    n = ext.shape[0]
    s, k = ext, 1
    while k < w:
        s = s + pltpu.roll(s, k if back else n - k, 0)
        k *= 2
    return s


def _rstd(x):
    return lax.rsqrt(jnp.mean(x * x, axis=-1, keepdims=True) + EPS)


def _rms_bwd(dy, x, g):
    r = _rstd(x)
    xh = x * r
    dg = jnp.sum(dy * xh, axis=0, keepdims=True)
    dxh = dy * g
    dx = r * (dxh - xh * jnp.mean(dxh * xh, axis=-1, keepdims=True))
    return dx, dg


def _sigmoid(x):
    return 1.0 / (1.0 + jnp.exp(-x))


ANY = pl.BlockSpec(memory_space=pl.ANY)


def _full(shape):
    n = len(shape)
    return pl.BlockSpec(shape, lambda *_: (0,) * n)


def _resident(shape):
    n = len(shape)
    return pl.BlockSpec(shape, lambda *_: (0,) * n, pipeline_mode=pl.Buffered(1))


def _tile(t):
    return 512 if t % 512 == 0 else t


def _tri(n, upper):
    r, c = np.indices((n, n))
    return ((c >= r) if upper else (c <= r)).astype(BF)


def _aug_consts():
    row, col = np.indices((3 * LANE, NH * LANE))
    piece, head = row // LANE, row % LANE
    ch, cl = col // LANE, col % LANE
    eq = ((head == ch) & (cl == HD + 8 * piece + head)).astype(BF)
    ek = -((head == ch) & (cl == HD + 24 + 8 * piece + head)).astype(BF)
    off = (np.arange(NH * LANE) % LANE - HD - np.arange(NH * LANE) // LANE)[None, :]
    rowq = ((off >= 24) & (off < 48) & (off % 8 == 0)).astype(np.float32)
    rowk = ((off >= 0) & (off < 24) & (off % 8 == 0)).astype(np.float32)
    return eq, ek, rowq, rowk


def _in_proj(x, g1, w_in_t, b_pad, tri, eq, ek, rowq, rowk, dep):
    t = x.shape[0]
    tt = _tile(t)

    def body(x_ref, g_ref, w_ref, b_ref, tri_ref, eq_ref, ek_ref, rq_ref, rk_ref, dep_ref,
             qa_ref, ka_ref, v_ref, u_ref, fl_ref, carry):
        i = pl.program_id(0)

        @pl.when(i == 0)
        def _():
            carry[...] = jnp.zeros_like(carry)

        xv = x_ref[...]
        hn = (xv * _rstd(xv) * g_ref[...]).astype(BF)
        z = _mm_nt(hn, w_ref[...])
        fl = z[:, 3 * DA:3 * DA + LANE] + b_ref[...]
        lane = lax.broadcasted_iota(jnp.int32, fl.shape, 1)
        lf = jnp.where(lane < NH, jnp.minimum(fl, 0.0) - jnp.log(1.0 + jnp.exp(-jnp.abs(fl))), 0.0)
        c = carry[...] + _dot3(tri_ref[...], lf)
        carry[...] = carry[...] + jnp.sum(lf, axis=0, keepdims=True)
        caug = jnp.concatenate(_split3(c), axis=1)
        aug_q = jnp.dot(caug, eq_ref[...], preferred_element_type=F32) + rq_ref[...]
        aug_k = jnp.dot(caug, ek_ref[...], preferred_element_type=F32) + rk_ref[...]
        low = lax.broadcasted_iota(jnp.int32, (tt, LANE), 1) < HD
        for p in range(NH // 2):
            qp = z[:, LANE * p:LANE * (p + 1)] * (1.0 / math.sqrt(HD))
            kp = z[:, DA + LANE * p:DA + LANE * (p + 1)]
            for h, (qh, kh) in enumerate(((qp, kp), (pltpu.roll(qp, HD, 1), pltpu.roll(kp, HD, 1)))):
                lo_, hi_ = LANE * (2 * p + h), LANE * (2 * p + h + 1)
                qa_ref[:, lo_:hi_] = jnp.where(low, qh, aug_q[:, lo_:hi_]).astype(BF)
                ka_ref[:, lo_:hi_] = jnp.where(low, kh, aug_k[:, lo_:hi_]).astype(BF)
        v_ref[...] = z[:, 2 * DA:3 * DA].astype(BF)
        u_ref[...] = z[:, 3 * DA + LANE:]
        fl_ref[...] = fl

    return pl.pallas_call(
        body, name="in_proj", grid=(t // tt,),
        in_specs=[pl.BlockSpec((tt, D), lambda i: (i, 0)), _full((1, D)), _resident((ZW, D)), _full((1, LANE)),
                  _full((tt, tt)), _full((3 * LANE, NH * LANE)), _full((3 * LANE, NH * LANE)),
                  _full((1, NH * LANE)), _full((1, NH * LANE)), ANY],
        out_specs=[pl.BlockSpec((tt, NH * LANE), lambda i: (i, 0)), pl.BlockSpec((tt, NH * LANE), lambda i: (i, 0)),
                   pl.BlockSpec((tt, DA), lambda i: (i, 0)), pl.BlockSpec((tt, DP), lambda i: (i, 0)),
                   pl.BlockSpec((tt, LANE), lambda i: (i, 0))],
        out_shape=[jax.ShapeDtypeStruct((t, NH * LANE), BF), jax.ShapeDtypeStruct((t, NH * LANE), BF),
                   jax.ShapeDtypeStruct((t, DA), BF), jax.ShapeDtypeStruct((t, DP), F32),
                   jax.ShapeDtypeStruct((t, LANE), F32)],
        scratch_shapes=[pltpu.VMEM((1, LANE), F32)],
        compiler_params=_cp(),
    )(x, g1, w_in_t, b_pad, tri, eq, ek, rowq, rowk, dep)


def _attn_fwd(qa, ka, v):
    t = qa.shape[0]
    ta = _tile(t)
    n = t // ta

    def body(q_ref, k_ref, v_ref, a_ref, lse_ref, m_ref, l_ref, acc_ref):
        i = pl.program_id(1)
        m_ref[...] = jnp.full_like(m_ref, -1e30)
        l_ref[...] = jnp.zeros_like(l_ref)
        acc_ref[...] = jnp.zeros_like(acc_ref)
        qs = [q_ref[:, LANE * h:LANE * (h + 1)] for h in range(2)]
        reps = ta // LANE

        def tile(j, width, masked):
            rows = pl.ds(pl.multiple_of(j * ta, ta), width * ta)
            v2 = v_ref[rows, :]
            s = [_mm_nt(qs[h], k_ref[rows, LANE * h:LANE * (h + 1)]) for h in range(2)]
            if masked:
                keep = (lax.broadcasted_iota(jnp.int32, (ta, ta), 1) <= lax.broadcasted_iota(jnp.int32, (ta, ta), 0))
                s = [jnp.where(keep, sh, -1e30) for sh in s]
            m_old = [m_ref[h] for h in range(2)]
            m_new = [jnp.maximum(m_old[h], jnp.max(s[h], axis=1, keepdims=True)) for h in range(2)]
            pe = [jnp.exp(s[h] - jnp.tile(m_new[h], (1, width * reps))) for h in range(2)]
            alpha = [jnp.exp(m_old[h] - m_new[h]) for h in range(2)]
            pv = [jnp.dot(pe[h].astype(BF), v2, preferred_element_type=F32) for h in range(2)]
            for h in range(2):
                l_ref[h] = alpha[h] * l_ref[h] + jnp.sum(pe[h], axis=1, keepdims=True)
                acc_ref[h] = alpha[h] * acc_ref[h] + pv[h]
                m_ref[h] = m_new[h]

        def step(jj, carry):
            tile(2 * jj, 2, False)
            return carry

        lax.fori_loop(0, i // 2, step, 0)

        @pl.when(i % 2 == 1)
        def _():
            tile(i - 1, 1, False)

        tile(i, 1, True)
        low = lax.broadcasted_iota(jnp.int32, (ta, LANE), 1) < HD
        a_ref[...] = jnp.where(low, acc_ref[0] / l_ref[0], acc_ref[1] / l_ref[1])
        lse_ref[...] = jnp.where(low, m_ref[0] + jnp.log(l_ref[0]), m_ref[1] + jnp.log(l_ref[1]))

    return pl.pallas_call(
        body, name="attn_fwd", grid=(NH // 2, n),
        in_specs=[pl.BlockSpec((ta, 2 * LANE), lambda p, i: (i, p)),
                  pl.BlockSpec((t, 2 * LANE), lambda p, i: (0, p)),
                  pl.BlockSpec((t, LANE), lambda p, i: (0, p))],
        out_specs=[pl.BlockSpec((ta, LANE), lambda p, i: (i, p)), pl.BlockSpec((ta, LANE), lambda p, i: (i, p))],
        out_shape=[jax.ShapeDtypeStruct((t, DA), F32), jax.ShapeDtypeStruct((t, DA), F32)],
        scratch_shapes=[pltpu.VMEM((2, ta, LANE), F32), pltpu.VMEM((2, ta, LANE), F32), pltpu.VMEM((2, ta, LANE), F32)],
        compiler_params=_cp(),
    )(qa, ka, v)


def _mix_out(a, u, x, w_pool, pool_scale, g_attn, g_pool, w_out, g_post, dep):
    t = a.shape[0]
    tt = _tile(t)
    hb = tt // HALO

    def body(a_ref, u_ref, up_ref, x_ref, wp_ref, ps_ref, ga_ref, gp_ref, wo_ref, go_ref, dep_ref,
             yb_ref, m_ref, o_ref, h1_ref):
        i = pl.program_id(0)
        prev = up_ref[...] * jnp.where(i > 0, 1.0, 0.0)
        tok = i * tt + lax.broadcasted_iota(jnp.int32, (tt, PC), 0)
        ms = []
        for g, w in enumerate(WINS):
            ug = u_ref[:, PC * g:PC * (g + 1)]
            ext = jnp.concatenate([prev[:, PC * g:PC * (g + 1)], ug], axis=0)
            cnt = jnp.minimum(tok + 1, w).astype(F32)
            y = (_window_sum(ext, w, True)[HALO:] / cnt - ug).astype(BF)
            yb_ref[:, PC * g:PC * (g + 1)] = y
            ms.append(_mm(y, wp_ref[g]) * ps_ref[:, PC * g:PC * (g + 1)])
        m = jnp.concatenate(ms, axis=1)
        m_ref[...] = m
        av = a_ref[...]
        mix = jnp.concatenate([av * _rstd(av) * ga_ref[...], m * _rstd(m) * gp_ref[...]], axis=1)
        o = _mm(mix, wo_ref[...])
        o_ref[...] = o
        h1_ref[...] = x_ref[...] + o * _rstd(o) * go_ref[...]

    return pl.pallas_call(
        body, name="mix_out", grid=(t // tt,),
        in_specs=[pl.BlockSpec((tt, DA), lambda i: (i, 0)), pl.BlockSpec((tt, DP), lambda i: (i, 0)),
                  pl.BlockSpec((HALO, DP), lambda i: (jnp.maximum(i * hb - 1, 0), 0)),
                  pl.BlockSpec((tt, D), lambda i: (i, 0)),
                  _full((len(WINS), PC, PC)), _full((1, DP)), _full((1, DA)), _full((1, DP)),
                  _resident((D, D)), _full((1, D)), ANY],
        out_specs=[pl.BlockSpec((tt, DP), lambda i: (i, 0)), pl.BlockSpec((tt, DP), lambda i: (i, 0)),
                   pl.BlockSpec((tt, D), lambda i: (i, 0)), pl.BlockSpec((tt, D), lambda i: (i, 0))],
        out_shape=[jax.ShapeDtypeStruct((t, DP), BF), jax.ShapeDtypeStruct((t, DP), F32),
                   jax.ShapeDtypeStruct((t, D), F32), jax.ShapeDtypeStruct((t, D), F32)],
        compiler_params=_cp(),
    )(a, u, u, x, w_pool, pool_scale, g_attn, g_pool, w_out, g_post, dep)


def _ffn_fwd(h1, g_pre, stacks2, g_post, p, w_ple, g_ple, w_pg, tgt):
    t = h1.shape[0]
    tt = 256 if t % 256 == 0 else t

    def body(h1_ref, gpre_ref, wg_ref, wu_ref, wd_ref, gpost_ref, p_ref, wple_ref, gple_ref, wpg_ref, tgt_ref,
             hn_ref, gate_ref, up_ref, dff_ref, dh2_ref, loss_ref, dwpg_ref, dwple_ref, dgple_ref, dgpost_ref):
        i = pl.program_id(0)

        @pl.when(i == 0)
        def _():
            loss_ref[...] = jnp.zeros_like(loss_ref)
            dwpg_ref[...] = jnp.zeros_like(dwpg_ref)
            dwple_ref[...] = jnp.zeros_like(dwple_ref)
            dgple_ref[...] = jnp.zeros_like(dgple_ref)
            dgpost_ref[...] = jnp.zeros_like(dgpost_ref)

        h1v = h1_ref[...]
        hn = (h1v * _rstd(h1v) * gpre_ref[...]).astype(BF)
        hn_ref[...] = hn
        gate = _mm_nt(hn, wg_ref[...].reshape(DFF, D))
        up = _mm_nt(hn, wu_ref[...].reshape(DFF, D))
        for k in range(DFF // FF_CH):
            gate_ref[k] = gate[:, FF_CH * k:FF_CH * (k + 1)].astype(BF)
            up_ref[k] = up[:, FF_CH * k:FF_CH * (k + 1)].astype(BF)
        ff = _mm(gate * _sigmoid(gate) * up, wd_ref[...].reshape(DFF, D))
        rff = _rstd(ff)
        ffh = ff * rff
        gpost = gpost_ref[...]
        h2 = h1v + ffh * gpost
        pv = p_ref[...]
        pe = _mm(pv, wple_ref[...])
        rpe = _rstd(pe)
        peh = pe * rpe
        gple = gple_ref[...]
        e = peh * gple
        sig = _sigmoid(_mm(h2, wpg_ref[...]))
        dv = h2 + sig * e - tgt_ref[...]
        sq = jnp.sum(jnp.sum(dv * dv, axis=1, keepdims=True), axis=0, keepdims=True)
        loss_ref[...] = loss_ref[...] + sq
        dy = dv * (1.0 / D)
        d_e = dy * sig
        d_gl = dy * e * sig * (1.0 - sig)
        dh2 = dy + _mm_nt(d_gl, wpg_ref[...])
        dh2_ref[...] = dh2
        dwpg_ref[...] = dwpg_ref[...] + _mm_tn(h2, d_gl)
        dgple_ref[...] = dgple_ref[...] + jnp.sum(d_e * peh, axis=0, keepdims=True)
        dpeh = d_e * gple
        d_pe = rpe * (dpeh - peh * jnp.mean(dpeh * peh, axis=-1, keepdims=True))
        dwple_ref[...] = dwple_ref[...] + _mm_tn(pv, d_pe)
        dgpost_ref[...] = dgpost_ref[...] + jnp.sum(dh2 * ffh, axis=0, keepdims=True)
        dffh = dh2 * gpost
        dff_ref[...] = (rff * (dffh - ffh * jnp.mean(dffh * ffh, axis=-1, keepdims=True))).astype(BF)

    row = lambda w: pl.BlockSpec((tt, w), lambda i: (i, 0))
    chunked = pl.BlockSpec((DFF // FF_CH, tt, FF_CH), lambda i: (0, i, 0))
    shard_rows = lambda k: pl.BlockSpec((NSHARD, FF_SH, D), lambda i: (0, k, 0), pipeline_mode=pl.Buffered(1))
    return pl.pallas_call(
        body, name="ffn_fwd", grid=(t // tt,),
        in_specs=[row(D), _full((1, D)), shard_rows(0), shard_rows(1), shard_rows(2), _full((1, D)),
                  row(DPLE), _resident((DPLE, D)), _full((1, D)), _resident((D, D)), row(D)],
        out_specs=[row(D), chunked, chunked, row(D), row(D), _full((8, LANE)), _full((D, D)), _full((DPLE, D)),
                   _full((1, D)), _full((1, D))],
        out_shape=[jax.ShapeDtypeStruct((t, D), BF), jax.ShapeDtypeStruct((DFF // FF_CH, t, FF_CH), BF),
                   jax.ShapeDtypeStruct((DFF // FF_CH, t, FF_CH), BF),
                   jax.ShapeDtypeStruct((t, D), BF), jax.ShapeDtypeStruct((t, D), F32), jax.ShapeDtypeStruct((8, LANE), F32),
                   jax.ShapeDtypeStruct((D, D), F32), jax.ShapeDtypeStruct((DPLE, D), F32),
                   jax.ShapeDtypeStruct((1, D), F32), jax.ShapeDtypeStruct((1, D), F32)],
        compiler_params=_cp(),
    )(h1, g_pre, stacks2, stacks2, stacks2, g_post, p, w_ple, g_ple, w_pg, tgt)


FF_CH = 256


def _ffn_bwd(hn2, gate, up, dff, wgu, wd):
    t = hn2.shape[0]
    tt = 1024 if t % 1024 == 0 else _tile(t)
    nt = t // tt
    ch = FF_CH
    nc = DFF // ch

    def body(hn_ref, gate_ref, up_ref, dff_ref, wgu_ref, wd_ref,
             dwg_ref, dwu_ref, dwd_ref, dhn_ref, acc, gu_acc, d_acc, sem):
        j, i = pl.program_id(0), pl.program_id(1)

        @pl.when(j == 0)
        def _():
            acc[pl.ds(pl.multiple_of(i * tt, tt), tt), :] = jnp.zeros((tt, D), F32)

        @pl.when(i == 0)
        def _():
            gu_acc[...] = jnp.zeros_like(gu_acc)
            d_acc[...] = jnp.zeros_like(d_acc)

        parts = 4 if tt % 1024 == 0 else 2
        half = tt // parts
        acts, dgus = [], []
        for hh in range(parts):
            r = slice(hh * half, (hh + 1) * half)
            gate_v = gate_ref[0, r, :].astype(F32)
            up_v = up_ref[0, r, :].astype(F32)
            sg = _sigmoid(gate_v)
            silu = gate_v * sg
            d_act = _mm_nt(dff_ref[pl.ds(pl.multiple_of(i * tt + hh * half, half), half), :], wd_ref[...])
            d_up = (d_act * silu).astype(BF)
            d_gate = (d_act * up_v * (sg * (1.0 + gate_v * (1.0 - sg)))).astype(BF)
            dgu = jnp.concatenate([d_gate, d_up], axis=1)
            rows = pl.ds(pl.multiple_of(i * tt + hh * half, half), half)
            acc[rows, :] = acc[rows, :] + jnp.dot(dgu, wgu_ref[0], preferred_element_type=F32)
            acts.append((silu * up_v).astype(BF))
            dgus.append(dgu)
        tile_rows = pl.ds(pl.multiple_of(i * tt, tt), tt)
        d_acc[...] = d_acc[...] + _mm_tn(jnp.concatenate(acts, axis=0), dff_ref[tile_rows, :])
        gu_acc[...] = gu_acc[...] + _mm_tn(jnp.concatenate(dgus, axis=0), hn_ref[tile_rows, :])

        @pl.when(i == nt - 1)
        def _():
            dwg_ref[...] = gu_acc[:ch].astype(BF)
            dwu_ref[...] = gu_acc[ch:].astype(BF)
            dwd_ref[...] = d_acc[...].astype(BF)

        @pl.when((j == nc - 1) & (i == nt - 1))
        def _():
            cp = pltpu.make_async_copy(acc, dhn_ref, sem)
            cp.start()
            cp.wait()

    chunk = pl.BlockSpec((ch, D), lambda j, i: (j, 0))
    pair = pl.BlockSpec((1, 2 * ch, D), lambda j, i: (j, 0, 0))
    return pl.pallas_call(
        body, name="ffn_bwd", grid=(nc, nt),
        in_specs=[_resident((t, D)), pl.BlockSpec((1, tt, ch), lambda j, i: (j, i, 0)),
                  pl.BlockSpec((1, tt, ch), lambda j, i: (j, i, 0)), _resident((t, D)), pair, chunk],
        out_specs=[chunk, chunk, chunk, pl.BlockSpec(memory_space=pl.ANY)],
        out_shape=[jax.ShapeDtypeStruct((DFF, D), BF), jax.ShapeDtypeStruct((DFF, D), BF),
                   jax.ShapeDtypeStruct((DFF, D), BF), jax.ShapeDtypeStruct((t, D), F32)],
        scratch_shapes=[pltpu.VMEM((t, D), F32), pltpu.VMEM((2 * ch, D), F32), pltpu.VMEM((ch, D), F32),
                        pltpu.SemaphoreType.DMA],
        compiler_params=_cp(),
    )(hn2, gate, up, dff, wgu, wd)


def _mix_bwd(d_hn2, dh2, h1, o, a, m, yb, g_ffn_pre, g_post, g_attn, g_pool, w_out, w_pool, pool_scale, dep):
    t = a.shape[0]
    tt = 256 if t % 256 == 0 else t

    def body(dhn_ref, dh2_ref, h1_ref, o_ref, a_ref, m_ref, yb_ref, gfp_ref, go_ref, ga_ref, gp_ref, wo_ref, wp_ref,
             ps_ref, dep_ref, dh1_ref, da_ref, dyc_ref, dgfp_ref, dgo_ref, dga_ref, dgp_ref, dps_ref, dwp_ref, dwo_ref):
        i = pl.program_id(0)

        @pl.when(i == 0)
        def _():
            for r in (dgfp_ref, dgo_ref, dga_ref, dgp_ref, dps_ref, dwp_ref, dwo_ref):
                r[...] = jnp.zeros_like(r)

        d1, dg = _rms_bwd(dhn_ref[...], h1_ref[...], gfp_ref[...])
        dgfp_ref[...] = dgfp_ref[...] + dg
        dh1 = dh2_ref[...] + d1
        dh1_ref[...] = dh1
        d_o, dg = _rms_bwd(dh1, o_ref[...], go_ref[...])
        dgo_ref[...] = dgo_ref[...] + dg
        d_mix = _mm_nt(d_o, wo_ref[...])
        av, mv = a_ref[...], m_ref[...]
        mix = jnp.concatenate([av * _rstd(av) * ga_ref[...], mv * _rstd(mv) * gp_ref[...]], axis=1)
        dwo_ref[...] = dwo_ref[...] + _mm_tn(mix, d_o)
        d_a, dg = _rms_bwd(d_mix[:, :DA], av, ga_ref[...])
        dga_ref[...] = dga_ref[...] + dg
        da_ref[...] = d_a
        d_m, dg = _rms_bwd(d_mix[:, DA:], mv, gp_ref[...])
        dgp_ref[...] = dgp_ref[...] + dg
        tok = i * tt + lax.broadcasted_iota(jnp.int32, (tt, PC), 0)
        dps = []
        for g, w in enumerate(WINS):
            sl = slice(PC * g, PC * (g + 1))
            ybg = yb_ref[:, sl]
            wpg = wp_ref[g].astype(BF)
            mlin = jnp.dot(ybg, wpg, preferred_element_type=F32)
            dmg = d_m[:, sl]
            dps.append(jnp.sum(dmg * mlin, axis=0, keepdims=True))
            dml = (dmg * ps_ref[:, sl]).astype(BF)
            dwp_ref[g] = dwp_ref[g] + _mm_tn(ybg, dml)
            dyc_ref[:, sl] = _mm_nt(dml, wpg) / jnp.minimum(tok + 1, w).astype(F32)
        dps_ref[...] = dps_ref[...] + jnp.concatenate(dps, axis=1)

    row = lambda w: pl.BlockSpec((tt, w), lambda i: (i, 0))
    return pl.pallas_call(
        body, name="mix_bwd", grid=(t // tt,),
        in_specs=[row(D), row(D), row(D), row(D), row(DA), row(DP), row(DP), _full((1, D)), _full((1, D)),
                  _full((1, DA)), _full((1, DP)), _resident((D, D)), _full((len(WINS), PC, PC)), _full((1, DP)), ANY],
        out_specs=[row(D), row(DA), row(DP), _full((1, D)), _full((1, D)), _full((1, DA)), _full((1, DP)),
                   _full((1, DP)), _full((len(WINS), PC, PC)), _full((D, D))],
        out_shape=[jax.ShapeDtypeStruct((t, D), F32), jax.ShapeDtypeStruct((t, DA), F32), jax.ShapeDtypeStruct((t, DP), F32),
                   jax.ShapeDtypeStruct((1, D), F32), jax.ShapeDtypeStruct((1, D), F32), jax.ShapeDtypeStruct((1, DA), F32),
                   jax.ShapeDtypeStruct((1, DP), F32), jax.ShapeDtypeStruct((1, DP), F32),
                   jax.ShapeDtypeStruct((len(WINS), PC, PC), F32), jax.ShapeDtypeStruct((D, D), F32)],
        compiler_params=_cp(),
    )(d_hn2, dh2, h1, o, a, m, yb, g_ffn_pre, g_post, g_attn, g_pool, w_out, w_pool, pool_scale, dep)


def _attn_bwd(qa, ka, v, a, d_a, lse, dep):
    t = qa.shape[0]
    ta = _tile(t)
    n = t // ta

    def body(q_ref, k_ref, v_ref, o_ref, do_ref, lse_ref, dep_ref, dq_ref, dk_ref, dv_ref):
        j = pl.program_id(1)

        @pl.when(j == 0)
        def _():
            dq_ref[...] = jnp.zeros_like(dq_ref)

        dk_ref[...] = jnp.zeros_like(dk_ref)
        dv_ref[...] = jnp.zeros_like(dv_ref)
        ks = [k_ref[:, LANE * h:LANE * (h + 1)] for h in range(2)]
        v2 = v_ref[...]
        lane = lax.broadcasted_iota(jnp.int32, (1, LANE), 1)
        mine = [lane < HD, lane >= HD]

        def tile(i, masked):
            rows = pl.ds(pl.multiple_of(i * ta, ta), ta)
            do2 = do_ref[rows, :]
            prod = do2 * o_ref[rows, :]
            lse2 = lse_ref[rows, :]
            do2b = do2.astype(BF)
            qh = [q_ref[rows, LANE * h:LANE * (h + 1)] for h in range(2)]
            s = [_mm_nt(qh[h], ks[h]) for h in range(2)]
            dp = [_mm_nt(jnp.where(mine[h], do2, 0.0), v2) for h in range(2)]
            delta = [jnp.sum(jnp.where(mine[h], prod, 0.0), axis=1, keepdims=True) for h in range(2)]
            lse_h = [jnp.sum(jnp.where(lane == HD * h, lse2, 0.0), axis=1, keepdims=True) for h in range(2)]
            pr = [jnp.exp(s[h] - lse_h[h]) for h in range(2)]
            if masked:
                keep = (lax.broadcasted_iota(jnp.int32, (ta, ta), 1) <= lax.broadcasted_iota(jnp.int32, (ta, ta), 0))
                pr = [jnp.where(keep, ph, 0.0) for ph in pr]
            ds = [(pr[h] * (dp[h] - delta[h])).astype(BF) for h in range(2)]
            dv_ref[...] = dv_ref[...] + jnp.where(mine[0], _mm_tn(pr[0], do2b), _mm_tn(pr[1], do2b))
            for h in range(2):
                sl = slice(LANE * h, LANE * (h + 1))
                dk_ref[:, sl] = dk_ref[:, sl] + _mm_tn(ds[h], qh[h])
                dq_ref[0, rows, sl] = dq_ref[0, rows, sl] + jnp.dot(ds[h], ks[h], preferred_element_type=F32)

        def step(i, carry):
            tile(i, False)
            return carry

        tile(j, True)
        lax.fori_loop(j + 1, n, step, 0)

    qrow = lambda w: pl.BlockSpec((t, w), lambda p, j: (0, p))
    krow = lambda w: pl.BlockSpec((ta, w), lambda p, j: (j, p))
    return pl.pallas_call(
        body, name="attn_bwd", grid=(NH // 2, n),
        in_specs=[qrow(2 * LANE), krow(2 * LANE), krow(LANE), qrow(LANE), qrow(LANE), qrow(LANE), ANY],
        out_specs=[pl.BlockSpec((1, t, 2 * LANE), lambda p, j: (p, 0, 0)), krow(2 * LANE), krow(LANE)],
        out_shape=[jax.ShapeDtypeStruct((NH // 2, t, 2 * LANE), F32), jax.ShapeDtypeStruct((t, NH * LANE), F32),
                   jax.ShapeDtypeStruct((t, DA), F32)],
        compiler_params=_cp(),
    )(qa, ka, v, a, d_a, lse, dep)


def _in_bwd(dqa, dka, dv, dyc, fl, x, dh1, g1, w_in_t, tri_u, dep):
    t = x.shape[0]
    tt = _tile(t)
    nt = t // tt
    hb = tt // HALO
    rev = lambda s: nt - 1 - s

    def body(dqa_ref, dka_ref, dv_ref, dyc_ref, dyn_ref, fl_ref, x_ref, dh1_ref, g_ref, w_ref, tri_ref,
             dep_ref, dx_ref, dw_ref, dg_ref, db_ref, carry, acc, sem):
        s = pl.program_id(0)
        i = nt - 1 - s

        @pl.when(s == 0)
        def _():
            carry[...] = jnp.zeros_like(carry)
            acc[...] = jnp.zeros_like(acc)
            dg_ref[...] = jnp.zeros_like(dg_ref)
            db_ref[...] = jnp.zeros_like(db_ref)

        dq_cat = jnp.concatenate([dqa_ref[p] for p in range(NH // 2)], axis=1)
        dk_cat = dka_ref[...]
        off = lax.broadcasted_iota(jnp.int32, (1, NH * LANE), 1)
        off = off % LANE - HD - off // LANE

        def picked(cat, lane_off):
            kept = jnp.where(off == lane_off, cat, 0.0)
            return functools.reduce(lambda a, b: a + b, [kept[:, LANE * h:LANE * (h + 1)] for h in range(NH)])

        dc = pltpu.roll(picked(dq_cat, 0), LANE - HD, 1) - pltpu.roll(picked(dk_cat, 24), LANE - HD - 24, 1)
        dlf = carry[...] + _dot3(tri_ref[...], dc)
        carry[...] = carry[...] + jnp.sum(dc, axis=0, keepdims=True)
        flv = fl_ref[...]
        lane = lax.broadcasted_iota(jnp.int32, flv.shape, 1)
        d_fl = jnp.where(lane < NH, dlf / (1.0 + jnp.exp(flv)), 0.0)
        db_ref[...] = db_ref[...] + jnp.sum(d_fl, axis=0, keepdims=True)
        low = lax.broadcasted_iota(jnp.int32, (tt, LANE), 1) < HD
        dqs, dks = [], []
        for p in range(NH // 2):
            b0, b1 = slice(2 * LANE * p, 2 * LANE * p + LANE), slice(2 * LANE * p + LANE, 2 * LANE * (p + 1))
            dqs.append(jnp.where(low, dq_cat[:, b0], pltpu.roll(dq_cat[:, b1], HD, 1)) * (1.0 / math.sqrt(HD)))
            dks.append(jnp.where(low, dk_cat[:, b0], pltpu.roll(dk_cat[:, b1], HD, 1)))
        nxt = dyn_ref[...] * jnp.where(i < nt - 1, 1.0, 0.0)
        tok = i * tt + lax.broadcasted_iota(jnp.int32, (tt, PC), 0)
        dus = []
        for g, w in enumerate(WINS):
            sl = slice(PC * g, PC * (g + 1))
            dycg = dyc_ref[:, sl]
            ext = jnp.concatenate([dycg, nxt[:, sl]], axis=0)
            dus.append(_window_sum(ext, w, False)[:tt] - dycg * jnp.minimum(tok + 1, w).astype(F32))
        d_z = jnp.concatenate(dqs + dks + [dv_ref[...], d_fl] + dus, axis=1).astype(BF)
        xv = x_ref[...]
        gv = g_ref[...]
        hn = (xv * _rstd(xv) * gv).astype(BF)
        d_hn = jnp.dot(d_z, w_ref[...], preferred_element_type=F32)
        acc[...] = acc[...] + _mm_tn(d_z, hn)
        d1, dg = _rms_bwd(d_hn, xv, gv)
        dg_ref[...] = dg_ref[...] + dg
        dx_ref[...] = dh1_ref[...] + d1

        @pl.when(s == nt - 1)
        def _():
            cp = pltpu.make_async_copy(acc, dw_ref, sem)
            cp.start()
            cp.wait()

    row = lambda w: pl.BlockSpec((tt, w), lambda s: (rev(s), 0))
    return pl.pallas_call(
        body, name="in_bwd", grid=(nt,),
        in_specs=[pl.BlockSpec((NH // 2, tt, 2 * LANE), lambda s: (0, rev(s), 0)), row(NH * LANE), row(DA), row(DP),
                  pl.BlockSpec((HALO, DP), lambda s: (jnp.minimum((rev(s) + 1) * hb, nt * hb - 1), 0)),
                  row(LANE), row(D), row(D), _full((1, D)), _resident((ZW, D)), _full((tt, tt)),
                  ANY],
        out_specs=[row(D), pl.BlockSpec(memory_space=pl.ANY), _full((1, D)), _full((1, LANE))],
        out_shape=[jax.ShapeDtypeStruct((t, D), F32), jax.ShapeDtypeStruct((ZW, D), F32),
                   jax.ShapeDtypeStruct((1, D), F32), jax.ShapeDtypeStruct((1, LANE), F32)],
        scratch_shapes=[pltpu.VMEM((1, LANE), F32), pltpu.VMEM((ZW, D), F32), pltpu.SemaphoreType.DMA],
        compiler_params=_cp(),
    )(dqa, dka, dv, dyc, dyc, fl, x, dh1, g1, w_in_t, tri_u, dep)


class _NoComm:
    def __init__(self, w2):
        self.w2 = w2
        self.dep = jnp.zeros((8, LANE), F32)

    def after_attention(self, after):
        return self.dep

    def weights2(self, after):
        return _stack2_full(*self.w2)

    def after_ffn(self, grads2):
        self.grads2 = grads2
        return self.dep

    def after_mix(self, after, early):
        self.early = early
        return self.dep

    def after_attn(self, after):
        return self.dep


def _local_step(x, p, tgt, sm, w1, comm):
    w_in_t, w_out = w1
    tt = _tile(x.shape[0])
    eq, ek, rowq, rowk = _aug_consts()
    b_pad = jnp.pad(sm["b_forget"], ((0, 0), (0, LANE - NH)))
    qa, ka, v, u, fl = _in_proj(x, sm["g_mix_pre"], w_in_t, b_pad, _tri(tt, False), eq, ek, rowq, rowk, comm.dep)
    a, lse = _attn_fwd(qa, ka, v)
    yb, m, o, h1 = _mix_out(a, u, x, sm["w_pool"], sm["pool_scale"], sm["g_attn_grp"],
                            sm["g_pool_grp"], w_out, sm["g_mix_post"], comm.after_attention(a))
    stacks2 = comm.weights2(h1)
    wg_t, wu_t, wd, w_ple, w_pg = _unstack2_full(stacks2)
    hn2, gate, up, dff, dh2, loss, dwpg, dwple, dgple, dgfpost = _ffn_fwd(
        h1, sm["g_ffn_pre"], stacks2, sm["g_ffn_post"], p, w_ple, sm["g_ple"], w_pg, tgt)
    chunks = lambda a: a.reshape(DFF // FF_CH, FF_CH, D)
    dwg_t, dwu_t, dwd, d_hn2 = _ffn_bwd(hn2, gate, up, dff, jnp.concatenate([chunks(wg_t), chunks(wu_t)], axis=1), wd)
    dep = comm.after_ffn((dwg_t, dwu_t, dwd, dwple, dwpg))
    dh1, d_a, dyc, dgfpre, dgpost, dgattn, dgpool, dps, dwpool, dwout = _mix_bwd(
        d_hn2, dh2, h1, o, a, m, yb, sm["g_ffn_pre"], sm["g_mix_post"], sm["g_attn_grp"], sm["g_pool_grp"],
        w_out, sm["w_pool"], sm["pool_scale"], dep)
    early = dict(loss=loss[0:1, 0:1], g_attn_grp=dgattn, g_pool_grp=dgpool, w_pool=dwpool, pool_scale=dps,
                 g_mix_post=dgpost, g_ffn_pre=dgfpre, g_ffn_post=dgfpost, g_ple=dgple)
    dqa, dka, dvv = _attn_bwd(qa, ka, v, a, d_a, lse, comm.after_mix(dh1, early))
    dx, dwin_t, dg1, dbf = _in_bwd(dqa, dka, dvv, dyc, fl, x, dh1, sm["g_mix_pre"], w_in_t, _tri(tt, True),
                                   comm.after_attn(dvv))
    return dx, (dwin_t, dwout), dict(g_mix_pre=dg1, b_forget=dbf[:, :NH])


def _place():
    x, y, c = lax.axis_index("x"), lax.axis_index("y"), lax.axis_index("c")
    return x, y, c, [(1 - x, y), (x, 1 - y), (1 - x, 1 - y)]


def _rows(c, h):
    return pl.ds(pl.multiple_of(c * h, 16), h)


def _plan_gather(h):
    def plan(src, land):
        x, y, c, chips = _place()
        return [(src.at[_rows(c, h), :], land.at[2 * x + y, _rows(c, h), :], (cx, cy, c),
                 land.at[2 * cx + cy, _rows(c, h), :]) for cx, cy in chips]
    return plan


def _plan_forward(h):
    def plan(land_in, own, land):
        x, y, c, chips = _place()
        sib, me = (x, y, 1 - c), 2 * x + y
        return ([(land_in.at[2 * cx + cy, _rows(c, h), :], land.at[2 * cx + cy, _rows(c, h), :], sib,
                  land.at[2 * cx + cy, _rows(1 - c, h), :]) for cx, cy in chips]
                + [(own, land.at[me], sib, land.at[me])])
    return plan


def _plan_swap_halves(h):
    def plan(buf_in, buf):
        x, y, c, _ = _place()
        return [(buf_in.at[_rows(c, h), :], buf.at[_rows(c, h), :], (x, y, 1 - c), buf.at[_rows(1 - c, h), :])]
    return plan


def _plan_pair_rows(h):
    def plan(src, land):
        x, y, c, _ = _place()
        return [(src.at[:, _rows(1 - c, h), :], land, (x, y, 1 - c), land)]
    return plan


def _plan_scatter(src, land):
    x, y, c, chips = _place()
    return [(src.at[2 * cx + cy], land.at[k], (cx, cy, c), land.at[k]) for k, (cx, cy) in enumerate(chips)]


def _plan_scatter8(h):
    def plan(src, land):
        x, y, c, chips = _place()
        copies = [(src.at[2 * x + y, _rows(1 - c, h), :], land.at[0], (x, y, 1 - c), land.at[0])]
        for k, (cx, cy) in enumerate(chips):
            for d, other in enumerate((c, 1 - c)):
                copies.append((src.at[2 * cx + cy, _rows(other, h), :], land.at[1 + 2 * k + c], (cx, cy, other),
                               land.at[1 + 2 * k + other]))
        return copies
    return plan


def _plan_all(src, land):
    x, y, c, _ = _place()
    copies = []
    for r in range(1, 8):
        px, py, pc = (1 - a if b else a for a, b in zip((x, y, c), (r >> 2 & 1, r >> 1 & 1, r & 1)))
        copies.append((src, land.at[4 * x + 2 * y + c], (px, py, pc), land.at[4 * px + 2 * py + pc]))
    return copies


def _remote(src, dst, send_sems, recv_sems, k, peer):
    return pltpu.make_async_remote_copy(src_ref=src, dst_ref=dst, send_sem=send_sems.at[k], recv_sem=recv_sems.at[k],
                                        device_id=peer, device_id_type=MESH)


def _exchange(name, n, src, land, plan):
    def body(src_ref, land_ref, send_sems, recv_sems):
        copies = plan(src_ref, land_ref)
        for k, (s, d, peer, _) in enumerate(copies):
            _remote(s, d, send_sems, recv_sems, k, peer).start()
        for k, (s, _, peer, mine) in enumerate(copies):
            _remote(s, mine, send_sems, recv_sems, k, peer).wait_recv()
        for k, (s, d, peer, _) in enumerate(copies):
            _remote(s, d, send_sems, recv_sems, k, peer).wait_send()

    return pl.pallas_call(
        body, name=name, in_specs=[ANY], out_specs=ANY, out_shape=land,
        scratch_shapes=[pltpu.SemaphoreType.DMA((n,)), pltpu.SemaphoreType.DMA((n,))],
    )(src)


def _exchange_inplace(name, n, buf, extra, plan):
    def body(*refs):
        ins, buf_ref, send_sems, recv_sems = refs[:1 + len(extra)], refs[1 + len(extra)], refs[-2], refs[-1]
        copies = plan(*ins, buf_ref)
        for k, (s, d, peer, _) in enumerate(copies):
            _remote(s, d, send_sems, recv_sems, k, peer).start()
        for k, (s, _, peer, mine) in enumerate(copies):
            _remote(s, mine, send_sems, recv_sems, k, peer).wait_recv()
        for k, (s, d, peer, _) in enumerate(copies):
            _remote(s, d, send_sems, recv_sems, k, peer).wait_send()

    return pl.pallas_call(
        body, name=name, in_specs=[ANY] * (1 + len(extra)), out_specs=ANY, out_shape=_sds(buf.shape, buf.dtype),
        input_output_aliases={0: 0},
        scratch_shapes=[pltpu.SemaphoreType.DMA((n,)), pltpu.SemaphoreType.DMA((n,))],
    )(buf, *extra)


HBM = pl.BlockSpec(memory_space=pltpu.HBM)
SEM = pl.BlockSpec(memory_space=pltpu.SEMAPHORE)
EFFECT = pltpu.SideEffectType.DATAFLOW_SIDE_EFFECTING


def _exchange_start(name, n, src, land, plan):
    def body(src_ref, land_ref, send_sems, recv_sems, src_thru, land_thru, token):
        for k, (s, d, peer, _) in enumerate(plan(src_ref, land_ref)):
            _remote(s, d, send_sems, recv_sems, k, peer).start()
        token[...] = jnp.zeros_like(token)

    return pl.pallas_call(
        body, name=name,
        out_shape=(pltpu.SemaphoreType.DMA((n,)), pltpu.SemaphoreType.DMA((n,)), pltpu.HBM(src.shape, src.dtype),
                   pltpu.HBM(land.shape, land.dtype), jax.ShapeDtypeStruct((8, LANE), F32)),
        in_specs=(HBM, HBM), out_specs=(SEM, SEM, HBM, HBM, pl.BlockSpec(memory_space=pltpu.VMEM)),
        input_output_aliases={0: 2, 1: 3},
        compiler_params=pltpu.CompilerParams(has_side_effects=EFFECT),
    )(pltpu.with_memory_space_constraint(src, pltpu.HBM), pltpu.with_memory_space_constraint(land, pltpu.HBM))


def _exchange_wait(name, started, plan, after):
    send_sems, recv_sems, src, land, _ = started

    def body(src_ref, land_ref, send_sems, recv_sems, after_ref, src_out, land_out):
        for k, (s, _, peer, mine) in enumerate(plan(src_ref, land_ref)):
            cp = _remote(s, mine, send_sems, recv_sems, k, peer)
            cp.wait_send()
            cp.wait_recv()

    return pl.pallas_call(
        body, name=name, out_shape=(pltpu.HBM(src.shape, src.dtype), pltpu.HBM(land.shape, land.dtype)),
        in_specs=(HBM, HBM, SEM, SEM, ANY), out_specs=(HBM, HBM), input_output_aliases={0: 0, 1: 1},
        compiler_params=pltpu.CompilerParams(has_side_effects=EFFECT),
    )(src, land, send_sems, recv_sems, after)


def _pair_sum(name, cidx, g, recv, br):
    h = recv.shape[1]
    nb = h // br

    def body(c_ref, g_ref, r_ref, out_ref):
        out_ref[...] = (g_ref[...] + r_ref[...]).astype(BF)

    return pl.pallas_call(
        body, name=name,
        grid_spec=pltpu.PrefetchScalarGridSpec(
            num_scalar_prefetch=1, grid=(NSHARD, nb),
            in_specs=[pl.BlockSpec((1, br, D), lambda s, i, c: (s, c[0] * nb + i, 0)),
                      pl.BlockSpec((1, br, D), lambda s, i, c: (s, i, 0))],
            out_specs=pl.BlockSpec((1, br, D), lambda s, i, c: (s, i, 0))),
        out_shape=jax.ShapeDtypeStruct((NSHARD, h, D), BF),
    )(cidx, g, recv)


def _chip_sum(name, place, pb, y, br):
    h = y.shape[1]
    nb = h // br

    def body(pl_ref, p_ref, y_ref, out_ref):
        acc = p_ref[0].astype(F32)
        for k in range(NSHARD - 1):
            acc = acc + y_ref[k].astype(F32)
        out_ref[...] = acc

    return pl.pallas_call(
        body, name=name,
        grid_spec=pltpu.PrefetchScalarGridSpec(
            num_scalar_prefetch=1, grid=(nb,),
            in_specs=[pl.BlockSpec((1, br, D), lambda i, s: (s[0], i, 0)),
                      pl.BlockSpec((NSHARD - 1, br, D), lambda i, s: (0, i, 0))],
            out_specs=pl.BlockSpec((br, D), lambda i, s: (s[1] * nb + i, 0))),
        out_shape=jax.ShapeDtypeStruct((2 * h, D), F32),
    )(place, pb, y)


def _sum8(name, place, g, land, br):
    h = land.shape[1]
    nb = h // br

    def body(pl_ref, g_ref, y_ref, out_ref):
        acc = g_ref[0].astype(F32)
        for k in range(land.shape[0]):
            acc = acc + y_ref[k].astype(F32)
        out_ref[...] = acc

    return pl.pallas_call(
        body, name=name,
        grid_spec=pltpu.PrefetchScalarGridSpec(
            num_scalar_prefetch=1, grid=(nb,),
            in_specs=[pl.BlockSpec((1, br, D), lambda i, s: (s[0], s[1] * nb + i, 0)),
                      pl.BlockSpec((land.shape[0], br, D), lambda i, s: (0, i, 0))],
            out_specs=pl.BlockSpec((br, D), lambda i, s: (s[1] * nb + i, 0))),
        out_shape=jax.ShapeDtypeStruct((2 * h, D), F32), compiler_params=_cp(),
    )(place, g, land)


def _sum_slots(name, v):
    def body(in_ref, out_ref):
        acc = in_ref[0]
        for k in range(1, 8):
            acc = acc + in_ref[k]
        out_ref[...] = acc

    vm = pl.BlockSpec(memory_space=pltpu.VMEM)
    return pl.pallas_call(body, name=name, in_specs=[vm], out_specs=vm,
                          out_shape=jax.ShapeDtypeStruct(v.shape[1:], F32))(v)


def _adamw_math(w, g, m, v):
    m = ADAM_B1 * m + (1.0 - ADAM_B1) * g
    v = ADAM_B2 * v + (1.0 - ADAM_B2) * (g * g)
    m_hat = m / (1.0 - ADAM_B1 ** ADAM_STEP)
    v_hat = v / (1.0 - ADAM_B2 ** ADAM_STEP)
    delta = -ADAM_LR * (m_hat / (jnp.sqrt(v_hat) + ADAM_EPS) + ADAM_WD * w)
    return delta, m, v


def _adamw(w, g, m, v, dep, row0=None):
    r, c = w.shape
    br = next(b for b in (256, 176, 128, r) if r % b == 0 and (row0 or 0) % b == 0)
    first = (row0 or 0) // br

    def body(w_ref, g_ref, m_ref, v_ref, dep_ref, *outs):
        gv = g_ref[...]
        outs[-3][...], outs[-2][...], outs[-1][...] = _adamw_math(w_ref[...], gv, m_ref[...], v_ref[...])
        if row0 is not None:
            outs[0][...] = gv

    spec = pl.BlockSpec((br, c), lambda i: (i, 0))
    n_out = 3 if row0 is None else 4
    out = pl.pallas_call(
        body, name="adamw", grid=(r // br,),
        in_specs=[spec, pl.BlockSpec((br, c), lambda i: (first + i, 0)), spec, spec, ANY], out_specs=[spec] * n_out,
        out_shape=[jax.ShapeDtypeStruct((r, c), F32)] * n_out, compiler_params=_cp(),
    )(w, g, m, v, dep)
    return out if row0 is not None else [g] + list(out)


def _adamw_small(ws, gs, ms, vs):
    n = len(ws)

    def body(*refs):
        ins, outs = refs[:4 * n], refs[4 * n:]
        for k in range(n):
            d, m, v = _adamw_math(ins[k][...], ins[n + k][...], ins[2 * n + k][...], ins[3 * n + k][...])
            outs[k][...] = d
            outs[n + k][...] = m
            outs[2 * n + k][...] = v

    vm = pl.BlockSpec(memory_space=pltpu.VMEM)
    out = pl.pallas_call(
        body, name="adamw_small", in_specs=[vm] * (4 * n), out_specs=[vm] * (3 * n),
        out_shape=[jax.ShapeDtypeStruct(w.shape, F32) for w in ws] * 3,
    )(*ws, *gs, *ms, *vs)
    return out[:n], out[n:2 * n], out[2 * n:]


BIG = ("w_in", "w_out", "w_ffn_gate", "w_ffn_up", "w_ffn_down", "w_ple_proj", "w_ple_gate")
SMALL = ("g_mix_pre", "b_forget", "g_attn_grp", "g_pool_grp", "w_pool", "pool_scale", "g_mix_post", "g_ffn_pre",
         "g_ffn_post", "g_ple")
TRANSPOSED = ("w_in", "w_ffn_gate", "w_ffn_up")
VECTORS = tuple(n for n in SMALL if n != "w_pool")
ORDER = ("g_mix_pre", "w_in", "b_forget", "g_attn_grp", "g_pool_grp", "w_pool", "pool_scale", "w_out", "g_mix_post",
         "g_ffn_pre", "w_ffn_gate", "w_ffn_up", "w_ffn_down", "g_ffn_post", "w_ple_proj", "g_ple", "w_ple_gate")


def _pad_rows(a, rows):
    return jnp.pad(a, ((0, rows - a.shape[0]), (0, 0)))


def _stack1(w_in, w_out):
    return _pad_rows(jnp.concatenate([_pad_rows(w_in.T, IN_PAD), w_out], axis=0), ROWS1)


def _stack2(wg, wu, wd, wple, wpg):
    return _pad_rows(jnp.concatenate([wg.T, wu.T, wd, wple.reshape(DPLE // NSHARD, D), wpg], axis=0), ROWS2)


def _unstack1(s):
    return s[:IN_SH], s[O1_OUT:USED1]


def _cat(g, lo, hi):
    return g[:, lo:hi].reshape(NSHARD * (hi - lo), D)


def _unstack1_full(g):
    w_in_t = _cat(g, 0, IN_SH)
    w_in_t = jnp.concatenate([w_in_t[:3 * DA], _pad_rows(w_in_t[3 * DA:3 * DA + NH], LANE), w_in_t[3 * DA + NH:]], axis=0)
    return w_in_t, _cat(g, O1_OUT, USED1)


def _unstack2_full(g):
    w_ple = g[:, O2_PLE:O2_PG].reshape(NSHARD, DPLE, DPLE).transpose(1, 0, 2).reshape(DPLE, D)
    return _cat(g, 0, O2_U), _cat(g, O2_U, O2_D), _cat(g, O2_D, O2_PLE), w_ple, _cat(g, O2_PG, USED2)


def _shards(a):
    return a.reshape(NSHARD, a.shape[0] // NSHARD, D)


def _stack1_full(dwin_t, dwout):
    dwin_t = jnp.concatenate([dwin_t[:3 * DA + NH], dwin_t[3 * DA + LANE:]], axis=0).reshape(NSHARD, IN_SH, D)
    zeros = lambda r: jnp.zeros((NSHARD, r, D), F32)
    return jnp.concatenate([dwin_t, zeros(IN_PAD - IN_SH), _shards(dwout), zeros(ROWS1 - USED1)], axis=1)


def _stack2_full(dwg_t, dwu_t, dwd, dwple, dwpg):
    dwple = dwple.reshape(DPLE, NSHARD, DPLE).transpose(1, 0, 2).reshape(NSHARD, DPLE // NSHARD, D)
    return jnp.concatenate([_shards(dwg_t), _shards(dwu_t), _shards(dwd), dwple, _shards(dwpg),
                            jnp.zeros((NSHARD, ROWS2 - USED2, D), dwd.dtype)], axis=1)


def _sds(shape, dtype):
    return jax.ShapeDtypeStruct(shape, dtype)


class _Comm:
    def __init__(self, stack2, me, c):
        self.stack2, self.me, self.c = stack2, me, c
        self.cidx = c.astype(jnp.int32).reshape(1)
        self.place = jnp.stack([me, c]).astype(jnp.int32)
        self.h = ROWS2 // 2
        self.gather = _exchange_start("gather2_start", 3, stack2, lax.empty((NSHARD, ROWS2, D), BF), _plan_gather(self.h))
        self.dep = self.gather[4]

    def after_attention(self, after):
        own, land = _exchange_wait("gather2_wait", self.gather, _plan_gather(self.h), after)
        fwd = _plan_forward(self.h)
        self.forward = lambda own_ref, land_ref: fwd(land_ref, own_ref, land_ref)
        self.passing = _exchange_start("forward2_start", 4, own, land, self.forward)
        return self.passing[4]

    def weights2(self, after):
        return _exchange_wait("forward2_wait", self.passing, self.forward, after)[1]

    def after_ffn(self, grads2):
        g = _stack2_full(*[a.astype(BF) for a in grads2])
        self.scatter = _plan_scatter8(self.h)
        self.chip = _exchange_start("reduce2_start", 7, g, lax.empty((7, self.h, D), BF), self.scatter)
        return self.chip[4]

    def after_mix(self, after, early):
        self.early_shapes = {n: early[n].shape for n in early}
        self.small = self.start_small("small", early)
        return self.small[4]

    def after_attn(self, after):
        g, y = _exchange_wait("reduce2_wait", self.chip, self.scatter, after)
        f = _sum8("sum2", self.place, g, y, RED2 // 2)
        self.early = self.finish_small("small", self.small, self.early_shapes, after)
        swap = _plan_swap_halves(self.h)
        self.swap = lambda _, buf: swap(buf, buf)
        self.swapping = _exchange_start("reduce2_gather_start", 1, self.dep, f, self.swap)
        return self.swapping[4]

    def reduced2(self, after):
        return _exchange_wait("reduce2_gather_wait", self.swapping, self.swap, after)[1]

    def start_small(self, name, small):
        v = _pack_small(small)
        return _exchange_start(name + "_start", 7, v, lax.empty((8,) + v.shape, F32), _plan_all)

    def finish_small(self, name, started, shapes, after):
        v, land = _exchange_wait(name + "_wait", started, _plan_all, after)
        land = lax.dynamic_update_slice(land, v[None], (2 * self.me + self.c, 0, 0))
        return _unpack_small(_sum_slots(name + "_sum", land), shapes)


def _pack_small(small):
    parts = []
    for name in small:
        flat = small[name].reshape(-1)
        parts.append(jnp.pad(flat, (0, -flat.shape[0] % LANE)).reshape(-1, LANE))
    v = jnp.concatenate(parts, axis=0)
    return _pad_rows(v, v.shape[0] + (-v.shape[0] % 8))


def _unpack_small(v, shapes):
    out, r = {}, 0
    for name in shapes:
        n = math.prod(shapes[name])
        rows = -(-n // LANE)
        out[name] = v[r:r + rows].reshape(-1)[:n].reshape(shapes[name])
        r += rows
    return out


def kernel(x, p, g_mix_pre, w_in, b_forget, g_attn_grp, g_pool_grp, w_pool, pool_scale, w_out, g_mix_post, g_ffn_pre, w_ffn_gate, w_ffn_up, w_ffn_down, g_ffn_post, w_ple_proj, g_ple, w_ple_gate, loss_target, m_g_mix_pre, m_w_in, m_b_forget, m_g_attn_grp, m_g_pool_grp, m_w_pool, m_pool_scale, m_w_out, m_g_mix_post, m_g_ffn_pre, m_w_ffn_gate, m_w_ffn_up, m_w_ffn_down, m_g_ffn_post, m_w_ple_proj, m_g_ple, m_w_ple_gate, v_g_mix_pre, v_w_in, v_b_forget, v_g_attn_grp, v_g_pool_grp, v_w_pool, v_pool_scale, v_w_out, v_g_mix_post, v_g_ffn_pre, v_w_ffn_gate, v_w_ffn_up, v_w_ffn_down, v_g_ffn_post, v_w_ple_proj, v_g_ple, v_w_ple_gate):
    args = dict(locals())
    strip = lambda n, a: a if n in VECTORS else a[0]
    w = {n: strip(n, args[n]) for n in ORDER}
    mom = {n: strip(n, args["m_" + n]) for n in ORDER}
    var = {n: strip(n, args["v_" + n]) for n in ORDER}
    sm = {n: w[n] for n in SMALL}

    c = lax.axis_index("c")
    me = 2 * lax.axis_index("x") + lax.axis_index("y")
    h1 = ROWS1 // 2
    bf = lambda n: w[n].astype(BF)
    stack1 = _stack1(bf("w_in"), bf("w_out"))
    stack2 = _stack2(*[bf(n) for n in BIG[2:]])
    land = _exchange("gather1", 3, stack1, _sds((NSHARD, ROWS1, D), BF), _plan_gather(h1))
    land, stack2 = lax.optimization_barrier((land, stack2))
    comm = _Comm(stack2, me, c)
    w1 = _unstack1_full(_exchange_inplace("gather1_forward", 4, land, (stack1,), _plan_forward(h1)))
    dx, grads1, late = _local_step(x[0], p[0, 0], loss_target[0], sm, w1, comm)

    flip = lambda n, a: a.T if n in TRANSPOSED else a
    grads, delta, new_m, new_v = {}, {}, {}, {}

    def update(n, g, dep, row0=None):
        g_, d_, m_, v_ = _adamw(flip(n, w[n]), g, flip(n, mom[n]), flip(n, var[n]), dep, row0)
        grads[n], delta[n], new_m[n], new_v[n] = flip(n, g_), flip(n, d_), flip(n, m_), flip(n, v_)
        return v_

    late_shapes = {n: late[n].shape for n in late}
    small2 = comm.start_small("small2", late)
    red2 = comm.reduced2(dx)
    g1 = _stack1_full(*grads1)
    pair1 = _exchange_start("reduce1_pair_start", 1, g1, lax.empty((NSHARD, h1, D), F32), _plan_pair_rows(h1))
    dep = update("w_ple_gate", red2, pair1[4] + small2[4], O2_PG)
    dep = update("w_ple_proj", red2[O2_PLE:O2_PG].reshape(DPLE, DPLE), dep)
    g1, recv = _exchange_wait("reduce1_pair_wait", pair1, _plan_pair_rows(h1), dep)
    pb = _pair_sum("pair_sum1", comm.cidx, g1, recv, RED1)
    chip1 = _exchange_start("reduce1_chip_start", 3, pb, lax.empty((NSHARD - 1, h1, D), BF), _plan_scatter)
    dep = update("w_ffn_gate", red2, chip1[4], 0)
    dep = update("w_ffn_up", red2, dep, O2_U)
    dep = update("w_ffn_down", red2, dep, O2_D)
    red_small = {**comm.early, **comm.finish_small("small2", small2, late_shapes, dep)}
    loss = 0.5 / D * red_small["loss"][0, 0]
    for n in SMALL:
        grads[n] = red_small[n].reshape(w[n].shape)
    two_d = lambda a: a.reshape(-1, a.shape[-1])
    ds, ms, vs = _adamw_small([two_d(w[n]) for n in SMALL], [two_d(grads[n]) for n in SMALL],
                              [two_d(mom[n]) for n in SMALL], [two_d(var[n]) for n in SMALL])
    for k, n in enumerate(SMALL):
        delta[n], new_m[n], new_v[n] = ds[k].reshape(w[n].shape), ms[k].reshape(w[n].shape), vs[k].reshape(w[n].shape)
    pb, y = _exchange_wait("reduce1_chip_wait", chip1, _plan_scatter, vs[0])
    f = _chip_sum("chip_sum1", comm.place, pb, y, RED1)
    reduced1 = _exchange_inplace("reduce1_gather", 1, f, (), _plan_swap_halves(h1))
    g_in, g_out = _unstack1(reduced1)
    update("w_out", g_out, update("w_in", g_in, reduced1))

    lead = lambda d: [d[n] if n in VECTORS else d[n][None] for n in ORDER]
    return (loss, dx[None], *lead(grads), *lead(delta), *lead(new_m), *lead(new_v))
```

```python
import functools
import math

import jax
import jax.numpy as jnp
import numpy as np
from jax import lax
from jax.experimental import pallas as pl
from jax.experimental.pallas import tpu as pltpu

F32 = jnp.float32
BF = jnp.bfloat16
MESH = pl.DeviceIdType.MESH

D = 1024
DA = 512
DP = 512
NH = 8
HD = 64
DFF = 2816
DPLE = 256
WINS = (2, 4, 8, 16)
PC = 128
ZW = 3 * DA + 128 + DP
EPS = 1e-6
NSHARD = 4

LANE = 128
HALO = 128

IN_SH = 514
IN_PAD = 528
FF_SH = DFF // NSHARD
O1_OUT, USED1, ROWS1 = 528, 784, 800
O2_U, O2_D, O2_PLE, O2_PG, USED2, ROWS2 = 704, 1408, 2112, 2176, 2432, 2560
RED1, RED2 = 400, 640

ADAM_LR, ADAM_B1, ADAM_B2, ADAM_EPS, ADAM_WD, ADAM_STEP = 0.001, 0.9, 0.999, 1e-8, 0.01, 10

VMEM_LIMIT = 56 * 1024 * 1024


def _cp(**kw):
    return pltpu.CompilerParams(vmem_limit_bytes=VMEM_LIMIT, **kw)


def _mm(a, b):
    return jnp.dot(a.astype(BF), b.astype(BF), preferred_element_type=F32)


def _mm_nt(a, b):
    return lax.dot_general(a.astype(BF), b.astype(BF), (((1,), (1,)), ((), ())), preferred_element_type=F32)


def _mm_tn(a, b):
    return lax.dot_general(a.astype(BF), b.astype(BF), (((0,), (0,)), ((), ())), preferred_element_type=F32)


def _split3(x):
    hi = x.astype(BF)
    r = x - hi.astype(F32)
    mid = r.astype(BF)
    lo = (r - mid.astype(F32)).astype(BF)
    return hi, mid, lo


def _dot3(m, x):
    hi, mid, lo = _split3(x)
    return (jnp.dot(m, hi, preferred_element_type=F32) + jnp.dot(m, mid, preferred_element_type=F32)
            + jnp.dot(m, lo, preferred_element_type=F32))


def _window_sum(ext, w, back):
    n = ext.shape[0]
    s, k = ext, 1
    while k < w:
        s = s + pltpu.roll(s, k if back else n - k, 0)
        k *= 2
    return s


def _rstd(x):
    return lax.rsqrt(jnp.mean(x * x, axis=-1, keepdims=True) + EPS)


def _rms_bwd(dy, x, g):
    r = _rstd(x)
    xh = x * r
    dg = jnp.sum(dy * xh, axis=0, keepdims=True)
    dxh = dy * g
    dx = r * (dxh - xh * jnp.mean(dxh * xh, axis=-1, keepdims=True))
    return dx, dg


def _sigmoid(x):
    return 1.0 / (1.0 + jnp.exp(-x))


ANY = pl.BlockSpec(memory_space=pl.ANY)


def _full(shape):
    n = len(shape)
    return pl.BlockSpec(shape, lambda *_: (0,) * n)


def _resident(shape):
    n = len(shape)
    return pl.BlockSpec(shape, lambda *_: (0,) * n, pipeline_mode=pl.Buffered(1))


def _tile(t):
    return 512 if t % 512 == 0 else t


def _tri(n, upper):
    r, c = np.indices((n, n))
    return ((c >= r) if upper else (c <= r)).astype(BF)


def _aug_consts():
    row, col = np.indices((3 * LANE, NH * LANE))
    piece, head = row // LANE, row % LANE
    ch, cl = col // LANE, col % LANE
    eq = ((head == ch) & (cl == HD + 8 * piece + head)).astype(BF)
    ek = -((head == ch) & (cl == HD + 24 + 8 * piece + head)).astype(BF)
    off = (np.arange(NH * LANE) % LANE - HD - np.arange(NH * LANE) // LANE)[None, :]
    rowq = ((off >= 24) & (off < 48) & (off % 8 == 0)).astype(np.float32)
    rowk = ((off >= 0) & (off < 24) & (off % 8 == 0)).astype(np.float32)
    return eq, ek, rowq, rowk


def _in_proj(x, g1, w_in_t, b_pad, tri, eq, ek, rowq, rowk, dep):
    t = x.shape[0]
    tt = _tile(t)

    def body(x_ref, g_ref, w_ref, b_ref, tri_ref, eq_ref, ek_ref, rq_ref, rk_ref, dep_ref,
             qa_ref, ka_ref, v_ref, u_ref, fl_ref, carry):
        i = pl.program_id(0)

        @pl.when(i == 0)
        def _():
            carry[...] = jnp.zeros_like(carry)

        xv = x_ref[...]
        hn = (xv * _rstd(xv) * g_ref[...]).astype(BF)
        z = _mm_nt(hn, w_ref[...])
        fl = z[:, 3 * DA:3 * DA + LANE] + b_ref[...]
        lane = lax.broadcasted_iota(jnp.int32, fl.shape, 1)
        lf = jnp.where(lane < NH, jnp.minimum(fl, 0.0) - jnp.log(1.0 + jnp.exp(-jnp.abs(fl))), 0.0)
        c = carry[...] + _dot3(tri_ref[...], lf)
        carry[...] = carry[...] + jnp.sum(lf, axis=0, keepdims=True)
        caug = jnp.concatenate(_split3(c), axis=1)
        aug_q = jnp.dot(caug, eq_ref[...], preferred_element_type=F32) + rq_ref[...]
        aug_k = jnp.dot(caug, ek_ref[...], preferred_element_type=F32) + rk_ref[...]
        low = lax.broadcasted_iota(jnp.int32, (tt, LANE), 1) < HD
        for p in range(NH // 2):
            qp = z[:, LANE * p:LANE * (p + 1)] * (1.0 / math.sqrt(HD))
            kp = z[:, DA + LANE * p:DA + LANE * (p + 1)]
            for h, (qh, kh) in enumerate(((qp, kp), (pltpu.roll(qp, HD, 1), pltpu.roll(kp, HD, 1)))):
                lo_, hi_ = LANE * (2 * p + h), LANE * (2 * p + h + 1)
                qa_ref[:, lo_:hi_] = jnp.where(low, qh, aug_q[:, lo_:hi_]).astype(BF)
                ka_ref[:, lo_:hi_] = jnp.where(low, kh, aug_k[:, lo_:hi_]).astype(BF)
        v_ref[...] = z[:, 2 * DA:3 * DA].astype(BF)
        u_ref[...] = z[:, 3 * DA + LANE:]
        fl_ref[...] = fl

    return pl.pallas_call(
        body, name="in_proj", grid=(t // tt,),
        in_specs=[pl.BlockSpec((tt, D), lambda i: (i, 0)), _full((1, D)), _resident((ZW, D)), _full((1, LANE)),
                  _full((tt, tt)), _full((3 * LANE, NH * LANE)), _full((3 * LANE, NH * LANE)),
                  _full((1, NH * LANE)), _full((1, NH * LANE)), ANY],
        out_specs=[pl.BlockSpec((tt, NH * LANE), lambda i: (i, 0)), pl.BlockSpec((tt, NH * LANE), lambda i: (i, 0)),
                   pl.BlockSpec((tt, DA), lambda i: (i, 0)), pl.BlockSpec((tt, DP), lambda i: (i, 0)),
                   pl.BlockSpec((tt, LANE), lambda i: (i, 0))],
        out_shape=[jax.ShapeDtypeStruct((t, NH * LANE), BF), jax.ShapeDtypeStruct((t, NH * LANE), BF),
                   jax.ShapeDtypeStruct((t, DA), BF), jax.ShapeDtypeStruct((t, DP), F32),
                   jax.ShapeDtypeStruct((t, LANE), F32)],
        scratch_shapes=[pltpu.VMEM((1, LANE), F32)],
        compiler_params=_cp(),
    )(x, g1, w_in_t, b_pad, tri, eq, ek, rowq, rowk, dep)


def _attn_fwd(qa, ka, v):
    t = qa.shape[0]
    ta = _tile(t)
    n = t // ta

    def body(q_ref, k_ref, v_ref, a_ref, lse_ref, m_ref, l_ref, acc_ref):
        i = pl.program_id(1)
        m_ref[...] = jnp.full_like(m_ref, -1e30)
        l_ref[...] = jnp.zeros_like(l_ref)
        acc_ref[...] = jnp.zeros_like(acc_ref)
        qs = [q_ref[:, LANE * h:LANE * (h + 1)] for h in range(2)]
        reps = ta // LANE

        def tile(j, width, masked):
            rows = pl.ds(pl.multiple_of(j * ta, ta), width * ta)
            v2 = v_ref[rows, :]
            s = [_mm_nt(qs[h], k_ref[rows, LANE * h:LANE * (h + 1)]) for h in range(2)]
            if masked:
                keep = (lax.broadcasted_iota(jnp.int32, (ta, ta), 1) <= lax.broadcasted_iota(jnp.int32, (ta, ta), 0))
                s = [jnp.where(keep, sh, -1e30) for sh in s]
            m_old = [m_ref[h] for h in range(2)]
            m_new = [jnp.maximum(m_old[h], jnp.max(s[h], axis=1, keepdims=True)) for h in range(2)]
            pe = [jnp.exp(s[h] - jnp.tile(m_new[h], (1, width * reps))) for h in range(2)]
            alpha = [jnp.exp(m_old[h] - m_new[h]) for h in range(2)]
            pv = [jnp.dot(pe[h].astype(BF), v2, preferred_element_type=F32) for h in range(2)]
            for h in range(2):
                l_ref[h] = alpha[h] * l_ref[h] + jnp.sum(pe[h], axis=1, keepdims=True)
                acc_ref[h] = alpha[h] * acc_ref[h] + pv[h]
                m_ref[h] = m_new[h]

        def step(jj, carry):
            tile(2 * jj, 2, False)
            return carry

        lax.fori_loop(0, i // 2, step, 0)

        @pl.when(i % 2 == 1)
        def _():
            tile(i - 1, 1, False)

        tile(i, 1, True)
        low = lax.broadcasted_iota(jnp.int32, (ta, LANE), 1) < HD
        a_ref[...] = jnp.where(low, acc_ref[0] / l_ref[0], acc_ref[1] / l_ref[1])
        lse_ref[...] = jnp.where(low, m_ref[0] + jnp.log(l_ref[0]), m_ref[1] + jnp.log(l_ref[1]))

    return pl.pallas_call(
        body, name="attn_fwd", grid=(NH // 2, n),
        in_specs=[pl.BlockSpec((ta, 2 * LANE), lambda p, i: (i, p)),
                  pl.BlockSpec((t, 2 * LANE), lambda p, i: (0, p)),
                  pl.BlockSpec((t, LANE), lambda p, i: (0, p))],
        out_specs=[pl.BlockSpec((ta, LANE), lambda p, i: (i, p)), pl.BlockSpec((ta, LANE), lambda p, i: (i, p))],
        out_shape=[jax.ShapeDtypeStruct((t, DA), F32), jax.ShapeDtypeStruct((t, DA), F32)],
        scratch_shapes=[pltpu.VMEM((2, ta, LANE), F32), pltpu.VMEM((2, ta, LANE), F32), pltpu.VMEM((2, ta, LANE), F32)],
        compiler_params=_cp(),
    )(qa, ka, v)


def _mix_out(a, u, x, w_pool, pool_scale, g_attn, g_pool, w_out, g_post, dep):
    t = a.shape[0]
    tt = _tile(t)
    hb = tt // HALO

    def body(a_ref, u_ref, up_ref, x_ref, wp_ref, ps_ref, ga_ref, gp_ref, wo_ref, go_ref, dep_ref,
             yb_ref, m_ref, o_ref, h1_ref):
        i = pl.program_id(0)
        prev = up_ref[...] * jnp.where(i > 0, 1.0, 0.0)
        tok = i * tt + lax.broadcasted_iota(jnp.int32, (tt, PC), 0)
        ms = []
        for g, w in enumerate(WINS):
            ug = u_ref[:, PC * g:PC * (g + 1)]
            ext = jnp.concatenate([prev[:, PC * g:PC * (g + 1)], ug], axis=0)
            cnt = jnp.minimum(tok + 1, w).astype(F32)
            y = (_window_sum(ext, w, True)[HALO:] / cnt - ug).astype(BF)
            yb_ref[:, PC * g:PC * (g + 1)] = y
            ms.append(_mm(y, wp_ref[g]) * ps_ref[:, PC * g:PC * (g + 1)])
        m = jnp.concatenate(ms, axis=1)
        m_ref[...] = m
        av = a_ref[...]
        mix = jnp.concatenate([av * _rstd(av) * ga_ref[...], m * _rstd(m) * gp_ref[...]], axis=1)
        o = _mm(mix, wo_ref[...])
        o_ref[...] = o
        h1_ref[...] = x_ref[...] + o * _rstd(o) * go_ref[...]

    return pl.pallas_call(
        body, name="mix_out", grid=(t // tt,),
        in_specs=[pl.BlockSpec((tt, DA), lambda i: (i, 0)), pl.BlockSpec((tt, DP), lambda i: (i, 0)),
                  pl.BlockSpec((HALO, DP), lambda i: (jnp.maximum(i * hb - 1, 0), 0)),
                  pl.BlockSpec((tt, D), lambda i: (i, 0)),
                  _full((len(WINS), PC, PC)), _full((1, DP)), _full((1, DA)), _full((1, DP)),
                  _resident((D, D)), _full((1, D)), ANY],
        out_specs=[pl.BlockSpec((tt, DP), lambda i: (i, 0)), pl.BlockSpec((tt, DP), lambda i: (i, 0)),
                   pl.BlockSpec((tt, D), lambda i: (i, 0)), pl.BlockSpec((tt, D), lambda i: (i, 0))],
        out_shape=[jax.ShapeDtypeStruct((t, DP), BF), jax.ShapeDtypeStruct((t, DP), F32),
                   jax.ShapeDtypeStruct((t, D), F32), jax.ShapeDtypeStruct((t, D), F32)],
        compiler_params=_cp(),
    )(a, u, u, x, w_pool, pool_scale, g_attn, g_pool, w_out, g_post, dep)


def _ffn_fwd(h1, g_pre, stacks2, g_post, p, w_ple, g_ple, w_pg, tgt):
    t = h1.shape[0]
    tt = 256 if t % 256 == 0 else t

    def body(h1_ref, gpre_ref, wg_ref, wu_ref, wd_ref, gpost_ref, p_ref, wple_ref, gple_ref, wpg_ref, tgt_ref,
             hn_ref, gate_ref, up_ref, dff_ref, dh2_ref, loss_ref, dwpg_ref, dwple_ref, dgple_ref, dgpost_ref):
        i = pl.program_id(0)

        @pl.when(i == 0)
        def _():
            loss_ref[...] = jnp.zeros_like(loss_ref)
            dwpg_ref[...] = jnp.zeros_like(dwpg_ref)
            dwple_ref[...] = jnp.zeros_like(dwple_ref)
            dgple_ref[...] = jnp.zeros_like(dgple_ref)
            dgpost_ref[...] = jnp.zeros_like(dgpost_ref)

        h1v = h1_ref[...]
        hn = (h1v * _rstd(h1v) * gpre_ref[...]).astype(BF)
        hn_ref[...] = hn
        gate = _mm_nt(hn, wg_ref[...].reshape(DFF, D))
        up = _mm_nt(hn, wu_ref[...].reshape(DFF, D))
        for k in range(DFF // FF_CH):
            gate_ref[k] = gate[:, FF_CH * k:FF_CH * (k + 1)].astype(BF)
            up_ref[k] = up[:, FF_CH * k:FF_CH * (k + 1)].astype(BF)
        ff = _mm(gate * _sigmoid(gate) * up, wd_ref[...].reshape(DFF, D))
        rff = _rstd(ff)
        ffh = ff * rff
        gpost = gpost_ref[...]
        h2 = h1v + ffh * gpost
        pv = p_ref[...]
        pe = _mm(pv, wple_ref[...])
        rpe = _rstd(pe)
        peh = pe * rpe
        gple = gple_ref[...]
        e = peh * gple
        sig = _sigmoid(_mm(h2, wpg_ref[...]))
        dv = h2 + sig * e - tgt_ref[...]
        sq = jnp.sum(jnp.sum(dv * dv, axis=1, keepdims=True), axis=0, keepdims=True)
        loss_ref[...] = loss_ref[...] + sq
        dy = dv * (1.0 / D)
        d_e = dy * sig
        d_gl = dy * e * sig * (1.0 - sig)
        dh2 = dy + _mm_nt(d_gl, wpg_ref[...])
        dh2_ref[...] = dh2
        dwpg_ref[...] = dwpg_ref[...] + _mm_tn(h2, d_gl)
        dgple_ref[...] = dgple_ref[...] + jnp.sum(d_e * peh, axis=0, keepdims=True)
        dpeh = d_e * gple
        d_pe = rpe * (dpeh - peh * jnp.mean(dpeh * peh, axis=-1, keepdims=True))
        dwple_ref[...] = dwple_ref[...] + _mm_tn(pv, d_pe)
        dgpost_ref[...] = dgpost_ref[...] + jnp.sum(dh2 * ffh, axis=0, keepdims=True)
        dffh = dh2 * gpost
        dff_ref[...] = (rff * (dffh - ffh * jnp.mean(dffh * ffh, axis=-1, keepdims=True))).astype(BF)

    row = lambda w: pl.BlockSpec((tt, w), lambda i: (i, 0))
    chunked = pl.BlockSpec((DFF // FF_CH, tt, FF_CH), lambda i: (0, i, 0))
    shard_rows = lambda k: pl.BlockSpec((NSHARD, FF_SH, D), lambda i: (0, k, 0), pipeline_mode=pl.Buffered(1))
    return pl.pallas_call(
        body, name="ffn_fwd", grid=(t // tt,),
        in_specs=[row(D), _full((1, D)), shard_rows(0), shard_rows(1), shard_rows(2), _full((1, D)),
                  row(DPLE), _resident((DPLE, D)), _full((1, D)), _resident((D, D)), row(D)],
        out_specs=[row(D), chunked, chunked, row(D), row(D), _full((8, LANE)), _full((D, D)), _full((DPLE, D)),
                   _full((1, D)), _full((1, D))],
        out_shape=[jax.ShapeDtypeStruct((t, D), BF), jax.ShapeDtypeStruct((DFF // FF_CH, t, FF_CH), BF),
                   jax.ShapeDtypeStruct((DFF // FF_CH, t, FF_CH), BF),
                   jax.ShapeDtypeStruct((t, D), BF), jax.ShapeDtypeStruct((t, D), F32), jax.ShapeDtypeStruct((8, LANE), F32),
                   jax.ShapeDtypeStruct((D, D), F32), jax.ShapeDtypeStruct((DPLE, D), F32),
                   jax.ShapeDtypeStruct((1, D), F32), jax.ShapeDtypeStruct((1, D), F32)],
        compiler_params=_cp(),
    )(h1, g_pre, stacks2, stacks2, stacks2, g_post, p, w_ple, g_ple, w_pg, tgt)


FF_CH = 256


def _ffn_bwd(hn2, gate, up, dff, wgu, wd):
    t = hn2.shape[0]
    tt = 1024 if t % 1024 == 0 else _tile(t)
    nt = t // tt
    ch = FF_CH
    nc = DFF // ch

    def body(hn_ref, gate_ref, up_ref, dff_ref, wgu_ref, wd_ref,
             dwg_ref, dwu_ref, dwd_ref, dhn_ref, acc, gu_acc, d_acc, sem):
        j, i = pl.program_id(0), pl.program_id(1)

        @pl.when(j == 0)
        def _():
            acc[pl.ds(pl.multiple_of(i * tt, tt), tt), :] = jnp.zeros((tt, D), F32)

        @pl.when(i == 0)
        def _():
            gu_acc[...] = jnp.zeros_like(gu_acc)
            d_acc[...] = jnp.zeros_like(d_acc)

        parts = 4 if tt % 1024 == 0 else 2
        half = tt // parts
        acts, dgus = [], []
        for hh in range(parts):
            r = slice(hh * half, (hh + 1) * half)
            gate_v = gate_ref[0, r, :].astype(F32)
            up_v = up_ref[0, r, :].astype(F32)
            sg = _sigmoid(gate_v)
            silu = gate_v * sg
            d_act = _mm_nt(dff_ref[r, :], wd_ref[...])
            d_up = (d_act * silu).astype(BF)
            d_gate = (d_act * up_v * (sg * (1.0 + gate_v * (1.0 - sg)))).astype(BF)
            dgu = jnp.concatenate([d_gate, d_up], axis=1)
            rows = pl.ds(pl.multiple_of(i * tt + hh * half, half), half)
            acc[rows, :] = acc[rows, :] + jnp.dot(dgu, wgu_ref[0], preferred_element_type=F32)
            acts.append((silu * up_v).astype(BF))
            dgus.append(dgu)
        d_acc[...] = d_acc[...] + _mm_tn(jnp.concatenate(acts, axis=0), dff_ref[...])
        gu_acc[...] = gu_acc[...] + _mm_tn(jnp.concatenate(dgus, axis=0), hn_ref[...])

        @pl.when(i == nt - 1)
        def _():
            dwg_ref[...] = gu_acc[:ch].astype(BF)
            dwu_ref[...] = gu_acc[ch:].astype(BF)
            dwd_ref[...] = d_acc[...].astype(BF)

        @pl.when((j == nc - 1) & (i == nt - 1))
        def _():
            cp = pltpu.make_async_copy(acc, dhn_ref, sem)
            cp.start()
            cp.wait()

    tok = lambda w: pl.BlockSpec((tt, w), lambda j, i: (i, 0))
    chunk = pl.BlockSpec((ch, D), lambda j, i: (j, 0))
    pair = pl.BlockSpec((1, 2 * ch, D), lambda j, i: (j, 0, 0))
    return pl.pallas_call(
        body, name="ffn_bwd", grid=(nc, nt),
        in_specs=[tok(D), pl.BlockSpec((1, tt, ch), lambda j, i: (j, i, 0)), pl.BlockSpec((1, tt, ch), lambda j, i: (j, i, 0)),
                  tok(D), pair, chunk],
        out_specs=[chunk, chunk, chunk, pl.BlockSpec(memory_space=pl.ANY)],
        out_shape=[jax.ShapeDtypeStruct((DFF, D), BF), jax.ShapeDtypeStruct((DFF, D), BF),
                   jax.ShapeDtypeStruct((DFF, D), BF), jax.ShapeDtypeStruct((t, D), F32)],
        scratch_shapes=[pltpu.VMEM((t, D), F32), pltpu.VMEM((2 * ch, D), F32), pltpu.VMEM((ch, D), F32),
                        pltpu.SemaphoreType.DMA],
        compiler_params=_cp(),
    )(hn2, gate, up, dff, wgu, wd)


def _mix_bwd(d_hn2, dh2, h1, o, a, m, yb, g_ffn_pre, g_post, g_attn, g_pool, w_out, w_pool, pool_scale, dep):
    t = a.shape[0]
    tt = 256 if t % 256 == 0 else t

    def body(dhn_ref, dh2_ref, h1_ref, o_ref, a_ref, m_ref, yb_ref, gfp_ref, go_ref, ga_ref, gp_ref, wo_ref, wp_ref,
             ps_ref, dep_ref, dh1_ref, da_ref, dyc_ref, dgfp_ref, dgo_ref, dga_ref, dgp_ref, dps_ref, dwp_ref, dwo_ref):
        i = pl.program_id(0)

        @pl.when(i == 0)
        def _():
            for r in (dgfp_ref, dgo_ref, dga_ref, dgp_ref, dps_ref, dwp_ref, dwo_ref):
                r[...] = jnp.zeros_like(r)

        d1, dg = _rms_bwd(dhn_ref[...], h1_ref[...], gfp_ref[...])
        dgfp_ref[...] = dgfp_ref[...] + dg
        dh1 = dh2_ref[...] + d1
        dh1_ref[...] = dh1
        d_o, dg = _rms_bwd(dh1, o_ref[...], go_ref[...])
        dgo_ref[...] = dgo_ref[...] + dg
        d_mix = _mm_nt(d_o, wo_ref[...])
        av, mv = a_ref[...], m_ref[...]
        mix = jnp.concatenate([av * _rstd(av) * ga_ref[...], mv * _rstd(mv) * gp_ref[...]], axis=1)
        dwo_ref[...] = dwo_ref[...] + _mm_tn(mix, d_o)
        d_a, dg = _rms_bwd(d_mix[:, :DA], av, ga_ref[...])
        dga_ref[...] = dga_ref[...] + dg
        da_ref[...] = d_a
        d_m, dg = _rms_bwd(d_mix[:, DA:], mv, gp_ref[...])
        dgp_ref[...] = dgp_ref[...] + dg
        tok = i * tt + lax.broadcasted_iota(jnp.int32, (tt, PC), 0)
        dps = []
        for g, w in enumerate(WINS):
            sl = slice(PC * g, PC * (g + 1))
            ybg = yb_ref[:, sl]
            wpg = wp_ref[g].astype(BF)
            mlin = jnp.dot(ybg, wpg, preferred_element_type=F32)
            dmg = d_m[:, sl]
            dps.append(jnp.sum(dmg * mlin, axis=0, keepdims=True))
            dml = (dmg * ps_ref[:, sl]).astype(BF)
            dwp_ref[g] = dwp_ref[g] + _mm_tn(ybg, dml)
            dyc_ref[:, sl] = _mm_nt(dml, wpg) / jnp.minimum(tok + 1, w).astype(F32)
        dps_ref[...] = dps_ref[...] + jnp.concatenate(dps, axis=1)

    row = lambda w: pl.BlockSpec((tt, w), lambda i: (i, 0))
    return pl.pallas_call(
        body, name="mix_bwd", grid=(t // tt,),
        in_specs=[row(D), row(D), row(D), row(D), row(DA), row(DP), row(DP), _full((1, D)), _full((1, D)),
                  _full((1, DA)), _full((1, DP)), _resident((D, D)), _full((len(WINS), PC, PC)), _full((1, DP)), ANY],
        out_specs=[row(D), row(DA), row(DP), _full((1, D)), _full((1, D)), _full((1, DA)), _full((1, DP)),
                   _full((1, DP)), _full((len(WINS), PC, PC)), _full((D, D))],
        out_shape=[jax.ShapeDtypeStruct((t, D), F32), jax.ShapeDtypeStruct((t, DA), F32), jax.ShapeDtypeStruct((t, DP), F32),
                   jax.ShapeDtypeStruct((1, D), F32), jax.ShapeDtypeStruct((1, D), F32), jax.ShapeDtypeStruct((1, DA), F32),
                   jax.ShapeDtypeStruct((1, DP), F32), jax.ShapeDtypeStruct((1, DP), F32),
                   jax.ShapeDtypeStruct((len(WINS), PC, PC), F32), jax.ShapeDtypeStruct((D, D), F32)],
        compiler_params=_cp(),
    )(d_hn2, dh2, h1, o, a, m, yb, g_ffn_pre, g_post, g_attn, g_pool, w_out, w_pool, pool_scale, dep)


def _attn_bwd(qa, ka, v, a, d_a, lse, dep):
    t = qa.shape[0]
    ta = _tile(t)
    n = t // ta

    def body(q_ref, k_ref, v_ref, o_ref, do_ref, lse_ref, dep_ref, dq_ref, dk_ref, dv_ref):
        j = pl.program_id(1)

        @pl.when(j == 0)
        def _():
            dq_ref[...] = jnp.zeros_like(dq_ref)

        dk_ref[...] = jnp.zeros_like(dk_ref)
        dv_ref[...] = jnp.zeros_like(dv_ref)
        ks = [k_ref[:, LANE * h:LANE * (h + 1)] for h in range(2)]
        v2 = v_ref[...]
        lane = lax.broadcasted_iota(jnp.int32, (1, LANE), 1)
        mine = [lane < HD, lane >= HD]

        def tile(i, masked):
            rows = pl.ds(pl.multiple_of(i * ta, ta), ta)
            do2 = do_ref[rows, :]
            prod = do2 * o_ref[rows, :]
            lse2 = lse_ref[rows, :]
            do2b = do2.astype(BF)
            qh = [q_ref[rows, LANE * h:LANE * (h + 1)] for h in range(2)]
            s = [_mm_nt(qh[h], ks[h]) for h in range(2)]
            dp = [_mm_nt(jnp.where(mine[h], do2, 0.0), v2) for h in range(2)]
            delta = [jnp.sum(jnp.where(mine[h], prod, 0.0), axis=1, keepdims=True) for h in range(2)]
            lse_h = [jnp.sum(jnp.where(lane == HD * h, lse2, 0.0), axis=1, keepdims=True) for h in range(2)]
            pr = [jnp.exp(s[h] - lse_h[h]) for h in range(2)]
            if masked:
                keep = (lax.broadcasted_iota(jnp.int32, (ta, ta), 1) <= lax.broadcasted_iota(jnp.int32, (ta, ta), 0))
                pr = [jnp.where(keep, ph, 0.0) for ph in pr]
            ds = [(pr[h] * (dp[h] - delta[h])).astype(BF) for h in range(2)]
            dv_ref[...] = dv_ref[...] + jnp.where(mine[0], _mm_tn(pr[0], do2b), _mm_tn(pr[1], do2b))
            for h in range(2):
                sl = slice(LANE * h, LANE * (h + 1))
                dk_ref[:, sl] = dk_ref[:, sl] + _mm_tn(ds[h], qh[h])
                dq_ref[0, rows, sl] = dq_ref[0, rows, sl] + jnp.dot(ds[h], ks[h], preferred_element_type=F32)

        def step(i, carry):
            tile(i, False)
            return carry

        tile(j, True)
        lax.fori_loop(j + 1, n, step, 0)

    qrow = lambda w: pl.BlockSpec((t, w), lambda p, j: (0, p))
    krow = lambda w: pl.BlockSpec((ta, w), lambda p, j: (j, p))
    return pl.pallas_call(
        body, name="attn_bwd", grid=(NH // 2, n),
        in_specs=[qrow(2 * LANE), krow(2 * LANE), krow(LANE), qrow(LANE), qrow(LANE), qrow(LANE), ANY],
        out_specs=[pl.BlockSpec((1, t, 2 * LANE), lambda p, j: (p, 0, 0)), krow(2 * LANE), krow(LANE)],
        out_shape=[jax.ShapeDtypeStruct((NH // 2, t, 2 * LANE), F32), jax.ShapeDtypeStruct((t, NH * LANE), F32),
                   jax.ShapeDtypeStruct((t, DA), F32)],
        compiler_params=_cp(),
    )(qa, ka, v, a, d_a, lse, dep)


def _in_bwd(dqa, dka, dv, dyc, fl, x, dh1, g1, w_in_t, tri_u, dep):
    t = x.shape[0]
    tt = _tile(t)
    nt = t // tt
    hb = tt // HALO
    rev = lambda s: nt - 1 - s

    def body(dqa_ref, dka_ref, dv_ref, dyc_ref, dyn_ref, fl_ref, x_ref, dh1_ref, g_ref, w_ref, tri_ref,
             dep_ref, dx_ref, dw_ref, dg_ref, db_ref, carry, acc, stage, sem):
        s = pl.program_id(0)
        i = nt - 1 - s

        @pl.when(s == 0)
        def _():
            carry[...] = jnp.zeros_like(carry)
            acc[...] = jnp.zeros_like(acc)
            dg_ref[...] = jnp.zeros_like(dg_ref)
            db_ref[...] = jnp.zeros_like(db_ref)

        dq_cat = jnp.concatenate([dqa_ref[p] for p in range(NH // 2)], axis=1)
        dk_cat = dka_ref[...]
        off = lax.broadcasted_iota(jnp.int32, (1, NH * LANE), 1)
        off = off % LANE - HD - off // LANE

        def picked(cat, lane_off):
            kept = jnp.where(off == lane_off, cat, 0.0)
            return functools.reduce(lambda a, b: a + b, [kept[:, LANE * h:LANE * (h + 1)] for h in range(NH)])

        dc = pltpu.roll(picked(dq_cat, 0), LANE - HD, 1) - pltpu.roll(picked(dk_cat, 24), LANE - HD - 24, 1)
        dlf = carry[...] + _dot3(tri_ref[...], dc)
        carry[...] = carry[...] + jnp.sum(dc, axis=0, keepdims=True)
        flv = fl_ref[...]
        lane = lax.broadcasted_iota(jnp.int32, flv.shape, 1)
        d_fl = jnp.where(lane < NH, dlf / (1.0 + jnp.exp(flv)), 0.0)
        db_ref[...] = db_ref[...] + jnp.sum(d_fl, axis=0, keepdims=True)
        low = lax.broadcasted_iota(jnp.int32, (tt, LANE), 1) < HD
        dqs, dks = [], []
        for p in range(NH // 2):
            b0, b1 = slice(2 * LANE * p, 2 * LANE * p + LANE), slice(2 * LANE * p + LANE, 2 * LANE * (p + 1))
            dqs.append(jnp.where(low, dq_cat[:, b0], pltpu.roll(dq_cat[:, b1], HD, 1)) * (1.0 / math.sqrt(HD)))
            dks.append(jnp.where(low, dk_cat[:, b0], pltpu.roll(dk_cat[:, b1], HD, 1)))
        nxt = dyn_ref[...] * jnp.where(i < nt - 1, 1.0, 0.0)
        tok = i * tt + lax.broadcasted_iota(jnp.int32, (tt, PC), 0)
        dus = []
        for g, w in enumerate(WINS):
            sl = slice(PC * g, PC * (g + 1))
            dycg = dyc_ref[:, sl]
            ext = jnp.concatenate([dycg, nxt[:, sl]], axis=0)
            dus.append(_window_sum(ext, w, False)[:tt] - dycg * jnp.minimum(tok + 1, w).astype(F32))
        d_z = jnp.concatenate(dqs + dks + [dv_ref[...], d_fl] + dus, axis=1).astype(BF)
        xv = x_ref[...]
        gv = g_ref[...]
        hn = (xv * _rstd(xv) * gv).astype(BF)
        d_hn = jnp.dot(d_z, w_ref[...], preferred_element_type=F32)
        acc[...] = acc[...] + _mm_tn(d_z, hn)
        d1, dg = _rms_bwd(d_hn, xv, gv)
        dg_ref[...] = dg_ref[...] + dg
        dx_ref[...] = dh1_ref[...] + d1

        @pl.when(s == nt - 1)
        def _():
            stage[...] = acc[...].astype(BF)
            cp = pltpu.make_async_copy(stage, dw_ref, sem)
            cp.start()
            cp.wait()

    row = lambda w: pl.BlockSpec((tt, w), lambda s: (rev(s), 0))
    return pl.pallas_call(
        body, name="in_bwd", grid=(nt,),
        in_specs=[pl.BlockSpec((NH // 2, tt, 2 * LANE), lambda s: (0, rev(s), 0)), row(NH * LANE), row(DA), row(DP),
                  pl.BlockSpec((HALO, DP), lambda s: (jnp.minimum((rev(s) + 1) * hb, nt * hb - 1), 0)),
                  row(LANE), row(D), row(D), _full((1, D)), _resident((ZW, D)), _full((tt, tt)),
                  ANY],
        out_specs=[row(D), pl.BlockSpec(memory_space=pl.ANY), _full((1, D)), _full((1, LANE))],
        out_shape=[jax.ShapeDtypeStruct((t, D), F32), jax.ShapeDtypeStruct((ZW, D), BF),
                   jax.ShapeDtypeStruct((1, D), F32), jax.ShapeDtypeStruct((1, LANE), F32)],
        scratch_shapes=[pltpu.VMEM((1, LANE), F32), pltpu.VMEM((ZW, D), F32), pltpu.VMEM((ZW, D), BF),
                        pltpu.SemaphoreType.DMA],
        compiler_params=_cp(),
    )(dqa, dka, dv, dyc, dyc, fl, x, dh1, g1, w_in_t, tri_u, dep)


class _NoComm:
    def __init__(self, w2):
        self.w2 = w2
        self.dep = jnp.zeros((8, LANE), F32)

    def after_attention(self, after):
        return self.dep

    def weights2(self, after):
        return _stack2_full(*self.w2)

    def after_ffn(self, grads2):
        self.grads2 = grads2
        return self.dep

    def after_mix(self, after, early):
        self.early = early
        return self.dep

    def after_attn(self, after):
        return self.dep


def _local_step(x, p, tgt, sm, w1, comm):
    w_in_t, w_out = w1
    tt = _tile(x.shape[0])
    eq, ek, rowq, rowk = _aug_consts()
    b_pad = jnp.pad(sm["b_forget"], ((0, 0), (0, LANE - NH)))
    qa, ka, v, u, fl = _in_proj(x, sm["g_mix_pre"], w_in_t, b_pad, _tri(tt, False), eq, ek, rowq, rowk, comm.dep)
    a, lse = _attn_fwd(qa, ka, v)
    yb, m, o, h1 = _mix_out(a, u, x, sm["w_pool"], sm["pool_scale"], sm["g_attn_grp"],
                            sm["g_pool_grp"], w_out, sm["g_mix_post"], comm.after_attention(a))
    stacks2 = comm.weights2(h1)
    wg_t, wu_t, wd, w_ple, w_pg = _unstack2_full(stacks2)
    hn2, gate, up, dff, dh2, loss, dwpg, dwple, dgple, dgfpost = _ffn_fwd(
        h1, sm["g_ffn_pre"], stacks2, sm["g_ffn_post"], p, w_ple, sm["g_ple"], w_pg, tgt)
    chunks = lambda a: a.reshape(DFF // FF_CH, FF_CH, D)
    dwg_t, dwu_t, dwd, d_hn2 = _ffn_bwd(hn2, gate, up, dff, jnp.concatenate([chunks(wg_t), chunks(wu_t)], axis=1), wd)
    dep = comm.after_ffn((dwg_t, dwu_t, dwd, dwple, dwpg))
    dh1, d_a, dyc, dgfpre, dgpost, dgattn, dgpool, dps, dwpool, dwout = _mix_bwd(
        d_hn2, dh2, h1, o, a, m, yb, sm["g_ffn_pre"], sm["g_mix_post"], sm["g_attn_grp"], sm["g_pool_grp"],
        w_out, sm["w_pool"], sm["pool_scale"], dep)
    early = dict(loss=loss[0:1, 0:1], g_attn_grp=dgattn, g_pool_grp=dgpool, w_pool=dwpool, pool_scale=dps,
                 g_mix_post=dgpost, g_ffn_pre=dgfpre, g_ffn_post=dgfpost, g_ple=dgple)
    dqa, dka, dvv = _attn_bwd(qa, ka, v, a, d_a, lse, comm.after_mix(dh1, early))
    dx, dwin_t, dg1, dbf = _in_bwd(dqa, dka, dvv, dyc, fl, x, dh1, sm["g_mix_pre"], w_in_t, _tri(tt, True),
                                   comm.after_attn(dvv))
    return dx, (dwin_t, dwout), dict(g_mix_pre=dg1, b_forget=dbf[:, :NH])


def _place():
    x, y, c = lax.axis_index("x"), lax.axis_index("y"), lax.axis_index("c")
    return x, y, c, [(1 - x, y), (x, 1 - y), (1 - x, 1 - y)]


def _rows(c, h):
    return pl.ds(pl.multiple_of(c * h, 16), h)


def _plan_gather(h):
    def plan(src, land):
        x, y, c, chips = _place()
        return [(src.at[_rows(c, h), :], land.at[2 * x + y, _rows(c, h), :], (cx, cy, c),
                 land.at[2 * cx + cy, _rows(c, h), :]) for cx, cy in chips]
    return plan


def _plan_forward(h):
    def plan(land_in, own, land):
        x, y, c, chips = _place()
        sib, me = (x, y, 1 - c), 2 * x + y
        return ([(land_in.at[2 * cx + cy, _rows(c, h), :], land.at[2 * cx + cy, _rows(c, h), :], sib,
                  land.at[2 * cx + cy, _rows(1 - c, h), :]) for cx, cy in chips]
                + [(own, land.at[me], sib, land.at[me])])
    return plan


def _plan_swap_halves(h):
    def plan(buf_in, buf):
        x, y, c, _ = _place()
        return [(buf_in.at[_rows(c, h), :], buf.at[_rows(c, h), :], (x, y, 1 - c), buf.at[_rows(1 - c, h), :])]
    return plan


def _plan_pair_rows(h):
    def plan(src, land):
        x, y, c, _ = _place()
        return [(src.at[:, _rows(1 - c, h), :], land, (x, y, 1 - c), land)]
    return plan


def _plan_scatter(src, land):
    x, y, c, chips = _place()
    return [(src.at[2 * cx + cy], land.at[k], (cx, cy, c), land.at[k]) for k, (cx, cy) in enumerate(chips)]


def _plan_scatter8(h):
    def plan(src, land):
        x, y, c, chips = _place()
        copies = [(src.at[2 * x + y, _rows(1 - c, h), :], land.at[0], (x, y, 1 - c), land.at[0])]
        for k, (cx, cy) in enumerate(chips):
            for d, other in enumerate((c, 1 - c)):
                copies.append((src.at[2 * cx + cy, _rows(other, h), :], land.at[1 + 2 * k + c], (cx, cy, other),
                               land.at[1 + 2 * k + other]))
        return copies
    return plan


def _plan_all(src, land):
    x, y, c, _ = _place()
    copies = []
    for r in range(1, 8):
        px, py, pc = (1 - a if b else a for a, b in zip((x, y, c), (r >> 2 & 1, r >> 1 & 1, r & 1)))
        copies.append((src, land.at[4 * x + 2 * y + c], (px, py, pc), land.at[4 * px + 2 * py + pc]))
    return copies


def _remote(src, dst, send_sems, recv_sems, k, peer):
    return pltpu.make_async_remote_copy(src_ref=src, dst_ref=dst, send_sem=send_sems.at[k], recv_sem=recv_sems.at[k],
                                        device_id=peer, device_id_type=MESH)


def _exchange(name, n, src, land, plan):
    def body(src_ref, land_ref, send_sems, recv_sems):
        copies = plan(src_ref, land_ref)
        for k, (s, d, peer, _) in enumerate(copies):
            _remote(s, d, send_sems, recv_sems, k, peer).start()
        for k, (s, _, peer, mine) in enumerate(copies):
            _remote(s, mine, send_sems, recv_sems, k, peer).wait_recv()
        for k, (s, d, peer, _) in enumerate(copies):
            _remote(s, d, send_sems, recv_sems, k, peer).wait_send()

    return pl.pallas_call(
        body, name=name, in_specs=[ANY], out_specs=ANY, out_shape=land,
        scratch_shapes=[pltpu.SemaphoreType.DMA((n,)), pltpu.SemaphoreType.DMA((n,))],
    )(src)


def _exchange_inplace(name, n, buf, extra, plan):
    def body(*refs):
        ins, buf_ref, send_sems, recv_sems = refs[:1 + len(extra)], refs[1 + len(extra)], refs[-2], refs[-1]
        copies = plan(*ins, buf_ref)
        for k, (s, d, peer, _) in enumerate(copies):
            _remote(s, d, send_sems, recv_sems, k, peer).start()
        for k, (s, _, peer, mine) in enumerate(copies):
            _remote(s, mine, send_sems, recv_sems, k, peer).wait_recv()
        for k, (s, d, peer, _) in enumerate(copies):
            _remote(s, d, send_sems, recv_sems, k, peer).wait_send()

    return pl.pallas_call(
        body, name=name, in_specs=[ANY] * (1 + len(extra)), out_specs=ANY, out_shape=_sds(buf.shape, buf.dtype),
        input_output_aliases={0: 0},
        scratch_shapes=[pltpu.SemaphoreType.DMA((n,)), pltpu.SemaphoreType.DMA((n,))],
    )(buf, *extra)


HBM = pl.BlockSpec(memory_space=pltpu.HBM)
SEM = pl.BlockSpec(memory_space=pltpu.SEMAPHORE)
EFFECT = pltpu.SideEffectType.DATAFLOW_SIDE_EFFECTING


def _exchange_start(name, n, src, land, plan):
    def body(src_ref, land_ref, send_sems, recv_sems, src_thru, land_thru, token):
        for k, (s, d, peer, _) in enumerate(plan(src_ref, land_ref)):
            _remote(s, d, send_sems, recv_sems, k, peer).start()
        token[...] = jnp.zeros_like(token)

    return pl.pallas_call(
        body, name=name,
        out_shape=(pltpu.SemaphoreType.DMA((n,)), pltpu.SemaphoreType.DMA((n,)), pltpu.HBM(src.shape, src.dtype),
                   pltpu.HBM(land.shape, land.dtype), jax.ShapeDtypeStruct((8, LANE), F32)),
        in_specs=(HBM, HBM), out_specs=(SEM, SEM, HBM, HBM, pl.BlockSpec(memory_space=pltpu.VMEM)),
        input_output_aliases={0: 2, 1: 3},
        compiler_params=pltpu.CompilerParams(has_side_effects=EFFECT),
    )(pltpu.with_memory_space_constraint(src, pltpu.HBM), pltpu.with_memory_space_constraint(land, pltpu.HBM))


def _exchange_wait(name, started, plan, after):
    send_sems, recv_sems, src, land, _ = started

    def body(src_ref, land_ref, send_sems, recv_sems, after_ref, src_out, land_out):
        for k, (s, _, peer, mine) in enumerate(plan(src_ref, land_ref)):
            cp = _remote(s, mine, send_sems, recv_sems, k, peer)
            cp.wait_send()
            cp.wait_recv()

    return pl.pallas_call(
        body, name=name, out_shape=(pltpu.HBM(src.shape, src.dtype), pltpu.HBM(land.shape, land.dtype)),
        in_specs=(HBM, HBM, SEM, SEM, ANY), out_specs=(HBM, HBM), input_output_aliases={0: 0, 1: 1},
        compiler_params=pltpu.CompilerParams(has_side_effects=EFFECT),
    )(src, land, send_sems, recv_sems, after)


def _pair_sum(name, cidx, g, recv, br):
    h = recv.shape[1]
    nb = h // br

    def body(c_ref, g_ref, r_ref, out_ref):
        out_ref[...] = (g_ref[...].astype(F32) + r_ref[...].astype(F32)).astype(BF)

    return pl.pallas_call(
        body, name=name,
        grid_spec=pltpu.PrefetchScalarGridSpec(
            num_scalar_prefetch=1, grid=(NSHARD, nb),
            in_specs=[pl.BlockSpec((1, br, D), lambda s, i, c: (s, c[0] * nb + i, 0)),
                      pl.BlockSpec((1, br, D), lambda s, i, c: (s, i, 0))],
            out_specs=pl.BlockSpec((1, br, D), lambda s, i, c: (s, i, 0))),
        out_shape=jax.ShapeDtypeStruct((NSHARD, h, D), BF),
    )(cidx, g, recv)


def _chip_sum(name, place, pb, y, br):
    h = y.shape[1]
    nb = h // br

    def body(pl_ref, p_ref, y_ref, out_ref):
        acc = p_ref[0].astype(F32)
        for k in range(NSHARD - 1):
            acc = acc + y_ref[k].astype(F32)
        out_ref[...] = acc

    return pl.pallas_call(
        body, name=name,
        grid_spec=pltpu.PrefetchScalarGridSpec(
            num_scalar_prefetch=1, grid=(nb,),
            in_specs=[pl.BlockSpec((1, br, D), lambda i, s: (s[0], i, 0)),
                      pl.BlockSpec((NSHARD - 1, br, D), lambda i, s: (0, i, 0))],
            out_specs=pl.BlockSpec((br, D), lambda i, s: (s[1] * nb + i, 0))),
        out_shape=jax.ShapeDtypeStruct((2 * h, D), F32),
    )(place, pb, y)


def _sum8(name, place, g, land, br):
    h = land.shape[1]
    nb = h // br

    def body(pl_ref, g_ref, y_ref, out_ref):
        acc = g_ref[0].astype(F32)
        for k in range(land.shape[0]):
            acc = acc + y_ref[k].astype(F32)
        out_ref[...] = acc

    return pl.pallas_call(
        body, name=name,
        grid_spec=pltpu.PrefetchScalarGridSpec(
            num_scalar_prefetch=1, grid=(nb,),
            in_specs=[pl.BlockSpec((1, br, D), lambda i, s: (s[0], s[1] * nb + i, 0)),
                      pl.BlockSpec((land.shape[0], br, D), lambda i, s: (0, i, 0))],
            out_specs=pl.BlockSpec((br, D), lambda i, s: (s[1] * nb + i, 0))),
        out_shape=jax.ShapeDtypeStruct((2 * h, D), F32), compiler_params=_cp(),
    )(place, g, land)


def _sum_slots(name, v):
    def body(in_ref, out_ref):
        acc = in_ref[0]
        for k in range(1, 8):
            acc = acc + in_ref[k]
        out_ref[...] = acc

    vm = pl.BlockSpec(memory_space=pltpu.VMEM)
    return pl.pallas_call(body, name=name, in_specs=[vm], out_specs=vm,
                          out_shape=jax.ShapeDtypeStruct(v.shape[1:], F32))(v)


def _adamw_math(w, g, m, v):
    m = ADAM_B1 * m + (1.0 - ADAM_B1) * g
    v = ADAM_B2 * v + (1.0 - ADAM_B2) * (g * g)
    m_hat = m / (1.0 - ADAM_B1 ** ADAM_STEP)
    v_hat = v / (1.0 - ADAM_B2 ** ADAM_STEP)
    delta = -ADAM_LR * (m_hat / (jnp.sqrt(v_hat) + ADAM_EPS) + ADAM_WD * w)
    return delta, m, v


def _adamw(w, g, m, v, dep, row0=None):
    r, c = w.shape
    br = next(b for b in (256, 176, 128, r) if r % b == 0 and (row0 or 0) % b == 0)
    first = (row0 or 0) // br

    def body(w_ref, g_ref, m_ref, v_ref, dep_ref, *outs):
        gv = g_ref[...]
        outs[-3][...], outs[-2][...], outs[-1][...] = _adamw_math(w_ref[...], gv, m_ref[...], v_ref[...])
        if row0 is not None:
            outs[0][...] = gv

    spec = pl.BlockSpec((br, c), lambda i: (i, 0))
    n_out = 3 if row0 is None else 4
    out = pl.pallas_call(
        body, name="adamw", grid=(r // br,),
        in_specs=[spec, pl.BlockSpec((br, c), lambda i: (first + i, 0)), spec, spec, ANY], out_specs=[spec] * n_out,
        out_shape=[jax.ShapeDtypeStruct((r, c), F32)] * n_out, compiler_params=_cp(),
    )(w, g, m, v, dep)
    return out if row0 is not None else [g] + list(out)


def _adamw_small(ws, gs, ms, vs):
    n = len(ws)

    def body(*refs):
        ins, outs = refs[:4 * n], refs[4 * n:]
        for k in range(n):
            d, m, v = _adamw_math(ins[k][...], ins[n + k][...], ins[2 * n + k][...], ins[3 * n + k][...])
            outs[k][...] = d
            outs[n + k][...] = m
            outs[2 * n + k][...] = v

    vm = pl.BlockSpec(memory_space=pltpu.VMEM)
    out = pl.pallas_call(
        body, name="adamw_small", in_specs=[vm] * (4 * n), out_specs=[vm] * (3 * n),
        out_shape=[jax.ShapeDtypeStruct(w.shape, F32) for w in ws] * 3,
    )(*ws, *gs, *ms, *vs)
    return out[:n], out[n:2 * n], out[2 * n:]


BIG = ("w_in", "w_out", "w_ffn_gate", "w_ffn_up", "w_ffn_down", "w_ple_proj", "w_ple_gate")
SMALL = ("g_mix_pre", "b_forget", "g_attn_grp", "g_pool_grp", "w_pool", "pool_scale", "g_mix_post", "g_ffn_pre",
         "g_ffn_post", "g_ple")
TRANSPOSED = ("w_in", "w_ffn_gate", "w_ffn_up")
VECTORS = tuple(n for n in SMALL if n != "w_pool")
ORDER = ("g_mix_pre", "w_in", "b_forget", "g_attn_grp", "g_pool_grp", "w_pool", "pool_scale", "w_out", "g_mix_post",
         "g_ffn_pre", "w_ffn_gate", "w_ffn_up", "w_ffn_down", "g_ffn_post", "w_ple_proj", "g_ple", "w_ple_gate")


def _pad_rows(a, rows):
    return jnp.pad(a, ((0, rows - a.shape[0]), (0, 0)))


def _stack1(w_in, w_out):
    return _pad_rows(jnp.concatenate([_pad_rows(w_in.T, IN_PAD), w_out], axis=0), ROWS1)


def _stack2(wg, wu, wd, wple, wpg):
    return _pad_rows(jnp.concatenate([wg.T, wu.T, wd, wple.reshape(DPLE // NSHARD, D), wpg], axis=0), ROWS2)


def _unstack1(s):
    return s[:IN_SH], s[O1_OUT:USED1]


def _cat(g, lo, hi):
    return g[:, lo:hi].reshape(NSHARD * (hi - lo), D)


def _unstack1_full(g):
    w_in_t = _cat(g, 0, IN_SH)
    w_in_t = jnp.concatenate([w_in_t[:3 * DA], _pad_rows(w_in_t[3 * DA:3 * DA + NH], LANE), w_in_t[3 * DA + NH:]], axis=0)
    return w_in_t, _cat(g, O1_OUT, USED1)


def _unstack2_full(g):
    w_ple = g[:, O2_PLE:O2_PG].reshape(NSHARD, DPLE, DPLE).transpose(1, 0, 2).reshape(DPLE, D)
    return _cat(g, 0, O2_U), _cat(g, O2_U, O2_D), _cat(g, O2_D, O2_PLE), w_ple, _cat(g, O2_PG, USED2)


def _shards(a):
    return a.reshape(NSHARD, a.shape[0] // NSHARD, D)


def _stack1_full(dwin_t, dwout):
    dwin_t = jnp.concatenate([dwin_t[:3 * DA + NH], dwin_t[3 * DA + LANE:]], axis=0).reshape(NSHARD, IN_SH, D)
    zeros = lambda r: jnp.zeros((NSHARD, r, D), dwin_t.dtype)
    return jnp.concatenate([dwin_t, zeros(IN_PAD - IN_SH), _shards(dwout), zeros(ROWS1 - USED1)], axis=1)


def _stack2_full(dwg_t, dwu_t, dwd, dwple, dwpg):
    dwple = dwple.reshape(DPLE, NSHARD, DPLE).transpose(1, 0, 2).reshape(NSHARD, DPLE // NSHARD, D)
    return jnp.concatenate([_shards(dwg_t), _shards(dwu_t), _shards(dwd), dwple, _shards(dwpg),
                            jnp.zeros((NSHARD, ROWS2 - USED2, D), dwd.dtype)], axis=1)


def _sds(shape, dtype):
    return jax.ShapeDtypeStruct(shape, dtype)


class _Comm:
    def __init__(self, stack2, me, c):
        self.stack2, self.me, self.c = stack2, me, c
        self.cidx = c.astype(jnp.int32).reshape(1)
        self.place = jnp.stack([me, c]).astype(jnp.int32)
        self.h = ROWS2 // 2
        self.gather = _exchange_start("gather2_start", 3, stack2, lax.empty((NSHARD, ROWS2, D), BF), _plan_gather(self.h))
        self.dep = self.gather[4]

    def after_attention(self, after):
        own, land = _exchange_wait("gather2_wait", self.gather, _plan_gather(self.h), after)
        fwd = _plan_forward(self.h)
        self.forward = lambda own_ref, land_ref: fwd(land_ref, own_ref, land_ref)
        self.passing = _exchange_start("forward2_start", 4, own, land, self.forward)
        return self.passing[4]

    def weights2(self, after):
        return _exchange_wait("forward2_wait", self.passing, self.forward, after)[1]

    def after_ffn(self, grads2):
        g = _stack2_full(*[a.astype(BF) for a in grads2])
        self.scatter = _plan_scatter8(self.h)
        self.chip = _exchange_start("reduce2_start", 7, g, lax.empty((7, self.h, D), BF), self.scatter)
        return self.chip[4]

    def after_mix(self, after, early):
        self.early_shapes = {n: early[n].shape for n in early}
        self.small = self.start_small("small", early)
        return self.small[4]

    def after_attn(self, after):
        g, y = _exchange_wait("reduce2_wait", self.chip, self.scatter, after)
        f = _sum8("sum2", self.place, g, y, RED2 // 2)
        self.early = self.finish_small("small", self.small, self.early_shapes, after)
        swap = _plan_swap_halves(self.h)
        self.swap = lambda _, buf: swap(buf, buf)
        self.swapping = _exchange_start("reduce2_gather_start", 1, self.dep, f, self.swap)
        return self.swapping[4]

    def reduced2(self, after):
        return _exchange_wait("reduce2_gather_wait", self.swapping, self.swap, after)[1]

    def start_small(self, name, small):
        v = _pack_small(small)
        return _exchange_start(name + "_start", 7, v, lax.empty((8,) + v.shape, F32), _plan_all)

    def finish_small(self, name, started, shapes, after):
        v, land = _exchange_wait(name + "_wait", started, _plan_all, after)
        land = lax.dynamic_update_slice(land, v[None], (2 * self.me + self.c, 0, 0))
        return _unpack_small(_sum_slots(name + "_sum", land), shapes)


def _pack_small(small):
    parts = []
    for name in small:
        flat = small[name].reshape(-1)
        parts.append(jnp.pad(flat, (0, -flat.shape[0] % LANE)).reshape(-1, LANE))
    v = jnp.concatenate(parts, axis=0)
    return _pad_rows(v, v.shape[0] + (-v.shape[0] % 8))


def _unpack_small(v, shapes):
    out, r = {}, 0
    for name in shapes:
        n = math.prod(shapes[name])
        rows = -(-n // LANE)
        out[name] = v[r:r + rows].reshape(-1)[:n].reshape(shapes[name])
        r += rows
    return out


def kernel(x, p, g_mix_pre, w_in, b_forget, g_attn_grp, g_pool_grp, w_pool, pool_scale, w_out, g_mix_post, g_ffn_pre, w_ffn_gate, w_ffn_up, w_ffn_down, g_ffn_post, w_ple_proj, g_ple, w_ple_gate, loss_target, m_g_mix_pre, m_w_in, m_b_forget, m_g_attn_grp, m_g_pool_grp, m_w_pool, m_pool_scale, m_w_out, m_g_mix_post, m_g_ffn_pre, m_w_ffn_gate, m_w_ffn_up, m_w_ffn_down, m_g_ffn_post, m_w_ple_proj, m_g_ple, m_w_ple_gate, v_g_mix_pre, v_w_in, v_b_forget, v_g_attn_grp, v_g_pool_grp, v_w_pool, v_pool_scale, v_w_out, v_g_mix_post, v_g_ffn_pre, v_w_ffn_gate, v_w_ffn_up, v_w_ffn_down, v_g_ffn_post, v_w_ple_proj, v_g_ple, v_w_ple_gate):
    args = dict(locals())
    strip = lambda n, a: a if n in VECTORS else a[0]
    w = {n: strip(n, args[n]) for n in ORDER}
    mom = {n: strip(n, args["m_" + n]) for n in ORDER}
    var = {n: strip(n, args["v_" + n]) for n in ORDER}
    sm = {n: w[n] for n in SMALL}

    c = lax.axis_index("c")
    me = 2 * lax.axis_index("x") + lax.axis_index("y")
    h1 = ROWS1 // 2
    bf = lambda n: w[n].astype(BF)
    stack1 = _stack1(bf("w_in"), bf("w_out"))
    stack2 = _stack2(*[bf(n) for n in BIG[2:]])
    land = _exchange("gather1", 3, stack1, _sds((NSHARD, ROWS1, D), BF), _plan_gather(h1))
    land, stack2 = lax.optimization_barrier((land, stack2))
    comm = _Comm(stack2, me, c)
    w1 = _unstack1_full(_exchange_inplace("gather1_forward", 4, land, (stack1,), _plan_forward(h1)))
    dx, grads1, late = _local_step(x[0], p[0, 0], loss_target[0], sm, w1, comm)

    flip = lambda n, a: a.T if n in TRANSPOSED else a
    grads, delta, new_m, new_v = {}, {}, {}, {}

    def update(n, g, dep, row0=None):
        g_, d_, m_, v_ = _adamw(flip(n, w[n]), g, flip(n, mom[n]), flip(n, var[n]), dep, row0)
        grads[n], delta[n], new_m[n], new_v[n] = flip(n, g_), flip(n, d_), flip(n, m_), flip(n, v_)
        return v_

    late_shapes = {n: late[n].shape for n in late}
    small2 = comm.start_small("small2", late)
    red2 = comm.reduced2(dx)
    g1 = _stack1_full(grads1[0], grads1[1].astype(BF))
    pair1 = _exchange_start("reduce1_pair_start", 1, g1, lax.empty((NSHARD, h1, D), BF), _plan_pair_rows(h1))
    dep = update("w_ple_gate", red2, pair1[4] + small2[4], O2_PG)
    dep = update("w_ple_proj", red2[O2_PLE:O2_PG].reshape(DPLE, DPLE), dep)
    g1, recv = _exchange_wait("reduce1_pair_wait", pair1, _plan_pair_rows(h1), dep)
    pb = _pair_sum("pair_sum1", comm.cidx, g1, recv, RED1)
    chip1 = _exchange_start("reduce1_chip_start", 3, pb, lax.empty((NSHARD - 1, h1, D), BF), _plan_scatter)
    dep = update("w_ffn_gate", red2, chip1[4], 0)
    dep = update("w_ffn_up", red2, dep, O2_U)
    dep = update("w_ffn_down", red2, dep, O2_D)
    red_small = {**comm.early, **comm.finish_small("small2", small2, late_shapes, dep)}
    loss = 0.5 / D * red_small["loss"][0, 0]
    for n in SMALL:
        grads[n] = red_small[n].reshape(w[n].shape)
    two_d = lambda a: a.reshape(-1, a.shape[-1])
    ds, ms, vs = _adamw_small([two_d(w[n]) for n in SMALL], [two_d(grads[n]) for n in SMALL],
                              [two_d(mom[n]) for n in SMALL], [two_d(var[n]) for n in SMALL])
    for k, n in enumerate(SMALL):
        delta[n], new_m[n], new_v[n] = ds[k].reshape(w[n].shape), ms[k].reshape(w[n].shape), vs[k].reshape(w[n].shape)
    pb, y = _exchange_wait("reduce1_chip_wait", chip1, _plan_scatter, vs[0])
    f = _chip_sum("chip_sum1", comm.place, pb, y, RED1)
    reduced1 = _exchange_inplace("reduce1_gather", 1, f, (), _plan_swap_halves(h1))
    g_in, g_out = _unstack1(reduced1)
    update("w_out", g_out, update("w_in", g_in, reduced1))

    lead = lambda d: [d[n] if n in VECTORS else d[n][None] for n in ORDER]
    return (loss, dx[None], *lead(grads), *lead(delta), *lead(new_m), *lead(new_v))
```

```python
import functools
import math

import jax
import jax.numpy as jnp
import numpy as np
from jax import lax
from jax.experimental import pallas as pl
from jax.experimental.pallas import tpu as pltpu

F32 = jnp.float32
BF = jnp.bfloat16
MESH = pl.DeviceIdType.MESH

D = 1024
DA = 512
DP = 512
NH = 8
HD = 64
DFF = 2816
DPLE = 256
WINS = (2, 4, 8, 16)
PC = 128
ZW = 3 * DA + 128 + DP
EPS = 1e-6
NSHARD = 4

LANE = 128
HALO = 128

IN_SH = 514
IN_PAD = 528
FF_SH = DFF // NSHARD
O1_OUT, USED1, ROWS1 = 528, 784, 800
O2_U, O2_D, O2_PLE, O2_PG, USED2, ROWS2 = 704, 1408, 2112, 2176, 2432, 2560
RED1, RED2 = 400, 640

ADAM_LR, ADAM_B1, ADAM_B2, ADAM_EPS, ADAM_WD, ADAM_STEP = 0.001, 0.9, 0.999, 1e-8, 0.01, 10

VMEM_LIMIT = 56 * 1024 * 1024


def _cp(**kw):
    return pltpu.CompilerParams(vmem_limit_bytes=VMEM_LIMIT, **kw)


def _mm(a, b):
    return jnp.dot(a.astype(BF), b.astype(BF), preferred_element_type=F32)


def _mm_nt(a, b):
    return lax.dot_general(a.astype(BF), b.astype(BF), (((1,), (1,)), ((), ())), preferred_element_type=F32)


def _mm_tn(a, b):
    return lax.dot_general(a.astype(BF), b.astype(BF), (((0,), (0,)), ((), ())), preferred_element_type=F32)


def _split3(x):
    hi = x.astype(BF)
    r = x - hi.astype(F32)
    mid = r.astype(BF)
    lo = (r - mid.astype(F32)).astype(BF)
    return hi, mid, lo


def _dot3(m, x):
    hi, mid, lo = _split3(x)
    return (jnp.dot(m, hi, preferred_element_type=F32) + jnp.dot(m, mid, preferred_element_type=F32)
            + jnp.dot(m, lo, preferred_element_type=F32))


def _window_sum(ext, w, back):
    n = ext.shape[0]
    s, k = ext, 1
    while k < w:
        s = s + pltpu.roll(s, k if back else n - k, 0)
        k *= 2
    return s


def _rstd(x):
    return lax.rsqrt(jnp.mean(x * x, axis=-1, keepdims=True) + EPS)


def _rms_bwd(dy, x, g):
    r = _rstd(x)
    xh = x * r
    dg = jnp.sum(dy * xh, axis=0, keepdims=True)
    dxh = dy * g
    dx = r * (dxh - xh * jnp.mean(dxh * xh, axis=-1, keepdims=True))
    return dx, dg


def _sigmoid(x):
    return 1.0 / (1.0 + jnp.exp(-x))


ANY = pl.BlockSpec(memory_space=pl.ANY)


def _full(shape):
    n = len(shape)
    return pl.BlockSpec(shape, lambda *_: (0,) * n)


def _resident(shape):
    n = len(shape)
    return pl.BlockSpec(shape, lambda *_: (0,) * n, pipeline_mode=pl.Buffered(1))


def _tile(t):
    return 512 if t % 512 == 0 else t


def _tri(n, upper):
    r, c = np.indices((n, n))
    return ((c >= r) if upper else (c <= r)).astype(BF)


def _aug_consts():
    row, col = np.indices((3 * LANE, NH * LANE))
    piece, head = row // LANE, row % LANE
    ch, cl = col // LANE, col % LANE
    eq = ((head == ch) & (cl == HD + 8 * piece + head)).astype(BF)
    ek = -((head == ch) & (cl == HD + 24 + 8 * piece + head)).astype(BF)
    off = (np.arange(NH * LANE) % LANE - HD - np.arange(NH * LANE) // LANE)[None, :]
    rowq = ((off >= 24) & (off < 48) & (off % 8 == 0)).astype(np.float32)
    rowk = ((off >= 0) & (off < 24) & (off % 8 == 0)).astype(np.float32)
    return eq, ek, rowq, rowk


def _in_proj(x, g1, w_in_t, b_pad, tri, eq, ek, rowq, rowk, dep):
    t = x.shape[0]
    tt = _tile(t)

    def body(x_ref, g_ref, w_ref, b_ref, tri_ref, eq_ref, ek_ref, rq_ref, rk_ref, dep_ref,
             qa_ref, ka_ref, v_ref, u_ref, fl_ref, carry):
        i = pl.program_id(0)

        @pl.when(i == 0)
        def _():
            carry[...] = jnp.zeros_like(carry)

        xv = x_ref[...]
        hn = (xv * _rstd(xv) * g_ref[...]).astype(BF)
        z = _mm_nt(hn, w_ref[...])
        fl = z[:, 3 * DA:3 * DA + LANE] + b_ref[...]
        lane = lax.broadcasted_iota(jnp.int32, fl.shape, 1)
        lf = jnp.where(lane < NH, jnp.minimum(fl, 0.0) - jnp.log(1.0 + jnp.exp(-jnp.abs(fl))), 0.0)
        c = carry[...] + _dot3(tri_ref[...], lf)
        carry[...] = carry[...] + jnp.sum(lf, axis=0, keepdims=True)
        caug = jnp.concatenate(_split3(c), axis=1)
        aug_q = jnp.dot(caug, eq_ref[...], preferred_element_type=F32) + rq_ref[...]
        aug_k = jnp.dot(caug, ek_ref[...], preferred_element_type=F32) + rk_ref[...]
        low = lax.broadcasted_iota(jnp.int32, (tt, LANE), 1) < HD
        for p in range(NH // 2):
            qp = z[:, LANE * p:LANE * (p + 1)] * (1.0 / math.sqrt(HD))
            kp = z[:, DA + LANE * p:DA + LANE * (p + 1)]
            for h, (qh, kh) in enumerate(((qp, kp), (pltpu.roll(qp, HD, 1), pltpu.roll(kp, HD, 1)))):
                lo_, hi_ = LANE * (2 * p + h), LANE * (2 * p + h + 1)
                qa_ref[:, lo_:hi_] = jnp.where(low, qh, aug_q[:, lo_:hi_]).astype(BF)
                ka_ref[:, lo_:hi_] = jnp.where(low, kh, aug_k[:, lo_:hi_]).astype(BF)
        v_ref[...] = z[:, 2 * DA:3 * DA].astype(BF)
        u_ref[...] = z[:, 3 * DA + LANE:]
        fl_ref[...] = fl

    return pl.pallas_call(
        body, name="in_proj", grid=(t // tt,),
        in_specs=[pl.BlockSpec((tt, D), lambda i: (i, 0)), _full((1, D)), _resident((ZW, D)), _full((1, LANE)),
                  _full((tt, tt)), _full((3 * LANE, NH * LANE)), _full((3 * LANE, NH * LANE)),
                  _full((1, NH * LANE)), _full((1, NH * LANE)), ANY],
        out_specs=[pl.BlockSpec((tt, NH * LANE), lambda i: (i, 0)), pl.BlockSpec((tt, NH * LANE), lambda i: (i, 0)),
                   pl.BlockSpec((tt, DA), lambda i: (i, 0)), pl.BlockSpec((tt, DP), lambda i: (i, 0)),
                   pl.BlockSpec((tt, LANE), lambda i: (i, 0))],
        out_shape=[jax.ShapeDtypeStruct((t, NH * LANE), BF), jax.ShapeDtypeStruct((t, NH * LANE), BF),
                   jax.ShapeDtypeStruct((t, DA), BF), jax.ShapeDtypeStruct((t, DP), F32),
                   jax.ShapeDtypeStruct((t, LANE), F32)],
        scratch_shapes=[pltpu.VMEM((1, LANE), F32)],
        compiler_params=_cp(),
    )(x, g1, w_in_t, b_pad, tri, eq, ek, rowq, rowk, dep)


def _attn_fwd(qa, ka, v):
    t = qa.shape[0]
    ta = _tile(t)
    n = t // ta

    def body(q_ref, k_ref, v_ref, a_ref, lse_ref, m_ref, l_ref, acc_ref):
        i = pl.program_id(1)
        m_ref[...] = jnp.full_like(m_ref, -1e30)
        l_ref[...] = jnp.zeros_like(l_ref)
        acc_ref[...] = jnp.zeros_like(acc_ref)
        qs = [q_ref[:, LANE * h:LANE * (h + 1)] for h in range(2)]
        reps = ta // LANE

        def tile(j, width, masked):
            rows = pl.ds(pl.multiple_of(j * ta, ta), width * ta)
            v2 = v_ref[rows, :]
            s = [_mm_nt(qs[h], k_ref[rows, LANE * h:LANE * (h + 1)]) for h in range(2)]
            if masked:
                keep = (lax.broadcasted_iota(jnp.int32, (ta, ta), 1) <= lax.broadcasted_iota(jnp.int32, (ta, ta), 0))
                s = [jnp.where(keep, sh, -1e30) for sh in s]
            m_old = [m_ref[h] for h in range(2)]
            m_new = [jnp.maximum(m_old[h], jnp.max(s[h], axis=1, keepdims=True)) for h in range(2)]
            pe = [jnp.exp(s[h] - jnp.tile(m_new[h], (1, width * reps))) for h in range(2)]
            alpha = [jnp.exp(m_old[h] - m_new[h]) for h in range(2)]
            pv = [jnp.dot(pe[h].astype(BF), v2, preferred_element_type=F32) for h in range(2)]
            for h in range(2):
                l_ref[h] = alpha[h] * l_ref[h] + jnp.sum(pe[h], axis=1, keepdims=True)
                acc_ref[h] = alpha[h] * acc_ref[h] + pv[h]
                m_ref[h] = m_new[h]

        def step(jj, carry):
            tile(2 * jj, 2, False)
            return carry

        lax.fori_loop(0, i // 2, step, 0)

        @pl.when(i % 2 == 1)
        def _():
            tile(i - 1, 1, False)

        tile(i, 1, True)
        low = lax.broadcasted_iota(jnp.int32, (ta, LANE), 1) < HD
        a_ref[...] = jnp.where(low, acc_ref[0] / l_ref[0], acc_ref[1] / l_ref[1])
        lse_ref[...] = jnp.where(low, m_ref[0] + jnp.log(l_ref[0]), m_ref[1] + jnp.log(l_ref[1]))

    return pl.pallas_call(
        body, name="attn_fwd", grid=(NH // 2, n),
        in_specs=[pl.BlockSpec((ta, 2 * LANE), lambda p, i: (i, p)),
                  pl.BlockSpec((t, 2 * LANE), lambda p, i: (0, p)),
                  pl.BlockSpec((t, LANE), lambda p, i: (0, p))],
        out_specs=[pl.BlockSpec((ta, LANE), lambda p, i: (i, p)), pl.BlockSpec((ta, LANE), lambda p, i: (i, p))],
        out_shape=[jax.ShapeDtypeStruct((t, DA), F32), jax.ShapeDtypeStruct((t, DA), F32)],
        scratch_shapes=[pltpu.VMEM((2, ta, LANE), F32), pltpu.VMEM((2, ta, LANE), F32), pltpu.VMEM((2, ta, LANE), F32)],
        compiler_params=_cp(),
    )(qa, ka, v)


def _mix_out(a, u, x, w_pool, pool_scale, g_attn, g_pool, w_out, g_post, dep):
    t = a.shape[0]
    tt = _tile(t)
    hb = tt // HALO

    def body(a_ref, u_ref, up_ref, x_ref, wp_ref, ps_ref, ga_ref, gp_ref, wo_ref, go_ref, dep_ref,
             yb_ref, m_ref, o_ref, h1_ref):
        i = pl.program_id(0)
        prev = up_ref[...] * jnp.where(i > 0, 1.0, 0.0)
        tok = i * tt + lax.broadcasted_iota(jnp.int32, (tt, PC), 0)
        ms = []
        for g, w in enumerate(WINS):
            ug = u_ref[:, PC * g:PC * (g + 1)]
            ext = jnp.concatenate([prev[:, PC * g:PC * (g + 1)], ug], axis=0)
            cnt = jnp.minimum(tok + 1, w).astype(F32)
            y = (_window_sum(ext, w, True)[HALO:] / cnt - ug).astype(BF)
            yb_ref[:, PC * g:PC * (g + 1)] = y
            ms.append(_mm(y, wp_ref[g]) * ps_ref[:, PC * g:PC * (g + 1)])
        m = jnp.concatenate(ms, axis=1)
        m_ref[...] = m
        av = a_ref[...]
        mix = jnp.concatenate([av * _rstd(av) * ga_ref[...], m * _rstd(m) * gp_ref[...]], axis=1)
        o = _mm(mix, wo_ref[...])
        o_ref[...] = o
        h1_ref[...] = x_ref[...] + o * _rstd(o) * go_ref[...]

    return pl.pallas_call(
        body, name="mix_out", grid=(t // tt,),
        in_specs=[pl.BlockSpec((tt, DA), lambda i: (i, 0)), pl.BlockSpec((tt, DP), lambda i: (i, 0)),
                  pl.BlockSpec((HALO, DP), lambda i: (jnp.maximum(i * hb - 1, 0), 0)),
                  pl.BlockSpec((tt, D), lambda i: (i, 0)),
                  _full((len(WINS), PC, PC)), _full((1, DP)), _full((1, DA)), _full((1, DP)),
                  _resident((D, D)), _full((1, D)), ANY],
        out_specs=[pl.BlockSpec((tt, DP), lambda i: (i, 0)), pl.BlockSpec((tt, DP), lambda i: (i, 0)),
                   pl.BlockSpec((tt, D), lambda i: (i, 0)), pl.BlockSpec((tt, D), lambda i: (i, 0))],
        out_shape=[jax.ShapeDtypeStruct((t, DP), BF), jax.ShapeDtypeStruct((t, DP), F32),
                   jax.ShapeDtypeStruct((t, D), F32), jax.ShapeDtypeStruct((t, D), F32)],
        compiler_params=_cp(),
    )(a, u, u, x, w_pool, pool_scale, g_attn, g_pool, w_out, g_post, dep)


def _ffn_fwd(h1, g_pre, stacks2, g_post, p, w_ple, g_ple, w_pg, tgt):
    t = h1.shape[0]
    tt = 256 if t % 256 == 0 else t

    def body(h1_ref, gpre_ref, wg_ref, wu_ref, wd_ref, gpost_ref, p_ref, wple_ref, gple_ref, wpg_ref, tgt_ref,
             hn_ref, gate_ref, up_ref, dff_ref, dh2_ref, loss_ref, dwpg_ref, dwple_ref, dgple_ref, dgpost_ref):
        i = pl.program_id(0)

        @pl.when(i == 0)
        def _():
            loss_ref[...] = jnp.zeros_like(loss_ref)
            dwpg_ref[...] = jnp.zeros_like(dwpg_ref)
            dwple_ref[...] = jnp.zeros_like(dwple_ref)
            dgple_ref[...] = jnp.zeros_like(dgple_ref)
            dgpost_ref[...] = jnp.zeros_like(dgpost_ref)

        h1v = h1_ref[...]
        hn = (h1v * _rstd(h1v) * gpre_ref[...]).astype(BF)
        hn_ref[...] = hn
        gate = _mm_nt(hn, wg_ref[...].reshape(DFF, D))
        up = _mm_nt(hn, wu_ref[...].reshape(DFF, D))
        for k in range(DFF // FF_CH):
            gate_ref[k] = gate[:, FF_CH * k:FF_CH * (k + 1)].astype(BF)
            up_ref[k] = up[:, FF_CH * k:FF_CH * (k + 1)].astype(BF)
        ff = _mm(gate * _sigmoid(gate) * up, wd_ref[...].reshape(DFF, D))
        rff = _rstd(ff)
        ffh = ff * rff
        gpost = gpost_ref[...]
        h2 = h1v + ffh * gpost
        pv = p_ref[...]
        pe = _mm(pv, wple_ref[...])
        rpe = _rstd(pe)
        peh = pe * rpe
        gple = gple_ref[...]
        e = peh * gple
        sig = _sigmoid(_mm(h2, wpg_ref[...]))
        dv = h2 + sig * e - tgt_ref[...]
        sq = jnp.sum(jnp.sum(dv * dv, axis=1, keepdims=True), axis=0, keepdims=True)
        loss_ref[...] = loss_ref[...] + sq
        dy = dv * (1.0 / D)
        d_e = dy * sig
        d_gl = dy * e * sig * (1.0 - sig)
        dh2 = dy + _mm_nt(d_gl, wpg_ref[...])
        dh2_ref[...] = dh2
        dwpg_ref[...] = dwpg_ref[...] + _mm_tn(h2, d_gl)
        dgple_ref[...] = dgple_ref[...] + jnp.sum(d_e * peh, axis=0, keepdims=True)
        dpeh = d_e * gple
        d_pe = rpe * (dpeh - peh * jnp.mean(dpeh * peh, axis=-1, keepdims=True))
        dwple_ref[...] = dwple_ref[...] + _mm_tn(pv, d_pe)
        dgpost_ref[...] = dgpost_ref[...] + jnp.sum(dh2 * ffh, axis=0, keepdims=True)
        dffh = dh2 * gpost
        dff_ref[...] = (rff * (dffh - ffh * jnp.mean(dffh * ffh, axis=-1, keepdims=True))).astype(BF)

    row = lambda w: pl.BlockSpec((tt, w), lambda i: (i, 0))
    chunked = pl.BlockSpec((DFF // FF_CH, tt, FF_CH), lambda i: (0, i, 0))
    shard_rows = lambda k: pl.BlockSpec((NSHARD, FF_SH, D), lambda i: (0, k, 0), pipeline_mode=pl.Buffered(1))
    return pl.pallas_call(
        body, name="ffn_fwd", grid=(t // tt,),
        in_specs=[row(D), _full((1, D)), shard_rows(0), shard_rows(1), shard_rows(2), _full((1, D)),
                  row(DPLE), _resident((DPLE, D)), _full((1, D)), _resident((D, D)), row(D)],
        out_specs=[row(D), chunked, chunked, row(D), row(D), _full((8, LANE)), _full((D, D)), _full((DPLE, D)),
                   _full((1, D)), _full((1, D))],
        out_shape=[jax.ShapeDtypeStruct((t, D), BF), jax.ShapeDtypeStruct((DFF // FF_CH, t, FF_CH), BF),
                   jax.ShapeDtypeStruct((DFF // FF_CH, t, FF_CH), BF),
                   jax.ShapeDtypeStruct((t, D), BF), jax.ShapeDtypeStruct((t, D), F32), jax.ShapeDtypeStruct((8, LANE), F32),
                   jax.ShapeDtypeStruct((D, D), F32), jax.ShapeDtypeStruct((DPLE, D), F32),
                   jax.ShapeDtypeStruct((1, D), F32), jax.ShapeDtypeStruct((1, D), F32)],
        compiler_params=_cp(),
    )(h1, g_pre, stacks2, stacks2, stacks2, g_post, p, w_ple, g_ple, w_pg, tgt)


FF_CH = 256


def _ffn_bwd(hn2, gate, up, dff, wgu, wd):
    t = hn2.shape[0]
    tt = 1024 if t % 1024 == 0 else _tile(t)
    nt = t // tt
    ch = FF_CH
    nc = DFF // ch

    def body(hn_ref, gate_ref, up_ref, dff_ref, wgu_ref, wd_ref,
             dwg_ref, dwu_ref, dwd_ref, dhn_ref, acc, gu_acc, d_acc, sem):
        j, i = pl.program_id(0), pl.program_id(1)

        @pl.when(j == 0)
        def _():
            acc[pl.ds(pl.multiple_of(i * tt, tt), tt), :] = jnp.zeros((tt, D), F32)

        @pl.when(i == 0)
        def _():
            gu_acc[...] = jnp.zeros_like(gu_acc)
            d_acc[...] = jnp.zeros_like(d_acc)

        parts = 4 if tt % 1024 == 0 else 2
        half = tt // parts
        acts, dgus = [], []
        for hh in range(parts):
            r = slice(hh * half, (hh + 1) * half)
            gate_v = gate_ref[0, r, :].astype(F32)
            up_v = up_ref[0, r, :].astype(F32)
            sg = _sigmoid(gate_v)
            silu = gate_v * sg
            d_act = _mm_nt(dff_ref[r, :], wd_ref[...])
            d_up = (d_act * silu).astype(BF)
            d_gate = (d_act * up_v * (sg * (1.0 + gate_v * (1.0 - sg)))).astype(BF)
            dgu = jnp.concatenate([d_gate, d_up], axis=1)
            rows = pl.ds(pl.multiple_of(i * tt + hh * half, half), half)
            acc[rows, :] = acc[rows, :] + jnp.dot(dgu, wgu_ref[0], preferred_element_type=F32)
            acts.append((silu * up_v).astype(BF))
            dgus.append(dgu)
        d_acc[...] = d_acc[...] + _mm_tn(jnp.concatenate(acts, axis=0), dff_ref[...])
        gu_acc[...] = gu_acc[...] + _mm_tn(jnp.concatenate(dgus, axis=0), hn_ref[...])

        @pl.when(i == nt - 1)
        def _():
            dwg_ref[...] = gu_acc[:ch].astype(BF)
            dwu_ref[...] = gu_acc[ch:].astype(BF)
            dwd_ref[...] = d_acc[...].astype(BF)

        @pl.when((j == nc - 1) & (i == nt - 1))
        def _():
            cp = pltpu.make_async_copy(acc, dhn_ref, sem)
            cp.start()
            cp.wait()

    tok = lambda w: pl.BlockSpec((tt, w), lambda j, i: (i, 0))
    chunk = pl.BlockSpec((ch, D), lambda j, i: (j, 0))
    pair = pl.BlockSpec((1, 2 * ch, D), lambda j, i: (j, 0, 0))
    return pl.pallas_call(
        body, name="ffn_bwd", grid=(nc, nt),
        in_specs=[tok(D), pl.BlockSpec((1, tt, ch), lambda j, i: (j, i, 0)), pl.BlockSpec((1, tt, ch), lambda j, i: (j, i, 0)),
                  tok(D), pair, chunk],
        out_specs=[chunk, chunk, chunk, pl.BlockSpec(memory_space=pl.ANY)],
        out_shape=[jax.ShapeDtypeStruct((DFF, D), BF), jax.ShapeDtypeStruct((DFF, D), BF),
                   jax.ShapeDtypeStruct((DFF, D), BF), jax.ShapeDtypeStruct((t, D), F32)],
        scratch_shapes=[pltpu.VMEM((t, D), F32), pltpu.VMEM((2 * ch, D), F32), pltpu.VMEM((ch, D), F32),
                        pltpu.SemaphoreType.DMA],
        compiler_params=_cp(),
    )(hn2, gate, up, dff, wgu, wd)


def _mix_bwd(d_hn2, dh2, h1, o, a, m, yb, g_ffn_pre, g_post, g_attn, g_pool, w_out, w_pool, pool_scale, dep):
    t = a.shape[0]
    tt = _tile(t)

    def body(dhn_ref, dh2_ref, h1_ref, o_ref, a_ref, m_ref, yb_ref, gfp_ref, go_ref, ga_ref, gp_ref, wo_ref, wp_ref,
             ps_ref, dep_ref, dh1_ref, da_ref, dyc_ref, dgfp_ref, dgo_ref, dga_ref, dgp_ref, dps_ref, dwp_ref, dwo_ref):
        i = pl.program_id(0)

        @pl.when(i == 0)
        def _():
            for r in (dgfp_ref, dgo_ref, dga_ref, dgp_ref, dps_ref, dwp_ref, dwo_ref):
                r[...] = jnp.zeros_like(r)

        d1, dg = _rms_bwd(dhn_ref[...], h1_ref[...], gfp_ref[...])
        dgfp_ref[...] = dgfp_ref[...] + dg
        dh1 = dh2_ref[...] + d1
        dh1_ref[...] = dh1
        d_o, dg = _rms_bwd(dh1, o_ref[...], go_ref[...])
        dgo_ref[...] = dgo_ref[...] + dg
        d_mix = _mm_nt(d_o, wo_ref[...])
        av, mv = a_ref[...], m_ref[...]
        mix = jnp.concatenate([av * _rstd(av) * ga_ref[...], mv * _rstd(mv) * gp_ref[...]], axis=1)
        dwo_ref[...] = dwo_ref[...] + _mm_tn(mix, d_o)
        d_a, dg = _rms_bwd(d_mix[:, :DA], av, ga_ref[...])
        dga_ref[...] = dga_ref[...] + dg
        da_ref[...] = d_a
        d_m, dg = _rms_bwd(d_mix[:, DA:], mv, gp_ref[...])
        dgp_ref[...] = dgp_ref[...] + dg
        tok = i * tt + lax.broadcasted_iota(jnp.int32, (tt, PC), 0)
        dps = []
        for g, w in enumerate(WINS):
            sl = slice(PC * g, PC * (g + 1))
            ybg = yb_ref[:, sl]
            wpg = wp_ref[g].astype(BF)
            mlin = jnp.dot(ybg, wpg, preferred_element_type=F32)
            dmg = d_m[:, sl]
            dps.append(jnp.sum(dmg * mlin, axis=0, keepdims=True))
            dml = (dmg * ps_ref[:, sl]).astype(BF)
            dwp_ref[g] = dwp_ref[g] + _mm_tn(ybg, dml)
            dyc_ref[:, sl] = _mm_nt(dml, wpg) / jnp.minimum(tok + 1, w).astype(F32)
        dps_ref[...] = dps_ref[...] + jnp.concatenate(dps, axis=1)

    row = lambda w: pl.BlockSpec((tt, w), lambda i: (i, 0))
    return pl.pallas_call(
        body, name="mix_bwd", grid=(t // tt,),
        in_specs=[row(D), row(D), row(D), row(D), row(DA), row(DP), row(DP), _full((1, D)), _full((1, D)),
                  _full((1, DA)), _full((1, DP)), _resident((D, D)), _full((len(WINS), PC, PC)), _full((1, DP)), ANY],
        out_specs=[row(D), row(DA), row(DP), _full((1, D)), _full((1, D)), _full((1, DA)), _full((1, DP)),
                   _full((1, DP)), _full((len(WINS), PC, PC)), _full((D, D))],
        out_shape=[jax.ShapeDtypeStruct((t, D), F32), jax.ShapeDtypeStruct((t, DA), F32), jax.ShapeDtypeStruct((t, DP), F32),
                   jax.ShapeDtypeStruct((1, D), F32), jax.ShapeDtypeStruct((1, D), F32), jax.ShapeDtypeStruct((1, DA), F32),
                   jax.ShapeDtypeStruct((1, DP), F32), jax.ShapeDtypeStruct((1, DP), F32),
                   jax.ShapeDtypeStruct((len(WINS), PC, PC), F32), jax.ShapeDtypeStruct((D, D), F32)],
        compiler_params=_cp(),
    )(d_hn2, dh2, h1, o, a, m, yb, g_ffn_pre, g_post, g_attn, g_pool, w_out, w_pool, pool_scale, dep)


def _attn_bwd(qa, ka, v, a, d_a, lse, dep):
    t = qa.shape[0]
    ta = _tile(t)
    n = t // ta

    def body(q_ref, k_ref, v_ref, o_ref, do_ref, lse_ref, dep_ref, dq_ref, dk_ref, dv_ref):
        j = pl.program_id(1)

        @pl.when(j == 0)
        def _():
            dq_ref[...] = jnp.zeros_like(dq_ref)

        dk_ref[...] = jnp.zeros_like(dk_ref)
        dv_ref[...] = jnp.zeros_like(dv_ref)
        ks = [k_ref[:, LANE * h:LANE * (h + 1)] for h in range(2)]
        v2 = v_ref[...]
        lane = lax.broadcasted_iota(jnp.int32, (1, LANE), 1)
        mine = [lane < HD, lane >= HD]

        def tile(i, masked):
            rows = pl.ds(pl.multiple_of(i * ta, ta), ta)
            do2 = do_ref[rows, :]
            prod = do2 * o_ref[rows, :]
            lse2 = lse_ref[rows, :]
            do2b = do2.astype(BF)
            qh = [q_ref[rows, LANE * h:LANE * (h + 1)] for h in range(2)]
            s = [_mm_nt(qh[h], ks[h]) for h in range(2)]
            dp = [_mm_nt(jnp.where(mine[h], do2, 0.0), v2) for h in range(2)]
            delta = [jnp.sum(jnp.where(mine[h], prod, 0.0), axis=1, keepdims=True) for h in range(2)]
            lse_h = [jnp.sum(jnp.where(lane == HD * h, lse2, 0.0), axis=1, keepdims=True) for h in range(2)]
            pr = [jnp.exp(s[h] - lse_h[h]) for h in range(2)]
            if masked:
                keep = (lax.broadcasted_iota(jnp.int32, (ta, ta), 1) <= lax.broadcasted_iota(jnp.int32, (ta, ta), 0))
                pr = [jnp.where(keep, ph, 0.0) for ph in pr]
            ds = [(pr[h] * (dp[h] - delta[h])).astype(BF) for h in range(2)]
            dv_ref[...] = dv_ref[...] + jnp.where(mine[0], _mm_tn(pr[0], do2b), _mm_tn(pr[1], do2b))
            for h in range(2):
                sl = slice(LANE * h, LANE * (h + 1))
                dk_ref[:, sl] = dk_ref[:, sl] + _mm_tn(ds[h], qh[h])
                dq_ref[0, rows, sl] = dq_ref[0, rows, sl] + jnp.dot(ds[h], ks[h], preferred_element_type=F32)

        def step(i, carry):
            tile(i, False)
            return carry

        tile(j, True)
        lax.fori_loop(j + 1, n, step, 0)

    qrow = lambda w: pl.BlockSpec((t, w), lambda p, j: (0, p))
    krow = lambda w: pl.BlockSpec((ta, w), lambda p, j: (j, p))
    return pl.pallas_call(
        body, name="attn_bwd", grid=(NH // 2, n),
        in_specs=[qrow(2 * LANE), krow(2 * LANE), krow(LANE), qrow(LANE), qrow(LANE), qrow(LANE), ANY],
        out_specs=[pl.BlockSpec((1, t, 2 * LANE), lambda p, j: (p, 0, 0)), krow(2 * LANE), krow(LANE)],
        out_shape=[jax.ShapeDtypeStruct((NH // 2, t, 2 * LANE), F32), jax.ShapeDtypeStruct((t, NH * LANE), F32),
                   jax.ShapeDtypeStruct((t, DA), F32)],
        compiler_params=_cp(),
    )(qa, ka, v, a, d_a, lse, dep)


def _in_bwd(dqa, dka, dv, dyc, fl, x, dh1, g1, w_in_t, tri_u, dep):
    t = x.shape[0]
    tt = _tile(t)
    nt = t // tt
    hb = tt // HALO
    rev = lambda s: nt - 1 - s

    def body(dqa_ref, dka_ref, dv_ref, dyc_ref, dyn_ref, fl_ref, x_ref, dh1_ref, g_ref, w_ref, tri_ref,
             dep_ref, dx_ref, dw_ref, dg_ref, db_ref, carry, acc, stage, sem):
        s = pl.program_id(0)
        i = nt - 1 - s

        @pl.when(s == 0)
        def _():
            carry[...] = jnp.zeros_like(carry)
            acc[...] = jnp.zeros_like(acc)
            dg_ref[...] = jnp.zeros_like(dg_ref)
            db_ref[...] = jnp.zeros_like(db_ref)

        dq_cat = jnp.concatenate([dqa_ref[p] for p in range(NH // 2)], axis=1)
        dk_cat = dka_ref[...]
        off = lax.broadcasted_iota(jnp.int32, (1, NH * LANE), 1)
        off = off % LANE - HD - off // LANE

        def picked(cat, lane_off):
            kept = jnp.where(off == lane_off, cat, 0.0)
            return functools.reduce(lambda a, b: a + b, [kept[:, LANE * h:LANE * (h + 1)] for h in range(NH)])

        dc = pltpu.roll(picked(dq_cat, 0), LANE - HD, 1) - pltpu.roll(picked(dk_cat, 24), LANE - HD - 24, 1)
        dlf = carry[...] + _dot3(tri_ref[...], dc)
        carry[...] = carry[...] + jnp.sum(dc, axis=0, keepdims=True)
        flv = fl_ref[...]
        lane = lax.broadcasted_iota(jnp.int32, flv.shape, 1)
        d_fl = jnp.where(lane < NH, dlf / (1.0 + jnp.exp(flv)), 0.0)
        db_ref[...] = db_ref[...] + jnp.sum(d_fl, axis=0, keepdims=True)
        low = lax.broadcasted_iota(jnp.int32, (tt, LANE), 1) < HD
        dqs, dks = [], []
        for p in range(NH // 2):
            b0, b1 = slice(2 * LANE * p, 2 * LANE * p + LANE), slice(2 * LANE * p + LANE, 2 * LANE * (p + 1))
            dqs.append(jnp.where(low, dq_cat[:, b0], pltpu.roll(dq_cat[:, b1], HD, 1)) * (1.0 / math.sqrt(HD)))
            dks.append(jnp.where(low, dk_cat[:, b0], pltpu.roll(dk_cat[:, b1], HD, 1)))
        nxt = dyn_ref[...] * jnp.where(i < nt - 1, 1.0, 0.0)
        tok = i * tt + lax.broadcasted_iota(jnp.int32, (tt, PC), 0)
        dus = []
        for g, w in enumerate(WINS):
            sl = slice(PC * g, PC * (g + 1))
            dycg = dyc_ref[:, sl]
            ext = jnp.concatenate([dycg, nxt[:, sl]], axis=0)
            dus.append(_window_sum(ext, w, False)[:tt] - dycg * jnp.minimum(tok + 1, w).astype(F32))
        d_z = jnp.concatenate(dqs + dks + [dv_ref[...], d_fl] + dus, axis=1).astype(BF)
        xv = x_ref[...]
        gv = g_ref[...]
        hn = (xv * _rstd(xv) * gv).astype(BF)
        d_hn = jnp.dot(d_z, w_ref[...], preferred_element_type=F32)
        acc[...] = acc[...] + _mm_tn(d_z, hn)
        d1, dg = _rms_bwd(d_hn, xv, gv)
        dg_ref[...] = dg_ref[...] + dg
        dx_ref[...] = dh1_ref[...] + d1

        @pl.when(s == nt - 1)
        def _():
            stage[...] = acc[...].astype(BF)
            cp = pltpu.make_async_copy(stage, dw_ref, sem)
            cp.start()
            cp.wait()

    row = lambda w: pl.BlockSpec((tt, w), lambda s: (rev(s), 0))
    return pl.pallas_call(
        body, name="in_bwd", grid=(nt,),
        in_specs=[pl.BlockSpec((NH // 2, tt, 2 * LANE), lambda s: (0, rev(s), 0)), row(NH * LANE), row(DA), row(DP),
                  pl.BlockSpec((HALO, DP), lambda s: (jnp.minimum((rev(s) + 1) * hb, nt * hb - 1), 0)),
                  row(LANE), row(D), row(D), _full((1, D)), _resident((ZW, D)), _full((tt, tt)),
                  ANY],
        out_specs=[row(D), pl.BlockSpec(memory_space=pl.ANY), _full((1, D)), _full((1, LANE))],
        out_shape=[jax.ShapeDtypeStruct((t, D), F32), jax.ShapeDtypeStruct((ZW, D), BF),
                   jax.ShapeDtypeStruct((1, D), F32), jax.ShapeDtypeStruct((1, LANE), F32)],
        scratch_shapes=[pltpu.VMEM((1, LANE), F32), pltpu.VMEM((ZW, D), F32), pltpu.VMEM((ZW, D), BF),
                        pltpu.SemaphoreType.DMA],
        compiler_params=_cp(),
    )(dqa, dka, dv, dyc, dyc, fl, x, dh1, g1, w_in_t, tri_u, dep)


class _NoComm:
    def __init__(self, w2):
        self.w2 = w2
        self.dep = jnp.zeros((8, LANE), F32)

    def after_attention(self, after):
        return self.dep

    def weights2(self, after):
        return _stack2_full(*self.w2)

    def after_ffn(self, grads2):
        self.grads2 = grads2
        return self.dep

    def after_mix(self, after, early):
        self.early = early
        return self.dep

    def after_attn(self, after):
        return self.dep


def _local_step(x, p, tgt, sm, w1, comm):
    w_in_t, w_out = w1
    tt = _tile(x.shape[0])
    eq, ek, rowq, rowk = _aug_consts()
    b_pad = jnp.pad(sm["b_forget"], ((0, 0), (0, LANE - NH)))
    qa, ka, v, u, fl = _in_proj(x, sm["g_mix_pre"], w_in_t, b_pad, _tri(tt, False), eq, ek, rowq, rowk, comm.dep)
    a, lse = _attn_fwd(qa, ka, v)
    yb, m, o, h1 = _mix_out(a, u, x, sm["w_pool"], sm["pool_scale"], sm["g_attn_grp"],
                            sm["g_pool_grp"], w_out, sm["g_mix_post"], comm.after_attention(a))
    stacks2 = comm.weights2(h1)
    wg_t, wu_t, wd, w_ple, w_pg = _unstack2_full(stacks2)
    hn2, gate, up, dff, dh2, loss, dwpg, dwple, dgple, dgfpost = _ffn_fwd(
        h1, sm["g_ffn_pre"], stacks2, sm["g_ffn_post"], p, w_ple, sm["g_ple"], w_pg, tgt)
    chunks = lambda a: a.reshape(DFF // FF_CH, FF_CH, D)
    dwg_t, dwu_t, dwd, d_hn2 = _ffn_bwd(hn2, gate, up, dff, jnp.concatenate([chunks(wg_t), chunks(wu_t)], axis=1), wd)
    dep = comm.after_ffn((dwg_t, dwu_t, dwd, dwple, dwpg))
    dh1, d_a, dyc, dgfpre, dgpost, dgattn, dgpool, dps, dwpool, dwout = _mix_bwd(
        d_hn2, dh2, h1, o, a, m, yb, sm["g_ffn_pre"], sm["g_mix_post"], sm["g_attn_grp"], sm["g_pool_grp"],
        w_out, sm["w_pool"], sm["pool_scale"], dep)
    early = dict(loss=loss[0:1, 0:1], g_attn_grp=dgattn, g_pool_grp=dgpool, w_pool=dwpool, pool_scale=dps,
                 g_mix_post=dgpost, g_ffn_pre=dgfpre, g_ffn_post=dgfpost, g_ple=dgple)
    dqa, dka, dvv = _attn_bwd(qa, ka, v, a, d_a, lse, comm.after_mix(dh1, early))
    dx, dwin_t, dg1, dbf = _in_bwd(dqa, dka, dvv, dyc, fl, x, dh1, sm["g_mix_pre"], w_in_t, _tri(tt, True),
                                   comm.after_attn(dvv))
    return dx, (dwin_t, dwout), dict(g_mix_pre=dg1, b_forget=dbf[:, :NH])


def _place():
    x, y, c = lax.axis_index("x"), lax.axis_index("y"), lax.axis_index("c")
    return x, y, c, [(1 - x, y), (x, 1 - y), (1 - x, 1 - y)]


def _rows(c, h):
    return pl.ds(pl.multiple_of(c * h, 16), h)


def _plan_gather(h):
    def plan(src, land):
        x, y, c, chips = _place()
        return [(src.at[_rows(c, h), :], land.at[2 * x + y, _rows(c, h), :], (cx, cy, c),
                 land.at[2 * cx + cy, _rows(c, h), :]) for cx, cy in chips]
    return plan


def _plan_forward(h):
    def plan(land_in, own, land):
        x, y, c, chips = _place()
        sib, me = (x, y, 1 - c), 2 * x + y
        return ([(land_in.at[2 * cx + cy, _rows(c, h), :], land.at[2 * cx + cy, _rows(c, h), :], sib,
                  land.at[2 * cx + cy, _rows(1 - c, h), :]) for cx, cy in chips]
                + [(own, land.at[me], sib, land.at[me])])
    return plan


def _plan_swap_halves(h):
    def plan(buf_in, buf):
        x, y, c, _ = _place()
        return [(buf_in.at[_rows(c, h), :], buf.at[_rows(c, h), :], (x, y, 1 - c), buf.at[_rows(1 - c, h), :])]
    return plan


def _plan_pair_rows(h):
    def plan(src, land):
        x, y, c, _ = _place()
        return [(src.at[:, _rows(1 - c, h), :], land, (x, y, 1 - c), land)]
    return plan


def _plan_scatter(src, land):
    x, y, c, chips = _place()
    return [(src.at[2 * cx + cy], land.at[k], (cx, cy, c), land.at[k]) for k, (cx, cy) in enumerate(chips)]


def _plan_scatter8(h):
    def plan(src, land):
        x, y, c, chips = _place()
        copies = [(src.at[2 * x + y, _rows(1 - c, h), :], land.at[0], (x, y, 1 - c), land.at[0])]
        for k, (cx, cy) in enumerate(chips):
            for d, other in enumerate((c, 1 - c)):
                copies.append((src.at[2 * cx + cy, _rows(other, h), :], land.at[1 + 2 * k + c], (cx, cy, other),
                               land.at[1 + 2 * k + other]))
        return copies
    return plan


def _plan_all(src, land):
    x, y, c, _ = _place()
    copies = []
    for r in range(1, 8):
        px, py, pc = (1 - a if b else a for a, b in zip((x, y, c), (r >> 2 & 1, r >> 1 & 1, r & 1)))
        copies.append((src, land.at[4 * x + 2 * y + c], (px, py, pc), land.at[4 * px + 2 * py + pc]))
    return copies


def _remote(src, dst, send_sems, recv_sems, k, peer):
    return pltpu.make_async_remote_copy(src_ref=src, dst_ref=dst, send_sem=send_sems.at[k], recv_sem=recv_sems.at[k],
                                        device_id=peer, device_id_type=MESH)


def _exchange(name, n, src, land, plan):
    def body(src_ref, land_ref, send_sems, recv_sems):
        copies = plan(src_ref, land_ref)
        for k, (s, d, peer, _) in enumerate(copies):
            _remote(s, d, send_sems, recv_sems, k, peer).start()
        for k, (s, _, peer, mine) in enumerate(copies):
            _remote(s, mine, send_sems, recv_sems, k, peer).wait_recv()
        for k, (s, d, peer, _) in enumerate(copies):
            _remote(s, d, send_sems, recv_sems, k, peer).wait_send()

    return pl.pallas_call(
        body, name=name, in_specs=[ANY], out_specs=ANY, out_shape=land,
        scratch_shapes=[pltpu.SemaphoreType.DMA((n,)), pltpu.SemaphoreType.DMA((n,))],
    )(src)


def _exchange_inplace(name, n, buf, extra, plan):
    def body(*refs):
        ins, buf_ref, send_sems, recv_sems = refs[:1 + len(extra)], refs[1 + len(extra)], refs[-2], refs[-1]
        copies = plan(*ins, buf_ref)
        for k, (s, d, peer, _) in enumerate(copies):
            _remote(s, d, send_sems, recv_sems, k, peer).start()
        for k, (s, _, peer, mine) in enumerate(copies):
            _remote(s, mine, send_sems, recv_sems, k, peer).wait_recv()
        for k, (s, d, peer, _) in enumerate(copies):
            _remote(s, d, send_sems, recv_sems, k, peer).wait_send()

    return pl.pallas_call(
        body, name=name, in_specs=[ANY] * (1 + len(extra)), out_specs=ANY, out_shape=_sds(buf.shape, buf.dtype),
        input_output_aliases={0: 0},
        scratch_shapes=[pltpu.SemaphoreType.DMA((n,)), pltpu.SemaphoreType.DMA((n,))],
    )(buf, *extra)


HBM = pl.BlockSpec(memory_space=pltpu.HBM)
SEM = pl.BlockSpec(memory_space=pltpu.SEMAPHORE)
EFFECT = pltpu.SideEffectType.DATAFLOW_SIDE_EFFECTING


def _exchange_start(name, n, src, land, plan):
    def body(src_ref, land_ref, send_sems, recv_sems, src_thru, land_thru, token):
        for k, (s, d, peer, _) in enumerate(plan(src_ref, land_ref)):
            _remote(s, d, send_sems, recv_sems, k, peer).start()
        token[...] = jnp.zeros_like(token)

    return pl.pallas_call(
        body, name=name,
        out_shape=(pltpu.SemaphoreType.DMA((n,)), pltpu.SemaphoreType.DMA((n,)), pltpu.HBM(src.shape, src.dtype),
                   pltpu.HBM(land.shape, land.dtype), jax.ShapeDtypeStruct((8, LANE), F32)),
        in_specs=(HBM, HBM), out_specs=(SEM, SEM, HBM, HBM, pl.BlockSpec(memory_space=pltpu.VMEM)),
        input_output_aliases={0: 2, 1: 3},
        compiler_params=pltpu.CompilerParams(has_side_effects=EFFECT),
    )(pltpu.with_memory_space_constraint(src, pltpu.HBM), pltpu.with_memory_space_constraint(land, pltpu.HBM))


def _exchange_wait(name, started, plan, after):
    send_sems, recv_sems, src, land, _ = started

    def body(src_ref, land_ref, send_sems, recv_sems, after_ref, src_out, land_out):
        for k, (s, _, peer, mine) in enumerate(plan(src_ref, land_ref)):
            cp = _remote(s, mine, send_sems, recv_sems, k, peer)
            cp.wait_send()
            cp.wait_recv()

    return pl.pallas_call(
        body, name=name, out_shape=(pltpu.HBM(src.shape, src.dtype), pltpu.HBM(land.shape, land.dtype)),
        in_specs=(HBM, HBM, SEM, SEM, ANY), out_specs=(HBM, HBM), input_output_aliases={0: 0, 1: 1},
        compiler_params=pltpu.CompilerParams(has_side_effects=EFFECT),
    )(src, land, send_sems, recv_sems, after)


def _pair_sum(name, cidx, g, recv, br):
    h = recv.shape[1]
    nb = h // br

    def body(c_ref, g_ref, r_ref, out_ref):
        out_ref[...] = (g_ref[...].astype(F32) + r_ref[...].astype(F32)).astype(BF)

    return pl.pallas_call(
        body, name=name,
        grid_spec=pltpu.PrefetchScalarGridSpec(
            num_scalar_prefetch=1, grid=(NSHARD, nb),
            in_specs=[pl.BlockSpec((1, br, D), lambda s, i, c: (s, c[0] * nb + i, 0)),
                      pl.BlockSpec((1, br, D), lambda s, i, c: (s, i, 0))],
            out_specs=pl.BlockSpec((1, br, D), lambda s, i, c: (s, i, 0))),
        out_shape=jax.ShapeDtypeStruct((NSHARD, h, D), BF),
    )(cidx, g, recv)


def _chip_sum(name, place, pb, y, br):
    h = y.shape[1]
    nb = h // br

    def body(pl_ref, p_ref, y_ref, out_ref):
        acc = p_ref[0].astype(F32)
        for k in range(NSHARD - 1):
            acc = acc + y_ref[k].astype(F32)
        out_ref[...] = acc

    return pl.pallas_call(
        body, name=name,
        grid_spec=pltpu.PrefetchScalarGridSpec(
            num_scalar_prefetch=1, grid=(nb,),
            in_specs=[pl.BlockSpec((1, br, D), lambda i, s: (s[0], i, 0)),
                      pl.BlockSpec((NSHARD - 1, br, D), lambda i, s: (0, i, 0))],
            out_specs=pl.BlockSpec((br, D), lambda i, s: (s[1] * nb + i, 0))),
        out_shape=jax.ShapeDtypeStruct((2 * h, D), F32),
    )(place, pb, y)


def _sum8(name, place, g, land, br):
    h = land.shape[1]
    nb = h // br

    def body(pl_ref, g_ref, y_ref, out_ref):
        acc = g_ref[0].astype(F32)
        for k in range(land.shape[0]):
            acc = acc + y_ref[k].astype(F32)
        out_ref[...] = acc

    return pl.pallas_call(
        body, name=name,
        grid_spec=pltpu.PrefetchScalarGridSpec(
            num_scalar_prefetch=1, grid=(nb,),
            in_specs=[pl.BlockSpec((1, br, D), lambda i, s: (s[0], s[1] * nb + i, 0)),
                      pl.BlockSpec((land.shape[0], br, D), lambda i, s: (0, i, 0))],
            out_specs=pl.BlockSpec((br, D), lambda i, s: (s[1] * nb + i, 0))),
        out_shape=jax.ShapeDtypeStruct((2 * h, D), F32), compiler_params=_cp(),
    )(place, g, land)


def _sum_slots(name, v):
    def body(in_ref, out_ref):
        acc = in_ref[0]
        for k in range(1, 8):
            acc = acc + in_ref[k]
        out_ref[...] = acc

    vm = pl.BlockSpec(memory_space=pltpu.VMEM)
    return pl.pallas_call(body, name=name, in_specs=[vm], out_specs=vm,
                          out_shape=jax.ShapeDtypeStruct(v.shape[1:], F32))(v)


def _adamw_math(w, g, m, v):
    m = ADAM_B1 * m + (1.0 - ADAM_B1) * g
    v = ADAM_B2 * v + (1.0 - ADAM_B2) * (g * g)
    m_hat = m / (1.0 - ADAM_B1 ** ADAM_STEP)
    v_hat = v / (1.0 - ADAM_B2 ** ADAM_STEP)
    delta = -ADAM_LR * (m_hat / (jnp.sqrt(v_hat) + ADAM_EPS) + ADAM_WD * w)
    return delta, m, v


def _adamw(w, g, m, v, dep, row0=None):
    r, c = w.shape
    br = next(b for b in (352, 256, 128, r) if r % b == 0 and (row0 or 0) % b == 0)
    first = (row0 or 0) // br

    def body(w_ref, g_ref, m_ref, v_ref, dep_ref, *outs):
        gv = g_ref[...]
        outs[-3][...], outs[-2][...], outs[-1][...] = _adamw_math(w_ref[...], gv, m_ref[...], v_ref[...])
        if row0 is not None:
            outs[0][...] = gv

    spec = pl.BlockSpec((br, c), lambda i: (i, 0))
    n_out = 3 if row0 is None else 4
    out = pl.pallas_call(
        body, name="adamw", grid=(r // br,),
        in_specs=[spec, pl.BlockSpec((br, c), lambda i: (first + i, 0)), spec, spec, ANY], out_specs=[spec] * n_out,
        out_shape=[jax.ShapeDtypeStruct((r, c), F32)] * n_out, compiler_params=_cp(),
    )(w, g, m, v, dep)
    return out if row0 is not None else [g] + list(out)


def _adamw_small(ws, gs, ms, vs):
    n = len(ws)

    def body(*refs):
        ins, outs = refs[:4 * n], refs[4 * n:]
        for k in range(n):
            d, m, v = _adamw_math(ins[k][...], ins[n + k][...], ins[2 * n + k][...], ins[3 * n + k][...])
            outs[k][...] = d
            outs[n + k][...] = m
            outs[2 * n + k][...] = v

    vm = pl.BlockSpec(memory_space=pltpu.VMEM)
    out = pl.pallas_call(
        body, name="adamw_small", in_specs=[vm] * (4 * n), out_specs=[vm] * (3 * n),
        out_shape=[jax.ShapeDtypeStruct(w.shape, F32) for w in ws] * 3,
    )(*ws, *gs, *ms, *vs)
    return out[:n], out[n:2 * n], out[2 * n:]


BIG = ("w_in", "w_out", "w_ffn_gate", "w_ffn_up", "w_ffn_down", "w_ple_proj", "w_ple_gate")
SMALL = ("g_mix_pre", "b_forget", "g_attn_grp", "g_pool_grp", "w_pool", "pool_scale", "g_mix_post", "g_ffn_pre",
         "g_ffn_post", "g_ple")
TRANSPOSED = ("w_in", "w_ffn_gate", "w_ffn_up")
VECTORS = tuple(n for n in SMALL if n != "w_pool")
ORDER = ("g_mix_pre", "w_in", "b_forget", "g_attn_grp", "g_pool_grp", "w_pool", "pool_scale", "w_out", "g_mix_post",
         "g_ffn_pre", "w_ffn_gate", "w_ffn_up", "w_ffn_down", "g_ffn_post", "w_ple_proj", "g_ple", "w_ple_gate")


def _pad_rows(a, rows):
    return jnp.pad(a, ((0, rows - a.shape[0]), (0, 0)))


def _stack1(w_in, w_out):
    return _pad_rows(jnp.concatenate([_pad_rows(w_in.T, IN_PAD), w_out], axis=0), ROWS1)


def _stack2(wg, wu, wd, wple, wpg):
    return _pad_rows(jnp.concatenate([wg.T, wu.T, wd, wple.reshape(DPLE // NSHARD, D), wpg], axis=0), ROWS2)


def _unstack1(s):
    return s[:IN_SH], s[O1_OUT:USED1]


def _cat(g, lo, hi):
    return g[:, lo:hi].reshape(NSHARD * (hi - lo), D)


def _unstack1_full(g):
    w_in_t = _cat(g, 0, IN_SH)
    w_in_t = jnp.concatenate([w_in_t[:3 * DA], _pad_rows(w_in_t[3 * DA:3 * DA + NH], LANE), w_in_t[3 * DA + NH:]], axis=0)
    return w_in_t, _cat(g, O1_OUT, USED1)


def _unstack2_full(g):
    w_ple = g[:, O2_PLE:O2_PG].reshape(NSHARD, DPLE, DPLE).transpose(1, 0, 2).reshape(DPLE, D)
    return _cat(g, 0, O2_U), _cat(g, O2_U, O2_D), _cat(g, O2_D, O2_PLE), w_ple, _cat(g, O2_PG, USED2)


def _shards(a):
    return a.reshape(NSHARD, a.shape[0] // NSHARD, D)


def _stack1_full(dwin_t, dwout):
    dwin_t = jnp.concatenate([dwin_t[:3 * DA + NH], dwin_t[3 * DA + LANE:]], axis=0).reshape(NSHARD, IN_SH, D)
    zeros = lambda r: jnp.zeros((NSHARD, r, D), dwin_t.dtype)
    return jnp.concatenate([dwin_t, zeros(IN_PAD - IN_SH), _shards(dwout), zeros(ROWS1 - USED1)], axis=1)


def _stack2_full(dwg_t, dwu_t, dwd, dwple, dwpg):
    dwple = dwple.reshape(DPLE, NSHARD, DPLE).transpose(1, 0, 2).reshape(NSHARD, DPLE // NSHARD, D)
    return jnp.concatenate([_shards(dwg_t), _shards(dwu_t), _shards(dwd), dwple, _shards(dwpg),
                            jnp.zeros((NSHARD, ROWS2 - USED2, D), dwd.dtype)], axis=1)


def _sds(shape, dtype):
    return jax.ShapeDtypeStruct(shape, dtype)


class _Comm:
    def __init__(self, stack2, me, c):
        self.stack2, self.me, self.c = stack2, me, c
        self.cidx = c.astype(jnp.int32).reshape(1)
        self.place = jnp.stack([me, c]).astype(jnp.int32)
        self.h = ROWS2 // 2
        self.gather = _exchange_start("gather2_start", 3, stack2, lax.empty((NSHARD, ROWS2, D), BF), _plan_gather(self.h))
        self.dep = self.gather[4]

    def after_attention(self, after):
        own, land = _exchange_wait("gather2_wait", self.gather, _plan_gather(self.h), after)
        fwd = _plan_forward(self.h)
        self.forward = lambda own_ref, land_ref: fwd(land_ref, own_ref, land_ref)
        self.passing = _exchange_start("forward2_start", 4, own, land, self.forward)
        return self.passing[4]

    def weights2(self, after):
        return _exchange_wait("forward2_wait", self.passing, self.forward, after)[1]

    def after_ffn(self, grads2):
        g = _stack2_full(*[a.astype(BF) for a in grads2])
        self.scatter = _plan_scatter8(self.h)
        self.chip = _exchange_start("reduce2_start", 7, g, lax.empty((7, self.h, D), BF), self.scatter)
        return self.chip[4]

    def after_mix(self, after, early):
        self.early_shapes = {n: early[n].shape for n in early}
        self.small = self.start_small("small", early)
        return self.small[4]

    def after_attn(self, after):
        g, y = _exchange_wait("reduce2_wait", self.chip, self.scatter, after)
        f = _sum8("sum2", self.place, g, y, RED2 // 2)
        self.early = self.finish_small("small", self.small, self.early_shapes, after)
        swap = _plan_swap_halves(self.h)
        self.swap = lambda _, buf: swap(buf, buf)
        self.swapping = _exchange_start("reduce2_gather_start", 1, self.dep, f, self.swap)
        return self.swapping[4]

    def reduced2(self, after):
        return _exchange_wait("reduce2_gather_wait", self.swapping, self.swap, after)[1]

    def start_small(self, name, small):
        v = _pack_small(small)
        return _exchange_start(name + "_start", 7, v, lax.empty((8,) + v.shape, F32), _plan_all)

    def finish_small(self, name, started, shapes, after):
        v, land = _exchange_wait(name + "_wait", started, _plan_all, after)
        land = lax.dynamic_update_slice(land, v[None], (2 * self.me + self.c, 0, 0))
        return _unpack_small(_sum_slots(name + "_sum", land), shapes)


def _pack_small(small):
    parts = []
    for name in small:
        flat = small[name].reshape(-1)
        parts.append(jnp.pad(flat, (0, -flat.shape[0] % LANE)).reshape(-1, LANE))
    v = jnp.concatenate(parts, axis=0)
    return _pad_rows(v, v.shape[0] + (-v.shape[0] % 8))


def _unpack_small(v, shapes):
    out, r = {}, 0
    for name in shapes:
        n = math.prod(shapes[name])
        rows = -(-n // LANE)
        out[name] = v[r:r + rows].reshape(-1)[:n].reshape(shapes[name])
        r += rows
    return out


def kernel(x, p, g_mix_pre, w_in, b_forget, g_attn_grp, g_pool_grp, w_pool, pool_scale, w_out, g_mix_post, g_ffn_pre, w_ffn_gate, w_ffn_up, w_ffn_down, g_ffn_post, w_ple_proj, g_ple, w_ple_gate, loss_target, m_g_mix_pre, m_w_in, m_b_forget, m_g_attn_grp, m_g_pool_grp, m_w_pool, m_pool_scale, m_w_out, m_g_mix_post, m_g_ffn_pre, m_w_ffn_gate, m_w_ffn_up, m_w_ffn_down, m_g_ffn_post, m_w_ple_proj, m_g_ple, m_w_ple_gate, v_g_mix_pre, v_w_in, v_b_forget, v_g_attn_grp, v_g_pool_grp, v_w_pool, v_pool_scale, v_w_out, v_g_mix_post, v_g_ffn_pre, v_w_ffn_gate, v_w_ffn_up, v_w_ffn_down, v_g_ffn_post, v_w_ple_proj, v_g_ple, v_w_ple_gate):
    args = dict(locals())
    strip = lambda n, a: a if n in VECTORS else a[0]
    w = {n: strip(n, args[n]) for n in ORDER}
    mom = {n: strip(n, args["m_" + n]) for n in ORDER}
    var = {n: strip(n, args["v_" + n]) for n in ORDER}
    sm = {n: w[n] for n in SMALL}

    c = lax.axis_index("c")
    me = 2 * lax.axis_index("x") + lax.axis_index("y")
    h1 = ROWS1 // 2
    bf = lambda n: w[n].astype(BF)
    stack1 = _stack1(bf("w_in"), bf("w_out"))
    stack2 = _stack2(*[bf(n) for n in BIG[2:]])
    land = _exchange("gather1", 3, stack1, _sds((NSHARD, ROWS1, D), BF), _plan_gather(h1))
    land, stack2 = lax.optimization_barrier((land, stack2))
    comm = _Comm(stack2, me, c)
    w1 = _unstack1_full(_exchange_inplace("gather1_forward", 4, land, (stack1,), _plan_forward(h1)))
    dx, grads1, late = _local_step(x[0], p[0, 0], loss_target[0], sm, w1, comm)

    flip = lambda n, a: a.T if n in TRANSPOSED else a
    grads, delta, new_m, new_v = {}, {}, {}, {}

    def update(n, g, dep, row0=None):
        g_, d_, m_, v_ = _adamw(flip(n, w[n]), g, flip(n, mom[n]), flip(n, var[n]), dep, row0)
        grads[n], delta[n], new_m[n], new_v[n] = flip(n, g_), flip(n, d_), flip(n, m_), flip(n, v_)
        return v_

    late_shapes = {n: late[n].shape for n in late}
    small2 = comm.start_small("small2", late)
    red2 = comm.reduced2(dx)
    g1 = _stack1_full(grads1[0], grads1[1].astype(BF))
    pair1 = _exchange_start("reduce1_pair_start", 1, g1, lax.empty((NSHARD, h1, D), BF), _plan_pair_rows(h1))
    dep = update("w_ple_gate", red2, pair1[4] + small2[4], O2_PG)
    dep = update("w_ple_proj", red2[O2_PLE:O2_PG].reshape(DPLE, DPLE), dep)
    g1, recv = _exchange_wait("reduce1_pair_wait", pair1, _plan_pair_rows(h1), dep)
    pb = _pair_sum("pair_sum1", comm.cidx, g1, recv, RED1)
    chip1 = _exchange_start("reduce1_chip_start", 3, pb, lax.empty((NSHARD - 1, h1, D), BF), _plan_scatter)
    dep = update("w_ffn_gate", red2, chip1[4], 0)
    dep = update("w_ffn_up", red2, dep, O2_U)
    dep = update("w_ffn_down", red2, dep, O2_D)
    red_small = {**comm.early, **comm.finish_small("small2", small2, late_shapes, dep)}
    loss = 0.5 / D * red_small["loss"][0, 0]
    for n in SMALL:
        grads[n] = red_small[n].reshape(w[n].shape)
    two_d = lambda a: a.reshape(-1, a.shape[-1])
    ds, ms, vs = _adamw_small([two_d(w[n]) for n in SMALL], [two_d(grads[n]) for n in SMALL],
                              [two_d(mom[n]) for n in SMALL], [two_d(var[n]) for n in SMALL])
    for k, n in enumerate(SMALL):
        delta[n], new_m[n], new_v[n] = ds[k].reshape(w[n].shape), ms[k].reshape(w[n].shape), vs[k].reshape(w[n].shape)
    pb, y = _exchange_wait("reduce1_chip_wait", chip1, _plan_scatter, vs[0])
    f = _chip_sum("chip_sum1", comm.place, pb, y, RED1)
    reduced1 = _exchange_inplace("reduce1_gather", 1, f, (), _plan_swap_halves(h1))
    g_in, g_out = _unstack1(reduced1)
    update("w_out", g_out, update("w_in", g_in, reduced1))

    lead = lambda d: [d[n] if n in VECTORS else d[n][None] for n in ORDER]
    return (loss, dx[None], *lead(grads), *lead(delta), *lead(new_m), *lead(new_v))
```

```python
import functools
import math

import jax
import jax.numpy as jnp
import numpy as np
from jax import lax
from jax.experimental import pallas as pl
from jax.experimental.pallas import tpu as pltpu

F32 = jnp.float32
BF = jnp.bfloat16
MESH = pl.DeviceIdType.MESH

D = 1024
DA = 512
DP = 512
NH = 8
HD = 64
DFF = 2816
DPLE = 256
WINS = (2, 4, 8, 16)
PC = 128
ZW = 3 * DA + 128 + DP
EPS = 1e-6
NSHARD = 4

LANE = 128
HALO = 128

IN_SH = 514
IN_PAD = 528
FF_SH = DFF // NSHARD
O1_OUT, USED1, ROWS1 = 528, 784, 800
O2_U, O2_D, O2_PLE, O2_PG, USED2, ROWS2 = 704, 1408, 2112, 2176, 2432, 2560
RED1, RED2 = 400, 640

ADAM_LR, ADAM_B1, ADAM_B2, ADAM_EPS, ADAM_WD, ADAM_STEP = 0.001, 0.9, 0.999, 1e-8, 0.01, 10

VMEM_LIMIT = 56 * 1024 * 1024


def _cp(**kw):
    return pltpu.CompilerParams(vmem_limit_bytes=VMEM_LIMIT, **kw)


def _mm(a, b):
    return jnp.dot(a.astype(BF), b.astype(BF), preferred_element_type=F32)


def _mm_nt(a, b):
    return lax.dot_general(a.astype(BF), b.astype(BF), (((1,), (1,)), ((), ())), preferred_element_type=F32)


def _mm_tn(a, b):
    return lax.dot_general(a.astype(BF), b.astype(BF), (((0,), (0,)), ((), ())), preferred_element_type=F32)


def _split3(x):
    hi = x.astype(BF)
    r = x - hi.astype(F32)
    mid = r.astype(BF)
    lo = (r - mid.astype(F32)).astype(BF)
    return hi, mid, lo


def _dot3(m, x):
    hi, mid, lo = _split3(x)
    return (jnp.dot(m, hi, preferred_element_type=F32) + jnp.dot(m, mid, preferred_element_type=F32)
            + jnp.dot(m, lo, preferred_element_type=F32))


def _window_sum(ext, w, back):
    n = ext.shape[0]
    s, k = ext, 1
    while k < w:
        s = s + pltpu.roll(s, k if back else n - k, 0)
        k *= 2
    return s


def _rstd(x):
    return lax.rsqrt(jnp.mean(x * x, axis=-1, keepdims=True) + EPS)


def _rms_bwd(dy, x, g):
    r = _rstd(x)
    xh = x * r
    dg = jnp.sum(dy * xh, axis=0, keepdims=True)
    dxh = dy * g
    dx = r * (dxh - xh * jnp.mean(dxh * xh, axis=-1, keepdims=True))
    return dx, dg


def _sigmoid(x):
    return 1.0 / (1.0 + jnp.exp(-x))


ANY = pl.BlockSpec(memory_space=pl.ANY)


def _full(shape):
    n = len(shape)
    return pl.BlockSpec(shape, lambda *_: (0,) * n)


def _resident(shape):
    n = len(shape)
    return pl.BlockSpec(shape, lambda *_: (0,) * n, pipeline_mode=pl.Buffered(1))


def _tile(t):
    return 512 if t % 512 == 0 else t


def _tri(n, upper):
    r, c = np.indices((n, n))
    return ((c >= r) if upper else (c <= r)).astype(BF)


def _aug_consts():
    row, col = np.indices((LANE, NH * LANE))
    piece, head = row // NH, row % NH
    ch, cl = col // LANE, col % LANE
    eq = ((piece < 3) & (head == ch) & (cl == HD + 8 * piece + head)).astype(BF)
    ek = -((piece < 3) & (head == ch) & (cl == HD + 24 + 8 * piece + head)).astype(BF)
    off = (np.arange(NH * LANE) % LANE - HD - np.arange(NH * LANE) // LANE)[None, :]
    rowq = ((off >= 24) & (off < 48) & (off % 8 == 0)).astype(np.float32)
    rowk = ((off >= 0) & (off < 24) & (off % 8 == 0)).astype(np.float32)
    return eq, ek, rowq, rowk


def _in_proj(x, g1, w_in_t, b_pad, tri, eq, ek, rowq, rowk, dep):
    t = x.shape[0]
    tt = _tile(t)

    def body(x_ref, g_ref, w_ref, b_ref, tri_ref, eq_ref, ek_ref, rq_ref, rk_ref, dep_ref,
             qa_ref, ka_ref, v_ref, u_ref, fl_ref, carry):
        i = pl.program_id(0)

        @pl.when(i == 0)
        def _():
            carry[...] = jnp.zeros_like(carry)

        xv = x_ref[...]
        hn = (xv * _rstd(xv) * g_ref[...]).astype(BF)
        z = _mm_nt(hn, w_ref[...])
        fl = z[:, 3 * DA:3 * DA + LANE] + b_ref[...]
        lane = lax.broadcasted_iota(jnp.int32, fl.shape, 1)
        lf = jnp.where(lane < NH, jnp.minimum(fl, 0.0) - jnp.log(1.0 + jnp.exp(-jnp.abs(fl))), 0.0)
        c = carry[...] + _dot3(tri_ref[...], lf)
        carry[...] = carry[...] + jnp.sum(lf, axis=0, keepdims=True)
        hi, mid, lo = _split3(c)
        caug = (hi.astype(F32) + pltpu.roll(mid.astype(F32), NH, 1) + pltpu.roll(lo.astype(F32), 2 * NH, 1)).astype(BF)
        aug_q = jnp.dot(caug, eq_ref[...], preferred_element_type=F32) + rq_ref[...]
        aug_k = jnp.dot(caug, ek_ref[...], preferred_element_type=F32) + rk_ref[...]
        low = lax.broadcasted_iota(jnp.int32, (tt, LANE), 1) < HD
        for p in range(NH // 2):
            qp = z[:, LANE * p:LANE * (p + 1)] * (1.0 / math.sqrt(HD))
            kp = z[:, DA + LANE * p:DA + LANE * (p + 1)]
            for h, (qh, kh) in enumerate(((qp, kp), (pltpu.roll(qp, HD, 1), pltpu.roll(kp, HD, 1)))):
                lo_, hi_ = LANE * (2 * p + h), LANE * (2 * p + h + 1)
                qa_ref[:, lo_:hi_] = jnp.where(low, qh, aug_q[:, lo_:hi_]).astype(BF)
                ka_ref[:, lo_:hi_] = jnp.where(low, kh, aug_k[:, lo_:hi_]).astype(BF)
        v_ref[...] = z[:, 2 * DA:3 * DA].astype(BF)
        u_ref[...] = z[:, 3 * DA + LANE:]
        fl_ref[...] = fl

    return pl.pallas_call(
        body, name="in_proj", grid=(t // tt,),
        in_specs=[pl.BlockSpec((tt, D), lambda i: (i, 0)), _full((1, D)), _resident((ZW, D)), _full((1, LANE)),
                  _full((tt, tt)), _full((LANE, NH * LANE)), _full((LANE, NH * LANE)),
                  _full((1, NH * LANE)), _full((1, NH * LANE)), ANY],
        out_specs=[pl.BlockSpec((tt, NH * LANE), lambda i: (i, 0)), pl.BlockSpec((tt, NH * LANE), lambda i: (i, 0)),
                   pl.BlockSpec((tt, DA), lambda i: (i, 0)), pl.BlockSpec((tt, DP), lambda i: (i, 0)),
                   pl.BlockSpec((tt, LANE), lambda i: (i, 0))],
        out_shape=[jax.ShapeDtypeStruct((t, NH * LANE), BF), jax.ShapeDtypeStruct((t, NH * LANE), BF),
                   jax.ShapeDtypeStruct((t, DA), BF), jax.ShapeDtypeStruct((t, DP), F32),
                   jax.ShapeDtypeStruct((t, LANE), F32)],
        scratch_shapes=[pltpu.VMEM((1, LANE), F32)],
        compiler_params=_cp(),
    )(x, g1, w_in_t, b_pad, tri, eq, ek, rowq, rowk, dep)


def _attn_fwd(qa, ka, v):
    t = qa.shape[0]
    ta = _tile(t)
    n = t // ta

    def body(q_ref, k_ref, v_ref, a_ref, lse_ref, m_ref, l_ref, acc_ref):
        i = pl.program_id(1)
        m_ref[...] = jnp.full_like(m_ref, -1e30)
        l_ref[...] = jnp.zeros_like(l_ref)
        acc_ref[...] = jnp.zeros_like(acc_ref)
        qs = [q_ref[:, LANE * h:LANE * (h + 1)] for h in range(2)]
        reps = ta // LANE

        def tile(j, width, masked):
            rows = pl.ds(pl.multiple_of(j * ta, ta), width * ta)
            v2 = v_ref[rows, :]
            s = [_mm_nt(qs[h], k_ref[rows, LANE * h:LANE * (h + 1)]) for h in range(2)]
            if masked:
                keep = (lax.broadcasted_iota(jnp.int32, (ta, ta), 1) <= lax.broadcasted_iota(jnp.int32, (ta, ta), 0))
                s = [jnp.where(keep, sh, -1e30) for sh in s]
            m_old = [m_ref[h] for h in range(2)]
            m_new = [jnp.maximum(m_old[h], jnp.max(s[h], axis=1, keepdims=True)) for h in range(2)]
            pe = [jnp.exp(s[h] - jnp.tile(m_new[h], (1, width * reps))) for h in range(2)]
            alpha = [jnp.exp(m_old[h] - m_new[h]) for h in range(2)]
            pv = [jnp.dot(pe[h].astype(BF), v2, preferred_element_type=F32) for h in range(2)]
            for h in range(2):
                l_ref[h] = alpha[h] * l_ref[h] + jnp.sum(pe[h], axis=1, keepdims=True)
                acc_ref[h] = alpha[h] * acc_ref[h] + pv[h]
                m_ref[h] = m_new[h]

        def step(jj, carry):
            tile(2 * jj, 2, False)
            return carry

        lax.fori_loop(0, i // 2, step, 0)

        @pl.when(i % 2 == 1)
        def _():
            tile(i - 1, 1, False)

        tile(i, 1, True)
        low = lax.broadcasted_iota(jnp.int32, (ta, LANE), 1) < HD
        a_ref[...] = jnp.where(low, acc_ref[0] / l_ref[0], acc_ref[1] / l_ref[1])
        lse_ref[...] = jnp.where(low, m_ref[0] + jnp.log(l_ref[0]), m_ref[1] + jnp.log(l_ref[1]))

    return pl.pallas_call(
        body, name="attn_fwd", grid=(NH // 2, n),
        in_specs=[pl.BlockSpec((ta, 2 * LANE), lambda p, i: (i, p)),
                  pl.BlockSpec((t, 2 * LANE), lambda p, i: (0, p)),
                  pl.BlockSpec((t, LANE), lambda p, i: (0, p))],
        out_specs=[pl.BlockSpec((ta, LANE), lambda p, i: (i, p)), pl.BlockSpec((ta, LANE), lambda p, i: (i, p))],
        out_shape=[jax.ShapeDtypeStruct((t, DA), F32), jax.ShapeDtypeStruct((t, DA), F32)],
        scratch_shapes=[pltpu.VMEM((2, ta, LANE), F32), pltpu.VMEM((2, ta, LANE), F32), pltpu.VMEM((2, ta, LANE), F32)],
        compiler_params=_cp(),
    )(qa, ka, v)


def _mix_out(a, u, x, w_pool, pool_scale, g_attn, g_pool, w_out, g_post, dep):
    t = a.shape[0]
    tt = _tile(t)
    hb = tt // HALO

    def body(a_ref, u_ref, up_ref, x_ref, wp_ref, ps_ref, ga_ref, gp_ref, wo_ref, go_ref, dep_ref,
             yb_ref, m_ref, o_ref, h1_ref):
        i = pl.program_id(0)
        prev = up_ref[...] * jnp.where(i > 0, 1.0, 0.0)
        tok = i * tt + lax.broadcasted_iota(jnp.int32, (tt, PC), 0)
        ms = []
        for g, w in enumerate(WINS):
            ug = u_ref[:, PC * g:PC * (g + 1)]
            ext = jnp.concatenate([prev[:, PC * g:PC * (g + 1)], ug], axis=0)
            cnt = jnp.minimum(tok + 1, w).astype(F32)
            y = (_window_sum(ext, w, True)[HALO:] / cnt - ug).astype(BF)
            yb_ref[:, PC * g:PC * (g + 1)] = y
            ms.append(_mm(y, wp_ref[g]) * ps_ref[:, PC * g:PC * (g + 1)])
        m = jnp.concatenate(ms, axis=1)
        m_ref[...] = m
        av = a_ref[...]
        mix = jnp.concatenate([av * _rstd(av) * ga_ref[...], m * _rstd(m) * gp_ref[...]], axis=1)
        o = _mm(mix, wo_ref[...])
        o_ref[...] = o
        h1_ref[...] = x_ref[...] + o * _rstd(o) * go_ref[...]

    return pl.pallas_call(
        body, name="mix_out", grid=(t // tt,),
        in_specs=[pl.BlockSpec((tt, DA), lambda i: (i, 0)), pl.BlockSpec((tt, DP), lambda i: (i, 0)),
                  pl.BlockSpec((HALO, DP), lambda i: (jnp.maximum(i * hb - 1, 0), 0)),
                  pl.BlockSpec((tt, D), lambda i: (i, 0)),
                  _full((len(WINS), PC, PC)), _full((1, DP)), _full((1, DA)), _full((1, DP)),
                  _resident((D, D)), _full((1, D)), ANY],
        out_specs=[pl.BlockSpec((tt, DP), lambda i: (i, 0)), pl.BlockSpec((tt, DP), lambda i: (i, 0)),
                   pl.BlockSpec((tt, D), lambda i: (i, 0)), pl.BlockSpec((tt, D), lambda i: (i, 0))],
        out_shape=[jax.ShapeDtypeStruct((t, DP), BF), jax.ShapeDtypeStruct((t, DP), F32),
                   jax.ShapeDtypeStruct((t, D), F32), jax.ShapeDtypeStruct((t, D), F32)],
        compiler_params=_cp(),
    )(a, u, u, x, w_pool, pool_scale, g_attn, g_pool, w_out, g_post, dep)


def _ffn_fwd(h1, g_pre, stacks2, g_post, p, w_ple, g_ple, w_pg, tgt):
    t = h1.shape[0]
    tt = 256 if t % 256 == 0 else t

    def body(h1_ref, gpre_ref, wg_ref, wu_ref, wd_ref, gpost_ref, p_ref, wple_ref, gple_ref, wpg_ref, tgt_ref,
             hn_ref, gate_ref, up_ref, dff_ref, dh2_ref, loss_ref, dwpg_ref, dwple_ref, dgple_ref, dgpost_ref):
        i = pl.program_id(0)

        @pl.when(i == 0)
        def _():
            loss_ref[...] = jnp.zeros_like(loss_ref)
            dwpg_ref[...] = jnp.zeros_like(dwpg_ref)
            dwple_ref[...] = jnp.zeros_like(dwple_ref)
            dgple_ref[...] = jnp.zeros_like(dgple_ref)
            dgpost_ref[...] = jnp.zeros_like(dgpost_ref)

        h1v = h1_ref[...]
        hn = (h1v * _rstd(h1v) * gpre_ref[...]).astype(BF)
        hn_ref[...] = hn
        gate = _mm_nt(hn, wg_ref[...].reshape(DFF, D))
        up = _mm_nt(hn, wu_ref[...].reshape(DFF, D))
        for k in range(DFF // FF_CH):
            gate_ref[k] = gate[:, FF_CH * k:FF_CH * (k + 1)].astype(BF)
            up_ref[k] = up[:, FF_CH * k:FF_CH * (k + 1)].astype(BF)
        ff = _mm(gate * _sigmoid(gate) * up, wd_ref[...].reshape(DFF, D))
        rff = _rstd(ff)
        ffh = ff * rff
        gpost = gpost_ref[...]
        h2 = h1v + ffh * gpost
        pv = p_ref[...]
        pe = _mm(pv, wple_ref[...])
        rpe = _rstd(pe)
        peh = pe * rpe
        gple = gple_ref[...]
        e = peh * gple
        sig = _sigmoid(_mm(h2, wpg_ref[...]))
        dv = h2 + sig * e - tgt_ref[...]
        sq = jnp.sum(jnp.sum(dv * dv, axis=1, keepdims=True), axis=0, keepdims=True)
        loss_ref[...] = loss_ref[...] + sq
        dy = dv * (1.0 / D)
        d_e = dy * sig
        d_gl = dy * e * sig * (1.0 - sig)
        dh2 = dy + _mm_nt(d_gl, wpg_ref[...])
        dh2_ref[...] = dh2
        dwpg_ref[...] = dwpg_ref[...] + _mm_tn(h2, d_gl)
        dgple_ref[...] = dgple_ref[...] + jnp.sum(d_e * peh, axis=0, keepdims=True)
        dpeh = d_e * gple
        d_pe = rpe * (dpeh - peh * jnp.mean(dpeh * peh, axis=-1, keepdims=True))
        dwple_ref[...] = dwple_ref[...] + _mm_tn(pv, d_pe)
        dgpost_ref[...] = dgpost_ref[...] + jnp.sum(dh2 * ffh, axis=0, keepdims=True)
        dffh = dh2 * gpost
        dff_ref[...] = (rff * (dffh - ffh * jnp.mean(dffh * ffh, axis=-1, keepdims=True))).astype(BF)

    row = lambda w: pl.BlockSpec((tt, w), lambda i: (i, 0))
    chunked = pl.BlockSpec((DFF // FF_CH, tt, FF_CH), lambda i: (0, i, 0))
    shard_rows = lambda k: pl.BlockSpec((NSHARD, FF_SH, D), lambda i: (0, k, 0), pipeline_mode=pl.Buffered(1))
    return pl.pallas_call(
        body, name="ffn_fwd", grid=(t // tt,),
        in_specs=[row(D), _full((1, D)), shard_rows(0), shard_rows(1), shard_rows(2), _full((1, D)),
                  row(DPLE), _resident((DPLE, D)), _full((1, D)), _resident((D, D)), row(D)],
        out_specs=[row(D), chunked, chunked, row(D), row(D), _full((8, LANE)), _full((D, D)), _full((DPLE, D)),
                   _full((1, D)), _full((1, D))],
        out_shape=[jax.ShapeDtypeStruct((t, D), BF), jax.ShapeDtypeStruct((DFF // FF_CH, t, FF_CH), BF),
                   jax.ShapeDtypeStruct((DFF // FF_CH, t, FF_CH), BF),
                   jax.ShapeDtypeStruct((t, D), BF), jax.ShapeDtypeStruct((t, D), F32), jax.ShapeDtypeStruct((8, LANE), F32),
                   jax.ShapeDtypeStruct((D, D), F32), jax.ShapeDtypeStruct((DPLE, D), F32),
                   jax.ShapeDtypeStruct((1, D), F32), jax.ShapeDtypeStruct((1, D), F32)],
        compiler_params=_cp(),
    )(h1, g_pre, stacks2, stacks2, stacks2, g_post, p, w_ple, g_ple, w_pg, tgt)


FF_CH = 256


def _ffn_bwd(hn2, gate, up, dff, wgu, wd):
    t = hn2.shape[0]
    tt = 1024 if t % 1024 == 0 else _tile(t)
    nt = t // tt
    ch = FF_CH
    nc = DFF // ch

    def body(hn_ref, gate_ref, up_ref, dff_ref, wgu_ref, wd_ref,
             dwg_ref, dwu_ref, dwd_ref, dhn_ref, acc, gu_acc, d_acc, sem):
        j, i = pl.program_id(0), pl.program_id(1)

        @pl.when(j == 0)
        def _():
            acc[pl.ds(pl.multiple_of(i * tt, tt), tt), :] = jnp.zeros((tt, D), F32)

        @pl.when(i == 0)
        def _():
            gu_acc[...] = jnp.zeros_like(gu_acc)
            d_acc[...] = jnp.zeros_like(d_acc)

        parts = 4 if tt % 1024 == 0 else 2
        half = tt // parts
        acts, dgus = [], []
        for hh in range(parts):
            r = slice(hh * half, (hh + 1) * half)
            gate_v = gate_ref[0, r, :].astype(F32)
            up_v = up_ref[0, r, :].astype(F32)
            sg = _sigmoid(gate_v)
            silu = gate_v * sg
            d_act = _mm_nt(dff_ref[r, :], wd_ref[...])
            d_up = (d_act * silu).astype(BF)
            d_gate = (d_act * up_v * (sg * (1.0 + gate_v * (1.0 - sg)))).astype(BF)
            dgu = jnp.concatenate([d_gate, d_up], axis=1)
            rows = pl.ds(pl.multiple_of(i * tt + hh * half, half), half)
            acc[rows, :] = acc[rows, :] + jnp.dot(dgu, wgu_ref[0], preferred_element_type=F32)
            acts.append((silu * up_v).astype(BF))
            dgus.append(dgu)
        d_acc[...] = d_acc[...] + _mm_tn(jnp.concatenate(acts, axis=0), dff_ref[...])
        gu_acc[...] = gu_acc[...] + _mm_tn(jnp.concatenate(dgus, axis=0), hn_ref[...])

        @pl.when(i == nt - 1)
        def _():
            dwg_ref[...] = gu_acc[:ch].astype(BF)
            dwu_ref[...] = gu_acc[ch:].astype(BF)
            dwd_ref[...] = d_acc[...].astype(BF)

        @pl.when((j == nc - 1) & (i == nt - 1))
        def _():
            cp = pltpu.make_async_copy(acc, dhn_ref, sem)
            cp.start()
            cp.wait()

    tok = lambda w: pl.BlockSpec((tt, w), lambda j, i: (i, 0))
    chunk = pl.BlockSpec((ch, D), lambda j, i: (j, 0))
    pair = pl.BlockSpec((1, 2 * ch, D), lambda j, i: (j, 0, 0))
    return pl.pallas_call(
        body, name="ffn_bwd", grid=(nc, nt),
        in_specs=[tok(D), pl.BlockSpec((1, tt, ch), lambda j, i: (j, i, 0)), pl.BlockSpec((1, tt, ch), lambda j, i: (j, i, 0)),
                  tok(D), pair, chunk],
        out_specs=[chunk, chunk, chunk, pl.BlockSpec(memory_space=pl.ANY)],
        out_shape=[jax.ShapeDtypeStruct((DFF, D), BF), jax.ShapeDtypeStruct((DFF, D), BF),
                   jax.ShapeDtypeStruct((DFF, D), BF), jax.ShapeDtypeStruct((t, D), F32)],
        scratch_shapes=[pltpu.VMEM((t, D), F32), pltpu.VMEM((2 * ch, D), F32), pltpu.VMEM((ch, D), F32),
                        pltpu.SemaphoreType.DMA],
        compiler_params=_cp(),
    )(hn2, gate, up, dff, wgu, wd)


def _mix_bwd(d_hn2, dh2, h1, o, a, m, yb, g_ffn_pre, g_post, g_attn, g_pool, w_out, w_pool, pool_scale, dep):
    t = a.shape[0]
    tt = _tile(t)

    def body(dhn_ref, dh2_ref, h1_ref, o_ref, a_ref, m_ref, yb_ref, gfp_ref, go_ref, ga_ref, gp_ref, wo_ref, wp_ref,
             ps_ref, dep_ref, dh1_ref, da_ref, dyc_ref, dgfp_ref, dgo_ref, dga_ref, dgp_ref, dps_ref, dwp_ref, dwo_ref):
        i = pl.program_id(0)

        @pl.when(i == 0)
        def _():
            for r in (dgfp_ref, dgo_ref, dga_ref, dgp_ref, dps_ref, dwp_ref, dwo_ref):
                r[...] = jnp.zeros_like(r)

        d1, dg = _rms_bwd(dhn_ref[...], h1_ref[...], gfp_ref[...])
        dgfp_ref[...] = dgfp_ref[...] + dg
        dh1 = dh2_ref[...] + d1
        dh1_ref[...] = dh1
        d_o, dg = _rms_bwd(dh1, o_ref[...], go_ref[...])
        dgo_ref[...] = dgo_ref[...] + dg
        d_mix = _mm_nt(d_o, wo_ref[...])
        av, mv = a_ref[...], m_ref[...]
        mix = jnp.concatenate([av * _rstd(av) * ga_ref[...], mv * _rstd(mv) * gp_ref[...]], axis=1)
        dwo_ref[...] = dwo_ref[...] + _mm_tn(mix, d_o)
        d_a, dg = _rms_bwd(d_mix[:, :DA], av, ga_ref[...])
        dga_ref[...] = dga_ref[...] + dg
        da_ref[...] = d_a
        d_m, dg = _rms_bwd(d_mix[:, DA:], mv, gp_ref[...])
        dgp_ref[...] = dgp_ref[...] + dg
        tok = i * tt + lax.broadcasted_iota(jnp.int32, (tt, PC), 0)
        dps = []
        for g, w in enumerate(WINS):
            sl = slice(PC * g, PC * (g + 1))
            ybg = yb_ref[:, sl]
            wpg = wp_ref[g].astype(BF)
            mlin = jnp.dot(ybg, wpg, preferred_element_type=F32)
            dmg = d_m[:, sl]
            dps.append(jnp.sum(dmg * mlin, axis=0, keepdims=True))
            dml = (dmg * ps_ref[:, sl]).astype(BF)
            dwp_ref[g] = dwp_ref[g] + _mm_tn(ybg, dml)
            dyc_ref[:, sl] = _mm_nt(dml, wpg) / jnp.minimum(tok + 1, w).astype(F32)
        dps_ref[...] = dps_ref[...] + jnp.concatenate(dps, axis=1)

    row = lambda w: pl.BlockSpec((tt, w), lambda i: (i, 0))
    return pl.pallas_call(
        body, name="mix_bwd", grid=(t // tt,),
        in_specs=[row(D), row(D), row(D), row(D), row(DA), row(DP), row(DP), _full((1, D)), _full((1, D)),
                  _full((1, DA)), _full((1, DP)), _resident((D, D)), _full((len(WINS), PC, PC)), _full((1, DP)), ANY],
        out_specs=[row(D), row(DA), row(DP), _full((1, D)), _full((1, D)), _full((1, DA)), _full((1, DP)),
                   _full((1, DP)), _full((len(WINS), PC, PC)), _full((D, D))],
        out_shape=[jax.ShapeDtypeStruct((t, D), F32), jax.ShapeDtypeStruct((t, DA), F32), jax.ShapeDtypeStruct((t, DP), F32),
                   jax.ShapeDtypeStruct((1, D), F32), jax.ShapeDtypeStruct((1, D), F32), jax.ShapeDtypeStruct((1, DA), F32),
                   jax.ShapeDtypeStruct((1, DP), F32), jax.ShapeDtypeStruct((1, DP), F32),
                   jax.ShapeDtypeStruct((len(WINS), PC, PC), F32), jax.ShapeDtypeStruct((D, D), F32)],
        compiler_params=_cp(),
    )(d_hn2, dh2, h1, o, a, m, yb, g_ffn_pre, g_post, g_attn, g_pool, w_out, w_pool, pool_scale, dep)


def _attn_bwd(qa, ka, v, a, d_a, lse, dep):
    t = qa.shape[0]
    ta = _tile(t)
    n = t // ta

    def body(q_ref, k_ref, v_ref, o_ref, do_ref, lse_ref, dep_ref, dq_ref, dk_ref, dv_ref):
        j = pl.program_id(1)

        @pl.when(j == 0)
        def _():
            dq_ref[...] = jnp.zeros_like(dq_ref)

        dk_ref[...] = jnp.zeros_like(dk_ref)
        dv_ref[...] = jnp.zeros_like(dv_ref)
        ks = [k_ref[:, LANE * h:LANE * (h + 1)] for h in range(2)]
        v2 = v_ref[...]
        lane = lax.broadcasted_iota(jnp.int32, (1, LANE), 1)
        mine = [lane < HD, lane >= HD]

        def tile(i, masked):
            rows = pl.ds(pl.multiple_of(i * ta, ta), ta)
            do2 = do_ref[rows, :]
            prod = do2 * o_ref[rows, :]
            lse2 = lse_ref[rows, :]
            do2b = do2.astype(BF)
            qh = [q_ref[rows, LANE * h:LANE * (h + 1)] for h in range(2)]
            s = [_mm_nt(qh[h], ks[h]) for h in range(2)]
            dp = [_mm_nt(jnp.where(mine[h], do2, 0.0), v2) for h in range(2)]
            delta = [jnp.sum(jnp.where(mine[h], prod, 0.0), axis=1, keepdims=True) for h in range(2)]
            lse_h = [jnp.sum(jnp.where(lane == HD * h, lse2, 0.0), axis=1, keepdims=True) for h in range(2)]
            pr = [jnp.exp(s[h] - lse_h[h]) for h in range(2)]
            if masked:
                keep = (lax.broadcasted_iota(jnp.int32, (ta, ta), 1) <= lax.broadcasted_iota(jnp.int32, (ta, ta), 0))
                pr = [jnp.where(keep, ph, 0.0) for ph in pr]
            ds = [(pr[h] * (dp[h] - delta[h])).astype(BF) for h in range(2)]
            dv_ref[...] = dv_ref[...] + jnp.where(mine[0], _mm_tn(pr[0], do2b), _mm_tn(pr[1], do2b))
            for h in range(2):
                sl = slice(LANE * h, LANE * (h + 1))
                dk_ref[:, sl] = dk_ref[:, sl] + _mm_tn(ds[h], qh[h])
                dq_ref[0, rows, sl] = dq_ref[0, rows, sl] + jnp.dot(ds[h], ks[h], preferred_element_type=F32)

        def step(i, carry):
            tile(i, False)
            return carry

        tile(j, True)
        lax.fori_loop(j + 1, n, step, 0)

    qrow = lambda w: pl.BlockSpec((t, w), lambda p, j: (0, p))
    krow = lambda w: pl.BlockSpec((ta, w), lambda p, j: (j, p))
    return pl.pallas_call(
        body, name="attn_bwd", grid=(NH // 2, n),
        in_specs=[qrow(2 * LANE), krow(2 * LANE), krow(LANE), qrow(LANE), qrow(LANE), qrow(LANE), ANY],
        out_specs=[pl.BlockSpec((1, t, 2 * LANE), lambda p, j: (p, 0, 0)), krow(2 * LANE), krow(LANE)],
        out_shape=[jax.ShapeDtypeStruct((NH // 2, t, 2 * LANE), F32), jax.ShapeDtypeStruct((t, NH * LANE), F32),
                   jax.ShapeDtypeStruct((t, DA), F32)],
        compiler_params=_cp(),
    )(qa, ka, v, a, d_a, lse, dep)


def _in_bwd(dqa, dka, dv, dyc, fl, x, dh1, g1, w_in_t, tri_u, dep):
    t = x.shape[0]
    tt = _tile(t)
    nt = t // tt
    hb = tt // HALO
    rev = lambda s: nt - 1 - s

    def body(dqa_ref, dka_ref, dv_ref, dyc_ref, dyn_ref, fl_ref, x_ref, dh1_ref, g_ref, w_ref, tri_ref,
             dep_ref, dx_ref, dw_ref, dg_ref, db_ref, carry, acc, stage, sem):
        s = pl.program_id(0)
        i = nt - 1 - s

        @pl.when(s == 0)
        def _():
            carry[...] = jnp.zeros_like(carry)
            acc[...] = jnp.zeros_like(acc)
            dg_ref[...] = jnp.zeros_like(dg_ref)
            db_ref[...] = jnp.zeros_like(db_ref)

        dq_cat = jnp.concatenate([dqa_ref[p] for p in range(NH // 2)], axis=1)
        dk_cat = dka_ref[...]
        off = lax.broadcasted_iota(jnp.int32, (1, NH * LANE), 1)
        off = off % LANE - HD - off // LANE

        def picked(cat, lane_off):
            kept = jnp.where(off == lane_off, cat, 0.0)
            return functools.reduce(lambda a, b: a + b, [kept[:, LANE * h:LANE * (h + 1)] for h in range(NH)])

        dc = pltpu.roll(picked(dq_cat, 0), LANE - HD, 1) - pltpu.roll(picked(dk_cat, 24), LANE - HD - 24, 1)
        dlf = carry[...] + _dot3(tri_ref[...], dc)
        carry[...] = carry[...] + jnp.sum(dc, axis=0, keepdims=True)
        flv = fl_ref[...]
        lane = lax.broadcasted_iota(jnp.int32, flv.shape, 1)
        d_fl = jnp.where(lane < NH, dlf / (1.0 + jnp.exp(flv)), 0.0)
        db_ref[...] = db_ref[...] + jnp.sum(d_fl, axis=0, keepdims=True)
        low = lax.broadcasted_iota(jnp.int32, (tt, LANE), 1) < HD
        dqs, dks = [], []
        for p in range(NH // 2):
            b0, b1 = slice(2 * LANE * p, 2 * LANE * p + LANE), slice(2 * LANE * p + LANE, 2 * LANE * (p + 1))
            dqs.append(jnp.where(low, dq_cat[:, b0], pltpu.roll(dq_cat[:, b1], HD, 1)) * (1.0 / math.sqrt(HD)))
            dks.append(jnp.where(low, dk_cat[:, b0], pltpu.roll(dk_cat[:, b1], HD, 1)))
        nxt = dyn_ref[...] * jnp.where(i < nt - 1, 1.0, 0.0)
        tok = i * tt + lax.broadcasted_iota(jnp.int32, (tt, PC), 0)
        dus = []
        for g, w in enumerate(WINS):
            sl = slice(PC * g, PC * (g + 1))
            dycg = dyc_ref[:, sl]
            ext = jnp.concatenate([dycg, nxt[:, sl]], axis=0)
            dus.append(_window_sum(ext, w, False)[:tt] - dycg * jnp.minimum(tok + 1, w).astype(F32))
        d_z = jnp.concatenate(dqs + dks + [dv_ref[...], d_fl] + dus, axis=1).astype(BF)
        xv = x_ref[...]
        gv = g_ref[...]
        hn = (xv * _rstd(xv) * gv).astype(BF)
        d_hn = jnp.dot(d_z, w_ref[...], preferred_element_type=F32)
        acc[...] = acc[...] + _mm_tn(d_z, hn)
        d1, dg = _rms_bwd(d_hn, xv, gv)
        dg_ref[...] = dg_ref[...] + dg
        dx_ref[...] = dh1_ref[...] + d1

        @pl.when(s == nt - 1)
        def _():
            stage[...] = acc[...].astype(BF)
            cp = pltpu.make_async_copy(stage, dw_ref, sem)
            cp.start()
            cp.wait()

    row = lambda w: pl.BlockSpec((tt, w), lambda s: (rev(s), 0))
    return pl.pallas_call(
        body, name="in_bwd", grid=(nt,),
        in_specs=[pl.BlockSpec((NH // 2, tt, 2 * LANE), lambda s: (0, rev(s), 0)), row(NH * LANE), row(DA), row(DP),
                  pl.BlockSpec((HALO, DP), lambda s: (jnp.minimum((rev(s) + 1) * hb, nt * hb - 1), 0)),
                  row(LANE), row(D), row(D), _full((1, D)), _resident((ZW, D)), _full((tt, tt)),
                  ANY],
        out_specs=[row(D), pl.BlockSpec(memory_space=pl.ANY), _full((1, D)), _full((1, LANE))],
        out_shape=[jax.ShapeDtypeStruct((t, D), F32), jax.ShapeDtypeStruct((ZW, D), BF),
                   jax.ShapeDtypeStruct((1, D), F32), jax.ShapeDtypeStruct((1, LANE), F32)],
        scratch_shapes=[pltpu.VMEM((1, LANE), F32), pltpu.VMEM((ZW, D), F32), pltpu.VMEM((ZW, D), BF),
                        pltpu.SemaphoreType.DMA],
        compiler_params=_cp(),
    )(dqa, dka, dv, dyc, dyc, fl, x, dh1, g1, w_in_t, tri_u, dep)


class _NoComm:
    def __init__(self, w2):
        self.w2 = w2
        self.dep = jnp.zeros((8, LANE), F32)

    def after_attention(self, after):
        return self.dep

    def weights2(self, after):
        return _stack2_full(*self.w2)

    def after_ffn(self, grads2):
        self.grads2 = grads2
        return self.dep

    def after_mix(self, after, early):
        self.early = early
        return self.dep

    def after_attn(self, after):
        return self.dep


def _local_step(x, p, tgt, sm, w1, comm):
    w_in_t, w_out = w1
    tt = _tile(x.shape[0])
    eq, ek, rowq, rowk = _aug_consts()
    b_pad = jnp.pad(sm["b_forget"], ((0, 0), (0, LANE - NH)))
    qa, ka, v, u, fl = _in_proj(x, sm["g_mix_pre"], w_in_t, b_pad, _tri(tt, False), eq, ek, rowq, rowk, comm.dep)
    a, lse = _attn_fwd(qa, ka, v)
    yb, m, o, h1 = _mix_out(a, u, x, sm["w_pool"], sm["pool_scale"], sm["g_attn_grp"],
                            sm["g_pool_grp"], w_out, sm["g_mix_post"], comm.after_attention(a))
    stacks2 = comm.weights2(h1)
    wg_t, wu_t, wd, w_ple, w_pg = _unstack2_full(stacks2)
    hn2, gate, up, dff, dh2, loss, dwpg, dwple, dgple, dgfpost = _ffn_fwd(
        h1, sm["g_ffn_pre"], stacks2, sm["g_ffn_post"], p, w_ple, sm["g_ple"], w_pg, tgt)
    chunks = lambda a: a.reshape(DFF // FF_CH, FF_CH, D)
    dwg_t, dwu_t, dwd, d_hn2 = _ffn_bwd(hn2, gate, up, dff, jnp.concatenate([chunks(wg_t), chunks(wu_t)], axis=1), wd)
    dep = comm.after_ffn((dwg_t, dwu_t, dwd, dwple, dwpg))
    dh1, d_a, dyc, dgfpre, dgpost, dgattn, dgpool, dps, dwpool, dwout = _mix_bwd(
        d_hn2, dh2, h1, o, a, m, yb, sm["g_ffn_pre"], sm["g_mix_post"], sm["g_attn_grp"], sm["g_pool_grp"],
        w_out, sm["w_pool"], sm["pool_scale"], dep)
    early = dict(loss=loss[0:1, 0:1], g_attn_grp=dgattn, g_pool_grp=dgpool, w_pool=dwpool, pool_scale=dps,
                 g_mix_post=dgpost, g_ffn_pre=dgfpre, g_ffn_post=dgfpost, g_ple=dgple)
    dqa, dka, dvv = _attn_bwd(qa, ka, v, a, d_a, lse, comm.after_mix(dh1, early))
    dx, dwin_t, dg1, dbf = _in_bwd(dqa, dka, dvv, dyc, fl, x, dh1, sm["g_mix_pre"], w_in_t, _tri(tt, True),
                                   comm.after_attn(dvv))
    return dx, (dwin_t, dwout), dict(g_mix_pre=dg1, b_forget=dbf[:, :NH])


def _place():
    x, y, c = lax.axis_index("x"), lax.axis_index("y"), lax.axis_index("c")
    return x, y, c, [(1 - x, y), (x, 1 - y), (1 - x, 1 - y)]


def _rows(c, h):
    return pl.ds(pl.multiple_of(c * h, 16), h)


def _plan_gather(h):
    def plan(src, land):
        x, y, c, chips = _place()
        return [(src.at[_rows(c, h), :], land.at[2 * x + y, _rows(c, h), :], (cx, cy, c),
                 land.at[2 * cx + cy, _rows(c, h), :]) for cx, cy in chips]
    return plan


def _plan_forward(h):
    def plan(land_in, own, land):
        x, y, c, chips = _place()
        sib, me = (x, y, 1 - c), 2 * x + y
        return ([(land_in.at[2 * cx + cy, _rows(c, h), :], land.at[2 * cx + cy, _rows(c, h), :], sib,
                  land.at[2 * cx + cy, _rows(1 - c, h), :]) for cx, cy in chips]
                + [(own, land.at[me], sib, land.at[me])])
    return plan


def _plan_swap_halves(h):
    def plan(buf_in, buf):
        x, y, c, _ = _place()
        return [(buf_in.at[_rows(c, h), :], buf.at[_rows(c, h), :], (x, y, 1 - c), buf.at[_rows(1 - c, h), :])]
    return plan


def _plan_pair_rows(h):
    def plan(src, land):
        x, y, c, _ = _place()
        return [(src.at[:, _rows(1 - c, h), :], land, (x, y, 1 - c), land)]
    return plan


def _plan_scatter(src, land):
    x, y, c, chips = _place()
    return [(src.at[2 * cx + cy], land.at[k], (cx, cy, c), land.at[k]) for k, (cx, cy) in enumerate(chips)]


def _plan_scatter8(h):
    def plan(src, land):
        x, y, c, chips = _place()
        copies = [(src.at[2 * x + y, _rows(1 - c, h), :], land.at[0], (x, y, 1 - c), land.at[0])]
        for k, (cx, cy) in enumerate(chips):
            for d, other in enumerate((c, 1 - c)):
                copies.append((src.at[2 * cx + cy, _rows(other, h), :], land.at[1 + 2 * k + c], (cx, cy, other),
                               land.at[1 + 2 * k + other]))
        return copies
    return plan


def _plan_all(src, land):
    x, y, c, _ = _place()
    copies = []
    for r in range(1, 8):
        px, py, pc = (1 - a if b else a for a, b in zip((x, y, c), (r >> 2 & 1, r >> 1 & 1, r & 1)))
        copies.append((src, land.at[4 * x + 2 * y + c], (px, py, pc), land.at[4 * px + 2 * py + pc]))
    return copies


def _remote(src, dst, send_sems, recv_sems, k, peer):
    return pltpu.make_async_remote_copy(src_ref=src, dst_ref=dst, send_sem=send_sems.at[k], recv_sem=recv_sems.at[k],
                                        device_id=peer, device_id_type=MESH)


def _exchange(name, n, src, land, plan):
    def body(src_ref, land_ref, send_sems, recv_sems):
        copies = plan(src_ref, land_ref)
        for k, (s, d, peer, _) in enumerate(copies):
            _remote(s, d, send_sems, recv_sems, k, peer).start()
        for k, (s, _, peer, mine) in enumerate(copies):
            _remote(s, mine, send_sems, recv_sems, k, peer).wait_recv()
        for k, (s, d, peer, _) in enumerate(copies):
            _remote(s, d, send_sems, recv_sems, k, peer).wait_send()

    return pl.pallas_call(
        body, name=name, in_specs=[ANY], out_specs=ANY, out_shape=land,
        scratch_shapes=[pltpu.SemaphoreType.DMA((n,)), pltpu.SemaphoreType.DMA((n,))],
    )(src)


def _exchange_inplace(name, n, buf, extra, plan):
    def body(*refs):
        ins, buf_ref, send_sems, recv_sems = refs[:1 + len(extra)], refs[1 + len(extra)], refs[-2], refs[-1]
        copies = plan(*ins, buf_ref)
        for k, (s, d, peer, _) in enumerate(copies):
            _remote(s, d, send_sems, recv_sems, k, peer).start()
        for k, (s, _, peer, mine) in enumerate(copies):
            _remote(s, mine, send_sems, recv_sems, k, peer).wait_recv()
        for k, (s, d, peer, _) in enumerate(copies):
            _remote(s, d, send_sems, recv_sems, k, peer).wait_send()

    return pl.pallas_call(
        body, name=name, in_specs=[ANY] * (1 + len(extra)), out_specs=ANY, out_shape=_sds(buf.shape, buf.dtype),
        input_output_aliases={0: 0},
        scratch_shapes=[pltpu.SemaphoreType.DMA((n,)), pltpu.SemaphoreType.DMA((n,))],
    )(buf, *extra)


HBM = pl.BlockSpec(memory_space=pltpu.HBM)
SEM = pl.BlockSpec(memory_space=pltpu.SEMAPHORE)
EFFECT = pltpu.SideEffectType.DATAFLOW_SIDE_EFFECTING


def _exchange_start(name, n, src, land, plan):
    def body(src_ref, land_ref, send_sems, recv_sems, src_thru, land_thru, token):
        for k, (s, d, peer, _) in enumerate(plan(src_ref, land_ref)):
            _remote(s, d, send_sems, recv_sems, k, peer).start()
        token[...] = jnp.zeros_like(token)

    return pl.pallas_call(
        body, name=name,
        out_shape=(pltpu.SemaphoreType.DMA((n,)), pltpu.SemaphoreType.DMA((n,)), pltpu.HBM(src.shape, src.dtype),
                   pltpu.HBM(land.shape, land.dtype), jax.ShapeDtypeStruct((8, LANE), F32)),
        in_specs=(HBM, HBM), out_specs=(SEM, SEM, HBM, HBM, pl.BlockSpec(memory_space=pltpu.VMEM)),
        input_output_aliases={0: 2, 1: 3},
        compiler_params=pltpu.CompilerParams(has_side_effects=EFFECT),
    )(pltpu.with_memory_space_constraint(src, pltpu.HBM), pltpu.with_memory_space_constraint(land, pltpu.HBM))


def _exchange_wait(name, started, plan, after):
    send_sems, recv_sems, src, land, _ = started

    def body(src_ref, land_ref, send_sems, recv_sems, after_ref, src_out, land_out):
        for k, (s, _, peer, mine) in enumerate(plan(src_ref, land_ref)):
            cp = _remote(s, mine, send_sems, recv_sems, k, peer)
            cp.wait_send()
            cp.wait_recv()

    return pl.pallas_call(
        body, name=name, out_shape=(pltpu.HBM(src.shape, src.dtype), pltpu.HBM(land.shape, land.dtype)),
        in_specs=(HBM, HBM, SEM, SEM, ANY), out_specs=(HBM, HBM), input_output_aliases={0: 0, 1: 1},
        compiler_params=pltpu.CompilerParams(has_side_effects=EFFECT),
    )(src, land, send_sems, recv_sems, after)


def _pair_sum(name, cidx, g, recv, br):
    h = recv.shape[1]
    nb = h // br

    def body(c_ref, g_ref, r_ref, out_ref):
        out_ref[...] = (g_ref[...].astype(F32) + r_ref[...].astype(F32)).astype(BF)

    return pl.pallas_call(
        body, name=name,
        grid_spec=pltpu.PrefetchScalarGridSpec(
            num_scalar_prefetch=1, grid=(NSHARD, nb),
            in_specs=[pl.BlockSpec((1, br, D), lambda s, i, c: (s, c[0] * nb + i, 0)),
                      pl.BlockSpec((1, br, D), lambda s, i, c: (s, i, 0))],
            out_specs=pl.BlockSpec((1, br, D), lambda s, i, c: (s, i, 0))),
        out_shape=jax.ShapeDtypeStruct((NSHARD, h, D), BF),
    )(cidx, g, recv)


def _chip_sum(name, place, pb, y, br):
    h = y.shape[1]
    nb = h // br

    def body(pl_ref, p_ref, y_ref, out_ref):
        acc = p_ref[0].astype(F32)
        for k in range(NSHARD - 1):
            acc = acc + y_ref[k].astype(F32)
        out_ref[...] = acc

    return pl.pallas_call(
        body, name=name,
        grid_spec=pltpu.PrefetchScalarGridSpec(
            num_scalar_prefetch=1, grid=(nb,),
            in_specs=[pl.BlockSpec((1, br, D), lambda i, s: (s[0], i, 0)),
                      pl.BlockSpec((NSHARD - 1, br, D), lambda i, s: (0, i, 0))],
            out_specs=pl.BlockSpec((br, D), lambda i, s: (s[1] * nb + i, 0))),
        out_shape=jax.ShapeDtypeStruct((2 * h, D), F32),
    )(place, pb, y)


def _sum8(name, place, g, land, br):
    h = land.shape[1]
    nb = h // br

    def body(pl_ref, g_ref, y_ref, out_ref):
        acc = g_ref[0].astype(F32)
        for k in range(land.shape[0]):
            acc = acc + y_ref[k].astype(F32)
        out_ref[...] = acc

    return pl.pallas_call(
        body, name=name,
        grid_spec=pltpu.PrefetchScalarGridSpec(
            num_scalar_prefetch=1, grid=(nb,),
            in_specs=[pl.BlockSpec((1, br, D), lambda i, s: (s[0], s[1] * nb + i, 0)),
                      pl.BlockSpec((land.shape[0], br, D), lambda i, s: (0, i, 0))],
            out_specs=pl.BlockSpec((br, D), lambda i, s: (s[1] * nb + i, 0))),
        out_shape=jax.ShapeDtypeStruct((2 * h, D), F32), compiler_params=_cp(),
    )(place, g, land)


def _sum_slots(name, v):
    def body(in_ref, out_ref):
        acc = in_ref[0]
        for k in range(1, 8):
            acc = acc + in_ref[k]
        out_ref[...] = acc

    vm = pl.BlockSpec(memory_space=pltpu.VMEM)
    return pl.pallas_call(body, name=name, in_specs=[vm], out_specs=vm,
                          out_shape=jax.ShapeDtypeStruct(v.shape[1:], F32))(v)


def _adamw_math(w, g, m, v):
    m = ADAM_B1 * m + (1.0 - ADAM_B1) * g
    v = ADAM_B2 * v + (1.0 - ADAM_B2) * (g * g)
    m_hat = m / (1.0 - ADAM_B1 ** ADAM_STEP)
    v_hat = v / (1.0 - ADAM_B2 ** ADAM_STEP)
    delta = -ADAM_LR * (m_hat / (jnp.sqrt(v_hat) + ADAM_EPS) + ADAM_WD * w)
    return delta, m, v


def _adamw(w, g, m, v, dep, row0=None):
    r, c = w.shape
    br = next(b for b in (352, 256, 128, r) if r % b == 0 and (row0 or 0) % b == 0)
    first = (row0 or 0) // br

    def body(w_ref, g_ref, m_ref, v_ref, dep_ref, *outs):
        gv = g_ref[...]
        outs[-3][...], outs[-2][...], outs[-1][...] = _adamw_math(w_ref[...], gv, m_ref[...], v_ref[...])
        if row0 is not None:
            outs[0][...] = gv

    spec = pl.BlockSpec((br, c), lambda i: (i, 0))
    n_out = 3 if row0 is None else 4
    out = pl.pallas_call(
        body, name="adamw", grid=(r // br,),
        in_specs=[spec, pl.BlockSpec((br, c), lambda i: (first + i, 0)), spec, spec, ANY], out_specs=[spec] * n_out,
        out_shape=[jax.ShapeDtypeStruct((r, c), F32)] * n_out, compiler_params=_cp(),
    )(w, g, m, v, dep)
    return out if row0 is not None else [g] + list(out)


def _adamw_small(ws, gs, ms, vs):
    n = len(ws)

    def body(*refs):
        ins, outs = refs[:4 * n], refs[4 * n:]
        for k in range(n):
            d, m, v = _adamw_math(ins[k][...], ins[n + k][...], ins[2 * n + k][...], ins[3 * n + k][...])
            outs[k][...] = d
            outs[n + k][...] = m
            outs[2 * n + k][...] = v

    vm = pl.BlockSpec(memory_space=pltpu.VMEM)
    out = pl.pallas_call(
        body, name="adamw_small", in_specs=[vm] * (4 * n), out_specs=[vm] * (3 * n),
        out_shape=[jax.ShapeDtypeStruct(w.shape, F32) for w in ws] * 3,
    )(*ws, *gs, *ms, *vs)
    return out[:n], out[n:2 * n], out[2 * n:]


BIG = ("w_in", "w_out", "w_ffn_gate", "w_ffn_up", "w_ffn_down", "w_ple_proj", "w_ple_gate")
SMALL = ("g_mix_pre", "b_forget", "g_attn_grp", "g_pool_grp", "w_pool", "pool_scale", "g_mix_post", "g_ffn_pre",
         "g_ffn_post", "g_ple")
TRANSPOSED = ("w_in", "w_ffn_gate", "w_ffn_up")
VECTORS = tuple(n for n in SMALL if n != "w_pool")
ORDER = ("g_mix_pre", "w_in", "b_forget", "g_attn_grp", "g_pool_grp", "w_pool", "pool_scale", "w_out", "g_mix_post",
         "g_ffn_pre", "w_ffn_gate", "w_ffn_up", "w_ffn_down", "g_ffn_post", "w_ple_proj", "g_ple", "w_ple_gate")


def _pad_rows(a, rows):
    return jnp.pad(a, ((0, rows - a.shape[0]), (0, 0)))


def _stack1(w_in, w_out):
    return _pad_rows(jnp.concatenate([_pad_rows(w_in.T, IN_PAD), w_out], axis=0), ROWS1)


def _stack2(wg, wu, wd, wple, wpg):
    return _pad_rows(jnp.concatenate([wg.T, wu.T, wd, wple.reshape(DPLE // NSHARD, D), wpg], axis=0), ROWS2)


def _unstack1(s):
    return s[:IN_SH], s[O1_OUT:USED1]


def _cat(g, lo, hi):
    return g[:, lo:hi].reshape(NSHARD * (hi - lo), D)


def _unstack1_full(g):
    w_in_t = _cat(g, 0, IN_SH)
    w_in_t = jnp.concatenate([w_in_t[:3 * DA], _pad_rows(w_in_t[3 * DA:3 * DA + NH], LANE), w_in_t[3 * DA + NH:]], axis=0)
    return w_in_t, _cat(g, O1_OUT, USED1)


def _unstack2_full(g):
    w_ple = g[:, O2_PLE:O2_PG].reshape(NSHARD, DPLE, DPLE).transpose(1, 0, 2).reshape(DPLE, D)
    return _cat(g, 0, O2_U), _cat(g, O2_U, O2_D), _cat(g, O2_D, O2_PLE), w_ple, _cat(g, O2_PG, USED2)


def _shards(a):
    return a.reshape(NSHARD, a.shape[0] // NSHARD, D)


def _stack1_full(dwin_t, dwout):
    dwin_t = jnp.concatenate([dwin_t[:3 * DA + NH], dwin_t[3 * DA + LANE:]], axis=0).reshape(NSHARD, IN_SH, D)
    zeros = lambda r: jnp.zeros((NSHARD, r, D), dwin_t.dtype)
    return jnp.concatenate([dwin_t, zeros(IN_PAD - IN_SH), _shards(dwout), zeros(ROWS1 - USED1)], axis=1)


def _stack2_full(dwg_t, dwu_t, dwd, dwple, dwpg):
    dwple = dwple.reshape(DPLE, NSHARD, DPLE).transpose(1, 0, 2).reshape(NSHARD, DPLE // NSHARD, D)
    return jnp.concatenate([_shards(dwg_t), _shards(dwu_t), _shards(dwd), dwple, _shards(dwpg),
                            jnp.zeros((NSHARD, ROWS2 - USED2, D), dwd.dtype)], axis=1)


def _sds(shape, dtype):
    return jax.ShapeDtypeStruct(shape, dtype)


class _Comm:
    def __init__(self, stack2, me, c):
        self.stack2, self.me, self.c = stack2, me, c
        self.cidx = c.astype(jnp.int32).reshape(1)
        self.place = jnp.stack([me, c]).astype(jnp.int32)
        self.h = ROWS2 // 2
        self.gather = _exchange_start("gather2_start", 3, stack2, lax.empty((NSHARD, ROWS2, D), BF), _plan_gather(self.h))
        self.dep = self.gather[4]

    def after_attention(self, after):
        own, land = _exchange_wait("gather2_wait", self.gather, _plan_gather(self.h), after)
        fwd = _plan_forward(self.h)
        self.forward = lambda own_ref, land_ref: fwd(land_ref, own_ref, land_ref)
        self.passing = _exchange_start("forward2_start", 4, own, land, self.forward)
        return self.passing[4]

    def weights2(self, after):
        return _exchange_wait("forward2_wait", self.passing, self.forward, after)[1]

    def after_ffn(self, grads2):
        g = _stack2_full(*[a.astype(BF) for a in grads2])
        self.scatter = _plan_scatter8(self.h)
        self.chip = _exchange_start("reduce2_start", 7, g, lax.empty((7, self.h, D), BF), self.scatter)
        return self.chip[4]

    def after_mix(self, after, early):
        self.early_shapes = {n: early[n].shape for n in early}
        self.small = self.start_small("small", early)
        return self.small[4]

    def after_attn(self, after):
        g, y = _exchange_wait("reduce2_wait", self.chip, self.scatter, after)
        f = _sum8("sum2", self.place, g, y, RED2 // 2)
        self.early = self.finish_small("small", self.small, self.early_shapes, after)
        swap = _plan_swap_halves(self.h)
        self.swap = lambda _, buf: swap(buf, buf)
        self.swapping = _exchange_start("reduce2_gather_start", 1, self.dep, f, self.swap)
        return self.swapping[4]

    def reduced2(self, after):
        return _exchange_wait("reduce2_gather_wait", self.swapping, self.swap, after)[1]

    def start_small(self, name, small):
        v = _pack_small(small)
        return _exchange_start(name + "_start", 7, v, lax.empty((8,) + v.shape, F32), _plan_all)

    def finish_small(self, name, started, shapes, after):
        v, land = _exchange_wait(name + "_wait", started, _plan_all, after)
        land = lax.dynamic_update_slice(land, v[None], (2 * self.me + self.c, 0, 0))
        return _unpack_small(_sum_slots(name + "_sum", land), shapes)


def _pack_small(small):
    parts = []
    for name in small:
        flat = small[name].reshape(-1)
        parts.append(jnp.pad(flat, (0, -flat.shape[0] % LANE)).reshape(-1, LANE))
    v = jnp.concatenate(parts, axis=0)
    return _pad_rows(v, v.shape[0] + (-v.shape[0] % 8))


def _unpack_small(v, shapes):
    out, r = {}, 0
    for name in shapes:
        n = math.prod(shapes[name])
        rows = -(-n // LANE)
        out[name] = v[r:r + rows].reshape(-1)[:n].reshape(shapes[name])
        r += rows
    return out


def kernel(x, p, g_mix_pre, w_in, b_forget, g_attn_grp, g_pool_grp, w_pool, pool_scale, w_out, g_mix_post, g_ffn_pre, w_ffn_gate, w_ffn_up, w_ffn_down, g_ffn_post, w_ple_proj, g_ple, w_ple_gate, loss_target, m_g_mix_pre, m_w_in, m_b_forget, m_g_attn_grp, m_g_pool_grp, m_w_pool, m_pool_scale, m_w_out, m_g_mix_post, m_g_ffn_pre, m_w_ffn_gate, m_w_ffn_up, m_w_ffn_down, m_g_ffn_post, m_w_ple_proj, m_g_ple, m_w_ple_gate, v_g_mix_pre, v_w_in, v_b_forget, v_g_attn_grp, v_g_pool_grp, v_w_pool, v_pool_scale, v_w_out, v_g_mix_post, v_g_ffn_pre, v_w_ffn_gate, v_w_ffn_up, v_w_ffn_down, v_g_ffn_post, v_w_ple_proj, v_g_ple, v_w_ple_gate):
    args = dict(locals())
    strip = lambda n, a: a if n in VECTORS else a[0]
    w = {n: strip(n, args[n]) for n in ORDER}
    mom = {n: strip(n, args["m_" + n]) for n in ORDER}
    var = {n: strip(n, args["v_" + n]) for n in ORDER}
    sm = {n: w[n] for n in SMALL}

    c = lax.axis_index("c")
    me = 2 * lax.axis_index("x") + lax.axis_index("y")
    h1 = ROWS1 // 2
    bf = lambda n: w[n].astype(BF)
    stack1 = _stack1(bf("w_in"), bf("w_out"))
    stack2 = _stack2(*[bf(n) for n in BIG[2:]])
    land = _exchange("gather1", 3, stack1, _sds((NSHARD, ROWS1, D), BF), _plan_gather(h1))
    land, stack2 = lax.optimization_barrier((land, stack2))
    comm = _Comm(stack2, me, c)
    w1 = _unstack1_full(_exchange_inplace("gather1_forward", 4, land, (stack1,), _plan_forward(h1)))
    dx, grads1, late = _local_step(x[0], p[0, 0], loss_target[0], sm, w1, comm)

    flip = lambda n, a: a.T if n in TRANSPOSED else a
    grads, delta, new_m, new_v = {}, {}, {}, {}

    def update(n, g, dep, row0=None):
        g_, d_, m_, v_ = _adamw(flip(n, w[n]), g, flip(n, mom[n]), flip(n, var[n]), dep, row0)
        grads[n], delta[n], new_m[n], new_v[n] = flip(n, g_), flip(n, d_), flip(n, m_), flip(n, v_)
        return v_

    late_shapes = {n: late[n].shape for n in late}
    small2 = comm.start_small("small2", late)
    red2 = comm.reduced2(dx)
    g1 = _stack1_full(grads1[0], grads1[1].astype(BF))
    pair1 = _exchange_start("reduce1_pair_start", 1, g1, lax.empty((NSHARD, h1, D), BF), _plan_pair_rows(h1))
    dep = update("w_ple_gate", red2, pair1[4] + small2[4], O2_PG)
    dep = update("w_ple_proj", red2[O2_PLE:O2_PG].reshape(DPLE, DPLE), dep)
    g1, recv = _exchange_wait("reduce1_pair_wait", pair1, _plan_pair_rows(h1), dep)
    pb = _pair_sum("pair_sum1", comm.cidx, g1, recv, RED1)
    chip1 = _exchange_start("reduce1_chip_start", 3, pb, lax.empty((NSHARD - 1, h1, D), BF), _plan_scatter)
    dep = update("w_ffn_gate", red2, chip1[4], 0)
    dep = update("w_ffn_up", red2, dep, O2_U)
    dep = update("w_ffn_down", red2, dep, O2_D)
    red_small = {**comm.early, **comm.finish_small("small2", small2, late_shapes, dep)}
    loss = 0.5 / D * red_small["loss"][0, 0]
    for n in SMALL:
        grads[n] = red_small[n].reshape(w[n].shape)
    two_d = lambda a: a.reshape(-1, a.shape[-1])
    ds, ms, vs = _adamw_small([two_d(w[n]) for n in SMALL], [two_d(grads[n]) for n in SMALL],
                              [two_d(mom[n]) for n in SMALL], [two_d(var[n]) for n in SMALL])
    for k, n in enumerate(SMALL):
        delta[n], new_m[n], new_v[n] = ds[k].reshape(w[n].shape), ms[k].reshape(w[n].shape), vs[k].reshape(w[n].shape)
    pb, y = _exchange_wait("reduce1_chip_wait", chip1, _plan_scatter, vs[0])
    f = _chip_sum("chip_sum1", comm.place, pb, y, RED1)
    reduced1 = _exchange_inplace("reduce1_gather", 1, f, (), _plan_swap_halves(h1))
    g_in, g_out = _unstack1(reduced1)
    update("w_out", g_out, update("w_in", g_in, reduced1))

    lead = lambda d: [d[n] if n in VECTORS else d[n][None] for n in ORDER]
    return (loss, dx[None], *lead(grads), *lead(delta), *lead(new_m), *lead(new_v))
```

```python
import functools
import math

import jax
import jax.numpy as jnp
import numpy as np
from jax import lax
from jax.experimental import pallas as pl
from jax.experimental.pallas import tpu as pltpu

F32 = jnp.float32
BF = jnp.bfloat16
MESH = pl.DeviceIdType.MESH

D = 1024
DA = 512
DP = 512
NH = 8
HD = 64
DFF = 2816
DPLE = 256
WINS = (2, 4, 8, 16)
PC = 128
ZW = 3 * DA + 128 + DP
EPS = 1e-6
NSHARD = 4

LANE = 128
HALO = 128

IN_SH = 514
IN_PAD = 528
FF_SH = DFF // NSHARD
O1_OUT, USED1, ROWS1 = 528, 784, 800
O2_U, O2_D, O2_PLE, O2_PG, USED2, ROWS2 = 704, 1408, 2112, 2176, 2432, 2560
RED1, RED2 = 400, 640

ADAM_LR, ADAM_B1, ADAM_B2, ADAM_EPS, ADAM_WD, ADAM_STEP = 0.001, 0.9, 0.999, 1e-8, 0.01, 10

VMEM_LIMIT = 56 * 1024 * 1024


def _cp(**kw):
    return pltpu.CompilerParams(vmem_limit_bytes=VMEM_LIMIT, **kw)


def _mm(a, b):
    return jnp.dot(a.astype(BF), b.astype(BF), preferred_element_type=F32)


def _mm_nt(a, b):
    return lax.dot_general(a.astype(BF), b.astype(BF), (((1,), (1,)), ((), ())), preferred_element_type=F32)


def _mm_tn(a, b):
    return lax.dot_general(a.astype(BF), b.astype(BF), (((0,), (0,)), ((), ())), preferred_element_type=F32)


def _split3(x):
    hi = x.astype(BF)
    r = x - hi.astype(F32)
    mid = r.astype(BF)
    lo = (r - mid.astype(F32)).astype(BF)
    return hi, mid, lo


def _dot3(m, x):
    hi, mid, lo = _split3(x)
    return (jnp.dot(m, hi, preferred_element_type=F32) + jnp.dot(m, mid, preferred_element_type=F32)
            + jnp.dot(m, lo, preferred_element_type=F32))


def _window_sum(ext, w, back):
    n = ext.shape[0]
    s, k = ext, 1
    while k < w:
        s = s + pltpu.roll(s, k if back else n - k, 0)
        k *= 2
    return s


def _rstd(x):
    return lax.rsqrt(jnp.mean(x * x, axis=-1, keepdims=True) + EPS)


def _rms_bwd(dy, x, g):
    r = _rstd(x)
    xh = x * r
    dg = jnp.sum(dy * xh, axis=0, keepdims=True)
    dxh = dy * g
    dx = r * (dxh - xh * jnp.mean(dxh * xh, axis=-1, keepdims=True))
    return dx, dg


def _sigmoid(x):
    return 1.0 / (1.0 + jnp.exp(-x))


ANY = pl.BlockSpec(memory_space=pl.ANY)


def _full(shape):
    n = len(shape)
    return pl.BlockSpec(shape, lambda *_: (0,) * n)


def _resident(shape):
    n = len(shape)
    return pl.BlockSpec(shape, lambda *_: (0,) * n, pipeline_mode=pl.Buffered(1))


def _tile(t):
    return 512 if t % 512 == 0 else t


def _tri(n, upper):
    r, c = np.indices((n, n))
    return ((c >= r) if upper else (c <= r)).astype(BF)


def _aug_consts():
    row, col = np.indices((LANE, NH * LANE))
    piece, head = row // NH, row % NH
    ch, cl = col // LANE, col % LANE
    eq = ((piece < 3) & (head == ch) & (cl == HD + 8 * piece + head)).astype(BF)
    ek = -((piece < 3) & (head == ch) & (cl == HD + 24 + 8 * piece + head)).astype(BF)
    off = (np.arange(NH * LANE) % LANE - HD - np.arange(NH * LANE) // LANE)[None, :]
    rowq = ((off >= 24) & (off < 48) & (off % 8 == 0)).astype(np.float32)
    rowk = ((off >= 0) & (off < 24) & (off % 8 == 0)).astype(np.float32)
    return eq, ek, rowq, rowk


def _in_proj(x, g1, w_in_t, b_pad, tri, eq, ek, rowq, rowk, dep):
    t = x.shape[0]
    tt = _tile(t)

    def body(x_ref, g_ref, w_ref, b_ref, tri_ref, eq_ref, ek_ref, rq_ref, rk_ref, dep_ref,
             qa_ref, ka_ref, v_ref, u_ref, fl_ref, carry):
        i = pl.program_id(0)

        @pl.when(i == 0)
        def _():
            carry[...] = jnp.zeros_like(carry)

        xv = x_ref[...]
        hn = (xv * _rstd(xv) * g_ref[...]).astype(BF)
        z = _mm_nt(hn, w_ref[...])
        fl = z[:, 3 * DA:3 * DA + LANE] + b_ref[...]
        lane = lax.broadcasted_iota(jnp.int32, fl.shape, 1)
        lf = jnp.where(lane < NH, jnp.minimum(fl, 0.0) - jnp.log(1.0 + jnp.exp(-jnp.abs(fl))), 0.0)
        c = carry[...] + _dot3(tri_ref[...], lf)
        carry[...] = carry[...] + jnp.sum(lf, axis=0, keepdims=True)
        hi, mid, lo = _split3(c)
        caug = (hi.astype(F32) + pltpu.roll(mid.astype(F32), NH, 1) + pltpu.roll(lo.astype(F32), 2 * NH, 1)).astype(BF)
        aug_q = jnp.dot(caug, eq_ref[...], preferred_element_type=F32) + rq_ref[...]
        aug_k = jnp.dot(caug, ek_ref[...], preferred_element_type=F32) + rk_ref[...]
        low = lax.broadcasted_iota(jnp.int32, (tt, LANE), 1) < HD
        for p in range(NH // 2):
            qp = z[:, LANE * p:LANE * (p + 1)] * (1.0 / math.sqrt(HD))
            kp = z[:, DA + LANE * p:DA + LANE * (p + 1)]
            for h, (qh, kh) in enumerate(((qp, kp), (pltpu.roll(qp, HD, 1), pltpu.roll(kp, HD, 1)))):
                lo_, hi_ = LANE * (2 * p + h), LANE * (2 * p + h + 1)
                qa_ref[:, lo_:hi_] = jnp.where(low, qh, aug_q[:, lo_:hi_]).astype(BF)
                ka_ref[:, lo_:hi_] = jnp.where(low, kh, aug_k[:, lo_:hi_]).astype(BF)
        v_ref[...] = z[:, 2 * DA:3 * DA].astype(BF)
        u_ref[...] = z[:, 3 * DA + LANE:]
        fl_ref[...] = fl

    return pl.pallas_call(
        body, name="in_proj", grid=(t // tt,),
        in_specs=[pl.BlockSpec((tt, D), lambda i: (i, 0)), _full((1, D)), _resident((ZW, D)), _full((1, LANE)),
                  _full((tt, tt)), _full((LANE, NH * LANE)), _full((LANE, NH * LANE)),
                  _full((1, NH * LANE)), _full((1, NH * LANE)), ANY],
        out_specs=[pl.BlockSpec((tt, NH * LANE), lambda i: (i, 0)), pl.BlockSpec((tt, NH * LANE), lambda i: (i, 0)),
                   pl.BlockSpec((tt, DA), lambda i: (i, 0)), pl.BlockSpec((tt, DP), lambda i: (i, 0)),
                   pl.BlockSpec((tt, LANE), lambda i: (i, 0))],
        out_shape=[jax.ShapeDtypeStruct((t, NH * LANE), BF), jax.ShapeDtypeStruct((t, NH * LANE), BF),
                   jax.ShapeDtypeStruct((t, DA), BF), jax.ShapeDtypeStruct((t, DP), F32),
                   jax.ShapeDtypeStruct((t, LANE), F32)],
        scratch_shapes=[pltpu.VMEM((1, LANE), F32)],
        compiler_params=_cp(),
    )(x, g1, w_in_t, b_pad, tri, eq, ek, rowq, rowk, dep)


def _attn_fwd(qa, ka, v):
    t = qa.shape[0]
    ta = _tile(t)
    n = t // ta

    def body(q_ref, k_ref, v_ref, a_ref, lse_ref, m_ref, l_ref, acc_ref):
        i = pl.program_id(1)
        m_ref[...] = jnp.full_like(m_ref, -1e30)
        l_ref[...] = jnp.zeros_like(l_ref)
        acc_ref[...] = jnp.zeros_like(acc_ref)
        qs = [q_ref[:, LANE * h:LANE * (h + 1)] for h in range(2)]
        reps = ta // LANE

        def tile(j, width, masked):
            rows = pl.ds(pl.multiple_of(j * ta, ta), width * ta)
            v2 = v_ref[rows, :]
            s = [_mm_nt(qs[h], k_ref[rows, LANE * h:LANE * (h + 1)]) for h in range(2)]
            if masked:
                keep = (lax.broadcasted_iota(jnp.int32, (ta, ta), 1) <= lax.broadcasted_iota(jnp.int32, (ta, ta), 0))
                s = [jnp.where(keep, sh, -1e30) for sh in s]
            m_old = [m_ref[h] for h in range(2)]
            m_new = [jnp.maximum(m_old[h], jnp.max(s[h], axis=1, keepdims=True)) for h in range(2)]
            pe = [jnp.exp(s[h] - jnp.tile(m_new[h], (1, width * reps))) for h in range(2)]
            alpha = [jnp.exp(m_old[h] - m_new[h]) for h in range(2)]
            pv = [jnp.dot(pe[h].astype(BF), v2, preferred_element_type=F32) for h in range(2)]
            for h in range(2):
                l_ref[h] = alpha[h] * l_ref[h] + jnp.sum(pe[h], axis=1, keepdims=True)
                acc_ref[h] = alpha[h] * acc_ref[h] + pv[h]
                m_ref[h] = m_new[h]

        def step(jj, carry):
            tile(2 * jj, 2, False)
            return carry

        lax.fori_loop(0, i // 2, step, 0)

        @pl.when(i % 2 == 1)
        def _():
            tile(i - 1, 1, False)

        tile(i, 1, True)
        low = lax.broadcasted_iota(jnp.int32, (ta, LANE), 1) < HD
        a_ref[...] = jnp.where(low, acc_ref[0] / l_ref[0], acc_ref[1] / l_ref[1])
        lse_ref[...] = jnp.where(low, m_ref[0] + jnp.log(l_ref[0]), m_ref[1] + jnp.log(l_ref[1]))

    return pl.pallas_call(
        body, name="attn_fwd", grid=(NH // 2, n),
        in_specs=[pl.BlockSpec((ta, 2 * LANE), lambda p, i: (i, p)),
                  pl.BlockSpec((t, 2 * LANE), lambda p, i: (0, p)),
                  pl.BlockSpec((t, LANE), lambda p, i: (0, p))],
        out_specs=[pl.BlockSpec((ta, LANE), lambda p, i: (i, p)), pl.BlockSpec((ta, LANE), lambda p, i: (i, p))],
        out_shape=[jax.ShapeDtypeStruct((t, DA), F32), jax.ShapeDtypeStruct((t, DA), F32)],
        scratch_shapes=[pltpu.VMEM((2, ta, LANE), F32), pltpu.VMEM((2, ta, LANE), F32), pltpu.VMEM((2, ta, LANE), F32)],
        compiler_params=_cp(),
    )(qa, ka, v)


def _mix_out(a, u, x, w_pool, pool_scale, g_attn, g_pool, w_out, g_post, dep):
    t = a.shape[0]
    tt = _tile(t)
    hb = tt // HALO

    def body(a_ref, u_ref, up_ref, x_ref, wp_ref, ps_ref, ga_ref, gp_ref, wo_ref, go_ref, dep_ref,
             yb_ref, m_ref, o_ref, h1_ref):
        i = pl.program_id(0)
        prev = up_ref[...] * jnp.where(i > 0, 1.0, 0.0)
        tok = i * tt + lax.broadcasted_iota(jnp.int32, (tt, PC), 0)
        ms = []
        for g, w in enumerate(WINS):
            ug = u_ref[:, PC * g:PC * (g + 1)]
            ext = jnp.concatenate([prev[:, PC * g:PC * (g + 1)], ug], axis=0)
            cnt = jnp.minimum(tok + 1, w).astype(F32)
            y = (_window_sum(ext, w, True)[HALO:] / cnt - ug).astype(BF)
            yb_ref[:, PC * g:PC * (g + 1)] = y
            ms.append(_mm(y, wp_ref[g]) * ps_ref[:, PC * g:PC * (g + 1)])
        m = jnp.concatenate(ms, axis=1)
        m_ref[...] = m
        av = a_ref[...]
        mix = jnp.concatenate([av * _rstd(av) * ga_ref[...], m * _rstd(m) * gp_ref[...]], axis=1)
        o = _mm(mix, wo_ref[...])
        o_ref[...] = o
        h1_ref[...] = x_ref[...] + o * _rstd(o) * go_ref[...]

    return pl.pallas_call(
        body, name="mix_out", grid=(t // tt,),
        in_specs=[pl.BlockSpec((tt, DA), lambda i: (i, 0)), pl.BlockSpec((tt, DP), lambda i: (i, 0)),
                  pl.BlockSpec((HALO, DP), lambda i: (jnp.maximum(i * hb - 1, 0), 0)),
                  pl.BlockSpec((tt, D), lambda i: (i, 0)),
                  _full((len(WINS), PC, PC)), _full((1, DP)), _full((1, DA)), _full((1, DP)),
                  _resident((D, D)), _full((1, D)), ANY],
        out_specs=[pl.BlockSpec((tt, DP), lambda i: (i, 0)), pl.BlockSpec((tt, DP), lambda i: (i, 0)),
                   pl.BlockSpec((tt, D), lambda i: (i, 0)), pl.BlockSpec((tt, D), lambda i: (i, 0))],
        out_shape=[jax.ShapeDtypeStruct((t, DP), BF), jax.ShapeDtypeStruct((t, DP), F32),
                   jax.ShapeDtypeStruct((t, D), F32), jax.ShapeDtypeStruct((t, D), F32)],
        compiler_params=_cp(),
    )(a, u, u, x, w_pool, pool_scale, g_attn, g_pool, w_out, g_post, dep)


def _ffn_fwd(h1, g_pre, stacks2, g_post, p, w_ple, g_ple, w_pg, tgt):
    t = h1.shape[0]
    tt = 256 if t % 256 == 0 else t

    def body(h1_ref, gpre_ref, wg_ref, wu_ref, wd_ref, gpost_ref, p_ref, wple_ref, gple_ref, wpg_ref, tgt_ref,
             hn_ref, gate_ref, up_ref, dff_ref, dh2_ref, loss_ref, dwpg_ref, dwple_ref, dgple_ref, dgpost_ref):
        i = pl.program_id(0)

        @pl.when(i == 0)
        def _():
            loss_ref[...] = jnp.zeros_like(loss_ref)
            dwpg_ref[...] = jnp.zeros_like(dwpg_ref)
            dwple_ref[...] = jnp.zeros_like(dwple_ref)
            dgple_ref[...] = jnp.zeros_like(dgple_ref)
            dgpost_ref[...] = jnp.zeros_like(dgpost_ref)

        h1v = h1_ref[...]
        hn = (h1v * _rstd(h1v) * gpre_ref[...]).astype(BF)
        hn_ref[...] = hn
        gate = _mm_nt(hn, wg_ref[...].reshape(DFF, D))
        up = _mm_nt(hn, wu_ref[...].reshape(DFF, D))
        for k in range(DFF // FF_CH):
            gate_ref[k] = gate[:, FF_CH * k:FF_CH * (k + 1)].astype(BF)
            up_ref[k] = up[:, FF_CH * k:FF_CH * (k + 1)].astype(BF)
        ff = _mm(gate * _sigmoid(gate) * up, wd_ref[...].reshape(DFF, D))
        rff = _rstd(ff)
        ffh = ff * rff
        gpost = gpost_ref[...]
        h2 = h1v + ffh * gpost
        pv = p_ref[...]
        pe = _mm(pv, wple_ref[...])
        rpe = _rstd(pe)
        peh = pe * rpe
        gple = gple_ref[...]
        e = peh * gple
        sig = _sigmoid(_mm(h2, wpg_ref[...]))
        dv = h2 + sig * e - tgt_ref[...]
        sq = jnp.sum(jnp.sum(dv * dv, axis=1, keepdims=True), axis=0, keepdims=True)
        loss_ref[...] = loss_ref[...] + sq
        dy = dv * (1.0 / D)
        d_e = dy * sig
        d_gl = dy * e * sig * (1.0 - sig)
        dh2 = dy + _mm_nt(d_gl, wpg_ref[...])
        dh2_ref[...] = dh2
        dwpg_ref[...] = dwpg_ref[...] + _mm_tn(h2, d_gl)
        dgple_ref[...] = dgple_ref[...] + jnp.sum(d_e * peh, axis=0, keepdims=True)
        dpeh = d_e * gple
        d_pe = rpe * (dpeh - peh * jnp.mean(dpeh * peh, axis=-1, keepdims=True))
        dwple_ref[...] = dwple_ref[...] + _mm_tn(pv, d_pe)
        dgpost_ref[...] = dgpost_ref[...] + jnp.sum(dh2 * ffh, axis=0, keepdims=True)
        dffh = dh2 * gpost
        dff_ref[...] = (rff * (dffh - ffh * jnp.mean(dffh * ffh, axis=-1, keepdims=True))).astype(BF)

    row = lambda w: pl.BlockSpec((tt, w), lambda i: (i, 0))
    chunked = pl.BlockSpec((DFF // FF_CH, tt, FF_CH), lambda i: (0, i, 0))
    shard_rows = lambda k: pl.BlockSpec((NSHARD, FF_SH, D), lambda i: (0, k, 0), pipeline_mode=pl.Buffered(1))
    return pl.pallas_call(
        body, name="ffn_fwd", grid=(t // tt,),
        in_specs=[row(D), _full((1, D)), shard_rows(0), shard_rows(1), shard_rows(2), _full((1, D)),
                  row(DPLE), _resident((DPLE, D)), _full((1, D)), _resident((D, D)), row(D)],
        out_specs=[row(D), chunked, chunked, row(D), row(D), _full((8, LANE)), _full((D, D)), _full((DPLE, D)),
                   _full((1, D)), _full((1, D))],
        out_shape=[jax.ShapeDtypeStruct((t, D), BF), jax.ShapeDtypeStruct((DFF // FF_CH, t, FF_CH), BF),
                   jax.ShapeDtypeStruct((DFF // FF_CH, t, FF_CH), BF),
                   jax.ShapeDtypeStruct((t, D), BF), jax.ShapeDtypeStruct((t, D), F32), jax.ShapeDtypeStruct((8, LANE), F32),
                   jax.ShapeDtypeStruct((D, D), F32), jax.ShapeDtypeStruct((DPLE, D), F32),
                   jax.ShapeDtypeStruct((1, D), F32), jax.ShapeDtypeStruct((1, D), F32)],
        compiler_params=_cp(),
    )(h1, g_pre, stacks2, stacks2, stacks2, g_post, p, w_ple, g_ple, w_pg, tgt)


FF_CH = 256


def _ffn_bwd(hn2, gate, up, dff, wgu, wd):
    t = hn2.shape[0]
    tt = 1024 if t % 1024 == 0 else _tile(t)
    nt = t // tt
    ch = FF_CH
    nc = DFF // ch

    def body(hn_ref, gate_ref, up_ref, dff_ref, wgu_ref, wd_ref,
             dwg_ref, dwu_ref, dwd_ref, dhn_ref, acc, gu_acc, d_acc, sem):
        j, i = pl.program_id(0), pl.program_id(1)

        @pl.when(j == 0)
        def _():
            acc[pl.ds(pl.multiple_of(i * tt, tt), tt), :] = jnp.zeros((tt, D), F32)

        @pl.when(i == 0)
        def _():
            gu_acc[...] = jnp.zeros_like(gu_acc)
            d_acc[...] = jnp.zeros_like(d_acc)

        parts = 4 if tt % 1024 == 0 else 2
        half = tt // parts
        acts, dgus = [], []
        for hh in range(parts):
            r = slice(hh * half, (hh + 1) * half)
            gate_v = gate_ref[0, r, :].astype(F32)
            up_v = up_ref[0, r, :].astype(F32)
            sg = _sigmoid(gate_v)
            silu = gate_v * sg
            d_act = _mm_nt(dff_ref[r, :], wd_ref[...])
            d_up = (d_act * silu).astype(BF)
            d_gate = (d_act * up_v * (sg * (1.0 + gate_v * (1.0 - sg)))).astype(BF)
            dgu = jnp.concatenate([d_gate, d_up], axis=1)
            rows = pl.ds(pl.multiple_of(i * tt + hh * half, half), half)
            acc[rows, :] = acc[rows, :] + jnp.dot(dgu, wgu_ref[0], preferred_element_type=F32)
            acts.append((silu * up_v).astype(BF))
            dgus.append(dgu)
        d_acc[...] = d_acc[...] + _mm_tn(jnp.concatenate(acts, axis=0), dff_ref[...])
        gu_acc[...] = gu_acc[...] + _mm_tn(jnp.concatenate(dgus, axis=0), hn_ref[...])

        @pl.when(i == nt - 1)
        def _():
            dwg_ref[...] = gu_acc[:ch].astype(BF)
            dwu_ref[...] = gu_acc[ch:].astype(BF)
            dwd_ref[...] = d_acc[...].astype(BF)

        @pl.when((j == nc - 1) & (i == nt - 1))
        def _():
            cp = pltpu.make_async_copy(acc, dhn_ref, sem)
            cp.start()
            cp.wait()

    tok = lambda w: pl.BlockSpec((tt, w), lambda j, i: (i, 0))
    chunk = pl.BlockSpec((ch, D), lambda j, i: (j, 0))
    pair = pl.BlockSpec((1, 2 * ch, D), lambda j, i: (j, 0, 0))
    return pl.pallas_call(
        body, name="ffn_bwd", grid=(nc, nt),
        in_specs=[tok(D), pl.BlockSpec((1, tt, ch), lambda j, i: (j, i, 0)), pl.BlockSpec((1, tt, ch), lambda j, i: (j, i, 0)),
                  tok(D), pair, chunk],
        out_specs=[chunk, chunk, chunk, pl.BlockSpec(memory_space=pl.ANY)],
        out_shape=[jax.ShapeDtypeStruct((DFF, D), BF), jax.ShapeDtypeStruct((DFF, D), BF),
                   jax.ShapeDtypeStruct((DFF, D), BF), jax.ShapeDtypeStruct((t, D), F32)],
        scratch_shapes=[pltpu.VMEM((t, D), F32), pltpu.VMEM((2 * ch, D), F32), pltpu.VMEM((ch, D), F32),
                        pltpu.SemaphoreType.DMA],
        compiler_params=_cp(),
    )(hn2, gate, up, dff, wgu, wd)


def _mix_bwd(d_hn2, dh2, h1, o, a, m, yb, g_ffn_pre, g_post, g_attn, g_pool, w_out, w_pool, pool_scale, dep):
    t = a.shape[0]
    tt = _tile(t)

    def body(dhn_ref, dh2_ref, h1_ref, o_ref, a_ref, m_ref, yb_ref, gfp_ref, go_ref, ga_ref, gp_ref, wo_ref, wp_ref,
             ps_ref, dep_ref, dh1_ref, da_ref, dyc_ref, dgfp_ref, dgo_ref, dga_ref, dgp_ref, dps_ref, dwp_ref, dwo_ref):
        i = pl.program_id(0)

        @pl.when(i == 0)
        def _():
            for r in (dgfp_ref, dgo_ref, dga_ref, dgp_ref, dps_ref, dwp_ref, dwo_ref):
                r[...] = jnp.zeros_like(r)

        d1, dg = _rms_bwd(dhn_ref[...], h1_ref[...], gfp_ref[...])
        dgfp_ref[...] = dgfp_ref[...] + dg
        dh1 = dh2_ref[...] + d1
        dh1_ref[...] = dh1
        d_o, dg = _rms_bwd(dh1, o_ref[...], go_ref[...])
        dgo_ref[...] = dgo_ref[...] + dg
        d_mix = _mm_nt(d_o, wo_ref[...])
        av, mv = a_ref[...], m_ref[...]
        mix = jnp.concatenate([av * _rstd(av) * ga_ref[...], mv * _rstd(mv) * gp_ref[...]], axis=1)
        dwo_ref[...] = dwo_ref[...] + _mm_tn(mix, d_o)
        d_a, dg = _rms_bwd(d_mix[:, :DA], av, ga_ref[...])
        dga_ref[...] = dga_ref[...] + dg
        da_ref[...] = d_a
        d_m, dg = _rms_bwd(d_mix[:, DA:], mv, gp_ref[...])
        dgp_ref[...] = dgp_ref[...] + dg
        tok = i * tt + lax.broadcasted_iota(jnp.int32, (tt, PC), 0)
        dps = []
        for g, w in enumerate(WINS):
            sl = slice(PC * g, PC * (g + 1))
            ybg = yb_ref[:, sl]
            wpg = wp_ref[g].astype(BF)
            mlin = jnp.dot(ybg, wpg, preferred_element_type=F32)
            dmg = d_m[:, sl]
            dps.append(jnp.sum(dmg * mlin, axis=0, keepdims=True))
            dml = (dmg * ps_ref[:, sl]).astype(BF)
            dwp_ref[g] = dwp_ref[g] + _mm_tn(ybg, dml)
            dyc_ref[:, sl] = _mm_nt(dml, wpg) / jnp.minimum(tok + 1, w).astype(F32)
        dps_ref[...] = dps_ref[...] + jnp.concatenate(dps, axis=1)

    row = lambda w: pl.BlockSpec((tt, w), lambda i: (i, 0))
    return pl.pallas_call(
        body, name="mix_bwd", grid=(t // tt,),
        in_specs=[row(D), row(D), row(D), row(D), row(DA), row(DP), row(DP), _full((1, D)), _full((1, D)),
                  _full((1, DA)), _full((1, DP)), _resident((D, D)), _full((len(WINS), PC, PC)), _full((1, DP)), ANY],
        out_specs=[row(D), row(DA), row(DP), _full((1, D)), _full((1, D)), _full((1, DA)), _full((1, DP)),
                   _full((1, DP)), _full((len(WINS), PC, PC)), _full((D, D))],
        out_shape=[jax.ShapeDtypeStruct((t, D), F32), jax.ShapeDtypeStruct((t, DA), F32), jax.ShapeDtypeStruct((t, DP), F32),
                   jax.ShapeDtypeStruct((1, D), F32), jax.ShapeDtypeStruct((1, D), F32), jax.ShapeDtypeStruct((1, DA), F32),
                   jax.ShapeDtypeStruct((1, DP), F32), jax.ShapeDtypeStruct((1, DP), F32),
                   jax.ShapeDtypeStruct((len(WINS), PC, PC), F32), jax.ShapeDtypeStruct((D, D), F32)],
        compiler_params=_cp(),
    )(d_hn2, dh2, h1, o, a, m, yb, g_ffn_pre, g_post, g_attn, g_pool, w_out, w_pool, pool_scale, dep)


def _attn_bwd(qa, ka, v, a, d_a, lse, dep):
    t = qa.shape[0]
    ta = _tile(t)
    n = t // ta

    def body(q_ref, k_ref, v_ref, o_ref, do_ref, lse_ref, dep_ref, dq_ref, dk_ref, dv_ref):
        j = pl.program_id(1)

        @pl.when(j == 0)
        def _():
            dq_ref[...] = jnp.zeros_like(dq_ref)

        dk_ref[...] = jnp.zeros_like(dk_ref)
        dv_ref[...] = jnp.zeros_like(dv_ref)
        ks = [k_ref[:, LANE * h:LANE * (h + 1)] for h in range(2)]
        v2 = v_ref[...]
        lane = lax.broadcasted_iota(jnp.int32, (1, LANE), 1)
        mine = [lane < HD, lane >= HD]

        def block(q0, nq, nk, shift):
            rows = pl.ds(pl.multiple_of(q0, nq), nq)
            do2 = do_ref[rows, :]
            prod = do2 * o_ref[rows, :]
            lse2 = lse_ref[rows, :]
            do2b = do2.astype(BF)
            qh = [q_ref[rows, LANE * h:LANE * (h + 1)] for h in range(2)]
            kk, vv = [kh[:nk] for kh in ks], v2[:nk]
            s = [_mm_nt(qh[h], kk[h]) for h in range(2)]
            dp = [_mm_nt(jnp.where(mine[h], do2, 0.0), vv) for h in range(2)]
            delta = [jnp.sum(jnp.where(mine[h], prod, 0.0), axis=1, keepdims=True) for h in range(2)]
            lse_h = [jnp.sum(jnp.where(lane == HD * h, lse2, 0.0), axis=1, keepdims=True) for h in range(2)]
            pr = [jnp.exp(s[h] - lse_h[h]) for h in range(2)]
            if shift is not None:
                keep = (lax.broadcasted_iota(jnp.int32, (nq, nk), 1)
                        <= lax.broadcasted_iota(jnp.int32, (nq, nk), 0) + shift)
                pr = [jnp.where(keep, ph, 0.0) for ph in pr]
            ds = [(pr[h] * (dp[h] - delta[h])).astype(BF) for h in range(2)]
            dv_ref[:nk, :] = dv_ref[:nk, :] + jnp.where(mine[0], _mm_tn(pr[0], do2b), _mm_tn(pr[1], do2b))
            for h in range(2):
                sl = slice(LANE * h, LANE * (h + 1))
                dk_ref[:nk, sl] = dk_ref[:nk, sl] + _mm_tn(ds[h], qh[h])
                dq_ref[0, rows, sl] = dq_ref[0, rows, sl] + jnp.dot(ds[h], kk[h], preferred_element_type=F32)

        def step(i, carry):
            block(i * ta, ta, ta, None)
            return carry

        half = ta // 2
        block(j * ta, half, half, 0)
        block(j * ta + half, half, ta, half)
        lax.fori_loop(j + 1, n, step, 0)

    qrow = lambda w: pl.BlockSpec((t, w), lambda p, j: (0, p))
    krow = lambda w: pl.BlockSpec((ta, w), lambda p, j: (j, p))
    return pl.pallas_call(
        body, name="attn_bwd", grid=(NH // 2, n),
        in_specs=[qrow(2 * LANE), krow(2 * LANE), krow(LANE), qrow(LANE), qrow(LANE), qrow(LANE), ANY],
        out_specs=[pl.BlockSpec((1, t, 2 * LANE), lambda p, j: (p, 0, 0)), krow(2 * LANE), krow(LANE)],
        out_shape=[jax.ShapeDtypeStruct((NH // 2, t, 2 * LANE), F32), jax.ShapeDtypeStruct((t, NH * LANE), F32),
                   jax.ShapeDtypeStruct((t, DA), F32)],
        compiler_params=_cp(),
    )(qa, ka, v, a, d_a, lse, dep)


def _in_bwd(dqa, dka, dv, dyc, fl, x, dh1, g1, w_in_t, tri_u, dep):
    t = x.shape[0]
    tt = _tile(t)
    nt = t // tt
    hb = tt // HALO
    rev = lambda s: nt - 1 - s

    def body(dqa_ref, dka_ref, dv_ref, dyc_ref, dyn_ref, fl_ref, x_ref, dh1_ref, g_ref, w_ref, tri_ref,
             dep_ref, dx_ref, dw_ref, dg_ref, db_ref, carry, acc, stage, sem):
        s = pl.program_id(0)
        i = nt - 1 - s

        @pl.when(s == 0)
        def _():
            carry[...] = jnp.zeros_like(carry)
            acc[...] = jnp.zeros_like(acc)
            dg_ref[...] = jnp.zeros_like(dg_ref)
            db_ref[...] = jnp.zeros_like(db_ref)

        dq_cat = jnp.concatenate([dqa_ref[p] for p in range(NH // 2)], axis=1)
        dk_cat = dka_ref[...]
        off = lax.broadcasted_iota(jnp.int32, (1, NH * LANE), 1)
        off = off % LANE - HD - off // LANE

        def picked(cat, lane_off):
            kept = jnp.where(off == lane_off, cat, 0.0)
            return functools.reduce(lambda a, b: a + b, [kept[:, LANE * h:LANE * (h + 1)] for h in range(NH)])

        dc = pltpu.roll(picked(dq_cat, 0), LANE - HD, 1) - pltpu.roll(picked(dk_cat, 24), LANE - HD - 24, 1)
        dlf = carry[...] + _dot3(tri_ref[...], dc)
        carry[...] = carry[...] + jnp.sum(dc, axis=0, keepdims=True)
        flv = fl_ref[...]
        lane = lax.broadcasted_iota(jnp.int32, flv.shape, 1)
        d_fl = jnp.where(lane < NH, dlf / (1.0 + jnp.exp(flv)), 0.0)
        db_ref[...] = db_ref[...] + jnp.sum(d_fl, axis=0, keepdims=True)
        low = lax.broadcasted_iota(jnp.int32, (tt, LANE), 1) < HD
        dqs, dks = [], []
        for p in range(NH // 2):
            b0, b1 = slice(2 * LANE * p, 2 * LANE * p + LANE), slice(2 * LANE * p + LANE, 2 * LANE * (p + 1))
            dqs.append(jnp.where(low, dq_cat[:, b0], pltpu.roll(dq_cat[:, b1], HD, 1)) * (1.0 / math.sqrt(HD)))
            dks.append(jnp.where(low, dk_cat[:, b0], pltpu.roll(dk_cat[:, b1], HD, 1)))
        nxt = dyn_ref[...] * jnp.where(i < nt - 1, 1.0, 0.0)
        tok = i * tt + lax.broadcasted_iota(jnp.int32, (tt, PC), 0)
        dus = []
        for g, w in enumerate(WINS):
            sl = slice(PC * g, PC * (g + 1))
            dycg = dyc_ref[:, sl]
            ext = jnp.concatenate([dycg, nxt[:, sl]], axis=0)
            dus.append(_window_sum(ext, w, False)[:tt] - dycg * jnp.minimum(tok + 1, w).astype(F32))
        d_z = jnp.concatenate(dqs + dks + [dv_ref[...], d_fl] + dus, axis=1).astype(BF)
        xv = x_ref[...]
        gv = g_ref[...]
        hn = (xv * _rstd(xv) * gv).astype(BF)
        d_hn = jnp.dot(d_z, w_ref[...], preferred_element_type=F32)
        acc[...] = acc[...] + _mm_tn(d_z, hn)
        d1, dg = _rms_bwd(d_hn, xv, gv)
        dg_ref[...] = dg_ref[...] + dg
        dx_ref[...] = dh1_ref[...] + d1

        @pl.when(s == nt - 1)
        def _():
            stage[...] = acc[...].astype(BF)
            cp = pltpu.make_async_copy(stage, dw_ref, sem)
            cp.start()
            cp.wait()

    row = lambda w: pl.BlockSpec((tt, w), lambda s: (rev(s), 0))
    return pl.pallas_call(
        body, name="in_bwd", grid=(nt,),
        in_specs=[pl.BlockSpec((NH // 2, tt, 2 * LANE), lambda s: (0, rev(s), 0)), row(NH * LANE), row(DA), row(DP),
                  pl.BlockSpec((HALO, DP), lambda s: (jnp.minimum((rev(s) + 1) * hb, nt * hb - 1), 0)),
                  row(LANE), row(D), row(D), _full((1, D)), _resident((ZW, D)), _full((tt, tt)),
                  ANY],
        out_specs=[row(D), pl.BlockSpec(memory_space=pl.ANY), _full((1, D)), _full((1, LANE))],
        out_shape=[jax.ShapeDtypeStruct((t, D), F32), jax.ShapeDtypeStruct((ZW, D), BF),
                   jax.ShapeDtypeStruct((1, D), F32), jax.ShapeDtypeStruct((1, LANE), F32)],
        scratch_shapes=[pltpu.VMEM((1, LANE), F32), pltpu.VMEM((ZW, D), F32), pltpu.VMEM((ZW, D), BF),
                        pltpu.SemaphoreType.DMA],
        compiler_params=_cp(),
    )(dqa, dka, dv, dyc, dyc, fl, x, dh1, g1, w_in_t, tri_u, dep)


class _NoComm:
    def __init__(self, w2):
        self.w2 = w2
        self.dep = jnp.zeros((8, LANE), F32)

    def after_attention(self, after):
        return self.dep

    def weights2(self, after):
        return _stack2_full(*self.w2)

    def after_ffn(self, grads2):
        self.grads2 = grads2
        return self.dep

    def after_mix(self, after, early):
        self.early = early
        return self.dep

    def after_attn(self, after):
        return self.dep


def _local_step(x, p, tgt, sm, w1, comm):
    w_in_t, w_out = w1
    tt = _tile(x.shape[0])
    eq, ek, rowq, rowk = _aug_consts()
    b_pad = jnp.pad(sm["b_forget"], ((0, 0), (0, LANE - NH)))
    qa, ka, v, u, fl = _in_proj(x, sm["g_mix_pre"], w_in_t, b_pad, _tri(tt, False), eq, ek, rowq, rowk, comm.dep)
    a, lse = _attn_fwd(qa, ka, v)
    yb, m, o, h1 = _mix_out(a, u, x, sm["w_pool"], sm["pool_scale"], sm["g_attn_grp"],
                            sm["g_pool_grp"], w_out, sm["g_mix_post"], comm.after_attention(a))
    stacks2 = comm.weights2(h1)
    wg_t, wu_t, wd, w_ple, w_pg = _unstack2_full(stacks2)
    hn2, gate, up, dff, dh2, loss, dwpg, dwple, dgple, dgfpost = _ffn_fwd(
        h1, sm["g_ffn_pre"], stacks2, sm["g_ffn_post"], p, w_ple, sm["g_ple"], w_pg, tgt)
    chunks = lambda a: a.reshape(DFF // FF_CH, FF_CH, D)
    dwg_t, dwu_t, dwd, d_hn2 = _ffn_bwd(hn2, gate, up, dff, jnp.concatenate([chunks(wg_t), chunks(wu_t)], axis=1), wd)
    dep = comm.after_ffn((dwg_t, dwu_t, dwd, dwple, dwpg))
    dh1, d_a, dyc, dgfpre, dgpost, dgattn, dgpool, dps, dwpool, dwout = _mix_bwd(
        d_hn2, dh2, h1, o, a, m, yb, sm["g_ffn_pre"], sm["g_mix_post"], sm["g_attn_grp"], sm["g_pool_grp"],
        w_out, sm["w_pool"], sm["pool_scale"], dep)
    early = dict(loss=loss[0:1, 0:1], g_attn_grp=dgattn, g_pool_grp=dgpool, w_pool=dwpool, pool_scale=dps,
                 g_mix_post=dgpost, g_ffn_pre=dgfpre, g_ffn_post=dgfpost, g_ple=dgple)
    dqa, dka, dvv = _attn_bwd(qa, ka, v, a, d_a, lse, comm.after_mix(dh1, early))
    dx, dwin_t, dg1, dbf = _in_bwd(dqa, dka, dvv, dyc, fl, x, dh1, sm["g_mix_pre"], w_in_t, _tri(tt, True),
                                   comm.after_attn(dvv))
    return dx, (dwin_t, dwout), dict(g_mix_pre=dg1, b_forget=dbf[:, :NH])


def _place():
    x, y, c = lax.axis_index("x"), lax.axis_index("y"), lax.axis_index("c")
    return x, y, c, [(1 - x, y), (x, 1 - y), (1 - x, 1 - y)]


def _rows(c, h):
    return pl.ds(pl.multiple_of(c * h, 16), h)


def _plan_gather(h):
    def plan(src, land):
        x, y, c, chips = _place()
        return [(src.at[_rows(c, h), :], land.at[2 * x + y, _rows(c, h), :], (cx, cy, c),
                 land.at[2 * cx + cy, _rows(c, h), :]) for cx, cy in chips]
    return plan


def _plan_forward(h):
    def plan(land_in, own, land):
        x, y, c, chips = _place()
        sib, me = (x, y, 1 - c), 2 * x + y
        return ([(land_in.at[2 * cx + cy, _rows(c, h), :], land.at[2 * cx + cy, _rows(c, h), :], sib,
                  land.at[2 * cx + cy, _rows(1 - c, h), :]) for cx, cy in chips]
                + [(own, land.at[me], sib, land.at[me])])
    return plan


def _plan_swap_halves(h):
    def plan(buf_in, buf):
        x, y, c, _ = _place()
        return [(buf_in.at[_rows(c, h), :], buf.at[_rows(c, h), :], (x, y, 1 - c), buf.at[_rows(1 - c, h), :])]
    return plan


def _plan_pair_rows(h):
    def plan(src, land):
        x, y, c, _ = _place()
        return [(src.at[:, _rows(1 - c, h), :], land, (x, y, 1 - c), land)]
    return plan


def _plan_scatter(src, land):
    x, y, c, chips = _place()
    return [(src.at[2 * cx + cy], land.at[k], (cx, cy, c), land.at[k]) for k, (cx, cy) in enumerate(chips)]


def _plan_scatter8(h):
    def plan(src, land):
        x, y, c, chips = _place()
        copies = [(src.at[2 * x + y, _rows(1 - c, h), :], land.at[0], (x, y, 1 - c), land.at[0])]
        for k, (cx, cy) in enumerate(chips):
            for d, other in enumerate((c, 1 - c)):
                copies.append((src.at[2 * cx + cy, _rows(other, h), :], land.at[1 + 2 * k + c], (cx, cy, other),
                               land.at[1 + 2 * k + other]))
        return copies
    return plan


def _plan_all(src, land):
    x, y, c, _ = _place()
    copies = []
    for r in range(1, 8):
        px, py, pc = (1 - a if b else a for a, b in zip((x, y, c), (r >> 2 & 1, r >> 1 & 1, r & 1)))
        copies.append((src, land.at[4 * x + 2 * y + c], (px, py, pc), land.at[4 * px + 2 * py + pc]))
    return copies


def _remote(src, dst, send_sems, recv_sems, k, peer):
    return pltpu.make_async_remote_copy(src_ref=src, dst_ref=dst, send_sem=send_sems.at[k], recv_sem=recv_sems.at[k],
                                        device_id=peer, device_id_type=MESH)


def _exchange_inplace(name, n, buf, extra, plan):
    def body(*refs):
        ins, buf_ref, send_sems, recv_sems = refs[:1 + len(extra)], refs[1 + len(extra)], refs[-2], refs[-1]
        copies = plan(*ins, buf_ref)
        for k, (s, d, peer, _) in enumerate(copies):
            _remote(s, d, send_sems, recv_sems, k, peer).start()
        for k, (s, _, peer, mine) in enumerate(copies):
            _remote(s, mine, send_sems, recv_sems, k, peer).wait_recv()
        for k, (s, d, peer, _) in enumerate(copies):
            _remote(s, d, send_sems, recv_sems, k, peer).wait_send()

    return pl.pallas_call(
        body, name=name, in_specs=[ANY] * (1 + len(extra)), out_specs=ANY, out_shape=_sds(buf.shape, buf.dtype),
        input_output_aliases={0: 0},
        scratch_shapes=[pltpu.SemaphoreType.DMA((n,)), pltpu.SemaphoreType.DMA((n,))],
    )(buf, *extra)


HBM = pl.BlockSpec(memory_space=pltpu.HBM)
SEM = pl.BlockSpec(memory_space=pltpu.SEMAPHORE)
EFFECT = pltpu.SideEffectType.DATAFLOW_SIDE_EFFECTING


def _exchange_start(name, n, src, land, plan):
    def body(src_ref, land_ref, send_sems, recv_sems, src_thru, land_thru, token):
        for k, (s, d, peer, _) in enumerate(plan(src_ref, land_ref)):
            _remote(s, d, send_sems, recv_sems, k, peer).start()
        token[...] = jnp.zeros_like(token)

    return pl.pallas_call(
        body, name=name,
        out_shape=(pltpu.SemaphoreType.DMA((n,)), pltpu.SemaphoreType.DMA((n,)), pltpu.HBM(src.shape, src.dtype),
                   pltpu.HBM(land.shape, land.dtype), jax.ShapeDtypeStruct((8, LANE), F32)),
        in_specs=(HBM, HBM), out_specs=(SEM, SEM, HBM, HBM, pl.BlockSpec(memory_space=pltpu.VMEM)),
        input_output_aliases={0: 2, 1: 3},
        compiler_params=pltpu.CompilerParams(has_side_effects=EFFECT),
    )(pltpu.with_memory_space_constraint(src, pltpu.HBM), pltpu.with_memory_space_constraint(land, pltpu.HBM))


def _exchange_wait(name, started, plan, after):
    send_sems, recv_sems, src, land, _ = started

    def body(src_ref, land_ref, send_sems, recv_sems, after_ref, src_out, land_out):
        for k, (s, _, peer, mine) in enumerate(plan(src_ref, land_ref)):
            cp = _remote(s, mine, send_sems, recv_sems, k, peer)
            cp.wait_send()
            cp.wait_recv()

    return pl.pallas_call(
        body, name=name, out_shape=(pltpu.HBM(src.shape, src.dtype), pltpu.HBM(land.shape, land.dtype)),
        in_specs=(HBM, HBM, SEM, SEM, ANY), out_specs=(HBM, HBM), input_output_aliases={0: 0, 1: 1},
        compiler_params=pltpu.CompilerParams(has_side_effects=EFFECT),
    )(src, land, send_sems, recv_sems, after)


def _pair_sum(name, cidx, g, recv, br):
    h = recv.shape[1]
    nb = h // br

    def body(c_ref, g_ref, r_ref, out_ref):
        out_ref[...] = (g_ref[...].astype(F32) + r_ref[...].astype(F32)).astype(BF)

    return pl.pallas_call(
        body, name=name,
        grid_spec=pltpu.PrefetchScalarGridSpec(
            num_scalar_prefetch=1, grid=(NSHARD, nb),
            in_specs=[pl.BlockSpec((1, br, D), lambda s, i, c: (s, c[0] * nb + i, 0)),
                      pl.BlockSpec((1, br, D), lambda s, i, c: (s, i, 0))],
            out_specs=pl.BlockSpec((1, br, D), lambda s, i, c: (s, i, 0))),
        out_shape=jax.ShapeDtypeStruct((NSHARD, h, D), BF),
    )(cidx, g, recv)


def _chip_sum(name, place, pb, y, br):
    h = y.shape[1]
    nb = h // br

    def body(pl_ref, p_ref, y_ref, out_ref):
        acc = p_ref[0].astype(F32)
        for k in range(NSHARD - 1):
            acc = acc + y_ref[k].astype(F32)
        out_ref[...] = acc

    return pl.pallas_call(
        body, name=name,
        grid_spec=pltpu.PrefetchScalarGridSpec(
            num_scalar_prefetch=1, grid=(nb,),
            in_specs=[pl.BlockSpec((1, br, D), lambda i, s: (s[0], i, 0)),
                      pl.BlockSpec((NSHARD - 1, br, D), lambda i, s: (0, i, 0))],
            out_specs=pl.BlockSpec((br, D), lambda i, s: (s[1] * nb + i, 0))),
        out_shape=jax.ShapeDtypeStruct((2 * h, D), F32),
    )(place, pb, y)


def _sum8(name, place, g, land, br):
    h = land.shape[1]
    nb = h // br

    def body(pl_ref, g_ref, y_ref, out_ref):
        acc = g_ref[0].astype(F32)
        for k in range(land.shape[0]):
            acc = acc + y_ref[k].astype(F32)
        out_ref[...] = acc

    return pl.pallas_call(
        body, name=name,
        grid_spec=pltpu.PrefetchScalarGridSpec(
            num_scalar_prefetch=1, grid=(nb,),
            in_specs=[pl.BlockSpec((1, br, D), lambda i, s: (s[0], s[1] * nb + i, 0)),
                      pl.BlockSpec((land.shape[0], br, D), lambda i, s: (0, i, 0))],
            out_specs=pl.BlockSpec((br, D), lambda i, s: (s[1] * nb + i, 0))),
        out_shape=jax.ShapeDtypeStruct((2 * h, D), F32), compiler_params=_cp(),
    )(place, g, land)


def _sum_slots(name, v):
    def body(in_ref, out_ref):
        acc = in_ref[0]
        for k in range(1, 8):
            acc = acc + in_ref[k]
        out_ref[...] = acc

    vm = pl.BlockSpec(memory_space=pltpu.VMEM)
    return pl.pallas_call(body, name=name, in_specs=[vm], out_specs=vm,
                          out_shape=jax.ShapeDtypeStruct(v.shape[1:], F32))(v)


def _adamw_math(w, g, m, v):
    m = ADAM_B1 * m + (1.0 - ADAM_B1) * g
    v = ADAM_B2 * v + (1.0 - ADAM_B2) * (g * g)
    m_hat = m / (1.0 - ADAM_B1 ** ADAM_STEP)
    v_hat = v / (1.0 - ADAM_B2 ** ADAM_STEP)
    delta = -ADAM_LR * (m_hat / (jnp.sqrt(v_hat) + ADAM_EPS) + ADAM_WD * w)
    return delta, m, v


def _adamw(w, g, m, v, dep, row0=None):
    r, c = w.shape
    br = next(b for b in (352, 256, 128, r) if r % b == 0 and (row0 or 0) % b == 0)
    first = (row0 or 0) // br

    def body(w_ref, g_ref, m_ref, v_ref, dep_ref, *outs):
        gv = g_ref[...]
        outs[-3][...], outs[-2][...], outs[-1][...] = _adamw_math(w_ref[...], gv, m_ref[...], v_ref[...])
        if row0 is not None:
            outs[0][...] = gv

    spec = pl.BlockSpec((br, c), lambda i: (i, 0))
    n_out = 3 if row0 is None else 4
    out = pl.pallas_call(
        body, name="adamw", grid=(r // br,),
        in_specs=[spec, pl.BlockSpec((br, c), lambda i: (first + i, 0)), spec, spec, ANY], out_specs=[spec] * n_out,
        out_shape=[jax.ShapeDtypeStruct((r, c), F32)] * n_out, compiler_params=_cp(),
    )(w, g, m, v, dep)
    return out if row0 is not None else [g] + list(out)


def _adamw_small(ws, gs, ms, vs):
    n = len(ws)

    def body(*refs):
        ins, outs = refs[:4 * n], refs[4 * n:]
        for k in range(n):
            d, m, v = _adamw_math(ins[k][...], ins[n + k][...], ins[2 * n + k][...], ins[3 * n + k][...])
            outs[k][...] = d
            outs[n + k][...] = m
            outs[2 * n + k][...] = v

    vm = pl.BlockSpec(memory_space=pltpu.VMEM)
    out = pl.pallas_call(
        body, name="adamw_small", in_specs=[vm] * (4 * n), out_specs=[vm] * (3 * n),
        out_shape=[jax.ShapeDtypeStruct(w.shape, F32) for w in ws] * 3,
    )(*ws, *gs, *ms, *vs)
    return out[:n], out[n:2 * n], out[2 * n:]


BIG = ("w_in", "w_out", "w_ffn_gate", "w_ffn_up", "w_ffn_down", "w_ple_proj", "w_ple_gate")
SMALL = ("g_mix_pre", "b_forget", "g_attn_grp", "g_pool_grp", "w_pool", "pool_scale", "g_mix_post", "g_ffn_pre",
         "g_ffn_post", "g_ple")
TRANSPOSED = ("w_in", "w_ffn_gate", "w_ffn_up")
VECTORS = tuple(n for n in SMALL if n != "w_pool")
ORDER = ("g_mix_pre", "w_in", "b_forget", "g_attn_grp", "g_pool_grp", "w_pool", "pool_scale", "w_out", "g_mix_post",
         "g_ffn_pre", "w_ffn_gate", "w_ffn_up", "w_ffn_down", "g_ffn_post", "w_ple_proj", "g_ple", "w_ple_gate")


def _pad_rows(a, rows):
    return jnp.pad(a, ((0, rows - a.shape[0]), (0, 0)))


def _stack1(w_in, w_out):
    return _pad_rows(jnp.concatenate([_pad_rows(w_in.T, IN_PAD), w_out], axis=0), ROWS1)


def _stack2(wg, wu, wd, wple, wpg):
    return _pad_rows(jnp.concatenate([wg.T, wu.T, wd, wple.reshape(DPLE // NSHARD, D), wpg], axis=0), ROWS2)


def _unstack1(s):
    return s[:IN_SH], s[O1_OUT:USED1]


def _cat(g, lo, hi):
    return g[:, lo:hi].reshape(NSHARD * (hi - lo), D)


def _unstack1_full(g):
    w_in_t = _cat(g, 0, IN_SH)
    w_in_t = jnp.concatenate([w_in_t[:3 * DA], _pad_rows(w_in_t[3 * DA:3 * DA + NH], LANE), w_in_t[3 * DA + NH:]], axis=0)
    return w_in_t, _cat(g, O1_OUT, USED1)


def _unstack2_full(g):
    w_ple = g[:, O2_PLE:O2_PG].reshape(NSHARD, DPLE, DPLE).transpose(1, 0, 2).reshape(DPLE, D)
    return _cat(g, 0, O2_U), _cat(g, O2_U, O2_D), _cat(g, O2_D, O2_PLE), w_ple, _cat(g, O2_PG, USED2)


def _shards(a):
    return a.reshape(NSHARD, a.shape[0] // NSHARD, D)


def _stack1_full(dwin_t, dwout):
    dwin_t = jnp.concatenate([dwin_t[:3 * DA + NH], dwin_t[3 * DA + LANE:]], axis=0).reshape(NSHARD, IN_SH, D)
    zeros = lambda r: jnp.zeros((NSHARD, r, D), dwin_t.dtype)
    return jnp.concatenate([dwin_t, zeros(IN_PAD - IN_SH), _shards(dwout), zeros(ROWS1 - USED1)], axis=1)


def _stack2_full(dwg_t, dwu_t, dwd, dwple, dwpg):
    dwple = dwple.reshape(DPLE, NSHARD, DPLE).transpose(1, 0, 2).reshape(NSHARD, DPLE // NSHARD, D)
    return jnp.concatenate([_shards(dwg_t), _shards(dwu_t), _shards(dwd), dwple, _shards(dwpg),
                            jnp.zeros((NSHARD, ROWS2 - USED2, D), dwd.dtype)], axis=1)


def _sds(shape, dtype):
    return jax.ShapeDtypeStruct(shape, dtype)


class _Comm:
    def __init__(self, stack2, me, c):
        self.stack2, self.me, self.c = stack2, me, c
        self.cidx = c.astype(jnp.int32).reshape(1)
        self.place = jnp.stack([me, c]).astype(jnp.int32)
        self.h = ROWS2 // 2
        self.gather = _exchange_start("gather2_start", 3, stack2, lax.empty((NSHARD, ROWS2, D), BF), _plan_gather(self.h))
        self.dep = self.gather[4]

    def after_attention(self, after):
        own, land = _exchange_wait("gather2_wait", self.gather, _plan_gather(self.h), after)
        fwd = _plan_forward(self.h)
        self.forward = lambda own_ref, land_ref: fwd(land_ref, own_ref, land_ref)
        self.passing = _exchange_start("forward2_start", 4, own, land, self.forward)
        return self.passing[4]

    def weights2(self, after):
        return _exchange_wait("forward2_wait", self.passing, self.forward, after)[1]

    def after_ffn(self, grads2):
        g = _stack2_full(*[a.astype(BF) for a in grads2])
        self.scatter = _plan_scatter8(self.h)
        self.chip = _exchange_start("reduce2_start", 7, g, lax.empty((7, self.h, D), BF), self.scatter)
        return self.chip[4]

    def after_mix(self, after, early):
        self.early_shapes = {n: early[n].shape for n in early}
        self.small = self.start_small("small", early)
        return self.small[4]

    def after_attn(self, after):
        g, y = _exchange_wait("reduce2_wait", self.chip, self.scatter, after)
        f = _sum8("sum2", self.place, g, y, RED2 // 2)
        self.early = self.finish_small("small", self.small, self.early_shapes, after)
        swap = _plan_swap_halves(self.h)
        self.swap = lambda _, buf: swap(buf, buf)
        self.swapping = _exchange_start("reduce2_gather_start", 1, self.dep, f, self.swap)
        return self.swapping[4]

    def reduced2(self, after):
        return _exchange_wait("reduce2_gather_wait", self.swapping, self.swap, after)[1]

    def start_small(self, name, small):
        v = _pack_small(small)
        return _exchange_start(name + "_start", 7, v, lax.empty((8,) + v.shape, F32), _plan_all)

    def finish_small(self, name, started, shapes, after):
        v, land = _exchange_wait(name + "_wait", started, _plan_all, after)
        land = lax.dynamic_update_slice(land, v[None], (2 * self.me + self.c, 0, 0))
        return _unpack_small(_sum_slots(name + "_sum", land), shapes)


def _pack_small(small):
    parts = []
    for name in small:
        flat = small[name].reshape(-1)
        parts.append(jnp.pad(flat, (0, -flat.shape[0] % LANE)).reshape(-1, LANE))
    v = jnp.concatenate(parts, axis=0)
    return _pad_rows(v, v.shape[0] + (-v.shape[0] % 8))


def _unpack_small(v, shapes):
    out, r = {}, 0
    for name in shapes:
        n = math.prod(shapes[name])
        rows = -(-n // LANE)
        out[name] = v[r:r + rows].reshape(-1)[:n].reshape(shapes[name])
        r += rows
    return out


def kernel(x, p, g_mix_pre, w_in, b_forget, g_attn_grp, g_pool_grp, w_pool, pool_scale, w_out, g_mix_post, g_ffn_pre, w_ffn_gate, w_ffn_up, w_ffn_down, g_ffn_post, w_ple_proj, g_ple, w_ple_gate, loss_target, m_g_mix_pre, m_w_in, m_b_forget, m_g_attn_grp, m_g_pool_grp, m_w_pool, m_pool_scale, m_w_out, m_g_mix_post, m_g_ffn_pre, m_w_ffn_gate, m_w_ffn_up, m_w_ffn_down, m_g_ffn_post, m_w_ple_proj, m_g_ple, m_w_ple_gate, v_g_mix_pre, v_w_in, v_b_forget, v_g_attn_grp, v_g_pool_grp, v_w_pool, v_pool_scale, v_w_out, v_g_mix_post, v_g_ffn_pre, v_w_ffn_gate, v_w_ffn_up, v_w_ffn_down, v_g_ffn_post, v_w_ple_proj, v_g_ple, v_w_ple_gate):
    args = dict(locals())
    strip = lambda n, a: a if n in VECTORS else a[0]
    w = {n: strip(n, args[n]) for n in ORDER}
    mom = {n: strip(n, args["m_" + n]) for n in ORDER}
    var = {n: strip(n, args["v_" + n]) for n in ORDER}
    sm = {n: w[n] for n in SMALL}

    c = lax.axis_index("c")
    me = 2 * lax.axis_index("x") + lax.axis_index("y")
    h1 = ROWS1 // 2
    bf = lambda n: w[n].astype(BF)
    stack1 = _stack1(bf("w_in"), bf("w_out"))
    gather1 = _exchange_start("gather1_start", 3, stack1, lax.empty((NSHARD, ROWS1, D), BF), _plan_gather(h1))
    _, *shards2 = lax.optimization_barrier((gather1[4], *[w[n] for n in BIG[2:]]))
    stack2 = _stack2(*[a.astype(BF) for a in shards2])
    stack1, land = _exchange_wait("gather1_wait", gather1, _plan_gather(h1), stack2)
    land, stack2 = lax.optimization_barrier((land, stack2))
    comm = _Comm(stack2, me, c)
    w1 = _unstack1_full(_exchange_inplace("gather1_forward", 4, land, (stack1,), _plan_forward(h1)))
    dx, grads1, late = _local_step(x[0], p[0, 0], loss_target[0], sm, w1, comm)

    flip = lambda n, a: a.T if n in TRANSPOSED else a
    grads, delta, new_m, new_v = {}, {}, {}, {}

    def update(n, g, dep, row0=None):
        g_, d_, m_, v_ = _adamw(flip(n, w[n]), g, flip(n, mom[n]), flip(n, var[n]), dep, row0)
        grads[n], delta[n], new_m[n], new_v[n] = flip(n, g_), flip(n, d_), flip(n, m_), flip(n, v_)
        return v_

    late_shapes = {n: late[n].shape for n in late}
    small2 = comm.start_small("small2", late)
    red2 = comm.reduced2(dx)
    g1 = _stack1_full(grads1[0], grads1[1].astype(BF))
    pair1 = _exchange_start("reduce1_pair_start", 1, g1, lax.empty((NSHARD, h1, D), BF), _plan_pair_rows(h1))
    dep = update("w_ple_gate", red2, pair1[4] + small2[4], O2_PG)
    dep = update("w_ple_proj", red2[O2_PLE:O2_PG].reshape(DPLE, DPLE), dep)
    g1, recv = _exchange_wait("reduce1_pair_wait", pair1, _plan_pair_rows(h1), dep)
    pb = _pair_sum("pair_sum1", comm.cidx, g1, recv, RED1)
    chip1 = _exchange_start("reduce1_chip_start", 3, pb, lax.empty((NSHARD - 1, h1, D), BF), _plan_scatter)
    dep = update("w_ffn_gate", red2, chip1[4], 0)
    dep = update("w_ffn_up", red2, dep, O2_U)
    dep = update("w_ffn_down", red2, dep, O2_D)
    red_small = {**comm.early, **comm.finish_small("small2", small2, late_shapes, dep)}
    loss = 0.5 / D * red_small["loss"][0, 0]
    for n in SMALL:
        grads[n] = red_small[n].reshape(w[n].shape)
    two_d = lambda a: a.reshape(-1, a.shape[-1])
    ds, ms, vs = _adamw_small([two_d(w[n]) for n in SMALL], [two_d(grads[n]) for n in SMALL],
                              [two_d(mom[n]) for n in SMALL], [two_d(var[n]) for n in SMALL])
    for k, n in enumerate(SMALL):
        delta[n], new_m[n], new_v[n] = ds[k].reshape(w[n].shape), ms[k].reshape(w[n].shape), vs[k].reshape(w[n].shape)
    pb, y = _exchange_wait("reduce1_chip_wait", chip1, _plan_scatter, vs[0])
    f = _chip_sum("chip_sum1", comm.place, pb, y, RED1)
    reduced1 = _exchange_inplace("reduce1_gather", 1, f, (), _plan_swap_halves(h1))
    g_in, g_out = _unstack1(reduced1)
    update("w_out", g_out, update("w_in", g_in, reduced1))

    lead = lambda d: [d[n] if n in VECTORS else d[n][None] for n in ORDER]
    return (loss, dx[None], *lead(grads), *lead(delta), *lead(new_m), *lead(new_v))
```

```python
import functools
import math

import jax
import jax.numpy as jnp
import numpy as np
from jax import lax
from jax.experimental import pallas as pl
from jax.experimental.pallas import tpu as pltpu

F32 = jnp.float32
BF = jnp.bfloat16
MESH = pl.DeviceIdType.MESH

D = 1024
DA = 512
DP = 512
NH = 8
HD = 64
DFF = 2816
DPLE = 256
WINS = (2, 4, 8, 16)
PC = 128
ZW = 3 * DA + 128 + DP
EPS = 1e-6
NSHARD = 4

LANE = 128
HALO = 128

IN_SH = 514
IN_PAD = 528
FF_SH = DFF // NSHARD
O1_OUT, USED1, ROWS1 = 528, 784, 800
O2_U, O2_D, O2_PLE, O2_PG, USED2, ROWS2 = 704, 1408, 2112, 2176, 2432, 2560
RED1, RED2 = 400, 640

ADAM_LR, ADAM_B1, ADAM_B2, ADAM_EPS, ADAM_WD, ADAM_STEP = 0.001, 0.9, 0.999, 1e-8, 0.01, 10

VMEM_LIMIT = 56 * 1024 * 1024


def _cp(**kw):
    return pltpu.CompilerParams(vmem_limit_bytes=VMEM_LIMIT, **kw)


def _mm(a, b):
    return jnp.dot(a.astype(BF), b.astype(BF), preferred_element_type=F32)


def _mm_nt(a, b):
    return lax.dot_general(a.astype(BF), b.astype(BF), (((1,), (1,)), ((), ())), preferred_element_type=F32)


def _mm_tn(a, b):
    return lax.dot_general(a.astype(BF), b.astype(BF), (((0,), (0,)), ((), ())), preferred_element_type=F32)


def _split3(x):
    hi = x.astype(BF)
    r = x - hi.astype(F32)
    mid = r.astype(BF)
    lo = (r - mid.astype(F32)).astype(BF)
    return hi, mid, lo


def _dot3(m, x):
    hi, mid, lo = _split3(x)
    return (jnp.dot(m, hi, preferred_element_type=F32) + jnp.dot(m, mid, preferred_element_type=F32)
            + jnp.dot(m, lo, preferred_element_type=F32))


def _window_sum(ext, w, back):
    n = ext.shape[0]
    s, k = ext, 1
    while k < w:
        s = s + pltpu.roll(s, k if back else n - k, 0)
        k *= 2
    return s


def _rstd(x):
    return lax.rsqrt(jnp.mean(x * x, axis=-1, keepdims=True) + EPS)


def _rms_bwd(dy, x, g):
    r = _rstd(x)
    xh = x * r
    dg = jnp.sum(dy * xh, axis=0, keepdims=True)
    dxh = dy * g
    dx = r * (dxh - xh * jnp.mean(dxh * xh, axis=-1, keepdims=True))
    return dx, dg


def _sigmoid(x):
    return 1.0 / (1.0 + jnp.exp(-x))


ANY = pl.BlockSpec(memory_space=pl.ANY)


def _full(shape):
    n = len(shape)
    return pl.BlockSpec(shape, lambda *_: (0,) * n)


def _resident(shape):
    n = len(shape)
    return pl.BlockSpec(shape, lambda *_: (0,) * n, pipeline_mode=pl.Buffered(1))


def _tile(t):
    return 512 if t % 512 == 0 else t


def _tri(n, upper):
    r, c = np.indices((n, n))
    return ((c >= r) if upper else (c <= r)).astype(BF)


def _aug_consts():
    row, col = np.indices((LANE, NH * LANE))
    piece, head = row // NH, row % NH
    ch, cl = col // LANE, col % LANE
    eq = ((piece < 3) & (head == ch) & (cl == HD + 8 * piece + head)).astype(BF)
    ek = -((piece < 3) & (head == ch) & (cl == HD + 24 + 8 * piece + head)).astype(BF)
    off = (np.arange(NH * LANE) % LANE - HD - np.arange(NH * LANE) // LANE)[None, :]
    rowq = ((off >= 24) & (off < 48) & (off % 8 == 0)).astype(np.float32)
    rowk = ((off >= 0) & (off < 24) & (off % 8 == 0)).astype(np.float32)
    return eq, ek, rowq, rowk


def _in_proj(x, g1, w_in_t, b_pad, tri, eq, ek, rowq, rowk, dep):
    t = x.shape[0]
    tt = _tile(t)

    def body(x_ref, g_ref, w_ref, b_ref, tri_ref, eq_ref, ek_ref, rq_ref, rk_ref, dep_ref,
             qa_ref, ka_ref, v_ref, u_ref, fl_ref, carry):
        i = pl.program_id(0)

        @pl.when(i == 0)
        def _():
            carry[...] = jnp.zeros_like(carry)

        xv = x_ref[...]
        hn = (xv * _rstd(xv) * g_ref[...]).astype(BF)
        z = _mm_nt(hn, w_ref[...])
        fl = z[:, 3 * DA:3 * DA + LANE] + b_ref[...]
        lane = lax.broadcasted_iota(jnp.int32, fl.shape, 1)
        lf = jnp.where(lane < NH, jnp.minimum(fl, 0.0) - jnp.log(1.0 + jnp.exp(-jnp.abs(fl))), 0.0)
        c = carry[...] + _dot3(tri_ref[...], lf)
        carry[...] = carry[...] + jnp.sum(lf, axis=0, keepdims=True)
        hi, mid, lo = _split3(c)
        caug = (hi.astype(F32) + pltpu.roll(mid.astype(F32), NH, 1) + pltpu.roll(lo.astype(F32), 2 * NH, 1)).astype(BF)
        aug_q = jnp.dot(caug, eq_ref[...], preferred_element_type=F32) + rq_ref[...]
        aug_k = jnp.dot(caug, ek_ref[...], preferred_element_type=F32) + rk_ref[...]
        low = lax.broadcasted_iota(jnp.int32, (tt, LANE), 1) < HD
        for p in range(NH // 2):
            qp = z[:, LANE * p:LANE * (p + 1)] * (1.0 / math.sqrt(HD))
            kp = z[:, DA + LANE * p:DA + LANE * (p + 1)]
            for h, (qh, kh) in enumerate(((qp, kp), (pltpu.roll(qp, HD, 1), pltpu.roll(kp, HD, 1)))):
                lo_, hi_ = LANE * (2 * p + h), LANE * (2 * p + h + 1)
                qa_ref[:, lo_:hi_] = jnp.where(low, qh, aug_q[:, lo_:hi_]).astype(BF)
                ka_ref[:, lo_:hi_] = jnp.where(low, kh, aug_k[:, lo_:hi_]).astype(BF)
        v_ref[...] = z[:, 2 * DA:3 * DA].astype(BF)
        u_ref[...] = z[:, 3 * DA + LANE:]
        fl_ref[...] = fl

    return pl.pallas_call(
        body, name="in_proj", grid=(t // tt,),
        in_specs=[pl.BlockSpec((tt, D), lambda i: (i, 0)), _full((1, D)), _resident((ZW, D)), _full((1, LANE)),
                  _full((tt, tt)), _full((LANE, NH * LANE)), _full((LANE, NH * LANE)),
                  _full((1, NH * LANE)), _full((1, NH * LANE)), ANY],
        out_specs=[pl.BlockSpec((tt, NH * LANE), lambda i: (i, 0)), pl.BlockSpec((tt, NH * LANE), lambda i: (i, 0)),
                   pl.BlockSpec((tt, DA), lambda i: (i, 0)), pl.BlockSpec((tt, DP), lambda i: (i, 0)),
                   pl.BlockSpec((tt, LANE), lambda i: (i, 0))],
        out_shape=[jax.ShapeDtypeStruct((t, NH * LANE), BF), jax.ShapeDtypeStruct((t, NH * LANE), BF),
                   jax.ShapeDtypeStruct((t, DA), BF), jax.ShapeDtypeStruct((t, DP), F32),
                   jax.ShapeDtypeStruct((t, LANE), F32)],
        scratch_shapes=[pltpu.VMEM((1, LANE), F32)],
        compiler_params=_cp(),
    )(x, g1, w_in_t, b_pad, tri, eq, ek, rowq, rowk, dep)


def _attn_fwd(qa, ka, v):
    t = qa.shape[0]
    ta = _tile(t)
    n = t // ta

    def body(q_ref, k_ref, v_ref, a_ref, lse_ref, m_ref, l_ref, acc_ref):
        i = pl.program_id(1)
        m_ref[...] = jnp.full_like(m_ref, -1e30)
        l_ref[...] = jnp.zeros_like(l_ref)
        acc_ref[...] = jnp.zeros_like(acc_ref)
        qs = [q_ref[:, LANE * h:LANE * (h + 1)] for h in range(2)]
        reps = ta // LANE

        def tile(j, width, masked):
            rows = pl.ds(pl.multiple_of(j * ta, ta), width * ta)
            v2 = v_ref[rows, :]
            s = [_mm_nt(qs[h], k_ref[rows, LANE * h:LANE * (h + 1)]) for h in range(2)]
            if masked:
                keep = (lax.broadcasted_iota(jnp.int32, (ta, ta), 1) <= lax.broadcasted_iota(jnp.int32, (ta, ta), 0))
                s = [jnp.where(keep, sh, -1e30) for sh in s]
            m_old = [m_ref[h] for h in range(2)]
            m_new = [jnp.maximum(m_old[h], jnp.max(s[h], axis=1, keepdims=True)) for h in range(2)]
            pe = [jnp.exp(s[h] - jnp.tile(m_new[h], (1, width * reps))) for h in range(2)]
            alpha = [jnp.exp(m_old[h] - m_new[h]) for h in range(2)]
            pv = [jnp.dot(pe[h].astype(BF), v2, preferred_element_type=F32) for h in range(2)]
            for h in range(2):
                l_ref[h] = alpha[h] * l_ref[h] + jnp.sum(pe[h], axis=1, keepdims=True)
                acc_ref[h] = alpha[h] * acc_ref[h] + pv[h]
                m_ref[h] = m_new[h]

        def step(jj, carry):
            tile(2 * jj, 2, False)
            return carry

        lax.fori_loop(0, i // 2, step, 0)

        @pl.when(i % 2 == 1)
        def _():
            tile(i - 1, 1, False)

        tile(i, 1, True)
        low = lax.broadcasted_iota(jnp.int32, (ta, LANE), 1) < HD
        a_ref[...] = jnp.where(low, acc_ref[0] / l_ref[0], acc_ref[1] / l_ref[1])
        lse_ref[...] = jnp.where(low, m_ref[0] + jnp.log(l_ref[0]), m_ref[1] + jnp.log(l_ref[1]))

    return pl.pallas_call(
        body, name="attn_fwd", grid=(NH // 2, n),
        in_specs=[pl.BlockSpec((ta, 2 * LANE), lambda p, i: (i, p)),
                  pl.BlockSpec((t, 2 * LANE), lambda p, i: (0, p)),
                  pl.BlockSpec((t, LANE), lambda p, i: (0, p))],
        out_specs=[pl.BlockSpec((ta, LANE), lambda p, i: (i, p)), pl.BlockSpec((ta, LANE), lambda p, i: (i, p))],
        out_shape=[jax.ShapeDtypeStruct((t, DA), F32), jax.ShapeDtypeStruct((t, DA), F32)],
        scratch_shapes=[pltpu.VMEM((2, ta, LANE), F32), pltpu.VMEM((2, ta, LANE), F32), pltpu.VMEM((2, ta, LANE), F32)],
        compiler_params=_cp(),
    )(qa, ka, v)


def _mix_out(a, u, x, w_pool, pool_scale, g_attn, g_pool, w_out, g_post, dep):
    t = a.shape[0]
    tt = _tile(t)
    hb = tt // HALO

    def body(a_ref, u_ref, up_ref, x_ref, wp_ref, ps_ref, ga_ref, gp_ref, wo_ref, go_ref, dep_ref,
             yb_ref, m_ref, o_ref, h1_ref):
        i = pl.program_id(0)
        prev = up_ref[...] * jnp.where(i > 0, 1.0, 0.0)
        tok = i * tt + lax.broadcasted_iota(jnp.int32, (tt, PC), 0)
        ms = []
        for g, w in enumerate(WINS):
            ug = u_ref[:, PC * g:PC * (g + 1)]
            ext = jnp.concatenate([prev[:, PC * g:PC * (g + 1)], ug], axis=0)
            cnt = jnp.minimum(tok + 1, w).astype(F32)
            y = (_window_sum(ext, w, True)[HALO:] / cnt - ug).astype(BF)
            yb_ref[:, PC * g:PC * (g + 1)] = y
            ms.append(_mm(y, wp_ref[g]) * ps_ref[:, PC * g:PC * (g + 1)])
        m = jnp.concatenate(ms, axis=1)
        m_ref[...] = m
        av = a_ref[...]
        mix = jnp.concatenate([av * _rstd(av) * ga_ref[...], m * _rstd(m) * gp_ref[...]], axis=1)
        o = _mm(mix, wo_ref[...])
        o_ref[...] = o
        h1_ref[...] = x_ref[...] + o * _rstd(o) * go_ref[...]

    return pl.pallas_call(
        body, name="mix_out", grid=(t // tt,),
        in_specs=[pl.BlockSpec((tt, DA), lambda i: (i, 0)), pl.BlockSpec((tt, DP), lambda i: (i, 0)),
                  pl.BlockSpec((HALO, DP), lambda i: (jnp.maximum(i * hb - 1, 0), 0)),
                  pl.BlockSpec((tt, D), lambda i: (i, 0)),
                  _full((len(WINS), PC, PC)), _full((1, DP)), _full((1, DA)), _full((1, DP)),
                  _resident((D, D)), _full((1, D)), ANY],
        out_specs=[pl.BlockSpec((tt, DP), lambda i: (i, 0)), pl.BlockSpec((tt, DP), lambda i: (i, 0)),
                   pl.BlockSpec((tt, D), lambda i: (i, 0)), pl.BlockSpec((tt, D), lambda i: (i, 0))],
        out_shape=[jax.ShapeDtypeStruct((t, DP), BF), jax.ShapeDtypeStruct((t, DP), F32),
                   jax.ShapeDtypeStruct((t, D), F32), jax.ShapeDtypeStruct((t, D), F32)],
        compiler_params=_cp(),
    )(a, u, u, x, w_pool, pool_scale, g_attn, g_pool, w_out, g_post, dep)


def _ffn_fwd(h1, g_pre, stacks2, g_post, p, w_ple, g_ple, w_pg, tgt):
    t = h1.shape[0]
    tt = 256 if t % 256 == 0 else t

    def body(h1_ref, gpre_ref, wg_ref, wu_ref, wd_ref, gpost_ref, p_ref, wple_ref, gple_ref, wpg_ref, tgt_ref,
             hn_ref, gate_ref, up_ref, dff_ref, dh2_ref, loss_ref, dwpg_ref, dwple_ref, dgple_ref, dgpost_ref):
        i = pl.program_id(0)

        @pl.when(i == 0)
        def _():
            loss_ref[...] = jnp.zeros_like(loss_ref)
            dwpg_ref[...] = jnp.zeros_like(dwpg_ref)
            dwple_ref[...] = jnp.zeros_like(dwple_ref)
            dgple_ref[...] = jnp.zeros_like(dgple_ref)
            dgpost_ref[...] = jnp.zeros_like(dgpost_ref)

        h1v = h1_ref[...]
        hn = (h1v * _rstd(h1v) * gpre_ref[...]).astype(BF)
        hn_ref[...] = hn
        gate = _mm_nt(hn, wg_ref[...].reshape(DFF, D))
        up = _mm_nt(hn, wu_ref[...].reshape(DFF, D))
        for k in range(DFF // FF_CH):
            gate_ref[k] = gate[:, FF_CH * k:FF_CH * (k + 1)].astype(BF)
            up_ref[k] = up[:, FF_CH * k:FF_CH * (k + 1)].astype(BF)
        ff = _mm(gate * _sigmoid(gate) * up, wd_ref[...].reshape(DFF, D))
        rff = _rstd(ff)
        ffh = ff * rff
        gpost = gpost_ref[...]
        h2 = h1v + ffh * gpost
        pv = p_ref[...]
        pe = _mm(pv, wple_ref[...])
        rpe = _rstd(pe)
        peh = pe * rpe
        gple = gple_ref[...]
        e = peh * gple
        sig = _sigmoid(_mm(h2, wpg_ref[...]))
        dv = h2 + sig * e - tgt_ref[...]
        sq = jnp.sum(jnp.sum(dv * dv, axis=1, keepdims=True), axis=0, keepdims=True)
        loss_ref[...] = loss_ref[...] + sq
        dy = dv * (1.0 / D)
        d_e = dy * sig
        d_gl = dy * e * sig * (1.0 - sig)
        dh2 = dy + _mm_nt(d_gl, wpg_ref[...])
        dh2_ref[...] = dh2
        dwpg_ref[...] = dwpg_ref[...] + _mm_tn(h2, d_gl)
        dgple_ref[...] = dgple_ref[...] + jnp.sum(d_e * peh, axis=0, keepdims=True)
        dpeh = d_e * gple
        d_pe = rpe * (dpeh - peh * jnp.mean(dpeh * peh, axis=-1, keepdims=True))
        dwple_ref[...] = dwple_ref[...] + _mm_tn(pv, d_pe)
        dgpost_ref[...] = dgpost_ref[...] + jnp.sum(dh2 * ffh, axis=0, keepdims=True)
        dffh = dh2 * gpost
        dff_ref[...] = (rff * (dffh - ffh * jnp.mean(dffh * ffh, axis=-1, keepdims=True))).astype(BF)

    row = lambda w: pl.BlockSpec((tt, w), lambda i: (i, 0))
    chunked = pl.BlockSpec((DFF // FF_CH, tt, FF_CH), lambda i: (0, i, 0))
    shard_rows = lambda k: pl.BlockSpec((NSHARD, FF_SH, D), lambda i: (0, k, 0), pipeline_mode=pl.Buffered(1))
    return pl.pallas_call(
        body, name="ffn_fwd", grid=(t // tt,),
        in_specs=[row(D), _full((1, D)), shard_rows(0), shard_rows(1), shard_rows(2), _full((1, D)),
                  row(DPLE), _resident((DPLE, D)), _full((1, D)), _resident((D, D)), row(D)],
        out_specs=[row(D), chunked, chunked, row(D), row(D), _full((8, LANE)), _full((D, D)), _full((DPLE, D)),
                   _full((1, D)), _full((1, D))],
        out_shape=[jax.ShapeDtypeStruct((t, D), BF), jax.ShapeDtypeStruct((DFF // FF_CH, t, FF_CH), BF),
                   jax.ShapeDtypeStruct((DFF // FF_CH, t, FF_CH), BF),
                   jax.ShapeDtypeStruct((t, D), BF), jax.ShapeDtypeStruct((t, D), F32), jax.ShapeDtypeStruct((8, LANE), F32),
                   jax.ShapeDtypeStruct((D, D), F32), jax.ShapeDtypeStruct((DPLE, D), F32),
                   jax.ShapeDtypeStruct((1, D), F32), jax.ShapeDtypeStruct((1, D), F32)],
        compiler_params=_cp(),
    )(h1, g_pre, stacks2, stacks2, stacks2, g_post, p, w_ple, g_ple, w_pg, tgt)


FF_CH = 256


def _ffn_bwd(hn2, gate, up, dff, wg_t, wu_t, wd):
    t = hn2.shape[0]
    tt = 1024 if t % 1024 == 0 else _tile(t)
    nt = t // tt
    ch = FF_CH
    nc = DFF // ch

    def body(hn_ref, gate_ref, up_ref, dff_ref, wg_ref, wu_ref, wd_ref,
             dwg_ref, dwu_ref, dwd_ref, dhn_ref, acc, gu_acc, d_acc, sem):
        j, i = pl.program_id(0), pl.program_id(1)

        @pl.when(j == 0)
        def _():
            acc[pl.ds(pl.multiple_of(i * tt, tt), tt), :] = jnp.zeros((tt, D), F32)

        @pl.when(i == 0)
        def _():
            gu_acc[...] = jnp.zeros_like(gu_acc)
            d_acc[...] = jnp.zeros_like(d_acc)

        wgu = jnp.concatenate([wg_ref[...], wu_ref[...]], axis=0)
        parts = 4 if tt % 1024 == 0 else 2
        half = tt // parts
        acts, dgus = [], []
        for hh in range(parts):
            r = slice(hh * half, (hh + 1) * half)
            gate_v = gate_ref[0, r, :].astype(F32)
            up_v = up_ref[0, r, :].astype(F32)
            sg = _sigmoid(gate_v)
            silu = gate_v * sg
            d_act = _mm_nt(dff_ref[r, :], wd_ref[...])
            d_up = (d_act * silu).astype(BF)
            d_gate = (d_act * up_v * (sg * (1.0 + gate_v * (1.0 - sg)))).astype(BF)
            dgu = jnp.concatenate([d_gate, d_up], axis=1)
            rows = pl.ds(pl.multiple_of(i * tt + hh * half, half), half)
            acc[rows, :] = acc[rows, :] + jnp.dot(dgu, wgu, preferred_element_type=F32)
            acts.append((silu * up_v).astype(BF))
            dgus.append(dgu)
        d_acc[...] = d_acc[...] + _mm_tn(jnp.concatenate(acts, axis=0), dff_ref[...])
        gu_acc[...] = gu_acc[...] + _mm_tn(jnp.concatenate(dgus, axis=0), hn_ref[...])

        @pl.when(i == nt - 1)
        def _():
            dwg_ref[...] = gu_acc[:ch].astype(BF)
            dwu_ref[...] = gu_acc[ch:].astype(BF)
            dwd_ref[...] = d_acc[...].astype(BF)

        @pl.when((j == nc - 1) & (i == nt - 1))
        def _():
            cp = pltpu.make_async_copy(acc, dhn_ref, sem)
            cp.start()
            cp.wait()

    tok = lambda w: pl.BlockSpec((tt, w), lambda j, i: (i, 0))
    chunk = pl.BlockSpec((ch, D), lambda j, i: (j, 0))
    return pl.pallas_call(
        body, name="ffn_bwd", grid=(nc, nt),
        in_specs=[tok(D), pl.BlockSpec((1, tt, ch), lambda j, i: (j, i, 0)), pl.BlockSpec((1, tt, ch), lambda j, i: (j, i, 0)),
                  tok(D), chunk, chunk, chunk],
        out_specs=[chunk, chunk, chunk, pl.BlockSpec(memory_space=pl.ANY)],
        out_shape=[jax.ShapeDtypeStruct((DFF, D), BF), jax.ShapeDtypeStruct((DFF, D), BF),
                   jax.ShapeDtypeStruct((DFF, D), BF), jax.ShapeDtypeStruct((t, D), F32)],
        scratch_shapes=[pltpu.VMEM((t, D), F32), pltpu.VMEM((2 * ch, D), F32), pltpu.VMEM((ch, D), F32),
                        pltpu.SemaphoreType.DMA],
        compiler_params=_cp(),
    )(hn2, gate, up, dff, wg_t, wu_t, wd)


def _mix_bwd(d_hn2, dh2, h1, o, a, m, yb, g_ffn_pre, g_post, g_attn, g_pool, w_out, w_pool, pool_scale, dep):
    t = a.shape[0]
    tt = _tile(t)

    def body(dhn_ref, dh2_ref, h1_ref, o_ref, a_ref, m_ref, yb_ref, gfp_ref, go_ref, ga_ref, gp_ref, wo_ref, wp_ref,
             ps_ref, dep_ref, dh1_ref, da_ref, dyc_ref, dgfp_ref, dgo_ref, dga_ref, dgp_ref, dps_ref, dwp_ref, dwo_ref):
        i = pl.program_id(0)

        @pl.when(i == 0)
        def _():
            for r in (dgfp_ref, dgo_ref, dga_ref, dgp_ref, dps_ref, dwp_ref, dwo_ref):
                r[...] = jnp.zeros_like(r)

        d1, dg = _rms_bwd(dhn_ref[...], h1_ref[...], gfp_ref[...])
        dgfp_ref[...] = dgfp_ref[...] + dg
        dh1 = dh2_ref[...] + d1
        dh1_ref[...] = dh1
        d_o, dg = _rms_bwd(dh1, o_ref[...], go_ref[...])
        dgo_ref[...] = dgo_ref[...] + dg
        d_mix = _mm_nt(d_o, wo_ref[...])
        av, mv = a_ref[...], m_ref[...]
        mix = jnp.concatenate([av * _rstd(av) * ga_ref[...], mv * _rstd(mv) * gp_ref[...]], axis=1)
        dwo_ref[...] = dwo_ref[...] + _mm_tn(mix, d_o)
        d_a, dg = _rms_bwd(d_mix[:, :DA], av, ga_ref[...])
        dga_ref[...] = dga_ref[...] + dg
        da_ref[...] = d_a
        d_m, dg = _rms_bwd(d_mix[:, DA:], mv, gp_ref[...])
        dgp_ref[...] = dgp_ref[...] + dg
        tok = i * tt + lax.broadcasted_iota(jnp.int32, (tt, PC), 0)
        dps = []
        for g, w in enumerate(WINS):
            sl = slice(PC * g, PC * (g + 1))
            ybg = yb_ref[:, sl]
            wpg = wp_ref[g].astype(BF)
            mlin = jnp.dot(ybg, wpg, preferred_element_type=F32)
            dmg = d_m[:, sl]
            dps.append(jnp.sum(dmg * mlin, axis=0, keepdims=True))
            dml = (dmg * ps_ref[:, sl]).astype(BF)
            dwp_ref[g] = dwp_ref[g] + _mm_tn(ybg, dml)
            dyc_ref[:, sl] = _mm_nt(dml, wpg) / jnp.minimum(tok + 1, w).astype(F32)
        dps_ref[...] = dps_ref[...] + jnp.concatenate(dps, axis=1)

    row = lambda w: pl.BlockSpec((tt, w), lambda i: (i, 0))
    return pl.pallas_call(
        body, name="mix_bwd", grid=(t // tt,),
        in_specs=[row(D), row(D), row(D), row(D), row(DA), row(DP), row(DP), _full((1, D)), _full((1, D)),
                  _full((1, DA)), _full((1, DP)), _resident((D, D)), _full((len(WINS), PC, PC)), _full((1, DP)), ANY],
        out_specs=[row(D), row(DA), row(DP), _full((1, D)), _full((1, D)), _full((1, DA)), _full((1, DP)),
                   _full((1, DP)), _full((len(WINS), PC, PC)), _full((D, D))],
        out_shape=[jax.ShapeDtypeStruct((t, D), F32), jax.ShapeDtypeStruct((t, DA), F32), jax.ShapeDtypeStruct((t, DP), F32),
                   jax.ShapeDtypeStruct((1, D), F32), jax.ShapeDtypeStruct((1, D), F32), jax.ShapeDtypeStruct((1, DA), F32),
                   jax.ShapeDtypeStruct((1, DP), F32), jax.ShapeDtypeStruct((1, DP), F32),
                   jax.ShapeDtypeStruct((len(WINS), PC, PC), F32), jax.ShapeDtypeStruct((D, D), F32)],
        compiler_params=_cp(),
    )(d_hn2, dh2, h1, o, a, m, yb, g_ffn_pre, g_post, g_attn, g_pool, w_out, w_pool, pool_scale, dep)


def _attn_bwd(qa, ka, v, a, d_a, lse, dep):
    t = qa.shape[0]
    ta = _tile(t)
    n = t // ta

    def body(q_ref, k_ref, v_ref, o_ref, do_ref, lse_ref, dep_ref, dq_ref, dk_ref, dv_ref):
        j = pl.program_id(1)

        @pl.when(j == 0)
        def _():
            dq_ref[...] = jnp.zeros_like(dq_ref)

        dk_ref[...] = jnp.zeros_like(dk_ref)
        dv_ref[...] = jnp.zeros_like(dv_ref)
        ks = [k_ref[:, LANE * h:LANE * (h + 1)] for h in range(2)]
        v2 = v_ref[...]
        lane = lax.broadcasted_iota(jnp.int32, (1, LANE), 1)
        mine = [lane < HD, lane >= HD]

        def block(q0, nq, nk, shift):
            rows = pl.ds(pl.multiple_of(q0, nq), nq)
            do2 = do_ref[rows, :]
            prod = do2 * o_ref[rows, :]
            lse2 = lse_ref[rows, :]
            do2b = do2.astype(BF)
            qh = [q_ref[rows, LANE * h:LANE * (h + 1)] for h in range(2)]
            kk, vv = [kh[:nk] for kh in ks], v2[:nk]
            s = [_mm_nt(qh[h], kk[h]) for h in range(2)]
            dp = [_mm_nt(jnp.where(mine[h], do2, 0.0), vv) for h in range(2)]
            delta = [jnp.sum(jnp.where(mine[h], prod, 0.0), axis=1, keepdims=True) for h in range(2)]
            lse_h = [jnp.sum(jnp.where(lane == HD * h, lse2, 0.0), axis=1, keepdims=True) for h in range(2)]
            pr = [jnp.exp(s[h] - lse_h[h]) for h in range(2)]
            if shift is not None:
                keep = (lax.broadcasted_iota(jnp.int32, (nq, nk), 1)
                        <= lax.broadcasted_iota(jnp.int32, (nq, nk), 0) + shift)
                pr = [jnp.where(keep, ph, 0.0) for ph in pr]
            ds = [(pr[h] * (dp[h] - delta[h])).astype(BF) for h in range(2)]
            dv_ref[:nk, :] = dv_ref[:nk, :] + jnp.where(mine[0], _mm_tn(pr[0], do2b), _mm_tn(pr[1], do2b))
            for h in range(2):
                sl = slice(LANE * h, LANE * (h + 1))
                dk_ref[:nk, sl] = dk_ref[:nk, sl] + _mm_tn(ds[h], qh[h])
                dq_ref[0, rows, sl] = dq_ref[0, rows, sl] + jnp.dot(ds[h], kk[h], preferred_element_type=F32)

        def step(i, carry):
            block(i * ta, ta, ta, None)
            return carry

        half = ta // 2
        block(j * ta, half, half, 0)
        block(j * ta + half, half, ta, half)
        lax.fori_loop(j + 1, n, step, 0)

    qrow = lambda w: pl.BlockSpec((t, w), lambda p, j: (0, p))
    krow = lambda w: pl.BlockSpec((ta, w), lambda p, j: (j, p))
    return pl.pallas_call(
        body, name="attn_bwd", grid=(NH // 2, n),
        in_specs=[qrow(2 * LANE), krow(2 * LANE), krow(LANE), qrow(LANE), qrow(LANE), qrow(LANE), ANY],
        out_specs=[pl.BlockSpec((1, t, 2 * LANE), lambda p, j: (p, 0, 0)), krow(2 * LANE), krow(LANE)],
        out_shape=[jax.ShapeDtypeStruct((NH // 2, t, 2 * LANE), F32), jax.ShapeDtypeStruct((t, NH * LANE), F32),
                   jax.ShapeDtypeStruct((t, DA), F32)],
        compiler_params=_cp(),
    )(qa, ka, v, a, d_a, lse, dep)


def _in_bwd(dqa, dka, dv, dyc, fl, x, dh1, g1, w_in_t, tri_u, dep):
    t = x.shape[0]
    tt = _tile(t)
    nt = t // tt
    hb = tt // HALO
    rev = lambda s: nt - 1 - s

    def body(dqa_ref, dka_ref, dv_ref, dyc_ref, dyn_ref, fl_ref, x_ref, dh1_ref, g_ref, w_ref, tri_ref,
             dep_ref, dx_ref, dw_ref, dg_ref, db_ref, carry, acc, stage, sem):
        s = pl.program_id(0)
        i = nt - 1 - s

        @pl.when(s == 0)
        def _():
            carry[...] = jnp.zeros_like(carry)
            acc[...] = jnp.zeros_like(acc)
            dg_ref[...] = jnp.zeros_like(dg_ref)
            db_ref[...] = jnp.zeros_like(db_ref)

        dq_cat = jnp.concatenate([dqa_ref[p] for p in range(NH // 2)], axis=1)
        dk_cat = dka_ref[...]
        off = lax.broadcasted_iota(jnp.int32, (1, NH * LANE), 1)
        off = off % LANE - HD - off // LANE

        def picked(cat, lane_off):
            kept = jnp.where(off == lane_off, cat, 0.0)
            return functools.reduce(lambda a, b: a + b, [kept[:, LANE * h:LANE * (h + 1)] for h in range(NH)])

        dc = pltpu.roll(picked(dq_cat, 0), LANE - HD, 1) - pltpu.roll(picked(dk_cat, 24), LANE - HD - 24, 1)
        dlf = carry[...] + _dot3(tri_ref[...], dc)
        carry[...] = carry[...] + jnp.sum(dc, axis=0, keepdims=True)
        flv = fl_ref[...]
        lane = lax.broadcasted_iota(jnp.int32, flv.shape, 1)
        d_fl = jnp.where(lane < NH, dlf / (1.0 + jnp.exp(flv)), 0.0)
        db_ref[...] = db_ref[...] + jnp.sum(d_fl, axis=0, keepdims=True)
        low = lax.broadcasted_iota(jnp.int32, (tt, LANE), 1) < HD
        dqs, dks = [], []
        for p in range(NH // 2):
            b0, b1 = slice(2 * LANE * p, 2 * LANE * p + LANE), slice(2 * LANE * p + LANE, 2 * LANE * (p + 1))
            dqs.append(jnp.where(low, dq_cat[:, b0], pltpu.roll(dq_cat[:, b1], HD, 1)) * (1.0 / math.sqrt(HD)))
            dks.append(jnp.where(low, dk_cat[:, b0], pltpu.roll(dk_cat[:, b1], HD, 1)))
        nxt = dyn_ref[...] * jnp.where(i < nt - 1, 1.0, 0.0)
        tok = i * tt + lax.broadcasted_iota(jnp.int32, (tt, PC), 0)
        dus = []
        for g, w in enumerate(WINS):
            sl = slice(PC * g, PC * (g + 1))
            dycg = dyc_ref[:, sl]
            ext = jnp.concatenate([dycg, nxt[:, sl]], axis=0)
            dus.append(_window_sum(ext, w, False)[:tt] - dycg * jnp.minimum(tok + 1, w).astype(F32))
        d_z = jnp.concatenate(dqs + dks + [dv_ref[...], d_fl] + dus, axis=1).astype(BF)
        xv = x_ref[...]
        gv = g_ref[...]
        hn = (xv * _rstd(xv) * gv).astype(BF)
        d_hn = jnp.dot(d_z, w_ref[...], preferred_element_type=F32)
        acc[...] = acc[...] + _mm_tn(d_z, hn)
        d1, dg = _rms_bwd(d_hn, xv, gv)
        dg_ref[...] = dg_ref[...] + dg
        dx_ref[...] = dh1_ref[...] + d1

        @pl.when(s == nt - 1)
        def _():
            stage[...] = acc[...].astype(BF)
            cp = pltpu.make_async_copy(stage, dw_ref, sem)
            cp.start()
            cp.wait()

    row = lambda w: pl.BlockSpec((tt, w), lambda s: (rev(s), 0))
    return pl.pallas_call(
        body, name="in_bwd", grid=(nt,),
        in_specs=[pl.BlockSpec((NH // 2, tt, 2 * LANE), lambda s: (0, rev(s), 0)), row(NH * LANE), row(DA), row(DP),
                  pl.BlockSpec((HALO, DP), lambda s: (jnp.minimum((rev(s) + 1) * hb, nt * hb - 1), 0)),
                  row(LANE), row(D), row(D), _full((1, D)), _resident((ZW, D)), _full((tt, tt)),
                  ANY],
        out_specs=[row(D), pl.BlockSpec(memory_space=pl.ANY), _full((1, D)), _full((1, LANE))],
        out_shape=[jax.ShapeDtypeStruct((t, D), F32), jax.ShapeDtypeStruct((ZW, D), BF),
                   jax.ShapeDtypeStruct((1, D), F32), jax.ShapeDtypeStruct((1, LANE), F32)],
        scratch_shapes=[pltpu.VMEM((1, LANE), F32), pltpu.VMEM((ZW, D), F32), pltpu.VMEM((ZW, D), BF),
                        pltpu.SemaphoreType.DMA],
        compiler_params=_cp(),
    )(dqa, dka, dv, dyc, dyc, fl, x, dh1, g1, w_in_t, tri_u, dep)


class _NoComm:
    def __init__(self, w2):
        self.w2 = w2
        self.dep = jnp.zeros((8, LANE), F32)

    def after_attention(self, after):
        return self.dep

    def weights2(self, after):
        return _stack2_full(*self.w2)

    def after_ffn(self, grads2):
        self.grads2 = grads2
        return self.dep

    def after_mix(self, after, early):
        self.early = early
        return self.dep

    def after_attn(self, after):
        return self.dep


def _local_step(x, p, tgt, sm, w1, comm):
    w_in_t, w_out = w1
    tt = _tile(x.shape[0])
    eq, ek, rowq, rowk = _aug_consts()
    b_pad = jnp.pad(sm["b_forget"], ((0, 0), (0, LANE - NH)))
    qa, ka, v, u, fl = _in_proj(x, sm["g_mix_pre"], w_in_t, b_pad, _tri(tt, False), eq, ek, rowq, rowk, comm.dep)
    a, lse = _attn_fwd(qa, ka, v)
    yb, m, o, h1 = _mix_out(a, u, x, sm["w_pool"], sm["pool_scale"], sm["g_attn_grp"],
                            sm["g_pool_grp"], w_out, sm["g_mix_post"], comm.after_attention(a))
    stacks2 = comm.weights2(h1)
    wg_t, wu_t, wd, w_ple, w_pg = _unstack2_full(stacks2)
    hn2, gate, up, dff, dh2, loss, dwpg, dwple, dgple, dgfpost = _ffn_fwd(
        h1, sm["g_ffn_pre"], stacks2, sm["g_ffn_post"], p, w_ple, sm["g_ple"], w_pg, tgt)
    dwg_t, dwu_t, dwd, d_hn2 = _ffn_bwd(hn2, gate, up, dff, wg_t, wu_t, wd)
    dep = comm.after_ffn((dwg_t, dwu_t, dwd, dwple, dwpg))
    dh1, d_a, dyc, dgfpre, dgpost, dgattn, dgpool, dps, dwpool, dwout = _mix_bwd(
        d_hn2, dh2, h1, o, a, m, yb, sm["g_ffn_pre"], sm["g_mix_post"], sm["g_attn_grp"], sm["g_pool_grp"],
        w_out, sm["w_pool"], sm["pool_scale"], dep)
    early = dict(loss=loss[0:1, 0:1], g_attn_grp=dgattn, g_pool_grp=dgpool, w_pool=dwpool, pool_scale=dps,
                 g_mix_post=dgpost, g_ffn_pre=dgfpre, g_ffn_post=dgfpost, g_ple=dgple)
    dqa, dka, dvv = _attn_bwd(qa, ka, v, a, d_a, lse, comm.after_mix(dh1, early))
    dx, dwin_t, dg1, dbf = _in_bwd(dqa, dka, dvv, dyc, fl, x, dh1, sm["g_mix_pre"], w_in_t, _tri(tt, True),
                                   comm.after_attn(dvv))
    return dx, (dwin_t, dwout), dict(g_mix_pre=dg1, b_forget=dbf[:, :NH])


def _place():
    x, y, c = lax.axis_index("x"), lax.axis_index("y"), lax.axis_index("c")
    return x, y, c, [(1 - x, y), (x, 1 - y), (1 - x, 1 - y)]


def _rows(c, h):
    return pl.ds(pl.multiple_of(c * h, 16), h)


def _plan_gather(h):
    def plan(src, land):
        x, y, c, chips = _place()
        return [(src.at[_rows(c, h), :], land.at[2 * x + y, _rows(c, h), :], (cx, cy, c),
                 land.at[2 * cx + cy, _rows(c, h), :]) for cx, cy in chips]
    return plan


def _plan_forward(h):
    def plan(land_in, own, land):
        x, y, c, chips = _place()
        sib, me = (x, y, 1 - c), 2 * x + y
        return ([(land_in.at[2 * cx + cy, _rows(c, h), :], land.at[2 * cx + cy, _rows(c, h), :], sib,
                  land.at[2 * cx + cy, _rows(1 - c, h), :]) for cx, cy in chips]
                + [(own, land.at[me], sib, land.at[me])])
    return plan


def _plan_swap_halves(h):
    def plan(buf_in, buf):
        x, y, c, _ = _place()
        return [(buf_in.at[_rows(c, h), :], buf.at[_rows(c, h), :], (x, y, 1 - c), buf.at[_rows(1 - c, h), :])]
    return plan


def _plan_pair_rows(h):
    def plan(src, land):
        x, y, c, _ = _place()
        return [(src.at[:, _rows(1 - c, h), :], land, (x, y, 1 - c), land)]
    return plan


def _plan_scatter(src, land):
    x, y, c, chips = _place()
    return [(src.at[2 * cx + cy], land.at[k], (cx, cy, c), land.at[k]) for k, (cx, cy) in enumerate(chips)]


def _plan_scatter8(h):
    def plan(src, land):
        x, y, c, chips = _place()
        copies = [(src.at[2 * x + y, _rows(1 - c, h), :], land.at[0], (x, y, 1 - c), land.at[0])]
        for k, (cx, cy) in enumerate(chips):
            for d, other in enumerate((c, 1 - c)):
                copies.append((src.at[2 * cx + cy, _rows(other, h), :], land.at[1 + 2 * k + c], (cx, cy, other),
                               land.at[1 + 2 * k + other]))
        return copies
    return plan


def _plan_all(src, land):
    x, y, c, _ = _place()
    copies = []
    for r in range(1, 8):
        px, py, pc = (1 - a if b else a for a, b in zip((x, y, c), (r >> 2 & 1, r >> 1 & 1, r & 1)))
        copies.append((src, land.at[4 * x + 2 * y + c], (px, py, pc), land.at[4 * px + 2 * py + pc]))
    return copies


def _remote(src, dst, send_sems, recv_sems, k, peer):
    return pltpu.make_async_remote_copy(src_ref=src, dst_ref=dst, send_sem=send_sems.at[k], recv_sem=recv_sems.at[k],
                                        device_id=peer, device_id_type=MESH)


def _exchange_inplace(name, n, buf, extra, plan):
    def body(*refs):
        ins, buf_ref, send_sems, recv_sems = refs[:1 + len(extra)], refs[1 + len(extra)], refs[-2], refs[-1]
        copies = plan(*ins, buf_ref)
        for k, (s, d, peer, _) in enumerate(copies):
            _remote(s, d, send_sems, recv_sems, k, peer).start()
        for k, (s, _, peer, mine) in enumerate(copies):
            _remote(s, mine, send_sems, recv_sems, k, peer).wait_recv()
        for k, (s, d, peer, _) in enumerate(copies):
            _remote(s, d, send_sems, recv_sems, k, peer).wait_send()

    return pl.pallas_call(
        body, name=name, in_specs=[ANY] * (1 + len(extra)), out_specs=ANY, out_shape=_sds(buf.shape, buf.dtype),
        input_output_aliases={0: 0},
        scratch_shapes=[pltpu.SemaphoreType.DMA((n,)), pltpu.SemaphoreType.DMA((n,))],
    )(buf, *extra)


HBM = pl.BlockSpec(memory_space=pltpu.HBM)
SEM = pl.BlockSpec(memory_space=pltpu.SEMAPHORE)
EFFECT = pltpu.SideEffectType.DATAFLOW_SIDE_EFFECTING


def _exchange_start(name, n, src, land, plan):
    def body(src_ref, land_ref, send_sems, recv_sems, src_thru, land_thru, token):
        for k, (s, d, peer, _) in enumerate(plan(src_ref, land_ref)):
            _remote(s, d, send_sems, recv_sems, k, peer).start()
        token[...] = jnp.zeros_like(token)

    return pl.pallas_call(
        body, name=name,
        out_shape=(pltpu.SemaphoreType.DMA((n,)), pltpu.SemaphoreType.DMA((n,)), pltpu.HBM(src.shape, src.dtype),
                   pltpu.HBM(land.shape, land.dtype), jax.ShapeDtypeStruct((8, LANE), F32)),
        in_specs=(HBM, HBM), out_specs=(SEM, SEM, HBM, HBM, pl.BlockSpec(memory_space=pltpu.VMEM)),
        input_output_aliases={0: 2, 1: 3},
        compiler_params=pltpu.CompilerParams(has_side_effects=EFFECT),
    )(pltpu.with_memory_space_constraint(src, pltpu.HBM), pltpu.with_memory_space_constraint(land, pltpu.HBM))


def _exchange_wait(name, started, plan, after):
    send_sems, recv_sems, src, land, _ = started

    def body(src_ref, land_ref, send_sems, recv_sems, after_ref, src_out, land_out):
        for k, (s, _, peer, mine) in enumerate(plan(src_ref, land_ref)):
            cp = _remote(s, mine, send_sems, recv_sems, k, peer)
            cp.wait_send()
            cp.wait_recv()

    return pl.pallas_call(
        body, name=name, out_shape=(pltpu.HBM(src.shape, src.dtype), pltpu.HBM(land.shape, land.dtype)),
        in_specs=(HBM, HBM, SEM, SEM, ANY), out_specs=(HBM, HBM), input_output_aliases={0: 0, 1: 1},
        compiler_params=pltpu.CompilerParams(has_side_effects=EFFECT),
    )(src, land, send_sems, recv_sems, after)


def _pair_sum(name, cidx, g, recv, br):
    h = recv.shape[1]
    nb = h // br

    def body(c_ref, g_ref, r_ref, out_ref):
        out_ref[...] = (g_ref[...].astype(F32) + r_ref[...].astype(F32)).astype(BF)

    return pl.pallas_call(
        body, name=name,
        grid_spec=pltpu.PrefetchScalarGridSpec(
            num_scalar_prefetch=1, grid=(NSHARD, nb),
            in_specs=[pl.BlockSpec((1, br, D), lambda s, i, c: (s, c[0] * nb + i, 0)),
                      pl.BlockSpec((1, br, D), lambda s, i, c: (s, i, 0))],
            out_specs=pl.BlockSpec((1, br, D), lambda s, i, c: (s, i, 0))),
        out_shape=jax.ShapeDtypeStruct((NSHARD, h, D), BF),
    )(cidx, g, recv)


def _chip_sum(name, place, pb, y, br):
    h = y.shape[1]
    nb = h // br

    def body(pl_ref, p_ref, y_ref, out_ref):
        acc = p_ref[0].astype(F32)
        for k in range(NSHARD - 1):
            acc = acc + y_ref[k].astype(F32)
        out_ref[...] = acc

    return pl.pallas_call(
        body, name=name,
        grid_spec=pltpu.PrefetchScalarGridSpec(
            num_scalar_prefetch=1, grid=(nb,),
            in_specs=[pl.BlockSpec((1, br, D), lambda i, s: (s[0], i, 0)),
                      pl.BlockSpec((NSHARD - 1, br, D), lambda i, s: (0, i, 0))],
            out_specs=pl.BlockSpec((br, D), lambda i, s: (s[1] * nb + i, 0))),
        out_shape=jax.ShapeDtypeStruct((2 * h, D), F32),
    )(place, pb, y)


def _sum8(name, place, g, land, br):
    h = land.shape[1]
    nb = h // br

    def body(pl_ref, g_ref, y_ref, out_ref):
        acc = g_ref[0].astype(F32)
        for k in range(land.shape[0]):
            acc = acc + y_ref[k].astype(F32)
        out_ref[...] = acc

    return pl.pallas_call(
        body, name=name,
        grid_spec=pltpu.PrefetchScalarGridSpec(
            num_scalar_prefetch=1, grid=(nb,),
            in_specs=[pl.BlockSpec((1, br, D), lambda i, s: (s[0], s[1] * nb + i, 0)),
                      pl.BlockSpec((land.shape[0], br, D), lambda i, s: (0, i, 0))],
            out_specs=pl.BlockSpec((br, D), lambda i, s: (s[1] * nb + i, 0))),
        out_shape=jax.ShapeDtypeStruct((2 * h, D), F32), compiler_params=_cp(),
    )(place, g, land)


def _sum_slots(name, v):
    def body(in_ref, out_ref):
        acc = in_ref[0]
        for k in range(1, 8):
            acc = acc + in_ref[k]
        out_ref[...] = acc

    vm = pl.BlockSpec(memory_space=pltpu.VMEM)
    return pl.pallas_call(body, name=name, in_specs=[vm], out_specs=vm,
                          out_shape=jax.ShapeDtypeStruct(v.shape[1:], F32))(v)


def _adamw_math(w, g, m, v):
    m = ADAM_B1 * m + (1.0 - ADAM_B1) * g
    v = ADAM_B2 * v + (1.0 - ADAM_B2) * (g * g)
    m_hat = m / (1.0 - ADAM_B1 ** ADAM_STEP)
    v_hat = v / (1.0 - ADAM_B2 ** ADAM_STEP)
    delta = -ADAM_LR * (m_hat / (jnp.sqrt(v_hat) + ADAM_EPS) + ADAM_WD * w)
    return delta, m, v


def _adamw(w, g, m, v, dep, row0=None):
    r, c = w.shape
    br = next(b for b in (352, 256, 128, r) if r % b == 0 and (row0 or 0) % b == 0)
    first = (row0 or 0) // br

    def body(w_ref, g_ref, m_ref, v_ref, dep_ref, *outs):
        gv = g_ref[...]
        outs[-3][...], outs[-2][...], outs[-1][...] = _adamw_math(w_ref[...], gv, m_ref[...], v_ref[...])
        if row0 is not None:
            outs[0][...] = gv

    spec = pl.BlockSpec((br, c), lambda i: (i, 0))
    n_out = 3 if row0 is None else 4
    out = pl.pallas_call(
        body, name="adamw", grid=(r // br,),
        in_specs=[spec, pl.BlockSpec((br, c), lambda i: (first + i, 0)), spec, spec, ANY], out_specs=[spec] * n_out,
        out_shape=[jax.ShapeDtypeStruct((r, c), F32)] * n_out, compiler_params=_cp(),
    )(w, g, m, v, dep)
    return out if row0 is not None else [g] + list(out)


def _adamw_small(ws, gs, ms, vs):
    n = len(ws)

    def body(*refs):
        ins, outs = refs[:4 * n], refs[4 * n:]
        for k in range(n):
            d, m, v = _adamw_math(ins[k][...], ins[n + k][...], ins[2 * n + k][...], ins[3 * n + k][...])
            outs[k][...] = d
            outs[n + k][...] = m
            outs[2 * n + k][...] = v

    vm = pl.BlockSpec(memory_space=pltpu.VMEM)
    out = pl.pallas_call(
        body, name="adamw_small", in_specs=[vm] * (4 * n), out_specs=[vm] * (3 * n),
        out_shape=[jax.ShapeDtypeStruct(w.shape, F32) for w in ws] * 3,
    )(*ws, *gs, *ms, *vs)
    return out[:n], out[n:2 * n], out[2 * n:]


BIG = ("w_in", "w_out", "w_ffn_gate", "w_ffn_up", "w_ffn_down", "w_ple_proj", "w_ple_gate")
SMALL = ("g_mix_pre", "b_forget", "g_attn_grp", "g_pool_grp", "w_pool", "pool_scale", "g_mix_post", "g_ffn_pre",
         "g_ffn_post", "g_ple")
TRANSPOSED = ("w_in", "w_ffn_gate", "w_ffn_up")
VECTORS = tuple(n for n in SMALL if n != "w_pool")
ORDER = ("g_mix_pre", "w_in", "b_forget", "g_attn_grp", "g_pool_grp", "w_pool", "pool_scale", "w_out", "g_mix_post",
         "g_ffn_pre", "w_ffn_gate", "w_ffn_up", "w_ffn_down", "g_ffn_post", "w_ple_proj", "g_ple", "w_ple_gate")


def _pad_rows(a, rows):
    return jnp.pad(a, ((0, rows - a.shape[0]), (0, 0)))


def _stack1(w_in, w_out):
    return _pad_rows(jnp.concatenate([_pad_rows(w_in.T, IN_PAD), w_out], axis=0), ROWS1)


def _stack2(wg, wu, wd, wple, wpg):
    return _pad_rows(jnp.concatenate([wg.T, wu.T, wd, wple.reshape(DPLE // NSHARD, D), wpg], axis=0), ROWS2)


def _unstack1(s):
    return s[:IN_SH], s[O1_OUT:USED1]


def _cat(g, lo, hi):
    return g[:, lo:hi].reshape(NSHARD * (hi - lo), D)


def _unstack1_full(g):
    w_in_t = _cat(g, 0, IN_SH)
    w_in_t = jnp.concatenate([w_in_t[:3 * DA], _pad_rows(w_in_t[3 * DA:3 * DA + NH], LANE), w_in_t[3 * DA + NH:]], axis=0)
    return w_in_t, _cat(g, O1_OUT, USED1)


def _unstack2_full(g):
    w_ple = g[:, O2_PLE:O2_PG].reshape(NSHARD, DPLE, DPLE).transpose(1, 0, 2).reshape(DPLE, D)
    return _cat(g, 0, O2_U), _cat(g, O2_U, O2_D), _cat(g, O2_D, O2_PLE), w_ple, _cat(g, O2_PG, USED2)


def _shards(a):
    return a.reshape(NSHARD, a.shape[0] // NSHARD, D)


def _stack1_full(dwin_t, dwout):
    dwin_t = jnp.concatenate([dwin_t[:3 * DA + NH], dwin_t[3 * DA + LANE:]], axis=0).reshape(NSHARD, IN_SH, D)
    zeros = lambda r: jnp.zeros((NSHARD, r, D), dwin_t.dtype)
    return jnp.concatenate([dwin_t, zeros(IN_PAD - IN_SH), _shards(dwout), zeros(ROWS1 - USED1)], axis=1)


def _stack2_full(dwg_t, dwu_t, dwd, dwple, dwpg):
    dwple = dwple.reshape(DPLE, NSHARD, DPLE).transpose(1, 0, 2).reshape(NSHARD, DPLE // NSHARD, D)
    return jnp.concatenate([_shards(dwg_t), _shards(dwu_t), _shards(dwd), dwple, _shards(dwpg),
                            jnp.zeros((NSHARD, ROWS2 - USED2, D), dwd.dtype)], axis=1)


def _sds(shape, dtype):
    return jax.ShapeDtypeStruct(shape, dtype)


class _Comm:
    def __init__(self, stack2, me, c):
        self.stack2, self.me, self.c = stack2, me, c
        self.cidx = c.astype(jnp.int32).reshape(1)
        self.place = jnp.stack([me, c]).astype(jnp.int32)
        self.h = ROWS2 // 2
        self.gather = _exchange_start("gather2_start", 3, stack2, lax.empty((NSHARD, ROWS2, D), BF), _plan_gather(self.h))
        self.dep = self.gather[4]

    def after_attention(self, after):
        own, land = _exchange_wait("gather2_wait", self.gather, _plan_gather(self.h), after)
        fwd = _plan_forward(self.h)
        self.forward = lambda own_ref, land_ref: fwd(land_ref, own_ref, land_ref)
        self.passing = _exchange_start("forward2_start", 4, own, land, self.forward)
        return self.passing[4]

    def weights2(self, after):
        return _exchange_wait("forward2_wait", self.passing, self.forward, after)[1]

    def after_ffn(self, grads2):
        g = _stack2_full(*[a.astype(BF) for a in grads2])
        self.scatter = _plan_scatter8(self.h)
        self.chip = _exchange_start("reduce2_start", 7, g, lax.empty((7, self.h, D), BF), self.scatter)
        return self.chip[4]

    def after_mix(self, after, early):
        self.early_shapes = {n: early[n].shape for n in early}
        self.small = self.start_small("small", early)
        return self.small[4]

    def after_attn(self, after):
        g, y = _exchange_wait("reduce2_wait", self.chip, self.scatter, after)
        f = _sum8("sum2", self.place, g, y, RED2 // 2)
        self.early = self.finish_small("small", self.small, self.early_shapes, after)
        swap = _plan_swap_halves(self.h)
        self.swap = lambda _, buf: swap(buf, buf)
        self.swapping = _exchange_start("reduce2_gather_start", 1, self.dep, f, self.swap)
        return self.swapping[4]

    def reduced2(self, after):
        return _exchange_wait("reduce2_gather_wait", self.swapping, self.swap, after)[1]

    def start_small(self, name, small):
        v = _pack_small(small)
        return _exchange_start(name + "_start", 7, v, lax.empty((8,) + v.shape, F32), _plan_all)

    def finish_small(self, name, started, shapes, after):
        v, land = _exchange_wait(name + "_wait", started, _plan_all, after)
        land = lax.dynamic_update_slice(land, v[None], (2 * self.me + self.c, 0, 0))
        return _unpack_small(_sum_slots(name + "_sum", land), shapes)


def _pack_small(small):
    parts = []
    for name in small:
        flat = small[name].reshape(-1)
        parts.append(jnp.pad(flat, (0, -flat.shape[0] % LANE)).reshape(-1, LANE))
    v = jnp.concatenate(parts, axis=0)
    return _pad_rows(v, v.shape[0] + (-v.shape[0] % 8))


def _unpack_small(v, shapes):
    out, r = {}, 0
    for name in shapes:
        n = math.prod(shapes[name])
        rows = -(-n // LANE)
        out[name] = v[r:r + rows].reshape(-1)[:n].reshape(shapes[name])
        r += rows
    return out


def kernel(x, p, g_mix_pre, w_in, b_forget, g_attn_grp, g_pool_grp, w_pool, pool_scale, w_out, g_mix_post, g_ffn_pre, w_ffn_gate, w_ffn_up, w_ffn_down, g_ffn_post, w_ple_proj, g_ple, w_ple_gate, loss_target, m_g_mix_pre, m_w_in, m_b_forget, m_g_attn_grp, m_g_pool_grp, m_w_pool, m_pool_scale, m_w_out, m_g_mix_post, m_g_ffn_pre, m_w_ffn_gate, m_w_ffn_up, m_w_ffn_down, m_g_ffn_post, m_w_ple_proj, m_g_ple, m_w_ple_gate, v_g_mix_pre, v_w_in, v_b_forget, v_g_attn_grp, v_g_pool_grp, v_w_pool, v_pool_scale, v_w_out, v_g_mix_post, v_g_ffn_pre, v_w_ffn_gate, v_w_ffn_up, v_w_ffn_down, v_g_ffn_post, v_w_ple_proj, v_g_ple, v_w_ple_gate):
    args = dict(locals())
    strip = lambda n, a: a if n in VECTORS else a[0]
    w = {n: strip(n, args[n]) for n in ORDER}
    mom = {n: strip(n, args["m_" + n]) for n in ORDER}
    var = {n: strip(n, args["v_" + n]) for n in ORDER}
    sm = {n: w[n] for n in SMALL}

    c = lax.axis_index("c")
    me = 2 * lax.axis_index("x") + lax.axis_index("y")
    h1 = ROWS1 // 2
    bf = lambda n: w[n].astype(BF)
    stack1 = _stack1(bf("w_in"), bf("w_out"))
    gather1 = _exchange_start("gather1_start", 3, stack1, lax.empty((NSHARD, ROWS1, D), BF), _plan_gather(h1))
    _, *shards2 = lax.optimization_barrier((gather1[4], *[w[n] for n in BIG[2:]]))
    stack2 = _stack2(*[a.astype(BF) for a in shards2])
    stack1, land = _exchange_wait("gather1_wait", gather1, _plan_gather(h1), stack2)
    land, stack2 = lax.optimization_barrier((land, stack2))
    comm = _Comm(stack2, me, c)
    w1 = _unstack1_full(_exchange_inplace("gather1_forward", 4, land, (stack1,), _plan_forward(h1)))
    dx, grads1, late = _local_step(x[0], p[0, 0], loss_target[0], sm, w1, comm)

    flip = lambda n, a: a.T if n in TRANSPOSED else a
    grads, delta, new_m, new_v = {}, {}, {}, {}

    def update(n, g, dep, row0=None):
        g_, d_, m_, v_ = _adamw(flip(n, w[n]), g, flip(n, mom[n]), flip(n, var[n]), dep, row0)
        grads[n], delta[n], new_m[n], new_v[n] = flip(n, g_), flip(n, d_), flip(n, m_), flip(n, v_)
        return v_

    late_shapes = {n: late[n].shape for n in late}
    small2 = comm.start_small("small2", late)
    red2 = comm.reduced2(dx)
    g1 = _stack1_full(grads1[0], grads1[1].astype(BF))
    pair1 = _exchange_start("reduce1_pair_start", 1, g1, lax.empty((NSHARD, h1, D), BF), _plan_pair_rows(h1))
    dep = update("w_ple_gate", red2, pair1[4] + small2[4], O2_PG)
    dep = update("w_ple_proj", red2[O2_PLE:O2_PG].reshape(DPLE, DPLE), dep)
    g1, recv = _exchange_wait("reduce1_pair_wait", pair1, _plan_pair_rows(h1), dep)
    pb = _pair_sum("pair_sum1", comm.cidx, g1, recv, RED1)
    chip1 = _exchange_start("reduce1_chip_start", 3, pb, lax.empty((NSHARD - 1, h1, D), BF), _plan_scatter)
    dep = update("w_ffn_gate", red2, chip1[4], 0)
    dep = update("w_ffn_up", red2, dep, O2_U)
    dep = update("w_ffn_down", red2, dep, O2_D)
    red_small = {**comm.early, **comm.finish_small("small2", small2, late_shapes, dep)}
    loss = 0.5 / D * red_small["loss"][0, 0]
    for n in SMALL:
        grads[n] = red_small[n].reshape(w[n].shape)
    two_d = lambda a: a.reshape(-1, a.shape[-1])
    ds, ms, vs = _adamw_small([two_d(w[n]) for n in SMALL], [two_d(grads[n]) for n in SMALL],
                              [two_d(mom[n]) for n in SMALL], [two_d(var[n]) for n in SMALL])
    for k, n in enumerate(SMALL):
        delta[n], new_m[n], new_v[n] = ds[k].reshape(w[n].shape), ms[k].reshape(w[n].shape), vs[k].reshape(w[n].shape)
    pb, y = _exchange_wait("reduce1_chip_wait", chip1, _plan_scatter, vs[0])
    f = _chip_sum("chip_sum1", comm.place, pb, y, RED1)
    reduced1 = _exchange_inplace("reduce1_gather", 1, f, (), _plan_swap_halves(h1))
    g_in, g_out = _unstack1(reduced1)
    update("w_out", g_out, update("w_in", g_in, reduced1))

    lead = lambda d: [d[n] if n in VECTORS else d[n][None] for n in ORDER]
    return (loss, dx[None], *lead(grads), *lead(delta), *lead(new_m), *lead(new_v))
```

```python
import functools
import math

import jax
import jax.numpy as jnp
import numpy as np
from jax import lax
from jax.experimental import pallas as pl
from jax.experimental.pallas import tpu as pltpu

F32 = jnp.float32
BF = jnp.bfloat16
MESH = pl.DeviceIdType.MESH

D = 1024
DA = 512
DP = 512
NH = 8
HD = 64
DFF = 2816
DPLE = 256
WINS = (2, 4, 8, 16)
PC = 128
ZW = 3 * DA + 128 + DP
EPS = 1e-6
NSHARD = 4

LANE = 128
HALO = 128

IN_SH = 514
IN_PAD = 528
FF_SH = DFF // NSHARD
O1_OUT, USED1, ROWS1 = 528, 784, 800
O2_U, O2_D, O2_PLE, O2_PG, USED2, ROWS2 = 704, 1408, 2112, 2176, 2432, 2560
RED1, RED2 = 400, 640

ADAM_LR, ADAM_B1, ADAM_B2, ADAM_EPS, ADAM_WD, ADAM_STEP = 0.001, 0.9, 0.999, 1e-8, 0.01, 10

VMEM_LIMIT = 56 * 1024 * 1024
FFN_FWD_TOKENS = 256
FFN_BWD_TOKENS = 1024


def _cp(**kw):
    return pltpu.CompilerParams(vmem_limit_bytes=VMEM_LIMIT, **kw)


def _mm(a, b):
    return jnp.dot(a.astype(BF), b.astype(BF), preferred_element_type=F32)


def _mm_nt(a, b):
    return lax.dot_general(a.astype(BF), b.astype(BF), (((1,), (1,)), ((), ())), preferred_element_type=F32)


def _mm_tn(a, b):
    return lax.dot_general(a.astype(BF), b.astype(BF), (((0,), (0,)), ((), ())), preferred_element_type=F32)


def _split3(x):
    hi = x.astype(BF)
    r = x - hi.astype(F32)
    mid = r.astype(BF)
    lo = (r - mid.astype(F32)).astype(BF)
    return hi, mid, lo


def _dot3(m, x):
    hi, mid, lo = _split3(x)
    return (jnp.dot(m, hi, preferred_element_type=F32) + jnp.dot(m, mid, preferred_element_type=F32)
            + jnp.dot(m, lo, preferred_element_type=F32))


def _window_sum(ext, w, back):
    n = ext.shape[0]
    s, k = ext, 1
    while k < w:
        s = s + pltpu.roll(s, k if back else n - k, 0)
        k *= 2
    return s


def _rstd(x):
    return lax.rsqrt(jnp.mean(x * x, axis=-1, keepdims=True) + EPS)


def _rms_bwd(dy, x, g):
    r = _rstd(x)
    xh = x * r
    dg = jnp.sum(dy * xh, axis=0, keepdims=True)
    dxh = dy * g
    dx = r * (dxh - xh * jnp.mean(dxh * xh, axis=-1, keepdims=True))
    return dx, dg


def _sigmoid(x):
    return 1.0 / (1.0 + jnp.exp(-x))


ANY = pl.BlockSpec(memory_space=pl.ANY)


def _full(shape):
    n = len(shape)
    return pl.BlockSpec(shape, lambda *_: (0,) * n)


def _resident(shape):
    n = len(shape)
    return pl.BlockSpec(shape, lambda *_: (0,) * n, pipeline_mode=pl.Buffered(1))


def _tile(t):
    return 512 if t % 512 == 0 else t


def _tri(n, upper):
    r, c = np.indices((n, n))
    return ((c >= r) if upper else (c <= r)).astype(BF)


def _aug_consts():
    row, col = np.indices((LANE, NH * LANE))
    piece, head = row // NH, row % NH
    ch, cl = col // LANE, col % LANE
    eq = ((piece < 3) & (head == ch) & (cl == HD + 8 * piece + head)).astype(BF)
    ek = -((piece < 3) & (head == ch) & (cl == HD + 24 + 8 * piece + head)).astype(BF)
    off = (np.arange(NH * LANE) % LANE - HD - np.arange(NH * LANE) // LANE)[None, :]
    rowq = ((off >= 24) & (off < 48) & (off % 8 == 0)).astype(np.float32)
    rowk = ((off >= 0) & (off < 24) & (off % 8 == 0)).astype(np.float32)
    return eq, ek, rowq, rowk


def _in_proj(x, g1, w_in_t, b_pad, tri, eq, ek, rowq, rowk, dep):
    t = x.shape[0]
    tt = _tile(t)

    def body(x_ref, g_ref, w_ref, b_ref, tri_ref, eq_ref, ek_ref, rq_ref, rk_ref, dep_ref,
             qa_ref, ka_ref, v_ref, u_ref, fl_ref, carry):
        i = pl.program_id(0)

        @pl.when(i == 0)
        def _():
            carry[...] = jnp.zeros_like(carry)

        xv = x_ref[...]
        hn = (xv * _rstd(xv) * g_ref[...]).astype(BF)
        z = _mm_nt(hn, w_ref[...])
        fl = z[:, 3 * DA:3 * DA + LANE] + b_ref[...]
        lane = lax.broadcasted_iota(jnp.int32, fl.shape, 1)
        lf = jnp.where(lane < NH, jnp.minimum(fl, 0.0) - jnp.log(1.0 + jnp.exp(-jnp.abs(fl))), 0.0)
        c = carry[...] + _dot3(tri_ref[...], lf)
        carry[...] = carry[...] + jnp.sum(lf, axis=0, keepdims=True)
        hi, mid, lo = _split3(c)
        caug = (hi.astype(F32) + pltpu.roll(mid.astype(F32), NH, 1) + pltpu.roll(lo.astype(F32), 2 * NH, 1)).astype(BF)
        aug_q = jnp.dot(caug, eq_ref[...], preferred_element_type=F32) + rq_ref[...]
        aug_k = jnp.dot(caug, ek_ref[...], preferred_element_type=F32) + rk_ref[...]
        low = lax.broadcasted_iota(jnp.int32, (tt, LANE), 1) < HD
        for p in range(NH // 2):
            qp = z[:, LANE * p:LANE * (p + 1)] * (1.0 / math.sqrt(HD))
            kp = z[:, DA + LANE * p:DA + LANE * (p + 1)]
            for h, (qh, kh) in enumerate(((qp, kp), (pltpu.roll(qp, HD, 1), pltpu.roll(kp, HD, 1)))):
                lo_, hi_ = LANE * (2 * p + h), LANE * (2 * p + h + 1)
                qa_ref[:, lo_:hi_] = jnp.where(low, qh, aug_q[:, lo_:hi_]).astype(BF)
                ka_ref[:, lo_:hi_] = jnp.where(low, kh, aug_k[:, lo_:hi_]).astype(BF)
        v_ref[...] = z[:, 2 * DA:3 * DA].astype(BF)
        u_ref[...] = z[:, 3 * DA + LANE:]
        fl_ref[...] = fl

    return pl.pallas_call(
        body, name="in_proj", grid=(t // tt,),
        in_specs=[pl.BlockSpec((tt, D), lambda i: (i, 0)), _full((1, D)), _resident((ZW, D)), _full((1, LANE)),
                  _full((tt, tt)), _full((LANE, NH * LANE)), _full((LANE, NH * LANE)),
                  _full((1, NH * LANE)), _full((1, NH * LANE)), ANY],
        out_specs=[pl.BlockSpec((tt, NH * LANE), lambda i: (i, 0)), pl.BlockSpec((tt, NH * LANE), lambda i: (i, 0)),
                   pl.BlockSpec((tt, DA), lambda i: (i, 0)), pl.BlockSpec((tt, DP), lambda i: (i, 0)),
                   pl.BlockSpec((tt, LANE), lambda i: (i, 0))],
        out_shape=[jax.ShapeDtypeStruct((t, NH * LANE), BF), jax.ShapeDtypeStruct((t, NH * LANE), BF),
                   jax.ShapeDtypeStruct((t, DA), BF), jax.ShapeDtypeStruct((t, DP), F32),
                   jax.ShapeDtypeStruct((t, LANE), F32)],
        scratch_shapes=[pltpu.VMEM((1, LANE), F32)],
        compiler_params=_cp(),
    )(x, g1, w_in_t, b_pad, tri, eq, ek, rowq, rowk, dep)


def _attn_fwd(qa, ka, v):
    t = qa.shape[0]
    ta = _tile(t)
    n = t // ta

    def body(q_ref, k_ref, v_ref, a_ref, lse_ref, m_ref, l_ref, acc_ref):
        i = pl.program_id(1)
        m_ref[...] = jnp.full_like(m_ref, -1e30)
        l_ref[...] = jnp.zeros_like(l_ref)
        acc_ref[...] = jnp.zeros_like(acc_ref)
        qs = [q_ref[:, LANE * h:LANE * (h + 1)] for h in range(2)]
        reps = ta // LANE

        def tile(j, width, masked):
            rows = pl.ds(pl.multiple_of(j * ta, ta), width * ta)
            v2 = v_ref[rows, :]
            s = [_mm_nt(qs[h], k_ref[rows, LANE * h:LANE * (h + 1)]) for h in range(2)]
            if masked:
                keep = (lax.broadcasted_iota(jnp.int32, (ta, ta), 1) <= lax.broadcasted_iota(jnp.int32, (ta, ta), 0))
                s = [jnp.where(keep, sh, -1e30) for sh in s]
            m_old = [m_ref[h] for h in range(2)]
            m_new = [jnp.maximum(m_old[h], jnp.max(s[h], axis=1, keepdims=True)) for h in range(2)]
            pe = [jnp.exp(s[h] - jnp.tile(m_new[h], (1, width * reps))) for h in range(2)]
            alpha = [jnp.exp(m_old[h] - m_new[h]) for h in range(2)]
            pv = [jnp.dot(pe[h].astype(BF), v2, preferred_element_type=F32) for h in range(2)]
            for h in range(2):
                l_ref[h] = alpha[h] * l_ref[h] + jnp.sum(pe[h], axis=1, keepdims=True)
                acc_ref[h] = alpha[h] * acc_ref[h] + pv[h]
                m_ref[h] = m_new[h]

        def step(jj, carry):
            tile(2 * jj, 2, False)
            return carry

        lax.fori_loop(0, i // 2, step, 0)

        @pl.when(i % 2 == 1)
        def _():
            tile(i - 1, 1, False)

        tile(i, 1, True)
        low = lax.broadcasted_iota(jnp.int32, (ta, LANE), 1) < HD
        a_ref[...] = jnp.where(low, acc_ref[0] / l_ref[0], acc_ref[1] / l_ref[1])
        lse_ref[...] = jnp.where(low, m_ref[0] + jnp.log(l_ref[0]), m_ref[1] + jnp.log(l_ref[1]))

    return pl.pallas_call(
        body, name="attn_fwd", grid=(NH // 2, n),
        in_specs=[pl.BlockSpec((ta, 2 * LANE), lambda p, i: (i, p)),
                  pl.BlockSpec((t, 2 * LANE), lambda p, i: (0, p)),
                  pl.BlockSpec((t, LANE), lambda p, i: (0, p))],
        out_specs=[pl.BlockSpec((ta, LANE), lambda p, i: (i, p)), pl.BlockSpec((ta, LANE), lambda p, i: (i, p))],
        out_shape=[jax.ShapeDtypeStruct((t, DA), F32), jax.ShapeDtypeStruct((t, DA), F32)],
        scratch_shapes=[pltpu.VMEM((2, ta, LANE), F32), pltpu.VMEM((2, ta, LANE), F32), pltpu.VMEM((2, ta, LANE), F32)],
        compiler_params=_cp(),
    )(qa, ka, v)


def _mix_out(a, u, x, w_pool, pool_scale, g_attn, g_pool, w_out, g_post, dep):
    t = a.shape[0]
    tt = _tile(t)
    hb = tt // HALO

    def body(a_ref, u_ref, up_ref, x_ref, wp_ref, ps_ref, ga_ref, gp_ref, wo_ref, go_ref, dep_ref,
             yb_ref, m_ref, o_ref, h1_ref):
        i = pl.program_id(0)
        prev = up_ref[...] * jnp.where(i > 0, 1.0, 0.0)
        tok = i * tt + lax.broadcasted_iota(jnp.int32, (tt, PC), 0)
        ms = []
        for g, w in enumerate(WINS):
            ug = u_ref[:, PC * g:PC * (g + 1)]
            ext = jnp.concatenate([prev[:, PC * g:PC * (g + 1)], ug], axis=0)
            cnt = jnp.minimum(tok + 1, w).astype(F32)
            y = (_window_sum(ext, w, True)[HALO:] / cnt - ug).astype(BF)
            yb_ref[:, PC * g:PC * (g + 1)] = y
            ms.append(_mm(y, wp_ref[g]) * ps_ref[:, PC * g:PC * (g + 1)])
        m = jnp.concatenate(ms, axis=1)
        m_ref[...] = m
        av = a_ref[...]
        mix = jnp.concatenate([av * _rstd(av) * ga_ref[...], m * _rstd(m) * gp_ref[...]], axis=1)
        o = _mm(mix, wo_ref[...])
        o_ref[...] = o
        h1_ref[...] = x_ref[...] + o * _rstd(o) * go_ref[...]

    return pl.pallas_call(
        body, name="mix_out", grid=(t // tt,),
        in_specs=[pl.BlockSpec((tt, DA), lambda i: (i, 0)), pl.BlockSpec((tt, DP), lambda i: (i, 0)),
                  pl.BlockSpec((HALO, DP), lambda i: (jnp.maximum(i * hb - 1, 0), 0)),
                  pl.BlockSpec((tt, D), lambda i: (i, 0)),
                  _full((len(WINS), PC, PC)), _full((1, DP)), _full((1, DA)), _full((1, DP)),
                  _resident((D, D)), _full((1, D)), ANY],
        out_specs=[pl.BlockSpec((tt, DP), lambda i: (i, 0)), pl.BlockSpec((tt, DP), lambda i: (i, 0)),
                   pl.BlockSpec((tt, D), lambda i: (i, 0)), pl.BlockSpec((tt, D), lambda i: (i, 0))],
        out_shape=[jax.ShapeDtypeStruct((t, DP), BF), jax.ShapeDtypeStruct((t, DP), F32),
                   jax.ShapeDtypeStruct((t, D), F32), jax.ShapeDtypeStruct((t, D), F32)],
        compiler_params=_cp(),
    )(a, u, u, x, w_pool, pool_scale, g_attn, g_pool, w_out, g_post, dep)


def _ffn_fwd(h1, g_pre, stacks2, g_post, p, w_ple, g_ple, w_pg, tgt):
    t = h1.shape[0]
    tt = FFN_FWD_TOKENS if t % FFN_FWD_TOKENS == 0 else t

    def body(h1_ref, gpre_ref, wg_ref, wu_ref, wd_ref, gpost_ref, p_ref, wple_ref, gple_ref, wpg_ref, tgt_ref,
             hn_ref, gate_ref, up_ref, dff_ref, dh2_ref, loss_ref, dwpg_ref, dwple_ref, dgple_ref, dgpost_ref):
        i = pl.program_id(0)

        @pl.when(i == 0)
        def _():
            loss_ref[...] = jnp.zeros_like(loss_ref)
            dwpg_ref[...] = jnp.zeros_like(dwpg_ref)
            dwple_ref[...] = jnp.zeros_like(dwple_ref)
            dgple_ref[...] = jnp.zeros_like(dgple_ref)
            dgpost_ref[...] = jnp.zeros_like(dgpost_ref)

        h1v = h1_ref[...]
        hn = (h1v * _rstd(h1v) * gpre_ref[...]).astype(BF)
        hn_ref[...] = hn
        gate = _mm_nt(hn, wg_ref[...].reshape(DFF, D))
        up = _mm_nt(hn, wu_ref[...].reshape(DFF, D))
        gate_ref[...] = gate.astype(BF)
        up_ref[...] = up.astype(BF)
        ff = _mm(gate * _sigmoid(gate) * up, wd_ref[...].reshape(DFF, D))
        rff = _rstd(ff)
        ffh = ff * rff
        gpost = gpost_ref[...]
        h2 = h1v + ffh * gpost
        pv = p_ref[...]
        pe = _mm(pv, wple_ref[...])
        rpe = _rstd(pe)
        peh = pe * rpe
        gple = gple_ref[...]
        e = peh * gple
        sig = _sigmoid(_mm(h2, wpg_ref[...]))
        dv = h2 + sig * e - tgt_ref[...]
        sq = jnp.sum(jnp.sum(dv * dv, axis=1, keepdims=True), axis=0, keepdims=True)
        loss_ref[...] = loss_ref[...] + sq
        dy = dv * (1.0 / D)
        d_e = dy * sig
        d_gl = dy * e * sig * (1.0 - sig)
        dh2 = dy + _mm_nt(d_gl, wpg_ref[...])
        dh2_ref[...] = dh2
        dwpg_ref[...] = dwpg_ref[...] + _mm_tn(h2, d_gl)
        dgple_ref[...] = dgple_ref[...] + jnp.sum(d_e * peh, axis=0, keepdims=True)
        dpeh = d_e * gple
        d_pe = rpe * (dpeh - peh * jnp.mean(dpeh * peh, axis=-1, keepdims=True))
        dwple_ref[...] = dwple_ref[...] + _mm_tn(pv, d_pe)
        dgpost_ref[...] = dgpost_ref[...] + jnp.sum(dh2 * ffh, axis=0, keepdims=True)
        dffh = dh2 * gpost
        dff_ref[...] = (rff * (dffh - ffh * jnp.mean(dffh * ffh, axis=-1, keepdims=True))).astype(BF)

    row = lambda w: pl.BlockSpec((tt, w), lambda i: (i, 0))
    shard_rows = lambda k: pl.BlockSpec((NSHARD, FF_SH, D), lambda i: (0, k, 0), pipeline_mode=pl.Buffered(1))
    return pl.pallas_call(
        body, name="ffn_fwd", grid=(t // tt,),
        in_specs=[row(D), _full((1, D)), shard_rows(0), shard_rows(1), shard_rows(2), _full((1, D)),
                  row(DPLE), _resident((DPLE, D)), _full((1, D)), _resident((D, D)), row(D)],
        out_specs=[row(D), row(DFF), row(DFF), row(D), row(D), _full((8, LANE)), _full((D, D)), _full((DPLE, D)),
                   _full((1, D)), _full((1, D))],
        out_shape=[jax.ShapeDtypeStruct((t, D), BF), jax.ShapeDtypeStruct((t, DFF), BF), jax.ShapeDtypeStruct((t, DFF), BF),
                   jax.ShapeDtypeStruct((t, D), BF), jax.ShapeDtypeStruct((t, D), F32), jax.ShapeDtypeStruct((8, LANE), F32),
                   jax.ShapeDtypeStruct((D, D), F32), jax.ShapeDtypeStruct((DPLE, D), F32),
                   jax.ShapeDtypeStruct((1, D), F32), jax.ShapeDtypeStruct((1, D), F32)],
        compiler_params=_cp(),
    )(h1, g_pre, stacks2, stacks2, stacks2, g_post, p, w_ple, g_ple, w_pg, tgt)


FF_CH = 256


def _ffn_bwd(hn2, gate, up, dff, wg_t, wu_t, wd):
    t = hn2.shape[0]
    tt = FFN_BWD_TOKENS if t % FFN_BWD_TOKENS == 0 else _tile(t)
    nt = t // tt
    ch = FF_CH
    nc = DFF // ch

    def body(hn_ref, gate_ref, up_ref, dff_ref, wg_ref, wu_ref, wd_ref,
             dwg_ref, dwu_ref, dwd_ref, dhn_ref, acc, gu_acc, d_acc, sem):
        j, i = pl.program_id(0), pl.program_id(1)

        @pl.when(j == 0)
        def _():
            acc[pl.ds(pl.multiple_of(i * tt, tt), tt), :] = jnp.zeros((tt, D), F32)

        @pl.when(i == 0)
        def _():
            gu_acc[...] = jnp.zeros_like(gu_acc)
            d_acc[...] = jnp.zeros_like(d_acc)

        wgu = jnp.concatenate([wg_ref[...], wu_ref[...]], axis=0)
        parts = 4 if tt % 1024 == 0 else 2
        half = tt // parts
        acts, dgus = [], []
        for hh in range(parts):
            r = slice(hh * half, (hh + 1) * half)
            gate_v = gate_ref[r, :].astype(F32)
            up_v = up_ref[r, :].astype(F32)
            sg = _sigmoid(gate_v)
            silu = gate_v * sg
            d_act = _mm_nt(dff_ref[r, :], wd_ref[...])
            d_up = (d_act * silu).astype(BF)
            d_gate = (d_act * up_v * (sg * (1.0 + gate_v * (1.0 - sg)))).astype(BF)
            dgu = jnp.concatenate([d_gate, d_up], axis=1)
            rows = pl.ds(pl.multiple_of(i * tt + hh * half, half), half)
            acc[rows, :] = acc[rows, :] + jnp.dot(dgu, wgu, preferred_element_type=F32)
            acts.append((silu * up_v).astype(BF))
            dgus.append(dgu)
        d_acc[...] = d_acc[...] + _mm_tn(jnp.concatenate(acts, axis=0), dff_ref[...])
        gu_acc[...] = gu_acc[...] + _mm_tn(jnp.concatenate(dgus, axis=0), hn_ref[...])

        @pl.when(i == nt - 1)
        def _():
            dwg_ref[...] = gu_acc[:ch].astype(BF)
            dwu_ref[...] = gu_acc[ch:].astype(BF)
            dwd_ref[...] = d_acc[...].astype(BF)

        @pl.when((j == nc - 1) & (i == nt - 1))
        def _():
            cp = pltpu.make_async_copy(acc, dhn_ref, sem)
            cp.start()
            cp.wait()

    tok = lambda w: pl.BlockSpec((tt, w), lambda j, i: (i, 0))
    chunk = pl.BlockSpec((ch, D), lambda j, i: (j, 0))
    return pl.pallas_call(
        body, name="ffn_bwd", grid=(nc, nt),
        in_specs=[tok(D), pl.BlockSpec((tt, ch), lambda j, i: (i, j)), pl.BlockSpec((tt, ch), lambda j, i: (i, j)),
                  tok(D), chunk, chunk, chunk],
        out_specs=[chunk, chunk, chunk, pl.BlockSpec(memory_space=pl.ANY)],
        out_shape=[jax.ShapeDtypeStruct((DFF, D), BF), jax.ShapeDtypeStruct((DFF, D), BF),
                   jax.ShapeDtypeStruct((DFF, D), BF), jax.ShapeDtypeStruct((t, D), F32)],
        scratch_shapes=[pltpu.VMEM((t, D), F32), pltpu.VMEM((2 * ch, D), F32), pltpu.VMEM((ch, D), F32),
                        pltpu.SemaphoreType.DMA],
        compiler_params=_cp(),
    )(hn2, gate, up, dff, wg_t, wu_t, wd)


def _mix_bwd(d_hn2, dh2, h1, o, a, m, yb, g_ffn_pre, g_post, g_attn, g_pool, w_out, w_pool, pool_scale, dep):
    t = a.shape[0]
    tt = _tile(t)

    def body(dhn_ref, dh2_ref, h1_ref, o_ref, a_ref, m_ref, yb_ref, gfp_ref, go_ref, ga_ref, gp_ref, wo_ref, wp_ref,
             ps_ref, dep_ref, dh1_ref, da_ref, dyc_ref, dgfp_ref, dgo_ref, dga_ref, dgp_ref, dps_ref, dwp_ref, dwo_ref):
        i = pl.program_id(0)

        @pl.when(i == 0)
        def _():
            for r in (dgfp_ref, dgo_ref, dga_ref, dgp_ref, dps_ref, dwp_ref, dwo_ref):
                r[...] = jnp.zeros_like(r)

        d1, dg = _rms_bwd(dhn_ref[...], h1_ref[...], gfp_ref[...])
        dgfp_ref[...] = dgfp_ref[...] + dg
        dh1 = dh2_ref[...] + d1
        dh1_ref[...] = dh1
        d_o, dg = _rms_bwd(dh1, o_ref[...], go_ref[...])
        dgo_ref[...] = dgo_ref[...] + dg
        d_mix = _mm_nt(d_o, wo_ref[...])
        av, mv = a_ref[...], m_ref[...]
        mix = jnp.concatenate([av * _rstd(av) * ga_ref[...], mv * _rstd(mv) * gp_ref[...]], axis=1)
        dwo_ref[...] = dwo_ref[...] + _mm_tn(mix, d_o)
        d_a, dg = _rms_bwd(d_mix[:, :DA], av, ga_ref[...])
        dga_ref[...] = dga_ref[...] + dg
        da_ref[...] = d_a
        d_m, dg = _rms_bwd(d_mix[:, DA:], mv, gp_ref[...])
        dgp_ref[...] = dgp_ref[...] + dg
        tok = i * tt + lax.broadcasted_iota(jnp.int32, (tt, PC), 0)
        dps = []
        for g, w in enumerate(WINS):
            sl = slice(PC * g, PC * (g + 1))
            ybg = yb_ref[:, sl]
            wpg = wp_ref[g].astype(BF)
            mlin = jnp.dot(ybg, wpg, preferred_element_type=F32)
            dmg = d_m[:, sl]
            dps.append(jnp.sum(dmg * mlin, axis=0, keepdims=True))
            dml = (dmg * ps_ref[:, sl]).astype(BF)
            dwp_ref[g] = dwp_ref[g] + _mm_tn(ybg, dml)
            dyc_ref[:, sl] = _mm_nt(dml, wpg) / jnp.minimum(tok + 1, w).astype(F32)
        dps_ref[...] = dps_ref[...] + jnp.concatenate(dps, axis=1)

    row = lambda w: pl.BlockSpec((tt, w), lambda i: (i, 0))
    return pl.pallas_call(
        body, name="mix_bwd", grid=(t // tt,),
        in_specs=[row(D), row(D), row(D), row(D), row(DA), row(DP), row(DP), _full((1, D)), _full((1, D)),
                  _full((1, DA)), _full((1, DP)), _resident((D, D)), _full((len(WINS), PC, PC)), _full((1, DP)), ANY],
        out_specs=[row(D), row(DA), row(DP), _full((1, D)), _full((1, D)), _full((1, DA)), _full((1, DP)),
                   _full((1, DP)), _full((len(WINS), PC, PC)), _full((D, D))],
        out_shape=[jax.ShapeDtypeStruct((t, D), F32), jax.ShapeDtypeStruct((t, DA), F32), jax.ShapeDtypeStruct((t, DP), F32),
                   jax.ShapeDtypeStruct((1, D), F32), jax.ShapeDtypeStruct((1, D), F32), jax.ShapeDtypeStruct((1, DA), F32),
                   jax.ShapeDtypeStruct((1, DP), F32), jax.ShapeDtypeStruct((1, DP), F32),
                   jax.ShapeDtypeStruct((len(WINS), PC, PC), F32), jax.ShapeDtypeStruct((D, D), F32)],
        compiler_params=_cp(),
    )(d_hn2, dh2, h1, o, a, m, yb, g_ffn_pre, g_post, g_attn, g_pool, w_out, w_pool, pool_scale, dep)


def _attn_bwd(qa, ka, v, a, d_a, lse, dep):
    t = qa.shape[0]
    ta = _tile(t)
    n = t // ta

    def body(q_ref, k_ref, v_ref, o_ref, do_ref, lse_ref, dep_ref, dq_ref, dk_ref, dv_ref):
        j = pl.program_id(1)

        @pl.when(j == 0)
        def _():
            dq_ref[...] = jnp.zeros_like(dq_ref)

        dk_ref[...] = jnp.zeros_like(dk_ref)
        dv_ref[...] = jnp.zeros_like(dv_ref)
        ks = [k_ref[:, LANE * h:LANE * (h + 1)] for h in range(2)]
        v2 = v_ref[...]
        lane = lax.broadcasted_iota(jnp.int32, (1, LANE), 1)
        mine = [lane < HD, lane >= HD]

        def block(q0, nq, nk, shift):
            rows = pl.ds(pl.multiple_of(q0, nq), nq)
            do2 = do_ref[rows, :]
            prod = do2 * o_ref[rows, :]
            lse2 = lse_ref[rows, :]
            do2b = do2.astype(BF)
            qh = [q_ref[rows, LANE * h:LANE * (h + 1)] for h in range(2)]
            kk, vv = [kh[:nk] for kh in ks], v2[:nk]
            s = [_mm_nt(qh[h], kk[h]) for h in range(2)]
            dp = [_mm_nt(jnp.where(mine[h], do2, 0.0), vv) for h in range(2)]
            delta = [jnp.sum(jnp.where(mine[h], prod, 0.0), axis=1, keepdims=True) for h in range(2)]
            lse_h = [jnp.sum(jnp.where(lane == HD * h, lse2, 0.0), axis=1, keepdims=True) for h in range(2)]
            pr = [jnp.exp(s[h] - lse_h[h]) for h in range(2)]
            if shift is not None:
                keep = (lax.broadcasted_iota(jnp.int32, (nq, nk), 1)
                        <= lax.broadcasted_iota(jnp.int32, (nq, nk), 0) + shift)
                pr = [jnp.where(keep, ph, 0.0) for ph in pr]
            ds = [(pr[h] * (dp[h] - delta[h])).astype(BF) for h in range(2)]
            dv_ref[:nk, :] = dv_ref[:nk, :] + jnp.where(mine[0], _mm_tn(pr[0], do2b), _mm_tn(pr[1], do2b))
            for h in range(2):
                sl = slice(LANE * h, LANE * (h + 1))
                dk_ref[:nk, sl] = dk_ref[:nk, sl] + _mm_tn(ds[h], qh[h])
                dq_ref[0, rows, sl] = dq_ref[0, rows, sl] + jnp.dot(ds[h], kk[h], preferred_element_type=F32)

        def step(i, carry):
            block(i * ta, ta, ta, None)
            return carry

        half = ta // 2
        block(j * ta, half, half, 0)
        block(j * ta + half, half, ta, half)
        lax.fori_loop(j + 1, n, step, 0)

    qrow = lambda w: pl.BlockSpec((t, w), lambda p, j: (0, p))
    krow = lambda w: pl.BlockSpec((ta, w), lambda p, j: (j, p))
    return pl.pallas_call(
        body, name="attn_bwd", grid=(NH // 2, n),
        in_specs=[qrow(2 * LANE), krow(2 * LANE), krow(LANE), qrow(LANE), qrow(LANE), qrow(LANE), ANY],
        out_specs=[pl.BlockSpec((1, t, 2 * LANE), lambda p, j: (p, 0, 0)), krow(2 * LANE), krow(LANE)],
        out_shape=[jax.ShapeDtypeStruct((NH // 2, t, 2 * LANE), F32), jax.ShapeDtypeStruct((t, NH * LANE), F32),
                   jax.ShapeDtypeStruct((t, DA), F32)],
        compiler_params=_cp(),
    )(qa, ka, v, a, d_a, lse, dep)


def _in_bwd(dqa, dka, dv, dyc, fl, x, dh1, g1, w_in_t, tri_u, dep):
    t = x.shape[0]
    tt = _tile(t)
    nt = t // tt
    hb = tt // HALO
    rev = lambda s: nt - 1 - s

    def body(dqa_ref, dka_ref, dv_ref, dyc_ref, dyn_ref, fl_ref, x_ref, dh1_ref, g_ref, w_ref, tri_ref,
             dep_ref, dx_ref, dw_ref, dg_ref, db_ref, carry, acc, stage, sem):
        s = pl.program_id(0)
        i = nt - 1 - s

        @pl.when(s == 0)
        def _():
            carry[...] = jnp.zeros_like(carry)
            acc[...] = jnp.zeros_like(acc)
            dg_ref[...] = jnp.zeros_like(dg_ref)
            db_ref[...] = jnp.zeros_like(db_ref)

        dq_cat = jnp.concatenate([dqa_ref[p] for p in range(NH // 2)], axis=1)
        dk_cat = dka_ref[...]
        off = lax.broadcasted_iota(jnp.int32, (1, NH * LANE), 1)
        off = off % LANE - HD - off // LANE

        def picked(cat, lane_off):
            kept = jnp.where(off == lane_off, cat, 0.0)
            return functools.reduce(lambda a, b: a + b, [kept[:, LANE * h:LANE * (h + 1)] for h in range(NH)])

        dc = pltpu.roll(picked(dq_cat, 0), LANE - HD, 1) - pltpu.roll(picked(dk_cat, 24), LANE - HD - 24, 1)
        dlf = carry[...] + _dot3(tri_ref[...], dc)
        carry[...] = carry[...] + jnp.sum(dc, axis=0, keepdims=True)
        flv = fl_ref[...]
        lane = lax.broadcasted_iota(jnp.int32, flv.shape, 1)
        d_fl = jnp.where(lane < NH, dlf / (1.0 + jnp.exp(flv)), 0.0)
        db_ref[...] = db_ref[...] + jnp.sum(d_fl, axis=0, keepdims=True)
        low = lax.broadcasted_iota(jnp.int32, (tt, LANE), 1) < HD
        dqs, dks = [], []
        for p in range(NH // 2):
            b0, b1 = slice(2 * LANE * p, 2 * LANE * p + LANE), slice(2 * LANE * p + LANE, 2 * LANE * (p + 1))
            dqs.append(jnp.where(low, dq_cat[:, b0], pltpu.roll(dq_cat[:, b1], HD, 1)) * (1.0 / math.sqrt(HD)))
            dks.append(jnp.where(low, dk_cat[:, b0], pltpu.roll(dk_cat[:, b1], HD, 1)))
        nxt = dyn_ref[...] * jnp.where(i < nt - 1, 1.0, 0.0)
        tok = i * tt + lax.broadcasted_iota(jnp.int32, (tt, PC), 0)
        dus = []
        for g, w in enumerate(WINS):
            sl = slice(PC * g, PC * (g + 1))
            dycg = dyc_ref[:, sl]
            ext = jnp.concatenate([dycg, nxt[:, sl]], axis=0)
            dus.append(_window_sum(ext, w, False)[:tt] - dycg * jnp.minimum(tok + 1, w).astype(F32))
        d_z = jnp.concatenate(dqs + dks + [dv_ref[...], d_fl] + dus, axis=1).astype(BF)
        xv = x_ref[...]
        gv = g_ref[...]
        hn = (xv * _rstd(xv) * gv).astype(BF)
        d_hn = jnp.dot(d_z, w_ref[...], preferred_element_type=F32)
        acc[...] = acc[...] + _mm_tn(d_z, hn)
        d1, dg = _rms_bwd(d_hn, xv, gv)
        dg_ref[...] = dg_ref[...] + dg
        dx_ref[...] = dh1_ref[...] + d1

        @pl.when(s == nt - 1)
        def _():
            stage[...] = acc[...].astype(BF)
            cp = pltpu.make_async_copy(stage, dw_ref, sem)
            cp.start()
            cp.wait()

    row = lambda w: pl.BlockSpec((tt, w), lambda s: (rev(s), 0))
    return pl.pallas_call(
        body, name="in_bwd", grid=(nt,),
        in_specs=[pl.BlockSpec((NH // 2, tt, 2 * LANE), lambda s: (0, rev(s), 0)), row(NH * LANE), row(DA), row(DP),
                  pl.BlockSpec((HALO, DP), lambda s: (jnp.minimum((rev(s) + 1) * hb, nt * hb - 1), 0)),
                  row(LANE), row(D), row(D), _full((1, D)), _resident((ZW, D)), _full((tt, tt)),
                  ANY],
        out_specs=[row(D), pl.BlockSpec(memory_space=pl.ANY), _full((1, D)), _full((1, LANE))],
        out_shape=[jax.ShapeDtypeStruct((t, D), F32), jax.ShapeDtypeStruct((ZW, D), BF),
                   jax.ShapeDtypeStruct((1, D), F32), jax.ShapeDtypeStruct((1, LANE), F32)],
        scratch_shapes=[pltpu.VMEM((1, LANE), F32), pltpu.VMEM((ZW, D), F32), pltpu.VMEM((ZW, D), BF),
                        pltpu.SemaphoreType.DMA],
        compiler_params=_cp(),
    )(dqa, dka, dv, dyc, dyc, fl, x, dh1, g1, w_in_t, tri_u, dep)


class _NoComm:
    def __init__(self, w2):
        self.w2 = w2
        self.dep = jnp.zeros((8, LANE), F32)

    def after_attention(self, after):
        return self.dep

    def weights2(self, after):
        return _stack2_full(*self.w2)

    def after_ffn(self, grads2):
        self.grads2 = grads2
        return self.dep

    def after_mix(self, after, early):
        self.early = early
        return self.dep

    def after_attn(self, after):
        return self.dep


def _local_step(x, p, tgt, sm, w1, comm):
    w_in_t, w_out = w1
    tt = _tile(x.shape[0])
    eq, ek, rowq, rowk = _aug_consts()
    b_pad = jnp.pad(sm["b_forget"], ((0, 0), (0, LANE - NH)))
    qa, ka, v, u, fl = _in_proj(x, sm["g_mix_pre"], w_in_t, b_pad, _tri(tt, False), eq, ek, rowq, rowk, comm.dep)
    a, lse = _attn_fwd(qa, ka, v)
    yb, m, o, h1 = _mix_out(a, u, x, sm["w_pool"], sm["pool_scale"], sm["g_attn_grp"],
                            sm["g_pool_grp"], w_out, sm["g_mix_post"], comm.after_attention(a))
    stacks2 = comm.weights2(h1)
    wg_t, wu_t, wd, w_ple, w_pg = _unstack2_full(stacks2)
    hn2, gate, up, dff, dh2, loss, dwpg, dwple, dgple, dgfpost = _ffn_fwd(
        h1, sm["g_ffn_pre"], stacks2, sm["g_ffn_post"], p, w_ple, sm["g_ple"], w_pg, tgt)
    dwg_t, dwu_t, dwd, d_hn2 = _ffn_bwd(hn2, gate, up, dff, wg_t, wu_t, wd)
    dep = comm.after_ffn((dwg_t, dwu_t, dwd, dwple, dwpg))
    dh1, d_a, dyc, dgfpre, dgpost, dgattn, dgpool, dps, dwpool, dwout = _mix_bwd(
        d_hn2, dh2, h1, o, a, m, yb, sm["g_ffn_pre"], sm["g_mix_post"], sm["g_attn_grp"], sm["g_pool_grp"],
        w_out, sm["w_pool"], sm["pool_scale"], dep)
    early = dict(loss=loss[0:1, 0:1], g_attn_grp=dgattn, g_pool_grp=dgpool, w_pool=dwpool, pool_scale=dps,
                 g_mix_post=dgpost, g_ffn_pre=dgfpre, g_ffn_post=dgfpost, g_ple=dgple)
    dqa, dka, dvv = _attn_bwd(qa, ka, v, a, d_a, lse, comm.after_mix(dh1, early))
    dx, dwin_t, dg1, dbf = _in_bwd(dqa, dka, dvv, dyc, fl, x, dh1, sm["g_mix_pre"], w_in_t, _tri(tt, True),
                                   comm.after_attn(dvv))
    return dx, (dwin_t, dwout), dict(g_mix_pre=dg1, b_forget=dbf[:, :NH])


def _place():
    x, y, c = lax.axis_index("x"), lax.axis_index("y"), lax.axis_index("c")
    return x, y, c, [(1 - x, y), (x, 1 - y), (1 - x, 1 - y)]


def _rows(c, h):
    return pl.ds(pl.multiple_of(c * h, 16), h)


def _plan_gather(h):
    def plan(src, land):
        x, y, c, chips = _place()
        return [(src.at[_rows(c, h), :], land.at[2 * x + y, _rows(c, h), :], (cx, cy, c),
                 land.at[2 * cx + cy, _rows(c, h), :]) for cx, cy in chips]
    return plan


def _plan_forward(h):
    def plan(land_in, own, land):
        x, y, c, chips = _place()
        sib, me = (x, y, 1 - c), 2 * x + y
        return ([(land_in.at[2 * cx + cy, _rows(c, h), :], land.at[2 * cx + cy, _rows(c, h), :], sib,
                  land.at[2 * cx + cy, _rows(1 - c, h), :]) for cx, cy in chips]
                + [(own, land.at[me], sib, land.at[me])])
    return plan


def _plan_swap_halves(h):
    def plan(buf_in, buf):
        x, y, c, _ = _place()
        return [(buf_in.at[_rows(c, h), :], buf.at[_rows(c, h), :], (x, y, 1 - c), buf.at[_rows(1 - c, h), :])]
    return plan


def _plan_pair_rows(h):
    def plan(src, land):
        x, y, c, _ = _place()
        return [(src.at[:, _rows(1 - c, h), :], land, (x, y, 1 - c), land)]
    return plan


def _plan_scatter(src, land):
    x, y, c, chips = _place()
    return [(src.at[2 * cx + cy], land.at[k], (cx, cy, c), land.at[k]) for k, (cx, cy) in enumerate(chips)]


def _plan_scatter8(h):
    def plan(src, land):
        x, y, c, chips = _place()
        copies = [(src.at[2 * x + y, _rows(1 - c, h), :], land.at[0], (x, y, 1 - c), land.at[0])]
        for k, (cx, cy) in enumerate(chips):
            for d, other in enumerate((c, 1 - c)):
                copies.append((src.at[2 * cx + cy, _rows(other, h), :], land.at[1 + 2 * k + c], (cx, cy, other),
                               land.at[1 + 2 * k + other]))
        return copies
    return plan


def _plan_all(src, land):
    x, y, c, _ = _place()
    copies = []
    for r in range(1, 8):
        px, py, pc = (1 - a if b else a for a, b in zip((x, y, c), (r >> 2 & 1, r >> 1 & 1, r & 1)))
        copies.append((src, land.at[4 * x + 2 * y + c], (px, py, pc), land.at[4 * px + 2 * py + pc]))
    return copies


def _remote(src, dst, send_sems, recv_sems, k, peer):
    return pltpu.make_async_remote_copy(src_ref=src, dst_ref=dst, send_sem=send_sems.at[k], recv_sem=recv_sems.at[k],
                                        device_id=peer, device_id_type=MESH)


def _exchange_inplace(name, n, buf, extra, plan):
    def body(*refs):
        ins, buf_ref, send_sems, recv_sems = refs[:1 + len(extra)], refs[1 + len(extra)], refs[-2], refs[-1]
        copies = plan(*ins, buf_ref)
        for k, (s, d, peer, _) in enumerate(copies):
            _remote(s, d, send_sems, recv_sems, k, peer).start()
        for k, (s, _, peer, mine) in enumerate(copies):
            _remote(s, mine, send_sems, recv_sems, k, peer).wait_recv()
        for k, (s, d, peer, _) in enumerate(copies):
            _remote(s, d, send_sems, recv_sems, k, peer).wait_send()

    return pl.pallas_call(
        body, name=name, in_specs=[ANY] * (1 + len(extra)), out_specs=ANY, out_shape=_sds(buf.shape, buf.dtype),
        input_output_aliases={0: 0},
        scratch_shapes=[pltpu.SemaphoreType.DMA((n,)), pltpu.SemaphoreType.DMA((n,))],
    )(buf, *extra)


HBM = pl.BlockSpec(memory_space=pltpu.HBM)
SEM = pl.BlockSpec(memory_space=pltpu.SEMAPHORE)
EFFECT = pltpu.SideEffectType.DATAFLOW_SIDE_EFFECTING


def _exchange_start(name, n, src, land, plan):
    def body(src_ref, land_ref, send_sems, recv_sems, src_thru, land_thru, token):
        for k, (s, d, peer, _) in enumerate(plan(src_ref, land_ref)):
            _remote(s, d, send_sems, recv_sems, k, peer).start()
        token[...] = jnp.zeros_like(token)

    return pl.pallas_call(
        body, name=name,
        out_shape=(pltpu.SemaphoreType.DMA((n,)), pltpu.SemaphoreType.DMA((n,)), pltpu.HBM(src.shape, src.dtype),
                   pltpu.HBM(land.shape, land.dtype), jax.ShapeDtypeStruct((8, LANE), F32)),
        in_specs=(HBM, HBM), out_specs=(SEM, SEM, HBM, HBM, pl.BlockSpec(memory_space=pltpu.VMEM)),
        input_output_aliases={0: 2, 1: 3},
        compiler_params=pltpu.CompilerParams(has_side_effects=EFFECT),
    )(pltpu.with_memory_space_constraint(src, pltpu.HBM), pltpu.with_memory_space_constraint(land, pltpu.HBM))


def _exchange_wait(name, started, plan, after):
    send_sems, recv_sems, src, land, _ = started

    def body(src_ref, land_ref, send_sems, recv_sems, after_ref, src_out, land_out):
        for k, (s, _, peer, mine) in enumerate(plan(src_ref, land_ref)):
            cp = _remote(s, mine, send_sems, recv_sems, k, peer)
            cp.wait_send()
            cp.wait_recv()

    return pl.pallas_call(
        body, name=name, out_shape=(pltpu.HBM(src.shape, src.dtype), pltpu.HBM(land.shape, land.dtype)),
        in_specs=(HBM, HBM, SEM, SEM, ANY), out_specs=(HBM, HBM), input_output_aliases={0: 0, 1: 1},
        compiler_params=pltpu.CompilerParams(has_side_effects=EFFECT),
    )(src, land, send_sems, recv_sems, after)


def _pair_sum(name, cidx, g, recv, br):
    h = recv.shape[1]
    nb = h // br

    def body(c_ref, g_ref, r_ref, out_ref):
        out_ref[...] = (g_ref[...].astype(F32) + r_ref[...].astype(F32)).astype(BF)

    return pl.pallas_call(
        body, name=name,
        grid_spec=pltpu.PrefetchScalarGridSpec(
            num_scalar_prefetch=1, grid=(NSHARD, nb),
            in_specs=[pl.BlockSpec((1, br, D), lambda s, i, c: (s, c[0] * nb + i, 0)),
                      pl.BlockSpec((1, br, D), lambda s, i, c: (s, i, 0))],
            out_specs=pl.BlockSpec((1, br, D), lambda s, i, c: (s, i, 0))),
        out_shape=jax.ShapeDtypeStruct((NSHARD, h, D), BF),
    )(cidx, g, recv)


def _chip_sum(name, place, pb, y, br):
    h = y.shape[1]
    nb = h // br

    def body(pl_ref, p_ref, y_ref, out_ref):
        acc = p_ref[0].astype(F32)
        for k in range(NSHARD - 1):
            acc = acc + y_ref[k].astype(F32)
        out_ref[...] = acc

    return pl.pallas_call(
        body, name=name,
        grid_spec=pltpu.PrefetchScalarGridSpec(
            num_scalar_prefetch=1, grid=(nb,),
            in_specs=[pl.BlockSpec((1, br, D), lambda i, s: (s[0], i, 0)),
                      pl.BlockSpec((NSHARD - 1, br, D), lambda i, s: (0, i, 0))],
            out_specs=pl.BlockSpec((br, D), lambda i, s: (s[1] * nb + i, 0))),
        out_shape=jax.ShapeDtypeStruct((2 * h, D), F32),
    )(place, pb, y)


def _sum8(name, place, g, land, br):
    h = land.shape[1]
    nb = h // br

    def body(pl_ref, g_ref, y_ref, out_ref):
        acc = g_ref[0].astype(F32)
        for k in range(land.shape[0]):
            acc = acc + y_ref[k].astype(F32)
        out_ref[...] = acc

    return pl.pallas_call(
        body, name=name,
        grid_spec=pltpu.PrefetchScalarGridSpec(
            num_scalar_prefetch=1, grid=(nb,),
            in_specs=[pl.BlockSpec((1, br, D), lambda i, s: (s[0], s[1] * nb + i, 0)),
                      pl.BlockSpec((land.shape[0], br, D), lambda i, s: (0, i, 0))],
            out_specs=pl.BlockSpec((br, D), lambda i, s: (s[1] * nb + i, 0))),
        out_shape=jax.ShapeDtypeStruct((2 * h, D), F32), compiler_params=_cp(),
    )(place, g, land)


def _sum_slots(name, v):
    def body(in_ref, out_ref):
        acc = in_ref[0]
        for k in range(1, 8):
            acc = acc + in_ref[k]
        out_ref[...] = acc

    vm = pl.BlockSpec(memory_space=pltpu.VMEM)
    return pl.pallas_call(body, name=name, in_specs=[vm], out_specs=vm,
                          out_shape=jax.ShapeDtypeStruct(v.shape[1:], F32))(v)


def _adamw_math(w, g, m, v):
    m = ADAM_B1 * m + (1.0 - ADAM_B1) * g
    v = ADAM_B2 * v + (1.0 - ADAM_B2) * (g * g)
    m_hat = m / (1.0 - ADAM_B1 ** ADAM_STEP)
    v_hat = v / (1.0 - ADAM_B2 ** ADAM_STEP)
    delta = -ADAM_LR * (m_hat / (jnp.sqrt(v_hat) + ADAM_EPS) + ADAM_WD * w)
    return delta, m, v


def _adamw(w, g, m, v, dep, row0=None):
    r, c = w.shape
    br = next(b for b in (352, 256, 128, r) if r % b == 0 and (row0 or 0) % b == 0)
    first = (row0 or 0) // br

    def body(w_ref, g_ref, m_ref, v_ref, dep_ref, *outs):
        gv = g_ref[...]
        outs[-3][...], outs[-2][...], outs[-1][...] = _adamw_math(w_ref[...], gv, m_ref[...], v_ref[...])
        if row0 is not None:
            outs[0][...] = gv

    spec = pl.BlockSpec((br, c), lambda i: (i, 0))
    n_out = 3 if row0 is None else 4
    out = pl.pallas_call(
        body, name="adamw", grid=(r // br,),
        in_specs=[spec, pl.BlockSpec((br, c), lambda i: (first + i, 0)), spec, spec, ANY], out_specs=[spec] * n_out,
        out_shape=[jax.ShapeDtypeStruct((r, c), F32)] * n_out, compiler_params=_cp(),
    )(w, g, m, v, dep)
    return out if row0 is not None else [g] + list(out)


def _adamw_small(ws, gs, ms, vs):
    n = len(ws)

    def body(*refs):
        ins, outs = refs[:4 * n], refs[4 * n:]
        for k in range(n):
            d, m, v = _adamw_math(ins[k][...], ins[n + k][...], ins[2 * n + k][...], ins[3 * n + k][...])
            outs[k][...] = d
            outs[n + k][...] = m
            outs[2 * n + k][...] = v

    vm = pl.BlockSpec(memory_space=pltpu.VMEM)
    out = pl.pallas_call(
        body, name="adamw_small", in_specs=[vm] * (4 * n), out_specs=[vm] * (3 * n),
        out_shape=[jax.ShapeDtypeStruct(w.shape, F32) for w in ws] * 3,
    )(*ws, *gs, *ms, *vs)
    return out[:n], out[n:2 * n], out[2 * n:]


BIG = ("w_in", "w_out", "w_ffn_gate", "w_ffn_up", "w_ffn_down", "w_ple_proj", "w_ple_gate")
SMALL = ("g_mix_pre", "b_forget", "g_attn_grp", "g_pool_grp", "w_pool", "pool_scale", "g_mix_post", "g_ffn_pre",
         "g_ffn_post", "g_ple")
TRANSPOSED = ("w_in", "w_ffn_gate", "w_ffn_up")
VECTORS = tuple(n for n in SMALL if n != "w_pool")
ORDER = ("g_mix_pre", "w_in", "b_forget", "g_attn_grp", "g_pool_grp", "w_pool", "pool_scale", "w_out", "g_mix_post",
         "g_ffn_pre", "w_ffn_gate", "w_ffn_up", "w_ffn_down", "g_ffn_post", "w_ple_proj", "g_ple", "w_ple_gate")


def _pad_rows(a, rows):
    return jnp.pad(a, ((0, rows - a.shape[0]), (0, 0)))


def _stack1(w_in, w_out):
    return _pad_rows(jnp.concatenate([_pad_rows(w_in.T, IN_PAD), w_out], axis=0), ROWS1)


def _stack2(wg, wu, wd, wple, wpg):
    return _pad_rows(jnp.concatenate([wg.T, wu.T, wd, wple.reshape(DPLE // NSHARD, D), wpg], axis=0), ROWS2)


def _unstack1(s):
    return s[:IN_SH], s[O1_OUT:USED1]


def _cat(g, lo, hi):
    return g[:, lo:hi].reshape(NSHARD * (hi - lo), D)


def _unstack1_full(g):
    w_in_t = _cat(g, 0, IN_SH)
    w_in_t = jnp.concatenate([w_in_t[:3 * DA], _pad_rows(w_in_t[3 * DA:3 * DA + NH], LANE), w_in_t[3 * DA + NH:]], axis=0)
    return w_in_t, _cat(g, O1_OUT, USED1)


def _unstack2_full(g):
    w_ple = g[:, O2_PLE:O2_PG].reshape(NSHARD, DPLE, DPLE).transpose(1, 0, 2).reshape(DPLE, D)
    return _cat(g, 0, O2_U), _cat(g, O2_U, O2_D), _cat(g, O2_D, O2_PLE), w_ple, _cat(g, O2_PG, USED2)


def _shards(a):
    return a.reshape(NSHARD, a.shape[0] // NSHARD, D)


def _stack1_full(dwin_t, dwout):
    dwin_t = jnp.concatenate([dwin_t[:3 * DA + NH], dwin_t[3 * DA + LANE:]], axis=0).reshape(NSHARD, IN_SH, D)
    zeros = lambda r: jnp.zeros((NSHARD, r, D), dwin_t.dtype)
    return jnp.concatenate([dwin_t, zeros(IN_PAD - IN_SH), _shards(dwout), zeros(ROWS1 - USED1)], axis=1)


def _stack2_full(dwg_t, dwu_t, dwd, dwple, dwpg):
    dwple = dwple.reshape(DPLE, NSHARD, DPLE).transpose(1, 0, 2).reshape(NSHARD, DPLE // NSHARD, D)
    return jnp.concatenate([_shards(dwg_t), _shards(dwu_t), _shards(dwd), dwple, _shards(dwpg),
                            jnp.zeros((NSHARD, ROWS2 - USED2, D), dwd.dtype)], axis=1)


def _sds(shape, dtype):
    return jax.ShapeDtypeStruct(shape, dtype)


class _Comm:
    def __init__(self, stack2, me, c):
        self.stack2, self.me, self.c = stack2, me, c
        self.cidx = c.astype(jnp.int32).reshape(1)
        self.place = jnp.stack([me, c]).astype(jnp.int32)
        self.h = ROWS2 // 2
        self.gather = _exchange_start("gather2_start", 3, stack2, lax.empty((NSHARD, ROWS2, D), BF), _plan_gather(self.h))
        self.dep = self.gather[4]

    def after_attention(self, after):
        own, land = _exchange_wait("gather2_wait", self.gather, _plan_gather(self.h), after)
        fwd = _plan_forward(self.h)
        self.forward = lambda own_ref, land_ref: fwd(land_ref, own_ref, land_ref)
        self.passing = _exchange_start("forward2_start", 4, own, land, self.forward)
        return self.passing[4]

    def weights2(self, after):
        return _exchange_wait("forward2_wait", self.passing, self.forward, after)[1]

    def after_ffn(self, grads2):
        g = _stack2_full(*[a.astype(BF) for a in grads2])
        self.scatter = _plan_scatter8(self.h)
        self.chip = _exchange_start("reduce2_start", 7, g, lax.empty((7, self.h, D), BF), self.scatter)
        return self.chip[4]

    def after_mix(self, after, early):
        self.early_shapes = {n: early[n].shape for n in early}
        self.small = self.start_small("small", early)
        return self.small[4]

    def after_attn(self, after):
        g, y = _exchange_wait("reduce2_wait", self.chip, self.scatter, after)
        f = _sum8("sum2", self.place, g, y, RED2 // 2)
        self.early = self.finish_small("small", self.small, self.early_shapes, after)
        swap = _plan_swap_halves(self.h)
        self.swap = lambda _, buf: swap(buf, buf)
        self.swapping = _exchange_start("reduce2_gather_start", 1, self.dep, f, self.swap)
        return self.swapping[4]

    def reduced2(self, after):
        return _exchange_wait("reduce2_gather_wait", self.swapping, self.swap, after)[1]

    def start_small(self, name, small):
        v = _pack_small(small)
        return _exchange_start(name + "_start", 7, v, lax.empty((8,) + v.shape, F32), _plan_all)

    def finish_small(self, name, started, shapes, after):
        v, land = _exchange_wait(name + "_wait", started, _plan_all, after)
        land = lax.dynamic_update_slice(land, v[None], (2 * self.me + self.c, 0, 0))
        return _unpack_small(_sum_slots(name + "_sum", land), shapes)


def _pack_small(small):
    parts = []
    for name in small:
        flat = small[name].reshape(-1)
        parts.append(jnp.pad(flat, (0, -flat.shape[0] % LANE)).reshape(-1, LANE))
    v = jnp.concatenate(parts, axis=0)
    return _pad_rows(v, v.shape[0] + (-v.shape[0] % 8))


def _unpack_small(v, shapes):
    out, r = {}, 0
    for name in shapes:
        n = math.prod(shapes[name])
        rows = -(-n // LANE)
        out[name] = v[r:r + rows].reshape(-1)[:n].reshape(shapes[name])
        r += rows
    return out


def kernel(x, p, g_mix_pre, w_in, b_forget, g_attn_grp, g_pool_grp, w_pool, pool_scale, w_out, g_mix_post, g_ffn_pre, w_ffn_gate, w_ffn_up, w_ffn_down, g_ffn_post, w_ple_proj, g_ple, w_ple_gate, loss_target, m_g_mix_pre, m_w_in, m_b_forget, m_g_attn_grp, m_g_pool_grp, m_w_pool, m_pool_scale, m_w_out, m_g_mix_post, m_g_ffn_pre, m_w_ffn_gate, m_w_ffn_up, m_w_ffn_down, m_g_ffn_post, m_w_ple_proj, m_g_ple, m_w_ple_gate, v_g_mix_pre, v_w_in, v_b_forget, v_g_attn_grp, v_g_pool_grp, v_w_pool, v_pool_scale, v_w_out, v_g_mix_post, v_g_ffn_pre, v_w_ffn_gate, v_w_ffn_up, v_w_ffn_down, v_g_ffn_post, v_w_ple_proj, v_g_ple, v_w_ple_gate):
    args = dict(locals())
    strip = lambda n, a: a if n in VECTORS else a[0]
    w = {n: strip(n, args[n]) for n in ORDER}
    mom = {n: strip(n, args["m_" + n]) for n in ORDER}
    var = {n: strip(n, args["v_" + n]) for n in ORDER}
    sm = {n: w[n] for n in SMALL}

    c = lax.axis_index("c")
    me = 2 * lax.axis_index("x") + lax.axis_index("y")
    h1 = ROWS1 // 2
    bf = lambda n: w[n].astype(BF)
    stack1 = _stack1(bf("w_in"), bf("w_out"))
    gather1 = _exchange_start("gather1_start", 3, stack1, lax.empty((NSHARD, ROWS1, D), BF), _plan_gather(h1))
    _, *shards2 = lax.optimization_barrier((gather1[4], *[w[n] for n in BIG[2:]]))
    stack2 = _stack2(*[a.astype(BF) for a in shards2])
    stack1, land = _exchange_wait("gather1_wait", gather1, _plan_gather(h1), stack2)
    land, stack2 = lax.optimization_barrier((land, stack2))
    comm = _Comm(stack2, me, c)
    w1 = _unstack1_full(_exchange_inplace("gather1_forward", 4, land, (stack1,), _plan_forward(h1)))
    dx, grads1, late = _local_step(x[0], p[0, 0], loss_target[0], sm, w1, comm)

    flip = lambda n, a: a.T if n in TRANSPOSED else a
    grads, delta, new_m, new_v = {}, {}, {}, {}

    def update(n, g, dep, row0=None):
        g_, d_, m_, v_ = _adamw(flip(n, w[n]), g, flip(n, mom[n]), flip(n, var[n]), dep, row0)
        grads[n], delta[n], new_m[n], new_v[n] = flip(n, g_), flip(n, d_), flip(n, m_), flip(n, v_)
        return v_

    late_shapes = {n: late[n].shape for n in late}
    small2 = comm.start_small("small2", late)
    red2 = comm.reduced2(dx)
    g1 = _stack1_full(grads1[0], grads1[1].astype(BF))
    pair1 = _exchange_start("reduce1_pair_start", 1, g1, lax.empty((NSHARD, h1, D), BF), _plan_pair_rows(h1))
    dep = update("w_ple_gate", red2, pair1[4] + small2[4], O2_PG)
    dep = update("w_ple_proj", red2[O2_PLE:O2_PG].reshape(DPLE, DPLE), dep)
    g1, recv = _exchange_wait("reduce1_pair_wait", pair1, _plan_pair_rows(h1), dep)
    pb = _pair_sum("pair_sum1", comm.cidx, g1, recv, RED1)
    chip1 = _exchange_start("reduce1_chip_start", 3, pb, lax.empty((NSHARD - 1, h1, D), BF), _plan_scatter)
    dep = update("w_ffn_gate", red2, chip1[4], 0)
    dep = update("w_ffn_up", red2, dep, O2_U)
    dep = update("w_ffn_down", red2, dep, O2_D)
    red_small = {**comm.early, **comm.finish_small("small2", small2, late_shapes, dep)}
    loss = 0.5 / D * red_small["loss"][0, 0]
    for n in SMALL:
        grads[n] = red_small[n].reshape(w[n].shape)
    two_d = lambda a: a.reshape(-1, a.shape[-1])
    ds, ms, vs = _adamw_small([two_d(w[n]) for n in SMALL], [two_d(grads[n]) for n in SMALL],
                              [two_d(mom[n]) for n in SMALL], [two_d(var[n]) for n in SMALL])
    for k, n in enumerate(SMALL):
        delta[n], new_m[n], new_v[n] = ds[k].reshape(w[n].shape), ms[k].reshape(w[n].shape), vs[k].reshape(w[n].shape)
    pb, y = _exchange_wait("reduce1_chip_wait", chip1, _plan_scatter, vs[0])
    f = _chip_sum("chip_sum1", comm.place, pb, y, RED1)
    reduced1 = _exchange_inplace("reduce1_gather", 1, f, (), _plan_swap_halves(h1))
    g_in, g_out = _unstack1(reduced1)
    update("w_out", g_out, update("w_in", g_in, reduced1))

    lead = lambda d: [d[n] if n in VECTORS else d[n][None] for n in ORDER]
    return (loss, dx[None], *lead(grads), *lead(delta), *lead(new_m), *lead(new_v))
```

```python
import functools
import math

import jax
import jax.numpy as jnp
import numpy as np
from jax import lax
from jax.experimental import pallas as pl
from jax.experimental.pallas import tpu as pltpu

F32 = jnp.float32
BF = jnp.bfloat16
MESH = pl.DeviceIdType.MESH

D = 1024
DA = 512
DP = 512
NH = 8
HD = 64
DFF = 2816
DPLE = 256
WINS = (2, 4, 8, 16)
PC = 128
ZW = 3 * DA + 128 + DP
EPS = 1e-6
NSHARD = 4

LANE = 128
HALO = 128

IN_SH = 514
IN_PAD = 528
FF_SH = DFF // NSHARD
O1_OUT, USED1, ROWS1 = 528, 784, 800
O2_U, O2_D, O2_PLE, O2_PG, USED2, ROWS2 = 704, 1408, 2112, 2176, 2432, 2560
RED1, RED2 = 400, 640

ADAM_LR, ADAM_B1, ADAM_B2, ADAM_EPS, ADAM_WD, ADAM_STEP = 0.001, 0.9, 0.999, 1e-8, 0.01, 10

VMEM_LIMIT = 56 * 1024 * 1024
FFN_FWD_TOKENS = 256
FFN_BWD_TOKENS = 1024


def _cp(**kw):
    return pltpu.CompilerParams(vmem_limit_bytes=VMEM_LIMIT, **kw)


def _mm(a, b):
    return jnp.dot(a.astype(BF), b.astype(BF), preferred_element_type=F32)


def _mm_nt(a, b):
    return lax.dot_general(a.astype(BF), b.astype(BF), (((1,), (1,)), ((), ())), preferred_element_type=F32)


def _mm_tn(a, b):
    return lax.dot_general(a.astype(BF), b.astype(BF), (((0,), (0,)), ((), ())), preferred_element_type=F32)


def _split3(x):
    hi = x.astype(BF)
    r = x - hi.astype(F32)
    mid = r.astype(BF)
    lo = (r - mid.astype(F32)).astype(BF)
    return hi, mid, lo


def _dot3(m, x):
    hi, mid, lo = _split3(x)
    return (jnp.dot(m, hi, preferred_element_type=F32) + jnp.dot(m, mid, preferred_element_type=F32)
            + jnp.dot(m, lo, preferred_element_type=F32))


def _window_sum(ext, w, back):
    n = ext.shape[0]
    s, k = ext, 1
    while k < w:
        s = s + pltpu.roll(s, k if back else n - k, 0)
        k *= 2
    return s


def _rstd(x):
    return lax.rsqrt(jnp.mean(x * x, axis=-1, keepdims=True) + EPS)


def _rms_bwd(dy, x, g):
    r = _rstd(x)
    xh = x * r
    dg = jnp.sum(dy * xh, axis=0, keepdims=True)
    dxh = dy * g
    dx = r * (dxh - xh * jnp.mean(dxh * xh, axis=-1, keepdims=True))
    return dx, dg


def _sigmoid(x):
    return 1.0 / (1.0 + jnp.exp(-x))


ANY = pl.BlockSpec(memory_space=pl.ANY)


def _full(shape):
    n = len(shape)
    return pl.BlockSpec(shape, lambda *_: (0,) * n)


def _resident(shape):
    n = len(shape)
    return pl.BlockSpec(shape, lambda *_: (0,) * n, pipeline_mode=pl.Buffered(1))


def _tile(t):
    return 512 if t % 512 == 0 else t


def _tri(n, upper):
    r, c = np.indices((n, n))
    return ((c >= r) if upper else (c <= r)).astype(BF)


def _aug_consts():
    row, col = np.indices((LANE, NH * LANE))
    piece, head = row // NH, row % NH
    ch, cl = col // LANE, col % LANE
    eq = ((piece < 3) & (head == ch) & (cl == HD + 8 * piece + head)).astype(BF)
    ek = -((piece < 3) & (head == ch) & (cl == HD + 24 + 8 * piece + head)).astype(BF)
    off = (np.arange(NH * LANE) % LANE - HD - np.arange(NH * LANE) // LANE)[None, :]
    rowq = ((off >= 24) & (off < 48) & (off % 8 == 0)).astype(np.float32)
    rowk = ((off >= 0) & (off < 24) & (off % 8 == 0)).astype(np.float32)
    return eq, ek, rowq, rowk


def _in_proj(x, g1, w_in_t, b_pad, tri, eq, ek, rowq, rowk, dep):
    t = x.shape[0]
    tt = _tile(t)

    def body(x_ref, g_ref, w_ref, b_ref, tri_ref, eq_ref, ek_ref, rq_ref, rk_ref, dep_ref,
             qa_ref, ka_ref, v_ref, u_ref, fl_ref, carry):
        i = pl.program_id(0)

        @pl.when(i == 0)
        def _():
            carry[...] = jnp.zeros_like(carry)

        xv = x_ref[...]
        hn = (xv * _rstd(xv) * g_ref[...]).astype(BF)
        z = _mm_nt(hn, w_ref[...])
        fl = z[:, 3 * DA:3 * DA + LANE] + b_ref[...]
        lane = lax.broadcasted_iota(jnp.int32, fl.shape, 1)
        lf = jnp.where(lane < NH, jnp.minimum(fl, 0.0) - jnp.log(1.0 + jnp.exp(-jnp.abs(fl))), 0.0)
        c = carry[...] + _dot3(tri_ref[...], lf)
        carry[...] = carry[...] + jnp.sum(lf, axis=0, keepdims=True)
        hi, mid, lo = _split3(c)
        caug = (hi.astype(F32) + pltpu.roll(mid.astype(F32), NH, 1) + pltpu.roll(lo.astype(F32), 2 * NH, 1)).astype(BF)
        aug_q = jnp.dot(caug, eq_ref[...], preferred_element_type=F32) + rq_ref[...]
        aug_k = jnp.dot(caug, ek_ref[...], preferred_element_type=F32) + rk_ref[...]
        low = lax.broadcasted_iota(jnp.int32, (tt, LANE), 1) < HD
        for p in range(NH // 2):
            qp = z[:, LANE * p:LANE * (p + 1)] * (1.0 / math.sqrt(HD))
            kp = z[:, DA + LANE * p:DA + LANE * (p + 1)]
            for h, (qh, kh) in enumerate(((qp, kp), (pltpu.roll(qp, HD, 1), pltpu.roll(kp, HD, 1)))):
                lo_, hi_ = LANE * (2 * p + h), LANE * (2 * p + h + 1)
                qa_ref[:, lo_:hi_] = jnp.where(low, qh, aug_q[:, lo_:hi_]).astype(BF)
                ka_ref[:, lo_:hi_] = jnp.where(low, kh, aug_k[:, lo_:hi_]).astype(BF)
        v_ref[...] = z[:, 2 * DA:3 * DA].astype(BF)
        u_ref[...] = z[:, 3 * DA + LANE:]
        fl_ref[...] = fl

    return pl.pallas_call(
        body, name="in_proj", grid=(t // tt,),
        in_specs=[pl.BlockSpec((tt, D), lambda i: (i, 0)), _full((1, D)), _resident((ZW, D)), _full((1, LANE)),
                  _full((tt, tt)), _full((LANE, NH * LANE)), _full((LANE, NH * LANE)),
                  _full((1, NH * LANE)), _full((1, NH * LANE)), ANY],
        out_specs=[pl.BlockSpec((tt, NH * LANE), lambda i: (i, 0)), pl.BlockSpec((tt, NH * LANE), lambda i: (i, 0)),
                   pl.BlockSpec((tt, DA), lambda i: (i, 0)), pl.BlockSpec((tt, DP), lambda i: (i, 0)),
                   pl.BlockSpec((tt, LANE), lambda i: (i, 0))],
        out_shape=[jax.ShapeDtypeStruct((t, NH * LANE), BF), jax.ShapeDtypeStruct((t, NH * LANE), BF),
                   jax.ShapeDtypeStruct((t, DA), BF), jax.ShapeDtypeStruct((t, DP), F32),
                   jax.ShapeDtypeStruct((t, LANE), F32)],
        scratch_shapes=[pltpu.VMEM((1, LANE), F32)],
        compiler_params=_cp(),
    )(x, g1, w_in_t, b_pad, tri, eq, ek, rowq, rowk, dep)


def _attn_fwd(qa, ka, v):
    t = qa.shape[0]
    ta = _tile(t)
    n = t // ta

    def body(q_ref, k_ref, v_ref, a_ref, lse_ref, m_ref, l_ref, acc_ref):
        i = pl.program_id(1)
        m_ref[...] = jnp.full_like(m_ref, -1e30)
        l_ref[...] = jnp.zeros_like(l_ref)
        acc_ref[...] = jnp.zeros_like(acc_ref)
        qs = [q_ref[:, LANE * h:LANE * (h + 1)] for h in range(2)]
        reps = ta // LANE

        def tile(j, width, masked):
            rows = pl.ds(pl.multiple_of(j * ta, ta), width * ta)
            v2 = v_ref[rows, :]
            s = [_mm_nt(qs[h], k_ref[rows, LANE * h:LANE * (h + 1)]) for h in range(2)]
            if masked:
                keep = (lax.broadcasted_iota(jnp.int32, (ta, ta), 1) <= lax.broadcasted_iota(jnp.int32, (ta, ta), 0))
                s = [jnp.where(keep, sh, -1e30) for sh in s]
            m_old = [m_ref[h] for h in range(2)]
            m_new = [jnp.maximum(m_old[h], jnp.max(s[h], axis=1, keepdims=True)) for h in range(2)]
            pe = [jnp.exp(s[h] - jnp.tile(m_new[h], (1, width * reps))) for h in range(2)]
            alpha = [jnp.exp(m_old[h] - m_new[h]) for h in range(2)]
            pv = [jnp.dot(pe[h].astype(BF), v2, preferred_element_type=F32) for h in range(2)]
            for h in range(2):
                l_ref[h] = alpha[h] * l_ref[h] + jnp.sum(pe[h], axis=1, keepdims=True)
                acc_ref[h] = alpha[h] * acc_ref[h] + pv[h]
                m_ref[h] = m_new[h]

        def step(jj, carry):
            tile(2 * jj, 2, False)
            return carry

        lax.fori_loop(0, i // 2, step, 0)

        @pl.when(i % 2 == 1)
        def _():
            tile(i - 1, 1, False)

        tile(i, 1, True)
        low = lax.broadcasted_iota(jnp.int32, (ta, LANE), 1) < HD
        a_ref[...] = jnp.where(low, acc_ref[0] / l_ref[0], acc_ref[1] / l_ref[1])
        lse_ref[...] = jnp.where(low, m_ref[0] + jnp.log(l_ref[0]), m_ref[1] + jnp.log(l_ref[1]))

    return pl.pallas_call(
        body, name="attn_fwd", grid=(NH // 2, n),
        in_specs=[pl.BlockSpec((ta, 2 * LANE), lambda p, i: (i, p)),
                  pl.BlockSpec((t, 2 * LANE), lambda p, i: (0, p)),
                  pl.BlockSpec((t, LANE), lambda p, i: (0, p))],
        out_specs=[pl.BlockSpec((ta, LANE), lambda p, i: (i, p)), pl.BlockSpec((ta, LANE), lambda p, i: (i, p))],
        out_shape=[jax.ShapeDtypeStruct((t, DA), F32), jax.ShapeDtypeStruct((t, DA), F32)],
        scratch_shapes=[pltpu.VMEM((2, ta, LANE), F32), pltpu.VMEM((2, ta, LANE), F32), pltpu.VMEM((2, ta, LANE), F32)],
        compiler_params=_cp(),
    )(qa, ka, v)


def _mix_out(a, u, x, w_pool, pool_scale, g_attn, g_pool, w_out, g_post, dep):
    t = a.shape[0]
    tt = _tile(t)
    hb = tt // HALO

    def body(a_ref, u_ref, up_ref, x_ref, wp_ref, ps_ref, ga_ref, gp_ref, wo_ref, go_ref, dep_ref,
             yb_ref, m_ref, o_ref, h1_ref):
        i = pl.program_id(0)
        prev = up_ref[...] * jnp.where(i > 0, 1.0, 0.0)
        tok = i * tt + lax.broadcasted_iota(jnp.int32, (tt, PC), 0)
        ms = []
        for g, w in enumerate(WINS):
            ug = u_ref[:, PC * g:PC * (g + 1)]
            ext = jnp.concatenate([prev[:, PC * g:PC * (g + 1)], ug], axis=0)
            cnt = jnp.minimum(tok + 1, w).astype(F32)
            y = (_window_sum(ext, w, True)[HALO:] / cnt - ug).astype(BF)
            yb_ref[:, PC * g:PC * (g + 1)] = y
            ms.append(_mm(y, wp_ref[g]) * ps_ref[:, PC * g:PC * (g + 1)])
        m = jnp.concatenate(ms, axis=1)
        m_ref[...] = m
        av = a_ref[...]
        mix = jnp.concatenate([av * _rstd(av) * ga_ref[...], m * _rstd(m) * gp_ref[...]], axis=1)
        o = _mm(mix, wo_ref[...])
        o_ref[...] = o
        h1_ref[...] = x_ref[...] + o * _rstd(o) * go_ref[...]

    return pl.pallas_call(
        body, name="mix_out", grid=(t // tt,),
        in_specs=[pl.BlockSpec((tt, DA), lambda i: (i, 0)), pl.BlockSpec((tt, DP), lambda i: (i, 0)),
                  pl.BlockSpec((HALO, DP), lambda i: (jnp.maximum(i * hb - 1, 0), 0)),
                  pl.BlockSpec((tt, D), lambda i: (i, 0)),
                  _full((len(WINS), PC, PC)), _full((1, DP)), _full((1, DA)), _full((1, DP)),
                  _resident((D, D)), _full((1, D)), ANY],
        out_specs=[pl.BlockSpec((tt, DP), lambda i: (i, 0)), pl.BlockSpec((tt, DP), lambda i: (i, 0)),
                   pl.BlockSpec((tt, D), lambda i: (i, 0)), pl.BlockSpec((tt, D), lambda i: (i, 0))],
        out_shape=[jax.ShapeDtypeStruct((t, DP), BF), jax.ShapeDtypeStruct((t, DP), F32),
                   jax.ShapeDtypeStruct((t, D), F32), jax.ShapeDtypeStruct((t, D), F32)],
        compiler_params=_cp(),
    )(a, u, u, x, w_pool, pool_scale, g_attn, g_pool, w_out, g_post, dep)


def _ffn_fwd(h1, g_pre, stacks2, g_post, p, w_ple, g_ple, w_pg, tgt):
    t = h1.shape[0]
    tt = FFN_FWD_TOKENS if t % FFN_FWD_TOKENS == 0 else t

    def body(h1_ref, gpre_ref, wg_ref, wu_ref, wd_ref, gpost_ref, p_ref, wple_ref, gple_ref, wpg_ref, tgt_ref,
             hn_ref, gate_ref, up_ref, dff_ref, dh2_ref, loss_ref, dwpg_ref, dwple_ref, dgple_ref, dgpost_ref):
        i = pl.program_id(0)

        @pl.when(i == 0)
        def _():
            loss_ref[...] = jnp.zeros_like(loss_ref)
            dwpg_ref[...] = jnp.zeros_like(dwpg_ref)
            dwple_ref[...] = jnp.zeros_like(dwple_ref)
            dgple_ref[...] = jnp.zeros_like(dgple_ref)
            dgpost_ref[...] = jnp.zeros_like(dgpost_ref)

        h1v = h1_ref[...]
        hn = (h1v * _rstd(h1v) * gpre_ref[...]).astype(BF)
        hn_ref[...] = hn
        gate = _mm_nt(hn, wg_ref[...].reshape(DFF, D))
        up = _mm_nt(hn, wu_ref[...].reshape(DFF, D))
        gate_ref[...] = gate.astype(BF)
        up_ref[...] = up.astype(BF)
        ff = _mm(gate * _sigmoid(gate) * up, wd_ref[...].reshape(DFF, D))
        rff = _rstd(ff)
        ffh = ff * rff
        gpost = gpost_ref[...]
        h2 = h1v + ffh * gpost
        pv = p_ref[...]
        pe = _mm(pv, wple_ref[...])
        rpe = _rstd(pe)
        peh = pe * rpe
        gple = gple_ref[...]
        e = peh * gple
        sig = _sigmoid(_mm(h2, wpg_ref[...]))
        dv = h2 + sig * e - tgt_ref[...]
        sq = jnp.sum(jnp.sum(dv * dv, axis=1, keepdims=True), axis=0, keepdims=True)
        loss_ref[...] = loss_ref[...] + sq
        dy = dv * (1.0 / D)
        d_e = dy * sig
        d_gl = dy * e * sig * (1.0 - sig)
        dh2 = dy + _mm_nt(d_gl, wpg_ref[...])
        dh2_ref[...] = dh2
        dwpg_ref[...] = dwpg_ref[...] + _mm_tn(h2, d_gl)
        dgple_ref[...] = dgple_ref[...] + jnp.sum(d_e * peh, axis=0, keepdims=True)
        dpeh = d_e * gple
        d_pe = rpe * (dpeh - peh * jnp.mean(dpeh * peh, axis=-1, keepdims=True))
        dwple_ref[...] = dwple_ref[...] + _mm_tn(pv, d_pe)
        dgpost_ref[...] = dgpost_ref[...] + jnp.sum(dh2 * ffh, axis=0, keepdims=True)
        dffh = dh2 * gpost
        dff_ref[...] = (rff * (dffh - ffh * jnp.mean(dffh * ffh, axis=-1, keepdims=True))).astype(BF)

    row = lambda w: pl.BlockSpec((tt, w), lambda i: (i, 0))
    shard_rows = lambda k: pl.BlockSpec((NSHARD, FF_SH, D), lambda i: (0, k, 0), pipeline_mode=pl.Buffered(1))
    return pl.pallas_call(
        body, name="ffn_fwd", grid=(t // tt,),
        in_specs=[row(D), _full((1, D)), shard_rows(0), shard_rows(1), shard_rows(2), _full((1, D)),
                  row(DPLE), _resident((DPLE, D)), _full((1, D)), _resident((D, D)), row(D)],
        out_specs=[row(D), row(DFF), row(DFF), row(D), row(D), _full((8, LANE)), _full((D, D)), _full((DPLE, D)),
                   _full((1, D)), _full((1, D))],
        out_shape=[jax.ShapeDtypeStruct((t, D), BF), jax.ShapeDtypeStruct((t, DFF), BF), jax.ShapeDtypeStruct((t, DFF), BF),
                   jax.ShapeDtypeStruct((t, D), BF), jax.ShapeDtypeStruct((t, D), F32), jax.ShapeDtypeStruct((8, LANE), F32),
                   jax.ShapeDtypeStruct((D, D), F32), jax.ShapeDtypeStruct((DPLE, D), F32),
                   jax.ShapeDtypeStruct((1, D), F32), jax.ShapeDtypeStruct((1, D), F32)],
        compiler_params=_cp(),
    )(h1, g_pre, stacks2, stacks2, stacks2, g_post, p, w_ple, g_ple, w_pg, tgt)


FF_CH = 256


def _ffn_bwd(hn2, gate, up, dff, wg_t, wu_t, wd):
    t = hn2.shape[0]
    tt = FFN_BWD_TOKENS if t % FFN_BWD_TOKENS == 0 else _tile(t)
    nt = t // tt
    ch = FF_CH
    nc = DFF // ch

    def body(hn_ref, gate_ref, up_ref, dff_ref, wg_ref, wu_ref, wd_ref,
             dwg_ref, dwu_ref, dwd_ref, dhn_ref, acc, gu_acc, d_acc, sem):
        j, i = pl.program_id(0), pl.program_id(1)

        @pl.when(j == 0)
        def _():
            acc[pl.ds(pl.multiple_of(i * tt, tt), tt), :] = jnp.zeros((tt, D), F32)

        @pl.when(i == 0)
        def _():
            gu_acc[...] = jnp.zeros_like(gu_acc)
            d_acc[...] = jnp.zeros_like(d_acc)

        wgu = jnp.concatenate([wg_ref[...], wu_ref[...]], axis=0)
        parts = 4 if tt % 1024 == 0 else 2
        half = tt // parts
        acts, dgus = [], []
        for hh in range(parts):
            r = slice(hh * half, (hh + 1) * half)
            gate_v = gate_ref[r, :].astype(F32)
            up_v = up_ref[r, :].astype(F32)
            sg = _sigmoid(gate_v)
            silu = gate_v * sg
            d_act = _mm_nt(dff_ref[r, :], wd_ref[...])
            d_up = (d_act * silu).astype(BF)
            d_gate = (d_act * up_v * (sg * (1.0 + gate_v * (1.0 - sg)))).astype(BF)
            dgu = jnp.concatenate([d_gate, d_up], axis=1)
            rows = pl.ds(pl.multiple_of(i * tt + hh * half, half), half)
            acc[rows, :] = acc[rows, :] + jnp.dot(dgu, wgu, preferred_element_type=F32)
            acts.append((silu * up_v).astype(BF))
            dgus.append(dgu)
        d_acc[...] = d_acc[...] + _mm_tn(jnp.concatenate(acts, axis=0), dff_ref[...])
        gu_acc[...] = gu_acc[...] + _mm_tn(jnp.concatenate(dgus, axis=0), hn_ref[...])

        @pl.when(i == nt - 1)
        def _():
            dwg_ref[...] = gu_acc[:ch].astype(BF)
            dwu_ref[...] = gu_acc[ch:].astype(BF)
            dwd_ref[...] = d_acc[...].astype(BF)

        @pl.when((j == nc - 1) & (i == nt - 1))
        def _():
            cp = pltpu.make_async_copy(acc, dhn_ref, sem)
            cp.start()
            cp.wait()

    tok = lambda w: pl.BlockSpec((tt, w), lambda j, i: (i, 0))
    chunk = pl.BlockSpec((ch, D), lambda j, i: (j, 0))
    return pl.pallas_call(
        body, name="ffn_bwd", grid=(nc, nt),
        in_specs=[tok(D), pl.BlockSpec((tt, ch), lambda j, i: (i, j)), pl.BlockSpec((tt, ch), lambda j, i: (i, j)),
                  tok(D), chunk, chunk, chunk],
        out_specs=[chunk, chunk, chunk, pl.BlockSpec(memory_space=pl.ANY)],
        out_shape=[jax.ShapeDtypeStruct((DFF, D), BF), jax.ShapeDtypeStruct((DFF, D), BF),
                   jax.ShapeDtypeStruct((DFF, D), BF), jax.ShapeDtypeStruct((t, D), F32)],
        scratch_shapes=[pltpu.VMEM((t, D), F32), pltpu.VMEM((2 * ch, D), F32), pltpu.VMEM((ch, D), F32),
                        pltpu.SemaphoreType.DMA],
        compiler_params=_cp(),
    )(hn2, gate, up, dff, wg_t, wu_t, wd)


def _mix_bwd(d_hn2, dh2, h1, o, a, m, yb, g_ffn_pre, g_post, g_attn, g_pool, w_out, w_pool, pool_scale, dep):
    t = a.shape[0]
    tt = _tile(t)

    nt = t // tt
    depth = 3

    def body(dhn_hbm, dh2_hbm, h1_hbm, o_hbm, a_ref, m_ref, yb_ref, gfp_ref, go_ref, ga_ref, gp_ref, wo_ref, wp_ref,
             ps_ref, dep_ref, dh1_ref, da_ref, dyc_ref, dgfp_ref, dgo_ref, dga_ref, dgp_ref, dps_ref, dwp_ref, dwo_ref,
             wide, sems):
        i = pl.program_id(0)

        def fetches(step):
            slot = step % depth
            rows = pl.ds(pl.multiple_of(step * tt, tt), tt)
            return [pltpu.make_async_copy(src.at[rows, :], wide.at[k, slot], sems.at[k, slot])
                    for k, src in enumerate((dhn_hbm, dh2_hbm, h1_hbm, o_hbm))]

        @pl.when(i == 0)
        def _():
            for r in (dgfp_ref, dgo_ref, dga_ref, dgp_ref, dps_ref, dwp_ref, dwo_ref):
                r[...] = jnp.zeros_like(r)
            for ahead in range(min(depth - 1, nt)):
                for cp in fetches(ahead):
                    cp.start()

        @pl.when(i + depth - 1 < nt)
        def _():
            for cp in fetches(i + depth - 1):
                cp.start()

        for cp in fetches(i):
            cp.wait()
        dhn_v, dh2_v, h1_v, o_v = [wide[k, i % depth] for k in range(4)]
        d1, dg = _rms_bwd(dhn_v, h1_v, gfp_ref[...])
        dgfp_ref[...] = dgfp_ref[...] + dg
        dh1 = dh2_v + d1
        dh1_ref[...] = dh1
        d_o, dg = _rms_bwd(dh1, o_v, go_ref[...])
        dgo_ref[...] = dgo_ref[...] + dg
        d_mix = _mm_nt(d_o, wo_ref[...])
        av, mv = a_ref[...], m_ref[...]
        mix = jnp.concatenate([av * _rstd(av) * ga_ref[...], mv * _rstd(mv) * gp_ref[...]], axis=1)
        dwo_ref[...] = dwo_ref[...] + _mm_tn(mix, d_o)
        d_a, dg = _rms_bwd(d_mix[:, :DA], av, ga_ref[...])
        dga_ref[...] = dga_ref[...] + dg
        da_ref[...] = d_a
        d_m, dg = _rms_bwd(d_mix[:, DA:], mv, gp_ref[...])
        dgp_ref[...] = dgp_ref[...] + dg
        tok = i * tt + lax.broadcasted_iota(jnp.int32, (tt, PC), 0)
        dps = []
        for g, w in enumerate(WINS):
            sl = slice(PC * g, PC * (g + 1))
            ybg = yb_ref[:, sl]
            wpg = wp_ref[g].astype(BF)
            mlin = jnp.dot(ybg, wpg, preferred_element_type=F32)
            dmg = d_m[:, sl]
            dps.append(jnp.sum(dmg * mlin, axis=0, keepdims=True))
            dml = (dmg * ps_ref[:, sl]).astype(BF)
            dwp_ref[g] = dwp_ref[g] + _mm_tn(ybg, dml)
            dyc_ref[:, sl] = _mm_nt(dml, wpg) / jnp.minimum(tok + 1, w).astype(F32)
        dps_ref[...] = dps_ref[...] + jnp.concatenate(dps, axis=1)

    row = lambda w: pl.BlockSpec((tt, w), lambda i: (i, 0))
    return pl.pallas_call(
        body, name="mix_bwd", grid=(nt,),
        in_specs=[ANY, ANY, ANY, ANY, row(DA), row(DP), row(DP), _full((1, D)), _full((1, D)),
                  _full((1, DA)), _full((1, DP)), _resident((D, D)), _full((len(WINS), PC, PC)), _full((1, DP)), ANY],
        out_specs=[row(D), row(DA), row(DP), _full((1, D)), _full((1, D)), _full((1, DA)), _full((1, DP)),
                   _full((1, DP)), _full((len(WINS), PC, PC)), _full((D, D))],
        out_shape=[jax.ShapeDtypeStruct((t, D), F32), jax.ShapeDtypeStruct((t, DA), F32), jax.ShapeDtypeStruct((t, DP), F32),
                   jax.ShapeDtypeStruct((1, D), F32), jax.ShapeDtypeStruct((1, D), F32), jax.ShapeDtypeStruct((1, DA), F32),
                   jax.ShapeDtypeStruct((1, DP), F32), jax.ShapeDtypeStruct((1, DP), F32),
                   jax.ShapeDtypeStruct((len(WINS), PC, PC), F32), jax.ShapeDtypeStruct((D, D), F32)],
        scratch_shapes=[pltpu.VMEM((4, depth, tt, D), F32), pltpu.SemaphoreType.DMA((4, depth))],
        compiler_params=_cp(),
    )(d_hn2, dh2, h1, o, a, m, yb, g_ffn_pre, g_post, g_attn, g_pool, w_out, w_pool, pool_scale, dep)


def _attn_bwd(qa, ka, v, a, d_a, lse, dep):
    t = qa.shape[0]
    ta = _tile(t)
    n = t // ta

    def body(q_ref, k_ref, v_ref, o_ref, do_ref, lse_ref, dep_ref, dq_ref, dk_ref, dv_ref):
        j = pl.program_id(1)

        @pl.when(j == 0)
        def _():
            dq_ref[...] = jnp.zeros_like(dq_ref)

        dk_ref[...] = jnp.zeros_like(dk_ref)
        dv_ref[...] = jnp.zeros_like(dv_ref)
        ks = [k_ref[:, LANE * h:LANE * (h + 1)] for h in range(2)]
        v2 = v_ref[...]
        lane = lax.broadcasted_iota(jnp.int32, (1, LANE), 1)
        mine = [lane < HD, lane >= HD]

        def block(q0, nq, nk, shift):
            rows = pl.ds(pl.multiple_of(q0, nq), nq)
            do2 = do_ref[rows, :]
            prod = do2 * o_ref[rows, :]
            lse2 = lse_ref[rows, :]
            do2b = do2.astype(BF)
            qh = [q_ref[rows, LANE * h:LANE * (h + 1)] for h in range(2)]
            kk, vv = [kh[:nk] for kh in ks], v2[:nk]
            s = [_mm_nt(qh[h], kk[h]) for h in range(2)]
            dp = [_mm_nt(jnp.where(mine[h], do2, 0.0), vv) for h in range(2)]
            delta = [jnp.sum(jnp.where(mine[h], prod, 0.0), axis=1, keepdims=True) for h in range(2)]
            lse_h = [jnp.sum(jnp.where(lane == HD * h, lse2, 0.0), axis=1, keepdims=True) for h in range(2)]
            pr = [jnp.exp(s[h] - lse_h[h]) for h in range(2)]
            if shift is not None:
                keep = (lax.broadcasted_iota(jnp.int32, (nq, nk), 1)
                        <= lax.broadcasted_iota(jnp.int32, (nq, nk), 0) + shift)
                pr = [jnp.where(keep, ph, 0.0) for ph in pr]
            ds = [(pr[h] * (dp[h] - delta[h])).astype(BF) for h in range(2)]
            dv_ref[:nk, :] = dv_ref[:nk, :] + jnp.where(mine[0], _mm_tn(pr[0], do2b), _mm_tn(pr[1], do2b))
            for h in range(2):
                sl = slice(LANE * h, LANE * (h + 1))
                dk_ref[:nk, sl] = dk_ref[:nk, sl] + _mm_tn(ds[h], qh[h])
                dq_ref[0, rows, sl] = dq_ref[0, rows, sl] + jnp.dot(ds[h], kk[h], preferred_element_type=F32)

        def step(i, carry):
            block(i * ta, ta, ta, None)
            return carry

        half = ta // 2
        block(j * ta, half, half, 0)
        block(j * ta + half, half, ta, half)
        lax.fori_loop(j + 1, n, step, 0)

    qrow = lambda w: pl.BlockSpec((t, w), lambda p, j: (0, p))
    krow = lambda w: pl.BlockSpec((ta, w), lambda p, j: (j, p))
    return pl.pallas_call(
        body, name="attn_bwd", grid=(NH // 2, n),
        in_specs=[qrow(2 * LANE), krow(2 * LANE), krow(LANE), qrow(LANE), qrow(LANE), qrow(LANE), ANY],
        out_specs=[pl.BlockSpec((1, t, 2 * LANE), lambda p, j: (p, 0, 0)), krow(2 * LANE), krow(LANE)],
        out_shape=[jax.ShapeDtypeStruct((NH // 2, t, 2 * LANE), F32), jax.ShapeDtypeStruct((t, NH * LANE), F32),
                   jax.ShapeDtypeStruct((t, DA), F32)],
        compiler_params=_cp(),
    )(qa, ka, v, a, d_a, lse, dep)


def _in_bwd(dqa, dka, dv, dyc, fl, x, dh1, g1, w_in_t, tri_u, dep):
    t = x.shape[0]
    tt = _tile(t)
    nt = t // tt
    hb = tt // HALO
    rev = lambda s: nt - 1 - s

    def body(dqa_ref, dka_ref, dv_ref, dyc_ref, dyn_ref, fl_ref, x_ref, dh1_ref, g_ref, w_ref, tri_ref,
             dep_ref, dx_ref, dw_ref, dg_ref, db_ref, carry, acc, stage, sem):
        s = pl.program_id(0)
        i = nt - 1 - s

        @pl.when(s == 0)
        def _():
            carry[...] = jnp.zeros_like(carry)
            acc[...] = jnp.zeros_like(acc)
            dg_ref[...] = jnp.zeros_like(dg_ref)
            db_ref[...] = jnp.zeros_like(db_ref)

        dq_cat = jnp.concatenate([dqa_ref[p] for p in range(NH // 2)], axis=1)
        dk_cat = dka_ref[...]
        off = lax.broadcasted_iota(jnp.int32, (1, NH * LANE), 1)
        off = off % LANE - HD - off // LANE

        def picked(cat, lane_off):
            kept = jnp.where(off == lane_off, cat, 0.0)
            return functools.reduce(lambda a, b: a + b, [kept[:, LANE * h:LANE * (h + 1)] for h in range(NH)])

        dc = pltpu.roll(picked(dq_cat, 0), LANE - HD, 1) - pltpu.roll(picked(dk_cat, 24), LANE - HD - 24, 1)
        dlf = carry[...] + _dot3(tri_ref[...], dc)
        carry[...] = carry[...] + jnp.sum(dc, axis=0, keepdims=True)
        flv = fl_ref[...]
        lane = lax.broadcasted_iota(jnp.int32, flv.shape, 1)
        d_fl = jnp.where(lane < NH, dlf / (1.0 + jnp.exp(flv)), 0.0)
        db_ref[...] = db_ref[...] + jnp.sum(d_fl, axis=0, keepdims=True)
        low = lax.broadcasted_iota(jnp.int32, (tt, LANE), 1) < HD
        dqs, dks = [], []
        for p in range(NH // 2):
            b0, b1 = slice(2 * LANE * p, 2 * LANE * p + LANE), slice(2 * LANE * p + LANE, 2 * LANE * (p + 1))
            dqs.append(jnp.where(low, dq_cat[:, b0], pltpu.roll(dq_cat[:, b1], HD, 1)) * (1.0 / math.sqrt(HD)))
            dks.append(jnp.where(low, dk_cat[:, b0], pltpu.roll(dk_cat[:, b1], HD, 1)))
        nxt = dyn_ref[...] * jnp.where(i < nt - 1, 1.0, 0.0)
        tok = i * tt + lax.broadcasted_iota(jnp.int32, (tt, PC), 0)
        dus = []
        for g, w in enumerate(WINS):
            sl = slice(PC * g, PC * (g + 1))
            dycg = dyc_ref[:, sl]
            ext = jnp.concatenate([dycg, nxt[:, sl]], axis=0)
            dus.append(_window_sum(ext, w, False)[:tt] - dycg * jnp.minimum(tok + 1, w).astype(F32))
        d_z = jnp.concatenate(dqs + dks + [dv_ref[...], d_fl] + dus, axis=1).astype(BF)
        xv = x_ref[...]
        gv = g_ref[...]
        hn = (xv * _rstd(xv) * gv).astype(BF)
        d_hn = jnp.dot(d_z, w_ref[...], preferred_element_type=F32)
        acc[...] = acc[...] + _mm_tn(d_z, hn)
        d1, dg = _rms_bwd(d_hn, xv, gv)
        dg_ref[...] = dg_ref[...] + dg
        dx_ref[...] = dh1_ref[...] + d1

        @pl.when(s == nt - 1)
        def _():
            stage[...] = acc[...].astype(BF)
            cp = pltpu.make_async_copy(stage, dw_ref, sem)
            cp.start()
            cp.wait()

    row = lambda w: pl.BlockSpec((tt, w), lambda s: (rev(s), 0))
    return pl.pallas_call(
        body, name="in_bwd", grid=(nt,),
        in_specs=[pl.BlockSpec((NH // 2, tt, 2 * LANE), lambda s: (0, rev(s), 0)), row(NH * LANE), row(DA), row(DP),
                  pl.BlockSpec((HALO, DP), lambda s: (jnp.minimum((rev(s) + 1) * hb, nt * hb - 1), 0)),
                  row(LANE), row(D), row(D), _full((1, D)), _resident((ZW, D)), _full((tt, tt)),
                  ANY],
        out_specs=[row(D), pl.BlockSpec(memory_space=pl.ANY), _full((1, D)), _full((1, LANE))],
        out_shape=[jax.ShapeDtypeStruct((t, D), F32), jax.ShapeDtypeStruct((ZW, D), BF),
                   jax.ShapeDtypeStruct((1, D), F32), jax.ShapeDtypeStruct((1, LANE), F32)],
        scratch_shapes=[pltpu.VMEM((1, LANE), F32), pltpu.VMEM((ZW, D), F32), pltpu.VMEM((ZW, D), BF),
                        pltpu.SemaphoreType.DMA],
        compiler_params=_cp(),
    )(dqa, dka, dv, dyc, dyc, fl, x, dh1, g1, w_in_t, tri_u, dep)


class _NoComm:
    def __init__(self, w2):
        self.w2 = w2
        self.dep = jnp.zeros((8, LANE), F32)

    def after_attention(self, after):
        return self.dep

    def weights2(self, after):
        return _stack2_full(*self.w2)

    def after_ffn(self, grads2):
        self.grads2 = grads2
        return self.dep

    def after_mix(self, after, early):
        self.early = early
        return self.dep

    def after_attn(self, after):
        return self.dep


def _local_step(x, p, tgt, sm, w1, comm):
    w_in_t, w_out = w1
    tt = _tile(x.shape[0])
    eq, ek, rowq, rowk = _aug_consts()
    b_pad = jnp.pad(sm["b_forget"], ((0, 0), (0, LANE - NH)))
    qa, ka, v, u, fl = _in_proj(x, sm["g_mix_pre"], w_in_t, b_pad, _tri(tt, False), eq, ek, rowq, rowk, comm.dep)
    a, lse = _attn_fwd(qa, ka, v)
    yb, m, o, h1 = _mix_out(a, u, x, sm["w_pool"], sm["pool_scale"], sm["g_attn_grp"],
                            sm["g_pool_grp"], w_out, sm["g_mix_post"], comm.after_attention(a))
    stacks2 = comm.weights2(h1)
    wg_t, wu_t, wd, w_ple, w_pg = _unstack2_full(stacks2)
    hn2, gate, up, dff, dh2, loss, dwpg, dwple, dgple, dgfpost = _ffn_fwd(
        h1, sm["g_ffn_pre"], stacks2, sm["g_ffn_post"], p, w_ple, sm["g_ple"], w_pg, tgt)
    dwg_t, dwu_t, dwd, d_hn2 = _ffn_bwd(hn2, gate, up, dff, wg_t, wu_t, wd)
    dep = comm.after_ffn((dwg_t, dwu_t, dwd, dwple, dwpg))
    dh1, d_a, dyc, dgfpre, dgpost, dgattn, dgpool, dps, dwpool, dwout = _mix_bwd(
        d_hn2, dh2, h1, o, a, m, yb, sm["g_ffn_pre"], sm["g_mix_post"], sm["g_attn_grp"], sm["g_pool_grp"],
        w_out, sm["w_pool"], sm["pool_scale"], dep)
    early = dict(loss=loss[0:1, 0:1], g_attn_grp=dgattn, g_pool_grp=dgpool, w_pool=dwpool, pool_scale=dps,
                 g_mix_post=dgpost, g_ffn_pre=dgfpre, g_ffn_post=dgfpost, g_ple=dgple)
    dqa, dka, dvv = _attn_bwd(qa, ka, v, a, d_a, lse, comm.after_mix(dh1, early))
    dx, dwin_t, dg1, dbf = _in_bwd(dqa, dka, dvv, dyc, fl, x, dh1, sm["g_mix_pre"], w_in_t, _tri(tt, True),
                                   comm.after_attn(dvv))
    return dx, (dwin_t, dwout), dict(g_mix_pre=dg1, b_forget=dbf[:, :NH])


def _place():
    x, y, c = lax.axis_index("x"), lax.axis_index("y"), lax.axis_index("c")
    return x, y, c, [(1 - x, y), (x, 1 - y), (1 - x, 1 - y)]


def _rows(c, h):
    return pl.ds(pl.multiple_of(c * h, 16), h)


def _plan_gather(h):
    def plan(src, land):
        x, y, c, chips = _place()
        return [(src.at[_rows(c, h), :], land.at[2 * x + y, _rows(c, h), :], (cx, cy, c),
                 land.at[2 * cx + cy, _rows(c, h), :]) for cx, cy in chips]
    return plan


def _plan_forward(h):
    def plan(land_in, own, land):
        x, y, c, chips = _place()
        sib, me = (x, y, 1 - c), 2 * x + y
        return ([(land_in.at[2 * cx + cy, _rows(c, h), :], land.at[2 * cx + cy, _rows(c, h), :], sib,
                  land.at[2 * cx + cy, _rows(1 - c, h), :]) for cx, cy in chips]
                + [(own, land.at[me], sib, land.at[me])])
    return plan


def _plan_swap_halves(h):
    def plan(buf_in, buf):
        x, y, c, _ = _place()
        return [(buf_in.at[_rows(c, h), :], buf.at[_rows(c, h), :], (x, y, 1 - c), buf.at[_rows(1 - c, h), :])]
    return plan


def _plan_pair_rows(h):
    def plan(src, land):
        x, y, c, _ = _place()
        return [(src.at[:, _rows(1 - c, h), :], land, (x, y, 1 - c), land)]
    return plan


def _plan_scatter(src, land):
    x, y, c, chips = _place()
    return [(src.at[2 * cx + cy], land.at[k], (cx, cy, c), land.at[k]) for k, (cx, cy) in enumerate(chips)]


def _plan_scatter8(h):
    def plan(src, land):
        x, y, c, chips = _place()
        copies = [(src.at[2 * x + y, _rows(1 - c, h), :], land.at[0], (x, y, 1 - c), land.at[0])]
        for k, (cx, cy) in enumerate(chips):
            for d, other in enumerate((c, 1 - c)):
                copies.append((src.at[2 * cx + cy, _rows(other, h), :], land.at[1 + 2 * k + c], (cx, cy, other),
                               land.at[1 + 2 * k + other]))
        return copies
    return plan


def _plan_all(src, land):
    x, y, c, _ = _place()
    copies = []
    for r in range(1, 8):
        px, py, pc = (1 - a if b else a for a, b in zip((x, y, c), (r >> 2 & 1, r >> 1 & 1, r & 1)))
        copies.append((src, land.at[4 * x + 2 * y + c], (px, py, pc), land.at[4 * px + 2 * py + pc]))
    return copies


def _remote(src, dst, send_sems, recv_sems, k, peer):
    return pltpu.make_async_remote_copy(src_ref=src, dst_ref=dst, send_sem=send_sems.at[k], recv_sem=recv_sems.at[k],
                                        device_id=peer, device_id_type=MESH)


def _exchange_inplace(name, n, buf, extra, plan):
    def body(*refs):
        ins, buf_ref, send_sems, recv_sems = refs[:1 + len(extra)], refs[1 + len(extra)], refs[-2], refs[-1]
        copies = plan(*ins, buf_ref)
        for k, (s, d, peer, _) in enumerate(copies):
            _remote(s, d, send_sems, recv_sems, k, peer).start()
        for k, (s, _, peer, mine) in enumerate(copies):
            _remote(s, mine, send_sems, recv_sems, k, peer).wait_recv()
        for k, (s, d, peer, _) in enumerate(copies):
            _remote(s, d, send_sems, recv_sems, k, peer).wait_send()

    return pl.pallas_call(
        body, name=name, in_specs=[ANY] * (1 + len(extra)), out_specs=ANY, out_shape=_sds(buf.shape, buf.dtype),
        input_output_aliases={0: 0},
        scratch_shapes=[pltpu.SemaphoreType.DMA((n,)), pltpu.SemaphoreType.DMA((n,))],
    )(buf, *extra)


HBM = pl.BlockSpec(memory_space=pltpu.HBM)
SEM = pl.BlockSpec(memory_space=pltpu.SEMAPHORE)
EFFECT = pltpu.SideEffectType.DATAFLOW_SIDE_EFFECTING


def _exchange_start(name, n, src, land, plan):
    def body(src_ref, land_ref, send_sems, recv_sems, src_thru, land_thru, token):
        for k, (s, d, peer, _) in enumerate(plan(src_ref, land_ref)):
            _remote(s, d, send_sems, recv_sems, k, peer).start()
        token[...] = jnp.zeros_like(token)

    return pl.pallas_call(
        body, name=name,
        out_shape=(pltpu.SemaphoreType.DMA((n,)), pltpu.SemaphoreType.DMA((n,)), pltpu.HBM(src.shape, src.dtype),
                   pltpu.HBM(land.shape, land.dtype), jax.ShapeDtypeStruct((8, LANE), F32)),
        in_specs=(HBM, HBM), out_specs=(SEM, SEM, HBM, HBM, pl.BlockSpec(memory_space=pltpu.VMEM)),
        input_output_aliases={0: 2, 1: 3},
        compiler_params=pltpu.CompilerParams(has_side_effects=EFFECT),
    )(pltpu.with_memory_space_constraint(src, pltpu.HBM), pltpu.with_memory_space_constraint(land, pltpu.HBM))


def _exchange_wait(name, started, plan, after):
    send_sems, recv_sems, src, land, _ = started

    def body(src_ref, land_ref, send_sems, recv_sems, after_ref, src_out, land_out):
        for k, (s, _, peer, mine) in enumerate(plan(src_ref, land_ref)):
            cp = _remote(s, mine, send_sems, recv_sems, k, peer)
            cp.wait_send()
            cp.wait_recv()

    return pl.pallas_call(
        body, name=name, out_shape=(pltpu.HBM(src.shape, src.dtype), pltpu.HBM(land.shape, land.dtype)),
        in_specs=(HBM, HBM, SEM, SEM, ANY), out_specs=(HBM, HBM), input_output_aliases={0: 0, 1: 1},
        compiler_params=pltpu.CompilerParams(has_side_effects=EFFECT),
    )(src, land, send_sems, recv_sems, after)


def _pair_sum(name, cidx, g, recv, br):
    h = recv.shape[1]
    nb = h // br

    def body(c_ref, g_ref, r_ref, out_ref):
        out_ref[...] = (g_ref[...].astype(F32) + r_ref[...].astype(F32)).astype(BF)

    return pl.pallas_call(
        body, name=name,
        grid_spec=pltpu.PrefetchScalarGridSpec(
            num_scalar_prefetch=1, grid=(NSHARD, nb),
            in_specs=[pl.BlockSpec((1, br, D), lambda s, i, c: (s, c[0] * nb + i, 0)),
                      pl.BlockSpec((1, br, D), lambda s, i, c: (s, i, 0))],
            out_specs=pl.BlockSpec((1, br, D), lambda s, i, c: (s, i, 0))),
        out_shape=jax.ShapeDtypeStruct((NSHARD, h, D), BF),
    )(cidx, g, recv)


def _chip_sum(name, place, pb, y, br):
    h = y.shape[1]
    nb = h // br

    def body(pl_ref, p_ref, y_ref, out_ref):
        acc = p_ref[0].astype(F32)
        for k in range(NSHARD - 1):
            acc = acc + y_ref[k].astype(F32)
        out_ref[...] = acc

    return pl.pallas_call(
        body, name=name,
        grid_spec=pltpu.PrefetchScalarGridSpec(
            num_scalar_prefetch=1, grid=(nb,),
            in_specs=[pl.BlockSpec((1, br, D), lambda i, s: (s[0], i, 0)),
                      pl.BlockSpec((NSHARD - 1, br, D), lambda i, s: (0, i, 0))],
            out_specs=pl.BlockSpec((br, D), lambda i, s: (s[1] * nb + i, 0))),
        out_shape=jax.ShapeDtypeStruct((2 * h, D), F32),
    )(place, pb, y)


def _sum8(name, place, g, land, br):
    h = land.shape[1]
    nb = h // br

    def body(pl_ref, g_ref, y_ref, out_ref):
        acc = g_ref[0].astype(F32)
        for k in range(land.shape[0]):
            acc = acc + y_ref[k].astype(F32)
        out_ref[...] = acc

    return pl.pallas_call(
        body, name=name,
        grid_spec=pltpu.PrefetchScalarGridSpec(
            num_scalar_prefetch=1, grid=(nb,),
            in_specs=[pl.BlockSpec((1, br, D), lambda i, s: (s[0], s[1] * nb + i, 0)),
                      pl.BlockSpec((land.shape[0], br, D), lambda i, s: (0, i, 0))],
            out_specs=pl.BlockSpec((br, D), lambda i, s: (s[1] * nb + i, 0))),
        out_shape=jax.ShapeDtypeStruct((2 * h, D), F32), compiler_params=_cp(),
    )(place, g, land)


def _sum_slots(name, v):
    def body(in_ref, out_ref):
        acc = in_ref[0]
        for k in range(1, 8):
            acc = acc + in_ref[k]
        out_ref[...] = acc

    vm = pl.BlockSpec(memory_space=pltpu.VMEM)
    return pl.pallas_call(body, name=name, in_specs=[vm], out_specs=vm,
                          out_shape=jax.ShapeDtypeStruct(v.shape[1:], F32))(v)


def _adamw_math(w, g, m, v):
    m = ADAM_B1 * m + (1.0 - ADAM_B1) * g
    v = ADAM_B2 * v + (1.0 - ADAM_B2) * (g * g)
    m_hat = m / (1.0 - ADAM_B1 ** ADAM_STEP)
    v_hat = v / (1.0 - ADAM_B2 ** ADAM_STEP)
    delta = -ADAM_LR * (m_hat / (jnp.sqrt(v_hat) + ADAM_EPS) + ADAM_WD * w)
    return delta, m, v


def _adamw(w, g, m, v, dep, row0=None):
    r, c = w.shape
    br = next(b for b in (352, 256, 128, r) if r % b == 0 and (row0 or 0) % b == 0)
    first = (row0 or 0) // br

    def body(w_ref, g_ref, m_ref, v_ref, dep_ref, *outs):
        gv = g_ref[...]
        outs[-3][...], outs[-2][...], outs[-1][...] = _adamw_math(w_ref[...], gv, m_ref[...], v_ref[...])
        if row0 is not None:
            outs[0][...] = gv

    spec = pl.BlockSpec((br, c), lambda i: (i, 0))
    n_out = 3 if row0 is None else 4
    out = pl.pallas_call(
        body, name="adamw", grid=(r // br,),
        in_specs=[spec, pl.BlockSpec((br, c), lambda i: (first + i, 0)), spec, spec, ANY], out_specs=[spec] * n_out,
        out_shape=[jax.ShapeDtypeStruct((r, c), F32)] * n_out, compiler_params=_cp(),
    )(w, g, m, v, dep)
    return out if row0 is not None else [g] + list(out)


def _adamw_small(ws, gs, ms, vs):
    n = len(ws)

    def body(*refs):
        ins, outs = refs[:4 * n], refs[4 * n:]
        for k in range(n):
            d, m, v = _adamw_math(ins[k][...], ins[n + k][...], ins[2 * n + k][...], ins[3 * n + k][...])
            outs[k][...] = d
            outs[n + k][...] = m
            outs[2 * n + k][...] = v

    vm = pl.BlockSpec(memory_space=pltpu.VMEM)
    out = pl.pallas_call(
        body, name="adamw_small", in_specs=[vm] * (4 * n), out_specs=[vm] * (3 * n),
        out_shape=[jax.ShapeDtypeStruct(w.shape, F32) for w in ws] * 3,
    )(*ws, *gs, *ms, *vs)
    return out[:n], out[n:2 * n], out[2 * n:]


BIG = ("w_in", "w_out", "w_ffn_gate", "w_ffn_up", "w_ffn_down", "w_ple_proj", "w_ple_gate")
SMALL = ("g_mix_pre", "b_forget", "g_attn_grp", "g_pool_grp", "w_pool", "pool_scale", "g_mix_post", "g_ffn_pre",
         "g_ffn_post", "g_ple")
TRANSPOSED = ("w_in", "w_ffn_gate", "w_ffn_up")
VECTORS = tuple(n for n in SMALL if n != "w_pool")
ORDER = ("g_mix_pre", "w_in", "b_forget", "g_attn_grp", "g_pool_grp", "w_pool", "pool_scale", "w_out", "g_mix_post",
         "g_ffn_pre", "w_ffn_gate", "w_ffn_up", "w_ffn_down", "g_ffn_post", "w_ple_proj", "g_ple", "w_ple_gate")


def _pad_rows(a, rows):
    return jnp.pad(a, ((0, rows - a.shape[0]), (0, 0)))


def _stack1(w_in, w_out):
    return _pad_rows(jnp.concatenate([_pad_rows(w_in.T, IN_PAD), w_out], axis=0), ROWS1)


def _stack2(wg, wu, wd, wple, wpg):
    return _pad_rows(jnp.concatenate([wg.T, wu.T, wd, wple.reshape(DPLE // NSHARD, D), wpg], axis=0), ROWS2)


def _unstack1(s):
    return s[:IN_SH], s[O1_OUT:USED1]


def _cat(g, lo, hi):
    return g[:, lo:hi].reshape(NSHARD * (hi - lo), D)


def _unstack1_full(g):
    w_in_t = _cat(g, 0, IN_SH)
    w_in_t = jnp.concatenate([w_in_t[:3 * DA], _pad_rows(w_in_t[3 * DA:3 * DA + NH], LANE), w_in_t[3 * DA + NH:]], axis=0)
    return w_in_t, _cat(g, O1_OUT, USED1)


def _unstack2_full(g):
    w_ple = g[:, O2_PLE:O2_PG].reshape(NSHARD, DPLE, DPLE).transpose(1, 0, 2).reshape(DPLE, D)
    return _cat(g, 0, O2_U), _cat(g, O2_U, O2_D), _cat(g, O2_D, O2_PLE), w_ple, _cat(g, O2_PG, USED2)


def _shards(a):
    return a.reshape(NSHARD, a.shape[0] // NSHARD, D)


def _stack1_full(dwin_t, dwout):
    dwin_t = jnp.concatenate([dwin_t[:3 * DA + NH], dwin_t[3 * DA + LANE:]], axis=0).reshape(NSHARD, IN_SH, D)
    zeros = lambda r: jnp.zeros((NSHARD, r, D), dwin_t.dtype)
    return jnp.concatenate([dwin_t, zeros(IN_PAD - IN_SH), _shards(dwout), zeros(ROWS1 - USED1)], axis=1)


def _stack2_full(dwg_t, dwu_t, dwd, dwple, dwpg):
    dwple = dwple.reshape(DPLE, NSHARD, DPLE).transpose(1, 0, 2).reshape(NSHARD, DPLE // NSHARD, D)
    return jnp.concatenate([_shards(dwg_t), _shards(dwu_t), _shards(dwd), dwple, _shards(dwpg),
                            jnp.zeros((NSHARD, ROWS2 - USED2, D), dwd.dtype)], axis=1)


def _sds(shape, dtype):
    return jax.ShapeDtypeStruct(shape, dtype)


class _Comm:
    def __init__(self, stack2, me, c):
        self.stack2, self.me, self.c = stack2, me, c
        self.cidx = c.astype(jnp.int32).reshape(1)
        self.place = jnp.stack([me, c]).astype(jnp.int32)
        self.h = ROWS2 // 2
        self.gather = _exchange_start("gather2_start", 3, stack2, lax.empty((NSHARD, ROWS2, D), BF), _plan_gather(self.h))
        self.dep = self.gather[4]

    def after_attention(self, after):
        own, land = _exchange_wait("gather2_wait", self.gather, _plan_gather(self.h), after)
        fwd = _plan_forward(self.h)
        self.forward = lambda own_ref, land_ref: fwd(land_ref, own_ref, land_ref)
        self.passing = _exchange_start("forward2_start", 4, own, land, self.forward)
        return self.passing[4]

    def weights2(self, after):
        return _exchange_wait("forward2_wait", self.passing, self.forward, after)[1]

    def after_ffn(self, grads2):
        g = _stack2_full(*[a.astype(BF) for a in grads2])
        self.scatter = _plan_scatter8(self.h)
        self.chip = _exchange_start("reduce2_start", 7, g, lax.empty((7, self.h, D), BF), self.scatter)
        return self.chip[4]

    def after_mix(self, after, early):
        self.early_shapes = {n: early[n].shape for n in early}
        self.small = self.start_small("small", early)
        return self.small[4]

    def after_attn(self, after):
        g, y = _exchange_wait("reduce2_wait", self.chip, self.scatter, after)
        f = _sum8("sum2", self.place, g, y, RED2 // 2)
        self.early = self.finish_small("small", self.small, self.early_shapes, after)
        swap = _plan_swap_halves(self.h)
        self.swap = lambda _, buf: swap(buf, buf)
        self.swapping = _exchange_start("reduce2_gather_start", 1, self.dep, f, self.swap)
        return self.swapping[4]

    def reduced2(self, after):
        return _exchange_wait("reduce2_gather_wait", self.swapping, self.swap, after)[1]

    def start_small(self, name, small):
        v = _pack_small(small)
        return _exchange_start(name + "_start", 7, v, lax.empty((8,) + v.shape, F32), _plan_all)

    def finish_small(self, name, started, shapes, after):
        v, land = _exchange_wait(name + "_wait", started, _plan_all, after)
        land = lax.dynamic_update_slice(land, v[None], (2 * self.me + self.c, 0, 0))
        return _unpack_small(_sum_slots(name + "_sum", land), shapes)


def _pack_small(small):
    parts = []
    for name in small:
        flat = small[name].reshape(-1)
        parts.append(jnp.pad(flat, (0, -flat.shape[0] % LANE)).reshape(-1, LANE))
    v = jnp.concatenate(parts, axis=0)
    return _pad_rows(v, v.shape[0] + (-v.shape[0] % 8))


def _unpack_small(v, shapes):
    out, r = {}, 0
    for name in shapes:
        n = math.prod(shapes[name])
        rows = -(-n // LANE)
        out[name] = v[r:r + rows].reshape(-1)[:n].reshape(shapes[name])
        r += rows
    return out


def kernel(x, p, g_mix_pre, w_in, b_forget, g_attn_grp, g_pool_grp, w_pool, pool_scale, w_out, g_mix_post, g_ffn_pre, w_ffn_gate, w_ffn_up, w_ffn_down, g_ffn_post, w_ple_proj, g_ple, w_ple_gate, loss_target, m_g_mix_pre, m_w_in, m_b_forget, m_g_attn_grp, m_g_pool_grp, m_w_pool, m_pool_scale, m_w_out, m_g_mix_post, m_g_ffn_pre, m_w_ffn_gate, m_w_ffn_up, m_w_ffn_down, m_g_ffn_post, m_w_ple_proj, m_g_ple, m_w_ple_gate, v_g_mix_pre, v_w_in, v_b_forget, v_g_attn_grp, v_g_pool_grp, v_w_pool, v_pool_scale, v_w_out, v_g_mix_post, v_g_ffn_pre, v_w_ffn_gate, v_w_ffn_up, v_w_ffn_down, v_g_ffn_post, v_w_ple_proj, v_g_ple, v_w_ple_gate):
    args = dict(locals())
    strip = lambda n, a: a if n in VECTORS else a[0]
    w = {n: strip(n, args[n]) for n in ORDER}
    mom = {n: strip(n, args["m_" + n]) for n in ORDER}
    var = {n: strip(n, args["v_" + n]) for n in ORDER}
    sm = {n: w[n] for n in SMALL}

    c = lax.axis_index("c")
    me = 2 * lax.axis_index("x") + lax.axis_index("y")
    h1 = ROWS1 // 2
    bf = lambda n: w[n].astype(BF)
    stack1 = _stack1(bf("w_in"), bf("w_out"))
    gather1 = _exchange_start("gather1_start", 3, stack1, lax.empty((NSHARD, ROWS1, D), BF), _plan_gather(h1))
    _, *shards2 = lax.optimization_barrier((gather1[4], *[w[n] for n in BIG[2:]]))
    stack2 = _stack2(*[a.astype(BF) for a in shards2])
    stack1, land = _exchange_wait("gather1_wait", gather1, _plan_gather(h1), stack2)
    land, stack2 = lax.optimization_barrier((land, stack2))
    comm = _Comm(stack2, me, c)
    w1 = _unstack1_full(_exchange_inplace("gather1_forward", 4, land, (stack1,), _plan_forward(h1)))
    dx, grads1, late = _local_step(x[0], p[0, 0], loss_target[0], sm, w1, comm)

    flip = lambda n, a: a.T if n in TRANSPOSED else a
    grads, delta, new_m, new_v = {}, {}, {}, {}

    def update(n, g, dep, row0=None):
        g_, d_, m_, v_ = _adamw(flip(n, w[n]), g, flip(n, mom[n]), flip(n, var[n]), dep, row0)
        grads[n], delta[n], new_m[n], new_v[n] = flip(n, g_), flip(n, d_), flip(n, m_), flip(n, v_)
        return v_

    late_shapes = {n: late[n].shape for n in late}
    small2 = comm.start_small("small2", late)
    red2 = comm.reduced2(dx)
    g1 = _stack1_full(grads1[0], grads1[1].astype(BF))
    pair1 = _exchange_start("reduce1_pair_start", 1, g1, lax.empty((NSHARD, h1, D), BF), _plan_pair_rows(h1))
    dep = update("w_ple_gate", red2, pair1[4] + small2[4], O2_PG)
    dep = update("w_ple_proj", red2[O2_PLE:O2_PG].reshape(DPLE, DPLE), dep)
    g1, recv = _exchange_wait("reduce1_pair_wait", pair1, _plan_pair_rows(h1), dep)
    pb = _pair_sum("pair_sum1", comm.cidx, g1, recv, RED1)
    chip1 = _exchange_start("reduce1_chip_start", 3, pb, lax.empty((NSHARD - 1, h1, D), BF), _plan_scatter)
    dep = update("w_ffn_gate", red2, chip1[4], 0)
    dep = update("w_ffn_up", red2, dep, O2_U)
    dep = update("w_ffn_down", red2, dep, O2_D)
    red_small = {**comm.early, **comm.finish_small("small2", small2, late_shapes, dep)}
    loss = 0.5 / D * red_small["loss"][0, 0]
    for n in SMALL:
        grads[n] = red_small[n].reshape(w[n].shape)
    two_d = lambda a: a.reshape(-1, a.shape[-1])
    ds, ms, vs = _adamw_small([two_d(w[n]) for n in SMALL], [two_d(grads[n]) for n in SMALL],
                              [two_d(mom[n]) for n in SMALL], [two_d(var[n]) for n in SMALL])
    for k, n in enumerate(SMALL):
        delta[n], new_m[n], new_v[n] = ds[k].reshape(w[n].shape), ms[k].reshape(w[n].shape), vs[k].reshape(w[n].shape)
    pb, y = _exchange_wait("reduce1_chip_wait", chip1, _plan_scatter, vs[0])
    f = _chip_sum("chip_sum1", comm.place, pb, y, RED1)
    reduced1 = _exchange_inplace("reduce1_gather", 1, f, (), _plan_swap_halves(h1))
    g_in, g_out = _unstack1(reduced1)
    update("w_out", g_out, update("w_in", g_in, reduced1))

    lead = lambda d: [d[n] if n in VECTORS else d[n][None] for n in ORDER]
    return (loss, dx[None], *lead(grads), *lead(delta), *lead(new_m), *lead(new_v))
```
